```python
import jax
import jax.numpy as jnp
from jax import lax
import numpy as np

D_MODEL = 1024
BATCH = 8
SEQ = 4096
DEPTH = 2

N_MEM = 256
HEAD_DIM = 64
ROPE_THETA = 500000.0
ROPE_FRACTION = 4
RMS_EPS = 1e-6
NEG_INF = -1e30
TINY = 1e-20

DSA_HEADS = 8
DSA_IDX_HEADS = 8
DSA_IDX_DIM = 32
DSA_TOPK = 256
DSA_QBLOCK = 128

MOBA_HEADS = 8
MOBA_BLOCK = 256
MOBA_TOPK = 3
MOBA_QBLOCK = 32

NSA_HEADS = 16
NSA_GROUPS = 4
NSA_CMP_LEN = 32
NSA_CMP_STRIDE = 16
NSA_SEL_LEN = 64
NSA_SEL_TOPK = 16
NSA_WINDOW = 512
NSA_QBLOCK = 32
NSA_FORCE = 1e4

MEM_HEADS = 4
MEM_HEAD_DIM = 128

D_FF = ((8 * D_MODEL + 3 * 256 - 1) // (3 * 256)) * 256

AB_SIZES = (DSA_HEADS * HEAD_DIM, HEAD_DIM, HEAD_DIM, DSA_IDX_HEADS * DSA_IDX_DIM, DSA_IDX_DIM, DSA_IDX_HEADS, MOBA_HEADS * HEAD_DIM, MOBA_HEADS * HEAD_DIM, MOBA_HEADS * HEAD_DIM)
AB_WIDTH = sum(AB_SIZES)
AB_MIX_WIDTH = (DSA_HEADS + MOBA_HEADS) * HEAD_DIM
C_SIZES = (NSA_HEADS * HEAD_DIM,) + (NSA_GROUPS * HEAD_DIM,) * 6 + (NSA_HEADS * 3,)
C_WIDTH = sum(C_SIZES)
C_MIX_WIDTH = NSA_HEADS * HEAD_DIM

kernel_name = 'hybrid_dsa_moba_nsa_mem_block'


def rms_norm(x, gain):
    xf = x.astype(jnp.float32)
    y = xf * lax.rsqrt(jnp.mean(xf * xf, axis=-1, keepdims=True) + RMS_EPS)
    return (y * gain.astype(jnp.float32)).astype(x.dtype)


def partial_rope(x, positions):
    dh = x.shape[-1]
    rot = dh // ROPE_FRACTION
    half = rot // 2
    inv_freq = ROPE_THETA ** (-(jnp.arange(half, dtype=jnp.float32) * 2.0 / rot))
    ang = positions.astype(jnp.float32)[:, :, None] * inv_freq
    cos = jnp.cos(ang)[:, :, None, :]
    sin = jnp.sin(ang)[:, :, None, :]
    xf = x.astype(jnp.float32)
    x1 = xf[..., :half]
    x2 = xf[..., half:rot]
    out = jnp.concatenate([x1 * cos - x2 * sin, x2 * cos + x1 * sin, xf[..., rot:]], axis=-1)
    return out.astype(x.dtype)


def masked_softmax(s, mask):
    s = jnp.where(mask, s, NEG_INF)
    m = jnp.max(s, axis=-1, keepdims=True)
    p = jnp.where(mask, jnp.exp(s - m), 0.0)
    return p / jnp.maximum(jnp.sum(p, axis=-1, keepdims=True), TINY)


def _rows(t, start, size):
    return lax.dynamic_slice_in_dim(t, start, size, axis=1)


def _split(x, sizes):
    out = []
    start = 0
    for n in sizes:
        out.append(x[..., start:start + n])
        start += n
    return out


def _sweep_queries(fn, n_blocks):
    out = lax.map(fn, jnp.arange(n_blocks, dtype=jnp.int32))
    out = jnp.moveaxis(out, 0, 1)
    return out.reshape((out.shape[0], -1) + out.shape[3:])


def dsa_attention(q, k, v, q_idx, k_idx, w_idx):
    B, S, H, Dh = q.shape
    k_top = min(DSA_TOPK, S // 4)
    key_pos = jnp.arange(S)
    b_idx = jnp.arange(B)[:, None, None]
    scale = Dh ** -0.5

    def block(c):
        t0 = c * DSA_QBLOCK
        qpos = t0 + jnp.arange(DSA_QBLOCK)
        logits = jnp.einsum('bqhd,bsd->bqhs', _rows(q_idx, t0, DSA_QBLOCK), k_idx).astype(jnp.float32)
        score = jnp.einsum('bqh,bqhs->bqs', _rows(w_idx, t0, DSA_QBLOCK).astype(jnp.float32), jax.nn.relu(logits))
        score = jnp.where(key_pos[None, None, :] <= qpos[None, :, None], score, NEG_INF)
        _, sel = lax.top_k(score, k_top)
        k_sel = k[b_idx, sel]
        v_sel = v[b_idx, sel]
        s = jnp.einsum('bqhd,bqkd->bhqk', _rows(q, t0, DSA_QBLOCK), k_sel).astype(jnp.float32) * scale
        p = masked_softmax(s, (sel <= qpos[None, :, None])[:, None])
        return jnp.einsum('bhqk,bqkd->bqhd', p.astype(v.dtype), v_sel)

    return _sweep_queries(block, S // DSA_QBLOCK)


def moba_attention(q, k, v):
    B, S, H, Dh = q.shape
    n_blk = -(-S // MOBA_BLOCK)
    pad = n_blk * MOBA_BLOCK - S
    k_pad = jnp.pad(k, ((0, 0), (0, pad), (0, 0), (0, 0)))
    v_pad = jnp.pad(v, ((0, 0), (0, pad), (0, 0), (0, 0)))
    k_blk = k_pad.reshape(B, n_blk, MOBA_BLOCK, H, Dh).transpose(0, 3, 1, 2, 4)
    v_blk = v_pad.reshape(B, n_blk, MOBA_BLOCK, H, Dh).transpose(0, 3, 1, 2, 4)
    k_mean = jnp.mean(k_blk.astype(jnp.float32), axis=3)
    n_top = max(1, min(MOBA_TOPK, n_blk - 1))
    n_past = n_top * MOBA_BLOCK
    b_idx = jnp.arange(B)[:, None, None, None]
    h_idx = jnp.arange(H)[None, :, None, None]
    blk_ids = jnp.arange(n_blk)
    scale = Dh ** -0.5

    def block(c):
        t0 = c * MOBA_QBLOCK
        qpos = t0 + jnp.arange(MOBA_QBLOCK)
        own = t0 // MOBA_BLOCK
        qc = _rows(q, t0, MOBA_QBLOCK)
        gate = jnp.einsum('bqhd,bhnd->bhqn', qc.astype(jnp.float32), k_mean)
        gate = jnp.where(blk_ids < own, gate, NEG_INF)
        _, sel = lax.top_k(gate, n_top)
        k_sel = k_blk[b_idx, h_idx, sel]
        v_sel = v_blk[b_idx, h_idx, sel]
        s_past = jnp.einsum('bqhd,bhqnkd->bhqnk', qc, k_sel).astype(jnp.float32)
        m_past = jnp.broadcast_to((sel < own)[..., None], s_past.shape)
        k_own = _rows(k_pad, own * MOBA_BLOCK, MOBA_BLOCK)
        v_own = _rows(v_pad, own * MOBA_BLOCK, MOBA_BLOCK)
        kpos = own * MOBA_BLOCK + jnp.arange(MOBA_BLOCK)
        s_own = jnp.einsum('bqhd,bkhd->bhqk', qc, k_own).astype(jnp.float32)
        m_own = jnp.broadcast_to(kpos[None, :] <= qpos[:, None], s_own.shape)
        s = jnp.concatenate([s_past.reshape(B, H, MOBA_QBLOCK, n_past), s_own], axis=-1) * scale
        m = jnp.concatenate([m_past.reshape(B, H, MOBA_QBLOCK, n_past), m_own], axis=-1)
        p = masked_softmax(s, m).astype(v.dtype)
        p_past = p[..., :n_past].reshape(B, H, MOBA_QBLOCK, n_top, MOBA_BLOCK)
        return (jnp.einsum('bhqnk,bhqnkd->bqhd', p_past, v_sel)
                + jnp.einsum('bhqk,bkhd->bqhd', p[..., n_past:], v_own))

    return _sweep_queries(block, S // MOBA_QBLOCK)


def nsa_compress(x, pos_emb, w1, w2):
    B, S, G, Dh = x.shape
    n_cmp = (S - NSA_CMP_LEN) // NSA_CMP_STRIDE + 1
    idx = jnp.arange(n_cmp)[:, None] * NSA_CMP_STRIDE + jnp.arange(NSA_CMP_LEN)[None, :]
    blk = x[:, idx] + pos_emb[None, None, :, None, :]
    blk = blk.transpose(0, 1, 3, 2, 4).reshape(B, n_cmp, G, NSA_CMP_LEN * Dh)
    return jax.nn.silu(blk @ w1) @ w2


def nsa_attention(q, q_rot, k_cmp, v_cmp, k_sel, v_sel, k_win, v_win, gates):
    B, S, H, Dh = q.shape
    G = k_sel.shape[2]
    HG = H // G
    qg = q.reshape(B, S, G, HG, Dh)
    qr = q_rot.reshape(B, S, G, HG, Dh)
    n_cmp = k_cmp.shape[1]
    cmp_start = jnp.arange(n_cmp) * NSA_CMP_STRIDE
    cmp_end = cmp_start + NSA_CMP_LEN - 1
    n_sel = S // NSA_SEL_LEN
    n_top = min(NSA_SEL_TOPK, n_sel)
    sel_start = jnp.arange(n_sel) * NSA_SEL_LEN
    cover = ((cmp_start[:, None] < sel_start[None, :] + NSA_SEL_LEN)
             & (cmp_start[:, None] + NSA_CMP_LEN > sel_start[None, :])).astype(jnp.float32)
    ks_blk = k_sel.reshape(B, n_sel, NSA_SEL_LEN, G, Dh).transpose(0, 3, 1, 2, 4)
    vs_blk = v_sel.reshape(B, n_sel, NSA_SEL_LEN, G, Dh).transpose(0, 3, 1, 2, 4)
    kw_pad = jnp.pad(k_win, ((0, 0), (NSA_WINDOW, 0), (0, 0), (0, 0)))
    vw_pad = jnp.pad(v_win, ((0, 0), (NSA_WINDOW, 0), (0, 0), (0, 0)))
    b_idx = jnp.arange(B)[:, None, None, None]
    g_idx = jnp.arange(G)[None, :, None, None]
    blk_ids = jnp.arange(n_sel)[None, :]
    scale = Dh ** -0.5
    n_gather = n_top * NSA_SEL_LEN

    def block(c):
        t0 = c * NSA_QBLOCK
        qpos = t0 + jnp.arange(NSA_QBLOCK)
        qp = _rows(qg, t0, NSA_QBLOCK)
        qrc = _rows(qr, t0, NSA_QBLOCK)
        gc = _rows(gates, t0, NSA_QBLOCK)
        s_c = jnp.einsum('bqgjd,bngd->bgjqn', qp, k_cmp).astype(jnp.float32) * scale
        p_c = masked_softmax(s_c, cmp_end[None, :] <= qpos[:, None])
        o_c = jnp.einsum('bgjqn,bngd->bqgjd', p_c.astype(v_cmp.dtype), v_cmp)
        imp = jnp.einsum('bgjqn,ns->bgqs', p_c, cover)
        cur = (qpos // NSA_SEL_LEN)[:, None]
        forced = (blk_ids == 0) | (blk_ids == cur) | (blk_ids == cur - 1)
        imp = jnp.where(forced, NSA_FORCE, imp)
        imp = jnp.where(blk_ids <= cur, imp, NEG_INF)
        _, sel = lax.top_k(imp, n_top)
        kg = ks_blk[b_idx, g_idx, sel].reshape(B, G, NSA_QBLOCK, n_gather, Dh)
        vg = vs_blk[b_idx, g_idx, sel].reshape(B, G, NSA_QBLOCK, n_gather, Dh)
        kpos = (sel[..., None] * NSA_SEL_LEN + jnp.arange(NSA_SEL_LEN)).reshape(B, G, NSA_QBLOCK, n_gather)
        s_s = jnp.einsum('bqgjd,bgqmd->bgjqm', qrc, kg).astype(jnp.float32) * scale
        p_s = masked_softmax(s_s, (kpos <= qpos[None, None, :, None])[:, :, None])
        o_s = jnp.einsum('bgjqm,bgqmd->bqgjd', p_s.astype(vg.dtype), vg)
        kw = _rows(kw_pad, t0, NSA_WINDOW + NSA_QBLOCK)
        vw = _rows(vw_pad, t0, NSA_WINDOW + NSA_QBLOCK)
        wpos = t0 - NSA_WINDOW + jnp.arange(NSA_WINDOW + NSA_QBLOCK)
        dist = qpos[:, None] - wpos[None, :]
        m_w = (dist >= 0) & (dist < NSA_WINDOW) & (wpos[None, :] >= 0)
        s_w = jnp.einsum('bqgjd,bkgd->bgjqk', qrc, kw).astype(jnp.float32) * scale
        p_w = masked_softmax(s_w, m_w)
        o_w = jnp.einsum('bgjqk,bkgd->bqgjd', p_w.astype(vw.dtype), vw)
        o = gc[..., 0:1] * o_c + gc[..., 1:2] * o_s + gc[..., 2:3] * o_w
        return o.reshape(B, NSA_QBLOCK, H, Dh)

    return _sweep_queries(block, S // NSA_QBLOCK)


def dsa_moba_mixer(h, positions, w_in, w_out, a_q_norm, a_k_norm, b_q_norm, b_k_norm):
    B, S, _ = h.shape
    aq, ak, av, iq, ik, iw, bq, bk, bv = _split(h @ w_in, AB_SIZES)
    aq = partial_rope(rms_norm(aq.reshape(B, S, DSA_HEADS, HEAD_DIM), a_q_norm), positions)
    ak = partial_rope(rms_norm(ak.reshape(B, S, 1, HEAD_DIM), a_k_norm), positions)[:, :, 0]
    iq = partial_rope(iq.reshape(B, S, DSA_IDX_HEADS, DSA_IDX_DIM), positions)
    ik = partial_rope(ik.reshape(B, S, 1, DSA_IDX_DIM), positions)[:, :, 0]
    o_a = dsa_attention(aq, ak, av, iq, ik, iw)
    bq = partial_rope(rms_norm(bq.reshape(B, S, MOBA_HEADS, HEAD_DIM), b_q_norm), positions)
    bk = partial_rope(rms_norm(bk.reshape(B, S, MOBA_HEADS, HEAD_DIM), b_k_norm), positions)
    o_b = moba_attention(bq, bk, bv.reshape(B, S, MOBA_HEADS, HEAD_DIM))
    o = jnp.concatenate([o_a.reshape(B, S, -1), o_b.reshape(B, S, -1)], axis=-1)
    return o @ w_out


def nsa_mixer(h, positions, w_in, w_out, q_norm, kcmp_norm, ksel_norm, kwin_norm,
              pos_k, pos_v, w1_k, w2_k, w1_v, w2_v):
    B, S, _ = h.shape
    G = NSA_GROUPS
    HG = NSA_HEADS // NSA_GROUPS
    q, kc, vc, ks, vs, kw, vw, gt = _split(h @ w_in, C_SIZES)
    grp = lambda t: t.reshape(B, S, G, HEAD_DIM)
    q = rms_norm(q.reshape(B, S, NSA_HEADS, HEAD_DIM), q_norm)
    q_rot = partial_rope(q, positions)
    kc = rms_norm(nsa_compress(grp(kc), pos_k, w1_k, w2_k), kcmp_norm)
    vc = nsa_compress(grp(vc), pos_v, w1_v, w2_v)
    ks = partial_rope(rms_norm(grp(ks), ksel_norm), positions)
    kw = partial_rope(rms_norm(grp(kw), kwin_norm), positions)
    gates = jax.nn.sigmoid(gt.astype(jnp.float32)).reshape(B, S, G, HG, 3).astype(h.dtype)
    o = nsa_attention(q, q_rot, kc, vc, ks, grp(vs), kw, grp(vw), gates)
    return o.reshape(B, S, C_MIX_WIDTH) @ w_out


def memory_cross_attention(h, m, w_q, w_kv, w_o, q_norm, k_norm):
    B, S, _ = h.shape
    M = m.shape[1]
    q = rms_norm((h @ w_q).reshape(B, S, MEM_HEADS, MEM_HEAD_DIM), q_norm)
    k, v = _split(m @ w_kv, (MEM_HEADS * MEM_HEAD_DIM, MEM_HEADS * MEM_HEAD_DIM))
    k = rms_norm(k.reshape(B, M, MEM_HEADS, MEM_HEAD_DIM), k_norm)
    v = v.reshape(B, M, MEM_HEADS, MEM_HEAD_DIM)
    s = jnp.einsum('bshd,bmhd->bhsm', q, k).astype(jnp.float32) * (MEM_HEAD_DIM ** -0.5)
    p = jax.nn.softmax(s, axis=-1).astype(v.dtype)
    o = jnp.einsum('bhsm,bmhd->bshd', p, v).reshape(B, S, MEM_HEADS * MEM_HEAD_DIM)
    return o @ w_o


def swiglu(h, w_in, w_out):
    g, u = _split(h @ w_in, (D_FF, D_FF))
    return (jax.nn.silu(g) * u) @ w_out


def setup_inputs(seed: int = 0) -> dict:
    key = jax.random.key(seed)
    keys = jax.random.split(key, 48)
    ctr = [0]

    def nxt():
        k = keys[ctr[0]]
        ctr[0] += 1
        return k

    def nrm(shape, scale):
        return scale * jax.random.normal(nxt(), shape, jnp.float32)

    def gain(shape):
        return 1.0 + 0.02 * jax.random.normal(nxt(), shape, jnp.float32)

    n_even = (DEPTH + 1) // 2
    n_odd = DEPTH // 2
    d = D_MODEL
    x = nrm((BATCH, SEQ, d), 1.0)
    mem = nrm((BATCH, N_MEM, d), 1.0)
    offset = jax.random.randint(nxt(), (BATCH, 1), 0, SEQ)
    positions = (offset + jnp.arange(SEQ)[None, :]).astype(jnp.int32)
    cmp_in = NSA_CMP_LEN * HEAD_DIM
    return {
        'x': x,
        'mem': mem,
        'positions': positions,
        'norm_mix': gain((DEPTH, d)),
        'norm_mem': gain((DEPTH, d)),
        'norm_mem_src': gain((DEPTH, d)),
        'norm_ffn': gain((DEPTH, d)),
        'ab_w_in': nrm((n_even, d, AB_WIDTH), d ** -0.5),
        'ab_w_out': nrm((n_even, AB_MIX_WIDTH, d), AB_MIX_WIDTH ** -0.5),
        'dsa_q_norm': gain((n_even, HEAD_DIM)),
        'dsa_k_norm': gain((n_even, HEAD_DIM)),
        'moba_q_norm': gain((n_even, HEAD_DIM)),
        'moba_k_norm': gain((n_even, HEAD_DIM)),
        'nsa_w_in': nrm((n_odd, d, C_WIDTH), d ** -0.5),
        'nsa_w_out': nrm((n_odd, C_MIX_WIDTH, d), C_MIX_WIDTH ** -0.5),
        'nsa_q_norm': gain((n_odd, HEAD_DIM)),
        'nsa_kcmp_norm': gain((n_odd, HEAD_DIM)),
        'nsa_ksel_norm': gain((n_odd, HEAD_DIM)),
        'nsa_kwin_norm': gain((n_odd, HEAD_DIM)),
        'nsa_cmp_pos_k': nrm((n_odd, NSA_CMP_LEN, HEAD_DIM), 0.1),
        'nsa_cmp_pos_v': nrm((n_odd, NSA_CMP_LEN, HEAD_DIM), 0.1),
        'nsa_cmp_w1_k': nrm((n_odd, cmp_in, HEAD_DIM), cmp_in ** -0.5),
        'nsa_cmp_w2_k': nrm((n_odd, HEAD_DIM, HEAD_DIM), HEAD_DIM ** -0.5),
        'nsa_cmp_w1_v': nrm((n_odd, cmp_in, HEAD_DIM), cmp_in ** -0.5),
        'nsa_cmp_w2_v': nrm((n_odd, HEAD_DIM, HEAD_DIM), HEAD_DIM ** -0.5),
        'mem_w_q': nrm((DEPTH, d, MEM_HEADS * MEM_HEAD_DIM), d ** -0.5),
        'mem_w_kv': nrm((DEPTH, d, 2 * MEM_HEADS * MEM_HEAD_DIM), d ** -0.5),
        'mem_w_o': nrm((DEPTH, MEM_HEADS * MEM_HEAD_DIM, d), (MEM_HEADS * MEM_HEAD_DIM) ** -0.5),
        'mem_q_norm': gain((DEPTH, MEM_HEAD_DIM)),
        'mem_k_norm': gain((DEPTH, MEM_HEAD_DIM)),
        'ffn_w_in': nrm((DEPTH, d, 2 * D_FF), d ** -0.5),
        'ffn_w_out': nrm((DEPTH, D_FF, d), D_FF ** -0.5),
    }


def reference(x, mem, positions, norm_mix, norm_mem, norm_mem_src, norm_ffn,
              ab_w_in, ab_w_out, dsa_q_norm, dsa_k_norm, moba_q_norm, moba_k_norm,
              nsa_w_in, nsa_w_out, nsa_q_norm, nsa_kcmp_norm, nsa_ksel_norm, nsa_kwin_norm,
              nsa_cmp_pos_k, nsa_cmp_pos_v, nsa_cmp_w1_k, nsa_cmp_w2_k, nsa_cmp_w1_v, nsa_cmp_w2_v,
              mem_w_q, mem_w_kv, mem_w_o, mem_q_norm, mem_k_norm, ffn_w_in, ffn_w_out):
    for i in range(DEPTH):
        j = i // 2
        h = rms_norm(x, norm_mix[i])
        if i % 2 == 0:
            x = x + dsa_moba_mixer(h, positions, ab_w_in[j], ab_w_out[j], dsa_q_norm[j], dsa_k_norm[j],
                                   moba_q_norm[j], moba_k_norm[j])
        else:
            x = x + nsa_mixer(h, positions, nsa_w_in[j], nsa_w_out[j], nsa_q_norm[j], nsa_kcmp_norm[j],
                              nsa_ksel_norm[j], nsa_kwin_norm[j], nsa_cmp_pos_k[j], nsa_cmp_pos_v[j],
                              nsa_cmp_w1_k[j], nsa_cmp_w2_k[j], nsa_cmp_w1_v[j], nsa_cmp_w2_v[j])
        x = x + memory_cross_attention(rms_norm(x, norm_mem[i]), rms_norm(mem, norm_mem_src[i]),
                                       mem_w_q[i], mem_w_kv[i], mem_w_o[i], mem_q_norm[i], mem_k_norm[i])
        x = x + swiglu(rms_norm(x, norm_ffn[i]), ffn_w_in[i], ffn_w_out[i])
    return x
```

```python
import functools

import jax
import jax.numpy as jnp
from jax import lax
from jax.experimental import pallas as pl
from jax.experimental.pallas import tpu as pltpu

F32 = jnp.float32
BF16 = jnp.bfloat16
I32 = jnp.int32

D_MODEL = 1024
N_MEM = 256
HEAD_DIM = 64
ROPE_THETA = 500000.0
RMS_EPS = 1e-6
NEG_INF = -1e30
TINY = 1e-20

DSA_HEADS = 8
DSA_IDX_HEADS = 8
DSA_IDX_DIM = 32
DSA_TOPK = 256
MOBA_HEADS = 8
MOBA_BLOCK = 256
MOBA_TOPK = 3
NSA_HEADS = 16
NSA_GROUPS = 4
NSA_CMP_LEN = 32
NSA_CMP_STRIDE = 16
NSA_SEL_LEN = 64
NSA_SEL_TOPK = 16
NSA_WINDOW = 512
NSA_FORCE = 1e4
MEM_HEADS = 4
MEM_HEAD_DIM = 128
D_FF = ((8 * D_MODEL + 3 * 256 - 1) // (3 * 256)) * 256

LANES = 128
INT_MIN = -(2 ** 31)
VMEM_LIMIT = 56 * 1024 * 1024

NT_DIMS = (((1,), (1,)), ((), ()))


def _dot(a, b):
    return jnp.dot(a, b, preferred_element_type=F32)


def _dot_nt(a, b):
    return lax.dot_general(a, b, NT_DIMS, preferred_element_type=F32)


def _split_dot(a, b):
    hi = a.astype(BF16)
    lo = (a - hi.astype(F32)).astype(BF16)
    return _dot(hi, b) + _dot(lo, b)


def _rms_rows(x, gain):
    ms = jnp.mean(x * x, axis=-1, keepdims=True)
    return x * lax.rsqrt(ms + RMS_EPS) * gain


def _params(sem):
    return pltpu.CompilerParams(dimension_semantics=sem, vmem_limit_bytes=VMEM_LIMIT)


def _head_norm(y, norm_m, gain):
    ms = _split_dot(y * y, norm_m)
    return y * lax.rsqrt(ms + RMS_EPS) * gain


def _rope(y, c, s, lo_mask, half):
    sw = jnp.where(lo_mask, pltpu.roll(y, LANES - half, 1), pltpu.roll(y, half, 1))
    return y * c + sw * s


def _lane_iota(shape):
    return lax.broadcasted_iota(I32, shape, 1)


def _rope_tables(pos, ftab, period, half):
    ang = pos * ftab
    lane = _lane_iota(ang.shape) % period
    c = jnp.cos(ang)
    s = jnp.sin(ang) * jnp.where(lane < half, -1.0, 1.0)
    return c, s


AB_TM = 256


def _ab_prep_kernel(x_ref, pos_ref, gmix_ref, w_ref, f64_ref, f32_ref, nq_ref, nk_ref,
                    gaq_ref, gbq_ref, gak_ref, gbk_ref,
                    aq_ref, bq_ref, bkv_ref, akv_ref, iq_ref, ik_ref, iw_ref, km_ref):
    xn = _rms_rows(x_ref[...], gmix_ref[...]).astype(BF16)
    y = _dot(xn, w_ref[...])
    pos = pos_ref[...]
    c64, s64 = _rope_tables(pos, f64_ref[...], 64, 8)
    c32, s32 = _rope_tables(pos, f32_ref[...], 32, 4)
    lane = _lane_iota(c64.shape)
    lo64 = (lane % 64) < 8
    lo32 = (lane % 32) < 4
    first64 = lane < 64
    c64k = jnp.where(first64, c64, 1.0)
    s64k = jnp.where(first64, s64, 0.0)
    first32 = lane < 32
    c32k = jnp.where(first32, c32, 1.0)
    s32k = jnp.where(first32, s32, 0.0)
    nq = nq_ref[...]
    nk = nk_ref[...]

    def col(j):
        return y[:, j * LANES:(j + 1) * LANES]

    for j in range(4):
        q = _rope(_head_norm(col(j), nq, gaq_ref[...]), c64, s64, lo64, 8)
        aq_ref[2 * j] = q[:, :64].astype(BF16)
        aq_ref[2 * j + 1] = q[:, 64:].astype(BF16)
    for j in range(4):
        q = _rope(_head_norm(col(4 + j), nq, gbq_ref[...]), c64, s64, lo64, 8)
        bq_ref[2 * j] = q[:, :64].astype(BF16)
        bq_ref[2 * j + 1] = q[:, 64:].astype(BF16)
    for h in range(8):
        yc = col(8 + h)
        kn = jnp.where(first64, _head_norm(yc, nk, gbk_ref[...]), yc)
        kv = _rope(kn, c64k, s64k, lo64, 8)
        bkv_ref[h] = kv.astype(BF16)
        km_ref[h:h + 1, :] = jnp.mean(kv, axis=0, keepdims=True)
    yc = col(16)
    kn = jnp.where(first64, _head_norm(yc, nk, gak_ref[...]), yc)
    akv_ref[...] = _rope(kn, c64k, s64k, lo64, 8).astype(BF16)
    for j in range(2):
        q = _rope(col(17 + j), c32, s32, lo32, 4)
        for u in range(4):
            iq_ref[4 * j + u] = q[:, 32 * u:32 * (u + 1)].astype(BF16)
    yc = col(19)
    ik_ref[...] = _rope(yc, c32k, s32k, lo32, 4)[:, :32].astype(BF16)
    iw_ref[...] = yc[:, 32:40]


def _ab_prep(x2, pos2, gmix, w, tabs, B, S):
    T = x2.shape[0]
    tm = AB_TM
    nt = S // tm
    n_cols = w.shape[1]
    f64, f32t, nq, nk, gaq, gbq, gak, gbk = tabs

    def full(a):
        return pl.BlockSpec(a.shape, lambda i: (0,) * a.ndim)

    def hm(width, heads=8):
        return pl.BlockSpec((None, heads, tm, width), lambda i: (i // nt, 0, i % nt, 0))

    def tokm(width):
        return pl.BlockSpec((None, tm, width), lambda i: (i // nt, i % nt, 0))

    out_shape = (
        jax.ShapeDtypeStruct((B, 8, S, 64), BF16),
        jax.ShapeDtypeStruct((B, 8, S, 64), BF16),
        jax.ShapeDtypeStruct((B, 8, S, 128), BF16),
        jax.ShapeDtypeStruct((B, S, 128), BF16),
        jax.ShapeDtypeStruct((B, 8, S, 32), BF16),
        jax.ShapeDtypeStruct((B, S, 32), BF16),
        jax.ShapeDtypeStruct((B, S, 8), F32),
        jax.ShapeDtypeStruct((T // tm, 8, 128), F32),
    )
    out_specs = (hm(64), hm(64), hm(128), tokm(128), hm(32), tokm(32), tokm(8),
                 pl.BlockSpec((None, 8, 128), lambda i: (i, 0, 0)))
    in_specs = [pl.BlockSpec((tm, D_MODEL), lambda i: (i, 0)),
                pl.BlockSpec((tm, 1), lambda i: (i, 0)),
                full(gmix), pl.BlockSpec((D_MODEL, n_cols), lambda i: (0, 0)),
                full(f64), full(f32t), full(nq), full(nk), full(gaq), full(gbq), full(gak), full(gbk)]
    return pl.pallas_call(
        _ab_prep_kernel, grid=(T // tm,), in_specs=in_specs, out_specs=out_specs,
        out_shape=out_shape, compiler_params=_params(("parallel",)), name="ab_prep",
    )(x2, pos2, gmix, w, f64, f32t, nq, nk, gaq, gbq, gak, gbk)


DSA_TQ = 128
DSA_KC = 512


def _softmax_step(s, mask, m_prev, l_prev):
    sm = jnp.where(mask, s, NEG_INF)
    m_new = jnp.maximum(m_prev, jnp.max(sm, axis=-1, keepdims=True))
    p = jnp.where(mask, jnp.exp(sm - m_new), 0.0)
    alpha = jnp.exp(m_prev - m_new)
    l_new = alpha * l_prev + jnp.sum(p, axis=-1, keepdims=True)
    return p, m_new, l_new, alpha


def _dsa_kernel(iq_ref, iw_ref, ik_ref, aq_ref, akv_ref, o_ref, sk_ref, m_ref, l_ref, acc_ref,
                *, k_top):
    tq, kc = DSA_TQ, DSA_KC
    H = DSA_HEADS
    t0 = pl.program_id(1) * tq
    nkc = (t0 + tq + kc - 1) // kc
    qpos = t0 + lax.broadcasted_iota(I32, (tq, kc), 0)
    kiota = lax.broadcasted_iota(I32, (tq, kc), 1)

    def score_chunk(c, carry):
        k0 = pl.multiple_of(c * kc, kc)
        ikc = ik_ref[pl.ds(k0, kc), :]
        sc = jnp.zeros((tq, kc), F32)
        for h in range(DSA_IDX_HEADS):
            logit = _dot_nt(iq_ref[h], ikc)
            sc = sc + iw_ref[:, h:h + 1] * jnp.maximum(logit, 0.0)
        sc = jnp.where(sc == 0.0, 0.0, sc)
        bits = pltpu.bitcast(sc, I32)
        key = bits ^ ((bits >> 31) & 0x7FFFFFFF)
        sk_ref[c] = jnp.where(k0 + kiota <= qpos, key, INT_MIN)
        return carry

    lax.fori_loop(0, nkc, score_chunk, 0)

    def bit_step(b, thr):
        cand = thr + lax.shift_left(jnp.int32(1), 31 - b)

        def count_chunk(c, acc):
            blk = sk_ref[c]
            for j in range(kc // LANES):
                acc = acc + jnp.where(blk[:, j * LANES:(j + 1) * LANES] >= cand, 1.0, 0.0)
            return acc

        acc = lax.fori_loop(0, nkc, count_chunk, jnp.zeros((tq, LANES), F32))
        cnt = jnp.sum(acc, axis=1, keepdims=True)
        return jnp.where(cnt >= k_top, cand, thr)

    thr = lax.fori_loop(0, 32, bit_step, jnp.full((tq, 1), INT_MIN, I32))

    zpad = jnp.zeros((tq, HEAD_DIM), BF16)
    q_st = jnp.concatenate(
        [jnp.concatenate([aq_ref[h], zpad], axis=1) for h in range(H)], axis=0)
    m_ref[...] = jnp.full(m_ref.shape, NEG_INF, F32)
    l_ref[...] = jnp.zeros(l_ref.shape, F32)
    acc_ref[...] = jnp.zeros(acc_ref.shape, F32)

    def att_chunk(c, carry):
        k0 = pl.multiple_of(c * kc, kc)
        kv = akv_ref[pl.ds(k0, kc), :]
        s = _dot_nt(q_st, kv)
        mask = (sk_ref[c] >= thr) & (k0 + kiota <= qpos)
        ps = []
        for h in range(H):
            rows = slice(h * tq, (h + 1) * tq)
            p, m_new, l_new, alpha = _softmax_step(s[rows], mask, m_ref[rows], l_ref[rows])
            m_ref[rows] = m_new
            l_ref[rows] = l_new
            acc_ref[rows] = alpha * acc_ref[rows]
            ps.append(p.astype(BF16))
        acc_ref[...] += _dot(jnp.concatenate(ps, axis=0), kv)
        return carry

    lax.fori_loop(0, nkc, att_chunk, 0)

    o = acc_ref[...] / jnp.maximum(l_ref[...], TINY)
    lane = _lane_iota((tq, LANES))
    for j in range(H // 2):
        even = pltpu.roll(o[(2 * j) * tq:(2 * j + 1) * tq], 64, 1)
        odd = o[(2 * j + 1) * tq:(2 * j + 2) * tq]
        o_ref[:, j * LANES:(j + 1) * LANES] = jnp.where(lane < 64, even, odd).astype(o_ref.dtype)


def _dsa_attention(iq, iw, ik, aq, akv):
    B, _, S, _ = aq.shape
    tq, kc = DSA_TQ, DSA_KC
    k_top = min(DSA_TOPK, S // 4)
    grid = (B, S // tq)
    in_specs = [
        pl.BlockSpec((None, 8, tq, 32), lambda b, i: (b, 0, i, 0)),
        pl.BlockSpec((None, tq, 8), lambda b, i: (b, i, 0)),
        pl.BlockSpec((None, S, 32), lambda b, i: (b, 0, 0)),
        pl.BlockSpec((None, 8, tq, 64), lambda b, i: (b, 0, i, 0)),
        pl.BlockSpec((None, S, 128), lambda b, i: (b, 0, 0)),
    ]
    return pl.pallas_call(
        functools.partial(_dsa_kernel, k_top=k_top), grid=grid, in_specs=in_specs,
        out_specs=pl.BlockSpec((None, tq, 512), lambda b, i: (b, i, 0)),
        out_shape=jax.ShapeDtypeStruct((B, S, 512), BF16),
        scratch_shapes=[pltpu.VMEM((S // kc, tq, kc), I32),
                        pltpu.VMEM((DSA_HEADS * tq, 1), F32),
                        pltpu.VMEM((DSA_HEADS * tq, 1), F32),
                        pltpu.VMEM((DSA_HEADS * tq, LANES), F32)],
        compiler_params=_params(("parallel", "parallel")), name="dsa_attention",
    )(iq, iw, ik, aq, akv)


MOBA_TQ = MOBA_BLOCK


def _rank_select(v, n_valid, n_top):
    n = v.shape[1]
    lane = _lane_iota(v.shape)
    rank = jnp.zeros(v.shape, F32)
    for m in range(n):
        vm = v[:, m:m + 1]
        ahead = (vm > v) | ((vm == v) & (m < lane))
        if n_valid is not None:
            ahead = ahead & (m < n_valid)
        rank = rank + jnp.where(ahead, 1.0, 0.0)
    sel = rank < n_top
    if n_valid is not None:
        sel = sel & (lane < n_valid)
    return jnp.where(sel, 1.0, 0.0)


def _moba_kernel(q_ref, kv_ref, km_ref, o_ref, selb_ref, m_ref, l_ref, acc_ref, *, n_top):
    tq = MOBA_TQ
    blk = MOBA_BLOCK
    n_blk = km_ref.shape[1]
    own = pl.program_id(2)
    lane = _lane_iota((tq, LANES))
    zpad = jnp.zeros((tq, HEAD_DIM), BF16)
    outs = []
    for hh in range(2):
        q = jnp.concatenate([q_ref[hh], zpad], axis=1)
        km = km_ref[hh]
        km_hi = km.astype(BF16)
        km_lo = (km - km_hi.astype(F32)).astype(BF16)
        gate = _dot_nt(q, km_hi) + _dot_nt(q, km_lo)
        sel = _rank_select(gate, own, n_top)
        for n in range(n_blk):
            selb_ref[n] = jnp.broadcast_to(sel[:, n:n + 1], (tq, LANES))
        m_ref[...] = jnp.full(m_ref.shape, NEG_INF, F32)
        l_ref[...] = jnp.zeros(l_ref.shape, F32)
        acc_ref[...] = jnp.zeros(acc_ref.shape, F32)

        def update(kv, mask):
            s = _dot_nt(q, kv)
            p, m_new, l_new, alpha = _softmax_step(s, mask, m_ref[...], l_ref[...])
            m_ref[...] = m_new
            l_ref[...] = l_new
            acc_ref[...] = alpha * acc_ref[...] + _dot(p.astype(BF16), kv)

        def past_block(n, carry):
            k0 = pl.multiple_of(n * blk, blk)
            sb = selb_ref[n]
            mask = jnp.concatenate([sb] * (blk // LANES), axis=1) > 0.5
            update(kv_ref[hh, pl.ds(k0, blk), :], mask)
            return carry

        lax.fori_loop(0, own, past_block, 0)
        k0 = pl.multiple_of(own * blk, blk)
        causal = lax.broadcasted_iota(I32, (tq, blk), 1) <= lax.broadcasted_iota(I32, (tq, blk), 0)
        update(kv_ref[hh, pl.ds(k0, blk), :], causal)
        outs.append(acc_ref[...] / jnp.maximum(l_ref[...], TINY))
    o_ref[...] = jnp.where(lane < 64, pltpu.roll(outs[0], 64, 1), outs[1]).astype(o_ref.dtype)


def _moba_attention(bq, bkv, kmean):
    B, H, S, _ = bq.shape
    tq = MOBA_TQ
    n_blk = S // MOBA_BLOCK
    n_top = max(1, min(MOBA_TOPK, n_blk - 1))
    grid = (B, H // 2, S // tq)
    in_specs = [
        pl.BlockSpec((None, 2, tq, 64), lambda b, h, i: (b, h, i, 0)),
        pl.BlockSpec((None, 2, S, 128), lambda b, h, i: (b, h, 0, 0)),
        pl.BlockSpec((None, 2, n_blk, 128), lambda b, h, i: (b, h, 0, 0)),
    ]
    return pl.pallas_call(
        functools.partial(_moba_kernel, n_top=n_top), grid=grid, in_specs=in_specs,
        out_specs=pl.BlockSpec((None, tq, 128), lambda b, h, i: (b, i, h)),
        out_shape=jax.ShapeDtypeStruct((B, S, 512), BF16),
        scratch_shapes=[pltpu.VMEM((n_blk, tq, LANES), F32),
                        pltpu.VMEM((tq, 1), F32), pltpu.VMEM((tq, 1), F32),
                        pltpu.VMEM((tq, LANES), F32)],
        compiler_params=_params(("parallel", "parallel", "parallel")), name="moba_attention",
    )(bq, bkv, kmean)


def _out_proj_kernel(*refs, n_in):
    a_refs = refs[:n_in]
    w_refs = refs[n_in:2 * n_in]
    x_ref, o_ref = refs[2 * n_in], refs[2 * n_in + 1]
    acc = x_ref[...]
    for a_ref, w_ref in zip(a_refs, w_refs):
        acc = acc + _dot(a_ref[...], w_ref[...])
    o_ref[...] = acc


def _out_proj(parts, weights, x2, tm=512):
    T = x2.shape[0]
    n_in = len(parts)
    in_specs = ([pl.BlockSpec((tm, p.shape[1]), lambda i: (i, 0)) for p in parts]
                + [pl.BlockSpec(w.shape, lambda i: (0, 0)) for w in weights]
                + [pl.BlockSpec((tm, D_MODEL), lambda i: (i, 0))])
    return pl.pallas_call(
        functools.partial(_out_proj_kernel, n_in=n_in), grid=(T // tm,), in_specs=in_specs,
        out_specs=pl.BlockSpec((tm, D_MODEL), lambda i: (i, 0)),
        out_shape=jax.ShapeDtypeStruct((T, D_MODEL), F32),
        compiler_params=_params(("parallel",)), name="out_proj",
    )(*parts, *weights, x2)


def _lane_group_norm(y, gain, width):
    outs = []
    for j in range(y.shape[1] // width):
        yc = y[:, j * width:(j + 1) * width]
        outs.append(_rms_rows(yc, gain))
    return jnp.concatenate(outs, axis=1)


def _mem_kv_kernel(m_ref, g_ref, w_ref, gk_ref, o_ref):
    mn = _rms_rows(m_ref[...], g_ref[...]).astype(BF16)
    y = _dot(mn, w_ref[...])
    hw = MEM_HEADS * MEM_HEAD_DIM
    k = _lane_group_norm(y[:, :hw], gk_ref[...], MEM_HEAD_DIM)
    o_ref[...] = jnp.concatenate([k, y[:, hw:]], axis=1).astype(BF16)


def _mem_kv(mem, g, w, gk):
    B, M, _ = mem.shape
    n = w.shape[1]
    return pl.pallas_call(
        _mem_kv_kernel, grid=(B,),
        in_specs=[pl.BlockSpec((None, M, D_MODEL), lambda b: (b, 0, 0)),
                  pl.BlockSpec(g.shape, lambda b: (0, 0)),
                  pl.BlockSpec(w.shape, lambda b: (0, 0)),
                  pl.BlockSpec(gk.shape, lambda b: (0, 0))],
        out_specs=pl.BlockSpec((None, M, n), lambda b: (b, 0, 0)),
        out_shape=jax.ShapeDtypeStruct((B, M, n), BF16),
        compiler_params=_params(("parallel",)), name="mem_kv",
    )(mem, g, w, gk)


def _mem_attn_kernel(x_ref, g_ref, wq_ref, gq_ref, kv_ref, wo_ref, o_ref):
    x = x_ref[...]
    xn = _rms_rows(x, g_ref[...]).astype(BF16)
    q = _lane_group_norm(_dot(xn, wq_ref[...]), gq_ref[...], MEM_HEAD_DIM).astype(BF16)
    hw = MEM_HEADS * MEM_HEAD_DIM
    scale = MEM_HEAD_DIM ** -0.5
    outs = []
    for h in range(MEM_HEADS):
        cols = slice(h * MEM_HEAD_DIM, (h + 1) * MEM_HEAD_DIM)
        k = kv_ref[:, cols]
        v = kv_ref[:, hw + h * MEM_HEAD_DIM:hw + (h + 1) * MEM_HEAD_DIM]
        s = _dot_nt(q[:, cols], k) * scale
        p = jnp.exp(s - jnp.max(s, axis=-1, keepdims=True))
        p = p / jnp.sum(p, axis=-1, keepdims=True)
        outs.append(_dot(p.astype(BF16), v))
    o = jnp.concatenate(outs, axis=1).astype(BF16)
    o_ref[...] = x + _dot(o, wo_ref[...])


def _mem_attn(x2, g, wq, gq, kv, wo, S, tm=512):
    T = x2.shape[0]
    nt = S // tm
    M, n = kv.shape[1], kv.shape[2]
    return pl.pallas_call(
        _mem_attn_kernel, grid=(T // tm,),
        in_specs=[pl.BlockSpec((tm, D_MODEL), lambda i: (i, 0)),
                  pl.BlockSpec(g.shape, lambda i: (0, 0)),
                  pl.BlockSpec(wq.shape, lambda i: (0, 0)),
                  pl.BlockSpec(gq.shape, lambda i: (0, 0)),
                  pl.BlockSpec((None, M, n), lambda i: (i // nt, 0, 0)),
                  pl.BlockSpec(wo.shape, lambda i: (0, 0))],
        out_specs=pl.BlockSpec((tm, D_MODEL), lambda i: (i, 0)),
        out_shape=jax.ShapeDtypeStruct((T, D_MODEL), F32),
        compiler_params=_params(("parallel",)), name="mem_attn",
    )(x2, g, wq, gq, kv, wo)


def _ffn_kernel(x_ref, g_ref, wg_ref, wu_ref, wo_ref, o_ref, xn_ref, acc_ref):
    j = pl.program_id(1)

    @pl.when(j == 0)
    def _():
        xn_ref[...] = _rms_rows(x_ref[...], g_ref[...]).astype(BF16)
        acc_ref[...] = x_ref[...]

    xn = xn_ref[...]
    gate = _dot(xn, wg_ref[...])
    up = _dot(xn, wu_ref[...])
    act = (gate * jax.nn.sigmoid(gate) * up).astype(BF16)
    acc_ref[...] += _dot(act, wo_ref[...])

    @pl.when(j == pl.num_programs(1) - 1)
    def _():
        o_ref[...] = acc_ref[...]


def _ffn(x2, g, wg, wu, wo, tm=512, n_split=2):
    T = x2.shape[0]
    tf = D_FF // n_split
    return pl.pallas_call(
        _ffn_kernel, grid=(T // tm, n_split),
        in_specs=[pl.BlockSpec((tm, D_MODEL), lambda i, j: (i, 0)),
                  pl.BlockSpec(g.shape, lambda i, j: (0, 0)),
                  pl.BlockSpec((D_MODEL, tf), lambda i, j: (0, j)),
                  pl.BlockSpec((D_MODEL, tf), lambda i, j: (0, j)),
                  pl.BlockSpec((tf, D_MODEL), lambda i, j: (j, 0))],
        out_specs=pl.BlockSpec((tm, D_MODEL), lambda i, j: (i, 0)),
        out_shape=jax.ShapeDtypeStruct((T, D_MODEL), F32),
        scratch_shapes=[pltpu.VMEM((tm, D_MODEL), BF16), pltpu.VMEM((tm, D_MODEL), F32)],
        compiler_params=_params(("parallel", "arbitrary")), name="ffn",
    )(x2, g, wg, wu, wo)


NSA_TM = 256


def _nsa_prep_kernel(x_ref, pos_ref, gmix_ref, w_ref, f64_ref, nq_ref, nk_ref,
                     gq_ref, gks_ref, gkw_ref,
                     qc_ref, qr_ref, kvs_ref, kvw_ref, kc_ref, vc_ref, gt_ref):
    xn = _rms_rows(x_ref[...], gmix_ref[...]).astype(BF16)
    y = _dot(xn, w_ref[...])
    c64, s64 = _rope_tables(pos_ref[...], f64_ref[...], 64, 8)
    lane = _lane_iota(c64.shape)
    lo64 = (lane % 64) < 8
    first64 = lane < 64
    c64k = jnp.where(first64, c64, 1.0)
    s64k = jnp.where(first64, s64, 0.0)
    nq = nq_ref[...]
    nk = nk_ref[...]

    def col(j):
        return y[:, j * LANES:(j + 1) * LANES]

    for j in range(8):
        qn = _head_norm(col(j), nq, gq_ref[...])
        qr = _rope(qn, c64, s64, lo64, 8)
        qc_ref[2 * j] = qn[:, :64].astype(BF16)
        qc_ref[2 * j + 1] = qn[:, 64:].astype(BF16)
        qr_ref[2 * j] = qr[:, :64].astype(BF16)
        qr_ref[2 * j + 1] = qr[:, 64:].astype(BF16)
    for g in range(NSA_GROUPS):
        yc = col(8 + g)
        kn = jnp.where(first64, _head_norm(yc, nk, gks_ref[...]), yc)
        kvs_ref[g] = _rope(kn, c64k, s64k, lo64, 8).astype(BF16)
        yc = col(12 + g)
        kn = jnp.where(first64, _head_norm(yc, nk, gkw_ref[...]), yc)
        kvw_ref[g] = _rope(kn, c64k, s64k, lo64, 8).astype(BF16)
    kc_ref[...] = y[:, 16 * LANES:18 * LANES]
    vc_ref[...] = y[:, 18 * LANES:20 * LANES]
    gates = jax.nn.sigmoid(col(20))
    for g in range(NSA_GROUPS):
        gt_ref[g] = gates[:, 12 * g:12 * (g + 1)]


def _nsa_prep(x2, pos2, gmix, w, tabs, B, S):
    T = x2.shape[0]
    tm = NSA_TM
    nt = S // tm
    f64, nq, nk, gq, gks, gkw = tabs

    def full(a):
        return pl.BlockSpec(a.shape, lambda i: (0,) * a.ndim)

    def hm(width, heads):
        return pl.BlockSpec((None, heads, tm, width), lambda i: (i // nt, 0, i % nt, 0))

    def tokm(width):
        return pl.BlockSpec((None, tm, width), lambda i: (i // nt, i % nt, 0))

    out_shape = (
        jax.ShapeDtypeStruct((B, 16, S, 64), BF16),
        jax.ShapeDtypeStruct((B, 16, S, 64), BF16),
        jax.ShapeDtypeStruct((B, 4, S, 128), BF16),
        jax.ShapeDtypeStruct((B, 4, S, 128), BF16),
        jax.ShapeDtypeStruct((B, S, 256), F32),
        jax.ShapeDtypeStruct((B, S, 256), F32),
        jax.ShapeDtypeStruct((B, 4, S, 12), F32),
    )
    out_specs = (hm(64, 16), hm(64, 16), hm(128, 4), hm(128, 4), tokm(256), tokm(256), hm(12, 4))
    in_specs = [pl.BlockSpec((tm, D_MODEL), lambda i: (i, 0)),
                pl.BlockSpec((tm, 1), lambda i: (i, 0)),
                full(gmix), full(w), full(f64), full(nq), full(nk), full(gq), full(gks), full(gkw)]
    return pl.pallas_call(
        _nsa_prep_kernel, grid=(T // tm,), in_specs=in_specs, out_specs=out_specs,
        out_shape=out_shape, compiler_params=_params(("parallel",)), name="nsa_prep",
    )(x2, pos2, gmix, w, f64, nq, nk, gq, gks, gkw)


def _compress_one(x16, pa, pb, w1a, w1b, w2):
    n16 = x16.shape[0]
    h_a = _dot((x16 + pa).astype(BF16), w1a)
    h_b = _dot((x16 + pb).astype(BF16), w1b)
    pre = h_a + pltpu.roll(h_b, n16 - 1, 0)
    act = pre * jax.nn.sigmoid(pre)
    return _dot(act.astype(BF16), w2)


def _compress_kernel(xk_ref, xv_ref, pk_ref, pv_ref, w1k_ref, w1v_ref, w2k_ref, w2v_ref, gk_ref, o_ref):
    half = w1k_ref.shape[0] // 2
    k = _compress_one(xk_ref[...], pk_ref[0:1, :], pk_ref[1:2, :],
                      w1k_ref[:half, :], w1k_ref[half:, :], w2k_ref[...])
    k = _rms_rows(k, gk_ref[...])
    v = _compress_one(xv_ref[...], pv_ref[0:1, :], pv_ref[1:2, :],
                      w1v_ref[:half, :], w1v_ref[half:, :], w2v_ref[...])
    o_ref[...] = jnp.concatenate([k, v], axis=1).astype(BF16)


def _compress(xk16, xv16, pk, pv, w1k, w1v, w2k, w2v, gk):
    B, G, n16, width = xk16.shape

    def full(a):
        return pl.BlockSpec(a.shape, lambda b, g: (0,) * a.ndim)

    xspec = pl.BlockSpec((None, None, n16, width), lambda b, g: (b, g, 0, 0))
    return pl.pallas_call(
        _compress_kernel, grid=(B, G),
        in_specs=[xspec, xspec, full(pk), full(pv), full(w1k), full(w1v), full(w2k), full(w2v), full(gk)],
        out_specs=pl.BlockSpec((None, None, n16, 128), lambda b, g: (b, g, 0, 0)),
        out_shape=jax.ShapeDtypeStruct((B, G, n16, 128), BF16),
        compiler_params=_params(("parallel", "parallel")), name="nsa_compress",
    )(xk16, xv16, pk, pv, w1k, w1v, w2k, w2v, gk)


NSA_TQ = 128
NSA_KC = 512
NSA_WC = 128


def _nsa_kernel(qc_ref, qr_ref, kvc_ref, kvs_ref, kvw_ref, gt_ref, o_ref,
                mx_ref, m_ref, l_ref, acc_ref, *, n_cmp, n_top):
    tq, kc, wc = NSA_TQ, NSA_KC, NSA_WC
    HG = NSA_HEADS // NSA_GROUPS
    S = kvs_ref.shape[0]
    n_sel = S // NSA_SEL_LEN
    n16 = kvc_ref.shape[0]
    t0 = pl.program_id(2) * tq
    zpad = jnp.zeros((tq, HEAD_DIM), BF16)

    def stack(ref):
        return jnp.concatenate(
            [jnp.concatenate([ref[j], zpad], axis=1) for j in range(HG)], axis=0)

    def head_rows(j):
        return slice(j * tq, (j + 1) * tq)

    kvc = kvc_ref[...]
    s_c = _dot_nt(stack(qc_ref), kvc)
    n_id = lax.broadcasted_iota(I32, (tq, n16), 1)
    q_id = t0 + lax.broadcasted_iota(I32, (tq, n16), 0)
    mask_c = (n_id < n_cmp) & (n_id * NSA_CMP_STRIDE + (NSA_CMP_LEN - 1) <= q_id)
    p_sum = jnp.zeros((tq, n16), F32)
    pcs = []
    for j in range(HG):
        sm = jnp.where(mask_c, s_c[head_rows(j)], NEG_INF)
        mx = jnp.max(sm, axis=-1, keepdims=True)
        p = jnp.where(mask_c, jnp.exp(sm - mx), 0.0)
        p = p / jnp.maximum(jnp.sum(p, axis=-1, keepdims=True), TINY)
        p_sum = p_sum + p
        pcs.append(p.astype(BF16))
    o_c = _dot(jnp.concatenate(pcs, axis=0), kvc)

    r_id = lax.broadcasted_iota(I32, (n16, n_sel), 0) * NSA_CMP_STRIDE
    b_id = lax.broadcasted_iota(I32, (n16, n_sel), 1) * NSA_SEL_LEN
    cover = ((r_id < b_id + NSA_SEL_LEN) & (r_id + NSA_CMP_LEN > b_id)
             & (lax.broadcasted_iota(I32, (n16, n_sel), 0) < n_cmp))
    imp = _split_dot(p_sum, jnp.where(cover, 1.0, 0.0).astype(BF16))
    blk = lax.broadcasted_iota(I32, (tq, n_sel), 1)
    cur = (t0 + lax.broadcasted_iota(I32, (tq, n_sel), 0)) // NSA_SEL_LEN
    forced = (blk == 0) | (blk == cur) | (blk == cur - 1)
    imp = jnp.where(forced, NSA_FORCE, imp)
    imp = jnp.where(blk <= cur, imp, NEG_INF)
    sel = _rank_select(imp, None, n_top).astype(BF16)

    nkc = (t0 + tq + kc - 1) // kc
    per = kc // NSA_SEL_LEN
    for c in range(S // kc):
        @pl.when(c < nkc)
        def _():
            e_row = lax.broadcasted_iota(I32, (n_sel, kc), 0)
            e_col = lax.broadcasted_iota(I32, (n_sel, kc), 1) // NSA_SEL_LEN + c * per
            mx_ref[c] = _dot(sel, jnp.where(e_row == e_col, 1.0, 0.0).astype(BF16))

    def reset():
        m_ref[...] = jnp.full(m_ref.shape, NEG_INF, F32)
        l_ref[...] = jnp.zeros(l_ref.shape, F32)
        acc_ref[...] = jnp.zeros(acc_ref.shape, F32)

    def update(q_st, kv, mask):
        s = _dot_nt(q_st, kv)
        ps = []
        for j in range(HG):
            rows = head_rows(j)
            p, m_new, l_new, alpha = _softmax_step(s[rows], mask, m_ref[rows], l_ref[rows])
            m_ref[rows] = m_new
            l_ref[rows] = l_new
            acc_ref[rows] = alpha * acc_ref[rows]
            ps.append(p.astype(BF16))
        acc_ref[...] += _dot(jnp.concatenate(ps, axis=0), kv)

    def result():
        return acc_ref[...] / jnp.maximum(l_ref[...], TINY)

    qr_st = stack(qr_ref)

    reset()
    qpos_k = t0 + lax.broadcasted_iota(I32, (tq, kc), 0)
    kiota = lax.broadcasted_iota(I32, (tq, kc), 1)

    def sel_chunk(c, carry):
        k0 = pl.multiple_of(c * kc, kc)
        mask = (mx_ref[c] > 0.5) & (k0 + kiota <= qpos_k)
        update(qr_st, kvs_ref[pl.ds(k0, kc), :], mask)
        return carry

    lax.fori_loop(0, nkc, sel_chunk, 0)
    o_s = result()

    reset()
    r_w = lax.broadcasted_iota(I32, (tq, wc), 0)
    j_w = lax.broadcasted_iota(I32, (tq, wc), 1)
    for c in range((NSA_WINDOW + tq) // wc):
        start = t0 - NSA_WINDOW + c * wc

        @pl.when(start >= 0)
        def _():
            dist = (NSA_WINDOW - c * wc) + r_w - j_w
            mask = (dist >= 0) & (dist < NSA_WINDOW)
            k0 = pl.multiple_of(start, wc)
            update(qr_st, kvw_ref[pl.ds(k0, wc), :], mask)

    o_w = result()

    gt = gt_ref[...]
    lane = _lane_iota((tq, LANES))
    heads = []
    for j in range(HG):
        rows = head_rows(j)
        heads.append(gt[:, 3 * j:3 * j + 1] * o_c[rows] + gt[:, 3 * j + 1:3 * j + 2] * o_s[rows]
                     + gt[:, 3 * j + 2:3 * j + 3] * o_w[rows])
    for u in range(HG // 2):
        o_ref[:, u * LANES:(u + 1) * LANES] = jnp.where(
            lane < 64, pltpu.roll(heads[2 * u], 64, 1), heads[2 * u + 1]).astype(o_ref.dtype)


def _nsa_attention(qc, qr, kvc, kvs, kvw, gates):
    B, H, S, _ = qc.shape
    G = NSA_GROUPS
    HG = H // G
    tq, kc = NSA_TQ, NSA_KC
    n16 = kvc.shape[2]
    n_cmp = (S - NSA_CMP_LEN) // NSA_CMP_STRIDE + 1
    n_top = min(NSA_SEL_TOPK, S // NSA_SEL_LEN)
    grid = (B, G, S // tq)
    qspec = pl.BlockSpec((None, HG, tq, 64), lambda b, g, i: (b, g, i, 0))
    kvspec = pl.BlockSpec((None, None, S, 128), lambda b, g, i: (b, g, 0, 0))
    in_specs = [qspec, qspec,
                pl.BlockSpec((None, None, n16, 128), lambda b, g, i: (b, g, 0, 0)),
                kvspec, kvspec,
                pl.BlockSpec((None, None, tq, 12), lambda b, g, i: (b, g, i, 0))]
    return pl.pallas_call(
        functools.partial(_nsa_kernel, n_cmp=n_cmp, n_top=n_top), grid=grid, in_specs=in_specs,
        out_specs=pl.BlockSpec((None, tq, HG * 64), lambda b, g, i: (b, i, g)),
        out_shape=jax.ShapeDtypeStruct((B, S, H * 64), BF16),
        scratch_shapes=[pltpu.VMEM((S // kc, tq, kc), F32),
                        pltpu.VMEM((HG * tq, 1), F32), pltpu.VMEM((HG * tq, 1), F32),
                        pltpu.VMEM((HG * tq, LANES), F32)],
        compiler_params=_params(("parallel", "parallel", "parallel")), name="nsa_attention",
    )(qc, qr, kvc, kvs, kvw, gates)


def _rope_freq_row(period, rot):
    half = rot // 2
    inv_freq = ROPE_THETA ** (-(jnp.arange(half, dtype=F32) * 2.0 / rot))
    lane = jnp.arange(LANES) % period
    f = jnp.where(lane < rot, inv_freq[lane % half], 0.0)
    return f.reshape(1, LANES).astype(F32)


def _norm_matrices():
    r = jnp.arange(LANES)
    same = (r[:, None] // 64) == (r[None, :] // 64)
    nq = jnp.where(same, 1.0 / 64, 0.0).astype(BF16)
    nk = jnp.where(same & (r[:, None] < 64), 1.0 / 64, 0.0).astype(BF16)
    return nq, nk


def _q_gain(g, scale):
    return (jnp.tile(g.astype(F32), 2) * scale).reshape(1, LANES)


def _k_gain(g):
    return jnp.concatenate([g.astype(F32), jnp.ones((64,), F32)]).reshape(1, LANES)


def _interleave_kv(wk, wv, n_heads):
    d = wk.shape[0]
    wk = wk.reshape(d, n_heads, 64)
    wv = wv.reshape(d, n_heads, 64)
    return jnp.concatenate([wk, wv], axis=2).reshape(d, n_heads * 128)


def _split_cols(w, sizes):
    out, start = [], 0
    for n in sizes:
        out.append(w[:, start:start + n])
        start += n
    return out


def _mixer_layer0(x2, pos2, B, S, gmix, w_in, w_out, a_q_norm, a_k_norm, b_q_norm, b_k_norm):
    sizes = (512, 64, 64, 256, 32, 8, 512, 512, 512)
    waq, wak, wav, wiq, wik, wiw, wbq, wbk, wbv = _split_cols(w_in, sizes)
    pad = jnp.zeros((D_MODEL, LANES - 40), w_in.dtype)
    w = jnp.concatenate([waq, wbq, _interleave_kv(wbk, wbv, 8), wak, wav, wiq, wik, wiw, pad],
                        axis=1).astype(BF16)
    nq, nk = _norm_matrices()
    scale = HEAD_DIM ** -0.5
    tabs = (_rope_freq_row(64, 16), _rope_freq_row(32, 8), nq, nk,
            _q_gain(a_q_norm, scale), _q_gain(b_q_norm, scale), _k_gain(a_k_norm), _k_gain(b_k_norm))
    aq, bq, bkv, akv, iq, ik, iw, km = _ab_prep(x2, pos2, gmix, w, tabs, B, S)
    n_blk = S // MOBA_BLOCK
    kmean = km.reshape(B, n_blk, 8, 128).transpose(0, 2, 1, 3)
    o_a = _dsa_attention(iq, iw, ik, aq, akv).reshape(B * S, 512)
    o_b = _moba_attention(bq, bkv, kmean).reshape(B * S, 512)
    w_out = w_out.astype(BF16)
    return _out_proj([o_a, o_b], [w_out[:512], w_out[512:]], x2)


def _mixer_layer1(x2, pos2, B, S, gmix, w_in, w_out, q_norm, kcmp_norm, ksel_norm, kwin_norm,
                  pos_k, pos_v, w1_k, w2_k, w1_v, w2_v):
    G = NSA_GROUPS
    sizes = (1024,) + (256,) * 6 + (48,)
    wq, wkc, wvc, wks, wvs, wkw, wvw, wgt = _split_cols(w_in, sizes)
    pad = jnp.zeros((D_MODEL, LANES - 48), w_in.dtype)
    w = jnp.concatenate([wq, _interleave_kv(wks, wvs, G), _interleave_kv(wkw, wvw, G),
                         wkc, wvc, wgt, pad], axis=1).astype(BF16)
    nq, nk = _norm_matrices()
    tabs = (_rope_freq_row(64, 16), nq, nk, _q_gain(q_norm, HEAD_DIM ** -0.5),
            _k_gain(ksel_norm), _k_gain(kwin_norm))
    qc, qr, kvs, kvw, kc_raw, vc_raw, gates = _nsa_prep(x2, pos2, gmix, w, tabs, B, S)

    n16 = S // NSA_CMP_STRIDE

    def blocks16(t):
        return (t.reshape(B, n16, NSA_CMP_STRIDE, G, HEAD_DIM).transpose(0, 3, 1, 2, 4)
                .reshape(B, G, n16, NSA_CMP_STRIDE * HEAD_DIM))

    def pos_rows(p):
        return p.astype(F32).reshape(2, NSA_CMP_STRIDE * HEAD_DIM)

    kvc = _compress(blocks16(kc_raw), blocks16(vc_raw), pos_rows(pos_k), pos_rows(pos_v),
                    w1_k.astype(BF16), w1_v.astype(BF16), w2_k.astype(BF16), w2_v.astype(BF16),
                    kcmp_norm.astype(F32).reshape(1, HEAD_DIM))
    o = _nsa_attention(qc, qr, kvc, kvs, kvw, gates).reshape(B * S, NSA_HEADS * HEAD_DIM)
    return _out_proj([o], [w_out.astype(BF16)], x2)


def _mem_and_ffn(x2, mem, S, g_mem, g_src, w_q, w_kv, w_o, q_norm, k_norm, g_ffn, ffn_w_in, ffn_w_out):
    row = lambda v: v.astype(F32).reshape(1, -1)
    kv = _mem_kv(mem, row(g_src), w_kv.astype(BF16), row(k_norm))
    x2 = _mem_attn(x2, row(g_mem), w_q.astype(BF16), row(q_norm), kv, w_o.astype(BF16), S)
    wg = ffn_w_in[:, :D_FF].astype(BF16)
    wu = ffn_w_in[:, D_FF:].astype(BF16)
    return _ffn(x2, row(g_ffn), wg, wu, ffn_w_out.astype(BF16))


def kernel(x, mem, positions, norm_mix, norm_mem, norm_mem_src, norm_ffn, ab_w_in, ab_w_out, dsa_q_norm, dsa_k_norm, moba_q_norm, moba_k_norm, nsa_w_in, nsa_w_out, nsa_q_norm, nsa_kcmp_norm, nsa_ksel_norm, nsa_kwin_norm, nsa_cmp_pos_k, nsa_cmp_pos_v, nsa_cmp_w1_k, nsa_cmp_w2_k, nsa_cmp_w1_v, nsa_cmp_w2_v, mem_w_q, mem_w_kv, mem_w_o, mem_q_norm, mem_k_norm, ffn_w_in, ffn_w_out):
    B, S, D = x.shape
    depth = norm_mix.shape[0]
    x2 = x.reshape(B * S, D)
    pos2 = positions.astype(F32).reshape(B * S, 1)
    row = lambda v: v.astype(F32).reshape(1, -1)
    for i in range(depth):
        j = i // 2
        if i % 2 == 0:
            x2 = _mixer_layer0(x2, pos2, B, S, row(norm_mix[i]), ab_w_in[j], ab_w_out[j],
                               dsa_q_norm[j], dsa_k_norm[j], moba_q_norm[j], moba_k_norm[j])
        else:
            x2 = _mixer_layer1(x2, pos2, B, S, row(norm_mix[i]), nsa_w_in[j], nsa_w_out[j],
                               nsa_q_norm[j], nsa_kcmp_norm[j], nsa_ksel_norm[j], nsa_kwin_norm[j],
                               nsa_cmp_pos_k[j], nsa_cmp_pos_v[j], nsa_cmp_w1_k[j], nsa_cmp_w2_k[j],
                               nsa_cmp_w1_v[j], nsa_cmp_w2_v[j])
        x2 = _mem_and_ffn(x2, mem, S, norm_mem[i], norm_mem_src[i], mem_w_q[i], mem_w_kv[i], mem_w_o[i],
                          mem_q_norm[i], mem_k_norm[i], norm_ffn[i], ffn_w_in[i], ffn_w_out[i])
    return x2.reshape(B, S, D)
```

```python
import functools
import math

import jax
import jax.numpy as jnp
from jax import lax
from jax.experimental import pallas as pl
from jax.experimental.pallas import tpu as pltpu

F32 = jnp.float32
BF16 = jnp.bfloat16
I32 = jnp.int32

D_MODEL = 1024
N_MEM = 256
HEAD_DIM = 64
ROPE_THETA = 500000.0
RMS_EPS = 1e-6
NEG_INF = -1e30
TINY = 1e-20

DSA_HEADS = 8
DSA_IDX_HEADS = 8
DSA_IDX_DIM = 32
DSA_TOPK = 256
MOBA_HEADS = 8
MOBA_BLOCK = 256
MOBA_TOPK = 3
NSA_HEADS = 16
NSA_GROUPS = 4
NSA_CMP_LEN = 32
NSA_CMP_STRIDE = 16
NSA_SEL_LEN = 64
NSA_SEL_TOPK = 16
NSA_WINDOW = 512
NSA_FORCE = 1e4
MEM_HEADS = 4
MEM_HEAD_DIM = 128
D_FF = ((8 * D_MODEL + 3 * 256 - 1) // (3 * 256)) * 256

LANES = 128
SUBLANES = 8
INT_MIN = -(2 ** 31)
VMEM_LIMIT = 56 * 1024 * 1024

ATT_T = 256
MASK_BIAS = -1e30
M_FLOOR = -1e29
LOG2E = math.log2(math.e)
Q_SCALE = HEAD_DIM ** -0.5 * LOG2E

NT_DIMS = (((1,), (1,)), ((), ()))


def _dot(a, b):
    return jnp.dot(a, b, preferred_element_type=F32)


def _dot_nt(a, b):
    return lax.dot_general(a, b, NT_DIMS, preferred_element_type=F32)


def _split_bf16(a):
    hi = a.astype(BF16)
    return hi, (a - hi.astype(F32)).astype(BF16)


def _split_dot(a, b):
    hi, lo = _split_bf16(a)
    return _dot(hi, b) + _dot(lo, b)


def _rms_rows(x, gain):
    ms = jnp.mean(x * x, axis=-1, keepdims=True)
    return x * lax.rsqrt(ms + RMS_EPS) * gain


def _params(sem):
    return pltpu.CompilerParams(dimension_semantics=sem, vmem_limit_bytes=VMEM_LIMIT)


def _head_norm(y, norm_m, gain):
    ms = _split_dot(y * y, norm_m)
    return y * lax.rsqrt(ms + RMS_EPS) * gain


def _rope(y, c, s, lo_mask, half):
    sw = jnp.where(lo_mask, pltpu.roll(y, LANES - half, 1), pltpu.roll(y, half, 1))
    return y * c + sw * s


def _lane_iota(shape):
    return lax.broadcasted_iota(I32, shape, 1)


def _row_iota(shape):
    return lax.broadcasted_iota(I32, shape, 0)


def _rope_tables(pos, ftab, period, half):
    ang = pos * ftab
    lane = _lane_iota(ang.shape) % period
    c = jnp.cos(ang)
    s = jnp.sin(ang) * jnp.where(lane < half, -1.0, 1.0)
    return c, s


def _kv_column(yc, nk, gain, c64k, s64k, lo64, first64):
    kn = jnp.where(first64, _head_norm(yc, nk, gain), yc)
    return _rope(kn, c64k, s64k, lo64, 8)


def _ab_prep_kernel(x_ref, pos_ref, gmix_ref, w_ref, f64_ref, f32_ref, nq_ref, nk_ref,
                    gaq_ref, gbq_ref, gak_ref, gbk_ref,
                    aq_ref, bq_ref, bkv_ref, bkvt_ref, akv_ref, akvt_ref, iq_ref, ik_ref, iwt_ref, km_ref):
    xn = _rms_rows(x_ref[...], gmix_ref[...]).astype(BF16)
    y = _dot(xn, w_ref[...])
    pos = pos_ref[...]
    c64, s64 = _rope_tables(pos, f64_ref[...], 64, 8)
    c32, s32 = _rope_tables(pos, f32_ref[...], 32, 4)
    lane = _lane_iota(c64.shape)
    lo64 = (lane % 64) < 8
    lo32 = (lane % 32) < 4
    first64 = lane < 64
    c64k = jnp.where(first64, c64, 1.0)
    s64k = jnp.where(first64, s64, 0.0)
    first32 = lane < 32
    c32k = jnp.where(first32, c32, 1.0)
    s32k = jnp.where(first32, s32, 0.0)
    nq = nq_ref[...]
    nk = nk_ref[...]

    def col(j):
        return y[:, j * LANES:(j + 1) * LANES]

    for j in range(4):
        q = _rope(_head_norm(col(j), nq, gaq_ref[...]), c64, s64, lo64, 8)
        aq_ref[2 * j] = q[:, :64].astype(BF16)
        aq_ref[2 * j + 1] = q[:, 64:].astype(BF16)
    for j in range(4):
        q = _rope(_head_norm(col(4 + j), nq, gbq_ref[...]), c64, s64, lo64, 8)
        bq_ref[2 * j] = q[:, :64].astype(BF16)
        bq_ref[2 * j + 1] = q[:, 64:].astype(BF16)
    for h in range(8):
        kv = _kv_column(col(8 + h), nk, gbk_ref[...], c64k, s64k, lo64, first64)
        bkv_ref[h] = kv.astype(BF16)
        bkvt_ref[h] = kv.T.astype(BF16)
        km_ref[h:h + 1, :] = jnp.mean(kv, axis=0, keepdims=True)
    kv = _kv_column(col(16), nk, gak_ref[...], c64k, s64k, lo64, first64)
    akv_ref[...] = kv.astype(BF16)
    akvt_ref[...] = kv.T.astype(BF16)
    for j in range(2):
        q = _rope(col(17 + j), c32, s32, lo32, 4)
        for u in range(4):
            iq_ref[4 * j + u] = q[:, 32 * u:32 * (u + 1)].astype(BF16)
    yc = col(19)
    ik_ref[...] = _rope(yc, c32k, s32k, lo32, 4)[:, :32].astype(BF16)
    iwt_ref[...] = yc.T[32:40, :]


def _ab_prep(x2, pos2, gmix, w, tabs, B, S):
    T = x2.shape[0]
    tm = ATT_T
    nt = S // tm
    n_cols = w.shape[1]
    f64, f32t, nq, nk, gaq, gbq, gak, gbk = tabs

    def full(a):
        return pl.BlockSpec(a.shape, lambda i: (0,) * a.ndim)

    def hm(width, heads=8):
        return pl.BlockSpec((None, heads, tm, width), lambda i: (i // nt, 0, i % nt, 0))

    def tokm(width):
        return pl.BlockSpec((None, tm, width), lambda i: (i // nt, i % nt, 0))

    out_shape = (
        jax.ShapeDtypeStruct((B, 8, S, 64), BF16),
        jax.ShapeDtypeStruct((B, 8, S, 64), BF16),
        jax.ShapeDtypeStruct((B, 8, S, 128), BF16),
        jax.ShapeDtypeStruct((B, 8, nt, 128, tm), BF16),
        jax.ShapeDtypeStruct((B, S, 128), BF16),
        jax.ShapeDtypeStruct((B, nt, 128, tm), BF16),
        jax.ShapeDtypeStruct((B, 8, S, 32), BF16),
        jax.ShapeDtypeStruct((B, S, 32), BF16),
        jax.ShapeDtypeStruct((B, 8, S), F32),
        jax.ShapeDtypeStruct((T // tm, 8, 128), F32),
    )
    out_specs = (hm(64), hm(64), hm(128),
                 pl.BlockSpec((None, 8, None, 128, tm), lambda i: (i // nt, 0, i % nt, 0, 0)),
                 tokm(128),
                 pl.BlockSpec((None, None, 128, tm), lambda i: (i // nt, i % nt, 0, 0)),
                 hm(32), tokm(32),
                 pl.BlockSpec((None, 8, tm), lambda i: (i // nt, 0, i % nt)),
                 pl.BlockSpec((None, 8, 128), lambda i: (i, 0, 0)))
    in_specs = [pl.BlockSpec((tm, D_MODEL), lambda i: (i, 0)),
                pl.BlockSpec((tm, 1), lambda i: (i, 0)),
                full(gmix), pl.BlockSpec((D_MODEL, n_cols), lambda i: (0, 0)),
                full(f64), full(f32t), full(nq), full(nk), full(gaq), full(gbq), full(gak), full(gbk)]
    return pl.pallas_call(
        _ab_prep_kernel, grid=(T // tm,), in_specs=in_specs, out_specs=out_specs,
        out_shape=out_shape, compiler_params=_params(("parallel",)), name="ab_prep",
    )(x2, pos2, gmix, w, f64, f32t, nq, nk, gaq, gbq, gak, gbk)


def _pad_q(q):
    return jnp.concatenate([q, jnp.zeros_like(q)], axis=1)


def _flash_init(tq):
    return (jnp.full((1, tq), M_FLOOR, F32), jnp.zeros((SUBLANES, tq), F32), jnp.zeros((LANES, tq), F32))


def _flash_step(q, kv, kvt, bias, state):
    m, l8, acc = state
    s = _dot_nt(kv, q)
    if bias is not None:
        s = s + bias
    m_new = jnp.maximum(m, jnp.max(s, axis=0, keepdims=True))
    p = jnp.exp2(s - m_new)
    alpha = jnp.exp2(m - m_new)
    l8 = alpha * l8 + p.reshape(-1, SUBLANES, p.shape[1]).sum(axis=0)
    acc = alpha * acc + _dot(kvt, p.astype(BF16))
    return m_new, l8, acc


def _flash_out(state):
    _, l8, acc = state
    return acc / jnp.maximum(jnp.sum(l8, axis=0, keepdims=True), TINY)


def _causal_bias(t):
    return jnp.where(_row_iota((t, t)) <= _lane_iota((t, t)), 0.0, MASK_BIAS)


def _store_heads(o_ref, heads_t):
    tq = heads_t[0].shape[1]
    lane = _lane_iota((tq, LANES))
    for u in range(len(heads_t) // 2):
        even = pltpu.roll(heads_t[2 * u].T, 64, 1)
        odd = heads_t[2 * u + 1].T
        o_ref[:, u * LANES:(u + 1) * LANES] = jnp.where(lane < 64, even, odd).astype(o_ref.dtype)


def _rank_select_t(v, n_valid, n_top):
    n = v.shape[0]
    row = _row_iota(v.shape)
    rank = jnp.zeros(v.shape, F32)
    for m in range(n):
        vm = v[m:m + 1, :]
        ahead = (vm > v) | ((vm == v) & (m < row))
        if n_valid is not None:
            ahead = ahead & (m < n_valid)
        rank = rank + jnp.where(ahead, 1.0, 0.0)
    sel = rank < n_top
    if n_valid is not None:
        sel = sel & (row < n_valid)
    return jnp.where(sel, 1.0, 0.0)


def _dsa_kernel(iq_ref, iwt_ref, ik_ref, aq_ref, akv_ref, akvt_ref, o_ref,
                sk_ref, bias_ref, xcut_ref, *, k_top, index_bits):
    t = ATT_T
    i = pl.program_id(1)
    n_ch = i + 1
    kio = _row_iota((t, t))
    qio = _lane_iota((t, t))

    def causal(c):
        return (c - i) * t + kio <= qio

    def score_chunk(c, carry):
        k0 = pl.multiple_of(c * t, t)
        ikc = ik_ref[pl.ds(k0, t), :]
        sc = jnp.zeros((t, t), F32)
        for h in range(DSA_IDX_HEADS):
            logit = _dot_nt(ikc, iq_ref[h])
            sc = sc + iwt_ref[h:h + 1, :] * jnp.maximum(logit, 0.0)
        sc = jnp.where(sc == 0.0, 0.0, sc)
        bits = pltpu.bitcast(sc, I32)
        key = bits ^ ((bits >> 31) & 0x7FFFFFFF)
        sk_ref[c] = jnp.where(causal(c), key, INT_MIN)
        return carry

    lax.fori_loop(0, n_ch, score_chunk, 0)

    def count(pred):
        def body(c, acc8):
            ind = jnp.where(pred(sk_ref[c], c), 1.0, 0.0)
            return acc8 + ind.reshape(-1, SUBLANES, t).sum(axis=0)
        acc8 = lax.fori_loop(0, n_ch, body, jnp.zeros((SUBLANES, t), F32))
        return jnp.sum(acc8, axis=0, keepdims=True)

    def bit_step(b, thr):
        cand = thr + lax.shift_left(jnp.int32(1), 31 - b)
        return jnp.where(count(lambda blk, c: blk >= cand) >= k_top, cand, thr)

    thr = lax.fori_loop(0, 32, bit_step, jnp.full((1, t), INT_MIN, I32))

    need = k_top - count(lambda blk, c: blk > thr)
    n_ge = count(lambda blk, c: blk >= thr)
    xcut_ref[...] = jnp.full((1, t), 2 ** 30, I32)

    @pl.when(jnp.max(n_ge) > k_top)
    def _():
        def x_step(b, x):
            cand = x + lax.shift_left(jnp.int32(1), index_bits - 1 - b)
            ties_below = count(lambda blk, c: (blk == thr) & (c * t + kio < cand))
            return jnp.where(ties_below <= need, cand, x)
        xcut_ref[...] = lax.fori_loop(0, index_bits, x_step, jnp.zeros((1, t), I32))

    xcut = xcut_ref[...]

    def bias_chunk(c, carry):
        blk = sk_ref[c]
        keep = (blk > thr) | ((blk == thr) & (c * t + kio < xcut))
        bias_ref[c] = jnp.where(keep & causal(c), 0.0, MASK_BIAS)
        return carry

    lax.fori_loop(0, n_ch, bias_chunk, 0)

    outs = []
    for h in range(DSA_HEADS):
        q = _pad_q(aq_ref[h])

        def body(c, st):
            k0 = pl.multiple_of(c * t, t)
            return _flash_step(q, akv_ref[pl.ds(k0, t), :], akvt_ref[c], bias_ref[c], st)

        outs.append(_flash_out(lax.fori_loop(0, n_ch, body, _flash_init(t))))
    _store_heads(o_ref, outs)


def _dsa_attention(iq, iwt, ik, aq, akv, akvt):
    B, _, S, _ = aq.shape
    t = ATT_T
    nt = S // t
    k_top = min(DSA_TOPK, S // 4)
    in_specs = [
        pl.BlockSpec((None, 8, t, 32), lambda b, i: (b, 0, i, 0)),
        pl.BlockSpec((None, 8, t), lambda b, i: (b, 0, i)),
        pl.BlockSpec((None, S, 32), lambda b, i: (b, 0, 0)),
        pl.BlockSpec((None, 8, t, 64), lambda b, i: (b, 0, i, 0)),
        pl.BlockSpec((None, S, 128), lambda b, i: (b, 0, 0)),
        pl.BlockSpec((None, nt, 128, t), lambda b, i: (b, 0, 0, 0)),
    ]
    return pl.pallas_call(
        functools.partial(_dsa_kernel, k_top=k_top, index_bits=S.bit_length()),
        grid=(B, nt), in_specs=in_specs,
        out_specs=pl.BlockSpec((None, t, 512), lambda b, i: (b, i, 0)),
        out_shape=jax.ShapeDtypeStruct((B, S, 512), BF16),
        scratch_shapes=[pltpu.VMEM((nt, t, t), I32), pltpu.VMEM((nt, t, t), F32),
                        pltpu.VMEM((1, t), I32)],
        compiler_params=_params(("parallel", "parallel")), name="dsa_attention",
    )(iq, iwt, ik, aq, akv, akvt)


def _moba_kernel(q_ref, kv_ref, kvt_ref, km_ref, o_ref, sel_ref, *, n_top):
    t = ATT_T
    own = pl.program_id(2)
    causal = _causal_bias(t)
    outs = []
    for hh in range(2):
        q = _pad_q(q_ref[hh])
        km_hi, km_lo = _split_bf16(km_ref[hh])
        gate = _dot_nt(km_hi, q) + _dot_nt(km_lo, q)
        sel_ref[...] = (_rank_select_t(gate, own, n_top) - 1.0) * (-MASK_BIAS)

        def body(n, st):
            k0 = pl.multiple_of(n * t, t)
            return _flash_step(q, kv_ref[hh, pl.ds(k0, t), :], kvt_ref[hh, n],
                               sel_ref[pl.ds(n, 1), :], st)

        st = lax.fori_loop(0, own, body, _flash_init(t))
        k0 = pl.multiple_of(own * t, t)
        st = _flash_step(q, kv_ref[hh, pl.ds(k0, t), :], kvt_ref[hh, own], causal, st)
        outs.append(_flash_out(st))
    _store_heads(o_ref, outs)


def _moba_attention(bq, bkv, bkvt, kmean):
    B, H, S, _ = bq.shape
    t = ATT_T
    assert t == MOBA_BLOCK
    n_blk = S // MOBA_BLOCK
    n_top = max(1, min(MOBA_TOPK, n_blk - 1))
    in_specs = [
        pl.BlockSpec((None, 2, t, 64), lambda b, h, i: (b, h, i, 0)),
        pl.BlockSpec((None, 2, S, 128), lambda b, h, i: (b, h, 0, 0)),
        pl.BlockSpec((None, 2, n_blk, 128, t), lambda b, h, i: (b, h, 0, 0, 0)),
        pl.BlockSpec((None, 2, n_blk, 128), lambda b, h, i: (b, h, 0, 0)),
    ]
    return pl.pallas_call(
        functools.partial(_moba_kernel, n_top=n_top), grid=(B, H // 2, S // t), in_specs=in_specs,
        out_specs=pl.BlockSpec((None, t, 128), lambda b, h, i: (b, i, h)),
        out_shape=jax.ShapeDtypeStruct((B, S, 512), BF16),
        scratch_shapes=[pltpu.VMEM((n_blk, t), F32)],
        compiler_params=_params(("parallel", "parallel", "parallel")), name="moba_attention",
    )(bq, bkv, bkvt, kmean)


def _out_proj_kernel(*refs, n_in):
    a_refs = refs[:n_in]
    w_refs = refs[n_in:2 * n_in]
    x_ref, o_ref = refs[2 * n_in], refs[2 * n_in + 1]
    acc = x_ref[...]
    for a_ref, w_ref in zip(a_refs, w_refs):
        acc = acc + _dot(a_ref[...], w_ref[...])
    o_ref[...] = acc


def _out_proj(parts, weights, x2, tm=512):
    T = x2.shape[0]
    n_in = len(parts)
    in_specs = ([pl.BlockSpec((tm, p.shape[1]), lambda i: (i, 0)) for p in parts]
                + [pl.BlockSpec(w.shape, lambda i: (0, 0)) for w in weights]
                + [pl.BlockSpec((tm, D_MODEL), lambda i: (i, 0))])
    return pl.pallas_call(
        functools.partial(_out_proj_kernel, n_in=n_in), grid=(T // tm,), in_specs=in_specs,
        out_specs=pl.BlockSpec((tm, D_MODEL), lambda i: (i, 0)),
        out_shape=jax.ShapeDtypeStruct((T, D_MODEL), F32),
        compiler_params=_params(("parallel",)), name="out_proj",
    )(*parts, *weights, x2)


def _lane_group_norm(y, gain, width):
    outs = []
    for j in range(y.shape[1] // width):
        yc = y[:, j * width:(j + 1) * width]
        outs.append(_rms_rows(yc, gain))
    return jnp.concatenate(outs, axis=1)


def _mem_kv_kernel(m_ref, g_ref, w_ref, gk_ref, o_ref):
    mn = _rms_rows(m_ref[...], g_ref[...]).astype(BF16)
    y = _dot(mn, w_ref[...])
    hw = MEM_HEADS * MEM_HEAD_DIM
    k = _lane_group_norm(y[:, :hw], gk_ref[...], MEM_HEAD_DIM)
    o_ref[...] = jnp.concatenate([k, y[:, hw:]], axis=1).astype(BF16)


def _mem_kv(mem, g, w, gk):
    B, M, _ = mem.shape
    n = w.shape[1]
    return pl.pallas_call(
        _mem_kv_kernel, grid=(B,),
        in_specs=[pl.BlockSpec((None, M, D_MODEL), lambda b: (b, 0, 0)),
                  pl.BlockSpec(g.shape, lambda b: (0, 0)),
                  pl.BlockSpec(w.shape, lambda b: (0, 0)),
                  pl.BlockSpec(gk.shape, lambda b: (0, 0))],
        out_specs=pl.BlockSpec((None, M, n), lambda b: (b, 0, 0)),
        out_shape=jax.ShapeDtypeStruct((B, M, n), BF16),
        compiler_params=_params(("parallel",)), name="mem_kv",
    )(mem, g, w, gk)


def _mem_attn_kernel(x_ref, g_ref, wq_ref, gq_ref, kv_ref, wo_ref, o_ref):
    x = x_ref[...]
    xn = _rms_rows(x, g_ref[...]).astype(BF16)
    q = _lane_group_norm(_dot(xn, wq_ref[...]), gq_ref[...], MEM_HEAD_DIM).astype(BF16)
    hw = MEM_HEADS * MEM_HEAD_DIM
    scale = MEM_HEAD_DIM ** -0.5
    outs = []
    for h in range(MEM_HEADS):
        cols = slice(h * MEM_HEAD_DIM, (h + 1) * MEM_HEAD_DIM)
        k = kv_ref[:, cols]
        v = kv_ref[:, hw + h * MEM_HEAD_DIM:hw + (h + 1) * MEM_HEAD_DIM]
        s = _dot_nt(q[:, cols], k) * scale
        p = jnp.exp(s - jnp.max(s, axis=-1, keepdims=True))
        p = p / jnp.sum(p, axis=-1, keepdims=True)
        outs.append(_dot(p.astype(BF16), v))
    o = jnp.concatenate(outs, axis=1).astype(BF16)
    o_ref[...] = x + _dot(o, wo_ref[...])


def _mem_attn(x2, g, wq, gq, kv, wo, S, tm=512):
    T = x2.shape[0]
    nt = S // tm
    M, n = kv.shape[1], kv.shape[2]
    return pl.pallas_call(
        _mem_attn_kernel, grid=(T // tm,),
        in_specs=[pl.BlockSpec((tm, D_MODEL), lambda i: (i, 0)),
                  pl.BlockSpec(g.shape, lambda i: (0, 0)),
                  pl.BlockSpec(wq.shape, lambda i: (0, 0)),
                  pl.BlockSpec(gq.shape, lambda i: (0, 0)),
                  pl.BlockSpec((None, M, n), lambda i: (i // nt, 0, 0)),
                  pl.BlockSpec(wo.shape, lambda i: (0, 0))],
        out_specs=pl.BlockSpec((tm, D_MODEL), lambda i: (i, 0)),
        out_shape=jax.ShapeDtypeStruct((T, D_MODEL), F32),
        compiler_params=_params(("parallel",)), name="mem_attn",
    )(x2, g, wq, gq, kv, wo)


def _ffn_kernel(x_ref, g_ref, wg_ref, wu_ref, wo_ref, o_ref, xn_ref, acc_ref):
    j = pl.program_id(1)

    @pl.when(j == 0)
    def _():
        xn_ref[...] = _rms_rows(x_ref[...], g_ref[...]).astype(BF16)
        acc_ref[...] = x_ref[...]

    xn = xn_ref[...]
    gate = _dot(xn, wg_ref[...])
    up = _dot(xn, wu_ref[...])
    act = (gate * jax.nn.sigmoid(gate) * up).astype(BF16)
    acc_ref[...] += _dot(act, wo_ref[...])

    @pl.when(j == pl.num_programs(1) - 1)
    def _():
        o_ref[...] = acc_ref[...]


def _ffn(x2, g, wg, wu, wo, tm=512, n_split=2):
    T = x2.shape[0]
    tf = D_FF // n_split
    return pl.pallas_call(
        _ffn_kernel, grid=(T // tm, n_split),
        in_specs=[pl.BlockSpec((tm, D_MODEL), lambda i, j: (i, 0)),
                  pl.BlockSpec(g.shape, lambda i, j: (0, 0)),
                  pl.BlockSpec((D_MODEL, tf), lambda i, j: (0, j)),
                  pl.BlockSpec((D_MODEL, tf), lambda i, j: (0, j)),
                  pl.BlockSpec((tf, D_MODEL), lambda i, j: (j, 0))],
        out_specs=pl.BlockSpec((tm, D_MODEL), lambda i, j: (i, 0)),
        out_shape=jax.ShapeDtypeStruct((T, D_MODEL), F32),
        scratch_shapes=[pltpu.VMEM((tm, D_MODEL), BF16), pltpu.VMEM((tm, D_MODEL), F32)],
        compiler_params=_params(("parallel", "arbitrary")), name="ffn",
    )(x2, g, wg, wu, wo)


def _nsa_prep_kernel(x_ref, pos_ref, gmix_ref, w_ref, f64_ref, nq_ref, nk_ref,
                     gq_ref, gks_ref, gkw_ref,
                     qc_ref, qr_ref, kvs_ref, kvst_ref, kvw_ref, kvwt_ref, kc_ref, vc_ref, gtt_ref):
    xn = _rms_rows(x_ref[...], gmix_ref[...]).astype(BF16)
    y = _dot(xn, w_ref[...])
    c64, s64 = _rope_tables(pos_ref[...], f64_ref[...], 64, 8)
    lane = _lane_iota(c64.shape)
    lo64 = (lane % 64) < 8
    first64 = lane < 64
    c64k = jnp.where(first64, c64, 1.0)
    s64k = jnp.where(first64, s64, 0.0)
    nq = nq_ref[...]
    nk = nk_ref[...]

    def col(j):
        return y[:, j * LANES:(j + 1) * LANES]

    for j in range(8):
        qn = _head_norm(col(j), nq, gq_ref[...])
        qr = _rope(qn, c64, s64, lo64, 8)
        qc_ref[2 * j] = qn[:, :64].astype(BF16)
        qc_ref[2 * j + 1] = qn[:, 64:].astype(BF16)
        qr_ref[2 * j] = qr[:, :64].astype(BF16)
        qr_ref[2 * j + 1] = qr[:, 64:].astype(BF16)
    for g in range(NSA_GROUPS):
        kv = _kv_column(col(8 + g), nk, gks_ref[...], c64k, s64k, lo64, first64)
        kvs_ref[g] = kv.astype(BF16)
        kvst_ref[g] = kv.T.astype(BF16)
        kv = _kv_column(col(12 + g), nk, gkw_ref[...], c64k, s64k, lo64, first64)
        kvw_ref[g] = kv.astype(BF16)
        kvwt_ref[g] = kv.T.astype(BF16)
    kc_ref[...] = y[:, 16 * LANES:18 * LANES]
    vc_ref[...] = y[:, 18 * LANES:20 * LANES]
    gates_t = jax.nn.sigmoid(col(20)).T
    for g in range(NSA_GROUPS):
        gtt_ref[g] = gates_t[12 * g:12 * (g + 1), :]


def _nsa_prep(x2, pos2, gmix, w, tabs, B, S):
    T = x2.shape[0]
    tm = ATT_T
    nt = S // tm
    f64, nq, nk, gq, gks, gkw = tabs

    def full(a):
        return pl.BlockSpec(a.shape, lambda i: (0,) * a.ndim)

    def hm(width, heads):
        return pl.BlockSpec((None, heads, tm, width), lambda i: (i // nt, 0, i % nt, 0))

    def hmt(heads):
        return pl.BlockSpec((None, heads, None, 128, tm), lambda i: (i // nt, 0, i % nt, 0, 0))

    def tokm(width):
        return pl.BlockSpec((None, tm, width), lambda i: (i // nt, i % nt, 0))

    out_shape = (
        jax.ShapeDtypeStruct((B, 16, S, 64), BF16),
        jax.ShapeDtypeStruct((B, 16, S, 64), BF16),
        jax.ShapeDtypeStruct((B, 4, S, 128), BF16),
        jax.ShapeDtypeStruct((B, 4, nt, 128, tm), BF16),
        jax.ShapeDtypeStruct((B, 4, S, 128), BF16),
        jax.ShapeDtypeStruct((B, 4, nt, 128, tm), BF16),
        jax.ShapeDtypeStruct((B, S, 256), F32),
        jax.ShapeDtypeStruct((B, S, 256), F32),
        jax.ShapeDtypeStruct((B, 4, 12, S), F32),
    )
    out_specs = (hm(64, 16), hm(64, 16), hm(128, 4), hmt(4), hm(128, 4), hmt(4), tokm(256), tokm(256),
                 pl.BlockSpec((None, 4, 12, tm), lambda i: (i // nt, 0, 0, i % nt)))
    in_specs = [pl.BlockSpec((tm, D_MODEL), lambda i: (i, 0)),
                pl.BlockSpec((tm, 1), lambda i: (i, 0)),
                full(gmix), full(w), full(f64), full(nq), full(nk), full(gq), full(gks), full(gkw)]
    return pl.pallas_call(
        _nsa_prep_kernel, grid=(T // tm,), in_specs=in_specs, out_specs=out_specs,
        out_shape=out_shape, compiler_params=_params(("parallel",)), name="nsa_prep",
    )(x2, pos2, gmix, w, f64, nq, nk, gq, gks, gkw)


def _compress_one(x16, pa, pb, w1a, w1b, w2):
    n16 = x16.shape[0]
    h_a = _dot((x16 + pa).astype(BF16), w1a)
    h_b = _dot((x16 + pb).astype(BF16), w1b)
    pre = h_a + pltpu.roll(h_b, n16 - 1, 0)
    act = pre * jax.nn.sigmoid(pre)
    return _dot(act.astype(BF16), w2)


def _compress_kernel(xk_ref, xv_ref, pk_ref, pv_ref, w1k_ref, w1v_ref, w2k_ref, w2v_ref, gk_ref,
                     o_ref, ot_ref):
    half = w1k_ref.shape[0] // 2
    k = _compress_one(xk_ref[...], pk_ref[0:1, :], pk_ref[1:2, :],
                      w1k_ref[:half, :], w1k_ref[half:, :], w2k_ref[...])
    k = _rms_rows(k, gk_ref[...])
    v = _compress_one(xv_ref[...], pv_ref[0:1, :], pv_ref[1:2, :],
                      w1v_ref[:half, :], w1v_ref[half:, :], w2v_ref[...])
    kv = jnp.concatenate([k, v], axis=1)
    o_ref[...] = kv.astype(BF16)
    ot_ref[...] = kv.T.astype(BF16)


def _compress(xk16, xv16, pk, pv, w1k, w1v, w2k, w2v, gk):
    B, G, n16, width = xk16.shape

    def full(a):
        return pl.BlockSpec(a.shape, lambda b, g: (0,) * a.ndim)

    xspec = pl.BlockSpec((None, None, n16, width), lambda b, g: (b, g, 0, 0))
    return pl.pallas_call(
        _compress_kernel, grid=(B, G),
        in_specs=[xspec, xspec, full(pk), full(pv), full(w1k), full(w1v), full(w2k), full(w2v), full(gk)],
        out_specs=(pl.BlockSpec((None, None, n16, 128), lambda b, g: (b, g, 0, 0)),
                   pl.BlockSpec((None, None, 128, n16), lambda b, g: (b, g, 0, 0))),
        out_shape=(jax.ShapeDtypeStruct((B, G, n16, 128), BF16),
                   jax.ShapeDtypeStruct((B, G, 128, n16), BF16)),
        compiler_params=_params(("parallel", "parallel")), name="nsa_compress",
    )(xk16, xv16, pk, pv, w1k, w1v, w2k, w2v, gk)


def _nsa_kernel(qc_ref, qr_ref, kvc_ref, kvct_ref, kvs_ref, kvst_ref, kvw_ref, kvwt_ref, gtt_ref,
                o_ref, sel_ref, *, n_cmp, n_top):
    t = ATT_T
    HG = NSA_HEADS // NSA_GROUPS
    n_sel = sel_ref.shape[0]
    n16 = kvc_ref.shape[0]
    i = pl.program_id(2)
    t0 = i * t
    kio = _row_iota((t, t))
    qio = _lane_iota((t, t))

    kvc = kvc_ref[...]
    kvct = kvct_ref[...]
    n_id = _row_iota((n16, t))
    q_id = t0 + _lane_iota((n16, t))
    visible = (n_id < n_cmp) & (n_id * NSA_CMP_STRIDE + (NSA_CMP_LEN - 1) <= q_id)
    bias_c = jnp.where(visible, 0.0, MASK_BIAS)
    p_sum = jnp.zeros((n16, t), F32)
    o_c = []
    for j in range(HG):
        s = _dot_nt(kvc, _pad_q(qc_ref[j])) + bias_c
        m = jnp.maximum(jnp.max(s, axis=0, keepdims=True), M_FLOOR)
        p = jnp.exp2(s - m)
        p = p / jnp.maximum(jnp.sum(p, axis=0, keepdims=True), TINY)
        p_sum = p_sum + p
        o_c.append(_dot(kvct, p.astype(BF16)))

    b_id = _row_iota((n_sel, n16)) * NSA_SEL_LEN
    r_id = _lane_iota((n_sel, n16)) * NSA_CMP_STRIDE
    cover_t = ((r_id < b_id + NSA_SEL_LEN) & (r_id + NSA_CMP_LEN > b_id)
               & (_lane_iota((n_sel, n16)) < n_cmp))
    cover_t = jnp.where(cover_t, 1.0, 0.0).astype(BF16)
    p_hi, p_lo = _split_bf16(p_sum)
    imp = _dot(cover_t, p_hi) + _dot(cover_t, p_lo)
    blk = _row_iota((n_sel, t))
    cur = lax.shift_right_logical(t0 + _lane_iota((n_sel, t)), NSA_SEL_LEN.bit_length() - 1)
    forced = (blk == 0) | (blk == cur) | (blk == cur - 1)
    imp = jnp.where(forced, NSA_FORCE, imp)
    imp = jnp.where(blk <= cur, imp, NEG_INF)
    sel_ref[...] = (_rank_select_t(imp, None, n_top) - 1.0) * (-MASK_BIAS)

    per = t // NSA_SEL_LEN

    def sel_bias(c):
        return jnp.concatenate(
            [jnp.broadcast_to(sel_ref[pl.ds(c * per + u, 1), :], (NSA_SEL_LEN, t)) for u in range(per)],
            axis=0)

    causal = _causal_bias(t)
    o_s, o_w = [], []
    for j in range(HG):
        q = _pad_q(qr_ref[j])

        def sel_body(c, st):
            k0 = pl.multiple_of(c * t, t)
            return _flash_step(q, kvs_ref[pl.ds(k0, t), :], kvst_ref[c], sel_bias(c), st)

        st = lax.fori_loop(0, i, sel_body, _flash_init(t))
        k0 = pl.multiple_of(t0, t)
        st = _flash_step(q, kvs_ref[pl.ds(k0, t), :], kvst_ref[i], sel_bias(i) + causal, st)
        o_s.append(_flash_out(st))

        def win_body(c, st):
            k0 = pl.multiple_of(c * t, t)
            dist = (i - c) * t + qio - kio
            bias = jnp.where((dist >= 0) & (dist < NSA_WINDOW), 0.0, MASK_BIAS)
            return _flash_step(q, kvw_ref[pl.ds(k0, t), :], kvwt_ref[c], bias, st)

        st = lax.fori_loop(jnp.maximum(i - NSA_WINDOW // t, 0), i + 1, win_body, _flash_init(t))
        o_w.append(_flash_out(st))

    gt = gtt_ref[...]
    heads = [gt[3 * j:3 * j + 1, :] * o_c[j] + gt[3 * j + 1:3 * j + 2, :] * o_s[j]
             + gt[3 * j + 2:3 * j + 3, :] * o_w[j] for j in range(HG)]
    _store_heads(o_ref, heads)


def _nsa_attention(qc, qr, kvc, kvct, kvs, kvst, kvw, kvwt, gates_t):
    B, H, S, _ = qc.shape
    G = NSA_GROUPS
    HG = H // G
    t = ATT_T
    nt = S // t
    n16 = kvc.shape[2]
    n_cmp = (S - NSA_CMP_LEN) // NSA_CMP_STRIDE + 1
    n_sel = S // NSA_SEL_LEN
    n_top = min(NSA_SEL_TOPK, n_sel)
    qspec = pl.BlockSpec((None, HG, t, 64), lambda b, g, i: (b, g, i, 0))
    kvspec = pl.BlockSpec((None, None, S, 128), lambda b, g, i: (b, g, 0, 0))
    kvtspec = pl.BlockSpec((None, None, nt, 128, t), lambda b, g, i: (b, g, 0, 0, 0))
    in_specs = [qspec, qspec,
                pl.BlockSpec((None, None, n16, 128), lambda b, g, i: (b, g, 0, 0)),
                pl.BlockSpec((None, None, 128, n16), lambda b, g, i: (b, g, 0, 0)),
                kvspec, kvtspec, kvspec, kvtspec,
                pl.BlockSpec((None, None, 12, t), lambda b, g, i: (b, g, 0, i))]
    return pl.pallas_call(
        functools.partial(_nsa_kernel, n_cmp=n_cmp, n_top=n_top), grid=(B, G, nt), in_specs=in_specs,
        out_specs=pl.BlockSpec((None, t, HG * 64), lambda b, g, i: (b, i, g)),
        out_shape=jax.ShapeDtypeStruct((B, S, H * 64), BF16),
        scratch_shapes=[pltpu.VMEM((n_sel, t), F32)],
        compiler_params=_params(("parallel", "parallel", "parallel")), name="nsa_attention",
    )(qc, qr, kvc, kvct, kvs, kvst, kvw, kvwt, gates_t)


def _rope_freq_row(period, rot):
    half = rot // 2
    inv_freq = ROPE_THETA ** (-(jnp.arange(half, dtype=F32) * 2.0 / rot))
    lane = jnp.arange(LANES) % period
    f = jnp.where(lane < rot, inv_freq[lane % half], 0.0)
    return f.reshape(1, LANES).astype(F32)


def _norm_matrices():
    r = jnp.arange(LANES)
    same = (r[:, None] // 64) == (r[None, :] // 64)
    nq = jnp.where(same, 1.0 / 64, 0.0).astype(BF16)
    nk = jnp.where(same & (r[:, None] < 64), 1.0 / 64, 0.0).astype(BF16)
    return nq, nk


def _q_gain(g):
    return (jnp.tile(g.astype(F32), 2) * Q_SCALE).reshape(1, LANES)


def _k_gain(g):
    return jnp.concatenate([g.astype(F32), jnp.ones((64,), F32)]).reshape(1, LANES)


def _interleave_kv(wk, wv, n_heads):
    d = wk.shape[0]
    wk = wk.reshape(d, n_heads, 64)
    wv = wv.reshape(d, n_heads, 64)
    return jnp.concatenate([wk, wv], axis=2).reshape(d, n_heads * 128)


def _split_cols(w, sizes):
    out, start = [], 0
    for n in sizes:
        out.append(w[:, start:start + n])
        start += n
    return out


def _mixer_layer0(x2, pos2, B, S, gmix, w_in, w_out, a_q_norm, a_k_norm, b_q_norm, b_k_norm):
    sizes = (512, 64, 64, 256, 32, 8, 512, 512, 512)
    waq, wak, wav, wiq, wik, wiw, wbq, wbk, wbv = _split_cols(w_in, sizes)
    pad = jnp.zeros((D_MODEL, LANES - 40), w_in.dtype)
    w = jnp.concatenate([waq, wbq, _interleave_kv(wbk, wbv, 8), wak, wav, wiq, wik, wiw, pad],
                        axis=1).astype(BF16)
    nq, nk = _norm_matrices()
    tabs = (_rope_freq_row(64, 16), _rope_freq_row(32, 8), nq, nk,
            _q_gain(a_q_norm), _q_gain(b_q_norm), _k_gain(a_k_norm), _k_gain(b_k_norm))
    aq, bq, bkv, bkvt, akv, akvt, iq, ik, iwt, km = _ab_prep(x2, pos2, gmix, w, tabs, B, S)
    n_blk = S // MOBA_BLOCK
    kmean = km.reshape(B, n_blk, 8, 128).transpose(0, 2, 1, 3)
    o_a = _dsa_attention(iq, iwt, ik, aq, akv, akvt).reshape(B * S, 512)
    o_b = _moba_attention(bq, bkv, bkvt, kmean).reshape(B * S, 512)
    w_out = w_out.astype(BF16)
    return _out_proj([o_a, o_b], [w_out[:512], w_out[512:]], x2)


def _mixer_layer1(x2, pos2, B, S, gmix, w_in, w_out, q_norm, kcmp_norm, ksel_norm, kwin_norm,
                  pos_k, pos_v, w1_k, w2_k, w1_v, w2_v):
    G = NSA_GROUPS
    sizes = (1024,) + (256,) * 6 + (48,)
    wq, wkc, wvc, wks, wvs, wkw, wvw, wgt = _split_cols(w_in, sizes)
    pad = jnp.zeros((D_MODEL, LANES - 48), w_in.dtype)
    w = jnp.concatenate([wq, _interleave_kv(wks, wvs, G), _interleave_kv(wkw, wvw, G),
                         wkc, wvc, wgt, pad], axis=1).astype(BF16)
    nq, nk = _norm_matrices()
    tabs = (_rope_freq_row(64, 16), nq, nk, _q_gain(q_norm), _k_gain(ksel_norm), _k_gain(kwin_norm))
    qc, qr, kvs, kvst, kvw, kvwt, kc_raw, vc_raw, gates_t = _nsa_prep(x2, pos2, gmix, w, tabs, B, S)

    n16 = S // NSA_CMP_STRIDE

    def blocks16(t):
        return (t.reshape(B, n16, NSA_CMP_STRIDE, G, HEAD_DIM).transpose(0, 3, 1, 2, 4)
                .reshape(B, G, n16, NSA_CMP_STRIDE * HEAD_DIM))

    def pos_rows(p):
        return p.astype(F32).reshape(2, NSA_CMP_STRIDE * HEAD_DIM)

    kvc, kvct = _compress(blocks16(kc_raw), blocks16(vc_raw), pos_rows(pos_k), pos_rows(pos_v),
                          w1_k.astype(BF16), w1_v.astype(BF16), w2_k.astype(BF16), w2_v.astype(BF16),
                          kcmp_norm.astype(F32).reshape(1, HEAD_DIM))
    o = _nsa_attention(qc, qr, kvc, kvct, kvs, kvst, kvw, kvwt, gates_t)
    return _out_proj([o.reshape(B * S, NSA_HEADS * HEAD_DIM)], [w_out.astype(BF16)], x2)


def _mem_and_ffn(x2, mem, S, g_mem, g_src, w_q, w_kv, w_o, q_norm, k_norm, g_ffn, ffn_w_in, ffn_w_out):
    row = lambda v: v.astype(F32).reshape(1, -1)
    kv = _mem_kv(mem, row(g_src), w_kv.astype(BF16), row(k_norm))
    x2 = _mem_attn(x2, row(g_mem), w_q.astype(BF16), row(q_norm), kv, w_o.astype(BF16), S)
    wg = ffn_w_in[:, :D_FF].astype(BF16)
    wu = ffn_w_in[:, D_FF:].astype(BF16)
    return _ffn(x2, row(g_ffn), wg, wu, ffn_w_out.astype(BF16))


def kernel(x, mem, positions, norm_mix, norm_mem, norm_mem_src, norm_ffn, ab_w_in, ab_w_out, dsa_q_norm, dsa_k_norm, moba_q_norm, moba_k_norm, nsa_w_in, nsa_w_out, nsa_q_norm, nsa_kcmp_norm, nsa_ksel_norm, nsa_kwin_norm, nsa_cmp_pos_k, nsa_cmp_pos_v, nsa_cmp_w1_k, nsa_cmp_w2_k, nsa_cmp_w1_v, nsa_cmp_w2_v, mem_w_q, mem_w_kv, mem_w_o, mem_q_norm, mem_k_norm, ffn_w_in, ffn_w_out):
    B, S, D = x.shape
    depth = norm_mix.shape[0]
    x2 = x.reshape(B * S, D)
    pos2 = positions.astype(F32).reshape(B * S, 1)
    row = lambda v: v.astype(F32).reshape(1, -1)
    for i in range(depth):
        j = i // 2
        if i % 2 == 0:
            x2 = _mixer_layer0(x2, pos2, B, S, row(norm_mix[i]), ab_w_in[j], ab_w_out[j],
                               dsa_q_norm[j], dsa_k_norm[j], moba_q_norm[j], moba_k_norm[j])
        else:
            x2 = _mixer_layer1(x2, pos2, B, S, row(norm_mix[i]), nsa_w_in[j], nsa_w_out[j],
                               nsa_q_norm[j], nsa_kcmp_norm[j], nsa_ksel_norm[j], nsa_kwin_norm[j],
                               nsa_cmp_pos_k[j], nsa_cmp_pos_v[j], nsa_cmp_w1_k[j], nsa_cmp_w2_k[j],
                               nsa_cmp_w1_v[j], nsa_cmp_w2_v[j])
        x2 = _mem_and_ffn(x2, mem, S, norm_mem[i], norm_mem_src[i], mem_w_q[i], mem_w_kv[i], mem_w_o[i],
                          mem_q_norm[i], mem_k_norm[i], norm_ffn[i], ffn_w_in[i], ffn_w_out[i])
    return x2.reshape(B, S, D)
```

```python
import functools
import math

import jax
import jax.numpy as jnp
from jax import lax
from jax.experimental import pallas as pl
from jax.experimental.pallas import tpu as pltpu

F32 = jnp.float32
BF16 = jnp.bfloat16
I32 = jnp.int32

D_MODEL = 1024
N_MEM = 256
HEAD_DIM = 64
ROPE_THETA = 500000.0
RMS_EPS = 1e-6
NEG_INF = -1e30
TINY = 1e-20

DSA_HEADS = 8
DSA_IDX_HEADS = 8
DSA_IDX_DIM = 32
DSA_TOPK = 256
MOBA_HEADS = 8
MOBA_BLOCK = 256
MOBA_TOPK = 3
NSA_HEADS = 16
NSA_GROUPS = 4
NSA_CMP_LEN = 32
NSA_CMP_STRIDE = 16
NSA_SEL_LEN = 64
NSA_SEL_TOPK = 16
NSA_WINDOW = 512
NSA_FORCE = 1e4
MEM_HEADS = 4
MEM_HEAD_DIM = 128
D_FF = ((8 * D_MODEL + 3 * 256 - 1) // (3 * 256)) * 256

LANES = 128
SUBLANES = 8
INT_MIN = -(2 ** 31)
VMEM_LIMIT = 56 * 1024 * 1024

ATT_T = 256
MASK_BIAS = -1e30
M_FLOOR = -1e29
LOG2E = math.log2(math.e)
Q_SCALE = HEAD_DIM ** -0.5 * LOG2E

NT_DIMS = (((1,), (1,)), ((), ()))


def _dot(a, b):
    return jnp.dot(a, b, preferred_element_type=F32)


def _dot_nt(a, b):
    return lax.dot_general(a, b, NT_DIMS, preferred_element_type=F32)


def _split_bf16(a):
    hi = a.astype(BF16)
    return hi, (a - hi.astype(F32)).astype(BF16)


def _split_dot(a, b):
    hi, lo = _split_bf16(a)
    return _dot(hi, b) + _dot(lo, b)


def _rms_rows(x, gain):
    ms = jnp.mean(x * x, axis=-1, keepdims=True)
    return x * lax.rsqrt(ms + RMS_EPS) * gain


def _params(sem):
    return pltpu.CompilerParams(dimension_semantics=sem, vmem_limit_bytes=VMEM_LIMIT)


def _head_norm(y, norm_m, gain):
    ms = _split_dot(y * y, norm_m)
    return y * lax.rsqrt(ms + RMS_EPS) * gain


def _rope(y, c, s, lo_mask, half):
    sw = jnp.where(lo_mask, pltpu.roll(y, LANES - half, 1), pltpu.roll(y, half, 1))
    return y * c + sw * s


def _lane_iota(shape):
    return lax.broadcasted_iota(I32, shape, 1)


def _row_iota(shape):
    return lax.broadcasted_iota(I32, shape, 0)


def _rope_tables(pos, ftab, period, half):
    ang = pos * ftab
    lane = _lane_iota(ang.shape) % period
    c = jnp.cos(ang)
    s = jnp.sin(ang) * jnp.where(lane < half, -1.0, 1.0)
    return c, s


def _kv_column(yc, nk, gain, c64k, s64k, lo64, first64):
    kn = jnp.where(first64, _head_norm(yc, nk, gain), yc)
    return _rope(kn, c64k, s64k, lo64, 8)


def _ab_prep_kernel(x_ref, pos_ref, gmix_ref, w_ref, f64_ref, f32_ref, nq_ref, nk_ref,
                    gaq_ref, gbq_ref, gak_ref, gbk_ref,
                    aq_ref, bq_ref, bkv_ref, bkvt_ref, akv_ref, akvt_ref, iq_ref, ik_ref, iwt_ref, km_ref):
    xn = _rms_rows(x_ref[...], gmix_ref[...]).astype(BF16)
    y = _dot(xn, w_ref[...])
    pos = pos_ref[...]
    c64, s64 = _rope_tables(pos, f64_ref[...], 64, 8)
    c32, s32 = _rope_tables(pos, f32_ref[...], 32, 4)
    lane = _lane_iota(c64.shape)
    lo64 = (lane % 64) < 8
    lo32 = (lane % 32) < 4
    first64 = lane < 64
    c64k = jnp.where(first64, c64, 1.0)
    s64k = jnp.where(first64, s64, 0.0)
    first32 = lane < 32
    c32k = jnp.where(first32, c32, 1.0)
    s32k = jnp.where(first32, s32, 0.0)
    nq = nq_ref[...]
    nk = nk_ref[...]

    def col(j):
        return y[:, j * LANES:(j + 1) * LANES]

    for j in range(4):
        q = _rope(_head_norm(col(j), nq, gaq_ref[...]), c64, s64, lo64, 8)
        aq_ref[2 * j] = q[:, :64].astype(BF16)
        aq_ref[2 * j + 1] = q[:, 64:].astype(BF16)
    for j in range(4):
        q = _rope(_head_norm(col(4 + j), nq, gbq_ref[...]), c64, s64, lo64, 8)
        bq_ref[2 * j] = q[:, :64].astype(BF16)
        bq_ref[2 * j + 1] = q[:, 64:].astype(BF16)
    for h in range(8):
        kv = _kv_column(col(8 + h), nk, gbk_ref[...], c64k, s64k, lo64, first64)
        bkv_ref[h] = kv.astype(BF16)
        bkvt_ref[h] = kv.T.astype(BF16)
        km_ref[h:h + 1, :] = jnp.mean(kv, axis=0, keepdims=True)
    kv = _kv_column(col(16), nk, gak_ref[...], c64k, s64k, lo64, first64)
    akv_ref[...] = kv.astype(BF16)
    akvt_ref[...] = kv.T.astype(BF16)
    for j in range(2):
        q = _rope(col(17 + j), c32, s32, lo32, 4)
        for u in range(4):
            iq_ref[4 * j + u] = q[:, 32 * u:32 * (u + 1)].astype(BF16)
    yc = col(19)
    ik_ref[...] = _rope(yc, c32k, s32k, lo32, 4)[:, :32].astype(BF16)
    iwt_ref[...] = yc.T[32:40, :]


def _ab_prep(x2, pos2, gmix, w, tabs, B, S):
    T = x2.shape[0]
    tm = ATT_T
    nt = S // tm
    n_cols = w.shape[1]
    f64, f32t, nq, nk, gaq, gbq, gak, gbk = tabs

    def full(a):
        return pl.BlockSpec(a.shape, lambda i: (0,) * a.ndim)

    def hm(width, heads=8):
        return pl.BlockSpec((None, heads, tm, width), lambda i: (i // nt, 0, i % nt, 0))

    def tokm(width):
        return pl.BlockSpec((None, tm, width), lambda i: (i // nt, i % nt, 0))

    out_shape = (
        jax.ShapeDtypeStruct((B, 8, S, 64), BF16),
        jax.ShapeDtypeStruct((B, 8, S, 64), BF16),
        jax.ShapeDtypeStruct((B, 8, S, 128), BF16),
        jax.ShapeDtypeStruct((B, 8, nt, 128, tm), BF16),
        jax.ShapeDtypeStruct((B, S, 128), BF16),
        jax.ShapeDtypeStruct((B, nt, 128, tm), BF16),
        jax.ShapeDtypeStruct((B, 8, S, 32), BF16),
        jax.ShapeDtypeStruct((B, S, 32), BF16),
        jax.ShapeDtypeStruct((B, 8, S), F32),
        jax.ShapeDtypeStruct((T // tm, 8, 128), F32),
    )
    out_specs = (hm(64), hm(64), hm(128),
                 pl.BlockSpec((None, 8, None, 128, tm), lambda i: (i // nt, 0, i % nt, 0, 0)),
                 tokm(128),
                 pl.BlockSpec((None, None, 128, tm), lambda i: (i // nt, i % nt, 0, 0)),
                 hm(32), tokm(32),
                 pl.BlockSpec((None, 8, tm), lambda i: (i // nt, 0, i % nt)),
                 pl.BlockSpec((None, 8, 128), lambda i: (i, 0, 0)))
    in_specs = [pl.BlockSpec((tm, D_MODEL), lambda i: (i, 0)),
                pl.BlockSpec((tm, 1), lambda i: (i, 0)),
                full(gmix), pl.BlockSpec((D_MODEL, n_cols), lambda i: (0, 0)),
                full(f64), full(f32t), full(nq), full(nk), full(gaq), full(gbq), full(gak), full(gbk)]
    return pl.pallas_call(
        _ab_prep_kernel, grid=(T // tm,), in_specs=in_specs, out_specs=out_specs,
        out_shape=out_shape, compiler_params=_params(("parallel",)), name="ab_prep",
    )(x2, pos2, gmix, w, f64, f32t, nq, nk, gaq, gbq, gak, gbk)


def _pad_q(q):
    return jnp.concatenate([q, jnp.zeros_like(q)], axis=1)


class _Flash:
    def __init__(self, m_ref, l_ref, acc_ref, tq):
        self.m_ref, self.l_ref, self.acc_ref, self.tq = m_ref, l_ref, acc_ref, tq

    def reset(self):
        self.m_ref[...] = jnp.full(self.m_ref.shape, M_FLOOR, F32)
        self.l_ref[...] = jnp.zeros(self.l_ref.shape, F32)
        self.acc_ref[...] = jnp.zeros(self.acc_ref.shape, F32)

    def update(self, qs, kvs, kvts, biases):
        n = len(qs)
        tq = self.tq
        scores = [_dot_nt(kvs[i], qs[i]) for i in range(n)]
        probs, alphas = [], []
        for i in range(n):
            cols = slice(i * tq, (i + 1) * tq)
            s = scores[i] if biases[i] is None else scores[i] + biases[i]
            m = self.m_ref[:, cols]
            m_new = jnp.maximum(m, jnp.max(s, axis=0, keepdims=True))
            p = jnp.exp2(s - m_new)
            alpha = jnp.exp2(m - m_new)
            self.m_ref[:, cols] = m_new
            self.l_ref[:, cols] = alpha * self.l_ref[:, cols] + p.reshape(-1, SUBLANES, tq).sum(axis=0)
            probs.append(p.astype(BF16))
            alphas.append(alpha)
        for i in range(n):
            cols = slice(i * tq, (i + 1) * tq)
            pv, r0 = None, 0
            for kvt in kvts[i]:
                part = _dot(kvt, probs[i][r0:r0 + kvt.shape[1]])
                pv = part if pv is None else pv + part
                r0 += kvt.shape[1]
            self.acc_ref[:, cols] = alphas[i] * self.acc_ref[:, cols] + pv

    def result(self, slot):
        cols = slice(slot * self.tq, (slot + 1) * self.tq)
        l = jnp.sum(self.l_ref[:, cols], axis=0, keepdims=True)
        return self.acc_ref[:, cols] / jnp.maximum(l, TINY)


def _flash_scratch(n_slots, tq):
    return [pltpu.VMEM((1, n_slots * tq), F32), pltpu.VMEM((SUBLANES, n_slots * tq), F32),
            pltpu.VMEM((LANES, n_slots * tq), F32)]


def _softmax_direct(q, kv, kvts, bias):
    s = _dot_nt(kv, q) + bias
    m = jnp.maximum(jnp.max(s, axis=0, keepdims=True), M_FLOOR)
    p = jnp.exp2(s - m)
    inv_l = 1.0 / jnp.maximum(jnp.sum(p, axis=0, keepdims=True), TINY)
    pb = p.astype(BF16)
    o, r0 = None, 0
    for kvt in kvts:
        part = _dot(kvt, pb[r0:r0 + kvt.shape[1]])
        o = part if o is None else o + part
        r0 += kvt.shape[1]
    return p, inv_l, o


def _causal_bias(t):
    return jnp.where(_row_iota((t, t)) <= _lane_iota((t, t)), 0.0, MASK_BIAS)


def _store_heads(o_ref, heads_t):
    tq = heads_t[0].shape[1]
    lane = _lane_iota((tq, LANES))
    for u in range(len(heads_t) // 2):
        even = pltpu.roll(heads_t[2 * u].T, 64, 1)
        odd = heads_t[2 * u + 1].T
        o_ref[:, u * LANES:(u + 1) * LANES] = jnp.where(lane < 64, even, odd).astype(o_ref.dtype)


def _rank_select_t(v, n_valid, n_top):
    n = v.shape[0]
    row = _row_iota(v.shape)
    rank = jnp.zeros(v.shape, F32)
    for m in range(n):
        vm = v[m:m + 1, :]
        ahead = (vm > v) | ((vm == v) & (m < row))
        if n_valid is not None:
            ahead = ahead & (m < n_valid)
        rank = rank + jnp.where(ahead, 1.0, 0.0)
    sel = rank < n_top
    if n_valid is not None:
        sel = sel & (row < n_valid)
    return jnp.where(sel, 1.0, 0.0)


def _dsa_kernel(iq_ref, iwt_ref, ik_ref, aq_ref, akv_ref, akvt_ref, o_ref,
                sk_ref, bias_ref, xcut_ref, m_ref, l_ref, acc_ref, *, k_top, index_bits):
    t = ATT_T
    i = pl.program_id(1)
    n_ch = i + 1
    kio = _row_iota((t, t))
    qio = _lane_iota((t, t))

    def causal(c):
        return (c - i) * t + kio <= qio

    def score_chunk(c, carry):
        k0 = pl.multiple_of(c * t, t)
        ikc = ik_ref[pl.ds(k0, t), :]
        sc = jnp.zeros((t, t), F32)
        for h in range(DSA_IDX_HEADS):
            logit = _dot_nt(ikc, iq_ref[h])
            sc = sc + iwt_ref[h:h + 1, :] * jnp.maximum(logit, 0.0)
        sc = jnp.where(sc == 0.0, 0.0, sc)
        bits = pltpu.bitcast(sc, I32)
        key = bits ^ ((bits >> 31) & 0x7FFFFFFF)
        sk_ref[c] = jnp.where(causal(c), key, INT_MIN)
        return carry

    lax.fori_loop(0, n_ch, score_chunk, 0)

    def count(pred):
        def body(c, acc8):
            ind = jnp.where(pred(sk_ref[c], c), 1.0, 0.0)
            return acc8 + ind.reshape(-1, SUBLANES, t).sum(axis=0)
        acc8 = lax.fori_loop(0, n_ch, body, jnp.zeros((SUBLANES, t), F32))
        return jnp.sum(acc8, axis=0, keepdims=True)

    def bit_step(b, thr):
        cand = thr + lax.shift_left(jnp.int32(1), 31 - b)
        return jnp.where(count(lambda blk, c: blk >= cand) >= k_top, cand, thr)

    thr = lax.fori_loop(0, 32, bit_step, jnp.full((1, t), INT_MIN, I32))

    need = k_top - count(lambda blk, c: blk > thr)
    n_ge = count(lambda blk, c: blk >= thr)
    xcut_ref[...] = jnp.full((1, t), 2 ** 30, I32)

    @pl.when(jnp.max(n_ge) > k_top)
    def _():
        def x_step(b, x):
            cand = x + lax.shift_left(jnp.int32(1), index_bits - 1 - b)
            ties_below = count(lambda blk, c: (blk == thr) & (c * t + kio < cand))
            return jnp.where(ties_below <= need, cand, x)
        xcut_ref[...] = lax.fori_loop(0, index_bits, x_step, jnp.zeros((1, t), I32))

    xcut = xcut_ref[...]

    def bias_chunk(c, carry):
        blk = sk_ref[c]
        keep = (blk > thr) | ((blk == thr) & (c * t + kio < xcut))
        bias_ref[c] = jnp.where(keep & causal(c), 0.0, MASK_BIAS)
        return carry

    lax.fori_loop(0, n_ch, bias_chunk, 0)

    flash = _Flash(m_ref, l_ref, acc_ref, t)
    flash.reset()
    qs = [_pad_q(aq_ref[h]) for h in range(DSA_HEADS)]

    def att_chunk(c, carry):
        k0 = pl.multiple_of(c * t, t)
        kv = akv_ref[pl.ds(k0, t), :]
        kvt = akvt_ref[c]
        bias = bias_ref[c]
        n = DSA_HEADS
        flash.update(qs, [kv] * n, [[kvt]] * n, [bias] * n)
        return carry

    lax.fori_loop(0, n_ch, att_chunk, 0)
    _store_heads(o_ref, [flash.result(h) for h in range(DSA_HEADS)])


def _dsa_attention(iq, iwt, ik, aq, akv, akvt):
    B, _, S, _ = aq.shape
    t = ATT_T
    nt = S // t
    k_top = min(DSA_TOPK, S // 4)
    in_specs = [
        pl.BlockSpec((None, 8, t, 32), lambda b, i: (b, 0, i, 0)),
        pl.BlockSpec((None, 8, t), lambda b, i: (b, 0, i)),
        pl.BlockSpec((None, S, 32), lambda b, i: (b, 0, 0)),
        pl.BlockSpec((None, 8, t, 64), lambda b, i: (b, 0, i, 0)),
        pl.BlockSpec((None, S, 128), lambda b, i: (b, 0, 0)),
        pl.BlockSpec((None, nt, 128, t), lambda b, i: (b, 0, 0, 0)),
    ]
    return pl.pallas_call(
        functools.partial(_dsa_kernel, k_top=k_top, index_bits=S.bit_length()),
        grid=(B, nt), in_specs=in_specs,
        out_specs=pl.BlockSpec((None, t, 512), lambda b, i: (b, i, 0)),
        out_shape=jax.ShapeDtypeStruct((B, S, 512), BF16),
        scratch_shapes=[pltpu.VMEM((nt, t, t), I32), pltpu.VMEM((nt, t, t), F32),
                        pltpu.VMEM((1, t), I32)] + _flash_scratch(DSA_HEADS, t),
        compiler_params=_params(("parallel", "parallel")), name="dsa_attention",
    )(iq, iwt, ik, aq, akv, akvt)


MOBA_HPS = 4


def _moba_kernel(q_ref, kv_ref, kvt_ref, km_ref, o_ref, sel_ref, m_ref, l_ref, acc_ref, *, n_top):
    t = ATT_T
    own = pl.program_id(2)
    causal = _causal_bias(t)
    flash = _Flash(m_ref, l_ref, acc_ref, t)
    flash.reset()
    qs = []
    for hh in range(MOBA_HPS):
        q = _pad_q(q_ref[hh])
        km_hi, km_lo = _split_bf16(km_ref[hh])
        gate = _dot_nt(km_hi, q) + _dot_nt(km_lo, q)
        sel_ref[hh] = (_rank_select_t(gate, own, n_top) - 1.0) * (-MASK_BIAS)
        qs.append(q)

    def block_bias(hh, n):
        row = jnp.broadcast_to(sel_ref[hh, pl.ds(n, 1), :], (t, t))
        return jnp.where(n == own, causal, row)

    def pair(cc, carry):
        n0 = 2 * cc
        k0 = pl.multiple_of(n0 * t, 2 * t)
        heads = range(MOBA_HPS)
        flash.update(qs, [kv_ref[hh, pl.ds(k0, 2 * t), :] for hh in heads],
                     [[kvt_ref[hh, n0], kvt_ref[hh, n0 + 1]] for hh in heads],
                     [jnp.concatenate([block_bias(hh, n0), block_bias(hh, n0 + 1)], axis=0) for hh in heads])
        return carry

    lax.fori_loop(0, (own + 2) // 2, pair, 0)
    _store_heads(o_ref, [flash.result(hh) for hh in range(MOBA_HPS)])


def _moba_attention(bq, bkv, bkvt, kmean):
    B, H, S, _ = bq.shape
    t = ATT_T
    hps = MOBA_HPS
    n_blk = S // MOBA_BLOCK
    assert t == MOBA_BLOCK and n_blk % 2 == 0 and H % hps == 0
    n_top = max(1, min(MOBA_TOPK, n_blk - 1))
    in_specs = [
        pl.BlockSpec((None, hps, t, 64), lambda b, h, i: (b, h, i, 0)),
        pl.BlockSpec((None, hps, S, 128), lambda b, h, i: (b, h, 0, 0)),
        pl.BlockSpec((None, hps, n_blk, 128, t), lambda b, h, i: (b, h, 0, 0, 0)),
        pl.BlockSpec((None, hps, n_blk, 128), lambda b, h, i: (b, h, 0, 0)),
    ]
    return pl.pallas_call(
        functools.partial(_moba_kernel, n_top=n_top), grid=(B, H // hps, S // t), in_specs=in_specs,
        out_specs=pl.BlockSpec((None, t, hps * 64), lambda b, h, i: (b, i, h)),
        out_shape=jax.ShapeDtypeStruct((B, S, H * 64), BF16),
        scratch_shapes=[pltpu.VMEM((hps, n_blk, t), F32)] + _flash_scratch(hps, t),
        compiler_params=_params(("parallel", "parallel", "parallel")), name="moba_attention",
    )(bq, bkv, bkvt, kmean)


def _out_proj_kernel(*refs, n_in):
    a_refs = refs[:n_in]
    w_refs = refs[n_in:2 * n_in]
    x_ref, o_ref = refs[2 * n_in], refs[2 * n_in + 1]
    acc = x_ref[...]
    for a_ref, w_ref in zip(a_refs, w_refs):
        acc = acc + _dot(a_ref[...], w_ref[...])
    o_ref[...] = acc


def _out_proj(parts, weights, x2, tm=512):
    T = x2.shape[0]
    n_in = len(parts)
    in_specs = ([pl.BlockSpec((tm, p.shape[1]), lambda i: (i, 0)) for p in parts]
                + [pl.BlockSpec(w.shape, lambda i: (0, 0)) for w in weights]
                + [pl.BlockSpec((tm, D_MODEL), lambda i: (i, 0))])
    return pl.pallas_call(
        functools.partial(_out_proj_kernel, n_in=n_in), grid=(T // tm,), in_specs=in_specs,
        out_specs=pl.BlockSpec((tm, D_MODEL), lambda i: (i, 0)),
        out_shape=jax.ShapeDtypeStruct((T, D_MODEL), F32),
        compiler_params=_params(("parallel",)), name="out_proj",
    )(*parts, *weights, x2)


def _lane_group_norm(y, gain, width):
    outs = []
    for j in range(y.shape[1] // width):
        yc = y[:, j * width:(j + 1) * width]
        outs.append(_rms_rows(yc, gain))
    return jnp.concatenate(outs, axis=1)


def _mem_kv_kernel(m_ref, g_ref, w_ref, gk_ref, o_ref):
    mn = _rms_rows(m_ref[...], g_ref[...]).astype(BF16)
    y = _dot(mn, w_ref[...])
    hw = MEM_HEADS * MEM_HEAD_DIM
    k = _lane_group_norm(y[:, :hw], gk_ref[...], MEM_HEAD_DIM)
    o_ref[...] = jnp.concatenate([k, y[:, hw:]], axis=1).astype(BF16)


def _mem_kv(mem, g, w, gk):
    B, M, _ = mem.shape
    n = w.shape[1]
    return pl.pallas_call(
        _mem_kv_kernel, grid=(B,),
        in_specs=[pl.BlockSpec((None, M, D_MODEL), lambda b: (b, 0, 0)),
                  pl.BlockSpec(g.shape, lambda b: (0, 0)),
                  pl.BlockSpec(w.shape, lambda b: (0, 0)),
                  pl.BlockSpec(gk.shape, lambda b: (0, 0))],
        out_specs=pl.BlockSpec((None, M, n), lambda b: (b, 0, 0)),
        out_shape=jax.ShapeDtypeStruct((B, M, n), BF16),
        compiler_params=_params(("parallel",)), name="mem_kv",
    )(mem, g, w, gk)


def _mem_attn_kernel(x_ref, g_ref, wq_ref, gq_ref, kv_ref, wo_ref, o_ref):
    x = x_ref[...]
    xn = _rms_rows(x, g_ref[...]).astype(BF16)
    q = _lane_group_norm(_dot(xn, wq_ref[...]), gq_ref[...], MEM_HEAD_DIM).astype(BF16)
    hw = MEM_HEADS * MEM_HEAD_DIM
    scale = MEM_HEAD_DIM ** -0.5
    outs = []
    for h in range(MEM_HEADS):
        cols = slice(h * MEM_HEAD_DIM, (h + 1) * MEM_HEAD_DIM)
        k = kv_ref[:, cols]
        v = kv_ref[:, hw + h * MEM_HEAD_DIM:hw + (h + 1) * MEM_HEAD_DIM]
        s = _dot_nt(q[:, cols], k) * scale
        p = jnp.exp(s - jnp.max(s, axis=-1, keepdims=True))
        p = p / jnp.sum(p, axis=-1, keepdims=True)
        outs.append(_dot(p.astype(BF16), v))
    o = jnp.concatenate(outs, axis=1).astype(BF16)
    o_ref[...] = x + _dot(o, wo_ref[...])


def _mem_attn(x2, g, wq, gq, kv, wo, S, tm=512):
    T = x2.shape[0]
    nt = S // tm
    M, n = kv.shape[1], kv.shape[2]
    return pl.pallas_call(
        _mem_attn_kernel, grid=(T // tm,),
        in_specs=[pl.BlockSpec((tm, D_MODEL), lambda i: (i, 0)),
                  pl.BlockSpec(g.shape, lambda i: (0, 0)),
                  pl.BlockSpec(wq.shape, lambda i: (0, 0)),
                  pl.BlockSpec(gq.shape, lambda i: (0, 0)),
                  pl.BlockSpec((None, M, n), lambda i: (i // nt, 0, 0)),
                  pl.BlockSpec(wo.shape, lambda i: (0, 0))],
        out_specs=pl.BlockSpec((tm, D_MODEL), lambda i: (i, 0)),
        out_shape=jax.ShapeDtypeStruct((T, D_MODEL), F32),
        compiler_params=_params(("parallel",)), name="mem_attn",
    )(x2, g, wq, gq, kv, wo)


def _ffn_kernel(x_ref, g_ref, wg_ref, wu_ref, wo_ref, o_ref, xn_ref, acc_ref):
    j = pl.program_id(1)

    @pl.when(j == 0)
    def _():
        xn_ref[...] = _rms_rows(x_ref[...], g_ref[...]).astype(BF16)
        acc_ref[...] = x_ref[...]

    xn = xn_ref[...]
    gate = _dot(xn, wg_ref[...])
    up = _dot(xn, wu_ref[...])
    act = (gate * jax.nn.sigmoid(gate) * up).astype(BF16)
    acc_ref[...] += _dot(act, wo_ref[...])

    @pl.when(j == pl.num_programs(1) - 1)
    def _():
        o_ref[...] = acc_ref[...]


def _ffn(x2, g, wg, wu, wo, tm=512, n_split=2):
    T = x2.shape[0]
    tf = D_FF // n_split
    return pl.pallas_call(
        _ffn_kernel, grid=(T // tm, n_split),
        in_specs=[pl.BlockSpec((tm, D_MODEL), lambda i, j: (i, 0)),
                  pl.BlockSpec(g.shape, lambda i, j: (0, 0)),
                  pl.BlockSpec((D_MODEL, tf), lambda i, j: (0, j)),
                  pl.BlockSpec((D_MODEL, tf), lambda i, j: (0, j)),
                  pl.BlockSpec((tf, D_MODEL), lambda i, j: (j, 0))],
        out_specs=pl.BlockSpec((tm, D_MODEL), lambda i, j: (i, 0)),
        out_shape=jax.ShapeDtypeStruct((T, D_MODEL), F32),
        scratch_shapes=[pltpu.VMEM((tm, D_MODEL), BF16), pltpu.VMEM((tm, D_MODEL), F32)],
        compiler_params=_params(("parallel", "arbitrary")), name="ffn",
    )(x2, g, wg, wu, wo)


def _nsa_prep_kernel(x_ref, pos_ref, gmix_ref, w_ref, f64_ref, nq_ref, nk_ref,
                     gq_ref, gks_ref, gkw_ref,
                     qc_ref, qr_ref, kvs_ref, kvst_ref, kvw_ref, kvwt_ref, kc_ref, vc_ref, gtt_ref):
    xn = _rms_rows(x_ref[...], gmix_ref[...]).astype(BF16)
    y = _dot(xn, w_ref[...])
    c64, s64 = _rope_tables(pos_ref[...], f64_ref[...], 64, 8)
    lane = _lane_iota(c64.shape)
    lo64 = (lane % 64) < 8
    first64 = lane < 64
    c64k = jnp.where(first64, c64, 1.0)
    s64k = jnp.where(first64, s64, 0.0)
    nq = nq_ref[...]
    nk = nk_ref[...]

    def col(j):
        return y[:, j * LANES:(j + 1) * LANES]

    for j in range(8):
        qn = _head_norm(col(j), nq, gq_ref[...])
        qr = _rope(qn, c64, s64, lo64, 8)
        qc_ref[2 * j] = qn[:, :64].astype(BF16)
        qc_ref[2 * j + 1] = qn[:, 64:].astype(BF16)
        qr_ref[2 * j] = qr[:, :64].astype(BF16)
        qr_ref[2 * j + 1] = qr[:, 64:].astype(BF16)
    for g in range(NSA_GROUPS):
        kv = _kv_column(col(8 + g), nk, gks_ref[...], c64k, s64k, lo64, first64)
        kvs_ref[g] = kv.astype(BF16)
        kvst_ref[g] = kv.T.astype(BF16)
        kv = _kv_column(col(12 + g), nk, gkw_ref[...], c64k, s64k, lo64, first64)
        kvw_ref[g] = kv.astype(BF16)
        kvwt_ref[g] = kv.T.astype(BF16)
    kc_ref[...] = y[:, 16 * LANES:18 * LANES]
    vc_ref[...] = y[:, 18 * LANES:20 * LANES]
    gates_t = jax.nn.sigmoid(col(20)).T
    for g in range(NSA_GROUPS):
        gtt_ref[g] = gates_t[12 * g:12 * (g + 1), :]


def _nsa_prep(x2, pos2, gmix, w, tabs, B, S):
    T = x2.shape[0]
    tm = ATT_T
    nt = S // tm
    f64, nq, nk, gq, gks, gkw = tabs

    def full(a):
        return pl.BlockSpec(a.shape, lambda i: (0,) * a.ndim)

    def hm(width, heads):
        return pl.BlockSpec((None, heads, tm, width), lambda i: (i // nt, 0, i % nt, 0))

    def hmt(heads):
        return pl.BlockSpec((None, heads, None, 128, tm), lambda i: (i // nt, 0, i % nt, 0, 0))

    def tokm(width):
        return pl.BlockSpec((None, tm, width), lambda i: (i // nt, i % nt, 0))

    out_shape = (
        jax.ShapeDtypeStruct((B, 16, S, 64), BF16),
        jax.ShapeDtypeStruct((B, 16, S, 64), BF16),
        jax.ShapeDtypeStruct((B, 4, S, 128), BF16),
        jax.ShapeDtypeStruct((B, 4, nt, 128, tm), BF16),
        jax.ShapeDtypeStruct((B, 4, S, 128), BF16),
        jax.ShapeDtypeStruct((B, 4, nt, 128, tm), BF16),
        jax.ShapeDtypeStruct((B, S, 256), F32),
        jax.ShapeDtypeStruct((B, S, 256), F32),
        jax.ShapeDtypeStruct((B, 4, 12, S), F32),
    )
    out_specs = (hm(64, 16), hm(64, 16), hm(128, 4), hmt(4), hm(128, 4), hmt(4), tokm(256), tokm(256),
                 pl.BlockSpec((None, 4, 12, tm), lambda i: (i // nt, 0, 0, i % nt)))
    in_specs = [pl.BlockSpec((tm, D_MODEL), lambda i: (i, 0)),
                pl.BlockSpec((tm, 1), lambda i: (i, 0)),
                full(gmix), full(w), full(f64), full(nq), full(nk), full(gq), full(gks), full(gkw)]
    return pl.pallas_call(
        _nsa_prep_kernel, grid=(T // tm,), in_specs=in_specs, out_specs=out_specs,
        out_shape=out_shape, compiler_params=_params(("parallel",)), name="nsa_prep",
    )(x2, pos2, gmix, w, f64, nq, nk, gq, gks, gkw)


def _compress_one(x16, pa, pb, w1a, w1b, w2):
    n16 = x16.shape[0]
    h_a = _dot((x16 + pa).astype(BF16), w1a)
    h_b = _dot((x16 + pb).astype(BF16), w1b)
    pre = h_a + pltpu.roll(h_b, n16 - 1, 0)
    act = pre * jax.nn.sigmoid(pre)
    return _dot(act.astype(BF16), w2)


def _compress_kernel(xk_ref, xv_ref, pk_ref, pv_ref, w1k_ref, w1v_ref, w2k_ref, w2v_ref, gk_ref,
                     o_ref, ot_ref):
    half = w1k_ref.shape[0] // 2
    k = _compress_one(xk_ref[...], pk_ref[0:1, :], pk_ref[1:2, :],
                      w1k_ref[:half, :], w1k_ref[half:, :], w2k_ref[...])
    k = _rms_rows(k, gk_ref[...])
    v = _compress_one(xv_ref[...], pv_ref[0:1, :], pv_ref[1:2, :],
                      w1v_ref[:half, :], w1v_ref[half:, :], w2v_ref[...])
    kv = jnp.concatenate([k, v], axis=1)
    o_ref[...] = kv.astype(BF16)
    ot_ref[...] = kv.T.astype(BF16)


def _compress(xk16, xv16, pk, pv, w1k, w1v, w2k, w2v, gk):
    B, G, n16, width = xk16.shape

    def full(a):
        return pl.BlockSpec(a.shape, lambda b, g: (0,) * a.ndim)

    xspec = pl.BlockSpec((None, None, n16, width), lambda b, g: (b, g, 0, 0))
    return pl.pallas_call(
        _compress_kernel, grid=(B, G),
        in_specs=[xspec, xspec, full(pk), full(pv), full(w1k), full(w1v), full(w2k), full(w2v), full(gk)],
        out_specs=(pl.BlockSpec((None, None, n16, 128), lambda b, g: (b, g, 0, 0)),
                   pl.BlockSpec((None, None, 128, n16), lambda b, g: (b, g, 0, 0))),
        out_shape=(jax.ShapeDtypeStruct((B, G, n16, 128), BF16),
                   jax.ShapeDtypeStruct((B, G, 128, n16), BF16)),
        compiler_params=_params(("parallel", "parallel")), name="nsa_compress",
    )(xk16, xv16, pk, pv, w1k, w1v, w2k, w2v, gk)


def _nsa_kernel(qc_ref, qr_ref, kvc_ref, kvct_ref, kvs_ref, kvst_ref, kvw_ref, kvwt_ref, gtt_ref,
                o_ref, sel_ref, m_ref, l_ref, acc_ref, *, n_cmp, n_top):
    t = ATT_T
    HG = NSA_HEADS // NSA_GROUPS
    n_sel = sel_ref.shape[0]
    n16 = kvc_ref.shape[0]
    i = pl.program_id(2)
    t0 = i * t

    kvc = kvc_ref[...]
    kvct = kvct_ref[...]
    n_id = _row_iota((n16, t))
    q_id = t0 + _lane_iota((n16, t))
    visible = (n_id < n_cmp) & (n_id * NSA_CMP_STRIDE + (NSA_CMP_LEN - 1) <= q_id)
    bias_c = jnp.where(visible, 0.0, MASK_BIAS)
    p_sum = jnp.zeros((n16, t), F32)
    o_c = []
    for j in range(HG):
        p, inv_l, o = _softmax_direct(_pad_q(qc_ref[j]), kvc, [kvct], bias_c)
        p_sum = p_sum + p * inv_l
        o_c.append(o * inv_l)

    b_id = _row_iota((n_sel, n16)) * NSA_SEL_LEN
    r_id = _lane_iota((n_sel, n16)) * NSA_CMP_STRIDE
    cover_t = ((r_id < b_id + NSA_SEL_LEN) & (r_id + NSA_CMP_LEN > b_id)
               & (_lane_iota((n_sel, n16)) < n_cmp))
    cover_t = jnp.where(cover_t, 1.0, 0.0).astype(BF16)
    p_hi, p_lo = _split_bf16(p_sum)
    imp = _dot(cover_t, p_hi) + _dot(cover_t, p_lo)
    blk = _row_iota((n_sel, t))
    cur = lax.shift_right_logical(t0 + _lane_iota((n_sel, t)), NSA_SEL_LEN.bit_length() - 1)
    forced = (blk == 0) | (blk == cur) | (blk == cur - 1)
    imp = jnp.where(forced, NSA_FORCE, imp)
    imp = jnp.where(blk <= cur, imp, NEG_INF)
    sel_ref[...] = (_rank_select_t(imp, None, n_top) - 1.0) * (-MASK_BIAS)

    qs = [_pad_q(qr_ref[j]) for j in range(HG)]

    per = t // NSA_SEL_LEN
    kio2 = _row_iota((2 * t, t))
    qio2 = _lane_iota((2 * t, t))
    flash = _Flash(m_ref, l_ref, acc_ref, t)
    flash.reset()

    def sel_pair(cc, carry):
        c0 = 2 * cc
        k0 = pl.multiple_of(c0 * t, 2 * t)
        rows = [jnp.broadcast_to(sel_ref[pl.ds(c0 * per + u, 1), :], (NSA_SEL_LEN, t))
                for u in range(2 * per)]
        bias = jnp.concatenate(rows, axis=0)
        bias = jnp.where((c0 - i) * t + kio2 <= qio2, bias, MASK_BIAS)
        kv = kvs_ref[pl.ds(k0, 2 * t), :]
        kvts = [kvst_ref[c0], kvst_ref[c0 + 1]]
        flash.update(qs, [kv] * HG, [kvts] * HG, [bias] * HG)
        return carry

    lax.fori_loop(0, (i + 2) // 2, sel_pair, 0)
    o_s = [flash.result(j) for j in range(HG)]

    n_wc = NSA_WINDOW // t + 1
    cw = jnp.maximum(i - (n_wc - 1), 0)
    kw0 = pl.multiple_of(cw * t, t)
    dist = (i - cw) * t + _lane_iota((n_wc * t, t)) - _row_iota((n_wc * t, t))
    bias_w = jnp.where((dist >= 0) & (dist < NSA_WINDOW), 0.0, MASK_BIAS)
    kv = kvw_ref[pl.ds(kw0, n_wc * t), :]
    kvts = [kvwt_ref[cw + u] for u in range(n_wc)]
    o_w = []
    for j in range(HG):
        _, inv_l, o = _softmax_direct(qs[j], kv, kvts, bias_w)
        o_w.append(o * inv_l)

    gt = gtt_ref[...]
    heads = [gt[3 * j:3 * j + 1, :] * o_c[j] + gt[3 * j + 1:3 * j + 2, :] * o_s[j]
             + gt[3 * j + 2:3 * j + 3, :] * o_w[j] for j in range(HG)]
    _store_heads(o_ref, heads)


def _nsa_attention(qc, qr, kvc, kvct, kvs, kvst, kvw, kvwt, gates_t):
    B, H, S, _ = qc.shape
    G = NSA_GROUPS
    HG = H // G
    t = ATT_T
    nt = S // t
    n16 = kvc.shape[2]
    n_cmp = (S - NSA_CMP_LEN) // NSA_CMP_STRIDE + 1
    n_sel = S // NSA_SEL_LEN
    n_top = min(NSA_SEL_TOPK, n_sel)
    qspec = pl.BlockSpec((None, HG, t, 64), lambda b, g, i: (b, g, i, 0))
    kvspec = pl.BlockSpec((None, None, S, 128), lambda b, g, i: (b, g, 0, 0))
    kvtspec = pl.BlockSpec((None, None, nt, 128, t), lambda b, g, i: (b, g, 0, 0, 0))
    in_specs = [qspec, qspec,
                pl.BlockSpec((None, None, n16, 128), lambda b, g, i: (b, g, 0, 0)),
                pl.BlockSpec((None, None, 128, n16), lambda b, g, i: (b, g, 0, 0)),
                kvspec, kvtspec, kvspec, kvtspec,
                pl.BlockSpec((None, None, 12, t), lambda b, g, i: (b, g, 0, i))]
    return pl.pallas_call(
        functools.partial(_nsa_kernel, n_cmp=n_cmp, n_top=n_top), grid=(B, G, nt), in_specs=in_specs,
        out_specs=pl.BlockSpec((None, t, HG * 64), lambda b, g, i: (b, i, g)),
        out_shape=jax.ShapeDtypeStruct((B, S, H * 64), BF16),
        scratch_shapes=[pltpu.VMEM((n_sel, t), F32)] + _flash_scratch(HG, t),
        compiler_params=_params(("parallel", "parallel", "parallel")), name="nsa_attention",
    )(qc, qr, kvc, kvct, kvs, kvst, kvw, kvwt, gates_t)


def _rope_freq_row(period, rot):
    half = rot // 2
    inv_freq = ROPE_THETA ** (-(jnp.arange(half, dtype=F32) * 2.0 / rot))
    lane = jnp.arange(LANES) % period
    f = jnp.where(lane < rot, inv_freq[lane % half], 0.0)
    return f.reshape(1, LANES).astype(F32)


def _norm_matrices():
    r = jnp.arange(LANES)
    same = (r[:, None] // 64) == (r[None, :] // 64)
    nq = jnp.where(same, 1.0 / 64, 0.0).astype(BF16)
    nk = jnp.where(same & (r[:, None] < 64), 1.0 / 64, 0.0).astype(BF16)
    return nq, nk


def _q_gain(g):
    return (jnp.tile(g.astype(F32), 2) * Q_SCALE).reshape(1, LANES)


def _k_gain(g):
    return jnp.concatenate([g.astype(F32), jnp.ones((64,), F32)]).reshape(1, LANES)


def _interleave_kv(wk, wv, n_heads):
    d = wk.shape[0]
    wk = wk.reshape(d, n_heads, 64)
    wv = wv.reshape(d, n_heads, 64)
    return jnp.concatenate([wk, wv], axis=2).reshape(d, n_heads * 128)


def _split_cols(w, sizes):
    out, start = [], 0
    for n in sizes:
        out.append(w[:, start:start + n])
        start += n
    return out


def _mixer_layer0(x2, pos2, B, S, gmix, w_in, w_out, a_q_norm, a_k_norm, b_q_norm, b_k_norm):
    sizes = (512, 64, 64, 256, 32, 8, 512, 512, 512)
    waq, wak, wav, wiq, wik, wiw, wbq, wbk, wbv = _split_cols(w_in, sizes)
    pad = jnp.zeros((D_MODEL, LANES - 40), w_in.dtype)
    w = jnp.concatenate([waq, wbq, _interleave_kv(wbk, wbv, 8), wak, wav, wiq, wik, wiw, pad],
                        axis=1).astype(BF16)
    nq, nk = _norm_matrices()
    tabs = (_rope_freq_row(64, 16), _rope_freq_row(32, 8), nq, nk,
            _q_gain(a_q_norm), _q_gain(b_q_norm), _k_gain(a_k_norm), _k_gain(b_k_norm))
    aq, bq, bkv, bkvt, akv, akvt, iq, ik, iwt, km = _ab_prep(x2, pos2, gmix, w, tabs, B, S)
    n_blk = S // MOBA_BLOCK
    kmean = km.reshape(B, n_blk, 8, 128).transpose(0, 2, 1, 3)
    o_a = _dsa_attention(iq, iwt, ik, aq, akv, akvt).reshape(B * S, 512)
    o_b = _moba_attention(bq, bkv, bkvt, kmean).reshape(B * S, 512)
    w_out = w_out.astype(BF16)
    return _out_proj([o_a, o_b], [w_out[:512], w_out[512:]], x2)


def _mixer_layer1(x2, pos2, B, S, gmix, w_in, w_out, q_norm, kcmp_norm, ksel_norm, kwin_norm,
                  pos_k, pos_v, w1_k, w2_k, w1_v, w2_v):
    G = NSA_GROUPS
    sizes = (1024,) + (256,) * 6 + (48,)
    wq, wkc, wvc, wks, wvs, wkw, wvw, wgt = _split_cols(w_in, sizes)
    pad = jnp.zeros((D_MODEL, LANES - 48), w_in.dtype)
    w = jnp.concatenate([wq, _interleave_kv(wks, wvs, G), _interleave_kv(wkw, wvw, G),
                         wkc, wvc, wgt, pad], axis=1).astype(BF16)
    nq, nk = _norm_matrices()
    tabs = (_rope_freq_row(64, 16), nq, nk, _q_gain(q_norm), _k_gain(ksel_norm), _k_gain(kwin_norm))
    qc, qr, kvs, kvst, kvw, kvwt, kc_raw, vc_raw, gates_t = _nsa_prep(x2, pos2, gmix, w, tabs, B, S)

    n16 = S // NSA_CMP_STRIDE

    def blocks16(t):
        return (t.reshape(B, n16, NSA_CMP_STRIDE, G, HEAD_DIM).transpose(0, 3, 1, 2, 4)
                .reshape(B, G, n16, NSA_CMP_STRIDE * HEAD_DIM))

    def pos_rows(p):
        return p.astype(F32).reshape(2, NSA_CMP_STRIDE * HEAD_DIM)

    kvc, kvct = _compress(blocks16(kc_raw), blocks16(vc_raw), pos_rows(pos_k), pos_rows(pos_v),
                          w1_k.astype(BF16), w1_v.astype(BF16), w2_k.astype(BF16), w2_v.astype(BF16),
                          kcmp_norm.astype(F32).reshape(1, HEAD_DIM))
    o = _nsa_attention(qc, qr, kvc, kvct, kvs, kvst, kvw, kvwt, gates_t)
    return _out_proj([o.reshape(B * S, NSA_HEADS * HEAD_DIM)], [w_out.astype(BF16)], x2)


def _mem_and_ffn(x2, mem, S, g_mem, g_src, w_q, w_kv, w_o, q_norm, k_norm, g_ffn, ffn_w_in, ffn_w_out):
    row = lambda v: v.astype(F32).reshape(1, -1)
    kv = _mem_kv(mem, row(g_src), w_kv.astype(BF16), row(k_norm))
    x2 = _mem_attn(x2, row(g_mem), w_q.astype(BF16), row(q_norm), kv, w_o.astype(BF16), S)
    wg = ffn_w_in[:, :D_FF].astype(BF16)
    wu = ffn_w_in[:, D_FF:].astype(BF16)
    return _ffn(x2, row(g_ffn), wg, wu, ffn_w_out.astype(BF16))


def kernel(x, mem, positions, norm_mix, norm_mem, norm_mem_src, norm_ffn, ab_w_in, ab_w_out, dsa_q_norm, dsa_k_norm, moba_q_norm, moba_k_norm, nsa_w_in, nsa_w_out, nsa_q_norm, nsa_kcmp_norm, nsa_ksel_norm, nsa_kwin_norm, nsa_cmp_pos_k, nsa_cmp_pos_v, nsa_cmp_w1_k, nsa_cmp_w2_k, nsa_cmp_w1_v, nsa_cmp_w2_v, mem_w_q, mem_w_kv, mem_w_o, mem_q_norm, mem_k_norm, ffn_w_in, ffn_w_out):
    B, S, D = x.shape
    depth = norm_mix.shape[0]
    x2 = x.reshape(B * S, D)
    pos2 = positions.astype(F32).reshape(B * S, 1)
    row = lambda v: v.astype(F32).reshape(1, -1)
    for i in range(depth):
        j = i // 2
        if i % 2 == 0:
            x2 = _mixer_layer0(x2, pos2, B, S, row(norm_mix[i]), ab_w_in[j], ab_w_out[j],
                               dsa_q_norm[j], dsa_k_norm[j], moba_q_norm[j], moba_k_norm[j])
        else:
            x2 = _mixer_layer1(x2, pos2, B, S, row(norm_mix[i]), nsa_w_in[j], nsa_w_out[j],
                               nsa_q_norm[j], nsa_kcmp_norm[j], nsa_ksel_norm[j], nsa_kwin_norm[j],
                               nsa_cmp_pos_k[j], nsa_cmp_pos_v[j], nsa_cmp_w1_k[j], nsa_cmp_w2_k[j],
                               nsa_cmp_w1_v[j], nsa_cmp_w2_v[j])
        x2 = _mem_and_ffn(x2, mem, S, norm_mem[i], norm_mem_src[i], mem_w_q[i], mem_w_kv[i], mem_w_o[i],
                          mem_q_norm[i], mem_k_norm[i], norm_ffn[i], ffn_w_in[i], ffn_w_out[i])
    return x2.reshape(B, S, D)
```

```python
import functools
import math

import jax
import jax.numpy as jnp
from jax import lax
from jax.experimental import pallas as pl
from jax.experimental.pallas import tpu as pltpu

F32 = jnp.float32
BF16 = jnp.bfloat16
I32 = jnp.int32

D_MODEL = 1024
N_MEM = 256
HEAD_DIM = 64
ROPE_THETA = 500000.0
RMS_EPS = 1e-6
NEG_INF = -1e30
TINY = 1e-20

DSA_HEADS = 8
DSA_IDX_HEADS = 8
DSA_IDX_DIM = 32
DSA_TOPK = 256
MOBA_HEADS = 8
MOBA_BLOCK = 256
MOBA_TOPK = 3
NSA_HEADS = 16
NSA_GROUPS = 4
NSA_CMP_LEN = 32
NSA_CMP_STRIDE = 16
NSA_SEL_LEN = 64
NSA_SEL_TOPK = 16
NSA_WINDOW = 512
NSA_FORCE = 1e4
MEM_HEADS = 4
MEM_HEAD_DIM = 128
D_FF = ((8 * D_MODEL + 3 * 256 - 1) // (3 * 256)) * 256

LANES = 128
SUBLANES = 8
INT_MIN = -(2 ** 31)
VMEM_LIMIT = 56 * 1024 * 1024

ATT_T = 256
MASK_BIAS = -1e30
M_FLOOR = -1e29
LOG2E = math.log2(math.e)
Q_SCALE = HEAD_DIM ** -0.5 * LOG2E

NT_DIMS = (((1,), (1,)), ((), ()))


def _dot(a, b):
    return jnp.dot(a, b, preferred_element_type=F32)


def _dot_nt(a, b):
    return lax.dot_general(a, b, NT_DIMS, preferred_element_type=F32)


def _split_bf16(a):
    hi = a.astype(BF16)
    return hi, (a - hi.astype(F32)).astype(BF16)


def _split_dot(a, b):
    hi, lo = _split_bf16(a)
    return _dot(hi, b) + _dot(lo, b)


def _rms_rows(x, gain):
    ms = jnp.mean(x * x, axis=-1, keepdims=True)
    return x * lax.rsqrt(ms + RMS_EPS) * gain


def _params(sem):
    return pltpu.CompilerParams(dimension_semantics=sem, vmem_limit_bytes=VMEM_LIMIT)


def _head_norm(y, norm_m, gain):
    ms = _split_dot(y * y, norm_m)
    return y * lax.rsqrt(ms + RMS_EPS) * gain


def _rope(y, c, s, lo_mask, half):
    sw = jnp.where(lo_mask, pltpu.roll(y, LANES - half, 1), pltpu.roll(y, half, 1))
    return y * c + sw * s


def _lane_iota(shape):
    return lax.broadcasted_iota(I32, shape, 1)


def _row_iota(shape):
    return lax.broadcasted_iota(I32, shape, 0)


def _rope_tables(pos, ftab, period, half):
    ang = pos * ftab
    lane = _lane_iota(ang.shape) % period
    c = jnp.cos(ang)
    s = jnp.sin(ang) * jnp.where(lane < half, -1.0, 1.0)
    return c, s


def _kv_column(yc, nk, gain, c64k, s64k, lo64, first64):
    kn = jnp.where(first64, _head_norm(yc, nk, gain), yc)
    return _rope(kn, c64k, s64k, lo64, 8)


def _rope_trig_kernel(pos_ref, f64_ref, f32_ref, o_ref):
    pos = pos_ref[...]
    c64, s64 = _rope_tables(pos, f64_ref[...], 64, 8)
    c32, s32 = _rope_tables(pos, f32_ref[...], 32, 4)
    o_ref[...] = jnp.concatenate([c64, s64, c32, s32], axis=1)


def _rope_trig(pos2, f64, f32t, tm=1024):
    T = pos2.shape[0]
    return pl.pallas_call(
        _rope_trig_kernel, grid=(T // tm,),
        in_specs=[pl.BlockSpec((tm, 1), lambda i: (i, 0)),
                  pl.BlockSpec(f64.shape, lambda i: (0, 0)), pl.BlockSpec(f32t.shape, lambda i: (0, 0))],
        out_specs=pl.BlockSpec((tm, 4 * LANES), lambda i: (i, 0)),
        out_shape=jax.ShapeDtypeStruct((T, 4 * LANES), F32),
        compiler_params=_params(("parallel",)), name="rope_trig",
    )(pos2, f64, f32t)


def _ab_prep_kernel(x_ref, trig_ref, gmix_ref, w_ref, nq_ref, nk_ref,
                    gaq_ref, gbq_ref, gak_ref, gbk_ref,
                    aq_ref, bq_ref, bkv_ref, bkvt_ref, akv_ref, akvt_ref, iq_ref, ik_ref, iwt_ref, km_ref):
    xn = _rms_rows(x_ref[...], gmix_ref[...]).astype(BF16)
    y = _dot(xn, w_ref[...])
    c64, s64, c32, s32 = [trig_ref[:, j * LANES:(j + 1) * LANES] for j in range(4)]
    lane = _lane_iota(c64.shape)
    lo64 = (lane % 64) < 8
    lo32 = (lane % 32) < 4
    first64 = lane < 64
    c64k = jnp.where(first64, c64, 1.0)
    s64k = jnp.where(first64, s64, 0.0)
    first32 = lane < 32
    c32k = jnp.where(first32, c32, 1.0)
    s32k = jnp.where(first32, s32, 0.0)
    nq = nq_ref[...]
    nk = nk_ref[...]

    def col(j):
        return y[:, j * LANES:(j + 1) * LANES]

    for j in range(4):
        q = _rope(_head_norm(col(j), nq, gaq_ref[...]), c64, s64, lo64, 8)
        aq_ref[2 * j] = q[:, :64].astype(BF16)
        aq_ref[2 * j + 1] = q[:, 64:].astype(BF16)
    for j in range(4):
        q = _rope(_head_norm(col(4 + j), nq, gbq_ref[...]), c64, s64, lo64, 8)
        bq_ref[2 * j] = q[:, :64].astype(BF16)
        bq_ref[2 * j + 1] = q[:, 64:].astype(BF16)
    for h in range(8):
        kv = _kv_column(col(8 + h), nk, gbk_ref[...], c64k, s64k, lo64, first64)
        bkv_ref[h] = kv.astype(BF16)
        bkvt_ref[h] = kv.T.astype(BF16)
        km_ref[h:h + 1, :] = jnp.mean(kv, axis=0, keepdims=True)
    kv = _kv_column(col(16), nk, gak_ref[...], c64k, s64k, lo64, first64)
    akv_ref[...] = kv.astype(BF16)
    akvt_ref[...] = kv.T.astype(BF16)
    for j in range(2):
        q = _rope(col(17 + j), c32, s32, lo32, 4)
        for u in range(4):
            iq_ref[4 * j + u] = q[:, 32 * u:32 * (u + 1)].astype(BF16)
    yc = col(19)
    ik_ref[...] = _rope(yc, c32k, s32k, lo32, 4)[:, :32].astype(BF16)
    iwt_ref[...] = yc.T[32:40, :]


def _ab_prep(x2, trig, gmix, w, tabs, B, S):
    T = x2.shape[0]
    tm = ATT_T
    nt = S // tm
    n_cols = w.shape[1]
    nq, nk, gaq, gbq, gak, gbk = tabs

    def full(a):
        return pl.BlockSpec(a.shape, lambda i: (0,) * a.ndim)

    def hm(width, heads=8):
        return pl.BlockSpec((None, heads, tm, width), lambda i: (i // nt, 0, i % nt, 0))

    def tokm(width):
        return pl.BlockSpec((None, tm, width), lambda i: (i // nt, i % nt, 0))

    out_shape = (
        jax.ShapeDtypeStruct((B, 8, S, 64), BF16),
        jax.ShapeDtypeStruct((B, 8, S, 64), BF16),
        jax.ShapeDtypeStruct((B, 8, S, 128), BF16),
        jax.ShapeDtypeStruct((B, 8, nt, 128, tm), BF16),
        jax.ShapeDtypeStruct((B, S, 128), BF16),
        jax.ShapeDtypeStruct((B, nt, 128, tm), BF16),
        jax.ShapeDtypeStruct((B, 8, S, 32), BF16),
        jax.ShapeDtypeStruct((B, S, 32), BF16),
        jax.ShapeDtypeStruct((B, 8, S), F32),
        jax.ShapeDtypeStruct((T // tm, 8, 128), F32),
    )
    out_specs = (hm(64), hm(64), hm(128),
                 pl.BlockSpec((None, 8, None, 128, tm), lambda i: (i // nt, 0, i % nt, 0, 0)),
                 tokm(128),
                 pl.BlockSpec((None, None, 128, tm), lambda i: (i // nt, i % nt, 0, 0)),
                 hm(32), tokm(32),
                 pl.BlockSpec((None, 8, tm), lambda i: (i // nt, 0, i % nt)),
                 pl.BlockSpec((None, 8, 128), lambda i: (i, 0, 0)))
    in_specs = [pl.BlockSpec((tm, D_MODEL), lambda i: (i, 0)),
                pl.BlockSpec((tm, 4 * LANES), lambda i: (i, 0)),
                full(gmix), pl.BlockSpec((D_MODEL, n_cols), lambda i: (0, 0)),
                full(nq), full(nk), full(gaq), full(gbq), full(gak), full(gbk)]
    return pl.pallas_call(
        _ab_prep_kernel, grid=(T // tm,), in_specs=in_specs, out_specs=out_specs,
        out_shape=out_shape, compiler_params=_params(("parallel",)), name="ab_prep",
    )(x2, trig, gmix, w, nq, nk, gaq, gbq, gak, gbk)


def _pad_q(q):
    return jnp.concatenate([q, jnp.zeros_like(q)], axis=1)


class _Flash:
    def __init__(self, m_ref, l_ref, acc_ref, tq):
        self.m_ref, self.l_ref, self.acc_ref, self.tq = m_ref, l_ref, acc_ref, tq

    def reset(self):
        self.m_ref[...] = jnp.full(self.m_ref.shape, M_FLOOR, F32)
        self.l_ref[...] = jnp.zeros(self.l_ref.shape, F32)
        self.acc_ref[...] = jnp.zeros(self.acc_ref.shape, F32)

    def run(self, qs, n_chunks, keys_of, chunk_of, s_ref):
        n = len(qs)

        def scores_of(c):
            kvs = keys_of(c)
            return [_dot_nt(kvs[i], qs[i]) for i in range(n)]

        for i, s in enumerate(scores_of(0)):
            s_ref[0, i] = s

        def body(c, carry):
            cur = c % 2
            nxt_scores = scores_of(jnp.minimum(c + 1, n_chunks - 1))
            kvts, biases = chunk_of(c)
            self.update(qs, None, kvts, biases, scores=[s_ref[cur, i] for i in range(n)])
            for i in range(n):
                s_ref[1 - cur, i] = nxt_scores[i]
            return carry

        lax.fori_loop(0, n_chunks, body, 0)

    def update(self, qs, kvs, kvts, biases, scores=None):
        n = len(qs)
        tq = self.tq
        if scores is None:
            scores = [_dot_nt(kvs[i], qs[i]) for i in range(n)]
        probs, alphas = [], []
        for i in range(n):
            cols = slice(i * tq, (i + 1) * tq)
            s = scores[i] if biases[i] is None else scores[i] + biases[i]
            m = self.m_ref[:, cols]
            m_new = jnp.maximum(m, jnp.max(s, axis=0, keepdims=True))
            p = jnp.exp2(s - m_new)
            alpha = jnp.exp2(m - m_new)
            self.m_ref[:, cols] = m_new
            self.l_ref[:, cols] = alpha * self.l_ref[:, cols] + p.reshape(-1, SUBLANES, tq).sum(axis=0)
            probs.append(p.astype(BF16))
            alphas.append(alpha)
        for i in range(n):
            cols = slice(i * tq, (i + 1) * tq)
            pv, r0 = None, 0
            for kvt in kvts[i]:
                part = _dot(kvt, probs[i][r0:r0 + kvt.shape[1]])
                pv = part if pv is None else pv + part
                r0 += kvt.shape[1]
            self.acc_ref[:, cols] = alphas[i] * self.acc_ref[:, cols] + pv

    def result(self, slot):
        cols = slice(slot * self.tq, (slot + 1) * self.tq)
        l = jnp.sum(self.l_ref[:, cols], axis=0, keepdims=True)
        return self.acc_ref[:, cols] / jnp.maximum(l, TINY)


def _flash_scratch(n_slots, tq):
    return [pltpu.VMEM((1, n_slots * tq), F32), pltpu.VMEM((SUBLANES, n_slots * tq), F32),
            pltpu.VMEM((LANES, n_slots * tq), F32)]


def _softmax_direct(qs, kv, kvts, bias):
    scores = [_dot_nt(kv, q) for q in qs]
    probs, inv_ls = [], []
    for s in scores:
        s = s + bias
        m = jnp.maximum(jnp.max(s, axis=0, keepdims=True), M_FLOOR)
        p = jnp.exp2(s - m)
        inv_ls.append(1.0 / jnp.maximum(jnp.sum(p, axis=0, keepdims=True), TINY))
        probs.append(p)
    outs = []
    for p in probs:
        pb = p.astype(BF16)
        o, r0 = None, 0
        for kvt in kvts:
            part = _dot(kvt, pb[r0:r0 + kvt.shape[1]])
            o = part if o is None else o + part
            r0 += kvt.shape[1]
        outs.append(o)
    return probs, inv_ls, outs


def _causal_bias(t):
    return jnp.where(_row_iota((t, t)) <= _lane_iota((t, t)), 0.0, MASK_BIAS)


def _store_heads(o_ref, heads_t):
    tq = heads_t[0].shape[1]
    lane = _lane_iota((tq, LANES))
    for u in range(len(heads_t) // 2):
        even = pltpu.roll(heads_t[2 * u].T, 64, 1)
        odd = heads_t[2 * u + 1].T
        o_ref[:, u * LANES:(u + 1) * LANES] = jnp.where(lane < 64, even, odd).astype(o_ref.dtype)


def _rank_select_t(v, n_valid, n_top):
    n = v.shape[0]
    row = _row_iota(v.shape)
    rank = jnp.zeros(v.shape, F32)
    for m in range(n):
        vm = v[m:m + 1, :]
        ahead = (vm > v) | ((vm == v) & (m < row))
        if n_valid is not None:
            ahead = ahead & (m < n_valid)
        rank = rank + jnp.where(ahead, 1.0, 0.0)
    sel = rank < n_top
    if n_valid is not None:
        sel = sel & (row < n_valid)
    return jnp.where(sel, 1.0, 0.0)


def _dsa_kernel(iq_ref, iwt_ref, ik_ref, aq_ref, akv_ref, akvt_ref, o_ref,
                sk_ref, bias_ref, xcut_ref, m_ref, l_ref, acc_ref, *, k_top, index_bits):
    t = ATT_T
    i = pl.program_id(1)
    n_ch = i + 1
    kio = _row_iota((t, t))
    qio = _lane_iota((t, t))

    def causal(c):
        return (c - i) * t + kio <= qio

    def score_chunk(c, carry):
        k0 = pl.multiple_of(c * t, t)
        ikc = ik_ref[pl.ds(k0, t), :]
        sc = jnp.zeros((t, t), F32)
        for h in range(DSA_IDX_HEADS):
            logit = _dot_nt(ikc, iq_ref[h])
            sc = sc + iwt_ref[h:h + 1, :] * jnp.maximum(logit, 0.0)
        sc = jnp.where(sc == 0.0, 0.0, sc)
        bits = pltpu.bitcast(sc, I32)
        key = bits ^ ((bits >> 31) & 0x7FFFFFFF)
        sk_ref[c] = jnp.where(causal(c), key, INT_MIN)
        return carry

    lax.fori_loop(0, n_ch, score_chunk, 0)

    def count(pred):
        def body(c, acc8):
            ind = jnp.where(pred(sk_ref[c], c), 1.0, 0.0)
            return acc8 + ind.reshape(-1, SUBLANES, t).sum(axis=0)
        acc8 = lax.fori_loop(0, n_ch, body, jnp.zeros((SUBLANES, t), F32))
        return jnp.sum(acc8, axis=0, keepdims=True)

    def bit_step(b, thr):
        cand = thr + lax.shift_left(jnp.int32(1), 31 - b)
        return jnp.where(count(lambda blk, c: blk >= cand) >= k_top, cand, thr)

    thr = lax.fori_loop(0, 32, bit_step, jnp.full((1, t), INT_MIN, I32))

    need = k_top - count(lambda blk, c: blk > thr)
    n_ge = count(lambda blk, c: blk >= thr)
    xcut_ref[...] = jnp.full((1, t), 2 ** 30, I32)

    @pl.when(jnp.max(n_ge) > k_top)
    def _():
        def x_step(b, x):
            cand = x + lax.shift_left(jnp.int32(1), index_bits - 1 - b)
            ties_below = count(lambda blk, c: (blk == thr) & (c * t + kio < cand))
            return jnp.where(ties_below <= need, cand, x)
        xcut_ref[...] = lax.fori_loop(0, index_bits, x_step, jnp.zeros((1, t), I32))

    xcut = xcut_ref[...]

    n_pairs = (n_ch + 1) // 2

    def bias_chunk(c, carry):
        blk = sk_ref[jnp.minimum(c, i)]
        keep = (blk > thr) | ((blk == thr) & (c * t + kio < xcut))
        bias_ref[c] = jnp.where(keep & causal(c), 0.0, MASK_BIAS)
        return carry

    lax.fori_loop(0, 2 * n_pairs, bias_chunk, 0)

    flash = _Flash(m_ref, l_ref, acc_ref, t)
    flash.reset()
    qs = [_pad_q(aq_ref[h]) for h in range(DSA_HEADS)]

    n = DSA_HEADS

    def att_pair(cc, carry):
        c0 = 2 * cc
        k0 = pl.multiple_of(c0 * t, 2 * t)
        kv = akv_ref[pl.ds(k0, 2 * t), :]
        bias = jnp.concatenate([bias_ref[c0], bias_ref[c0 + 1]], axis=0)
        flash.update(qs, [kv] * n, [[akvt_ref[c0], akvt_ref[c0 + 1]]] * n, [bias] * n)
        return carry

    lax.fori_loop(0, n_pairs, att_pair, 0)
    _store_heads(o_ref, [flash.result(h) for h in range(DSA_HEADS)])


def _dsa_attention(iq, iwt, ik, aq, akv, akvt):
    B, _, S, _ = aq.shape
    t = ATT_T
    nt = S // t
    k_top = min(DSA_TOPK, S // 4)
    in_specs = [
        pl.BlockSpec((None, 8, t, 32), lambda b, i: (b, 0, i, 0)),
        pl.BlockSpec((None, 8, t), lambda b, i: (b, 0, i)),
        pl.BlockSpec((None, S, 32), lambda b, i: (b, 0, 0)),
        pl.BlockSpec((None, 8, t, 64), lambda b, i: (b, 0, i, 0)),
        pl.BlockSpec((None, S, 128), lambda b, i: (b, 0, 0)),
        pl.BlockSpec((None, nt, 128, t), lambda b, i: (b, 0, 0, 0)),
    ]
    return pl.pallas_call(
        functools.partial(_dsa_kernel, k_top=k_top, index_bits=S.bit_length()),
        grid=(B, nt), in_specs=in_specs,
        out_specs=pl.BlockSpec((None, t, 512), lambda b, i: (b, i, 0)),
        out_shape=jax.ShapeDtypeStruct((B, S, 512), BF16),
        scratch_shapes=[pltpu.VMEM((nt, t, t), I32), pltpu.VMEM((nt, t, t), F32),
                        pltpu.VMEM((1, t), I32)] + _flash_scratch(DSA_HEADS, t),
        compiler_params=_params(("parallel", "parallel")), name="dsa_attention",
    )(iq, iwt, ik, aq, akv, akvt)


MOBA_HPS = 4


def _moba_kernel(q_ref, kv_ref, kvt_ref, km_ref, o_ref, sel_ref, m_ref, l_ref, acc_ref, *, n_top):
    t = ATT_T
    own = pl.program_id(2)
    causal = _causal_bias(t)
    flash = _Flash(m_ref, l_ref, acc_ref, t)
    flash.reset()
    qs = []
    for hh in range(MOBA_HPS):
        q = _pad_q(q_ref[hh])
        km_hi, km_lo = _split_bf16(km_ref[hh])
        gate = _dot_nt(km_hi, q) + _dot_nt(km_lo, q)
        sel_ref[hh] = (_rank_select_t(gate, own, n_top) - 1.0) * (-MASK_BIAS)
        qs.append(q)

    def block_bias(hh, n):
        row = jnp.broadcast_to(sel_ref[hh, pl.ds(n, 1), :], (t, t))
        return jnp.where(n == own, causal, row)

    def pair(cc, carry):
        n0 = 2 * cc
        k0 = pl.multiple_of(n0 * t, 2 * t)
        heads = range(MOBA_HPS)
        flash.update(qs, [kv_ref[hh, pl.ds(k0, 2 * t), :] for hh in heads],
                     [[kvt_ref[hh, n0], kvt_ref[hh, n0 + 1]] for hh in heads],
                     [jnp.concatenate([block_bias(hh, n0), block_bias(hh, n0 + 1)], axis=0) for hh in heads])
        return carry

    lax.fori_loop(0, (own + 2) // 2, pair, 0)
    _store_heads(o_ref, [flash.result(hh) for hh in range(MOBA_HPS)])


def _moba_attention(bq, bkv, bkvt, kmean):
    B, H, S, _ = bq.shape
    t = ATT_T
    hps = MOBA_HPS
    n_blk = S // MOBA_BLOCK
    assert t == MOBA_BLOCK and n_blk % 2 == 0 and H % hps == 0
    n_top = max(1, min(MOBA_TOPK, n_blk - 1))
    in_specs = [
        pl.BlockSpec((None, hps, t, 64), lambda b, h, i: (b, h, i, 0)),
        pl.BlockSpec((None, hps, S, 128), lambda b, h, i: (b, h, 0, 0)),
        pl.BlockSpec((None, hps, n_blk, 128, t), lambda b, h, i: (b, h, 0, 0, 0)),
        pl.BlockSpec((None, hps, n_blk, 128), lambda b, h, i: (b, h, 0, 0)),
    ]
    return pl.pallas_call(
        functools.partial(_moba_kernel, n_top=n_top), grid=(B, H // hps, S // t), in_specs=in_specs,
        out_specs=pl.BlockSpec((None, t, hps * 64), lambda b, h, i: (b, i, h)),
        out_shape=jax.ShapeDtypeStruct((B, S, H * 64), BF16),
        scratch_shapes=[pltpu.VMEM((hps, n_blk, t), F32)] + _flash_scratch(hps, t),
        compiler_params=_params(("parallel", "parallel", "parallel")), name="moba_attention",
    )(bq, bkv, bkvt, kmean)


def _out_proj_kernel(*refs, n_in):
    a_refs = refs[:n_in]
    w_refs = refs[n_in:2 * n_in]
    x_ref, o_ref = refs[2 * n_in], refs[2 * n_in + 1]
    acc = x_ref[...]
    for a_ref, w_ref in zip(a_refs, w_refs):
        acc = acc + _dot(a_ref[...], w_ref[...])
    o_ref[...] = acc


def _out_proj(parts, weights, x2, tm=512):
    T = x2.shape[0]
    n_in = len(parts)
    in_specs = ([pl.BlockSpec((tm, p.shape[1]), lambda i: (i, 0)) for p in parts]
                + [pl.BlockSpec(w.shape, lambda i: (0, 0)) for w in weights]
                + [pl.BlockSpec((tm, D_MODEL), lambda i: (i, 0))])
    return pl.pallas_call(
        functools.partial(_out_proj_kernel, n_in=n_in), grid=(T // tm,), in_specs=in_specs,
        out_specs=pl.BlockSpec((tm, D_MODEL), lambda i: (i, 0)),
        out_shape=jax.ShapeDtypeStruct((T, D_MODEL), F32),
        compiler_params=_params(("parallel",)), name="out_proj",
    )(*parts, *weights, x2)


def _lane_group_norm(y, gain, width):
    outs = []
    for j in range(y.shape[1] // width):
        yc = y[:, j * width:(j + 1) * width]
        outs.append(_rms_rows(yc, gain))
    return jnp.concatenate(outs, axis=1)


def _mem_kv_kernel(m_ref, g_ref, w_ref, gk_ref, o_ref):
    mn = _rms_rows(m_ref[...], g_ref[...]).astype(BF16)
    y = _dot(mn, w_ref[...])
    hw = MEM_HEADS * MEM_HEAD_DIM
    k = _lane_group_norm(y[:, :hw], gk_ref[...], MEM_HEAD_DIM)
    o_ref[...] = jnp.concatenate([k, y[:, hw:]], axis=1).astype(BF16)


def _mem_kv(mem, g, w, gk):
    B, M, _ = mem.shape
    n = w.shape[1]
    return pl.pallas_call(
        _mem_kv_kernel, grid=(B,),
        in_specs=[pl.BlockSpec((None, M, D_MODEL), lambda b: (b, 0, 0)),
                  pl.BlockSpec(g.shape, lambda b: (0, 0)),
                  pl.BlockSpec(w.shape, lambda b: (0, 0)),
                  pl.BlockSpec(gk.shape, lambda b: (0, 0))],
        out_specs=pl.BlockSpec((None, M, n), lambda b: (b, 0, 0)),
        out_shape=jax.ShapeDtypeStruct((B, M, n), BF16),
        compiler_params=_params(("parallel",)), name="mem_kv",
    )(mem, g, w, gk)


def _mem_attn_kernel(x_ref, g_ref, wq_ref, gq_ref, kv_ref, wo_ref, o_ref):
    x = x_ref[...]
    xn = _rms_rows(x, g_ref[...]).astype(BF16)
    q = _lane_group_norm(_dot(xn, wq_ref[...]), gq_ref[...], MEM_HEAD_DIM).astype(BF16)
    hw = MEM_HEADS * MEM_HEAD_DIM
    scale = MEM_HEAD_DIM ** -0.5
    outs = []
    for h in range(MEM_HEADS):
        cols = slice(h * MEM_HEAD_DIM, (h + 1) * MEM_HEAD_DIM)
        k = kv_ref[:, cols]
        v = kv_ref[:, hw + h * MEM_HEAD_DIM:hw + (h + 1) * MEM_HEAD_DIM]
        s = _dot_nt(q[:, cols], k) * scale
        p = jnp.exp(s - jnp.max(s, axis=-1, keepdims=True))
        p = p / jnp.sum(p, axis=-1, keepdims=True)
        outs.append(_dot(p.astype(BF16), v))
    o = jnp.concatenate(outs, axis=1).astype(BF16)
    o_ref[...] = x + _dot(o, wo_ref[...])


def _mem_attn(x2, g, wq, gq, kv, wo, S, tm=512):
    T = x2.shape[0]
    nt = S // tm
    M, n = kv.shape[1], kv.shape[2]
    return pl.pallas_call(
        _mem_attn_kernel, grid=(T // tm,),
        in_specs=[pl.BlockSpec((tm, D_MODEL), lambda i: (i, 0)),
                  pl.BlockSpec(g.shape, lambda i: (0, 0)),
                  pl.BlockSpec(wq.shape, lambda i: (0, 0)),
                  pl.BlockSpec(gq.shape, lambda i: (0, 0)),
                  pl.BlockSpec((None, M, n), lambda i: (i // nt, 0, 0)),
                  pl.BlockSpec(wo.shape, lambda i: (0, 0))],
        out_specs=pl.BlockSpec((tm, D_MODEL), lambda i: (i, 0)),
        out_shape=jax.ShapeDtypeStruct((T, D_MODEL), F32),
        compiler_params=_params(("parallel",)), name="mem_attn",
    )(x2, g, wq, gq, kv, wo)


def _ffn_kernel(x_ref, g_ref, wg_ref, wu_ref, wo_ref, o_ref, xn_ref, acc_ref):
    j = pl.program_id(1)

    @pl.when(j == 0)
    def _():
        xn_ref[...] = _rms_rows(x_ref[...], g_ref[...]).astype(BF16)
        acc_ref[...] = x_ref[...]

    xn = xn_ref[...]
    gate = _dot(xn, wg_ref[...])
    up = _dot(xn, wu_ref[...])
    act = (gate * jax.nn.sigmoid(gate) * up).astype(BF16)
    acc_ref[...] += _dot(act, wo_ref[...])

    @pl.when(j == pl.num_programs(1) - 1)
    def _():
        o_ref[...] = acc_ref[...]


def _ffn(x2, g, wg, wu, wo, tm=512, n_split=2):
    T = x2.shape[0]
    tf = D_FF // n_split
    return pl.pallas_call(
        _ffn_kernel, grid=(T // tm, n_split),
        in_specs=[pl.BlockSpec((tm, D_MODEL), lambda i, j: (i, 0)),
                  pl.BlockSpec(g.shape, lambda i, j: (0, 0)),
                  pl.BlockSpec((D_MODEL, tf), lambda i, j: (0, j)),
                  pl.BlockSpec((D_MODEL, tf), lambda i, j: (0, j)),
                  pl.BlockSpec((tf, D_MODEL), lambda i, j: (j, 0))],
        out_specs=pl.BlockSpec((tm, D_MODEL), lambda i, j: (i, 0)),
        out_shape=jax.ShapeDtypeStruct((T, D_MODEL), F32),
        scratch_shapes=[pltpu.VMEM((tm, D_MODEL), BF16), pltpu.VMEM((tm, D_MODEL), F32)],
        compiler_params=_params(("parallel", "arbitrary")), name="ffn",
    )(x2, g, wg, wu, wo)


def _nsa_prep_kernel(x_ref, trig_ref, gmix_ref, w_ref, nq_ref, nk_ref,
                     gq_ref, gks_ref, gkw_ref,
                     qc_ref, qr_ref, kvs_ref, kvst_ref, kvw_ref, kvwt_ref, kc_ref, vc_ref, gtt_ref):
    xn = _rms_rows(x_ref[...], gmix_ref[...]).astype(BF16)
    y = _dot(xn, w_ref[...])
    c64, s64 = trig_ref[:, :LANES], trig_ref[:, LANES:]
    lane = _lane_iota(c64.shape)
    lo64 = (lane % 64) < 8
    first64 = lane < 64
    c64k = jnp.where(first64, c64, 1.0)
    s64k = jnp.where(first64, s64, 0.0)
    nq = nq_ref[...]
    nk = nk_ref[...]

    def col(j):
        return y[:, j * LANES:(j + 1) * LANES]

    for j in range(8):
        qn = _head_norm(col(j), nq, gq_ref[...])
        qr = _rope(qn, c64, s64, lo64, 8)
        qc_ref[2 * j] = qn[:, :64].astype(BF16)
        qc_ref[2 * j + 1] = qn[:, 64:].astype(BF16)
        qr_ref[2 * j] = qr[:, :64].astype(BF16)
        qr_ref[2 * j + 1] = qr[:, 64:].astype(BF16)
    for g in range(NSA_GROUPS):
        kv = _kv_column(col(8 + g), nk, gks_ref[...], c64k, s64k, lo64, first64)
        kvs_ref[g] = kv.astype(BF16)
        kvst_ref[g] = kv.T.astype(BF16)
        kv = _kv_column(col(12 + g), nk, gkw_ref[...], c64k, s64k, lo64, first64)
        kvw_ref[g] = kv.astype(BF16)
        kvwt_ref[g] = kv.T.astype(BF16)
    kc_ref[...] = y[:, 16 * LANES:18 * LANES]
    vc_ref[...] = y[:, 18 * LANES:20 * LANES]
    gates_t = jax.nn.sigmoid(col(20)).T
    for g in range(NSA_GROUPS):
        gtt_ref[g] = gates_t[12 * g:12 * (g + 1), :]


def _nsa_prep(x2, trig, gmix, w, tabs, B, S):
    T = x2.shape[0]
    tm = ATT_T
    nt = S // tm
    nq, nk, gq, gks, gkw = tabs

    def full(a):
        return pl.BlockSpec(a.shape, lambda i: (0,) * a.ndim)

    def hm(width, heads):
        return pl.BlockSpec((None, heads, tm, width), lambda i: (i // nt, 0, i % nt, 0))

    def hmt(heads):
        return pl.BlockSpec((None, heads, None, 128, tm), lambda i: (i // nt, 0, i % nt, 0, 0))

    def tokm(width):
        return pl.BlockSpec((None, tm, width), lambda i: (i // nt, i % nt, 0))

    out_shape = (
        jax.ShapeDtypeStruct((B, 16, S, 64), BF16),
        jax.ShapeDtypeStruct((B, 16, S, 64), BF16),
        jax.ShapeDtypeStruct((B, 4, S, 128), BF16),
        jax.ShapeDtypeStruct((B, 4, nt, 128, tm), BF16),
        jax.ShapeDtypeStruct((B, 4, S, 128), BF16),
        jax.ShapeDtypeStruct((B, 4, nt, 128, tm), BF16),
        jax.ShapeDtypeStruct((B, S, 256), F32),
        jax.ShapeDtypeStruct((B, S, 256), F32),
        jax.ShapeDtypeStruct((B, 4, 12, S), F32),
    )
    out_specs = (hm(64, 16), hm(64, 16), hm(128, 4), hmt(4), hm(128, 4), hmt(4), tokm(256), tokm(256),
                 pl.BlockSpec((None, 4, 12, tm), lambda i: (i // nt, 0, 0, i % nt)))
    in_specs = [pl.BlockSpec((tm, D_MODEL), lambda i: (i, 0)),
                pl.BlockSpec((tm, 2 * LANES), lambda i: (i, 0)),
                full(gmix), full(w), full(nq), full(nk), full(gq), full(gks), full(gkw)]
    return pl.pallas_call(
        _nsa_prep_kernel, grid=(T // tm,), in_specs=in_specs, out_specs=out_specs,
        out_shape=out_shape, compiler_params=_params(("parallel",)), name="nsa_prep",
    )(x2, trig, gmix, w, nq, nk, gq, gks, gkw)


def _compress_one(x16, pa, pb, w1a, w1b, w2):
    n16 = x16.shape[0]
    h_a = _dot((x16 + pa).astype(BF16), w1a)
    h_b = _dot((x16 + pb).astype(BF16), w1b)
    pre = h_a + pltpu.roll(h_b, n16 - 1, 0)
    act = pre * jax.nn.sigmoid(pre)
    return _dot(act.astype(BF16), w2)


def _compress_kernel(xk_ref, xv_ref, pk_ref, pv_ref, w1k_ref, w1v_ref, w2k_ref, w2v_ref, gk_ref,
                     o_ref, ot_ref):
    half = w1k_ref.shape[0] // 2
    k = _compress_one(xk_ref[...], pk_ref[0:1, :], pk_ref[1:2, :],
                      w1k_ref[:half, :], w1k_ref[half:, :], w2k_ref[...])
    k = _rms_rows(k, gk_ref[...])
    v = _compress_one(xv_ref[...], pv_ref[0:1, :], pv_ref[1:2, :],
                      w1v_ref[:half, :], w1v_ref[half:, :], w2v_ref[...])
    kv = jnp.concatenate([k, v], axis=1)
    o_ref[...] = kv.astype(BF16)
    ot_ref[...] = kv.T.astype(BF16)


def _compress(xk16, xv16, pk, pv, w1k, w1v, w2k, w2v, gk):
    B, G, n16, width = xk16.shape

    def full(a):
        return pl.BlockSpec(a.shape, lambda b, g: (0,) * a.ndim)

    xspec = pl.BlockSpec((None, None, n16, width), lambda b, g: (b, g, 0, 0))
    return pl.pallas_call(
        _compress_kernel, grid=(B, G),
        in_specs=[xspec, xspec, full(pk), full(pv), full(w1k), full(w1v), full(w2k), full(w2v), full(gk)],
        out_specs=(pl.BlockSpec((None, None, n16, 128), lambda b, g: (b, g, 0, 0)),
                   pl.BlockSpec((None, None, 128, n16), lambda b, g: (b, g, 0, 0))),
        out_shape=(jax.ShapeDtypeStruct((B, G, n16, 128), BF16),
                   jax.ShapeDtypeStruct((B, G, 128, n16), BF16)),
        compiler_params=_params(("parallel", "parallel")), name="nsa_compress",
    )(xk16, xv16, pk, pv, w1k, w1v, w2k, w2v, gk)


def _nsa_kernel(qc_ref, qr_ref, kvc_ref, kvct_ref, kvs_ref, kvst_ref, kvw_ref, kvwt_ref, gtt_ref,
                o_ref, sel_ref, m_ref, l_ref, acc_ref, *, n_cmp, n_top):
    t = ATT_T
    HG = NSA_HEADS // NSA_GROUPS
    n_sel = sel_ref.shape[0]
    n16 = kvc_ref.shape[0]
    i = pl.program_id(2)
    t0 = i * t

    kvc = kvc_ref[...]
    kvct = kvct_ref[...]
    n_id = _row_iota((n16, t))
    q_id = t0 + _lane_iota((n16, t))
    visible = (n_id < n_cmp) & (n_id * NSA_CMP_STRIDE + (NSA_CMP_LEN - 1) <= q_id)
    bias_c = jnp.where(visible, 0.0, MASK_BIAS)
    p_sum = jnp.zeros((n16, t), F32)
    o_c = []
    probs, inv_ls, outs = _softmax_direct([_pad_q(qc_ref[j]) for j in range(HG)], kvc, [kvct], bias_c)
    for j in range(HG):
        p_sum = p_sum + probs[j] * inv_ls[j]
        o_c.append(outs[j] * inv_ls[j])

    b_id = _row_iota((n_sel, n16)) * NSA_SEL_LEN
    r_id = _lane_iota((n_sel, n16)) * NSA_CMP_STRIDE
    cover_t = ((r_id < b_id + NSA_SEL_LEN) & (r_id + NSA_CMP_LEN > b_id)
               & (_lane_iota((n_sel, n16)) < n_cmp))
    cover_t = jnp.where(cover_t, 1.0, 0.0).astype(BF16)
    p_hi, p_lo = _split_bf16(p_sum)
    imp = _dot(cover_t, p_hi) + _dot(cover_t, p_lo)
    blk = _row_iota((n_sel, t))
    cur = lax.shift_right_logical(t0 + _lane_iota((n_sel, t)), NSA_SEL_LEN.bit_length() - 1)
    forced = (blk == 0) | (blk == cur) | (blk == cur - 1)
    imp = jnp.where(forced, NSA_FORCE, imp)
    visible_blk = blk <= cur
    imp = jnp.where(visible_blk, imp, NEG_INF)
    n_larger = jnp.zeros((n_sel, t), F32)
    for m in range(n_sel):
        n_larger = n_larger + jnp.where(imp[m:m + 1, :] > imp, 1.0, 0.0)
    sel_fast = n_larger < n_top
    n_picked = jnp.sum(jnp.where(sel_fast & visible_blk, 1.0, 0.0), axis=0, keepdims=True)
    n_wanted = jnp.minimum(cur[0:1, :] + 1, n_top).astype(F32)
    sel_ref[...] = jnp.where(sel_fast, 0.0, MASK_BIAS)

    @pl.when(jnp.max(jnp.abs(n_picked - n_wanted)) > 0.0)
    def _():
        sel_ref[...] = (_rank_select_t(imp, None, n_top) - 1.0) * (-MASK_BIAS)

    qs = [_pad_q(qr_ref[j]) for j in range(HG)]

    per = t // NSA_SEL_LEN
    kio2 = _row_iota((2 * t, t))
    qio2 = _lane_iota((2 * t, t))
    flash = _Flash(m_ref, l_ref, acc_ref, t)
    flash.reset()

    def sel_pair(cc, carry):
        c0 = 2 * cc
        k0 = pl.multiple_of(c0 * t, 2 * t)
        rows = [jnp.broadcast_to(sel_ref[pl.ds(c0 * per + u, 1), :], (NSA_SEL_LEN, t))
                for u in range(2 * per)]
        bias = jnp.concatenate(rows, axis=0)
        bias = jnp.where((c0 - i) * t + kio2 <= qio2, bias, MASK_BIAS)
        kv = kvs_ref[pl.ds(k0, 2 * t), :]
        kvts = [kvst_ref[c0], kvst_ref[c0 + 1]]
        flash.update(qs, [kv] * HG, [kvts] * HG, [bias] * HG)
        return carry

    lax.fori_loop(0, (i + 2) // 2, sel_pair, 0)
    o_s = [flash.result(j) for j in range(HG)]

    n_wc = NSA_WINDOW // t + 1
    cw = jnp.maximum(i - (n_wc - 1), 0)
    kw0 = pl.multiple_of(cw * t, t)
    dist = (i - cw) * t + _lane_iota((n_wc * t, t)) - _row_iota((n_wc * t, t))
    bias_w = jnp.where((dist >= 0) & (dist < NSA_WINDOW), 0.0, MASK_BIAS)
    kv = kvw_ref[pl.ds(kw0, n_wc * t), :]
    kvts = [kvwt_ref[cw + u] for u in range(n_wc)]
    _, inv_ls, outs = _softmax_direct(qs, kv, kvts, bias_w)
    o_w = [outs[j] * inv_ls[j] for j in range(HG)]

    gt = gtt_ref[...]
    heads = [gt[3 * j:3 * j + 1, :] * o_c[j] + gt[3 * j + 1:3 * j + 2, :] * o_s[j]
             + gt[3 * j + 2:3 * j + 3, :] * o_w[j] for j in range(HG)]
    _store_heads(o_ref, heads)


def _nsa_attention(qc, qr, kvc, kvct, kvs, kvst, kvw, kvwt, gates_t):
    B, H, S, _ = qc.shape
    G = NSA_GROUPS
    HG = H // G
    t = ATT_T
    nt = S // t
    n16 = kvc.shape[2]
    n_cmp = (S - NSA_CMP_LEN) // NSA_CMP_STRIDE + 1
    n_sel = S // NSA_SEL_LEN
    n_top = min(NSA_SEL_TOPK, n_sel)
    qspec = pl.BlockSpec((None, HG, t, 64), lambda b, g, i: (b, g, i, 0))
    kvspec = pl.BlockSpec((None, None, S, 128), lambda b, g, i: (b, g, 0, 0))
    kvtspec = pl.BlockSpec((None, None, nt, 128, t), lambda b, g, i: (b, g, 0, 0, 0))
    in_specs = [qspec, qspec,
                pl.BlockSpec((None, None, n16, 128), lambda b, g, i: (b, g, 0, 0)),
                pl.BlockSpec((None, None, 128, n16), lambda b, g, i: (b, g, 0, 0)),
                kvspec, kvtspec, kvspec, kvtspec,
                pl.BlockSpec((None, None, 12, t), lambda b, g, i: (b, g, 0, i))]
    return pl.pallas_call(
        functools.partial(_nsa_kernel, n_cmp=n_cmp, n_top=n_top), grid=(B, G, nt), in_specs=in_specs,
        out_specs=pl.BlockSpec((None, t, HG * 64), lambda b, g, i: (b, i, g)),
        out_shape=jax.ShapeDtypeStruct((B, S, H * 64), BF16),
        scratch_shapes=[pltpu.VMEM((n_sel, t), F32)] + _flash_scratch(HG, t),
        compiler_params=_params(("parallel", "parallel", "parallel")), name="nsa_attention",
    )(qc, qr, kvc, kvct, kvs, kvst, kvw, kvwt, gates_t)


def _rope_freq_row(period, rot):
    half = rot // 2
    inv_freq = ROPE_THETA ** (-(jnp.arange(half, dtype=F32) * 2.0 / rot))
    lane = jnp.arange(LANES) % period
    f = jnp.where(lane < rot, inv_freq[lane % half], 0.0)
    return f.reshape(1, LANES).astype(F32)


def _norm_matrices():
    r = jnp.arange(LANES)
    same = (r[:, None] // 64) == (r[None, :] // 64)
    nq = jnp.where(same, 1.0 / 64, 0.0).astype(BF16)
    nk = jnp.where(same & (r[:, None] < 64), 1.0 / 64, 0.0).astype(BF16)
    return nq, nk


def _q_gain(g):
    return (jnp.tile(g.astype(F32), 2) * Q_SCALE).reshape(1, LANES)


def _k_gain(g):
    return jnp.concatenate([g.astype(F32), jnp.ones((64,), F32)]).reshape(1, LANES)


def _interleave_kv(wk, wv, n_heads):
    d = wk.shape[0]
    wk = wk.reshape(d, n_heads, 64)
    wv = wv.reshape(d, n_heads, 64)
    return jnp.concatenate([wk, wv], axis=2).reshape(d, n_heads * 128)


def _split_cols(w, sizes):
    out, start = [], 0
    for n in sizes:
        out.append(w[:, start:start + n])
        start += n
    return out


def _mixer_layer0(x2, trig, B, S, gmix, w_in, w_out, a_q_norm, a_k_norm, b_q_norm, b_k_norm):
    sizes = (512, 64, 64, 256, 32, 8, 512, 512, 512)
    waq, wak, wav, wiq, wik, wiw, wbq, wbk, wbv = _split_cols(w_in, sizes)
    pad = jnp.zeros((D_MODEL, LANES - 40), w_in.dtype)
    w = jnp.concatenate([waq, wbq, _interleave_kv(wbk, wbv, 8), wak, wav, wiq, wik, wiw, pad],
                        axis=1).astype(BF16)
    nq, nk = _norm_matrices()
    tabs = (nq, nk, _q_gain(a_q_norm), _q_gain(b_q_norm), _k_gain(a_k_norm), _k_gain(b_k_norm))
    aq, bq, bkv, bkvt, akv, akvt, iq, ik, iwt, km = _ab_prep(x2, trig, gmix, w, tabs, B, S)
    n_blk = S // MOBA_BLOCK
    kmean = km.reshape(B, n_blk, 8, 128).transpose(0, 2, 1, 3)
    o_a = _dsa_attention(iq, iwt, ik, aq, akv, akvt).reshape(B * S, 512)
    o_b = _moba_attention(bq, bkv, bkvt, kmean).reshape(B * S, 512)
    w_out = w_out.astype(BF16)
    return _out_proj([o_a, o_b], [w_out[:512], w_out[512:]], x2)


def _mixer_layer1(x2, trig, B, S, gmix, w_in, w_out, q_norm, kcmp_norm, ksel_norm, kwin_norm,
                  pos_k, pos_v, w1_k, w2_k, w1_v, w2_v):
    G = NSA_GROUPS
    sizes = (1024,) + (256,) * 6 + (48,)
    wq, wkc, wvc, wks, wvs, wkw, wvw, wgt = _split_cols(w_in, sizes)
    pad = jnp.zeros((D_MODEL, LANES - 48), w_in.dtype)
    w = jnp.concatenate([wq, _interleave_kv(wks, wvs, G), _interleave_kv(wkw, wvw, G),
                         wkc, wvc, wgt, pad], axis=1).astype(BF16)
    nq, nk = _norm_matrices()
    tabs = (nq, nk, _q_gain(q_norm), _k_gain(ksel_norm), _k_gain(kwin_norm))
    qc, qr, kvs, kvst, kvw, kvwt, kc_raw, vc_raw, gates_t = _nsa_prep(x2, trig, gmix, w, tabs, B, S)

    n16 = S // NSA_CMP_STRIDE

    def blocks16(t):
        return (t.reshape(B, n16, NSA_CMP_STRIDE, G, HEAD_DIM).transpose(0, 3, 1, 2, 4)
                .reshape(B, G, n16, NSA_CMP_STRIDE * HEAD_DIM))

    def pos_rows(p):
        return p.astype(F32).reshape(2, NSA_CMP_STRIDE * HEAD_DIM)

    kvc, kvct = _compress(blocks16(kc_raw), blocks16(vc_raw), pos_rows(pos_k), pos_rows(pos_v),
                          w1_k.astype(BF16), w1_v.astype(BF16), w2_k.astype(BF16), w2_v.astype(BF16),
                          kcmp_norm.astype(F32).reshape(1, HEAD_DIM))
    o = _nsa_attention(qc, qr, kvc, kvct, kvs, kvst, kvw, kvwt, gates_t)
    return _out_proj([o.reshape(B * S, NSA_HEADS * HEAD_DIM)], [w_out.astype(BF16)], x2)


def _mem_and_ffn(x2, mem, S, g_mem, g_src, w_q, w_kv, w_o, q_norm, k_norm, g_ffn, ffn_w_in, ffn_w_out):
    row = lambda v: v.astype(F32).reshape(1, -1)
    kv = _mem_kv(mem, row(g_src), w_kv.astype(BF16), row(k_norm))
    x2 = _mem_attn(x2, row(g_mem), w_q.astype(BF16), row(q_norm), kv, w_o.astype(BF16), S)
    wg = ffn_w_in[:, :D_FF].astype(BF16)
    wu = ffn_w_in[:, D_FF:].astype(BF16)
    return _ffn(x2, row(g_ffn), wg, wu, ffn_w_out.astype(BF16))


def kernel(x, mem, positions, norm_mix, norm_mem, norm_mem_src, norm_ffn, ab_w_in, ab_w_out, dsa_q_norm, dsa_k_norm, moba_q_norm, moba_k_norm, nsa_w_in, nsa_w_out, nsa_q_norm, nsa_kcmp_norm, nsa_ksel_norm, nsa_kwin_norm, nsa_cmp_pos_k, nsa_cmp_pos_v, nsa_cmp_w1_k, nsa_cmp_w2_k, nsa_cmp_w1_v, nsa_cmp_w2_v, mem_w_q, mem_w_kv, mem_w_o, mem_q_norm, mem_k_norm, ffn_w_in, ffn_w_out):
    B, S, D = x.shape
    depth = norm_mix.shape[0]
    x2 = x.reshape(B * S, D)
    trig = _rope_trig(positions.astype(F32).reshape(B * S, 1), _rope_freq_row(64, 16), _rope_freq_row(32, 8))
    row = lambda v: v.astype(F32).reshape(1, -1)
    for i in range(depth):
        j = i // 2
        if i % 2 == 0:
            x2 = _mixer_layer0(x2, trig, B, S, row(norm_mix[i]), ab_w_in[j], ab_w_out[j],
                               dsa_q_norm[j], dsa_k_norm[j], moba_q_norm[j], moba_k_norm[j])
        else:
            x2 = _mixer_layer1(x2, trig, B, S, row(norm_mix[i]), nsa_w_in[j], nsa_w_out[j],
                               nsa_q_norm[j], nsa_kcmp_norm[j], nsa_ksel_norm[j], nsa_kwin_norm[j],
                               nsa_cmp_pos_k[j], nsa_cmp_pos_v[j], nsa_cmp_w1_k[j], nsa_cmp_w2_k[j],
                               nsa_cmp_w1_v[j], nsa_cmp_w2_v[j])
        x2 = _mem_and_ffn(x2, mem, S, norm_mem[i], norm_mem_src[i], mem_w_q[i], mem_w_kv[i], mem_w_o[i],
                          mem_q_norm[i], mem_k_norm[i], norm_ffn[i], ffn_w_in[i], ffn_w_out[i])
    return x2.reshape(B, S, D)
```

```python
import functools
import math

import jax
import jax.numpy as jnp
from jax import lax
from jax.experimental import pallas as pl
from jax.experimental.pallas import tpu as pltpu

F32 = jnp.float32
BF16 = jnp.bfloat16
I32 = jnp.int32
I16 = jnp.int16

D_MODEL = 1024
N_MEM = 256
HEAD_DIM = 64
ROPE_THETA = 500000.0
RMS_EPS = 1e-6
NEG_INF = -1e30
TINY = 1e-20

DSA_HEADS = 8
DSA_IDX_HEADS = 8
DSA_IDX_DIM = 32
DSA_TOPK = 256
MOBA_HEADS = 8
MOBA_BLOCK = 256
MOBA_TOPK = 3
NSA_HEADS = 16
NSA_GROUPS = 4
NSA_CMP_LEN = 32
NSA_CMP_STRIDE = 16
NSA_SEL_LEN = 64
NSA_SEL_TOPK = 16
NSA_WINDOW = 512
NSA_FORCE = 1e4
MEM_HEADS = 4
MEM_HEAD_DIM = 128
D_FF = ((8 * D_MODEL + 3 * 256 - 1) // (3 * 256)) * 256

LANES = 128
SUBLANES = 8
INT_MIN = -(2 ** 31)
VMEM_LIMIT = 56 * 1024 * 1024

ATT_T = 256
MASK_BIAS = -1e30
M_FLOOR = -1e29
LOG2E = math.log2(math.e)
Q_SCALE = HEAD_DIM ** -0.5 * LOG2E

NT_DIMS = (((1,), (1,)), ((), ()))


def _dot(a, b):
    return jnp.dot(a, b, preferred_element_type=F32)


def _dot_nt(a, b):
    return lax.dot_general(a, b, NT_DIMS, preferred_element_type=F32)


def _split_bf16(a):
    hi = a.astype(BF16)
    return hi, (a - hi.astype(F32)).astype(BF16)


def _split_dot(a, b):
    hi, lo = _split_bf16(a)
    return _dot(hi, b) + _dot(lo, b)


def _rms_rows(x, gain):
    ms = jnp.mean(x * x, axis=-1, keepdims=True)
    return x * lax.rsqrt(ms + RMS_EPS) * gain


def _params(sem):
    return pltpu.CompilerParams(dimension_semantics=sem, vmem_limit_bytes=VMEM_LIMIT)


def _head_norm(y, norm_m, gain):
    ms = _split_dot(y * y, norm_m)
    return y * lax.rsqrt(ms + RMS_EPS) * gain


def _rope(y, c, s, lo_mask, half):
    sw = jnp.where(lo_mask, pltpu.roll(y, LANES - half, 1), pltpu.roll(y, half, 1))
    return y * c + sw * s


def _lane_iota(shape):
    return lax.broadcasted_iota(I32, shape, 1)


def _row_iota(shape):
    return lax.broadcasted_iota(I32, shape, 0)


def _rope_tables(pos, ftab, period, half):
    ang = pos * ftab
    lane = _lane_iota(ang.shape) % period
    c = jnp.cos(ang)
    s = jnp.sin(ang) * jnp.where(lane < half, -1.0, 1.0)
    return c, s


def _kv_column(yc, nk, gain, c64k, s64k, lo64, first64):
    kn = jnp.where(first64, _head_norm(yc, nk, gain), yc)
    return _rope(kn, c64k, s64k, lo64, 8)


def _rope_trig_kernel(pos_ref, f64_ref, f32_ref, o_ref):
    pos = pos_ref[...]
    c64, s64 = _rope_tables(pos, f64_ref[...], 64, 8)
    c32, s32 = _rope_tables(pos, f32_ref[...], 32, 4)
    o_ref[...] = jnp.concatenate([c64, s64, c32, s32], axis=1)


def _rope_trig(pos2, f64, f32t, tm=1024):
    T = pos2.shape[0]
    return pl.pallas_call(
        _rope_trig_kernel, grid=(T // tm,),
        in_specs=[pl.BlockSpec((tm, 1), lambda i: (i, 0)),
                  pl.BlockSpec(f64.shape, lambda i: (0, 0)), pl.BlockSpec(f32t.shape, lambda i: (0, 0))],
        out_specs=pl.BlockSpec((tm, 4 * LANES), lambda i: (i, 0)),
        out_shape=jax.ShapeDtypeStruct((T, 4 * LANES), F32),
        compiler_params=_params(("parallel",)), name="rope_trig",
    )(pos2, f64, f32t)


def _ab_prep_kernel(x_ref, trig_ref, gmix_ref, w_ref, nq_ref, nk_ref,
                    gaq_ref, gbq_ref, gak_ref, gbk_ref,
                    aq_ref, bq_ref, bkv_ref, bkvt_ref, akv_ref, akvt_ref, iq_ref, ik_ref, iwt_ref, km_ref,
                    *, n_tiles):
    xn = _rms_rows(x_ref[...], gmix_ref[...]).astype(BF16)
    y = _dot(xn, w_ref[...])
    c64, s64, c32, s32 = [trig_ref[:, j * LANES:(j + 1) * LANES] for j in range(4)]
    lane = _lane_iota(c64.shape)
    lo64 = (lane % 64) < 8
    lo32 = (lane % 32) < 4
    first64 = lane < 64
    c64k = jnp.where(first64, c64, 1.0)
    s64k = jnp.where(first64, s64, 0.0)
    first32 = lane < 32
    c32k = jnp.where(first32, c32, 1.0)
    s32k = jnp.where(first32, s32, 0.0)
    nq = nq_ref[...]
    nk = nk_ref[...]

    def col(j):
        return y[:, j * LANES:(j + 1) * LANES]

    for j in range(4):
        q = _rope(_head_norm(col(j), nq, gaq_ref[...]), c64, s64, lo64, 8)
        aq_ref[2 * j] = q[:, :64].astype(BF16)
        aq_ref[2 * j + 1] = q[:, 64:].astype(BF16)
    for j in range(4):
        q = _rope(_head_norm(col(4 + j), nq, gbq_ref[...]), c64, s64, lo64, 8)
        bq_ref[2 * j] = q[:, :64].astype(BF16)
        bq_ref[2 * j + 1] = q[:, 64:].astype(BF16)
    blk_onehot = jnp.where(lane == HEAD_DIM + pl.program_id(0) % n_tiles, 1.0, 0.0)
    for h in range(8):
        kv = _kv_column(col(8 + h), nk, gbk_ref[...], c64k, s64k, lo64, first64)
        bkv_ref[h] = jnp.where(first64, kv, blk_onehot).astype(BF16)
        bkvt_ref[h] = kv.T.astype(BF16)
        km_ref[h:h + 1, :] = jnp.mean(kv, axis=0, keepdims=True)
    kv = _kv_column(col(16), nk, gak_ref[...], c64k, s64k, lo64, first64)
    akv_ref[...] = kv.astype(BF16)
    akvt_ref[...] = kv.T.astype(BF16)
    for j in range(2):
        q = _rope(col(17 + j), c32, s32, lo32, 4)
        for u in range(4):
            iq_ref[4 * j + u] = q[:, 32 * u:32 * (u + 1)].astype(BF16)
    yc = col(19)
    ik_ref[...] = _rope(yc, c32k, s32k, lo32, 4)[:, :32].astype(BF16)
    iwt_ref[...] = yc.T[32:40, :]


def _ab_prep(x2, trig, gmix, w, tabs, B, S):
    T = x2.shape[0]
    tm = ATT_T
    nt = S // tm
    n_cols = w.shape[1]
    nq, nk, gaq, gbq, gak, gbk = tabs

    def full(a):
        return pl.BlockSpec(a.shape, lambda i: (0,) * a.ndim)

    def hm(width, heads=8):
        return pl.BlockSpec((None, heads, tm, width), lambda i: (i // nt, 0, i % nt, 0))

    def tokm(width):
        return pl.BlockSpec((None, tm, width), lambda i: (i // nt, i % nt, 0))

    out_shape = (
        jax.ShapeDtypeStruct((B, 8, S, 64), BF16),
        jax.ShapeDtypeStruct((B, 8, S, 64), BF16),
        jax.ShapeDtypeStruct((B, 8, S, 128), BF16),
        jax.ShapeDtypeStruct((B, 8, nt, 128, tm), BF16),
        jax.ShapeDtypeStruct((B, S, 128), BF16),
        jax.ShapeDtypeStruct((B, nt, 128, tm), BF16),
        jax.ShapeDtypeStruct((B, 8, S, 32), BF16),
        jax.ShapeDtypeStruct((B, S, 32), BF16),
        jax.ShapeDtypeStruct((B, 8, S), F32),
        jax.ShapeDtypeStruct((T // tm, 8, 128), F32),
    )
    out_specs = (hm(64), hm(64), hm(128),
                 pl.BlockSpec((None, 8, None, 128, tm), lambda i: (i // nt, 0, i % nt, 0, 0)),
                 tokm(128),
                 pl.BlockSpec((None, None, 128, tm), lambda i: (i // nt, i % nt, 0, 0)),
                 hm(32), tokm(32),
                 pl.BlockSpec((None, 8, tm), lambda i: (i // nt, 0, i % nt)),
                 pl.BlockSpec((None, 8, 128), lambda i: (i, 0, 0)))
    in_specs = [pl.BlockSpec((tm, D_MODEL), lambda i: (i, 0)),
                pl.BlockSpec((tm, 4 * LANES), lambda i: (i, 0)),
                full(gmix), pl.BlockSpec((D_MODEL, n_cols), lambda i: (0, 0)),
                full(nq), full(nk), full(gaq), full(gbq), full(gak), full(gbk)]
    return pl.pallas_call(
        functools.partial(_ab_prep_kernel, n_tiles=nt), grid=(T // tm,), in_specs=in_specs, out_specs=out_specs,
        out_shape=out_shape, compiler_params=_params(("parallel",)), name="ab_prep",
    )(x2, trig, gmix, w, nq, nk, gaq, gbq, gak, gbk)


def _pad_q(q):
    return jnp.concatenate([q, jnp.zeros_like(q)], axis=1)


def _bias_lanes(q, rows):
    n, tq = rows.shape
    parts = [jnp.zeros((HEAD_DIM, tq), F32), rows]
    if n < HEAD_DIM:
        parts.append(jnp.zeros((HEAD_DIM - n, tq), F32))
    lanes = jnp.concatenate(parts, axis=0).T.astype(BF16)
    return jnp.where(_lane_iota(q.shape) < HEAD_DIM, q, lanes)


class _Flash:
    def __init__(self, m_ref, l_ref, acc_ref, s_ref, p_ref, tq):
        self.m_ref, self.l_ref, self.acc_ref, self.tq = m_ref, l_ref, acc_ref, tq
        self.s_ref, self.p_ref = s_ref, p_ref

    def reset(self):
        self.m_ref[...] = jnp.full(self.m_ref.shape, M_FLOOR, F32)
        self.l_ref[...] = jnp.zeros(self.l_ref.shape, F32)
        self.acc_ref[...] = jnp.zeros(self.acc_ref.shape, F32)

    def update(self, qs, kvs, kvts, biases):
        n = len(qs)
        tq = self.tq
        col_max = []
        for i in range(n):
            s = _dot_nt(kvs[i], qs[i])
            if biases[i] is not None:
                s = s + biases[i]
            self.s_ref[i] = s
            col_max.append(jnp.max(s, axis=0, keepdims=True))
        alphas = []
        for i in range(n):
            cols = slice(i * tq, (i + 1) * tq)
            m = self.m_ref[:, cols]
            m_new = jnp.maximum(m, col_max[i])
            p = jnp.exp2(self.s_ref[i] - m_new)
            alpha = jnp.exp2(m - m_new)
            self.m_ref[:, cols] = m_new
            self.l_ref[:, cols] = alpha * self.l_ref[:, cols] + p.reshape(-1, SUBLANES, tq).sum(axis=0)
            self.p_ref[i] = p.astype(BF16)
            alphas.append(alpha)
        for i in range(n):
            cols = slice(i * tq, (i + 1) * tq)
            pv, r0 = None, 0
            for kvt in kvts[i]:
                part = _dot(kvt, self.p_ref[i, r0:r0 + kvt.shape[1], :])
                pv = part if pv is None else pv + part
                r0 += kvt.shape[1]
            self.acc_ref[:, cols] = alphas[i] * self.acc_ref[:, cols] + pv

    def result(self, slot):
        cols = slice(slot * self.tq, (slot + 1) * self.tq)
        l = jnp.sum(self.l_ref[:, cols], axis=0, keepdims=True)
        return self.acc_ref[:, cols] / jnp.maximum(l, TINY)


def _flash_scratch(n_slots, tq, kc):
    return [pltpu.VMEM((1, n_slots * tq), F32), pltpu.VMEM((SUBLANES, n_slots * tq), F32),
            pltpu.VMEM((LANES, n_slots * tq), F32),
            pltpu.VMEM((n_slots, kc, tq), F32), pltpu.VMEM((n_slots, kc, tq), BF16)]


def _softmax_direct(qs, kv, kvts, bias):
    scores = [_dot_nt(kv, q) for q in qs]
    probs, inv_ls = [], []
    for s in scores:
        s = s + bias
        m = jnp.maximum(jnp.max(s, axis=0, keepdims=True), M_FLOOR)
        p = jnp.exp2(s - m)
        inv_ls.append(1.0 / jnp.maximum(jnp.sum(p, axis=0, keepdims=True), TINY))
        probs.append(p)
    outs = []
    for p in probs:
        pb = p.astype(BF16)
        o, r0 = None, 0
        for kvt in kvts:
            part = _dot(kvt, pb[r0:r0 + kvt.shape[1]])
            o = part if o is None else o + part
            r0 += kvt.shape[1]
        outs.append(o)
    return probs, inv_ls, outs


def _causal_bias(t):
    return jnp.where(_row_iota((t, t)) <= _lane_iota((t, t)), 0.0, MASK_BIAS)


def _store_heads(o_ref, heads_t):
    tq = heads_t[0].shape[1]
    lane = _lane_iota((tq, LANES))
    for u in range(len(heads_t) // 2):
        even = pltpu.roll(heads_t[2 * u].T, 64, 1)
        odd = heads_t[2 * u + 1].T
        o_ref[:, u * LANES:(u + 1) * LANES] = jnp.where(lane < 64, even, odd).astype(o_ref.dtype)


def _rank_select_t(v, n_valid, n_top):
    n = v.shape[0]
    row = _row_iota(v.shape)
    rank = jnp.zeros(v.shape, F32)
    for m in range(n):
        vm = v[m:m + 1, :]
        ahead = (vm > v) | ((vm == v) & (m < row))
        if n_valid is not None:
            ahead = ahead & (m < n_valid)
        rank = rank + jnp.where(ahead, 1.0, 0.0)
    sel = rank < n_top
    if n_valid is not None:
        sel = sel & (row < n_valid)
    return jnp.where(sel, 1.0, 0.0)


def _dsa_kernel(iq_ref, iwt_ref, ik_ref, aq_ref, akv_ref, akvt_ref, o_ref,
                sk_ref, half_ref, bias_ref, xcut_ref, *flash_refs, k_top, index_bits):
    t = ATT_T
    i = pl.program_id(1)
    n_ch = i + 1
    kio = _row_iota((t, t))
    qio = _lane_iota((t, t))

    def causal(c):
        return (c - i) * t + kio <= qio

    def score_chunk(c, carry):
        k0 = pl.multiple_of(c * t, t)
        ikc = ik_ref[pl.ds(k0, t), :]
        sc = jnp.zeros((t, t), F32)
        for h in range(DSA_IDX_HEADS):
            logit = _dot_nt(ikc, iq_ref[h])
            sc = sc + iwt_ref[h:h + 1, :] * jnp.maximum(logit, 0.0)
        sc = jnp.where(sc == 0.0, 0.0, sc)
        bits = pltpu.bitcast(sc, I32)
        key = bits ^ ((bits >> 31) & 0x7FFFFFFF)
        key = jnp.where(causal(c), key, INT_MIN)
        sk_ref[c] = key
        half_ref[c] = (key >> 16).astype(I16)
        return carry

    lax.fori_loop(0, n_ch, score_chunk, 0)

    def count(pred):
        def body(c, acc8):
            ind = jnp.where(pred(sk_ref[c], c), 1.0, 0.0)
            return acc8 + ind.reshape(-1, SUBLANES, t).sum(axis=0)
        acc8 = lax.fori_loop(0, n_ch, body, jnp.zeros((SUBLANES, t), F32))
        return jnp.sum(acc8, axis=0, keepdims=True)

    def count_half(cand):
        rows = 2 * SUBLANES

        def body(c, acc):
            ind = jnp.where(half_ref[c] >= cand, jnp.bfloat16(1), jnp.bfloat16(0))
            parts = [ind[rows * j:rows * (j + 1), :] for j in range(t // rows)]
            while len(parts) > 1:
                parts = [parts[2 * j] + parts[2 * j + 1] for j in range(len(parts) // 2)]
            return acc + parts[0].astype(F32)
        acc = lax.fori_loop(0, n_ch, body, jnp.zeros((rows, t), F32))
        return jnp.sum(acc, axis=0, keepdims=True)

    def half_search():
        def bit_step(b, v):
            cand = v + lax.shift_left(jnp.int32(1), 15 - b)
            return jnp.where(count_half(cand.astype(I16)) >= k_top, cand, v)
        return lax.fori_loop(0, 16, bit_step, jnp.full((1, t), -(2 ** 15), I32))

    thr_hi = half_search()

    def low_half_chunk(c, carry):
        key = sk_ref[c]
        hi = key >> 16
        lo = (key & 0xFFFF) - 2 ** 15
        half_ref[c] = jnp.where(hi > thr_hi, 2 ** 15 - 1, jnp.where(hi < thr_hi, -(2 ** 15), lo)).astype(I16)
        return carry

    lax.fori_loop(0, n_ch, low_half_chunk, 0)
    thr = lax.shift_left(thr_hi, 16) + (half_search() + 2 ** 15)

    need = k_top - count(lambda blk, c: blk > thr)
    n_ge = count(lambda blk, c: blk >= thr)
    xcut_ref[...] = jnp.full((1, t), 2 ** 30, I32)

    @pl.when(jnp.max(n_ge) > k_top)
    def _():
        def x_step(b, x):
            cand = x + lax.shift_left(jnp.int32(1), index_bits - 1 - b)
            ties_below = count(lambda blk, c: (blk == thr) & (c * t + kio < cand))
            return jnp.where(ties_below <= need, cand, x)
        xcut_ref[...] = lax.fori_loop(0, index_bits, x_step, jnp.zeros((1, t), I32))

    xcut = xcut_ref[...]

    n_pairs = (n_ch + 1) // 2

    def bias_chunk(c, carry):
        blk = sk_ref[jnp.minimum(c, i)]
        keep = (blk > thr) | ((blk == thr) & (c * t + kio < xcut))
        bias_ref[c] = jnp.where(keep & causal(c), 0.0, MASK_BIAS)
        return carry

    lax.fori_loop(0, 2 * n_pairs, bias_chunk, 0)

    flash = _Flash(*flash_refs, t)
    flash.reset()
    qs = [_pad_q(aq_ref[h]) for h in range(DSA_HEADS)]

    n = DSA_HEADS

    def att_pair(cc, carry):
        c0 = 2 * cc
        k0 = pl.multiple_of(c0 * t, 2 * t)
        kv = akv_ref[pl.ds(k0, 2 * t), :]
        bias = jnp.concatenate([bias_ref[c0], bias_ref[c0 + 1]], axis=0)
        flash.update(qs, [kv] * n, [[akvt_ref[c0], akvt_ref[c0 + 1]]] * n, [bias] * n)
        return carry

    lax.fori_loop(0, n_pairs, att_pair, 0)
    _store_heads(o_ref, [flash.result(h) for h in range(DSA_HEADS)])


def _dsa_attention(iq, iwt, ik, aq, akv, akvt):
    B, _, S, _ = aq.shape
    t = ATT_T
    nt = S // t
    k_top = min(DSA_TOPK, S // 4)
    in_specs = [
        pl.BlockSpec((None, 8, t, 32), lambda b, i: (b, 0, i, 0)),
        pl.BlockSpec((None, 8, t), lambda b, i: (b, 0, i)),
        pl.BlockSpec((None, S, 32), lambda b, i: (b, 0, 0)),
        pl.BlockSpec((None, 8, t, 64), lambda b, i: (b, 0, i, 0)),
        pl.BlockSpec((None, S, 128), lambda b, i: (b, 0, 0)),
        pl.BlockSpec((None, nt, 128, t), lambda b, i: (b, 0, 0, 0)),
    ]
    return pl.pallas_call(
        functools.partial(_dsa_kernel, k_top=k_top, index_bits=S.bit_length()),
        grid=(B, nt), in_specs=in_specs,
        out_specs=pl.BlockSpec((None, t, 512), lambda b, i: (b, i, 0)),
        out_shape=jax.ShapeDtypeStruct((B, S, 512), BF16),
        scratch_shapes=[pltpu.VMEM((nt, t, t), I32), pltpu.VMEM((nt, t, t), I16), pltpu.VMEM((nt, t, t), F32),
                        pltpu.VMEM((1, t), I32)] + _flash_scratch(DSA_HEADS, t, 2 * t),
        compiler_params=_params(("parallel", "parallel")), name="dsa_attention",
    )(iq, iwt, ik, aq, akv, akvt)


MOBA_HPS = 4


def _moba_kernel(q_ref, kv_ref, kvt_ref, km_ref, o_ref, *flash_refs, n_top):
    t = ATT_T
    own = pl.program_id(2)
    causal = _causal_bias(t)
    flash = _Flash(*flash_refs, t)
    flash.reset()
    qs = []
    for hh in range(MOBA_HPS):
        q = _pad_q(q_ref[hh])
        km_hi, km_lo = _split_bf16(km_ref[hh])
        gate = _dot_nt(km_hi, q) + _dot_nt(km_lo, q)
        keep = _rank_select_t(gate, own, n_top)
        keep = jnp.where(_row_iota(keep.shape) == own, 1.0, keep)
        qs.append(_bias_lanes(q, (keep - 1.0) * (-MASK_BIAS)))

    def operands(cc):
        n0 = 2 * cc
        k0 = pl.multiple_of(n0 * t, 2 * t)
        heads = range(MOBA_HPS)
        return (n0, [kv_ref[hh, pl.ds(k0, 2 * t), :] for hh in heads],
                [[kvt_ref[hh, n0], kvt_ref[hh, n0 + 1]] for hh in heads])

    def past_pair(cc, carry):
        _, kvs, kvts = operands(cc)
        flash.update(qs, kvs, kvts, [None] * MOBA_HPS)
        return carry

    n_pairs = (own + 2) // 2
    lax.fori_loop(0, n_pairs - 1, past_pair, 0)
    n0, kvs, kvts = operands(n_pairs - 1)
    own_bias = jnp.concatenate([jnp.where(n0 == own, causal, 0.0), jnp.where(n0 + 1 == own, causal, 0.0)],
                               axis=0)
    flash.update(qs, kvs, kvts, [own_bias] * MOBA_HPS)
    _store_heads(o_ref, [flash.result(hh) for hh in range(MOBA_HPS)])


def _moba_attention(bq, bkv, bkvt, kmean):
    B, H, S, _ = bq.shape
    t = ATT_T
    hps = MOBA_HPS
    n_blk = S // MOBA_BLOCK
    assert t == MOBA_BLOCK and n_blk % 2 == 0 and H % hps == 0
    n_top = max(1, min(MOBA_TOPK, n_blk - 1))
    in_specs = [
        pl.BlockSpec((None, hps, t, 64), lambda b, h, i: (b, h, i, 0)),
        pl.BlockSpec((None, hps, S, 128), lambda b, h, i: (b, h, 0, 0)),
        pl.BlockSpec((None, hps, n_blk, 128, t), lambda b, h, i: (b, h, 0, 0, 0)),
        pl.BlockSpec((None, hps, n_blk, 128), lambda b, h, i: (b, h, 0, 0)),
    ]
    return pl.pallas_call(
        functools.partial(_moba_kernel, n_top=n_top), grid=(B, H // hps, S // t), in_specs=in_specs,
        out_specs=pl.BlockSpec((None, t, hps * 64), lambda b, h, i: (b, i, h)),
        out_shape=jax.ShapeDtypeStruct((B, S, H * 64), BF16),
        scratch_shapes=_flash_scratch(hps, t, 2 * t),
        compiler_params=_params(("parallel", "parallel", "parallel")), name="moba_attention",
    )(bq, bkv, bkvt, kmean)


def _out_proj_kernel(*refs, n_in):
    a_refs = refs[:n_in]
    w_refs = refs[n_in:2 * n_in]
    x_ref, o_ref = refs[2 * n_in], refs[2 * n_in + 1]
    acc = x_ref[...]
    for a_ref, w_ref in zip(a_refs, w_refs):
        acc = acc + _dot(a_ref[...], w_ref[...])
    o_ref[...] = acc


def _out_proj(parts, weights, x2, tm=512):
    T = x2.shape[0]
    n_in = len(parts)
    in_specs = ([pl.BlockSpec((tm, p.shape[1]), lambda i: (i, 0)) for p in parts]
                + [pl.BlockSpec(w.shape, lambda i: (0, 0)) for w in weights]
                + [pl.BlockSpec((tm, D_MODEL), lambda i: (i, 0))])
    return pl.pallas_call(
        functools.partial(_out_proj_kernel, n_in=n_in), grid=(T // tm,), in_specs=in_specs,
        out_specs=pl.BlockSpec((tm, D_MODEL), lambda i: (i, 0)),
        out_shape=jax.ShapeDtypeStruct((T, D_MODEL), F32),
        compiler_params=_params(("parallel",)), name="out_proj",
    )(*parts, *weights, x2)


def _lane_group_norm(y, gain, width):
    outs = []
    for j in range(y.shape[1] // width):
        yc = y[:, j * width:(j + 1) * width]
        outs.append(_rms_rows(yc, gain))
    return jnp.concatenate(outs, axis=1)


def _mem_kv_kernel(m_ref, g_ref, w_ref, gk_ref, o_ref):
    mn = _rms_rows(m_ref[...], g_ref[...]).astype(BF16)
    y = _dot(mn, w_ref[...])
    hw = MEM_HEADS * MEM_HEAD_DIM
    k = _lane_group_norm(y[:, :hw], gk_ref[...], MEM_HEAD_DIM)
    o_ref[...] = jnp.concatenate([k, y[:, hw:]], axis=1).astype(BF16)


def _mem_kv(mem, g, w, gk):
    B, M, _ = mem.shape
    n = w.shape[1]
    return pl.pallas_call(
        _mem_kv_kernel, grid=(B,),
        in_specs=[pl.BlockSpec((None, M, D_MODEL), lambda b: (b, 0, 0)),
                  pl.BlockSpec(g.shape, lambda b: (0, 0)),
                  pl.BlockSpec(w.shape, lambda b: (0, 0)),
                  pl.BlockSpec(gk.shape, lambda b: (0, 0))],
        out_specs=pl.BlockSpec((None, M, n), lambda b: (b, 0, 0)),
        out_shape=jax.ShapeDtypeStruct((B, M, n), BF16),
        compiler_params=_params(("parallel",)), name="mem_kv",
    )(mem, g, w, gk)


def _mem_attn_kernel(x_ref, g_ref, wq_ref, gq_ref, kv_ref, wo_ref, o_ref):
    x = x_ref[...]
    xn = _rms_rows(x, g_ref[...]).astype(BF16)
    q = _lane_group_norm(_dot(xn, wq_ref[...]), gq_ref[...], MEM_HEAD_DIM).astype(BF16)
    hw = MEM_HEADS * MEM_HEAD_DIM
    scale = MEM_HEAD_DIM ** -0.5
    outs = []
    for h in range(MEM_HEADS):
        cols = slice(h * MEM_HEAD_DIM, (h + 1) * MEM_HEAD_DIM)
        k = kv_ref[:, cols]
        v = kv_ref[:, hw + h * MEM_HEAD_DIM:hw + (h + 1) * MEM_HEAD_DIM]
        s = _dot_nt(q[:, cols], k) * scale
        p = jnp.exp(s - jnp.max(s, axis=-1, keepdims=True))
        p = p / jnp.sum(p, axis=-1, keepdims=True)
        outs.append(_dot(p.astype(BF16), v))
    o = jnp.concatenate(outs, axis=1).astype(BF16)
    o_ref[...] = x + _dot(o, wo_ref[...])


def _mem_attn(x2, g, wq, gq, kv, wo, S, tm=512):
    T = x2.shape[0]
    nt = S // tm
    M, n = kv.shape[1], kv.shape[2]
    return pl.pallas_call(
        _mem_attn_kernel, grid=(T // tm,),
        in_specs=[pl.BlockSpec((tm, D_MODEL), lambda i: (i, 0)),
                  pl.BlockSpec(g.shape, lambda i: (0, 0)),
                  pl.BlockSpec(wq.shape, lambda i: (0, 0)),
                  pl.BlockSpec(gq.shape, lambda i: (0, 0)),
                  pl.BlockSpec((None, M, n), lambda i: (i // nt, 0, 0)),
                  pl.BlockSpec(wo.shape, lambda i: (0, 0))],
        out_specs=pl.BlockSpec((tm, D_MODEL), lambda i: (i, 0)),
        out_shape=jax.ShapeDtypeStruct((T, D_MODEL), F32),
        compiler_params=_params(("parallel",)), name="mem_attn",
    )(x2, g, wq, gq, kv, wo)


def _ffn_kernel(x_ref, g_ref, wg_ref, wu_ref, wo_ref, o_ref, xn_ref, acc_ref):
    j = pl.program_id(1)

    @pl.when(j == 0)
    def _():
        xn_ref[...] = _rms_rows(x_ref[...], g_ref[...]).astype(BF16)
        acc_ref[...] = x_ref[...]

    xn = xn_ref[...]
    gate = _dot(xn, wg_ref[...])
    up = _dot(xn, wu_ref[...])
    act = (gate * jax.nn.sigmoid(gate) * up).astype(BF16)
    acc_ref[...] += _dot(act, wo_ref[...])

    @pl.when(j == pl.num_programs(1) - 1)
    def _():
        o_ref[...] = acc_ref[...]


def _ffn(x2, g, wg, wu, wo, tm=512, n_split=2):
    T = x2.shape[0]
    tf = D_FF // n_split
    return pl.pallas_call(
        _ffn_kernel, grid=(T // tm, n_split),
        in_specs=[pl.BlockSpec((tm, D_MODEL), lambda i, j: (i, 0)),
                  pl.BlockSpec(g.shape, lambda i, j: (0, 0)),
                  pl.BlockSpec((D_MODEL, tf), lambda i, j: (0, j)),
                  pl.BlockSpec((D_MODEL, tf), lambda i, j: (0, j)),
                  pl.BlockSpec((tf, D_MODEL), lambda i, j: (j, 0))],
        out_specs=pl.BlockSpec((tm, D_MODEL), lambda i, j: (i, 0)),
        out_shape=jax.ShapeDtypeStruct((T, D_MODEL), F32),
        scratch_shapes=[pltpu.VMEM((tm, D_MODEL), BF16), pltpu.VMEM((tm, D_MODEL), F32)],
        compiler_params=_params(("parallel", "arbitrary")), name="ffn",
    )(x2, g, wg, wu, wo)


def _nsa_prep_kernel(x_ref, trig_ref, gmix_ref, w_ref, nq_ref, nk_ref,
                     gq_ref, gks_ref, gkw_ref,
                     qc_ref, qr_ref, kvs_ref, kvst_ref, kvw_ref, kvwt_ref, kc_ref, vc_ref, gtt_ref,
                     *, n_tiles):
    xn = _rms_rows(x_ref[...], gmix_ref[...]).astype(BF16)
    y = _dot(xn, w_ref[...])
    c64, s64 = trig_ref[:, :LANES], trig_ref[:, LANES:]
    lane = _lane_iota(c64.shape)
    lo64 = (lane % 64) < 8
    first64 = lane < 64
    c64k = jnp.where(first64, c64, 1.0)
    s64k = jnp.where(first64, s64, 0.0)
    nq = nq_ref[...]
    nk = nk_ref[...]

    def col(j):
        return y[:, j * LANES:(j + 1) * LANES]

    for j in range(8):
        qn = _head_norm(col(j), nq, gq_ref[...])
        qr = _rope(qn, c64, s64, lo64, 8)
        qc_ref[2 * j] = qn[:, :64].astype(BF16)
        qc_ref[2 * j + 1] = qn[:, 64:].astype(BF16)
        qr_ref[2 * j] = qr[:, :64].astype(BF16)
        qr_ref[2 * j + 1] = qr[:, 64:].astype(BF16)
    tile = pl.program_id(0) % n_tiles
    sel_blk = tile * (ATT_T // NSA_SEL_LEN) + lax.shift_right_logical(
        _row_iota(c64.shape), NSA_SEL_LEN.bit_length() - 1)
    blk_onehot = jnp.where(lane == HEAD_DIM + sel_blk, 1.0, 0.0)
    for g in range(NSA_GROUPS):
        kv = _kv_column(col(8 + g), nk, gks_ref[...], c64k, s64k, lo64, first64)
        kvs_ref[g] = jnp.where(first64, kv, blk_onehot).astype(BF16)
        kvst_ref[g] = kv.T.astype(BF16)
        kv = _kv_column(col(12 + g), nk, gkw_ref[...], c64k, s64k, lo64, first64)
        kvw_ref[g] = kv.astype(BF16)
        kvwt_ref[g] = kv.T.astype(BF16)
    kc_ref[...] = y[:, 16 * LANES:18 * LANES]
    vc_ref[...] = y[:, 18 * LANES:20 * LANES]
    gates_t = jax.nn.sigmoid(col(20)).T
    for g in range(NSA_GROUPS):
        gtt_ref[g] = gates_t[12 * g:12 * (g + 1), :]


def _nsa_prep(x2, trig, gmix, w, tabs, B, S):
    T = x2.shape[0]
    tm = ATT_T
    nt = S // tm
    nq, nk, gq, gks, gkw = tabs

    def full(a):
        return pl.BlockSpec(a.shape, lambda i: (0,) * a.ndim)

    def hm(width, heads):
        return pl.BlockSpec((None, heads, tm, width), lambda i: (i // nt, 0, i % nt, 0))

    def hmt(heads):
        return pl.BlockSpec((None, heads, None, 128, tm), lambda i: (i // nt, 0, i % nt, 0, 0))

    def tokm(width):
        return pl.BlockSpec((None, tm, width), lambda i: (i // nt, i % nt, 0))

    out_shape = (
        jax.ShapeDtypeStruct((B, 16, S, 64), BF16),
        jax.ShapeDtypeStruct((B, 16, S, 64), BF16),
        jax.ShapeDtypeStruct((B, 4, S, 128), BF16),
        jax.ShapeDtypeStruct((B, 4, nt, 128, tm), BF16),
        jax.ShapeDtypeStruct((B, 4, S, 128), BF16),
        jax.ShapeDtypeStruct((B, 4, nt, 128, tm), BF16),
        jax.ShapeDtypeStruct((B, S, 256), F32),
        jax.ShapeDtypeStruct((B, S, 256), F32),
        jax.ShapeDtypeStruct((B, 4, 12, S), F32),
    )
    out_specs = (hm(64, 16), hm(64, 16), hm(128, 4), hmt(4), hm(128, 4), hmt(4), tokm(256), tokm(256),
                 pl.BlockSpec((None, 4, 12, tm), lambda i: (i // nt, 0, 0, i % nt)))
    in_specs = [pl.BlockSpec((tm, D_MODEL), lambda i: (i, 0)),
                pl.BlockSpec((tm, 2 * LANES), lambda i: (i, 0)),
                full(gmix), full(w), full(nq), full(nk), full(gq), full(gks), full(gkw)]
    return pl.pallas_call(
        functools.partial(_nsa_prep_kernel, n_tiles=nt), grid=(T // tm,), in_specs=in_specs, out_specs=out_specs,
        out_shape=out_shape, compiler_params=_params(("parallel",)), name="nsa_prep",
    )(x2, trig, gmix, w, nq, nk, gq, gks, gkw)


def _compress_one(x16, pa, pb, w1a, w1b, w2):
    n16 = x16.shape[0]
    h_a = _dot((x16 + pa).astype(BF16), w1a)
    h_b = _dot((x16 + pb).astype(BF16), w1b)
    pre = h_a + pltpu.roll(h_b, n16 - 1, 0)
    act = pre * jax.nn.sigmoid(pre)
    return _dot(act.astype(BF16), w2)


def _compress_kernel(xk_ref, xv_ref, pk_ref, pv_ref, w1k_ref, w1v_ref, w2k_ref, w2v_ref, gk_ref,
                     o_ref, ot_ref):
    half = w1k_ref.shape[0] // 2
    k = _compress_one(xk_ref[...], pk_ref[0:1, :], pk_ref[1:2, :],
                      w1k_ref[:half, :], w1k_ref[half:, :], w2k_ref[...])
    k = _rms_rows(k, gk_ref[...])
    v = _compress_one(xv_ref[...], pv_ref[0:1, :], pv_ref[1:2, :],
                      w1v_ref[:half, :], w1v_ref[half:, :], w2v_ref[...])
    kv = jnp.concatenate([k, v], axis=1)
    o_ref[...] = kv.astype(BF16)
    ot_ref[...] = kv.T.astype(BF16)


def _compress(xk16, xv16, pk, pv, w1k, w1v, w2k, w2v, gk):
    B, G, n16, width = xk16.shape

    def full(a):
        return pl.BlockSpec(a.shape, lambda b, g: (0,) * a.ndim)

    xspec = pl.BlockSpec((None, None, n16, width), lambda b, g: (b, g, 0, 0))
    return pl.pallas_call(
        _compress_kernel, grid=(B, G),
        in_specs=[xspec, xspec, full(pk), full(pv), full(w1k), full(w1v), full(w2k), full(w2v), full(gk)],
        out_specs=(pl.BlockSpec((None, None, n16, 128), lambda b, g: (b, g, 0, 0)),
                   pl.BlockSpec((None, None, 128, n16), lambda b, g: (b, g, 0, 0))),
        out_shape=(jax.ShapeDtypeStruct((B, G, n16, 128), BF16),
                   jax.ShapeDtypeStruct((B, G, 128, n16), BF16)),
        compiler_params=_params(("parallel", "parallel")), name="nsa_compress",
    )(xk16, xv16, pk, pv, w1k, w1v, w2k, w2v, gk)


def _nsa_kernel(qc_ref, qr_ref, kvc_ref, kvct_ref, kvs_ref, kvst_ref, kvw_ref, kvwt_ref, gtt_ref,
                o_ref, sel_ref, *flash_refs, n_cmp, n_top):
    t = ATT_T
    HG = NSA_HEADS // NSA_GROUPS
    n_sel = sel_ref.shape[0]
    n16 = kvc_ref.shape[0]
    i = pl.program_id(2)
    t0 = i * t

    kvc = kvc_ref[...]
    kvct = kvct_ref[...]
    n_id = _row_iota((n16, t))
    q_id = t0 + _lane_iota((n16, t))
    visible = (n_id < n_cmp) & (n_id * NSA_CMP_STRIDE + (NSA_CMP_LEN - 1) <= q_id)
    bias_c = jnp.where(visible, 0.0, MASK_BIAS)
    p_sum = jnp.zeros((n16, t), F32)
    o_c = []
    probs, inv_ls, outs = _softmax_direct([_pad_q(qc_ref[j]) for j in range(HG)], kvc, [kvct], bias_c)
    for j in range(HG):
        p_sum = p_sum + probs[j] * inv_ls[j]
        o_c.append(outs[j] * inv_ls[j])

    b_id = _row_iota((n_sel, n16)) * NSA_SEL_LEN
    r_id = _lane_iota((n_sel, n16)) * NSA_CMP_STRIDE
    cover_t = ((r_id < b_id + NSA_SEL_LEN) & (r_id + NSA_CMP_LEN > b_id)
               & (_lane_iota((n_sel, n16)) < n_cmp))
    cover_t = jnp.where(cover_t, 1.0, 0.0).astype(BF16)
    p_hi, p_lo = _split_bf16(p_sum)
    imp = _dot(cover_t, p_hi) + _dot(cover_t, p_lo)
    blk = _row_iota((n_sel, t))
    cur = lax.shift_right_logical(t0 + _lane_iota((n_sel, t)), NSA_SEL_LEN.bit_length() - 1)
    forced = (blk == 0) | (blk == cur) | (blk == cur - 1)
    imp = jnp.where(forced, NSA_FORCE, imp)
    visible_blk = blk <= cur
    imp = jnp.where(visible_blk, imp, NEG_INF)
    n_larger = jnp.zeros((n_sel, t), F32)
    for m in range(n_sel):
        n_larger = n_larger + jnp.where(imp[m:m + 1, :] > imp, 1.0, 0.0)
    sel_fast = n_larger < n_top
    n_picked = jnp.sum(jnp.where(sel_fast & visible_blk, 1.0, 0.0), axis=0, keepdims=True)
    n_wanted = jnp.minimum(cur[0:1, :] + 1, n_top).astype(F32)
    sel_ref[...] = jnp.where(sel_fast, 0.0, MASK_BIAS)

    @pl.when(jnp.max(jnp.abs(n_picked - n_wanted)) > 0.0)
    def _():
        sel_ref[...] = (_rank_select_t(imp, None, n_top) - 1.0) * (-MASK_BIAS)

    qs = [_pad_q(qr_ref[j]) for j in range(HG)]

    sel_bias = sel_ref[...]
    qs_sel = [_bias_lanes(q, sel_bias) for q in qs]
    flash = _Flash(*flash_refs, t)
    flash.reset()

    def sel_operands(cc):
        c0 = 2 * cc
        k0 = pl.multiple_of(c0 * t, 2 * t)
        return c0, [kvs_ref[pl.ds(k0, 2 * t), :]] * HG, [[kvst_ref[c0], kvst_ref[c0 + 1]]] * HG

    def past_pair(cc, carry):
        _, kvs, kvts = sel_operands(cc)
        flash.update(qs_sel, kvs, kvts, [None] * HG)
        return carry

    n_pairs = (i + 2) // 2
    lax.fori_loop(0, n_pairs - 1, past_pair, 0)
    c0, kvs, kvts = sel_operands(n_pairs - 1)
    causal2 = jnp.where((c0 - i) * t + _row_iota((2 * t, t)) <= _lane_iota((2 * t, t)), 0.0, MASK_BIAS)
    flash.update(qs_sel, kvs, kvts, [causal2] * HG)
    o_s = [flash.result(j) for j in range(HG)]

    n_wc = NSA_WINDOW // t + 1
    cw = jnp.maximum(i - (n_wc - 1), 0)
    kw0 = pl.multiple_of(cw * t, t)
    dist = (i - cw) * t + _lane_iota((n_wc * t, t)) - _row_iota((n_wc * t, t))
    bias_w = jnp.where((dist >= 0) & (dist < NSA_WINDOW), 0.0, MASK_BIAS)
    kv = kvw_ref[pl.ds(kw0, n_wc * t), :]
    kvts = [kvwt_ref[cw + u] for u in range(n_wc)]
    _, inv_ls, outs = _softmax_direct(qs, kv, kvts, bias_w)
    o_w = [outs[j] * inv_ls[j] for j in range(HG)]

    gt = gtt_ref[...]
    heads = [gt[3 * j:3 * j + 1, :] * o_c[j] + gt[3 * j + 1:3 * j + 2, :] * o_s[j]
             + gt[3 * j + 2:3 * j + 3, :] * o_w[j] for j in range(HG)]
    _store_heads(o_ref, heads)


def _nsa_attention(qc, qr, kvc, kvct, kvs, kvst, kvw, kvwt, gates_t):
    B, H, S, _ = qc.shape
    G = NSA_GROUPS
    HG = H // G
    t = ATT_T
    nt = S // t
    n16 = kvc.shape[2]
    n_cmp = (S - NSA_CMP_LEN) // NSA_CMP_STRIDE + 1
    n_sel = S // NSA_SEL_LEN
    n_top = min(NSA_SEL_TOPK, n_sel)
    qspec = pl.BlockSpec((None, HG, t, 64), lambda b, g, i: (b, g, i, 0))
    kvspec = pl.BlockSpec((None, None, S, 128), lambda b, g, i: (b, g, 0, 0))
    kvtspec = pl.BlockSpec((None, None, nt, 128, t), lambda b, g, i: (b, g, 0, 0, 0))
    in_specs = [qspec, qspec,
                pl.BlockSpec((None, None, n16, 128), lambda b, g, i: (b, g, 0, 0)),
                pl.BlockSpec((None, None, 128, n16), lambda b, g, i: (b, g, 0, 0)),
                kvspec, kvtspec, kvspec, kvtspec,
                pl.BlockSpec((None, None, 12, t), lambda b, g, i: (b, g, 0, i))]
    return pl.pallas_call(
        functools.partial(_nsa_kernel, n_cmp=n_cmp, n_top=n_top), grid=(B, G, nt), in_specs=in_specs,
        out_specs=pl.BlockSpec((None, t, HG * 64), lambda b, g, i: (b, i, g)),
        out_shape=jax.ShapeDtypeStruct((B, S, H * 64), BF16),
        scratch_shapes=[pltpu.VMEM((n_sel, t), F32)] + _flash_scratch(HG, t, 2 * t),
        compiler_params=_params(("parallel", "parallel", "parallel")), name="nsa_attention",
    )(qc, qr, kvc, kvct, kvs, kvst, kvw, kvwt, gates_t)


def _rope_freq_row(period, rot):
    half = rot // 2
    inv_freq = ROPE_THETA ** (-(jnp.arange(half, dtype=F32) * 2.0 / rot))
    lane = jnp.arange(LANES) % period
    f = jnp.where(lane < rot, inv_freq[lane % half], 0.0)
    return f.reshape(1, LANES).astype(F32)


def _norm_matrices():
    r = jnp.arange(LANES)
    same = (r[:, None] // 64) == (r[None, :] // 64)
    nq = jnp.where(same, 1.0 / 64, 0.0).astype(BF16)
    nk = jnp.where(same & (r[:, None] < 64), 1.0 / 64, 0.0).astype(BF16)
    return nq, nk


def _q_gain(g):
    return (jnp.tile(g.astype(F32), 2) * Q_SCALE).reshape(1, LANES)


def _k_gain(g):
    return jnp.concatenate([g.astype(F32), jnp.ones((64,), F32)]).reshape(1, LANES)


def _interleave_kv(wk, wv, n_heads):
    d = wk.shape[0]
    wk = wk.reshape(d, n_heads, 64)
    wv = wv.reshape(d, n_heads, 64)
    return jnp.concatenate([wk, wv], axis=2).reshape(d, n_heads * 128)


def _split_cols(w, sizes):
    out, start = [], 0
    for n in sizes:
        out.append(w[:, start:start + n])
        start += n
    return out


def _mixer_layer0(x2, trig, B, S, gmix, w_in, w_out, a_q_norm, a_k_norm, b_q_norm, b_k_norm):
    sizes = (512, 64, 64, 256, 32, 8, 512, 512, 512)
    waq, wak, wav, wiq, wik, wiw, wbq, wbk, wbv = _split_cols(w_in, sizes)
    pad = jnp.zeros((D_MODEL, LANES - 40), w_in.dtype)
    w = jnp.concatenate([waq, wbq, _interleave_kv(wbk, wbv, 8), wak, wav, wiq, wik, wiw, pad],
                        axis=1).astype(BF16)
    nq, nk = _norm_matrices()
    tabs = (nq, nk, _q_gain(a_q_norm), _q_gain(b_q_norm), _k_gain(a_k_norm), _k_gain(b_k_norm))
    aq, bq, bkv, bkvt, akv, akvt, iq, ik, iwt, km = _ab_prep(x2, trig, gmix, w, tabs, B, S)
    n_blk = S // MOBA_BLOCK
    kmean = km.reshape(B, n_blk, 8, 128).transpose(0, 2, 1, 3)
    o_a = _dsa_attention(iq, iwt, ik, aq, akv, akvt).reshape(B * S, 512)
    o_b = _moba_attention(bq, bkv, bkvt, kmean).reshape(B * S, 512)
    w_out = w_out.astype(BF16)
    return _out_proj([o_a, o_b], [w_out[:512], w_out[512:]], x2)


def _mixer_layer1(x2, trig, B, S, gmix, w_in, w_out, q_norm, kcmp_norm, ksel_norm, kwin_norm,
                  pos_k, pos_v, w1_k, w2_k, w1_v, w2_v):
    G = NSA_GROUPS
    sizes = (1024,) + (256,) * 6 + (48,)
    wq, wkc, wvc, wks, wvs, wkw, wvw, wgt = _split_cols(w_in, sizes)
    pad = jnp.zeros((D_MODEL, LANES - 48), w_in.dtype)
    w = jnp.concatenate([wq, _interleave_kv(wks, wvs, G), _interleave_kv(wkw, wvw, G),
                         wkc, wvc, wgt, pad], axis=1).astype(BF16)
    nq, nk = _norm_matrices()
    tabs = (nq, nk, _q_gain(q_norm), _k_gain(ksel_norm), _k_gain(kwin_norm))
    qc, qr, kvs, kvst, kvw, kvwt, kc_raw, vc_raw, gates_t = _nsa_prep(x2, trig, gmix, w, tabs, B, S)

    n16 = S // NSA_CMP_STRIDE

    def blocks16(t):
        return (t.reshape(B, n16, NSA_CMP_STRIDE, G, HEAD_DIM).transpose(0, 3, 1, 2, 4)
                .reshape(B, G, n16, NSA_CMP_STRIDE * HEAD_DIM))

    def pos_rows(p):
        return p.astype(F32).reshape(2, NSA_CMP_STRIDE * HEAD_DIM)

    kvc, kvct = _compress(blocks16(kc_raw), blocks16(vc_raw), pos_rows(pos_k), pos_rows(pos_v),
                          w1_k.astype(BF16), w1_v.astype(BF16), w2_k.astype(BF16), w2_v.astype(BF16),
                          kcmp_norm.astype(F32).reshape(1, HEAD_DIM))
    o = _nsa_attention(qc, qr, kvc, kvct, kvs, kvst, kvw, kvwt, gates_t)
    return _out_proj([o.reshape(B * S, NSA_HEADS * HEAD_DIM)], [w_out.astype(BF16)], x2)


def _mem_and_ffn(x2, mem, S, g_mem, g_src, w_q, w_kv, w_o, q_norm, k_norm, g_ffn, ffn_w_in, ffn_w_out):
    row = lambda v: v.astype(F32).reshape(1, -1)
    kv = _mem_kv(mem, row(g_src), w_kv.astype(BF16), row(k_norm))
    x2 = _mem_attn(x2, row(g_mem), w_q.astype(BF16), row(q_norm), kv, w_o.astype(BF16), S)
    wg = ffn_w_in[:, :D_FF].astype(BF16)
    wu = ffn_w_in[:, D_FF:].astype(BF16)
    return _ffn(x2, row(g_ffn), wg, wu, ffn_w_out.astype(BF16))


def kernel(x, mem, positions, norm_mix, norm_mem, norm_mem_src, norm_ffn, ab_w_in, ab_w_out, dsa_q_norm, dsa_k_norm, moba_q_norm, moba_k_norm, nsa_w_in, nsa_w_out, nsa_q_norm, nsa_kcmp_norm, nsa_ksel_norm, nsa_kwin_norm, nsa_cmp_pos_k, nsa_cmp_pos_v, nsa_cmp_w1_k, nsa_cmp_w2_k, nsa_cmp_w1_v, nsa_cmp_w2_v, mem_w_q, mem_w_kv, mem_w_o, mem_q_norm, mem_k_norm, ffn_w_in, ffn_w_out):
    B, S, D = x.shape
    depth = norm_mix.shape[0]
    x2 = x.reshape(B * S, D)
    trig = _rope_trig(positions.astype(F32).reshape(B * S, 1), _rope_freq_row(64, 16), _rope_freq_row(32, 8))
    row = lambda v: v.astype(F32).reshape(1, -1)
    for i in range(depth):
        j = i // 2
        if i % 2 == 0:
            x2 = _mixer_layer0(x2, trig, B, S, row(norm_mix[i]), ab_w_in[j], ab_w_out[j],
                               dsa_q_norm[j], dsa_k_norm[j], moba_q_norm[j], moba_k_norm[j])
        else:
            x2 = _mixer_layer1(x2, trig, B, S, row(norm_mix[i]), nsa_w_in[j], nsa_w_out[j],
                               nsa_q_norm[j], nsa_kcmp_norm[j], nsa_ksel_norm[j], nsa_kwin_norm[j],
                               nsa_cmp_pos_k[j], nsa_cmp_pos_v[j], nsa_cmp_w1_k[j], nsa_cmp_w2_k[j],
                               nsa_cmp_w1_v[j], nsa_cmp_w2_v[j])
        x2 = _mem_and_ffn(x2, mem, S, norm_mem[i], norm_mem_src[i], mem_w_q[i], mem_w_kv[i], mem_w_o[i],
                          mem_q_norm[i], mem_k_norm[i], norm_ffn[i], ffn_w_in[i], ffn_w_out[i])
    return x2.reshape(B, S, D)
```

```python
import functools
import math

import jax
import jax.numpy as jnp
from jax import lax
from jax.experimental import pallas as pl
from jax.experimental.pallas import tpu as pltpu

F32 = jnp.float32
BF16 = jnp.bfloat16
I32 = jnp.int32
I16 = jnp.int16

D_MODEL = 1024
N_MEM = 256
HEAD_DIM = 64
ROPE_THETA = 500000.0
RMS_EPS = 1e-6
NEG_INF = -1e30
TINY = 1e-20

DSA_HEADS = 8
DSA_IDX_HEADS = 8
DSA_IDX_DIM = 32
DSA_TOPK = 256
MOBA_HEADS = 8
MOBA_BLOCK = 256
MOBA_TOPK = 3
NSA_HEADS = 16
NSA_GROUPS = 4
NSA_CMP_LEN = 32
NSA_CMP_STRIDE = 16
NSA_SEL_LEN = 64
NSA_SEL_TOPK = 16
NSA_WINDOW = 512
NSA_FORCE = 1e4
MEM_HEADS = 4
MEM_HEAD_DIM = 128
D_FF = ((8 * D_MODEL + 3 * 256 - 1) // (3 * 256)) * 256

LANES = 128
SUBLANES = 8
INT_MIN = -(2 ** 31)
VMEM_LIMIT = 56 * 1024 * 1024

PROJ_GROUP = 4
ATT_T = 256
MASK_BIAS = -1e30
M_FLOOR = -1e29
LOG2E = math.log2(math.e)
Q_SCALE = HEAD_DIM ** -0.5 * LOG2E

NT_DIMS = (((1,), (1,)), ((), ()))


def _dot(a, b):
    return jnp.dot(a, b, preferred_element_type=F32)


def _dot_nt(a, b):
    return lax.dot_general(a, b, NT_DIMS, preferred_element_type=F32)


def _split_bf16(a):
    hi = a.astype(BF16)
    return hi, (a - hi.astype(F32)).astype(BF16)


def _split_dot(a, b):
    hi, lo = _split_bf16(a)
    return _dot(hi, b) + _dot(lo, b)


def _rms_rows(x, gain):
    ms = jnp.mean(x * x, axis=-1, keepdims=True)
    return x * lax.rsqrt(ms + RMS_EPS) * gain


def _params(sem):
    return pltpu.CompilerParams(dimension_semantics=sem, vmem_limit_bytes=VMEM_LIMIT)


def _head_norm(y, norm_m, gain):
    ms = _split_dot(y * y, norm_m)
    return y * lax.rsqrt(ms + RMS_EPS) * gain


def _rope(y, c, s, lo_mask, half):
    sw = jnp.where(lo_mask, pltpu.roll(y, LANES - half, 1), pltpu.roll(y, half, 1))
    return y * c + sw * s


def _lane_iota(shape):
    return lax.broadcasted_iota(I32, shape, 1)


def _row_iota(shape):
    return lax.broadcasted_iota(I32, shape, 0)


def _rope_tables(pos, ftab, period, half):
    ang = pos * ftab
    lane = _lane_iota(ang.shape) % period
    c = jnp.cos(ang)
    s = jnp.sin(ang) * jnp.where(lane < half, -1.0, 1.0)
    return c, s


class _ColumnProjector:
    def __init__(self, xn, w_ref):
        self.xn, self.w_ref, self.groups = xn, w_ref, {}

    def __call__(self, j):
        g, u = divmod(j, PROJ_GROUP)
        if g not in self.groups:
            width = PROJ_GROUP * LANES
            lo = g * width
            hi = min(lo + width, self.w_ref.shape[1])
            self.groups[g] = _dot(self.xn, self.w_ref[:, lo:hi])
        return self.groups[g][:, u * LANES:(u + 1) * LANES]


def _kv_column(yc, nk, gain, c64k, s64k, lo64, first64):
    kn = jnp.where(first64, _head_norm(yc, nk, gain), yc)
    return _rope(kn, c64k, s64k, lo64, 8)


def _rope_trig_kernel(pos_ref, f64_ref, f32_ref, o_ref):
    pos = pos_ref[...]
    c64, s64 = _rope_tables(pos, f64_ref[...], 64, 8)
    c32, s32 = _rope_tables(pos, f32_ref[...], 32, 4)
    o_ref[...] = jnp.concatenate([c64, s64, c32, s32], axis=1)


def _rope_trig(pos2, f64, f32t, tm=1024):
    T = pos2.shape[0]
    return pl.pallas_call(
        _rope_trig_kernel, grid=(T // tm,),
        in_specs=[pl.BlockSpec((tm, 1), lambda i: (i, 0)),
                  pl.BlockSpec(f64.shape, lambda i: (0, 0)), pl.BlockSpec(f32t.shape, lambda i: (0, 0))],
        out_specs=pl.BlockSpec((tm, 4 * LANES), lambda i: (i, 0)),
        out_shape=jax.ShapeDtypeStruct((T, 4 * LANES), F32),
        compiler_params=_params(("parallel",)), name="rope_trig",
    )(pos2, f64, f32t)


def _ab_prep_kernel(x_ref, trig_ref, gmix_ref, w_ref, nq_ref, nk_ref,
                    gaq_ref, gbq_ref, gak_ref, gbk_ref,
                    aq_ref, bq_ref, bkv_ref, bkvt_ref, akv_ref, akvt_ref, iq_ref, ik_ref, iwt_ref, km_ref,
                    *, n_tiles):
    xn = _rms_rows(x_ref[...], gmix_ref[...]).astype(BF16)
    c64, s64, c32, s32 = [trig_ref[:, j * LANES:(j + 1) * LANES] for j in range(4)]
    lane = _lane_iota(c64.shape)
    lo64 = (lane % 64) < 8
    lo32 = (lane % 32) < 4
    first64 = lane < 64
    c64k = jnp.where(first64, c64, 1.0)
    s64k = jnp.where(first64, s64, 0.0)
    first32 = lane < 32
    c32k = jnp.where(first32, c32, 1.0)
    s32k = jnp.where(first32, s32, 0.0)
    nq = nq_ref[...]
    nk = nk_ref[...]

    col = _ColumnProjector(xn, w_ref)

    for j in range(4):
        q = _rope(_head_norm(col(j), nq, gaq_ref[...]), c64, s64, lo64, 8)
        aq_ref[2 * j] = q[:, :64].astype(BF16)
        aq_ref[2 * j + 1] = q[:, 64:].astype(BF16)
    for j in range(4):
        q = _rope(_head_norm(col(4 + j), nq, gbq_ref[...]), c64, s64, lo64, 8)
        bq_ref[2 * j] = q[:, :64].astype(BF16)
        bq_ref[2 * j + 1] = q[:, 64:].astype(BF16)
    blk_onehot = jnp.where(lane == HEAD_DIM + pl.program_id(0) % n_tiles, 1.0, 0.0)
    for h in range(8):
        kv = _kv_column(col(8 + h), nk, gbk_ref[...], c64k, s64k, lo64, first64)
        bkv_ref[h] = jnp.where(first64, kv, blk_onehot).astype(BF16)
        bkvt_ref[h] = kv.T.astype(BF16)
        km_ref[h:h + 1, :] = jnp.mean(kv, axis=0, keepdims=True)
    kv = _kv_column(col(16), nk, gak_ref[...], c64k, s64k, lo64, first64)
    akv_ref[...] = kv.astype(BF16)
    akvt_ref[...] = kv.T.astype(BF16)
    for j in range(2):
        q = _rope(col(17 + j), c32, s32, lo32, 4)
        for u in range(4):
            iq_ref[4 * j + u] = q[:, 32 * u:32 * (u + 1)].astype(BF16)
    yc = col(19)
    ik_ref[...] = _rope(yc, c32k, s32k, lo32, 4)[:, :32].astype(BF16)
    iwt_ref[...] = yc.T[32:40, :]


def _ab_prep(x2, trig, gmix, w, tabs, B, S):
    T = x2.shape[0]
    tm = ATT_T
    nt = S // tm
    n_cols = w.shape[1]
    nq, nk, gaq, gbq, gak, gbk = tabs

    def full(a):
        return pl.BlockSpec(a.shape, lambda i: (0,) * a.ndim)

    def hm(width, heads=8):
        return pl.BlockSpec((None, heads, tm, width), lambda i: (i // nt, 0, i % nt, 0))

    def tokm(width):
        return pl.BlockSpec((None, tm, width), lambda i: (i // nt, i % nt, 0))

    out_shape = (
        jax.ShapeDtypeStruct((B, 8, S, 64), BF16),
        jax.ShapeDtypeStruct((B, 8, S, 64), BF16),
        jax.ShapeDtypeStruct((B, 8, S, 128), BF16),
        jax.ShapeDtypeStruct((B, 8, nt, 128, tm), BF16),
        jax.ShapeDtypeStruct((B, S, 128), BF16),
        jax.ShapeDtypeStruct((B, nt, 128, tm), BF16),
        jax.ShapeDtypeStruct((B, 8, S, 32), BF16),
        jax.ShapeDtypeStruct((B, S, 32), BF16),
        jax.ShapeDtypeStruct((B, 8, S), F32),
        jax.ShapeDtypeStruct((T // tm, 8, 128), F32),
    )
    out_specs = (hm(64), hm(64), hm(128),
                 pl.BlockSpec((None, 8, None, 128, tm), lambda i: (i // nt, 0, i % nt, 0, 0)),
                 tokm(128),
                 pl.BlockSpec((None, None, 128, tm), lambda i: (i // nt, i % nt, 0, 0)),
                 hm(32), tokm(32),
                 pl.BlockSpec((None, 8, tm), lambda i: (i // nt, 0, i % nt)),
                 pl.BlockSpec((None, 8, 128), lambda i: (i, 0, 0)))
    in_specs = [pl.BlockSpec((tm, D_MODEL), lambda i: (i, 0)),
                pl.BlockSpec((tm, 4 * LANES), lambda i: (i, 0)),
                full(gmix), pl.BlockSpec((D_MODEL, n_cols), lambda i: (0, 0)),
                full(nq), full(nk), full(gaq), full(gbq), full(gak), full(gbk)]
    return pl.pallas_call(
        functools.partial(_ab_prep_kernel, n_tiles=nt), grid=(T // tm,), in_specs=in_specs, out_specs=out_specs,
        out_shape=out_shape, compiler_params=_params(("parallel",)), name="ab_prep",
    )(x2, trig, gmix, w, nq, nk, gaq, gbq, gak, gbk)


def _pad_q(q):
    return jnp.concatenate([q, jnp.zeros_like(q)], axis=1)


def _bias_lanes(q, rows):
    n, tq = rows.shape
    parts = [jnp.zeros((HEAD_DIM, tq), F32), rows]
    if n < HEAD_DIM:
        parts.append(jnp.zeros((HEAD_DIM - n, tq), F32))
    lanes = jnp.concatenate(parts, axis=0).T.astype(BF16)
    return jnp.where(_lane_iota(q.shape) < HEAD_DIM, q, lanes)


class _Flash:
    def __init__(self, m_ref, l_ref, acc_ref, s_ref, p_ref, tq):
        self.m_ref, self.l_ref, self.acc_ref, self.tq = m_ref, l_ref, acc_ref, tq
        self.s_ref, self.p_ref = s_ref, p_ref

    def reset(self):
        self.m_ref[...] = jnp.full(self.m_ref.shape, M_FLOOR, F32)
        self.l_ref[...] = jnp.zeros(self.l_ref.shape, F32)
        self.acc_ref[...] = jnp.zeros(self.acc_ref.shape, F32)

    def update(self, qs, kvs, kvts, biases):
        n = len(qs)
        tq = self.tq
        col_max = []
        for i in range(n):
            s = _dot_nt(kvs[i], qs[i])
            if biases[i] is not None:
                s = s + biases[i]
            self.s_ref[i] = s
            col_max.append(jnp.max(s, axis=0, keepdims=True))
        alphas = []
        for i in range(n):
            cols = slice(i * tq, (i + 1) * tq)
            m = self.m_ref[:, cols]
            m_new = jnp.maximum(m, col_max[i])
            p = jnp.exp2(self.s_ref[i] - m_new)
            alpha = jnp.exp2(m - m_new)
            self.m_ref[:, cols] = m_new
            self.l_ref[:, cols] = alpha * self.l_ref[:, cols] + p.reshape(-1, SUBLANES, tq).sum(axis=0)
            self.p_ref[i] = p.astype(BF16)
            alphas.append(alpha)
        for i in range(n):
            cols = slice(i * tq, (i + 1) * tq)
            pv, r0 = None, 0
            for kvt in kvts[i]:
                part = _dot(kvt, self.p_ref[i, r0:r0 + kvt.shape[1], :])
                pv = part if pv is None else pv + part
                r0 += kvt.shape[1]
            self.acc_ref[:, cols] = alphas[i] * self.acc_ref[:, cols] + pv

    def result(self, slot):
        cols = slice(slot * self.tq, (slot + 1) * self.tq)
        l = jnp.sum(self.l_ref[:, cols], axis=0, keepdims=True)
        return self.acc_ref[:, cols] / jnp.maximum(l, TINY)


def _flash_scratch(n_slots, tq, kc):
    return [pltpu.VMEM((1, n_slots * tq), F32), pltpu.VMEM((SUBLANES, n_slots * tq), F32),
            pltpu.VMEM((LANES, n_slots * tq), F32),
            pltpu.VMEM((n_slots, kc, tq), F32), pltpu.VMEM((n_slots, kc, tq), BF16)]


def _softmax_direct(qs, kv, kvts, bias):
    scores = [_dot_nt(kv, q) for q in qs]
    probs, inv_ls = [], []
    for s in scores:
        s = s + bias
        m = jnp.maximum(jnp.max(s, axis=0, keepdims=True), M_FLOOR)
        p = jnp.exp2(s - m)
        inv_ls.append(1.0 / jnp.maximum(jnp.sum(p, axis=0, keepdims=True), TINY))
        probs.append(p)
    outs = []
    for p in probs:
        pb = p.astype(BF16)
        o, r0 = None, 0
        for kvt in kvts:
            part = _dot(kvt, pb[r0:r0 + kvt.shape[1]])
            o = part if o is None else o + part
            r0 += kvt.shape[1]
        outs.append(o)
    return probs, inv_ls, outs


def _causal_bias(t):
    return jnp.where(_row_iota((t, t)) <= _lane_iota((t, t)), 0.0, MASK_BIAS)


def _store_heads(o_ref, heads_t):
    tq = heads_t[0].shape[1]
    lane = _lane_iota((tq, LANES))
    for u in range(len(heads_t) // 2):
        even = pltpu.roll(heads_t[2 * u].T, 64, 1)
        odd = heads_t[2 * u + 1].T
        o_ref[:, u * LANES:(u + 1) * LANES] = jnp.where(lane < 64, even, odd).astype(o_ref.dtype)


def _rank_select_t(v, n_valid, n_top):
    n = v.shape[0]
    row = _row_iota(v.shape)
    rank = jnp.zeros(v.shape, F32)
    for m in range(n):
        vm = v[m:m + 1, :]
        ahead = (vm > v) | ((vm == v) & (m < row))
        if n_valid is not None:
            ahead = ahead & (m < n_valid)
        rank = rank + jnp.where(ahead, 1.0, 0.0)
    sel = rank < n_top
    if n_valid is not None:
        sel = sel & (row < n_valid)
    return jnp.where(sel, 1.0, 0.0)


def _dsa_kernel(iq_ref, iwt_ref, ik_ref, aq_ref, akv_ref, akvt_ref, o_ref,
                sk_ref, half_ref, bias_ref, xcut_ref, *flash_refs, k_top, index_bits):
    t = ATT_T
    i = pl.program_id(1)
    n_ch = i + 1
    kio = _row_iota((t, t))
    qio = _lane_iota((t, t))

    def causal(c):
        return (c - i) * t + kio <= qio

    def score_chunk(c, carry):
        k0 = pl.multiple_of(c * t, t)
        ikc = ik_ref[pl.ds(k0, t), :]
        sc = jnp.zeros((t, t), F32)
        for h in range(DSA_IDX_HEADS):
            logit = _dot_nt(ikc, iq_ref[h])
            sc = sc + iwt_ref[h:h + 1, :] * jnp.maximum(logit, 0.0)
        sc = jnp.where(sc == 0.0, 0.0, sc)
        bits = pltpu.bitcast(sc, I32)
        key = bits ^ ((bits >> 31) & 0x7FFFFFFF)
        key = jnp.where(causal(c), key, INT_MIN)
        sk_ref[c] = key
        half_ref[c] = (key >> 16).astype(I16)
        return carry

    lax.fori_loop(0, n_ch, score_chunk, 0)

    def count(pred):
        def body(c, acc8):
            ind = jnp.where(pred(sk_ref[c], c), 1.0, 0.0)
            return acc8 + ind.reshape(-1, SUBLANES, t).sum(axis=0)
        acc8 = lax.fori_loop(0, n_ch, body, jnp.zeros((SUBLANES, t), F32))
        return jnp.sum(acc8, axis=0, keepdims=True)

    def count_half(cand):
        rows = 2 * SUBLANES

        def body(c, acc):
            ind = jnp.where(half_ref[c] >= cand, jnp.bfloat16(1), jnp.bfloat16(0))
            parts = [ind[rows * j:rows * (j + 1), :] for j in range(t // rows)]
            while len(parts) > 1:
                parts = [parts[2 * j] + parts[2 * j + 1] for j in range(len(parts) // 2)]
            return acc + parts[0].astype(F32)
        acc = lax.fori_loop(0, n_ch, body, jnp.zeros((rows, t), F32))
        return jnp.sum(acc, axis=0, keepdims=True)

    def half_search():
        def bit_step(b, v):
            cand = v + lax.shift_left(jnp.int32(1), 15 - b)
            return jnp.where(count_half(cand.astype(I16)) >= k_top, cand, v)
        return lax.fori_loop(0, 16, bit_step, jnp.full((1, t), -(2 ** 15), I32))

    thr_hi = half_search()

    def low_half_chunk(c, carry):
        key = sk_ref[c]
        hi = key >> 16
        lo = (key & 0xFFFF) - 2 ** 15
        half_ref[c] = jnp.where(hi > thr_hi, 2 ** 15 - 1, jnp.where(hi < thr_hi, -(2 ** 15), lo)).astype(I16)
        return carry

    lax.fori_loop(0, n_ch, low_half_chunk, 0)
    thr = lax.shift_left(thr_hi, 16) + (half_search() + 2 ** 15)

    need = k_top - count(lambda blk, c: blk > thr)
    n_ge = count(lambda blk, c: blk >= thr)
    xcut_ref[...] = jnp.full((1, t), 2 ** 30, I32)

    @pl.when(jnp.max(n_ge) > k_top)
    def _():
        def x_step(b, x):
            cand = x + lax.shift_left(jnp.int32(1), index_bits - 1 - b)
            ties_below = count(lambda blk, c: (blk == thr) & (c * t + kio < cand))
            return jnp.where(ties_below <= need, cand, x)
        xcut_ref[...] = lax.fori_loop(0, index_bits, x_step, jnp.zeros((1, t), I32))

    xcut = xcut_ref[...]

    n_pairs = (n_ch + 1) // 2

    def bias_chunk(c, carry):
        blk = sk_ref[jnp.minimum(c, i)]
        keep = (blk > thr) | ((blk == thr) & (c * t + kio < xcut))
        bias_ref[c] = jnp.where(keep & causal(c), 0.0, MASK_BIAS)
        return carry

    lax.fori_loop(0, 2 * n_pairs, bias_chunk, 0)

    flash = _Flash(*flash_refs, t)
    flash.reset()
    qs = [_pad_q(aq_ref[h]) for h in range(DSA_HEADS)]

    n = DSA_HEADS

    def att_pair(cc, carry):
        c0 = 2 * cc
        k0 = pl.multiple_of(c0 * t, 2 * t)
        kv = akv_ref[pl.ds(k0, 2 * t), :]
        bias = jnp.concatenate([bias_ref[c0], bias_ref[c0 + 1]], axis=0)
        flash.update(qs, [kv] * n, [[akvt_ref[c0], akvt_ref[c0 + 1]]] * n, [bias] * n)
        return carry

    lax.fori_loop(0, n_pairs, att_pair, 0)
    _store_heads(o_ref, [flash.result(h) for h in range(DSA_HEADS)])


def _dsa_attention(iq, iwt, ik, aq, akv, akvt):
    B, _, S, _ = aq.shape
    t = ATT_T
    nt = S // t
    k_top = min(DSA_TOPK, S // 4)
    in_specs = [
        pl.BlockSpec((None, 8, t, 32), lambda b, i: (b, 0, i, 0)),
        pl.BlockSpec((None, 8, t), lambda b, i: (b, 0, i)),
        pl.BlockSpec((None, S, 32), lambda b, i: (b, 0, 0)),
        pl.BlockSpec((None, 8, t, 64), lambda b, i: (b, 0, i, 0)),
        pl.BlockSpec((None, S, 128), lambda b, i: (b, 0, 0)),
        pl.BlockSpec((None, nt, 128, t), lambda b, i: (b, 0, 0, 0)),
    ]
    return pl.pallas_call(
        functools.partial(_dsa_kernel, k_top=k_top, index_bits=S.bit_length()),
        grid=(B, nt), in_specs=in_specs,
        out_specs=pl.BlockSpec((None, t, 512), lambda b, i: (b, i, 0)),
        out_shape=jax.ShapeDtypeStruct((B, S, 512), BF16),
        scratch_shapes=[pltpu.VMEM((nt, t, t), I32), pltpu.VMEM((nt, t, t), I16), pltpu.VMEM((nt, t, t), F32),
                        pltpu.VMEM((1, t), I32)] + _flash_scratch(DSA_HEADS, t, 2 * t),
        compiler_params=_params(("parallel", "parallel")), name="dsa_attention",
    )(iq, iwt, ik, aq, akv, akvt)


MOBA_HPS = 4


def _moba_kernel(q_ref, kv_ref, kvt_ref, km_ref, o_ref, *flash_refs, n_top):
    t = ATT_T
    own = pl.program_id(2)
    causal = _causal_bias(t)
    flash = _Flash(*flash_refs, t)
    flash.reset()
    qs = []
    for hh in range(MOBA_HPS):
        q = _pad_q(q_ref[hh])
        km_hi, km_lo = _split_bf16(km_ref[hh])
        gate = _dot_nt(km_hi, q) + _dot_nt(km_lo, q)
        keep = _rank_select_t(gate, own, n_top)
        keep = jnp.where(_row_iota(keep.shape) == own, 1.0, keep)
        qs.append(_bias_lanes(q, (keep - 1.0) * (-MASK_BIAS)))

    def operands(cc):
        n0 = 2 * cc
        k0 = pl.multiple_of(n0 * t, 2 * t)
        heads = range(MOBA_HPS)
        return (n0, [kv_ref[hh, pl.ds(k0, 2 * t), :] for hh in heads],
                [[kvt_ref[hh, n0], kvt_ref[hh, n0 + 1]] for hh in heads])

    def past_pair(cc, carry):
        _, kvs, kvts = operands(cc)
        flash.update(qs, kvs, kvts, [None] * MOBA_HPS)
        return carry

    n_pairs = (own + 2) // 2
    lax.fori_loop(0, n_pairs - 1, past_pair, 0)
    n0, kvs, kvts = operands(n_pairs - 1)
    own_bias = jnp.concatenate([jnp.where(n0 == own, causal, 0.0), jnp.where(n0 + 1 == own, causal, 0.0)],
                               axis=0)
    flash.update(qs, kvs, kvts, [own_bias] * MOBA_HPS)
    _store_heads(o_ref, [flash.result(hh) for hh in range(MOBA_HPS)])


def _moba_attention(bq, bkv, bkvt, kmean):
    B, H, S, _ = bq.shape
    t = ATT_T
    hps = MOBA_HPS
    n_blk = S // MOBA_BLOCK
    assert t == MOBA_BLOCK and n_blk % 2 == 0 and H % hps == 0
    n_top = max(1, min(MOBA_TOPK, n_blk - 1))
    in_specs = [
        pl.BlockSpec((None, hps, t, 64), lambda b, h, i: (b, h, i, 0)),
        pl.BlockSpec((None, hps, S, 128), lambda b, h, i: (b, h, 0, 0)),
        pl.BlockSpec((None, hps, n_blk, 128, t), lambda b, h, i: (b, h, 0, 0, 0)),
        pl.BlockSpec((None, hps, n_blk, 128), lambda b, h, i: (b, h, 0, 0)),
    ]
    return pl.pallas_call(
        functools.partial(_moba_kernel, n_top=n_top), grid=(B, H // hps, S // t), in_specs=in_specs,
        out_specs=pl.BlockSpec((None, t, hps * 64), lambda b, h, i: (b, i, h)),
        out_shape=jax.ShapeDtypeStruct((B, S, H * 64), BF16),
        scratch_shapes=_flash_scratch(hps, t, 2 * t),
        compiler_params=_params(("parallel", "parallel", "parallel")), name="moba_attention",
    )(bq, bkv, bkvt, kmean)


def _lane_group_norm(y, gain, width):
    outs = []
    for j in range(y.shape[1] // width):
        yc = y[:, j * width:(j + 1) * width]
        outs.append(_rms_rows(yc, gain))
    return jnp.concatenate(outs, axis=1)


def _mem_kv_kernel(m_ref, g_ref, w_ref, gk_ref, o_ref):
    mn = _rms_rows(m_ref[...], g_ref[...]).astype(BF16)
    y = _dot(mn, w_ref[...])
    hw = MEM_HEADS * MEM_HEAD_DIM
    k = _lane_group_norm(y[:, :hw], gk_ref[...], MEM_HEAD_DIM)
    o_ref[...] = jnp.concatenate([k, y[:, hw:]], axis=1).astype(BF16)


def _mem_kv(mem, g, w, gk):
    B, M, _ = mem.shape
    n = w.shape[1]
    return pl.pallas_call(
        _mem_kv_kernel, grid=(B,),
        in_specs=[pl.BlockSpec((None, M, D_MODEL), lambda b: (b, 0, 0)),
                  pl.BlockSpec(g.shape, lambda b: (0, 0)),
                  pl.BlockSpec(w.shape, lambda b: (0, 0)),
                  pl.BlockSpec(gk.shape, lambda b: (0, 0))],
        out_specs=pl.BlockSpec((None, M, n), lambda b: (b, 0, 0)),
        out_shape=jax.ShapeDtypeStruct((B, M, n), BF16),
        compiler_params=_params(("parallel",)), name="mem_kv",
    )(mem, g, w, gk)


def _mem_attend(x, g_ref, wq_ref, gq_ref, kv_ref, wo_ref):
    xn = _rms_rows(x, g_ref[...]).astype(BF16)
    q = _lane_group_norm(_dot(xn, wq_ref[...]), gq_ref[...], MEM_HEAD_DIM).astype(BF16)
    hw = MEM_HEADS * MEM_HEAD_DIM
    scale = MEM_HEAD_DIM ** -0.5
    outs = []
    for h in range(MEM_HEADS):
        cols = slice(h * MEM_HEAD_DIM, (h + 1) * MEM_HEAD_DIM)
        k = kv_ref[:, cols]
        v = kv_ref[:, hw + h * MEM_HEAD_DIM:hw + (h + 1) * MEM_HEAD_DIM]
        s = _dot_nt(q[:, cols], k) * scale
        p = jnp.exp(s - jnp.max(s, axis=-1, keepdims=True))
        p = p / jnp.sum(p, axis=-1, keepdims=True)
        outs.append(_dot(p.astype(BF16), v))
    o = jnp.concatenate(outs, axis=1).astype(BF16)
    return x + _dot(o, wo_ref[...])


def _post_mixer_kernel(*refs, n_in):
    a_refs = refs[:n_in]
    w_refs = refs[n_in:2 * n_in]
    (x_ref, gm_ref, wq_ref, gq_ref, kv_ref, wo_ref, gf_ref, wg_ref, wu_ref, wd_ref,
     o_ref, xn_ref, acc_ref) = refs[2 * n_in:]
    j = pl.program_id(1)

    @pl.when(j == 0)
    def _():
        x = x_ref[...]
        for a_ref, w_ref in zip(a_refs, w_refs):
            x = x + _dot(a_ref[...], w_ref[...])
        x = _mem_attend(x, gm_ref, wq_ref, gq_ref, kv_ref, wo_ref)
        xn_ref[...] = _rms_rows(x, gf_ref[...]).astype(BF16)
        acc_ref[...] = x

    xn = xn_ref[...]
    gate = _dot(xn, wg_ref[...])
    up = _dot(xn, wu_ref[...])
    act = (gate * jax.nn.sigmoid(gate) * up).astype(BF16)
    acc_ref[...] += _dot(act, wd_ref[...])

    @pl.when(j == pl.num_programs(1) - 1)
    def _():
        o_ref[...] = acc_ref[...]


def _post_mixer(parts, weights, x2, g_mem, wq, gq, kv, wo, g_ffn, wg, wu, wd, S, tm=512, n_split=2):
    T = x2.shape[0]
    nt = S // tm
    tf = D_FF // n_split
    n_in = len(parts)
    M, n = kv.shape[1], kv.shape[2]

    def const(a):
        return pl.BlockSpec(a.shape, lambda i, j: (0,) * a.ndim)

    in_specs = ([pl.BlockSpec((tm, p.shape[1]), lambda i, j: (i, 0)) for p in parts]
                + [const(w) for w in weights]
                + [pl.BlockSpec((tm, D_MODEL), lambda i, j: (i, 0)),
                   const(g_mem), const(wq), const(gq),
                   pl.BlockSpec((None, M, n), lambda i, j: (i // nt, 0, 0)),
                   const(wo), const(g_ffn),
                   pl.BlockSpec((D_MODEL, tf), lambda i, j: (0, j)),
                   pl.BlockSpec((D_MODEL, tf), lambda i, j: (0, j)),
                   pl.BlockSpec((tf, D_MODEL), lambda i, j: (j, 0))])
    return pl.pallas_call(
        functools.partial(_post_mixer_kernel, n_in=n_in), grid=(T // tm, n_split), in_specs=in_specs,
        out_specs=pl.BlockSpec((tm, D_MODEL), lambda i, j: (i, 0)),
        out_shape=jax.ShapeDtypeStruct((T, D_MODEL), F32),
        scratch_shapes=[pltpu.VMEM((tm, D_MODEL), BF16), pltpu.VMEM((tm, D_MODEL), F32)],
        compiler_params=_params(("parallel", "arbitrary")), name="post_mixer",
    )(*parts, *weights, x2, g_mem, wq, gq, kv, wo, g_ffn, wg, wu, wd)


def _nsa_prep_kernel(x_ref, trig_ref, gmix_ref, w_ref, nq_ref, nk_ref,
                     gq_ref, gks_ref, gkw_ref,
                     qc_ref, qr_ref, kvs_ref, kvst_ref, kvw_ref, kvwt_ref, kc_ref, vc_ref, gtt_ref,
                     *, n_tiles):
    xn = _rms_rows(x_ref[...], gmix_ref[...]).astype(BF16)
    c64, s64 = trig_ref[:, :LANES], trig_ref[:, LANES:]
    lane = _lane_iota(c64.shape)
    lo64 = (lane % 64) < 8
    first64 = lane < 64
    c64k = jnp.where(first64, c64, 1.0)
    s64k = jnp.where(first64, s64, 0.0)
    nq = nq_ref[...]
    nk = nk_ref[...]

    col = _ColumnProjector(xn, w_ref)

    for j in range(8):
        qn = _head_norm(col(j), nq, gq_ref[...])
        qr = _rope(qn, c64, s64, lo64, 8)
        qc_ref[2 * j] = qn[:, :64].astype(BF16)
        qc_ref[2 * j + 1] = qn[:, 64:].astype(BF16)
        qr_ref[2 * j] = qr[:, :64].astype(BF16)
        qr_ref[2 * j + 1] = qr[:, 64:].astype(BF16)
    tile = pl.program_id(0) % n_tiles
    sel_blk = tile * (ATT_T // NSA_SEL_LEN) + lax.shift_right_logical(
        _row_iota(c64.shape), NSA_SEL_LEN.bit_length() - 1)
    blk_onehot = jnp.where(lane == HEAD_DIM + sel_blk, 1.0, 0.0)
    for g in range(NSA_GROUPS):
        kv = _kv_column(col(8 + g), nk, gks_ref[...], c64k, s64k, lo64, first64)
        kvs_ref[g] = jnp.where(first64, kv, blk_onehot).astype(BF16)
        kvst_ref[g] = kv.T.astype(BF16)
        kv = _kv_column(col(12 + g), nk, gkw_ref[...], c64k, s64k, lo64, first64)
        kvw_ref[g] = kv.astype(BF16)
        kvwt_ref[g] = kv.T.astype(BF16)
    kc_ref[...] = jnp.concatenate([col(16), col(17)], axis=1)
    vc_ref[...] = jnp.concatenate([col(18), col(19)], axis=1)
    gates_t = jax.nn.sigmoid(col(20)).T
    for g in range(NSA_GROUPS):
        gtt_ref[g] = gates_t[12 * g:12 * (g + 1), :]


def _nsa_prep(x2, trig, gmix, w, tabs, B, S):
    T = x2.shape[0]
    tm = ATT_T
    nt = S // tm
    nq, nk, gq, gks, gkw = tabs

    def full(a):
        return pl.BlockSpec(a.shape, lambda i: (0,) * a.ndim)

    def hm(width, heads):
        return pl.BlockSpec((None, heads, tm, width), lambda i: (i // nt, 0, i % nt, 0))

    def hmt(heads):
        return pl.BlockSpec((None, heads, None, 128, tm), lambda i: (i // nt, 0, i % nt, 0, 0))

    def tokm(width):
        return pl.BlockSpec((None, tm, width), lambda i: (i // nt, i % nt, 0))

    out_shape = (
        jax.ShapeDtypeStruct((B, 16, S, 64), BF16),
        jax.ShapeDtypeStruct((B, 16, S, 64), BF16),
        jax.ShapeDtypeStruct((B, 4, S, 128), BF16),
        jax.ShapeDtypeStruct((B, 4, nt, 128, tm), BF16),
        jax.ShapeDtypeStruct((B, 4, S, 128), BF16),
        jax.ShapeDtypeStruct((B, 4, nt, 128, tm), BF16),
        jax.ShapeDtypeStruct((B, S, 256), F32),
        jax.ShapeDtypeStruct((B, S, 256), F32),
        jax.ShapeDtypeStruct((B, 4, 12, S), F32),
    )
    out_specs = (hm(64, 16), hm(64, 16), hm(128, 4), hmt(4), hm(128, 4), hmt(4), tokm(256), tokm(256),
                 pl.BlockSpec((None, 4, 12, tm), lambda i: (i // nt, 0, 0, i % nt)))
    in_specs = [pl.BlockSpec((tm, D_MODEL), lambda i: (i, 0)),
                pl.BlockSpec((tm, 2 * LANES), lambda i: (i, 0)),
                full(gmix), full(w), full(nq), full(nk), full(gq), full(gks), full(gkw)]
    return pl.pallas_call(
        functools.partial(_nsa_prep_kernel, n_tiles=nt), grid=(T // tm,), in_specs=in_specs, out_specs=out_specs,
        out_shape=out_shape, compiler_params=_params(("parallel",)), name="nsa_prep",
    )(x2, trig, gmix, w, nq, nk, gq, gks, gkw)


def _compress_one(x16, pa, pb, w1a, w1b, w2):
    n16 = x16.shape[0]
    h_a = _dot((x16 + pa).astype(BF16), w1a)
    h_b = _dot((x16 + pb).astype(BF16), w1b)
    pre = h_a + pltpu.roll(h_b, n16 - 1, 0)
    act = pre * jax.nn.sigmoid(pre)
    return _dot(act.astype(BF16), w2)


def _compress_kernel(xk_ref, xv_ref, pk_ref, pv_ref, w1k_ref, w1v_ref, w2k_ref, w2v_ref, gk_ref,
                     o_ref, ot_ref):
    half = w1k_ref.shape[0] // 2
    k = _compress_one(xk_ref[...], pk_ref[0:1, :], pk_ref[1:2, :],
                      w1k_ref[:half, :], w1k_ref[half:, :], w2k_ref[...])
    k = _rms_rows(k, gk_ref[...])
    v = _compress_one(xv_ref[...], pv_ref[0:1, :], pv_ref[1:2, :],
                      w1v_ref[:half, :], w1v_ref[half:, :], w2v_ref[...])
    kv = jnp.concatenate([k, v], axis=1)
    o_ref[...] = kv.astype(BF16)
    ot_ref[...] = kv.T.astype(BF16)


def _compress(xk16, xv16, pk, pv, w1k, w1v, w2k, w2v, gk):
    B, G, n16, width = xk16.shape

    def full(a):
        return pl.BlockSpec(a.shape, lambda b, g: (0,) * a.ndim)

    xspec = pl.BlockSpec((None, None, n16, width), lambda b, g: (b, g, 0, 0))
    return pl.pallas_call(
        _compress_kernel, grid=(B, G),
        in_specs=[xspec, xspec, full(pk), full(pv), full(w1k), full(w1v), full(w2k), full(w2v), full(gk)],
        out_specs=(pl.BlockSpec((None, None, n16, 128), lambda b, g: (b, g, 0, 0)),
                   pl.BlockSpec((None, None, 128, n16), lambda b, g: (b, g, 0, 0))),
        out_shape=(jax.ShapeDtypeStruct((B, G, n16, 128), BF16),
                   jax.ShapeDtypeStruct((B, G, 128, n16), BF16)),
        compiler_params=_params(("parallel", "parallel")), name="nsa_compress",
    )(xk16, xv16, pk, pv, w1k, w1v, w2k, w2v, gk)


def _nsa_kernel(qc_ref, qr_ref, kvc_ref, kvct_ref, kvs_ref, kvst_ref, kvw_ref, kvwt_ref, gtt_ref,
                o_ref, sel_ref, *flash_refs, n_cmp, n_top):
    t = ATT_T
    HG = NSA_HEADS // NSA_GROUPS
    n_sel = sel_ref.shape[0]
    n16 = kvc_ref.shape[0]
    i = pl.program_id(2)
    t0 = i * t

    kvc = kvc_ref[...]
    kvct = kvct_ref[...]
    n_id = _row_iota((n16, t))
    q_id = t0 + _lane_iota((n16, t))
    visible = (n_id < n_cmp) & (n_id * NSA_CMP_STRIDE + (NSA_CMP_LEN - 1) <= q_id)
    bias_c = jnp.where(visible, 0.0, MASK_BIAS)
    p_sum = jnp.zeros((n16, t), F32)
    o_c = []
    probs, inv_ls, outs = _softmax_direct([_pad_q(qc_ref[j]) for j in range(HG)], kvc, [kvct], bias_c)
    for j in range(HG):
        p_sum = p_sum + probs[j] * inv_ls[j]
        o_c.append(outs[j] * inv_ls[j])

    b_id = _row_iota((n_sel, n16)) * NSA_SEL_LEN
    r_id = _lane_iota((n_sel, n16)) * NSA_CMP_STRIDE
    cover_t = ((r_id < b_id + NSA_SEL_LEN) & (r_id + NSA_CMP_LEN > b_id)
               & (_lane_iota((n_sel, n16)) < n_cmp))
    cover_t = jnp.where(cover_t, 1.0, 0.0).astype(BF16)
    p_hi, p_lo = _split_bf16(p_sum)
    imp = _dot(cover_t, p_hi) + _dot(cover_t, p_lo)
    blk = _row_iota((n_sel, t))
    cur = lax.shift_right_logical(t0 + _lane_iota((n_sel, t)), NSA_SEL_LEN.bit_length() - 1)
    forced = (blk == 0) | (blk == cur) | (blk == cur - 1)
    imp = jnp.where(forced, NSA_FORCE, imp)
    visible_blk = blk <= cur
    imp = jnp.where(visible_blk, imp, NEG_INF)
    n_larger = jnp.zeros((n_sel, t), F32)
    for m in range(n_sel):
        n_larger = n_larger + jnp.where(imp[m:m + 1, :] > imp, 1.0, 0.0)
    sel_fast = n_larger < n_top
    n_picked = jnp.sum(jnp.where(sel_fast & visible_blk, 1.0, 0.0), axis=0, keepdims=True)
    n_wanted = jnp.minimum(cur[0:1, :] + 1, n_top).astype(F32)
    sel_ref[...] = jnp.where(sel_fast, 0.0, MASK_BIAS)

    @pl.when(jnp.max(jnp.abs(n_picked - n_wanted)) > 0.0)
    def _():
        sel_ref[...] = (_rank_select_t(imp, None, n_top) - 1.0) * (-MASK_BIAS)

    qs = [_pad_q(qr_ref[j]) for j in range(HG)]

    sel_bias = sel_ref[...]
    qs_sel = [_bias_lanes(q, sel_bias) for q in qs]
    flash = _Flash(*flash_refs, t)
    flash.reset()

    def sel_operands(cc):
        c0 = 2 * cc
        k0 = pl.multiple_of(c0 * t, 2 * t)
        return c0, [kvs_ref[pl.ds(k0, 2 * t), :]] * HG, [[kvst_ref[c0], kvst_ref[c0 + 1]]] * HG

    def past_pair(cc, carry):
        _, kvs, kvts = sel_operands(cc)
        flash.update(qs_sel, kvs, kvts, [None] * HG)
        return carry

    n_pairs = (i + 2) // 2
    lax.fori_loop(0, n_pairs - 1, past_pair, 0)
    c0, kvs, kvts = sel_operands(n_pairs - 1)
    causal2 = jnp.where((c0 - i) * t + _row_iota((2 * t, t)) <= _lane_iota((2 * t, t)), 0.0, MASK_BIAS)
    flash.update(qs_sel, kvs, kvts, [causal2] * HG)
    o_s = [flash.result(j) for j in range(HG)]

    n_wc = NSA_WINDOW // t + 1
    cw = jnp.maximum(i - (n_wc - 1), 0)
    kw0 = pl.multiple_of(cw * t, t)
    dist = (i - cw) * t + _lane_iota((n_wc * t, t)) - _row_iota((n_wc * t, t))
    bias_w = jnp.where((dist >= 0) & (dist < NSA_WINDOW), 0.0, MASK_BIAS)
    kv = kvw_ref[pl.ds(kw0, n_wc * t), :]
    kvts = [kvwt_ref[cw + u] for u in range(n_wc)]
    _, inv_ls, outs = _softmax_direct(qs, kv, kvts, bias_w)
    o_w = [outs[j] * inv_ls[j] for j in range(HG)]

    gt = gtt_ref[...]
    heads = [gt[3 * j:3 * j + 1, :] * o_c[j] + gt[3 * j + 1:3 * j + 2, :] * o_s[j]
             + gt[3 * j + 2:3 * j + 3, :] * o_w[j] for j in range(HG)]
    _store_heads(o_ref, heads)


def _nsa_attention(qc, qr, kvc, kvct, kvs, kvst, kvw, kvwt, gates_t):
    B, H, S, _ = qc.shape
    G = NSA_GROUPS
    HG = H // G
    t = ATT_T
    nt = S // t
    n16 = kvc.shape[2]
    n_cmp = (S - NSA_CMP_LEN) // NSA_CMP_STRIDE + 1
    n_sel = S // NSA_SEL_LEN
    n_top = min(NSA_SEL_TOPK, n_sel)
    qspec = pl.BlockSpec((None, HG, t, 64), lambda b, g, i: (b, g, i, 0))
    kvspec = pl.BlockSpec((None, None, S, 128), lambda b, g, i: (b, g, 0, 0))
    kvtspec = pl.BlockSpec((None, None, nt, 128, t), lambda b, g, i: (b, g, 0, 0, 0))
    in_specs = [qspec, qspec,
                pl.BlockSpec((None, None, n16, 128), lambda b, g, i: (b, g, 0, 0)),
                pl.BlockSpec((None, None, 128, n16), lambda b, g, i: (b, g, 0, 0)),
                kvspec, kvtspec, kvspec, kvtspec,
                pl.BlockSpec((None, None, 12, t), lambda b, g, i: (b, g, 0, i))]
    return pl.pallas_call(
        functools.partial(_nsa_kernel, n_cmp=n_cmp, n_top=n_top), grid=(B, G, nt), in_specs=in_specs,
        out_specs=pl.BlockSpec((None, t, HG * 64), lambda b, g, i: (b, i, g)),
        out_shape=jax.ShapeDtypeStruct((B, S, H * 64), BF16),
        scratch_shapes=[pltpu.VMEM((n_sel, t), F32)] + _flash_scratch(HG, t, 2 * t),
        compiler_params=_params(("parallel", "parallel", "parallel")), name="nsa_attention",
    )(qc, qr, kvc, kvct, kvs, kvst, kvw, kvwt, gates_t)


def _rope_freq_row(period, rot):
    half = rot // 2
    inv_freq = ROPE_THETA ** (-(jnp.arange(half, dtype=F32) * 2.0 / rot))
    lane = jnp.arange(LANES) % period
    f = jnp.where(lane < rot, inv_freq[lane % half], 0.0)
    return f.reshape(1, LANES).astype(F32)


def _norm_matrices():
    r = jnp.arange(LANES)
    same = (r[:, None] // 64) == (r[None, :] // 64)
    nq = jnp.where(same, 1.0 / 64, 0.0).astype(BF16)
    nk = jnp.where(same & (r[:, None] < 64), 1.0 / 64, 0.0).astype(BF16)
    return nq, nk


def _q_gain(g):
    return (jnp.tile(g.astype(F32), 2) * Q_SCALE).reshape(1, LANES)


def _k_gain(g):
    return jnp.concatenate([g.astype(F32), jnp.ones((64,), F32)]).reshape(1, LANES)


def _interleave_kv(wk, wv, n_heads):
    d = wk.shape[0]
    wk = wk.reshape(d, n_heads, 64)
    wv = wv.reshape(d, n_heads, 64)
    return jnp.concatenate([wk, wv], axis=2).reshape(d, n_heads * 128)


def _split_cols(w, sizes):
    out, start = [], 0
    for n in sizes:
        out.append(w[:, start:start + n])
        start += n
    return out


def _mixer_layer0(x2, trig, B, S, gmix, w_in, w_out, a_q_norm, a_k_norm, b_q_norm, b_k_norm):
    sizes = (512, 64, 64, 256, 32, 8, 512, 512, 512)
    waq, wak, wav, wiq, wik, wiw, wbq, wbk, wbv = _split_cols(w_in, sizes)
    pad = jnp.zeros((D_MODEL, LANES - 40), w_in.dtype)
    w = jnp.concatenate([waq, wbq, _interleave_kv(wbk, wbv, 8), wak, wav, wiq, wik, wiw, pad],
                        axis=1).astype(BF16)
    nq, nk = _norm_matrices()
    tabs = (nq, nk, _q_gain(a_q_norm), _q_gain(b_q_norm), _k_gain(a_k_norm), _k_gain(b_k_norm))
    aq, bq, bkv, bkvt, akv, akvt, iq, ik, iwt, km = _ab_prep(x2, trig, gmix, w, tabs, B, S)
    n_blk = S // MOBA_BLOCK
    kmean = km.reshape(B, n_blk, 8, 128).transpose(0, 2, 1, 3)
    o_a = _dsa_attention(iq, iwt, ik, aq, akv, akvt).reshape(B * S, 512)
    o_b = _moba_attention(bq, bkv, bkvt, kmean).reshape(B * S, 512)
    w_out = w_out.astype(BF16)
    return [o_a, o_b], [w_out[:512], w_out[512:]]


def _mixer_layer1(x2, trig, B, S, gmix, w_in, w_out, q_norm, kcmp_norm, ksel_norm, kwin_norm,
                  pos_k, pos_v, w1_k, w2_k, w1_v, w2_v):
    G = NSA_GROUPS
    sizes = (1024,) + (256,) * 6 + (48,)
    wq, wkc, wvc, wks, wvs, wkw, wvw, wgt = _split_cols(w_in, sizes)
    pad = jnp.zeros((D_MODEL, LANES - 48), w_in.dtype)
    w = jnp.concatenate([wq, _interleave_kv(wks, wvs, G), _interleave_kv(wkw, wvw, G),
                         wkc, wvc, wgt, pad], axis=1).astype(BF16)
    nq, nk = _norm_matrices()
    tabs = (nq, nk, _q_gain(q_norm), _k_gain(ksel_norm), _k_gain(kwin_norm))
    qc, qr, kvs, kvst, kvw, kvwt, kc_raw, vc_raw, gates_t = _nsa_prep(x2, trig, gmix, w, tabs, B, S)

    n16 = S // NSA_CMP_STRIDE

    def blocks16(t):
        return (t.reshape(B, n16, NSA_CMP_STRIDE, G, HEAD_DIM).transpose(0, 3, 1, 2, 4)
                .reshape(B, G, n16, NSA_CMP_STRIDE * HEAD_DIM))

    def pos_rows(p):
        return p.astype(F32).reshape(2, NSA_CMP_STRIDE * HEAD_DIM)

    kvc, kvct = _compress(blocks16(kc_raw), blocks16(vc_raw), pos_rows(pos_k), pos_rows(pos_v),
                          w1_k.astype(BF16), w1_v.astype(BF16), w2_k.astype(BF16), w2_v.astype(BF16),
                          kcmp_norm.astype(F32).reshape(1, HEAD_DIM))
    o = _nsa_attention(qc, qr, kvc, kvct, kvs, kvst, kvw, kvwt, gates_t)
    return [o.reshape(B * S, NSA_HEADS * HEAD_DIM)], [w_out.astype(BF16)]


def _finish_layer(parts, weights, x2, mem, S, g_mem, g_src, w_q, w_kv, w_o, q_norm, k_norm,
                  g_ffn, ffn_w_in, ffn_w_out):
    row = lambda v: v.astype(F32).reshape(1, -1)
    kv = _mem_kv(mem, row(g_src), w_kv.astype(BF16), row(k_norm))
    wg = ffn_w_in[:, :D_FF].astype(BF16)
    wu = ffn_w_in[:, D_FF:].astype(BF16)
    return _post_mixer(parts, weights, x2, row(g_mem), w_q.astype(BF16), row(q_norm), kv, w_o.astype(BF16),
                       row(g_ffn), wg, wu, ffn_w_out.astype(BF16), S)


def kernel(x, mem, positions, norm_mix, norm_mem, norm_mem_src, norm_ffn, ab_w_in, ab_w_out, dsa_q_norm, dsa_k_norm, moba_q_norm, moba_k_norm, nsa_w_in, nsa_w_out, nsa_q_norm, nsa_kcmp_norm, nsa_ksel_norm, nsa_kwin_norm, nsa_cmp_pos_k, nsa_cmp_pos_v, nsa_cmp_w1_k, nsa_cmp_w2_k, nsa_cmp_w1_v, nsa_cmp_w2_v, mem_w_q, mem_w_kv, mem_w_o, mem_q_norm, mem_k_norm, ffn_w_in, ffn_w_out):
    B, S, D = x.shape
    depth = norm_mix.shape[0]
    x2 = x.reshape(B * S, D)
    trig = _rope_trig(positions.astype(F32).reshape(B * S, 1), _rope_freq_row(64, 16), _rope_freq_row(32, 8))
    row = lambda v: v.astype(F32).reshape(1, -1)
    for i in range(depth):
        j = i // 2
        if i % 2 == 0:
            parts, weights = _mixer_layer0(x2, trig, B, S, row(norm_mix[i]), ab_w_in[j], ab_w_out[j],
                               dsa_q_norm[j], dsa_k_norm[j], moba_q_norm[j], moba_k_norm[j])
        else:
            parts, weights = _mixer_layer1(x2, trig, B, S, row(norm_mix[i]), nsa_w_in[j], nsa_w_out[j],
                               nsa_q_norm[j], nsa_kcmp_norm[j], nsa_ksel_norm[j], nsa_kwin_norm[j],
                               nsa_cmp_pos_k[j], nsa_cmp_pos_v[j], nsa_cmp_w1_k[j], nsa_cmp_w2_k[j],
                               nsa_cmp_w1_v[j], nsa_cmp_w2_v[j])
        x2 = _finish_layer(parts, weights, x2, mem, S, norm_mem[i], norm_mem_src[i], mem_w_q[i], mem_w_kv[i],
                           mem_w_o[i], mem_q_norm[i], mem_k_norm[i], norm_ffn[i], ffn_w_in[i], ffn_w_out[i])
    return x2.reshape(B, S, D)
```

```python
import functools
import math

import jax
import jax.numpy as jnp
from jax import lax
from jax.experimental import pallas as pl
from jax.experimental.pallas import tpu as pltpu

F32 = jnp.float32
BF16 = jnp.bfloat16
I32 = jnp.int32
I16 = jnp.int16

D_MODEL = 1024
N_MEM = 256
HEAD_DIM = 64
ROPE_THETA = 500000.0
RMS_EPS = 1e-6
NEG_INF = -1e30
TINY = 1e-20

DSA_HEADS = 8
DSA_IDX_HEADS = 8
DSA_IDX_DIM = 32
DSA_TOPK = 256
MOBA_HEADS = 8
MOBA_BLOCK = 256
MOBA_TOPK = 3
NSA_HEADS = 16
NSA_GROUPS = 4
NSA_CMP_LEN = 32
NSA_CMP_STRIDE = 16
NSA_SEL_LEN = 64
NSA_SEL_TOPK = 16
NSA_WINDOW = 512
NSA_FORCE = 1e4
MEM_HEADS = 4
MEM_HEAD_DIM = 128
D_FF = ((8 * D_MODEL + 3 * 256 - 1) // (3 * 256)) * 256

LANES = 128
SUBLANES = 8
INT_MIN = -(2 ** 31)
VMEM_LIMIT = 56 * 1024 * 1024

PROJ_GROUP = 4
ATT_T = 256
MASK_BIAS = -1e30
M_FLOOR = -1e29
LOG2E = math.log2(math.e)
Q_SCALE = HEAD_DIM ** -0.5 * LOG2E

NT_DIMS = (((1,), (1,)), ((), ()))


def _dot(a, b):
    return jnp.dot(a, b, preferred_element_type=F32)


def _dot_nt(a, b):
    return lax.dot_general(a, b, NT_DIMS, preferred_element_type=F32)


def _split_bf16(a):
    hi = a.astype(BF16)
    return hi, (a - hi.astype(F32)).astype(BF16)


def _split_dot(a, b):
    hi, lo = _split_bf16(a)
    return _dot(hi, b) + _dot(lo, b)


def _rms_rows(x, gain):
    ms = jnp.mean(x * x, axis=-1, keepdims=True)
    return x * lax.rsqrt(ms + RMS_EPS) * gain


def _params(sem):
    return pltpu.CompilerParams(dimension_semantics=sem, vmem_limit_bytes=VMEM_LIMIT)


def _head_norm(y, norm_m, gain):
    ms = _split_dot(y * y, norm_m)
    return y * lax.rsqrt(ms + RMS_EPS) * gain


def _rope(y, c, s, lo_mask, half):
    sw = jnp.where(lo_mask, pltpu.roll(y, LANES - half, 1), pltpu.roll(y, half, 1))
    return y * c + sw * s


def _lane_iota(shape):
    return lax.broadcasted_iota(I32, shape, 1)


def _row_iota(shape):
    return lax.broadcasted_iota(I32, shape, 0)


def _rope_tables(pos, ftab, period, half):
    ang = pos * ftab
    lane = _lane_iota(ang.shape) % period
    c = jnp.cos(ang)
    s = jnp.sin(ang) * jnp.where(lane < half, -1.0, 1.0)
    return c, s


class _ColumnProjector:
    def __init__(self, xn, w_ref):
        self.xn, self.w_ref, self.groups = xn, w_ref, {}

    def __call__(self, j):
        g, u = divmod(j, PROJ_GROUP)
        if g not in self.groups:
            width = PROJ_GROUP * LANES
            lo = g * width
            hi = min(lo + width, self.w_ref.shape[1])
            self.groups[g] = _dot(self.xn, self.w_ref[:, lo:hi])
        return self.groups[g][:, u * LANES:(u + 1) * LANES]


def _kv_column(yc, nk, gain, c64k, s64k, lo64, first64):
    kn = jnp.where(first64, _head_norm(yc, nk, gain), yc)
    return _rope(kn, c64k, s64k, lo64, 8)


def _rope_trig_kernel(pos_ref, f64_ref, f32_ref, o_ref):
    pos = pos_ref[...]
    c64, s64 = _rope_tables(pos, f64_ref[...], 64, 8)
    c32, s32 = _rope_tables(pos, f32_ref[...], 32, 4)
    o_ref[...] = jnp.concatenate([c64, s64, c32, s32], axis=1)


def _rope_trig(pos2, f64, f32t, tm=1024):
    T = pos2.shape[0]
    return pl.pallas_call(
        _rope_trig_kernel, grid=(T // tm,),
        in_specs=[pl.BlockSpec((tm, 1), lambda i: (i, 0)),
                  pl.BlockSpec(f64.shape, lambda i: (0, 0)), pl.BlockSpec(f32t.shape, lambda i: (0, 0))],
        out_specs=pl.BlockSpec((tm, 4 * LANES), lambda i: (i, 0)),
        out_shape=jax.ShapeDtypeStruct((T, 4 * LANES), F32),
        compiler_params=_params(("parallel",)), name="rope_trig",
    )(pos2, f64, f32t)


def _ab_prep_kernel(x_ref, trig_ref, gmix_ref, w_ref, nq_ref, nk_ref,
                    gaq_ref, gbq_ref, gak_ref, gbk_ref,
                    aq_ref, bq_ref, bkv_ref, bkvt_ref, akv_ref, akvt_ref, iq_ref, ik_ref, iwt_ref, km_ref,
                    *, n_tiles):
    xn = _rms_rows(x_ref[...], gmix_ref[...]).astype(BF16)
    c64, s64, c32, s32 = [trig_ref[:, j * LANES:(j + 1) * LANES] for j in range(4)]
    lane = _lane_iota(c64.shape)
    lo64 = (lane % 64) < 8
    lo32 = (lane % 32) < 4
    first64 = lane < 64
    c64k = jnp.where(first64, c64, 1.0)
    s64k = jnp.where(first64, s64, 0.0)
    first32 = lane < 32
    c32k = jnp.where(first32, c32, 1.0)
    s32k = jnp.where(first32, s32, 0.0)
    nq = nq_ref[...]
    nk = nk_ref[...]

    col = _ColumnProjector(xn, w_ref)

    for j in range(4):
        q = _rope(_head_norm(col(j), nq, gaq_ref[...]), c64, s64, lo64, 8)
        aq_ref[2 * j] = q[:, :64].astype(BF16)
        aq_ref[2 * j + 1] = q[:, 64:].astype(BF16)
    for j in range(4):
        q = _rope(_head_norm(col(4 + j), nq, gbq_ref[...]), c64, s64, lo64, 8)
        bq_ref[2 * j] = q[:, :64].astype(BF16)
        bq_ref[2 * j + 1] = q[:, 64:].astype(BF16)
    blk_onehot = jnp.where(lane == HEAD_DIM + pl.program_id(0) % n_tiles, 1.0, 0.0)
    for h in range(8):
        kv = _kv_column(col(8 + h), nk, gbk_ref[...], c64k, s64k, lo64, first64)
        bkv_ref[h] = jnp.where(first64, kv, blk_onehot).astype(BF16)
        bkvt_ref[h] = kv.T.astype(BF16)
        km_ref[h:h + 1, :] = jnp.mean(kv, axis=0, keepdims=True)
    kv = _kv_column(col(16), nk, gak_ref[...], c64k, s64k, lo64, first64)
    akv_ref[...] = kv.astype(BF16)
    akvt_ref[...] = kv.T.astype(BF16)
    for j in range(2):
        q = _rope(col(17 + j), c32, s32, lo32, 4)
        for u in range(4):
            iq_ref[4 * j + u] = q[:, 32 * u:32 * (u + 1)].astype(BF16)
    yc = col(19)
    ik_ref[...] = _rope(yc, c32k, s32k, lo32, 4)[:, :32].astype(BF16)
    iwt_ref[...] = yc.T[32:40, :]


def _ab_prep(x2, trig, gmix, w, tabs, B, S):
    T = x2.shape[0]
    tm = ATT_T
    nt = S // tm
    n_cols = w.shape[1]
    nq, nk, gaq, gbq, gak, gbk = tabs

    def full(a):
        return pl.BlockSpec(a.shape, lambda i: (0,) * a.ndim)

    def hm(width, heads=8):
        return pl.BlockSpec((None, heads, tm, width), lambda i: (i // nt, 0, i % nt, 0))

    def tokm(width):
        return pl.BlockSpec((None, tm, width), lambda i: (i // nt, i % nt, 0))

    out_shape = (
        jax.ShapeDtypeStruct((B, 8, S, 64), BF16),
        jax.ShapeDtypeStruct((B, 8, S, 64), BF16),
        jax.ShapeDtypeStruct((B, 8, S, 128), BF16),
        jax.ShapeDtypeStruct((B, 8, nt, 128, tm), BF16),
        jax.ShapeDtypeStruct((B, S, 128), BF16),
        jax.ShapeDtypeStruct((B, nt, 128, tm), BF16),
        jax.ShapeDtypeStruct((B, 8, S, 32), BF16),
        jax.ShapeDtypeStruct((B, S, 32), BF16),
        jax.ShapeDtypeStruct((B, 8, S), F32),
        jax.ShapeDtypeStruct((T // tm, 8, 128), F32),
    )
    out_specs = (hm(64), hm(64), hm(128),
                 pl.BlockSpec((None, 8, None, 128, tm), lambda i: (i // nt, 0, i % nt, 0, 0)),
                 tokm(128),
                 pl.BlockSpec((None, None, 128, tm), lambda i: (i // nt, i % nt, 0, 0)),
                 hm(32), tokm(32),
                 pl.BlockSpec((None, 8, tm), lambda i: (i // nt, 0, i % nt)),
                 pl.BlockSpec((None, 8, 128), lambda i: (i, 0, 0)))
    in_specs = [pl.BlockSpec((tm, D_MODEL), lambda i: (i, 0)),
                pl.BlockSpec((tm, 4 * LANES), lambda i: (i, 0)),
                full(gmix), pl.BlockSpec((D_MODEL, n_cols), lambda i: (0, 0)),
                full(nq), full(nk), full(gaq), full(gbq), full(gak), full(gbk)]
    return pl.pallas_call(
        functools.partial(_ab_prep_kernel, n_tiles=nt), grid=(T // tm,), in_specs=in_specs, out_specs=out_specs,
        out_shape=out_shape, compiler_params=_params(("parallel",)), name="ab_prep",
    )(x2, trig, gmix, w, nq, nk, gaq, gbq, gak, gbk)


def _pad_q(q):
    return jnp.concatenate([q, jnp.zeros_like(q)], axis=1)


def _bias_lanes(q, rows):
    n, tq = rows.shape
    parts = [jnp.zeros((HEAD_DIM, tq), F32), rows]
    if n < HEAD_DIM:
        parts.append(jnp.zeros((HEAD_DIM - n, tq), F32))
    lanes = jnp.concatenate(parts, axis=0).T.astype(BF16)
    return jnp.where(_lane_iota(q.shape) < HEAD_DIM, q, lanes)


class _Flash:
    def __init__(self, m_ref, l_ref, acc_ref, s_ref, cmax_ref, p_ref, tq):
        self.m_ref, self.l_ref, self.acc_ref, self.tq = m_ref, l_ref, acc_ref, tq
        self.s_ref, self.cmax_ref, self.p_ref = s_ref, cmax_ref, p_ref

    def reset(self):
        self.m_ref[...] = jnp.full(self.m_ref.shape, M_FLOOR, F32)
        self.l_ref[...] = jnp.zeros(self.l_ref.shape, F32)
        self.acc_ref[...] = jnp.zeros(self.acc_ref.shape, F32)

    def _scores(self, buf, qs, kvs, biases):
        tq = self.tq
        for i in range(len(qs)):
            s = _dot_nt(kvs[i], qs[i])
            if biases[i] is not None:
                s = s + biases[i]
            self.s_ref[buf, i] = s
            self.cmax_ref[buf, :, i * tq:(i + 1) * tq] = jnp.max(s, axis=0, keepdims=True)

    def update(self, qs, kvs, kvts, biases):
        self._scores(0, qs, kvs, biases)
        self._finish(0, kvts)

    def run(self, qs, count, operands):
        def scores(c, buf):
            kvs, _, biases = operands(c)
            self._scores(buf, qs, kvs, biases)

        def finish(c, buf):
            self._finish(buf, operands(c)[1])

        last = jnp.maximum(count - 1, 0)
        scores(0, 0)

        def two_chunks(pp, carry):
            c = 2 * pp
            scores(c + 1, 1)
            finish(c, 0)
            scores(jnp.minimum(c + 2, last), 0)
            finish(c + 1, 1)
            return carry

        lax.fori_loop(0, count // 2, two_chunks, 0)

        @pl.when(count % 2 == 1)
        def _():
            finish(count - 1, 0)

    def _finish(self, buf, kvts):
        n = len(kvts)
        tq = self.tq
        alphas = []
        for i in range(n):
            cols = slice(i * tq, (i + 1) * tq)
            m = self.m_ref[:, cols]
            m_new = jnp.maximum(m, self.cmax_ref[buf, :, cols])
            p = jnp.exp2(self.s_ref[buf, i] - m_new)
            alpha = jnp.exp2(m - m_new)
            self.m_ref[:, cols] = m_new
            self.l_ref[:, cols] = alpha * self.l_ref[:, cols] + p.reshape(-1, SUBLANES, tq).sum(axis=0)
            self.p_ref[i] = p.astype(BF16)
            alphas.append(alpha)
        for i in range(n):
            cols = slice(i * tq, (i + 1) * tq)
            pv, r0 = None, 0
            for kvt in kvts[i]:
                part = _dot(kvt, self.p_ref[i, r0:r0 + kvt.shape[1], :])
                pv = part if pv is None else pv + part
                r0 += kvt.shape[1]
            self.acc_ref[:, cols] = alphas[i] * self.acc_ref[:, cols] + pv

    def result(self, slot):
        cols = slice(slot * self.tq, (slot + 1) * self.tq)
        l = jnp.sum(self.l_ref[:, cols], axis=0, keepdims=True)
        return self.acc_ref[:, cols] / jnp.maximum(l, TINY)


def _flash_scratch(n_slots, tq, kc):
    return [pltpu.VMEM((1, n_slots * tq), F32), pltpu.VMEM((SUBLANES, n_slots * tq), F32),
            pltpu.VMEM((LANES, n_slots * tq), F32),
            pltpu.VMEM((2, n_slots, kc, tq), F32), pltpu.VMEM((2, 1, n_slots * tq), F32),
            pltpu.VMEM((n_slots, kc, tq), BF16)]


def _softmax_direct(qs, kv, kvts, bias):
    scores = [_dot_nt(kv, q) for q in qs]
    probs, inv_ls = [], []
    for s in scores:
        s = s + bias
        m = jnp.maximum(jnp.max(s, axis=0, keepdims=True), M_FLOOR)
        p = jnp.exp2(s - m)
        inv_ls.append(1.0 / jnp.maximum(jnp.sum(p, axis=0, keepdims=True), TINY))
        probs.append(p)
    outs = []
    for p in probs:
        pb = p.astype(BF16)
        o, r0 = None, 0
        for kvt in kvts:
            part = _dot(kvt, pb[r0:r0 + kvt.shape[1]])
            o = part if o is None else o + part
            r0 += kvt.shape[1]
        outs.append(o)
    return probs, inv_ls, outs


def _causal_bias(t):
    return jnp.where(_row_iota((t, t)) <= _lane_iota((t, t)), 0.0, MASK_BIAS)


def _store_heads(o_ref, heads_t):
    tq = heads_t[0].shape[1]
    lane = _lane_iota((tq, LANES))
    for u in range(len(heads_t) // 2):
        even = pltpu.roll(heads_t[2 * u].T, 64, 1)
        odd = heads_t[2 * u + 1].T
        o_ref[:, u * LANES:(u + 1) * LANES] = jnp.where(lane < 64, even, odd).astype(o_ref.dtype)


def _rank_select_t(v, n_valid, n_top):
    n = v.shape[0]
    row = _row_iota(v.shape)
    rank = jnp.zeros(v.shape, F32)
    for m in range(n):
        vm = v[m:m + 1, :]
        ahead = (vm > v) | ((vm == v) & (m < row))
        if n_valid is not None:
            ahead = ahead & (m < n_valid)
        rank = rank + jnp.where(ahead, 1.0, 0.0)
    sel = rank < n_top
    if n_valid is not None:
        sel = sel & (row < n_valid)
    return jnp.where(sel, 1.0, 0.0)


def _dsa_kernel(iq_ref, iwt_ref, ik_ref, aq_ref, akv_ref, akvt_ref, o_ref,
                sk_ref, half_ref, bias_ref, xcut_ref, *flash_refs, k_top, index_bits):
    t = ATT_T
    i = pl.program_id(1)
    n_ch = i + 1
    kio = _row_iota((t, t))
    qio = _lane_iota((t, t))

    def causal(c):
        return (c - i) * t + kio <= qio

    def score_chunk(c, carry):
        k0 = pl.multiple_of(c * t, t)
        ikc = ik_ref[pl.ds(k0, t), :]
        sc = jnp.zeros((t, t), F32)
        for h in range(DSA_IDX_HEADS):
            logit = _dot_nt(ikc, iq_ref[h])
            sc = sc + iwt_ref[h:h + 1, :] * jnp.maximum(logit, 0.0)
        sc = jnp.where(sc == 0.0, 0.0, sc)
        bits = pltpu.bitcast(sc, I32)
        key = bits ^ ((bits >> 31) & 0x7FFFFFFF)
        key = jnp.where(causal(c), key, INT_MIN)
        sk_ref[c] = key
        half_ref[c] = (key >> 16).astype(I16)
        return carry

    lax.fori_loop(0, n_ch, score_chunk, 0)

    def count(pred):
        def body(c, acc8):
            ind = jnp.where(pred(sk_ref[c], c), 1.0, 0.0)
            return acc8 + ind.reshape(-1, SUBLANES, t).sum(axis=0)
        acc8 = lax.fori_loop(0, n_ch, body, jnp.zeros((SUBLANES, t), F32))
        return jnp.sum(acc8, axis=0, keepdims=True)

    def count_half(cand):
        rows = 2 * SUBLANES

        def body(c, acc):
            ind = jnp.where(half_ref[c] >= cand, jnp.bfloat16(1), jnp.bfloat16(0))
            parts = [ind[rows * j:rows * (j + 1), :] for j in range(t // rows)]
            while len(parts) > 1:
                parts = [parts[2 * j] + parts[2 * j + 1] for j in range(len(parts) // 2)]
            return acc + parts[0].astype(F32)
        acc = lax.fori_loop(0, n_ch, body, jnp.zeros((rows, t), F32))
        return jnp.sum(acc, axis=0, keepdims=True)

    def half_search():
        def bit_step(b, v):
            cand = v + lax.shift_left(jnp.int32(1), 15 - b)
            return jnp.where(count_half(cand.astype(I16)) >= k_top, cand, v)
        return lax.fori_loop(0, 16, bit_step, jnp.full((1, t), -(2 ** 15), I32))

    thr_hi = half_search()

    def low_half_chunk(c, carry):
        key = sk_ref[c]
        hi = key >> 16
        lo = (key & 0xFFFF) - 2 ** 15
        half_ref[c] = jnp.where(hi > thr_hi, 2 ** 15 - 1, jnp.where(hi < thr_hi, -(2 ** 15), lo)).astype(I16)
        return carry

    lax.fori_loop(0, n_ch, low_half_chunk, 0)
    thr = lax.shift_left(thr_hi, 16) + (half_search() + 2 ** 15)

    need = k_top - count(lambda blk, c: blk > thr)
    n_ge = count(lambda blk, c: blk >= thr)
    xcut_ref[...] = jnp.full((1, t), 2 ** 30, I32)

    @pl.when(jnp.max(n_ge) > k_top)
    def _():
        def x_step(b, x):
            cand = x + lax.shift_left(jnp.int32(1), index_bits - 1 - b)
            ties_below = count(lambda blk, c: (blk == thr) & (c * t + kio < cand))
            return jnp.where(ties_below <= need, cand, x)
        xcut_ref[...] = lax.fori_loop(0, index_bits, x_step, jnp.zeros((1, t), I32))

    xcut = xcut_ref[...]

    n_pairs = (n_ch + 1) // 2

    def bias_chunk(c, carry):
        blk = sk_ref[jnp.minimum(c, i)]
        keep = (blk > thr) | ((blk == thr) & (c * t + kio < xcut))
        bias_ref[c] = jnp.where(keep & causal(c), 0.0, MASK_BIAS)
        return carry

    lax.fori_loop(0, 2 * n_pairs, bias_chunk, 0)

    flash = _Flash(*flash_refs, t)
    flash.reset()
    qs = [_pad_q(aq_ref[h]) for h in range(DSA_HEADS)]

    n = DSA_HEADS

    def att_pair(cc):
        c0 = 2 * cc
        k0 = pl.multiple_of(c0 * t, 2 * t)
        kv = akv_ref[pl.ds(k0, 2 * t), :]
        bias = jnp.concatenate([bias_ref[c0], bias_ref[c0 + 1]], axis=0)
        return [kv] * n, [[akvt_ref[c0], akvt_ref[c0 + 1]]] * n, [bias] * n

    flash.run(qs, n_pairs, att_pair)
    _store_heads(o_ref, [flash.result(h) for h in range(DSA_HEADS)])


def _dsa_attention(iq, iwt, ik, aq, akv, akvt):
    B, _, S, _ = aq.shape
    t = ATT_T
    nt = S // t
    k_top = min(DSA_TOPK, S // 4)
    in_specs = [
        pl.BlockSpec((None, 8, t, 32), lambda b, i: (b, 0, i, 0)),
        pl.BlockSpec((None, 8, t), lambda b, i: (b, 0, i)),
        pl.BlockSpec((None, S, 32), lambda b, i: (b, 0, 0)),
        pl.BlockSpec((None, 8, t, 64), lambda b, i: (b, 0, i, 0)),
        pl.BlockSpec((None, S, 128), lambda b, i: (b, 0, 0)),
        pl.BlockSpec((None, nt, 128, t), lambda b, i: (b, 0, 0, 0)),
    ]
    return pl.pallas_call(
        functools.partial(_dsa_kernel, k_top=k_top, index_bits=S.bit_length()),
        grid=(B, nt), in_specs=in_specs,
        out_specs=pl.BlockSpec((None, t, 512), lambda b, i: (b, i, 0)),
        out_shape=jax.ShapeDtypeStruct((B, S, 512), BF16),
        scratch_shapes=[pltpu.VMEM((nt, t, t), I32), pltpu.VMEM((nt, t, t), I16), pltpu.VMEM((nt, t, t), F32),
                        pltpu.VMEM((1, t), I32)] + _flash_scratch(DSA_HEADS, t, 2 * t),
        compiler_params=_params(("parallel", "parallel")), name="dsa_attention",
    )(iq, iwt, ik, aq, akv, akvt)


MOBA_HPS = 4


def _moba_kernel(q_ref, kv_ref, kvt_ref, km_ref, o_ref, *flash_refs, n_top):
    t = ATT_T
    own = pl.program_id(2)
    causal = _causal_bias(t)
    flash = _Flash(*flash_refs, t)
    flash.reset()
    qs = []
    for hh in range(MOBA_HPS):
        q = _pad_q(q_ref[hh])
        km_hi, km_lo = _split_bf16(km_ref[hh])
        gate = _dot_nt(km_hi, q) + _dot_nt(km_lo, q)
        keep = _rank_select_t(gate, own, n_top)
        keep = jnp.where(_row_iota(keep.shape) == own, 1.0, keep)
        qs.append(_bias_lanes(q, (keep - 1.0) * (-MASK_BIAS)))

    def operands(cc):
        n0 = 2 * cc
        k0 = pl.multiple_of(n0 * t, 2 * t)
        heads = range(MOBA_HPS)
        return (n0, [kv_ref[hh, pl.ds(k0, 2 * t), :] for hh in heads],
                [[kvt_ref[hh, n0], kvt_ref[hh, n0 + 1]] for hh in heads])

    def past_pair(cc):
        _, kvs, kvts = operands(cc)
        return kvs, kvts, [None] * MOBA_HPS

    n_pairs = (own + 2) // 2
    flash.run(qs, n_pairs - 1, past_pair)
    n0, kvs, kvts = operands(n_pairs - 1)
    own_bias = jnp.concatenate([jnp.where(n0 == own, causal, 0.0), jnp.where(n0 + 1 == own, causal, 0.0)],
                               axis=0)
    flash.update(qs, kvs, kvts, [own_bias] * MOBA_HPS)
    _store_heads(o_ref, [flash.result(hh) for hh in range(MOBA_HPS)])


def _moba_attention(bq, bkv, bkvt, kmean):
    B, H, S, _ = bq.shape
    t = ATT_T
    hps = MOBA_HPS
    n_blk = S // MOBA_BLOCK
    assert t == MOBA_BLOCK and n_blk % 2 == 0 and H % hps == 0
    n_top = max(1, min(MOBA_TOPK, n_blk - 1))
    in_specs = [
        pl.BlockSpec((None, hps, t, 64), lambda b, h, i: (b, h, i, 0)),
        pl.BlockSpec((None, hps, S, 128), lambda b, h, i: (b, h, 0, 0)),
        pl.BlockSpec((None, hps, n_blk, 128, t), lambda b, h, i: (b, h, 0, 0, 0)),
        pl.BlockSpec((None, hps, n_blk, 128), lambda b, h, i: (b, h, 0, 0)),
    ]
    return pl.pallas_call(
        functools.partial(_moba_kernel, n_top=n_top), grid=(B, H // hps, S // t), in_specs=in_specs,
        out_specs=pl.BlockSpec((None, t, hps * 64), lambda b, h, i: (b, i, h)),
        out_shape=jax.ShapeDtypeStruct((B, S, H * 64), BF16),
        scratch_shapes=_flash_scratch(hps, t, 2 * t),
        compiler_params=_params(("parallel", "parallel", "parallel")), name="moba_attention",
    )(bq, bkv, bkvt, kmean)


def _lane_group_norm(y, gain, width):
    outs = []
    for j in range(y.shape[1] // width):
        yc = y[:, j * width:(j + 1) * width]
        outs.append(_rms_rows(yc, gain))
    return jnp.concatenate(outs, axis=1)


def _mem_kv_kernel(m_ref, g_ref, w_ref, gk_ref, o_ref):
    mn = _rms_rows(m_ref[...], g_ref[...]).astype(BF16)
    y = _dot(mn, w_ref[...])
    hw = MEM_HEADS * MEM_HEAD_DIM
    k = _lane_group_norm(y[:, :hw], gk_ref[...], MEM_HEAD_DIM)
    o_ref[...] = jnp.concatenate([k, y[:, hw:]], axis=1).astype(BF16)


def _mem_kv(mem, g, w, gk):
    B, M, _ = mem.shape
    n = w.shape[1]
    return pl.pallas_call(
        _mem_kv_kernel, grid=(B,),
        in_specs=[pl.BlockSpec((None, M, D_MODEL), lambda b: (b, 0, 0)),
                  pl.BlockSpec(g.shape, lambda b: (0, 0)),
                  pl.BlockSpec(w.shape, lambda b: (0, 0)),
                  pl.BlockSpec(gk.shape, lambda b: (0, 0))],
        out_specs=pl.BlockSpec((None, M, n), lambda b: (b, 0, 0)),
        out_shape=jax.ShapeDtypeStruct((B, M, n), BF16),
        compiler_params=_params(("parallel",)), name="mem_kv",
    )(mem, g, w, gk)


def _mem_attend(x, g_ref, wq_ref, gq_ref, kv_ref, wo_ref):
    xn = _rms_rows(x, g_ref[...]).astype(BF16)
    q = _lane_group_norm(_dot(xn, wq_ref[...]), gq_ref[...], MEM_HEAD_DIM).astype(BF16)
    hw = MEM_HEADS * MEM_HEAD_DIM
    scale = MEM_HEAD_DIM ** -0.5
    outs = []
    for h in range(MEM_HEADS):
        cols = slice(h * MEM_HEAD_DIM, (h + 1) * MEM_HEAD_DIM)
        k = kv_ref[:, cols]
        v = kv_ref[:, hw + h * MEM_HEAD_DIM:hw + (h + 1) * MEM_HEAD_DIM]
        s = _dot_nt(q[:, cols], k) * scale
        p = jnp.exp(s - jnp.max(s, axis=-1, keepdims=True))
        p = p / jnp.sum(p, axis=-1, keepdims=True)
        outs.append(_dot(p.astype(BF16), v))
    o = jnp.concatenate(outs, axis=1).astype(BF16)
    return x + _dot(o, wo_ref[...])


def _post_mixer_kernel(*refs, n_in):
    a_refs = refs[:n_in]
    w_refs = refs[n_in:2 * n_in]
    (x_ref, gm_ref, wq_ref, gq_ref, kv_ref, wo_ref, gf_ref, wg_ref, wu_ref, wd_ref,
     o_ref, xn_ref, acc_ref) = refs[2 * n_in:]
    j = pl.program_id(1)

    @pl.when(j == 0)
    def _():
        x = x_ref[...]
        for a_ref, w_ref in zip(a_refs, w_refs):
            x = x + _dot(a_ref[...], w_ref[...])
        x = _mem_attend(x, gm_ref, wq_ref, gq_ref, kv_ref, wo_ref)
        xn_ref[...] = _rms_rows(x, gf_ref[...]).astype(BF16)
        acc_ref[...] = x

    xn = xn_ref[...]
    gate = _dot(xn, wg_ref[...])
    up = _dot(xn, wu_ref[...])
    act = (gate * jax.nn.sigmoid(gate) * up).astype(BF16)
    acc_ref[...] += _dot(act, wd_ref[...])

    @pl.when(j == pl.num_programs(1) - 1)
    def _():
        o_ref[...] = acc_ref[...]


def _post_mixer(parts, weights, x2, g_mem, wq, gq, kv, wo, g_ffn, wg, wu, wd, S, tm=512, n_split=2):
    T = x2.shape[0]
    nt = S // tm
    tf = D_FF // n_split
    n_in = len(parts)
    M, n = kv.shape[1], kv.shape[2]

    def const(a):
        return pl.BlockSpec(a.shape, lambda i, j: (0,) * a.ndim)

    in_specs = ([pl.BlockSpec((tm, p.shape[1]), lambda i, j: (i, 0)) for p in parts]
                + [const(w) for w in weights]
                + [pl.BlockSpec((tm, D_MODEL), lambda i, j: (i, 0)),
                   const(g_mem), const(wq), const(gq),
                   pl.BlockSpec((None, M, n), lambda i, j: (i // nt, 0, 0)),
                   const(wo), const(g_ffn),
                   pl.BlockSpec((D_MODEL, tf), lambda i, j: (0, j)),
                   pl.BlockSpec((D_MODEL, tf), lambda i, j: (0, j)),
                   pl.BlockSpec((tf, D_MODEL), lambda i, j: (j, 0))])
    return pl.pallas_call(
        functools.partial(_post_mixer_kernel, n_in=n_in), grid=(T // tm, n_split), in_specs=in_specs,
        out_specs=pl.BlockSpec((tm, D_MODEL), lambda i, j: (i, 0)),
        out_shape=jax.ShapeDtypeStruct((T, D_MODEL), F32),
        scratch_shapes=[pltpu.VMEM((tm, D_MODEL), BF16), pltpu.VMEM((tm, D_MODEL), F32)],
        compiler_params=_params(("parallel", "arbitrary")), name="post_mixer",
    )(*parts, *weights, x2, g_mem, wq, gq, kv, wo, g_ffn, wg, wu, wd)


def _nsa_prep_kernel(x_ref, trig_ref, gmix_ref, w_ref, nq_ref, nk_ref,
                     gq_ref, gks_ref, gkw_ref,
                     qc_ref, qr_ref, kvs_ref, kvst_ref, kvw_ref, kvwt_ref, kc_ref, vc_ref, gtt_ref,
                     *, n_tiles):
    xn = _rms_rows(x_ref[...], gmix_ref[...]).astype(BF16)
    c64, s64 = trig_ref[:, :LANES], trig_ref[:, LANES:]
    lane = _lane_iota(c64.shape)
    lo64 = (lane % 64) < 8
    first64 = lane < 64
    c64k = jnp.where(first64, c64, 1.0)
    s64k = jnp.where(first64, s64, 0.0)
    nq = nq_ref[...]
    nk = nk_ref[...]

    col = _ColumnProjector(xn, w_ref)

    for j in range(8):
        qn = _head_norm(col(j), nq, gq_ref[...])
        qr = _rope(qn, c64, s64, lo64, 8)
        qc_ref[2 * j] = qn[:, :64].astype(BF16)
        qc_ref[2 * j + 1] = qn[:, 64:].astype(BF16)
        qr_ref[2 * j] = qr[:, :64].astype(BF16)
        qr_ref[2 * j + 1] = qr[:, 64:].astype(BF16)
    tile = pl.program_id(0) % n_tiles
    sel_blk = tile * (ATT_T // NSA_SEL_LEN) + lax.shift_right_logical(
        _row_iota(c64.shape), NSA_SEL_LEN.bit_length() - 1)
    blk_onehot = jnp.where(lane == HEAD_DIM + sel_blk, 1.0, 0.0)
    for g in range(NSA_GROUPS):
        kv = _kv_column(col(8 + g), nk, gks_ref[...], c64k, s64k, lo64, first64)
        kvs_ref[g] = jnp.where(first64, kv, blk_onehot).astype(BF16)
        kvst_ref[g] = kv.T.astype(BF16)
        kv = _kv_column(col(12 + g), nk, gkw_ref[...], c64k, s64k, lo64, first64)
        kvw_ref[g] = kv.astype(BF16)
        kvwt_ref[g] = kv.T.astype(BF16)
    kc_ref[...] = jnp.concatenate([col(16), col(17)], axis=1)
    vc_ref[...] = jnp.concatenate([col(18), col(19)], axis=1)
    gates_t = jax.nn.sigmoid(col(20)).T
    for g in range(NSA_GROUPS):
        gtt_ref[g] = gates_t[12 * g:12 * (g + 1), :]


def _nsa_prep(x2, trig, gmix, w, tabs, B, S):
    T = x2.shape[0]
    tm = ATT_T
    nt = S // tm
    nq, nk, gq, gks, gkw = tabs

    def full(a):
        return pl.BlockSpec(a.shape, lambda i: (0,) * a.ndim)

    def hm(width, heads):
        return pl.BlockSpec((None, heads, tm, width), lambda i: (i // nt, 0, i % nt, 0))

    def hmt(heads):
        return pl.BlockSpec((None, heads, None, 128, tm), lambda i: (i // nt, 0, i % nt, 0, 0))

    def tokm(width):
        return pl.BlockSpec((None, tm, width), lambda i: (i // nt, i % nt, 0))

    out_shape = (
        jax.ShapeDtypeStruct((B, 16, S, 64), BF16),
        jax.ShapeDtypeStruct((B, 16, S, 64), BF16),
        jax.ShapeDtypeStruct((B, 4, S, 128), BF16),
        jax.ShapeDtypeStruct((B, 4, nt, 128, tm), BF16),
        jax.ShapeDtypeStruct((B, 4, S, 128), BF16),
        jax.ShapeDtypeStruct((B, 4, nt, 128, tm), BF16),
        jax.ShapeDtypeStruct((B, S, 256), F32),
        jax.ShapeDtypeStruct((B, S, 256), F32),
        jax.ShapeDtypeStruct((B, 4, 12, S), F32),
    )
    out_specs = (hm(64, 16), hm(64, 16), hm(128, 4), hmt(4), hm(128, 4), hmt(4), tokm(256), tokm(256),
                 pl.BlockSpec((None, 4, 12, tm), lambda i: (i // nt, 0, 0, i % nt)))
    in_specs = [pl.BlockSpec((tm, D_MODEL), lambda i: (i, 0)),
                pl.BlockSpec((tm, 2 * LANES), lambda i: (i, 0)),
                full(gmix), full(w), full(nq), full(nk), full(gq), full(gks), full(gkw)]
    return pl.pallas_call(
        functools.partial(_nsa_prep_kernel, n_tiles=nt), grid=(T // tm,), in_specs=in_specs, out_specs=out_specs,
        out_shape=out_shape, compiler_params=_params(("parallel",)), name="nsa_prep",
    )(x2, trig, gmix, w, nq, nk, gq, gks, gkw)


def _compress_one(x16, pa, pb, w1a, w1b, w2):
    n16 = x16.shape[0]
    h_a = _dot((x16 + pa).astype(BF16), w1a)
    h_b = _dot((x16 + pb).astype(BF16), w1b)
    pre = h_a + pltpu.roll(h_b, n16 - 1, 0)
    act = pre * jax.nn.sigmoid(pre)
    return _dot(act.astype(BF16), w2)


def _compress_kernel(xk_ref, xv_ref, pk_ref, pv_ref, w1k_ref, w1v_ref, w2k_ref, w2v_ref, gk_ref,
                     o_ref, ot_ref):
    half = w1k_ref.shape[0] // 2
    k = _compress_one(xk_ref[...], pk_ref[0:1, :], pk_ref[1:2, :],
                      w1k_ref[:half, :], w1k_ref[half:, :], w2k_ref[...])
    k = _rms_rows(k, gk_ref[...])
    v = _compress_one(xv_ref[...], pv_ref[0:1, :], pv_ref[1:2, :],
                      w1v_ref[:half, :], w1v_ref[half:, :], w2v_ref[...])
    kv = jnp.concatenate([k, v], axis=1)
    o_ref[...] = kv.astype(BF16)
    ot_ref[...] = kv.T.astype(BF16)


def _compress(xk16, xv16, pk, pv, w1k, w1v, w2k, w2v, gk):
    B, G, n16, width = xk16.shape

    def full(a):
        return pl.BlockSpec(a.shape, lambda b, g: (0,) * a.ndim)

    xspec = pl.BlockSpec((None, None, n16, width), lambda b, g: (b, g, 0, 0))
    return pl.pallas_call(
        _compress_kernel, grid=(B, G),
        in_specs=[xspec, xspec, full(pk), full(pv), full(w1k), full(w1v), full(w2k), full(w2v), full(gk)],
        out_specs=(pl.BlockSpec((None, None, n16, 128), lambda b, g: (b, g, 0, 0)),
                   pl.BlockSpec((None, None, 128, n16), lambda b, g: (b, g, 0, 0))),
        out_shape=(jax.ShapeDtypeStruct((B, G, n16, 128), BF16),
                   jax.ShapeDtypeStruct((B, G, 128, n16), BF16)),
        compiler_params=_params(("parallel", "parallel")), name="nsa_compress",
    )(xk16, xv16, pk, pv, w1k, w1v, w2k, w2v, gk)


def _nsa_kernel(qc_ref, qr_ref, kvc_ref, kvct_ref, kvs_ref, kvst_ref, kvw_ref, kvwt_ref, gtt_ref,
                o_ref, sel_ref, *flash_refs, n_cmp, n_top):
    t = ATT_T
    HG = NSA_HEADS // NSA_GROUPS
    n_sel = sel_ref.shape[0]
    n16 = kvc_ref.shape[0]
    i = pl.program_id(2)
    t0 = i * t

    kvc = kvc_ref[...]
    kvct = kvct_ref[...]
    n_id = _row_iota((n16, t))
    q_id = t0 + _lane_iota((n16, t))
    visible = (n_id < n_cmp) & (n_id * NSA_CMP_STRIDE + (NSA_CMP_LEN - 1) <= q_id)
    bias_c = jnp.where(visible, 0.0, MASK_BIAS)
    p_sum = jnp.zeros((n16, t), F32)
    o_c = []
    probs, inv_ls, outs = _softmax_direct([_pad_q(qc_ref[j]) for j in range(HG)], kvc, [kvct], bias_c)
    for j in range(HG):
        p_sum = p_sum + probs[j] * inv_ls[j]
        o_c.append(outs[j] * inv_ls[j])

    b_id = _row_iota((n_sel, n16)) * NSA_SEL_LEN
    r_id = _lane_iota((n_sel, n16)) * NSA_CMP_STRIDE
    cover_t = ((r_id < b_id + NSA_SEL_LEN) & (r_id + NSA_CMP_LEN > b_id)
               & (_lane_iota((n_sel, n16)) < n_cmp))
    cover_t = jnp.where(cover_t, 1.0, 0.0).astype(BF16)
    p_hi, p_lo = _split_bf16(p_sum)
    imp = _dot(cover_t, p_hi) + _dot(cover_t, p_lo)
    blk = _row_iota((n_sel, t))
    cur = lax.shift_right_logical(t0 + _lane_iota((n_sel, t)), NSA_SEL_LEN.bit_length() - 1)
    forced = (blk == 0) | (blk == cur) | (blk == cur - 1)
    imp = jnp.where(forced, NSA_FORCE, imp)
    visible_blk = blk <= cur
    imp = jnp.where(visible_blk, imp, NEG_INF)
    n_larger = jnp.zeros((n_sel, t), F32)
    for m in range(n_sel):
        n_larger = n_larger + jnp.where(imp[m:m + 1, :] > imp, 1.0, 0.0)
    sel_fast = n_larger < n_top
    n_picked = jnp.sum(jnp.where(sel_fast & visible_blk, 1.0, 0.0), axis=0, keepdims=True)
    n_wanted = jnp.minimum(cur[0:1, :] + 1, n_top).astype(F32)
    sel_ref[...] = jnp.where(sel_fast, 0.0, MASK_BIAS)

    @pl.when(jnp.max(jnp.abs(n_picked - n_wanted)) > 0.0)
    def _():
        sel_ref[...] = (_rank_select_t(imp, None, n_top) - 1.0) * (-MASK_BIAS)

    qs = [_pad_q(qr_ref[j]) for j in range(HG)]

    sel_bias = sel_ref[...]
    qs_sel = [_bias_lanes(q, sel_bias) for q in qs]
    flash = _Flash(*flash_refs, t)
    flash.reset()

    def sel_operands(cc):
        c0 = 2 * cc
        k0 = pl.multiple_of(c0 * t, 2 * t)
        return c0, [kvs_ref[pl.ds(k0, 2 * t), :]] * HG, [[kvst_ref[c0], kvst_ref[c0 + 1]]] * HG

    def past_pair(cc):
        _, kvs, kvts = sel_operands(cc)
        return kvs, kvts, [None] * HG

    n_pairs = (i + 2) // 2
    flash.run(qs_sel, n_pairs - 1, past_pair)
    c0, kvs, kvts = sel_operands(n_pairs - 1)
    causal2 = jnp.where((c0 - i) * t + _row_iota((2 * t, t)) <= _lane_iota((2 * t, t)), 0.0, MASK_BIAS)
    flash.update(qs_sel, kvs, kvts, [causal2] * HG)
    o_s = [flash.result(j) for j in range(HG)]

    n_wc = NSA_WINDOW // t + 1
    cw = jnp.maximum(i - (n_wc - 1), 0)
    kw0 = pl.multiple_of(cw * t, t)
    dist = (i - cw) * t + _lane_iota((n_wc * t, t)) - _row_iota((n_wc * t, t))
    bias_w = jnp.where((dist >= 0) & (dist < NSA_WINDOW), 0.0, MASK_BIAS)
    kv = kvw_ref[pl.ds(kw0, n_wc * t), :]
    kvts = [kvwt_ref[cw + u] for u in range(n_wc)]
    _, inv_ls, outs = _softmax_direct(qs, kv, kvts, bias_w)
    o_w = [outs[j] * inv_ls[j] for j in range(HG)]

    gt = gtt_ref[...]
    heads = [gt[3 * j:3 * j + 1, :] * o_c[j] + gt[3 * j + 1:3 * j + 2, :] * o_s[j]
             + gt[3 * j + 2:3 * j + 3, :] * o_w[j] for j in range(HG)]
    _store_heads(o_ref, heads)


def _nsa_attention(qc, qr, kvc, kvct, kvs, kvst, kvw, kvwt, gates_t):
    B, H, S, _ = qc.shape
    G = NSA_GROUPS
    HG = H // G
    t = ATT_T
    nt = S // t
    n16 = kvc.shape[2]
    n_cmp = (S - NSA_CMP_LEN) // NSA_CMP_STRIDE + 1
    n_sel = S // NSA_SEL_LEN
    n_top = min(NSA_SEL_TOPK, n_sel)
    qspec = pl.BlockSpec((None, HG, t, 64), lambda b, g, i: (b, g, i, 0))
    kvspec = pl.BlockSpec((None, None, S, 128), lambda b, g, i: (b, g, 0, 0))
    kvtspec = pl.BlockSpec((None, None, nt, 128, t), lambda b, g, i: (b, g, 0, 0, 0))
    in_specs = [qspec, qspec,
                pl.BlockSpec((None, None, n16, 128), lambda b, g, i: (b, g, 0, 0)),
                pl.BlockSpec((None, None, 128, n16), lambda b, g, i: (b, g, 0, 0)),
                kvspec, kvtspec, kvspec, kvtspec,
                pl.BlockSpec((None, None, 12, t), lambda b, g, i: (b, g, 0, i))]
    return pl.pallas_call(
        functools.partial(_nsa_kernel, n_cmp=n_cmp, n_top=n_top), grid=(B, G, nt), in_specs=in_specs,
        out_specs=pl.BlockSpec((None, t, HG * 64), lambda b, g, i: (b, i, g)),
        out_shape=jax.ShapeDtypeStruct((B, S, H * 64), BF16),
        scratch_shapes=[pltpu.VMEM((n_sel, t), F32)] + _flash_scratch(HG, t, 2 * t),
        compiler_params=_params(("parallel", "parallel", "parallel")), name="nsa_attention",
    )(qc, qr, kvc, kvct, kvs, kvst, kvw, kvwt, gates_t)


def _rope_freq_row(period, rot):
    half = rot // 2
    inv_freq = ROPE_THETA ** (-(jnp.arange(half, dtype=F32) * 2.0 / rot))
    lane = jnp.arange(LANES) % period
    f = jnp.where(lane < rot, inv_freq[lane % half], 0.0)
    return f.reshape(1, LANES).astype(F32)


def _norm_matrices():
    r = jnp.arange(LANES)
    same = (r[:, None] // 64) == (r[None, :] // 64)
    nq = jnp.where(same, 1.0 / 64, 0.0).astype(BF16)
    nk = jnp.where(same & (r[:, None] < 64), 1.0 / 64, 0.0).astype(BF16)
    return nq, nk


def _q_gain(g):
    return (jnp.tile(g.astype(F32), 2) * Q_SCALE).reshape(1, LANES)


def _k_gain(g):
    return jnp.concatenate([g.astype(F32), jnp.ones((64,), F32)]).reshape(1, LANES)


def _interleave_kv(wk, wv, n_heads):
    d = wk.shape[0]
    wk = wk.reshape(d, n_heads, 64)
    wv = wv.reshape(d, n_heads, 64)
    return jnp.concatenate([wk, wv], axis=2).reshape(d, n_heads * 128)


def _split_cols(w, sizes):
    out, start = [], 0
    for n in sizes:
        out.append(w[:, start:start + n])
        start += n
    return out


def _mixer_layer0(x2, trig, B, S, gmix, w_in, w_out, a_q_norm, a_k_norm, b_q_norm, b_k_norm):
    sizes = (512, 64, 64, 256, 32, 8, 512, 512, 512)
    waq, wak, wav, wiq, wik, wiw, wbq, wbk, wbv = _split_cols(w_in, sizes)
    pad = jnp.zeros((D_MODEL, LANES - 40), w_in.dtype)
    w = jnp.concatenate([waq, wbq, _interleave_kv(wbk, wbv, 8), wak, wav, wiq, wik, wiw, pad],
                        axis=1).astype(BF16)
    nq, nk = _norm_matrices()
    tabs = (nq, nk, _q_gain(a_q_norm), _q_gain(b_q_norm), _k_gain(a_k_norm), _k_gain(b_k_norm))
    aq, bq, bkv, bkvt, akv, akvt, iq, ik, iwt, km = _ab_prep(x2, trig, gmix, w, tabs, B, S)
    n_blk = S // MOBA_BLOCK
    kmean = km.reshape(B, n_blk, 8, 128).transpose(0, 2, 1, 3)
    o_a = _dsa_attention(iq, iwt, ik, aq, akv, akvt).reshape(B * S, 512)
    o_b = _moba_attention(bq, bkv, bkvt, kmean).reshape(B * S, 512)
    w_out = w_out.astype(BF16)
    return [o_a, o_b], [w_out[:512], w_out[512:]]


def _mixer_layer1(x2, trig, B, S, gmix, w_in, w_out, q_norm, kcmp_norm, ksel_norm, kwin_norm,
                  pos_k, pos_v, w1_k, w2_k, w1_v, w2_v):
    G = NSA_GROUPS
    sizes = (1024,) + (256,) * 6 + (48,)
    wq, wkc, wvc, wks, wvs, wkw, wvw, wgt = _split_cols(w_in, sizes)
    pad = jnp.zeros((D_MODEL, LANES - 48), w_in.dtype)
    w = jnp.concatenate([wq, _interleave_kv(wks, wvs, G), _interleave_kv(wkw, wvw, G),
                         wkc, wvc, wgt, pad], axis=1).astype(BF16)
    nq, nk = _norm_matrices()
    tabs = (nq, nk, _q_gain(q_norm), _k_gain(ksel_norm), _k_gain(kwin_norm))
    qc, qr, kvs, kvst, kvw, kvwt, kc_raw, vc_raw, gates_t = _nsa_prep(x2, trig, gmix, w, tabs, B, S)

    n16 = S // NSA_CMP_STRIDE

    def blocks16(t):
        return (t.reshape(B, n16, NSA_CMP_STRIDE, G, HEAD_DIM).transpose(0, 3, 1, 2, 4)
                .reshape(B, G, n16, NSA_CMP_STRIDE * HEAD_DIM))

    def pos_rows(p):
        return p.astype(F32).reshape(2, NSA_CMP_STRIDE * HEAD_DIM)

    kvc, kvct = _compress(blocks16(kc_raw), blocks16(vc_raw), pos_rows(pos_k), pos_rows(pos_v),
                          w1_k.astype(BF16), w1_v.astype(BF16), w2_k.astype(BF16), w2_v.astype(BF16),
                          kcmp_norm.astype(F32).reshape(1, HEAD_DIM))
    o = _nsa_attention(qc, qr, kvc, kvct, kvs, kvst, kvw, kvwt, gates_t)
    return [o.reshape(B * S, NSA_HEADS * HEAD_DIM)], [w_out.astype(BF16)]


def _finish_layer(parts, weights, x2, mem, S, g_mem, g_src, w_q, w_kv, w_o, q_norm, k_norm,
                  g_ffn, ffn_w_in, ffn_w_out):
    row = lambda v: v.astype(F32).reshape(1, -1)
    kv = _mem_kv(mem, row(g_src), w_kv.astype(BF16), row(k_norm))
    wg = ffn_w_in[:, :D_FF].astype(BF16)
    wu = ffn_w_in[:, D_FF:].astype(BF16)
    return _post_mixer(parts, weights, x2, row(g_mem), w_q.astype(BF16), row(q_norm), kv, w_o.astype(BF16),
                       row(g_ffn), wg, wu, ffn_w_out.astype(BF16), S)


def kernel(x, mem, positions, norm_mix, norm_mem, norm_mem_src, norm_ffn, ab_w_in, ab_w_out, dsa_q_norm, dsa_k_norm, moba_q_norm, moba_k_norm, nsa_w_in, nsa_w_out, nsa_q_norm, nsa_kcmp_norm, nsa_ksel_norm, nsa_kwin_norm, nsa_cmp_pos_k, nsa_cmp_pos_v, nsa_cmp_w1_k, nsa_cmp_w2_k, nsa_cmp_w1_v, nsa_cmp_w2_v, mem_w_q, mem_w_kv, mem_w_o, mem_q_norm, mem_k_norm, ffn_w_in, ffn_w_out):
    B, S, D = x.shape
    depth = norm_mix.shape[0]
    x2 = x.reshape(B * S, D)
    trig = _rope_trig(positions.astype(F32).reshape(B * S, 1), _rope_freq_row(64, 16), _rope_freq_row(32, 8))
    row = lambda v: v.astype(F32).reshape(1, -1)
    for i in range(depth):
        j = i // 2
        if i % 2 == 0:
            parts, weights = _mixer_layer0(x2, trig, B, S, row(norm_mix[i]), ab_w_in[j], ab_w_out[j],
                               dsa_q_norm[j], dsa_k_norm[j], moba_q_norm[j], moba_k_norm[j])
        else:
            parts, weights = _mixer_layer1(x2, trig, B, S, row(norm_mix[i]), nsa_w_in[j], nsa_w_out[j],
                               nsa_q_norm[j], nsa_kcmp_norm[j], nsa_ksel_norm[j], nsa_kwin_norm[j],
                               nsa_cmp_pos_k[j], nsa_cmp_pos_v[j], nsa_cmp_w1_k[j], nsa_cmp_w2_k[j],
                               nsa_cmp_w1_v[j], nsa_cmp_w2_v[j])
        x2 = _finish_layer(parts, weights, x2, mem, S, norm_mem[i], norm_mem_src[i], mem_w_q[i], mem_w_kv[i],
                           mem_w_o[i], mem_q_norm[i], mem_k_norm[i], norm_ffn[i], ffn_w_in[i], ffn_w_out[i])
    return x2.reshape(B, S, D)
```

```python
import functools
import math

import jax
import jax.numpy as jnp
from jax import lax
from jax.experimental import pallas as pl
from jax.experimental.pallas import tpu as pltpu

F32 = jnp.float32
BF16 = jnp.bfloat16
I32 = jnp.int32
I16 = jnp.int16

D_MODEL = 1024
N_MEM = 256
HEAD_DIM = 64
ROPE_THETA = 500000.0
RMS_EPS = 1e-6
NEG_INF = -1e30
TINY = 1e-20

DSA_HEADS = 8
DSA_IDX_HEADS = 8
DSA_IDX_DIM = 32
DSA_TOPK = 256
MOBA_HEADS = 8
MOBA_BLOCK = 256
MOBA_TOPK = 3
NSA_HEADS = 16
NSA_GROUPS = 4
NSA_CMP_LEN = 32
NSA_CMP_STRIDE = 16
NSA_SEL_LEN = 64
NSA_SEL_TOPK = 16
NSA_WINDOW = 512
NSA_FORCE = 1e4
MEM_HEADS = 4
MEM_HEAD_DIM = 128
D_FF = ((8 * D_MODEL + 3 * 256 - 1) // (3 * 256)) * 256

LANES = 128
SUBLANES = 8
INT_MIN = -(2 ** 31)
VMEM_LIMIT = 56 * 1024 * 1024

PROJ_GROUP = 4
ATT_T = 256
MASK_BIAS = -1e30
M_FLOOR = -1e29
LOG2E = math.log2(math.e)
Q_SCALE = HEAD_DIM ** -0.5 * LOG2E

NT_DIMS = (((1,), (1,)), ((), ()))


def _dot(a, b):
    return jnp.dot(a, b, preferred_element_type=F32)


def _dot_nt(a, b):
    return lax.dot_general(a, b, NT_DIMS, preferred_element_type=F32)


def _split_bf16(a):
    hi = a.astype(BF16)
    return hi, (a - hi.astype(F32)).astype(BF16)


def _split_dot(a, b):
    hi, lo = _split_bf16(a)
    return _dot(hi, b) + _dot(lo, b)


def _rms_rows(x, gain):
    ms = jnp.mean(x * x, axis=-1, keepdims=True)
    return x * lax.rsqrt(ms + RMS_EPS) * gain


def _params(sem):
    return pltpu.CompilerParams(dimension_semantics=sem, vmem_limit_bytes=VMEM_LIMIT)


def _head_norm(y, norm_m, gain):
    ms = _split_dot(y * y, norm_m)
    return y * lax.rsqrt(ms + RMS_EPS) * gain


def _rope(y, c, s, lo_mask, half):
    sw = jnp.where(lo_mask, pltpu.roll(y, LANES - half, 1), pltpu.roll(y, half, 1))
    return y * c + sw * s


def _lane_iota(shape):
    return lax.broadcasted_iota(I32, shape, 1)


def _row_iota(shape):
    return lax.broadcasted_iota(I32, shape, 0)


def _rope_tables(pos, ftab, period, half):
    ang = pos * ftab
    lane = _lane_iota(ang.shape) % period
    c = jnp.cos(ang)
    s = jnp.sin(ang) * jnp.where(lane < half, -1.0, 1.0)
    return c, s


class _ColumnProjector:
    def __init__(self, xn, w_ref):
        self.xn, self.w_ref, self.groups = xn, w_ref, {}

    def __call__(self, j):
        g, u = divmod(j, PROJ_GROUP)
        if g not in self.groups:
            width = PROJ_GROUP * LANES
            lo = g * width
            hi = min(lo + width, self.w_ref.shape[1])
            self.groups[g] = _dot(self.xn, self.w_ref[:, lo:hi])
        return self.groups[g][:, u * LANES:(u + 1) * LANES]


def _kv_column(yc, nk, gain, c64k, s64k, lo64, first64):
    kn = jnp.where(first64, _head_norm(yc, nk, gain), yc)
    return _rope(kn, c64k, s64k, lo64, 8)


def _rope_trig_kernel(pos_ref, f64_ref, f32_ref, o_ref):
    pos = pos_ref[...]
    c64, s64 = _rope_tables(pos, f64_ref[...], 64, 8)
    c32, s32 = _rope_tables(pos, f32_ref[...], 32, 4)
    o_ref[...] = jnp.concatenate([c64, s64, c32, s32], axis=1)


def _rope_trig(pos2, f64, f32t, tm=1024):
    T = pos2.shape[0]
    return pl.pallas_call(
        _rope_trig_kernel, grid=(T // tm,),
        in_specs=[pl.BlockSpec((tm, 1), lambda i: (i, 0)),
                  pl.BlockSpec(f64.shape, lambda i: (0, 0)), pl.BlockSpec(f32t.shape, lambda i: (0, 0))],
        out_specs=pl.BlockSpec((tm, 4 * LANES), lambda i: (i, 0)),
        out_shape=jax.ShapeDtypeStruct((T, 4 * LANES), F32),
        compiler_params=_params(("parallel",)), name="rope_trig",
    )(pos2, f64, f32t)


def _ab_prep_kernel(x_ref, trig_ref, gmix_ref, w_ref, nq_ref, nk_ref,
                    gaq_ref, gbq_ref, gak_ref, gbk_ref,
                    aq_ref, bq_ref, bkv_ref, bkvt_ref, akv_ref, akvt_ref, iq_ref, ik_ref, iwt_ref, km_ref,
                    *, n_tiles):
    xn = _rms_rows(x_ref[...], gmix_ref[...]).astype(BF16)
    c64, s64, c32, s32 = [trig_ref[:, j * LANES:(j + 1) * LANES] for j in range(4)]
    lane = _lane_iota(c64.shape)
    lo64 = (lane % 64) < 8
    lo32 = (lane % 32) < 4
    first64 = lane < 64
    c64k = jnp.where(first64, c64, 1.0)
    s64k = jnp.where(first64, s64, 0.0)
    first32 = lane < 32
    c32k = jnp.where(first32, c32, 1.0)
    s32k = jnp.where(first32, s32, 0.0)
    nq = nq_ref[...]
    nk = nk_ref[...]

    col = _ColumnProjector(xn, w_ref)

    for j in range(4):
        q = _rope(_head_norm(col(j), nq, gaq_ref[...]), c64, s64, lo64, 8)
        aq_ref[2 * j] = q[:, :64].astype(BF16)
        aq_ref[2 * j + 1] = q[:, 64:].astype(BF16)
    for j in range(4):
        q = _rope(_head_norm(col(4 + j), nq, gbq_ref[...]), c64, s64, lo64, 8)
        bq_ref[2 * j] = q[:, :64].astype(BF16)
        bq_ref[2 * j + 1] = q[:, 64:].astype(BF16)
    blk_onehot = jnp.where(lane == HEAD_DIM + pl.program_id(0) % n_tiles, 1.0, 0.0)
    for h in range(8):
        kv = _kv_column(col(8 + h), nk, gbk_ref[...], c64k, s64k, lo64, first64)
        bkv_ref[h] = jnp.where(first64, kv, blk_onehot).astype(BF16)
        bkvt_ref[h] = kv.T.astype(BF16)
        km_ref[h:h + 1, :] = jnp.mean(kv, axis=0, keepdims=True)
    kv = _kv_column(col(16), nk, gak_ref[...], c64k, s64k, lo64, first64)
    akv_ref[...] = kv.astype(BF16)
    akvt_ref[...] = kv.T.astype(BF16)
    for j in range(2):
        q = _rope(col(17 + j), c32, s32, lo32, 4)
        for u in range(4):
            iq_ref[4 * j + u] = q[:, 32 * u:32 * (u + 1)].astype(BF16)
    yc = col(19)
    ik_ref[...] = _rope(yc, c32k, s32k, lo32, 4)[:, :32].astype(BF16)
    iwt_ref[...] = yc.T[32:40, :]


def _ab_prep(x2, trig, gmix, w, tabs, B, S):
    T = x2.shape[0]
    tm = ATT_T
    nt = S // tm
    n_cols = w.shape[1]
    nq, nk, gaq, gbq, gak, gbk = tabs

    def full(a):
        return pl.BlockSpec(a.shape, lambda i: (0,) * a.ndim)

    def hm(width, heads=8):
        return pl.BlockSpec((None, heads, tm, width), lambda i: (i // nt, 0, i % nt, 0))

    def tokm(width):
        return pl.BlockSpec((None, tm, width), lambda i: (i // nt, i % nt, 0))

    out_shape = (
        jax.ShapeDtypeStruct((B, 8, S, 64), BF16),
        jax.ShapeDtypeStruct((B, 8, S, 64), BF16),
        jax.ShapeDtypeStruct((B, 8, S, 128), BF16),
        jax.ShapeDtypeStruct((B, 8, nt, 128, tm), BF16),
        jax.ShapeDtypeStruct((B, S, 128), BF16),
        jax.ShapeDtypeStruct((B, nt, 128, tm), BF16),
        jax.ShapeDtypeStruct((B, 8, S, 32), BF16),
        jax.ShapeDtypeStruct((B, S, 32), BF16),
        jax.ShapeDtypeStruct((B, 8, S), F32),
        jax.ShapeDtypeStruct((T // tm, 8, 128), F32),
    )
    out_specs = (hm(64), hm(64), hm(128),
                 pl.BlockSpec((None, 8, None, 128, tm), lambda i: (i // nt, 0, i % nt, 0, 0)),
                 tokm(128),
                 pl.BlockSpec((None, None, 128, tm), lambda i: (i // nt, i % nt, 0, 0)),
                 hm(32), tokm(32),
                 pl.BlockSpec((None, 8, tm), lambda i: (i // nt, 0, i % nt)),
                 pl.BlockSpec((None, 8, 128), lambda i: (i, 0, 0)))
    in_specs = [pl.BlockSpec((tm, D_MODEL), lambda i: (i, 0)),
                pl.BlockSpec((tm, 4 * LANES), lambda i: (i, 0)),
                full(gmix), pl.BlockSpec((D_MODEL, n_cols), lambda i: (0, 0)),
                full(nq), full(nk), full(gaq), full(gbq), full(gak), full(gbk)]
    return pl.pallas_call(
        functools.partial(_ab_prep_kernel, n_tiles=nt), grid=(T // tm,), in_specs=in_specs, out_specs=out_specs,
        out_shape=out_shape, compiler_params=_params(("parallel",)), name="ab_prep",
    )(x2, trig, gmix, w, nq, nk, gaq, gbq, gak, gbk)


def _pad_q(q):
    return jnp.concatenate([q, jnp.zeros_like(q)], axis=1)


def _bias_lanes(q, rows):
    n, tq = rows.shape
    parts = [jnp.zeros((HEAD_DIM, tq), F32), rows]
    if n < HEAD_DIM:
        parts.append(jnp.zeros((HEAD_DIM - n, tq), F32))
    lanes = jnp.concatenate(parts, axis=0).T.astype(BF16)
    return jnp.where(_lane_iota(q.shape) < HEAD_DIM, q, lanes)


class _Flash:
    def __init__(self, m_ref, l_ref, acc_ref, s_ref, cmax_ref, p_ref, tq):
        self.m_ref, self.l_ref, self.acc_ref, self.tq = m_ref, l_ref, acc_ref, tq
        self.s_ref, self.cmax_ref, self.p_ref = s_ref, cmax_ref, p_ref

    def reset(self):
        self.m_ref[...] = jnp.full(self.m_ref.shape, M_FLOOR, F32)
        self.l_ref[...] = jnp.zeros(self.l_ref.shape, F32)
        self.acc_ref[...] = jnp.zeros(self.acc_ref.shape, F32)

    def _scores(self, buf, qs, kvs, biases):
        tq = self.tq
        for i in range(len(qs)):
            s = _dot_nt(kvs[i], qs[i])
            if biases[i] is not None:
                s = s + biases[i]
            self.s_ref[buf, i, :s.shape[0], :] = s
            self.cmax_ref[buf, :, i * tq:(i + 1) * tq] = jnp.max(s, axis=0, keepdims=True)

    def update(self, qs, kvs, kvts, biases):
        self._scores(0, qs, kvs, biases)
        self._finish(0, kvts)

    def run(self, qs, count, operands):
        def scores(c, buf):
            kvs, _, biases = operands(c)
            self._scores(buf, qs, kvs, biases)

        def finish(c, buf):
            self._finish(buf, operands(c)[1])

        last = jnp.maximum(count - 1, 0)
        scores(0, 0)

        def two_chunks(pp, carry):
            c = 2 * pp
            scores(c + 1, 1)
            finish(c, 0)
            scores(jnp.minimum(c + 2, last), 0)
            finish(c + 1, 1)
            return carry

        lax.fori_loop(0, count // 2, two_chunks, 0)

        @pl.when(count % 2 == 1)
        def _():
            finish(count - 1, 0)

    def _finish(self, buf, kvts):
        n = len(kvts)
        tq = self.tq
        kc = sum(kvt.shape[1] for kvt in kvts[0])
        alphas = []
        for i in range(n):
            cols = slice(i * tq, (i + 1) * tq)
            m = self.m_ref[:, cols]
            m_new = jnp.maximum(m, self.cmax_ref[buf, :, cols])
            p = jnp.exp2(self.s_ref[buf, i, :kc, :] - m_new)
            alpha = jnp.exp2(m - m_new)
            self.m_ref[:, cols] = m_new
            self.l_ref[:, cols] = alpha * self.l_ref[:, cols] + p.reshape(-1, SUBLANES, tq).sum(axis=0)
            self.p_ref[i, :kc, :] = p.astype(BF16)
            alphas.append(alpha)
        for i in range(n):
            cols = slice(i * tq, (i + 1) * tq)
            pv, r0 = None, 0
            for kvt in kvts[i]:
                part = _dot(kvt, self.p_ref[i, r0:r0 + kvt.shape[1], :])
                pv = part if pv is None else pv + part
                r0 += kvt.shape[1]
            self.acc_ref[:, cols] = alphas[i] * self.acc_ref[:, cols] + pv

    def result(self, slot):
        cols = slice(slot * self.tq, (slot + 1) * self.tq)
        l = jnp.sum(self.l_ref[:, cols], axis=0, keepdims=True)
        return self.acc_ref[:, cols] / jnp.maximum(l, TINY)


def _flash_scratch(n_slots, tq, kc):
    return [pltpu.VMEM((1, n_slots * tq), F32), pltpu.VMEM((SUBLANES, n_slots * tq), F32),
            pltpu.VMEM((LANES, n_slots * tq), F32),
            pltpu.VMEM((2, n_slots, kc, tq), F32), pltpu.VMEM((2, 1, n_slots * tq), F32),
            pltpu.VMEM((n_slots, kc, tq), BF16)]


def _softmax_direct(qs, kv, kvts, bias):
    scores = [_dot_nt(kv, q) for q in qs]
    probs, inv_ls = [], []
    for s in scores:
        s = s + bias
        m = jnp.maximum(jnp.max(s, axis=0, keepdims=True), M_FLOOR)
        p = jnp.exp2(s - m)
        inv_ls.append(1.0 / jnp.maximum(jnp.sum(p, axis=0, keepdims=True), TINY))
        probs.append(p)
    outs = []
    for p in probs:
        pb = p.astype(BF16)
        o, r0 = None, 0
        for kvt in kvts:
            part = _dot(kvt, pb[r0:r0 + kvt.shape[1]])
            o = part if o is None else o + part
            r0 += kvt.shape[1]
        outs.append(o)
    return probs, inv_ls, outs


def _causal_bias(t):
    return jnp.where(_row_iota((t, t)) <= _lane_iota((t, t)), 0.0, MASK_BIAS)


def _store_heads(o_ref, heads_t):
    tq = heads_t[0].shape[1]
    lane = _lane_iota((tq, LANES))
    for u in range(len(heads_t) // 2):
        even = pltpu.roll(heads_t[2 * u].T, 64, 1)
        odd = heads_t[2 * u + 1].T
        o_ref[:, u * LANES:(u + 1) * LANES] = jnp.where(lane < 64, even, odd).astype(o_ref.dtype)


def _rank_select_t(v, n_valid, n_top):
    n = v.shape[0]
    row = _row_iota(v.shape)
    rank = jnp.zeros(v.shape, F32)
    for m in range(n):
        vm = v[m:m + 1, :]
        ahead = (vm > v) | ((vm == v) & (m < row))
        if n_valid is not None:
            ahead = ahead & (m < n_valid)
        rank = rank + jnp.where(ahead, 1.0, 0.0)
    sel = rank < n_top
    if n_valid is not None:
        sel = sel & (row < n_valid)
    return jnp.where(sel, 1.0, 0.0)


def _dsa_kernel(iq_ref, iwt_ref, ik_ref, aq_ref, akv_ref, akvt_ref, o_ref,
                sk_ref, half_ref, bias_ref, xcut_ref, *flash_refs, k_top, index_bits):
    t = ATT_T
    i = pl.program_id(1)
    n_ch = i + 1
    kio = _row_iota((t, t))
    qio = _lane_iota((t, t))

    def causal(c):
        return (c - i) * t + kio <= qio

    def score_chunk(c):
        k0 = pl.multiple_of(c * t, t)
        ikc = ik_ref[pl.ds(k0, t), :]
        sc = jnp.zeros((t, t), F32)
        for h in range(DSA_IDX_HEADS):
            logit = _dot_nt(ikc, iq_ref[h])
            sc = sc + iwt_ref[h:h + 1, :] * jnp.maximum(logit, 0.0)
        sc = jnp.where(sc == 0.0, 0.0, sc)
        bits = pltpu.bitcast(sc, I32)
        key = bits ^ ((bits >> 31) & 0x7FFFFFFF)
        key = jnp.where(causal(c), key, INT_MIN)
        sk_ref[c] = key
        half_ref[c] = (key >> 16).astype(I16)

    def score_pair(cc, carry):
        score_chunk(2 * cc)
        score_chunk(2 * cc + 1)
        return carry

    lax.fori_loop(0, (n_ch + 1) // 2, score_pair, 0)

    def count(pred):
        def body(c, acc8):
            ind = jnp.where(pred(sk_ref[c], c), 1.0, 0.0)
            return acc8 + ind.reshape(-1, SUBLANES, t).sum(axis=0)
        acc8 = lax.fori_loop(0, n_ch, body, jnp.zeros((SUBLANES, t), F32))
        return jnp.sum(acc8, axis=0, keepdims=True)

    def count_half(cand):
        rows = 2 * SUBLANES

        def body(c, acc):
            ind = jnp.where(half_ref[c] >= cand, jnp.bfloat16(1), jnp.bfloat16(0))
            parts = [ind[rows * j:rows * (j + 1), :] for j in range(t // rows)]
            while len(parts) > 1:
                parts = [parts[2 * j] + parts[2 * j + 1] for j in range(len(parts) // 2)]
            return acc + parts[0].astype(F32)
        acc = lax.fori_loop(0, n_ch, body, jnp.zeros((rows, t), F32))
        return jnp.sum(acc, axis=0, keepdims=True)

    def half_search():
        def bit_step(b, v):
            cand = v + lax.shift_left(jnp.int32(1), 15 - b)
            return jnp.where(count_half(cand.astype(I16)) >= k_top, cand, v)
        return lax.fori_loop(0, 16, bit_step, jnp.full((1, t), -(2 ** 15), I32))

    thr_hi = half_search()

    def low_half_chunk(c, carry):
        key = sk_ref[c]
        hi = key >> 16
        lo = (key & 0xFFFF) - 2 ** 15
        half_ref[c] = jnp.where(hi > thr_hi, 2 ** 15 - 1, jnp.where(hi < thr_hi, -(2 ** 15), lo)).astype(I16)
        return carry

    lax.fori_loop(0, n_ch, low_half_chunk, 0)
    thr = lax.shift_left(thr_hi, 16) + (half_search() + 2 ** 15)

    need = k_top - count(lambda blk, c: blk > thr)
    n_ge = count(lambda blk, c: blk >= thr)
    xcut_ref[...] = jnp.full((1, t), 2 ** 30, I32)

    @pl.when(jnp.max(n_ge) > k_top)
    def _():
        def x_step(b, x):
            cand = x + lax.shift_left(jnp.int32(1), index_bits - 1 - b)
            ties_below = count(lambda blk, c: (blk == thr) & (c * t + kio < cand))
            return jnp.where(ties_below <= need, cand, x)
        xcut_ref[...] = lax.fori_loop(0, index_bits, x_step, jnp.zeros((1, t), I32))

    xcut = xcut_ref[...]

    n_pairs = (n_ch + 1) // 2

    def bias_chunk(c, carry):
        blk = sk_ref[jnp.minimum(c, i)]
        keep = (blk > thr) | ((blk == thr) & (c * t + kio < xcut))
        bias_ref[c] = jnp.where(keep & causal(c), 0.0, MASK_BIAS)
        return carry

    lax.fori_loop(0, 2 * n_pairs, bias_chunk, 0)

    flash = _Flash(*flash_refs, t)
    flash.reset()
    qs = [_pad_q(aq_ref[h]) for h in range(DSA_HEADS)]

    n = DSA_HEADS

    def att_pair(cc):
        c0 = 2 * cc
        k0 = pl.multiple_of(c0 * t, 2 * t)
        kv = akv_ref[pl.ds(k0, 2 * t), :]
        bias = jnp.concatenate([bias_ref[c0], bias_ref[c0 + 1]], axis=0)
        return [kv] * n, [[akvt_ref[c0], akvt_ref[c0 + 1]]] * n, [bias] * n

    flash.run(qs, n_pairs, att_pair)
    _store_heads(o_ref, [flash.result(h) for h in range(DSA_HEADS)])


def _dsa_attention(iq, iwt, ik, aq, akv, akvt):
    B, _, S, _ = aq.shape
    t = ATT_T
    nt = S // t
    k_top = min(DSA_TOPK, S // 4)
    in_specs = [
        pl.BlockSpec((None, 8, t, 32), lambda b, i: (b, 0, i, 0)),
        pl.BlockSpec((None, 8, t), lambda b, i: (b, 0, i)),
        pl.BlockSpec((None, S, 32), lambda b, i: (b, 0, 0)),
        pl.BlockSpec((None, 8, t, 64), lambda b, i: (b, 0, i, 0)),
        pl.BlockSpec((None, S, 128), lambda b, i: (b, 0, 0)),
        pl.BlockSpec((None, nt, 128, t), lambda b, i: (b, 0, 0, 0)),
    ]
    return pl.pallas_call(
        functools.partial(_dsa_kernel, k_top=k_top, index_bits=S.bit_length()),
        grid=(B, nt), in_specs=in_specs,
        out_specs=pl.BlockSpec((None, t, 512), lambda b, i: (b, i, 0)),
        out_shape=jax.ShapeDtypeStruct((B, S, 512), BF16),
        scratch_shapes=[pltpu.VMEM((nt, t, t), I32), pltpu.VMEM((nt, t, t), I16), pltpu.VMEM((nt, t, t), F32),
                        pltpu.VMEM((1, t), I32)] + _flash_scratch(DSA_HEADS, t, 2 * t),
        compiler_params=_params(("parallel", "parallel")), name="dsa_attention",
    )(iq, iwt, ik, aq, akv, akvt)


MOBA_HPS = 4


def _moba_kernel(q_ref, kv_ref, kvt_ref, km_ref, o_ref, *flash_refs, n_top):
    t = ATT_T
    own = pl.program_id(2)
    causal = _causal_bias(t)
    flash = _Flash(*flash_refs, t)
    flash.reset()
    qs = []
    for hh in range(MOBA_HPS):
        q = _pad_q(q_ref[hh])
        km_hi, km_lo = _split_bf16(km_ref[hh])
        gate = _dot_nt(km_hi, q) + _dot_nt(km_lo, q)
        keep = _rank_select_t(gate, own, n_top)
        keep = jnp.where(_row_iota(keep.shape) == own, 1.0, keep)
        qs.append(_bias_lanes(q, (keep - 1.0) * (-MASK_BIAS)))

    def operands(cc):
        n0 = 2 * cc
        k0 = pl.multiple_of(n0 * t, 2 * t)
        heads = range(MOBA_HPS)
        return (n0, [kv_ref[hh, pl.ds(k0, 2 * t), :] for hh in heads],
                [[kvt_ref[hh, n0], kvt_ref[hh, n0 + 1]] for hh in heads])

    def past_pair(cc):
        _, kvs, kvts = operands(cc)
        return kvs, kvts, [None] * MOBA_HPS

    flash.run(qs, own // 2, past_pair)
    heads = range(MOBA_HPS)

    @pl.when(own % 2 == 0)
    def _():
        k0 = pl.multiple_of(own * t, t)
        flash.update(qs, [kv_ref[hh, pl.ds(k0, t), :] for hh in heads],
                     [[kvt_ref[hh, own]] for hh in heads], [causal] * MOBA_HPS)

    @pl.when(own % 2 == 1)
    def _():
        _, kvs, kvts = operands(own // 2)
        bias = jnp.concatenate([jnp.zeros((t, t), F32), causal], axis=0)
        flash.update(qs, kvs, kvts, [bias] * MOBA_HPS)

    _store_heads(o_ref, [flash.result(hh) for hh in range(MOBA_HPS)])


def _moba_attention(bq, bkv, bkvt, kmean):
    B, H, S, _ = bq.shape
    t = ATT_T
    hps = MOBA_HPS
    n_blk = S // MOBA_BLOCK
    assert t == MOBA_BLOCK and n_blk % 2 == 0 and H % hps == 0
    n_top = max(1, min(MOBA_TOPK, n_blk - 1))
    in_specs = [
        pl.BlockSpec((None, hps, t, 64), lambda b, h, i: (b, h, i, 0)),
        pl.BlockSpec((None, hps, S, 128), lambda b, h, i: (b, h, 0, 0)),
        pl.BlockSpec((None, hps, n_blk, 128, t), lambda b, h, i: (b, h, 0, 0, 0)),
        pl.BlockSpec((None, hps, n_blk, 128), lambda b, h, i: (b, h, 0, 0)),
    ]
    return pl.pallas_call(
        functools.partial(_moba_kernel, n_top=n_top), grid=(B, H // hps, S // t), in_specs=in_specs,
        out_specs=pl.BlockSpec((None, t, hps * 64), lambda b, h, i: (b, i, h)),
        out_shape=jax.ShapeDtypeStruct((B, S, H * 64), BF16),
        scratch_shapes=_flash_scratch(hps, t, 2 * t),
        compiler_params=_params(("parallel", "parallel", "parallel")), name="moba_attention",
    )(bq, bkv, bkvt, kmean)


def _lane_group_norm(y, gain, width):
    outs = []
    for j in range(y.shape[1] // width):
        yc = y[:, j * width:(j + 1) * width]
        outs.append(_rms_rows(yc, gain))
    return jnp.concatenate(outs, axis=1)


def _mem_kv_kernel(m_ref, g_ref, w_ref, gk_ref, o_ref):
    mn = _rms_rows(m_ref[...], g_ref[...]).astype(BF16)
    y = _dot(mn, w_ref[...])
    hw = MEM_HEADS * MEM_HEAD_DIM
    k = _lane_group_norm(y[:, :hw], gk_ref[...], MEM_HEAD_DIM)
    o_ref[...] = jnp.concatenate([k, y[:, hw:]], axis=1).astype(BF16)


def _mem_kv(mem, g, w, gk):
    B, M, _ = mem.shape
    n = w.shape[1]
    return pl.pallas_call(
        _mem_kv_kernel, grid=(B,),
        in_specs=[pl.BlockSpec((None, M, D_MODEL), lambda b: (b, 0, 0)),
                  pl.BlockSpec(g.shape, lambda b: (0, 0)),
                  pl.BlockSpec(w.shape, lambda b: (0, 0)),
                  pl.BlockSpec(gk.shape, lambda b: (0, 0))],
        out_specs=pl.BlockSpec((None, M, n), lambda b: (b, 0, 0)),
        out_shape=jax.ShapeDtypeStruct((B, M, n), BF16),
        compiler_params=_params(("parallel",)), name="mem_kv",
    )(mem, g, w, gk)


def _mem_attend(x, g_ref, wq_ref, gq_ref, kv_ref, wo_ref):
    xn = _rms_rows(x, g_ref[...]).astype(BF16)
    q = _lane_group_norm(_dot(xn, wq_ref[...]), gq_ref[...], MEM_HEAD_DIM).astype(BF16)
    hw = MEM_HEADS * MEM_HEAD_DIM
    scale = MEM_HEAD_DIM ** -0.5
    outs = []
    for h in range(MEM_HEADS):
        cols = slice(h * MEM_HEAD_DIM, (h + 1) * MEM_HEAD_DIM)
        k = kv_ref[:, cols]
        v = kv_ref[:, hw + h * MEM_HEAD_DIM:hw + (h + 1) * MEM_HEAD_DIM]
        s = _dot_nt(q[:, cols], k) * scale
        p = jnp.exp(s - jnp.max(s, axis=-1, keepdims=True))
        p = p / jnp.sum(p, axis=-1, keepdims=True)
        outs.append(_dot(p.astype(BF16), v))
    o = jnp.concatenate(outs, axis=1).astype(BF16)
    return x + _dot(o, wo_ref[...])


def _post_mixer_kernel(*refs, n_in):
    a_refs = refs[:n_in]
    w_refs = refs[n_in:2 * n_in]
    (x_ref, gm_ref, wq_ref, gq_ref, kv_ref, wo_ref, gf_ref, wg_ref, wu_ref, wd_ref,
     o_ref, xn_ref, acc_ref) = refs[2 * n_in:]
    j = pl.program_id(1)

    @pl.when(j == 0)
    def _():
        x = x_ref[...]
        for a_ref, w_ref in zip(a_refs, w_refs):
            x = x + _dot(a_ref[...], w_ref[...])
        x = _mem_attend(x, gm_ref, wq_ref, gq_ref, kv_ref, wo_ref)
        xn_ref[...] = _rms_rows(x, gf_ref[...]).astype(BF16)
        acc_ref[...] = x

    xn = xn_ref[...]
    gate = _dot(xn, wg_ref[...])
    up = _dot(xn, wu_ref[...])
    act = (gate * jax.nn.sigmoid(gate) * up).astype(BF16)
    acc_ref[...] += _dot(act, wd_ref[...])

    @pl.when(j == pl.num_programs(1) - 1)
    def _():
        o_ref[...] = acc_ref[...]


def _post_mixer(parts, weights, x2, g_mem, wq, gq, kv, wo, g_ffn, wg, wu, wd, S, tm=512, n_split=2):
    T = x2.shape[0]
    nt = S // tm
    tf = D_FF // n_split
    n_in = len(parts)
    M, n = kv.shape[1], kv.shape[2]

    def const(a):
        return pl.BlockSpec(a.shape, lambda i, j: (0,) * a.ndim)

    in_specs = ([pl.BlockSpec((tm, p.shape[1]), lambda i, j: (i, 0)) for p in parts]
                + [const(w) for w in weights]
                + [pl.BlockSpec((tm, D_MODEL), lambda i, j: (i, 0)),
                   const(g_mem), const(wq), const(gq),
                   pl.BlockSpec((None, M, n), lambda i, j: (i // nt, 0, 0)),
                   const(wo), const(g_ffn),
                   pl.BlockSpec((D_MODEL, tf), lambda i, j: (0, j)),
                   pl.BlockSpec((D_MODEL, tf), lambda i, j: (0, j)),
                   pl.BlockSpec((tf, D_MODEL), lambda i, j: (j, 0))])
    return pl.pallas_call(
        functools.partial(_post_mixer_kernel, n_in=n_in), grid=(T // tm, n_split), in_specs=in_specs,
        out_specs=pl.BlockSpec((tm, D_MODEL), lambda i, j: (i, 0)),
        out_shape=jax.ShapeDtypeStruct((T, D_MODEL), F32),
        scratch_shapes=[pltpu.VMEM((tm, D_MODEL), BF16), pltpu.VMEM((tm, D_MODEL), F32)],
        compiler_params=_params(("parallel", "arbitrary")), name="post_mixer",
    )(*parts, *weights, x2, g_mem, wq, gq, kv, wo, g_ffn, wg, wu, wd)


def _nsa_prep_kernel(x_ref, trig_ref, gmix_ref, w_ref, nq_ref, nk_ref,
                     gq_ref, gks_ref, gkw_ref,
                     qc_ref, qr_ref, kvs_ref, kvst_ref, kvw_ref, kvwt_ref, kc_ref, vc_ref, gtt_ref,
                     *, n_tiles):
    xn = _rms_rows(x_ref[...], gmix_ref[...]).astype(BF16)
    c64, s64 = trig_ref[:, :LANES], trig_ref[:, LANES:]
    lane = _lane_iota(c64.shape)
    lo64 = (lane % 64) < 8
    first64 = lane < 64
    c64k = jnp.where(first64, c64, 1.0)
    s64k = jnp.where(first64, s64, 0.0)
    nq = nq_ref[...]
    nk = nk_ref[...]

    col = _ColumnProjector(xn, w_ref)

    for j in range(8):
        qn = _head_norm(col(j), nq, gq_ref[...])
        qr = _rope(qn, c64, s64, lo64, 8)
        qc_ref[2 * j] = qn[:, :64].astype(BF16)
        qc_ref[2 * j + 1] = qn[:, 64:].astype(BF16)
        qr_ref[2 * j] = qr[:, :64].astype(BF16)
        qr_ref[2 * j + 1] = qr[:, 64:].astype(BF16)
    tile = pl.program_id(0) % n_tiles
    sel_blk = tile * (ATT_T // NSA_SEL_LEN) + lax.shift_right_logical(
        _row_iota(c64.shape), NSA_SEL_LEN.bit_length() - 1)
    blk_onehot = jnp.where(lane == HEAD_DIM + sel_blk, 1.0, 0.0)
    for g in range(NSA_GROUPS):
        kv = _kv_column(col(8 + g), nk, gks_ref[...], c64k, s64k, lo64, first64)
        kvs_ref[g] = jnp.where(first64, kv, blk_onehot).astype(BF16)
        kvst_ref[g] = kv.T.astype(BF16)
        kv = _kv_column(col(12 + g), nk, gkw_ref[...], c64k, s64k, lo64, first64)
        kvw_ref[g] = kv.astype(BF16)
        kvwt_ref[g] = kv.T.astype(BF16)
    kc_ref[...] = jnp.concatenate([col(16), col(17)], axis=1)
    vc_ref[...] = jnp.concatenate([col(18), col(19)], axis=1)
    gates_t = jax.nn.sigmoid(col(20)).T
    for g in range(NSA_GROUPS):
        gtt_ref[g] = gates_t[12 * g:12 * (g + 1), :]


def _nsa_prep(x2, trig, gmix, w, tabs, B, S):
    T = x2.shape[0]
    tm = ATT_T
    nt = S // tm
    nq, nk, gq, gks, gkw = tabs

    def full(a):
        return pl.BlockSpec(a.shape, lambda i: (0,) * a.ndim)

    def hm(width, heads):
        return pl.BlockSpec((None, heads, tm, width), lambda i: (i // nt, 0, i % nt, 0))

    def hmt(heads):
        return pl.BlockSpec((None, heads, None, 128, tm), lambda i: (i // nt, 0, i % nt, 0, 0))

    def tokm(width):
        return pl.BlockSpec((None, tm, width), lambda i: (i // nt, i % nt, 0))

    out_shape = (
        jax.ShapeDtypeStruct((B, 16, S, 64), BF16),
        jax.ShapeDtypeStruct((B, 16, S, 64), BF16),
        jax.ShapeDtypeStruct((B, 4, S, 128), BF16),
        jax.ShapeDtypeStruct((B, 4, nt, 128, tm), BF16),
        jax.ShapeDtypeStruct((B, 4, S, 128), BF16),
        jax.ShapeDtypeStruct((B, 4, nt, 128, tm), BF16),
        jax.ShapeDtypeStruct((B, S, 256), F32),
        jax.ShapeDtypeStruct((B, S, 256), F32),
        jax.ShapeDtypeStruct((B, 4, 12, S), F32),
    )
    out_specs = (hm(64, 16), hm(64, 16), hm(128, 4), hmt(4), hm(128, 4), hmt(4), tokm(256), tokm(256),
                 pl.BlockSpec((None, 4, 12, tm), lambda i: (i // nt, 0, 0, i % nt)))
    in_specs = [pl.BlockSpec((tm, D_MODEL), lambda i: (i, 0)),
                pl.BlockSpec((tm, 2 * LANES), lambda i: (i, 0)),
                full(gmix), full(w), full(nq), full(nk), full(gq), full(gks), full(gkw)]
    return pl.pallas_call(
        functools.partial(_nsa_prep_kernel, n_tiles=nt), grid=(T // tm,), in_specs=in_specs, out_specs=out_specs,
        out_shape=out_shape, compiler_params=_params(("parallel",)), name="nsa_prep",
    )(x2, trig, gmix, w, nq, nk, gq, gks, gkw)


def _compress_one(x16, pa, pb, w1a, w1b, w2):
    n16 = x16.shape[0]
    h_a = _dot((x16 + pa).astype(BF16), w1a)
    h_b = _dot((x16 + pb).astype(BF16), w1b)
    pre = h_a + pltpu.roll(h_b, n16 - 1, 0)
    act = pre * jax.nn.sigmoid(pre)
    return _dot(act.astype(BF16), w2)


def _compress_kernel(xk_ref, xv_ref, pk_ref, pv_ref, w1k_ref, w1v_ref, w2k_ref, w2v_ref, gk_ref,
                     o_ref, ot_ref):
    half = w1k_ref.shape[0] // 2
    k = _compress_one(xk_ref[...], pk_ref[0:1, :], pk_ref[1:2, :],
                      w1k_ref[:half, :], w1k_ref[half:, :], w2k_ref[...])
    k = _rms_rows(k, gk_ref[...])
    v = _compress_one(xv_ref[...], pv_ref[0:1, :], pv_ref[1:2, :],
                      w1v_ref[:half, :], w1v_ref[half:, :], w2v_ref[...])
    kv = jnp.concatenate([k, v], axis=1)
    o_ref[...] = kv.astype(BF16)
    ot_ref[...] = kv.T.astype(BF16)


def _compress(xk16, xv16, pk, pv, w1k, w1v, w2k, w2v, gk):
    B, G, n16, width = xk16.shape

    def full(a):
        return pl.BlockSpec(a.shape, lambda b, g: (0,) * a.ndim)

    xspec = pl.BlockSpec((None, None, n16, width), lambda b, g: (b, g, 0, 0))
    return pl.pallas_call(
        _compress_kernel, grid=(B, G),
        in_specs=[xspec, xspec, full(pk), full(pv), full(w1k), full(w1v), full(w2k), full(w2v), full(gk)],
        out_specs=(pl.BlockSpec((None, None, n16, 128), lambda b, g: (b, g, 0, 0)),
                   pl.BlockSpec((None, None, 128, n16), lambda b, g: (b, g, 0, 0))),
        out_shape=(jax.ShapeDtypeStruct((B, G, n16, 128), BF16),
                   jax.ShapeDtypeStruct((B, G, 128, n16), BF16)),
        compiler_params=_params(("parallel", "parallel")), name="nsa_compress",
    )(xk16, xv16, pk, pv, w1k, w1v, w2k, w2v, gk)


def _nsa_kernel(qc_ref, qr_ref, kvc_ref, kvct_ref, kvs_ref, kvst_ref, kvw_ref, kvwt_ref, gtt_ref,
                o_ref, sel_ref, part_ref, *flash_refs, n_cmp, n_top):
    t = ATT_T
    HG = NSA_HEADS // NSA_GROUPS
    n_sel = sel_ref.shape[0]
    n16 = kvc_ref.shape[0]
    i = pl.program_id(2)
    t0 = i * t

    kvc = kvc_ref[...]
    kvct = kvct_ref[...]
    n_id = _row_iota((n16, t))
    q_id = t0 + _lane_iota((n16, t))
    visible = (n_id < n_cmp) & (n_id * NSA_CMP_STRIDE + (NSA_CMP_LEN - 1) <= q_id)
    bias_c = jnp.where(visible, 0.0, MASK_BIAS)
    p_sum = jnp.zeros((n16, t), F32)
    o_c = []
    probs, inv_ls, outs = _softmax_direct([_pad_q(qc_ref[j]) for j in range(HG)], kvc, [kvct], bias_c)
    for j in range(HG):
        p_sum = p_sum + probs[j] * inv_ls[j]
        o_c.append(outs[j] * inv_ls[j])

    b_id = _row_iota((n_sel, n16)) * NSA_SEL_LEN
    r_id = _lane_iota((n_sel, n16)) * NSA_CMP_STRIDE
    cover_t = ((r_id < b_id + NSA_SEL_LEN) & (r_id + NSA_CMP_LEN > b_id)
               & (_lane_iota((n_sel, n16)) < n_cmp))
    cover_t = jnp.where(cover_t, 1.0, 0.0).astype(BF16)
    p_hi, p_lo = _split_bf16(p_sum)
    imp = _dot(cover_t, p_hi) + _dot(cover_t, p_lo)

    qs = [_pad_q(qr_ref[j]) for j in range(HG)]
    n_wc = NSA_WINDOW // t + 1
    cw = jnp.maximum(i - (n_wc - 1), 0)
    kw0 = pl.multiple_of(cw * t, t)
    dist = (i - cw) * t + _lane_iota((n_wc * t, t)) - _row_iota((n_wc * t, t))
    bias_w = jnp.where((dist >= 0) & (dist < NSA_WINDOW), 0.0, MASK_BIAS)
    _, inv_lw, out_w = _softmax_direct(qs, kvw_ref[pl.ds(kw0, n_wc * t), :],
                                       [kvwt_ref[cw + u] for u in range(n_wc)], bias_w)
    for j in range(HG):
        part_ref[:, j * t:(j + 1) * t] = (gtt_ref[3 * j:3 * j + 1, :] * o_c[j]
                                          + gtt_ref[3 * j + 2:3 * j + 3, :] * (out_w[j] * inv_lw[j]))

    blk = _row_iota((n_sel, t))
    cur = lax.shift_right_logical(t0 + _lane_iota((n_sel, t)), NSA_SEL_LEN.bit_length() - 1)
    forced = (blk == 0) | (blk == cur) | (blk == cur - 1)
    imp = jnp.where(forced, NSA_FORCE, imp)
    visible_blk = blk <= cur
    imp = jnp.where(visible_blk, imp, NEG_INF)
    n_larger = jnp.zeros((n_sel, t), F32)
    for m in range(n_sel):
        n_larger = n_larger + jnp.where(imp[m:m + 1, :] > imp, 1.0, 0.0)
    sel_fast = n_larger < n_top
    n_picked = jnp.sum(jnp.where(sel_fast & visible_blk, 1.0, 0.0), axis=0, keepdims=True)
    n_wanted = jnp.minimum(cur[0:1, :] + 1, n_top).astype(F32)
    sel_ref[...] = jnp.where(sel_fast, 0.0, MASK_BIAS)

    @pl.when(jnp.max(jnp.abs(n_picked - n_wanted)) > 0.0)
    def _():
        sel_ref[...] = (_rank_select_t(imp, None, n_top) - 1.0) * (-MASK_BIAS)

    sel_bias = sel_ref[...]
    qs_sel = [_bias_lanes(q, sel_bias) for q in qs]
    flash = _Flash(*flash_refs, t)
    flash.reset()

    def sel_operands(cc):
        c0 = 2 * cc
        k0 = pl.multiple_of(c0 * t, 2 * t)
        return c0, [kvs_ref[pl.ds(k0, 2 * t), :]] * HG, [[kvst_ref[c0], kvst_ref[c0 + 1]]] * HG

    def past_pair(cc):
        _, kvs, kvts = sel_operands(cc)
        return kvs, kvts, [None] * HG

    flash.run(qs_sel, i // 2, past_pair)
    causal = _causal_bias(t)

    @pl.when(i % 2 == 0)
    def _():
        k0 = pl.multiple_of(i * t, t)
        flash.update(qs_sel, [kvs_ref[pl.ds(k0, t), :]] * HG, [[kvst_ref[i]]] * HG, [causal] * HG)

    @pl.when(i % 2 == 1)
    def _():
        _, kvs, kvts = sel_operands(i // 2)
        bias = jnp.concatenate([jnp.zeros((t, t), F32), causal], axis=0)
        flash.update(qs_sel, kvs, kvts, [bias] * HG)

    o_s = [flash.result(j) for j in range(HG)]

    heads = [part_ref[:, j * t:(j + 1) * t] + gtt_ref[3 * j + 1:3 * j + 2, :] * o_s[j] for j in range(HG)]
    _store_heads(o_ref, heads)


def _nsa_attention(qc, qr, kvc, kvct, kvs, kvst, kvw, kvwt, gates_t):
    B, H, S, _ = qc.shape
    G = NSA_GROUPS
    HG = H // G
    t = ATT_T
    nt = S // t
    n16 = kvc.shape[2]
    n_cmp = (S - NSA_CMP_LEN) // NSA_CMP_STRIDE + 1
    n_sel = S // NSA_SEL_LEN
    n_top = min(NSA_SEL_TOPK, n_sel)
    qspec = pl.BlockSpec((None, HG, t, 64), lambda b, g, i: (b, g, i, 0))
    kvspec = pl.BlockSpec((None, None, S, 128), lambda b, g, i: (b, g, 0, 0))
    kvtspec = pl.BlockSpec((None, None, nt, 128, t), lambda b, g, i: (b, g, 0, 0, 0))
    in_specs = [qspec, qspec,
                pl.BlockSpec((None, None, n16, 128), lambda b, g, i: (b, g, 0, 0)),
                pl.BlockSpec((None, None, 128, n16), lambda b, g, i: (b, g, 0, 0)),
                kvspec, kvtspec, kvspec, kvtspec,
                pl.BlockSpec((None, None, 12, t), lambda b, g, i: (b, g, 0, i))]
    return pl.pallas_call(
        functools.partial(_nsa_kernel, n_cmp=n_cmp, n_top=n_top), grid=(B, G, nt), in_specs=in_specs,
        out_specs=pl.BlockSpec((None, t, HG * 64), lambda b, g, i: (b, i, g)),
        out_shape=jax.ShapeDtypeStruct((B, S, H * 64), BF16),
        scratch_shapes=[pltpu.VMEM((n_sel, t), F32), pltpu.VMEM((LANES, HG * t), F32)]
                       + _flash_scratch(HG, t, 2 * t),
        compiler_params=_params(("parallel", "parallel", "parallel")), name="nsa_attention",
    )(qc, qr, kvc, kvct, kvs, kvst, kvw, kvwt, gates_t)


def _rope_freq_row(period, rot):
    half = rot // 2
    inv_freq = ROPE_THETA ** (-(jnp.arange(half, dtype=F32) * 2.0 / rot))
    lane = jnp.arange(LANES) % period
    f = jnp.where(lane < rot, inv_freq[lane % half], 0.0)
    return f.reshape(1, LANES).astype(F32)


def _norm_matrices():
    r = jnp.arange(LANES)
    same = (r[:, None] // 64) == (r[None, :] // 64)
    nq = jnp.where(same, 1.0 / 64, 0.0).astype(BF16)
    nk = jnp.where(same & (r[:, None] < 64), 1.0 / 64, 0.0).astype(BF16)
    return nq, nk


def _q_gain(g):
    return (jnp.tile(g.astype(F32), 2) * Q_SCALE).reshape(1, LANES)


def _k_gain(g):
    return jnp.concatenate([g.astype(F32), jnp.ones((64,), F32)]).reshape(1, LANES)


def _interleave_kv(wk, wv, n_heads):
    d = wk.shape[0]
    wk = wk.reshape(d, n_heads, 64)
    wv = wv.reshape(d, n_heads, 64)
    return jnp.concatenate([wk, wv], axis=2).reshape(d, n_heads * 128)


def _split_cols(w, sizes):
    out, start = [], 0
    for n in sizes:
        out.append(w[:, start:start + n])
        start += n
    return out


def _mixer_layer0(x2, trig, B, S, gmix, w_in, w_out, a_q_norm, a_k_norm, b_q_norm, b_k_norm):
    sizes = (512, 64, 64, 256, 32, 8, 512, 512, 512)
    waq, wak, wav, wiq, wik, wiw, wbq, wbk, wbv = _split_cols(w_in, sizes)
    pad = jnp.zeros((D_MODEL, LANES - 40), w_in.dtype)
    w = jnp.concatenate([waq, wbq, _interleave_kv(wbk, wbv, 8), wak, wav, wiq, wik, wiw, pad],
                        axis=1).astype(BF16)
    nq, nk = _norm_matrices()
    tabs = (nq, nk, _q_gain(a_q_norm), _q_gain(b_q_norm), _k_gain(a_k_norm), _k_gain(b_k_norm))
    aq, bq, bkv, bkvt, akv, akvt, iq, ik, iwt, km = _ab_prep(x2, trig, gmix, w, tabs, B, S)
    n_blk = S // MOBA_BLOCK
    kmean = km.reshape(B, n_blk, 8, 128).transpose(0, 2, 1, 3)
    o_a = _dsa_attention(iq, iwt, ik, aq, akv, akvt).reshape(B * S, 512)
    o_b = _moba_attention(bq, bkv, bkvt, kmean).reshape(B * S, 512)
    w_out = w_out.astype(BF16)
    return [o_a, o_b], [w_out[:512], w_out[512:]]


def _mixer_layer1(x2, trig, B, S, gmix, w_in, w_out, q_norm, kcmp_norm, ksel_norm, kwin_norm,
                  pos_k, pos_v, w1_k, w2_k, w1_v, w2_v):
    G = NSA_GROUPS
    sizes = (1024,) + (256,) * 6 + (48,)
    wq, wkc, wvc, wks, wvs, wkw, wvw, wgt = _split_cols(w_in, sizes)
    pad = jnp.zeros((D_MODEL, LANES - 48), w_in.dtype)
    w = jnp.concatenate([wq, _interleave_kv(wks, wvs, G), _interleave_kv(wkw, wvw, G),
                         wkc, wvc, wgt, pad], axis=1).astype(BF16)
    nq, nk = _norm_matrices()
    tabs = (nq, nk, _q_gain(q_norm), _k_gain(ksel_norm), _k_gain(kwin_norm))
    qc, qr, kvs, kvst, kvw, kvwt, kc_raw, vc_raw, gates_t = _nsa_prep(x2, trig, gmix, w, tabs, B, S)

    n16 = S // NSA_CMP_STRIDE

    def blocks16(t):
        return (t.reshape(B, n16, NSA_CMP_STRIDE, G, HEAD_DIM).transpose(0, 3, 1, 2, 4)
                .reshape(B, G, n16, NSA_CMP_STRIDE * HEAD_DIM))

    def pos_rows(p):
        return p.astype(F32).reshape(2, NSA_CMP_STRIDE * HEAD_DIM)

    kvc, kvct = _compress(blocks16(kc_raw), blocks16(vc_raw), pos_rows(pos_k), pos_rows(pos_v),
                          w1_k.astype(BF16), w1_v.astype(BF16), w2_k.astype(BF16), w2_v.astype(BF16),
                          kcmp_norm.astype(F32).reshape(1, HEAD_DIM))
    o = _nsa_attention(qc, qr, kvc, kvct, kvs, kvst, kvw, kvwt, gates_t)
    return [o.reshape(B * S, NSA_HEADS * HEAD_DIM)], [w_out.astype(BF16)]


def _finish_layer(parts, weights, x2, mem, S, g_mem, g_src, w_q, w_kv, w_o, q_norm, k_norm,
                  g_ffn, ffn_w_in, ffn_w_out):
    row = lambda v: v.astype(F32).reshape(1, -1)
    kv = _mem_kv(mem, row(g_src), w_kv.astype(BF16), row(k_norm))
    wg = ffn_w_in[:, :D_FF].astype(BF16)
    wu = ffn_w_in[:, D_FF:].astype(BF16)
    return _post_mixer(parts, weights, x2, row(g_mem), w_q.astype(BF16), row(q_norm), kv, w_o.astype(BF16),
                       row(g_ffn), wg, wu, ffn_w_out.astype(BF16), S)


def kernel(x, mem, positions, norm_mix, norm_mem, norm_mem_src, norm_ffn, ab_w_in, ab_w_out, dsa_q_norm, dsa_k_norm, moba_q_norm, moba_k_norm, nsa_w_in, nsa_w_out, nsa_q_norm, nsa_kcmp_norm, nsa_ksel_norm, nsa_kwin_norm, nsa_cmp_pos_k, nsa_cmp_pos_v, nsa_cmp_w1_k, nsa_cmp_w2_k, nsa_cmp_w1_v, nsa_cmp_w2_v, mem_w_q, mem_w_kv, mem_w_o, mem_q_norm, mem_k_norm, ffn_w_in, ffn_w_out):
    B, S, D = x.shape
    depth = norm_mix.shape[0]
    x2 = x.reshape(B * S, D)
    trig = _rope_trig(positions.astype(F32).reshape(B * S, 1), _rope_freq_row(64, 16), _rope_freq_row(32, 8))
    row = lambda v: v.astype(F32).reshape(1, -1)
    for i in range(depth):
        j = i // 2
        if i % 2 == 0:
            parts, weights = _mixer_layer0(x2, trig, B, S, row(norm_mix[i]), ab_w_in[j], ab_w_out[j],
                               dsa_q_norm[j], dsa_k_norm[j], moba_q_norm[j], moba_k_norm[j])
        else:
            parts, weights = _mixer_layer1(x2, trig, B, S, row(norm_mix[i]), nsa_w_in[j], nsa_w_out[j],
                               nsa_q_norm[j], nsa_kcmp_norm[j], nsa_ksel_norm[j], nsa_kwin_norm[j],
                               nsa_cmp_pos_k[j], nsa_cmp_pos_v[j], nsa_cmp_w1_k[j], nsa_cmp_w2_k[j],
                               nsa_cmp_w1_v[j], nsa_cmp_w2_v[j])
        x2 = _finish_layer(parts, weights, x2, mem, S, norm_mem[i], norm_mem_src[i], mem_w_q[i], mem_w_kv[i],
                           mem_w_o[i], mem_q_norm[i], mem_k_norm[i], norm_ffn[i], ffn_w_in[i], ffn_w_out[i])
    return x2.reshape(B, S, D)
```

```python
import functools
import math

import jax
import jax.numpy as jnp
from jax import lax
from jax.experimental import pallas as pl
from jax.experimental.pallas import tpu as pltpu

F32 = jnp.float32
BF16 = jnp.bfloat16
I32 = jnp.int32
I16 = jnp.int16

D_MODEL = 1024
N_MEM = 256
HEAD_DIM = 64
ROPE_THETA = 500000.0
RMS_EPS = 1e-6
NEG_INF = -1e30
TINY = 1e-20

DSA_HEADS = 8
DSA_IDX_HEADS = 8
DSA_IDX_DIM = 32
DSA_TOPK = 256
MOBA_HEADS = 8
MOBA_BLOCK = 256
MOBA_TOPK = 3
NSA_HEADS = 16
NSA_GROUPS = 4
NSA_CMP_LEN = 32
NSA_CMP_STRIDE = 16
NSA_SEL_LEN = 64
NSA_SEL_TOPK = 16
NSA_WINDOW = 512
NSA_FORCE = 1e4
MEM_HEADS = 4
MEM_HEAD_DIM = 128
D_FF = ((8 * D_MODEL + 3 * 256 - 1) // (3 * 256)) * 256

LANES = 128
SUBLANES = 8
INT_MIN = -(2 ** 31)
VMEM_LIMIT = 56 * 1024 * 1024

PROJ_GROUP = 4
ATT_T = 256
MASK_BIAS = -1e30
M_FLOOR = -1e29
LOG2E = math.log2(math.e)
Q_SCALE = HEAD_DIM ** -0.5 * LOG2E

NT_DIMS = (((1,), (1,)), ((), ()))


def _dot(a, b):
    return jnp.dot(a, b, preferred_element_type=F32)


def _dot_nt(a, b):
    return lax.dot_general(a, b, NT_DIMS, preferred_element_type=F32)


def _split_bf16(a):
    hi = a.astype(BF16)
    return hi, (a - hi.astype(F32)).astype(BF16)


def _split_dot(a, b):
    hi, lo = _split_bf16(a)
    return _dot(hi, b) + _dot(lo, b)


def _rms_rows(x, gain):
    ms = jnp.mean(x * x, axis=-1, keepdims=True)
    return x * lax.rsqrt(ms + RMS_EPS) * gain


def _params(sem):
    return pltpu.CompilerParams(dimension_semantics=sem, vmem_limit_bytes=VMEM_LIMIT)


def _head_norm(y, norm_m, gain):
    ms = _split_dot(y * y, norm_m)
    return y * lax.rsqrt(ms + RMS_EPS) * gain


def _rope(y, c, s, lo_mask, half):
    sw = jnp.where(lo_mask, pltpu.roll(y, LANES - half, 1), pltpu.roll(y, half, 1))
    return y * c + sw * s


def _lane_iota(shape):
    return lax.broadcasted_iota(I32, shape, 1)


def _row_iota(shape):
    return lax.broadcasted_iota(I32, shape, 0)


def _rope_tables(pos, ftab, period, half):
    ang = pos * ftab
    lane = _lane_iota(ang.shape) % period
    c = jnp.cos(ang)
    s = jnp.sin(ang) * jnp.where(lane < half, -1.0, 1.0)
    return c, s


class _ColumnProjector:
    def __init__(self, xn, w_ref):
        self.xn, self.w_ref, self.groups = xn, w_ref, {}

    def __call__(self, j):
        g, u = divmod(j, PROJ_GROUP)
        if g not in self.groups:
            width = PROJ_GROUP * LANES
            lo = g * width
            hi = min(lo + width, self.w_ref.shape[1])
            self.groups[g] = _dot(self.xn, self.w_ref[:, lo:hi])
        return self.groups[g][:, u * LANES:(u + 1) * LANES]


def _kv_column(yc, nk, gain, c64k, s64k, lo64, first64):
    kn = jnp.where(first64, _head_norm(yc, nk, gain), yc)
    return _rope(kn, c64k, s64k, lo64, 8)


def _rope_trig_kernel(pos_ref, f64_ref, f32_ref, o_ref):
    pos = pos_ref[...]
    c64, s64 = _rope_tables(pos, f64_ref[...], 64, 8)
    c32, s32 = _rope_tables(pos, f32_ref[...], 32, 4)
    o_ref[...] = jnp.concatenate([c64, s64, c32, s32], axis=1)


def _rope_trig(pos2, f64, f32t, tm=1024):
    T = pos2.shape[0]
    return pl.pallas_call(
        _rope_trig_kernel, grid=(T // tm,),
        in_specs=[pl.BlockSpec((tm, 1), lambda i: (i, 0)),
                  pl.BlockSpec(f64.shape, lambda i: (0, 0)), pl.BlockSpec(f32t.shape, lambda i: (0, 0))],
        out_specs=pl.BlockSpec((tm, 4 * LANES), lambda i: (i, 0)),
        out_shape=jax.ShapeDtypeStruct((T, 4 * LANES), F32),
        compiler_params=_params(("parallel",)), name="rope_trig",
    )(pos2, f64, f32t)


def _ab_prep_kernel(x_ref, trig_ref, gmix_ref, w_ref, nq_ref, nk_ref,
                    gaq_ref, gbq_ref, gak_ref, gbk_ref,
                    aq_ref, bq_ref, bkv_ref, bkvt_ref, akv_ref, akvt_ref, iq_ref, ik_ref, iwt_ref, km_ref,
                    *, n_tiles):
    xn = _rms_rows(x_ref[...], gmix_ref[...]).astype(BF16)
    c64, s64, c32, s32 = [trig_ref[:, j * LANES:(j + 1) * LANES] for j in range(4)]
    lane = _lane_iota(c64.shape)
    lo64 = (lane % 64) < 8
    lo32 = (lane % 32) < 4
    first64 = lane < 64
    c64k = jnp.where(first64, c64, 1.0)
    s64k = jnp.where(first64, s64, 0.0)
    first32 = lane < 32
    c32k = jnp.where(first32, c32, 1.0)
    s32k = jnp.where(first32, s32, 0.0)
    nq = nq_ref[...]
    nk = nk_ref[...]

    col = _ColumnProjector(xn, w_ref)

    for j in range(4):
        q = _rope(_head_norm(col(j), nq, gaq_ref[...]), c64, s64, lo64, 8)
        aq_ref[2 * j] = q[:, :64].astype(BF16)
        aq_ref[2 * j + 1] = q[:, 64:].astype(BF16)
    for j in range(4):
        q = _rope(_head_norm(col(4 + j), nq, gbq_ref[...]), c64, s64, lo64, 8)
        bq_ref[2 * j] = q[:, :64].astype(BF16)
        bq_ref[2 * j + 1] = q[:, 64:].astype(BF16)
    blk_onehot = jnp.where(lane == HEAD_DIM + pl.program_id(0) % n_tiles, 1.0, 0.0)
    for h in range(8):
        kv = _kv_column(col(8 + h), nk, gbk_ref[...], c64k, s64k, lo64, first64)
        bkv_ref[h] = jnp.where(first64, kv, blk_onehot).astype(BF16)
        bkvt_ref[h] = kv.T.astype(BF16)
        km_ref[h:h + 1, :] = jnp.mean(kv, axis=0, keepdims=True)
    kv = _kv_column(col(16), nk, gak_ref[...], c64k, s64k, lo64, first64)
    akv_ref[...] = kv.astype(BF16)
    akvt_ref[...] = kv.T.astype(BF16)
    for j in range(2):
        q = _rope(col(17 + j), c32, s32, lo32, 4)
        for u in range(4):
            iq_ref[4 * j + u] = q[:, 32 * u:32 * (u + 1)].astype(BF16)
    yc = col(19)
    ik_ref[...] = _rope(yc, c32k, s32k, lo32, 4)[:, :32].astype(BF16)
    iwt_ref[...] = yc.T[32:40, :]


def _ab_prep(x2, trig, gmix, w, tabs, B, S):
    T = x2.shape[0]
    tm = ATT_T
    nt = S // tm
    n_cols = w.shape[1]
    nq, nk, gaq, gbq, gak, gbk = tabs

    def full(a):
        return pl.BlockSpec(a.shape, lambda i: (0,) * a.ndim)

    def hm(width, heads=8):
        return pl.BlockSpec((None, heads, tm, width), lambda i: (i // nt, 0, i % nt, 0))

    def tokm(width):
        return pl.BlockSpec((None, tm, width), lambda i: (i // nt, i % nt, 0))

    out_shape = (
        jax.ShapeDtypeStruct((B, 8, S, 64), BF16),
        jax.ShapeDtypeStruct((B, 8, S, 64), BF16),
        jax.ShapeDtypeStruct((B, 8, S, 128), BF16),
        jax.ShapeDtypeStruct((B, 8, nt, 128, tm), BF16),
        jax.ShapeDtypeStruct((B, S, 128), BF16),
        jax.ShapeDtypeStruct((B, nt, 128, tm), BF16),
        jax.ShapeDtypeStruct((B, 8, S, 32), BF16),
        jax.ShapeDtypeStruct((B, S, 32), BF16),
        jax.ShapeDtypeStruct((B, 8, S), F32),
        jax.ShapeDtypeStruct((T // tm, 8, 128), F32),
    )
    out_specs = (hm(64), hm(64), hm(128),
                 pl.BlockSpec((None, 8, None, 128, tm), lambda i: (i // nt, 0, i % nt, 0, 0)),
                 tokm(128),
                 pl.BlockSpec((None, None, 128, tm), lambda i: (i // nt, i % nt, 0, 0)),
                 hm(32), tokm(32),
                 pl.BlockSpec((None, 8, tm), lambda i: (i // nt, 0, i % nt)),
                 pl.BlockSpec((None, 8, 128), lambda i: (i, 0, 0)))
    in_specs = [pl.BlockSpec((tm, D_MODEL), lambda i: (i, 0)),
                pl.BlockSpec((tm, 4 * LANES), lambda i: (i, 0)),
                full(gmix), pl.BlockSpec((D_MODEL, n_cols), lambda i: (0, 0)),
                full(nq), full(nk), full(gaq), full(gbq), full(gak), full(gbk)]
    return pl.pallas_call(
        functools.partial(_ab_prep_kernel, n_tiles=nt), grid=(T // tm,), in_specs=in_specs, out_specs=out_specs,
        out_shape=out_shape, compiler_params=_params(("parallel",)), name="ab_prep",
    )(x2, trig, gmix, w, nq, nk, gaq, gbq, gak, gbk)


def _pad_q(q):
    return jnp.concatenate([q, jnp.zeros_like(q)], axis=1)


def _bias_lanes(q, rows):
    n, tq = rows.shape
    parts = [jnp.zeros((HEAD_DIM, tq), F32), rows]
    if n < HEAD_DIM:
        parts.append(jnp.zeros((HEAD_DIM - n, tq), F32))
    lanes = jnp.concatenate(parts, axis=0).T.astype(BF16)
    return jnp.where(_lane_iota(q.shape) < HEAD_DIM, q, lanes)


class _Flash:
    def __init__(self, m_ref, l_ref, acc_ref, s_ref, cmax_ref, p_ref, tq):
        self.m_ref, self.l_ref, self.acc_ref, self.tq = m_ref, l_ref, acc_ref, tq
        self.s_ref, self.cmax_ref, self.p_ref = s_ref, cmax_ref, p_ref

    def reset(self):
        self.m_ref[...] = jnp.full(self.m_ref.shape, M_FLOOR, F32)
        self.l_ref[...] = jnp.zeros(self.l_ref.shape, F32)
        self.acc_ref[...] = jnp.zeros(self.acc_ref.shape, F32)

    def _scores(self, buf, qs, kvs, biases):
        tq = self.tq
        for i in range(len(qs)):
            s = _dot_nt(kvs[i], qs[i])
            if biases[i] is not None:
                s = s + biases[i]
            self.s_ref[buf, i, :s.shape[0], :] = s
            self.cmax_ref[buf, :, i * tq:(i + 1) * tq] = jnp.max(s, axis=0, keepdims=True)

    def update(self, qs, kvs, kvts, biases):
        self._scores(0, qs, kvs, biases)
        self._finish(0, kvts)

    def run(self, qs, count, operands):
        def scores(c, buf):
            kvs, _, biases = operands(c)
            self._scores(buf, qs, kvs, biases)

        def finish(c, buf):
            self._finish(buf, operands(c)[1])

        last = jnp.maximum(count - 1, 0)
        scores(0, 0)

        def two_chunks(pp, carry):
            c = 2 * pp
            scores(c + 1, 1)
            finish(c, 0)
            scores(jnp.minimum(c + 2, last), 0)
            finish(c + 1, 1)
            return carry

        lax.fori_loop(0, count // 2, two_chunks, 0)

        @pl.when(count % 2 == 1)
        def _():
            finish(count - 1, 0)

    def _finish(self, buf, kvts):
        n = len(kvts)
        tq = self.tq
        kc = sum(kvt.shape[1] for kvt in kvts[0])
        alphas = []
        for i in range(n):
            cols = slice(i * tq, (i + 1) * tq)
            m = self.m_ref[:, cols]
            m_new = jnp.maximum(m, self.cmax_ref[buf, :, cols])
            p = jnp.exp2(self.s_ref[buf, i, :kc, :] - m_new)
            alpha = jnp.exp2(m - m_new)
            self.m_ref[:, cols] = m_new
            self.l_ref[:, cols] = alpha * self.l_ref[:, cols] + p.reshape(-1, SUBLANES, tq).sum(axis=0)
            self.p_ref[i, :kc, :] = p.astype(BF16)
            alphas.append(alpha)
        for i in range(n):
            cols = slice(i * tq, (i + 1) * tq)
            pv, r0 = None, 0
            for kvt in kvts[i]:
                part = _dot(kvt, self.p_ref[i, r0:r0 + kvt.shape[1], :])
                pv = part if pv is None else pv + part
                r0 += kvt.shape[1]
            self.acc_ref[:, cols] = alphas[i] * self.acc_ref[:, cols] + pv

    def result(self, slot):
        cols = slice(slot * self.tq, (slot + 1) * self.tq)
        l = jnp.sum(self.l_ref[:, cols], axis=0, keepdims=True)
        return self.acc_ref[:, cols] / jnp.maximum(l, TINY)


def _flash_scratch(n_slots, tq, kc):
    return [pltpu.VMEM((1, n_slots * tq), F32), pltpu.VMEM((SUBLANES, n_slots * tq), F32),
            pltpu.VMEM((LANES, n_slots * tq), F32),
            pltpu.VMEM((2, n_slots, kc, tq), F32), pltpu.VMEM((2, 1, n_slots * tq), F32),
            pltpu.VMEM((n_slots, kc, tq), BF16)]


def _softmax_direct(qs, kv, kvts, bias):
    scores = [_dot_nt(kv, q) for q in qs]
    probs, inv_ls = [], []
    for s in scores:
        s = s + bias
        m = jnp.maximum(jnp.max(s, axis=0, keepdims=True), M_FLOOR)
        p = jnp.exp2(s - m)
        inv_ls.append(1.0 / jnp.maximum(jnp.sum(p, axis=0, keepdims=True), TINY))
        probs.append(p)
    outs = []
    for p in probs:
        pb = p.astype(BF16)
        o, r0 = None, 0
        for kvt in kvts:
            part = _dot(kvt, pb[r0:r0 + kvt.shape[1]])
            o = part if o is None else o + part
            r0 += kvt.shape[1]
        outs.append(o)
    return probs, inv_ls, outs


def _causal_bias(t):
    return jnp.where(_row_iota((t, t)) <= _lane_iota((t, t)), 0.0, MASK_BIAS)


def _store_heads(o_ref, heads_t):
    tq = heads_t[0].shape[1]
    lane = _lane_iota((tq, LANES))
    for u in range(len(heads_t) // 2):
        even = pltpu.roll(heads_t[2 * u].T, 64, 1)
        odd = heads_t[2 * u + 1].T
        o_ref[:, u * LANES:(u + 1) * LANES] = jnp.where(lane < 64, even, odd).astype(o_ref.dtype)


def _rank_select_t(v, n_valid, n_top):
    n = v.shape[0]
    row = _row_iota(v.shape)
    rank = jnp.zeros(v.shape, F32)
    for m in range(n):
        vm = v[m:m + 1, :]
        ahead = (vm > v) | ((vm == v) & (m < row))
        if n_valid is not None:
            ahead = ahead & (m < n_valid)
        rank = rank + jnp.where(ahead, 1.0, 0.0)
    sel = rank < n_top
    if n_valid is not None:
        sel = sel & (row < n_valid)
    return jnp.where(sel, 1.0, 0.0)


def _dsa_kernel(iq_ref, iwt_ref, ik_ref, aq_ref, akv_ref, akvt_ref, o_ref,
                sk_ref, half_ref, bias_ref, xcut_ref, *flash_refs, k_top, index_bits):
    t = ATT_T
    i = pl.program_id(1)
    n_ch = i + 1
    kio = _row_iota((t, t))
    qio = _lane_iota((t, t))

    def causal(c):
        return (c - i) * t + kio <= qio

    def score_chunk(c):
        k0 = pl.multiple_of(c * t, t)
        ikc = ik_ref[pl.ds(k0, t), :]
        sc = jnp.zeros((t, t), F32)
        for h in range(DSA_IDX_HEADS):
            logit = _dot_nt(ikc, iq_ref[h])
            sc = sc + iwt_ref[h:h + 1, :] * jnp.maximum(logit, 0.0)
        sc = jnp.where(sc == 0.0, 0.0, sc)
        bits = pltpu.bitcast(sc, I32)
        key = bits ^ ((bits >> 31) & 0x7FFFFFFF)
        key = jnp.where(causal(c), key, INT_MIN)
        sk_ref[c] = key
        half_ref[c] = (key >> 16).astype(I16)

    def score_pair(cc, carry):
        score_chunk(2 * cc)
        score_chunk(2 * cc + 1)
        return carry

    lax.fori_loop(0, (n_ch + 1) // 2, score_pair, 0)

    def count(pred):
        def body(c, acc8):
            ind = jnp.where(pred(sk_ref[c], c), 1.0, 0.0)
            return acc8 + ind.reshape(-1, SUBLANES, t).sum(axis=0)
        acc8 = lax.fori_loop(0, n_ch, body, jnp.zeros((SUBLANES, t), F32))
        return jnp.sum(acc8, axis=0, keepdims=True)

    def count_half(cand):
        rows = 2 * SUBLANES

        def body(cc, acc):
            parts = []
            for c in (2 * cc, 2 * cc + 1):
                ind = jnp.where(half_ref[c] >= cand, jnp.bfloat16(1), jnp.bfloat16(0))
                parts += [ind[rows * j:rows * (j + 1), :] for j in range(t // rows)]
            while len(parts) > 1:
                parts = [parts[2 * j] + parts[2 * j + 1] for j in range(len(parts) // 2)]
            return acc + parts[0].astype(F32)
        acc = lax.fori_loop(0, (n_ch + 1) // 2, body, jnp.zeros((rows, t), F32))
        return jnp.sum(acc, axis=0, keepdims=True)

    def half_search():
        def bit_step(b, v):
            cand = v + lax.shift_left(jnp.int32(1), 15 - b)
            return jnp.where(count_half(cand.astype(I16)) >= k_top, cand, v)
        return lax.fori_loop(0, 16, bit_step, jnp.full((1, t), -(2 ** 15), I32))

    thr_hi = half_search()

    def low_half_chunk(c, carry):
        key = sk_ref[c]
        hi = key >> 16
        lo = (key & 0xFFFF) - 2 ** 15
        half_ref[c] = jnp.where(hi > thr_hi, 2 ** 15 - 1, jnp.where(hi < thr_hi, -(2 ** 15), lo)).astype(I16)
        return carry

    lax.fori_loop(0, n_ch, low_half_chunk, 0)
    thr = lax.shift_left(thr_hi, 16) + (half_search() + 2 ** 15)

    need = k_top - count(lambda blk, c: blk > thr)
    n_ge = count(lambda blk, c: blk >= thr)
    xcut_ref[...] = jnp.full((1, t), 2 ** 30, I32)

    @pl.when(jnp.max(n_ge) > k_top)
    def _():
        def x_step(b, x):
            cand = x + lax.shift_left(jnp.int32(1), index_bits - 1 - b)
            ties_below = count(lambda blk, c: (blk == thr) & (c * t + kio < cand))
            return jnp.where(ties_below <= need, cand, x)
        xcut_ref[...] = lax.fori_loop(0, index_bits, x_step, jnp.zeros((1, t), I32))

    xcut = xcut_ref[...]

    n_pairs = (n_ch + 1) // 2

    def bias_chunk(c, carry):
        blk = sk_ref[jnp.minimum(c, i)]
        keep = (blk > thr) | ((blk == thr) & (c * t + kio < xcut))
        bias_ref[c] = jnp.where(keep & causal(c), 0.0, MASK_BIAS)
        return carry

    lax.fori_loop(0, 2 * n_pairs, bias_chunk, 0)

    flash = _Flash(*flash_refs, t)
    flash.reset()
    qs = [_pad_q(aq_ref[h]) for h in range(DSA_HEADS)]

    n = DSA_HEADS

    def att_pair(cc):
        c0 = 2 * cc
        k0 = pl.multiple_of(c0 * t, 2 * t)
        kv = akv_ref[pl.ds(k0, 2 * t), :]
        bias = jnp.concatenate([bias_ref[c0], bias_ref[c0 + 1]], axis=0)
        return [kv] * n, [[akvt_ref[c0], akvt_ref[c0 + 1]]] * n, [bias] * n

    flash.run(qs, n_pairs, att_pair)
    _store_heads(o_ref, [flash.result(h) for h in range(DSA_HEADS)])


def _dsa_attention(iq, iwt, ik, aq, akv, akvt):
    B, _, S, _ = aq.shape
    t = ATT_T
    nt = S // t
    k_top = min(DSA_TOPK, S // 4)
    in_specs = [
        pl.BlockSpec((None, 8, t, 32), lambda b, i: (b, 0, i, 0)),
        pl.BlockSpec((None, 8, t), lambda b, i: (b, 0, i)),
        pl.BlockSpec((None, S, 32), lambda b, i: (b, 0, 0)),
        pl.BlockSpec((None, 8, t, 64), lambda b, i: (b, 0, i, 0)),
        pl.BlockSpec((None, S, 128), lambda b, i: (b, 0, 0)),
        pl.BlockSpec((None, nt, 128, t), lambda b, i: (b, 0, 0, 0)),
    ]
    return pl.pallas_call(
        functools.partial(_dsa_kernel, k_top=k_top, index_bits=S.bit_length()),
        grid=(B, nt), in_specs=in_specs,
        out_specs=pl.BlockSpec((None, t, 512), lambda b, i: (b, i, 0)),
        out_shape=jax.ShapeDtypeStruct((B, S, 512), BF16),
        scratch_shapes=[pltpu.VMEM((nt, t, t), I32), pltpu.VMEM((nt, t, t), I16), pltpu.VMEM((nt, t, t), F32),
                        pltpu.VMEM((1, t), I32)] + _flash_scratch(DSA_HEADS, t, 2 * t),
        compiler_params=_params(("parallel", "parallel")), name="dsa_attention",
    )(iq, iwt, ik, aq, akv, akvt)


MOBA_HPS = 4


def _moba_kernel(q_ref, kv_ref, kvt_ref, km_ref, o_ref, *flash_refs, n_top):
    t = ATT_T
    own = pl.program_id(2)
    causal = _causal_bias(t)
    flash = _Flash(*flash_refs, t)
    flash.reset()
    qs = []
    for hh in range(MOBA_HPS):
        q = _pad_q(q_ref[hh])
        km_hi, km_lo = _split_bf16(km_ref[hh])
        gate = _dot_nt(km_hi, q) + _dot_nt(km_lo, q)
        keep = _rank_select_t(gate, own, n_top)
        keep = jnp.where(_row_iota(keep.shape) == own, 1.0, keep)
        qs.append(_bias_lanes(q, (keep - 1.0) * (-MASK_BIAS)))

    def operands(cc):
        n0 = 2 * cc
        k0 = pl.multiple_of(n0 * t, 2 * t)
        heads = range(MOBA_HPS)
        return (n0, [kv_ref[hh, pl.ds(k0, 2 * t), :] for hh in heads],
                [[kvt_ref[hh, n0], kvt_ref[hh, n0 + 1]] for hh in heads])

    def past_pair(cc):
        _, kvs, kvts = operands(cc)
        return kvs, kvts, [None] * MOBA_HPS

    flash.run(qs, own // 2, past_pair)
    heads = range(MOBA_HPS)

    @pl.when(own % 2 == 0)
    def _():
        k0 = pl.multiple_of(own * t, t)
        flash.update(qs, [kv_ref[hh, pl.ds(k0, t), :] for hh in heads],
                     [[kvt_ref[hh, own]] for hh in heads], [causal] * MOBA_HPS)

    @pl.when(own % 2 == 1)
    def _():
        _, kvs, kvts = operands(own // 2)
        bias = jnp.concatenate([jnp.zeros((t, t), F32), causal], axis=0)
        flash.update(qs, kvs, kvts, [bias] * MOBA_HPS)

    _store_heads(o_ref, [flash.result(hh) for hh in range(MOBA_HPS)])


def _moba_attention(bq, bkv, bkvt, kmean):
    B, H, S, _ = bq.shape
    t = ATT_T
    hps = MOBA_HPS
    n_blk = S // MOBA_BLOCK
    assert t == MOBA_BLOCK and n_blk % 2 == 0 and H % hps == 0
    n_top = max(1, min(MOBA_TOPK, n_blk - 1))
    in_specs = [
        pl.BlockSpec((None, hps, t, 64), lambda b, h, i: (b, h, i, 0)),
        pl.BlockSpec((None, hps, S, 128), lambda b, h, i: (b, h, 0, 0)),
        pl.BlockSpec((None, hps, n_blk, 128, t), lambda b, h, i: (b, h, 0, 0, 0)),
        pl.BlockSpec((None, hps, n_blk, 128), lambda b, h, i: (b, h, 0, 0)),
    ]
    return pl.pallas_call(
        functools.partial(_moba_kernel, n_top=n_top), grid=(B, H // hps, S // t), in_specs=in_specs,
        out_specs=pl.BlockSpec((None, t, hps * 64), lambda b, h, i: (b, i, h)),
        out_shape=jax.ShapeDtypeStruct((B, S, H * 64), BF16),
        scratch_shapes=_flash_scratch(hps, t, 2 * t),
        compiler_params=_params(("parallel", "parallel", "parallel")), name="moba_attention",
    )(bq, bkv, bkvt, kmean)


def _lane_group_norm(y, gain, width):
    outs = []
    for j in range(y.shape[1] // width):
        yc = y[:, j * width:(j + 1) * width]
        outs.append(_rms_rows(yc, gain))
    return jnp.concatenate(outs, axis=1)


def _mem_kv_kernel(m_ref, g_ref, w_ref, gk_ref, o_ref):
    mn = _rms_rows(m_ref[...], g_ref[...]).astype(BF16)
    y = _dot(mn, w_ref[...])
    hw = MEM_HEADS * MEM_HEAD_DIM
    k = _lane_group_norm(y[:, :hw], gk_ref[...], MEM_HEAD_DIM)
    o_ref[...] = jnp.concatenate([k, y[:, hw:]], axis=1).astype(BF16)


def _mem_kv(mem, g, w, gk):
    B, M, _ = mem.shape
    n = w.shape[1]
    return pl.pallas_call(
        _mem_kv_kernel, grid=(B,),
        in_specs=[pl.BlockSpec((None, M, D_MODEL), lambda b: (b, 0, 0)),
                  pl.BlockSpec(g.shape, lambda b: (0, 0)),
                  pl.BlockSpec(w.shape, lambda b: (0, 0)),
                  pl.BlockSpec(gk.shape, lambda b: (0, 0))],
        out_specs=pl.BlockSpec((None, M, n), lambda b: (b, 0, 0)),
        out_shape=jax.ShapeDtypeStruct((B, M, n), BF16),
        compiler_params=_params(("parallel",)), name="mem_kv",
    )(mem, g, w, gk)


def _mem_attend(x, g_ref, wq_ref, gq_ref, kv_ref, wo_ref):
    xn = _rms_rows(x, g_ref[...]).astype(BF16)
    q = _lane_group_norm(_dot(xn, wq_ref[...]), gq_ref[...], MEM_HEAD_DIM).astype(BF16)
    hw = MEM_HEADS * MEM_HEAD_DIM
    scale = MEM_HEAD_DIM ** -0.5
    outs = []
    for h in range(MEM_HEADS):
        cols = slice(h * MEM_HEAD_DIM, (h + 1) * MEM_HEAD_DIM)
        k = kv_ref[:, cols]
        v = kv_ref[:, hw + h * MEM_HEAD_DIM:hw + (h + 1) * MEM_HEAD_DIM]
        s = _dot_nt(q[:, cols], k) * scale
        p = jnp.exp(s - jnp.max(s, axis=-1, keepdims=True))
        p = p / jnp.sum(p, axis=-1, keepdims=True)
        outs.append(_dot(p.astype(BF16), v))
    o = jnp.concatenate(outs, axis=1).astype(BF16)
    return x + _dot(o, wo_ref[...])


def _post_mixer_kernel(*refs, n_in):
    a_refs = refs[:n_in]
    w_refs = refs[n_in:2 * n_in]
    (x_ref, gm_ref, wq_ref, gq_ref, kv_ref, wo_ref, gf_ref, wg_ref, wu_ref, wd_ref,
     o_ref, xn_ref, acc_ref) = refs[2 * n_in:]
    j = pl.program_id(1)

    @pl.when(j == 0)
    def _():
        x = x_ref[...]
        for a_ref, w_ref in zip(a_refs, w_refs):
            x = x + _dot(a_ref[...], w_ref[...])
        x = _mem_attend(x, gm_ref, wq_ref, gq_ref, kv_ref, wo_ref)
        xn_ref[...] = _rms_rows(x, gf_ref[...]).astype(BF16)
        acc_ref[...] = x

    xn = xn_ref[...]
    gate = _dot(xn, wg_ref[...])
    up = _dot(xn, wu_ref[...])
    act = (gate * jax.nn.sigmoid(gate) * up).astype(BF16)
    acc_ref[...] += _dot(act, wd_ref[...])

    @pl.when(j == pl.num_programs(1) - 1)
    def _():
        o_ref[...] = acc_ref[...]


def _post_mixer(parts, weights, x2, g_mem, wq, gq, kv, wo, g_ffn, wg, wu, wd, S, tm=512, n_split=2):
    T = x2.shape[0]
    nt = S // tm
    tf = D_FF // n_split
    n_in = len(parts)
    M, n = kv.shape[1], kv.shape[2]

    def const(a):
        return pl.BlockSpec(a.shape, lambda i, j: (0,) * a.ndim)

    in_specs = ([pl.BlockSpec((tm, p.shape[1]), lambda i, j: (i, 0)) for p in parts]
                + [const(w) for w in weights]
                + [pl.BlockSpec((tm, D_MODEL), lambda i, j: (i, 0)),
                   const(g_mem), const(wq), const(gq),
                   pl.BlockSpec((None, M, n), lambda i, j: (i // nt, 0, 0)),
                   const(wo), const(g_ffn),
                   pl.BlockSpec((D_MODEL, tf), lambda i, j: (0, j)),
                   pl.BlockSpec((D_MODEL, tf), lambda i, j: (0, j)),
                   pl.BlockSpec((tf, D_MODEL), lambda i, j: (j, 0))])
    return pl.pallas_call(
        functools.partial(_post_mixer_kernel, n_in=n_in), grid=(T // tm, n_split), in_specs=in_specs,
        out_specs=pl.BlockSpec((tm, D_MODEL), lambda i, j: (i, 0)),
        out_shape=jax.ShapeDtypeStruct((T, D_MODEL), F32),
        scratch_shapes=[pltpu.VMEM((tm, D_MODEL), BF16), pltpu.VMEM((tm, D_MODEL), F32)],
        compiler_params=_params(("parallel", "arbitrary")), name="post_mixer",
    )(*parts, *weights, x2, g_mem, wq, gq, kv, wo, g_ffn, wg, wu, wd)


def _nsa_prep_kernel(x_ref, trig_ref, gmix_ref, w_ref, nq_ref, nk_ref,
                     gq_ref, gks_ref, gkw_ref,
                     qc_ref, qr_ref, kvs_ref, kvst_ref, kvw_ref, kvwt_ref, kc_ref, vc_ref, gtt_ref,
                     *, n_tiles):
    xn = _rms_rows(x_ref[...], gmix_ref[...]).astype(BF16)
    c64, s64 = trig_ref[:, :LANES], trig_ref[:, LANES:]
    lane = _lane_iota(c64.shape)
    lo64 = (lane % 64) < 8
    first64 = lane < 64
    c64k = jnp.where(first64, c64, 1.0)
    s64k = jnp.where(first64, s64, 0.0)
    nq = nq_ref[...]
    nk = nk_ref[...]

    col = _ColumnProjector(xn, w_ref)

    for j in range(8):
        qn = _head_norm(col(j), nq, gq_ref[...])
        qr = _rope(qn, c64, s64, lo64, 8)
        qc_ref[2 * j] = qn[:, :64].astype(BF16)
        qc_ref[2 * j + 1] = qn[:, 64:].astype(BF16)
        qr_ref[2 * j] = qr[:, :64].astype(BF16)
        qr_ref[2 * j + 1] = qr[:, 64:].astype(BF16)
    tile = pl.program_id(0) % n_tiles
    sel_blk = tile * (ATT_T // NSA_SEL_LEN) + lax.shift_right_logical(
        _row_iota(c64.shape), NSA_SEL_LEN.bit_length() - 1)
    blk_onehot = jnp.where(lane == HEAD_DIM + sel_blk, 1.0, 0.0)
    for g in range(NSA_GROUPS):
        kv = _kv_column(col(8 + g), nk, gks_ref[...], c64k, s64k, lo64, first64)
        kvs_ref[g] = jnp.where(first64, kv, blk_onehot).astype(BF16)
        kvst_ref[g] = kv.T.astype(BF16)
        kv = _kv_column(col(12 + g), nk, gkw_ref[...], c64k, s64k, lo64, first64)
        kvw_ref[g] = kv.astype(BF16)
        kvwt_ref[g] = kv.T.astype(BF16)
    kc_ref[...] = jnp.concatenate([col(16), col(17)], axis=1)
    vc_ref[...] = jnp.concatenate([col(18), col(19)], axis=1)
    gates_t = jax.nn.sigmoid(col(20)).T
    for g in range(NSA_GROUPS):
        gtt_ref[g] = gates_t[12 * g:12 * (g + 1), :]


def _nsa_prep(x2, trig, gmix, w, tabs, B, S):
    T = x2.shape[0]
    tm = ATT_T
    nt = S // tm
    nq, nk, gq, gks, gkw = tabs

    def full(a):
        return pl.BlockSpec(a.shape, lambda i: (0,) * a.ndim)

    def hm(width, heads):
        return pl.BlockSpec((None, heads, tm, width), lambda i: (i // nt, 0, i % nt, 0))

    def hmt(heads):
        return pl.BlockSpec((None, heads, None, 128, tm), lambda i: (i // nt, 0, i % nt, 0, 0))

    def tokm(width):
        return pl.BlockSpec((None, tm, width), lambda i: (i // nt, i % nt, 0))

    out_shape = (
        jax.ShapeDtypeStruct((B, 16, S, 64), BF16),
        jax.ShapeDtypeStruct((B, 16, S, 64), BF16),
        jax.ShapeDtypeStruct((B, 4, S, 128), BF16),
        jax.ShapeDtypeStruct((B, 4, nt, 128, tm), BF16),
        jax.ShapeDtypeStruct((B, 4, S, 128), BF16),
        jax.ShapeDtypeStruct((B, 4, nt, 128, tm), BF16),
        jax.ShapeDtypeStruct((B, S, 256), F32),
        jax.ShapeDtypeStruct((B, S, 256), F32),
        jax.ShapeDtypeStruct((B, 4, 12, S), F32),
    )
    out_specs = (hm(64, 16), hm(64, 16), hm(128, 4), hmt(4), hm(128, 4), hmt(4), tokm(256), tokm(256),
                 pl.BlockSpec((None, 4, 12, tm), lambda i: (i // nt, 0, 0, i % nt)))
    in_specs = [pl.BlockSpec((tm, D_MODEL), lambda i: (i, 0)),
                pl.BlockSpec((tm, 2 * LANES), lambda i: (i, 0)),
                full(gmix), full(w), full(nq), full(nk), full(gq), full(gks), full(gkw)]
    return pl.pallas_call(
        functools.partial(_nsa_prep_kernel, n_tiles=nt), grid=(T // tm,), in_specs=in_specs, out_specs=out_specs,
        out_shape=out_shape, compiler_params=_params(("parallel",)), name="nsa_prep",
    )(x2, trig, gmix, w, nq, nk, gq, gks, gkw)


def _compress_one(x16, pa, pb, w1a, w1b, w2):
    n16 = x16.shape[0]
    h_a = _dot((x16 + pa).astype(BF16), w1a)
    h_b = _dot((x16 + pb).astype(BF16), w1b)
    pre = h_a + pltpu.roll(h_b, n16 - 1, 0)
    act = pre * jax.nn.sigmoid(pre)
    return _dot(act.astype(BF16), w2)


def _compress_kernel(xk_ref, xv_ref, pk_ref, pv_ref, w1k_ref, w1v_ref, w2k_ref, w2v_ref, gk_ref,
                     o_ref, ot_ref):
    half = w1k_ref.shape[0] // 2
    k = _compress_one(xk_ref[...], pk_ref[0:1, :], pk_ref[1:2, :],
                      w1k_ref[:half, :], w1k_ref[half:, :], w2k_ref[...])
    k = _rms_rows(k, gk_ref[...])
    v = _compress_one(xv_ref[...], pv_ref[0:1, :], pv_ref[1:2, :],
                      w1v_ref[:half, :], w1v_ref[half:, :], w2v_ref[...])
    kv = jnp.concatenate([k, v], axis=1)
    o_ref[...] = kv.astype(BF16)
    ot_ref[...] = kv.T.astype(BF16)


def _compress(xk16, xv16, pk, pv, w1k, w1v, w2k, w2v, gk):
    B, G, n16, width = xk16.shape

    def full(a):
        return pl.BlockSpec(a.shape, lambda b, g: (0,) * a.ndim)

    xspec = pl.BlockSpec((None, None, n16, width), lambda b, g: (b, g, 0, 0))
    return pl.pallas_call(
        _compress_kernel, grid=(B, G),
        in_specs=[xspec, xspec, full(pk), full(pv), full(w1k), full(w1v), full(w2k), full(w2v), full(gk)],
        out_specs=(pl.BlockSpec((None, None, n16, 128), lambda b, g: (b, g, 0, 0)),
                   pl.BlockSpec((None, None, 128, n16), lambda b, g: (b, g, 0, 0))),
        out_shape=(jax.ShapeDtypeStruct((B, G, n16, 128), BF16),
                   jax.ShapeDtypeStruct((B, G, 128, n16), BF16)),
        compiler_params=_params(("parallel", "parallel")), name="nsa_compress",
    )(xk16, xv16, pk, pv, w1k, w1v, w2k, w2v, gk)


NSA_GPS = 2


def _nsa_front(qc_ref, qr_ref, kvc_ref, kvct_ref, kvw_ref, kvwt_ref, gtt_ref, sel_ref, part_ref,
               i, n_cmp, n_top):
    t = ATT_T
    HG = NSA_HEADS // NSA_GROUPS
    n_sel = sel_ref.shape[0]
    n16 = kvc_ref.shape[0]
    t0 = i * t

    kvc = kvc_ref[...]
    kvct = kvct_ref[...]
    n_id = _row_iota((n16, t))
    q_id = t0 + _lane_iota((n16, t))
    visible = (n_id < n_cmp) & (n_id * NSA_CMP_STRIDE + (NSA_CMP_LEN - 1) <= q_id)
    bias_c = jnp.where(visible, 0.0, MASK_BIAS)
    p_sum = jnp.zeros((n16, t), F32)
    o_c = []
    probs, inv_ls, outs = _softmax_direct([_pad_q(qc_ref[j]) for j in range(HG)], kvc, [kvct], bias_c)
    for j in range(HG):
        p_sum = p_sum + probs[j] * inv_ls[j]
        o_c.append(outs[j] * inv_ls[j])

    b_id = _row_iota((n_sel, n16)) * NSA_SEL_LEN
    r_id = _lane_iota((n_sel, n16)) * NSA_CMP_STRIDE
    cover_t = ((r_id < b_id + NSA_SEL_LEN) & (r_id + NSA_CMP_LEN > b_id)
               & (_lane_iota((n_sel, n16)) < n_cmp))
    cover_t = jnp.where(cover_t, 1.0, 0.0).astype(BF16)
    p_hi, p_lo = _split_bf16(p_sum)
    imp = _dot(cover_t, p_hi) + _dot(cover_t, p_lo)

    qs = [_pad_q(qr_ref[j]) for j in range(HG)]
    n_wc = NSA_WINDOW // t + 1
    cw = jnp.maximum(i - (n_wc - 1), 0)
    kw0 = pl.multiple_of(cw * t, t)
    dist = (i - cw) * t + _lane_iota((n_wc * t, t)) - _row_iota((n_wc * t, t))
    bias_w = jnp.where((dist >= 0) & (dist < NSA_WINDOW), 0.0, MASK_BIAS)
    _, inv_lw, out_w = _softmax_direct(qs, kvw_ref[pl.ds(kw0, n_wc * t), :],
                                       [kvwt_ref[cw + u] for u in range(n_wc)], bias_w)
    for j in range(HG):
        part_ref[:, j * t:(j + 1) * t] = (gtt_ref[3 * j:3 * j + 1, :] * o_c[j]
                                          + gtt_ref[3 * j + 2:3 * j + 3, :] * (out_w[j] * inv_lw[j]))

    blk = _row_iota((n_sel, t))
    cur = lax.shift_right_logical(t0 + _lane_iota((n_sel, t)), NSA_SEL_LEN.bit_length() - 1)
    forced = (blk == 0) | (blk == cur) | (blk == cur - 1)
    imp = jnp.where(forced, NSA_FORCE, imp)
    visible_blk = blk <= cur
    imp = jnp.where(visible_blk, imp, NEG_INF)
    n_larger = jnp.zeros((n_sel, t), F32)
    for m in range(n_sel):
        n_larger = n_larger + jnp.where(imp[m:m + 1, :] > imp, 1.0, 0.0)
    sel_fast = n_larger < n_top
    n_picked = jnp.sum(jnp.where(sel_fast & visible_blk, 1.0, 0.0), axis=0, keepdims=True)
    n_wanted = jnp.minimum(cur[0:1, :] + 1, n_top).astype(F32)
    sel_ref[...] = jnp.where(sel_fast, 0.0, MASK_BIAS)

    @pl.when(jnp.max(jnp.abs(n_picked - n_wanted)) > 0.0)
    def _():
        sel_ref[...] = (_rank_select_t(imp, None, n_top) - 1.0) * (-MASK_BIAS)

    return qs


def _nsa_kernel(qc_ref, qr_ref, kvc_ref, kvct_ref, kvs_ref, kvst_ref, kvw_ref, kvwt_ref, gtt_ref,
                o_ref, sel_ref, part_ref, *flash_refs, n_cmp, n_top):
    t = ATT_T
    HG = NSA_HEADS // NSA_GROUPS
    n_slots = NSA_GPS * HG
    i = pl.program_id(2)

    qs = []
    for g in range(NSA_GPS):
        heads_g = pl.ds(g * HG, HG)
        qs += _nsa_front(qc_ref.at[heads_g], qr_ref.at[heads_g], kvc_ref.at[g], kvct_ref.at[g],
                         kvw_ref.at[g], kvwt_ref.at[g], gtt_ref.at[g], sel_ref.at[g],
                         part_ref.at[:, pl.ds(g * HG * t, HG * t)], i, n_cmp, n_top)

    qs_sel = [_bias_lanes(qs[s], sel_ref[s // HG]) for s in range(n_slots)]
    flash = _Flash(*flash_refs, t)
    flash.reset()

    def sel_operands(cc):
        c0 = 2 * cc
        k0 = pl.multiple_of(c0 * t, 2 * t)
        kvs = [kvs_ref[s // HG, pl.ds(k0, 2 * t), :] for s in range(n_slots)]
        kvts = [[kvst_ref[s // HG, c0], kvst_ref[s // HG, c0 + 1]] for s in range(n_slots)]
        return kvs, kvts

    def past_pair(cc):
        kvs, kvts = sel_operands(cc)
        return kvs, kvts, [None] * n_slots

    flash.run(qs_sel, i // 2, past_pair)
    causal = _causal_bias(t)

    @pl.when(i % 2 == 0)
    def _():
        k0 = pl.multiple_of(i * t, t)
        flash.update(qs_sel, [kvs_ref[s // HG, pl.ds(k0, t), :] for s in range(n_slots)],
                     [[kvst_ref[s // HG, i]] for s in range(n_slots)], [causal] * n_slots)

    @pl.when(i % 2 == 1)
    def _():
        kvs, kvts = sel_operands(i // 2)
        bias = jnp.concatenate([jnp.zeros((t, t), F32), causal], axis=0)
        flash.update(qs_sel, kvs, kvts, [bias] * n_slots)

    heads = []
    for s in range(n_slots):
        g, j = divmod(s, HG)
        heads.append(part_ref[:, s * t:(s + 1) * t] + gtt_ref[g, 3 * j + 1:3 * j + 2, :] * flash.result(s))
    _store_heads(o_ref, heads)


def _nsa_attention(qc, qr, kvc, kvct, kvs, kvst, kvw, kvwt, gates_t):
    B, H, S, _ = qc.shape
    G = NSA_GROUPS
    HG = H // G
    t = ATT_T
    nt = S // t
    n16 = kvc.shape[2]
    n_cmp = (S - NSA_CMP_LEN) // NSA_CMP_STRIDE + 1
    n_sel = S // NSA_SEL_LEN
    n_top = min(NSA_SEL_TOPK, n_sel)
    gps = NSA_GPS
    assert G % gps == 0 and nt % 2 == 0 and n_sel <= HEAD_DIM and S >= (NSA_WINDOW // t + 1) * t
    qspec = pl.BlockSpec((None, gps * HG, t, 64), lambda b, g, i: (b, g, i, 0))
    kvspec = pl.BlockSpec((None, gps, S, 128), lambda b, g, i: (b, g, 0, 0))
    kvtspec = pl.BlockSpec((None, gps, nt, 128, t), lambda b, g, i: (b, g, 0, 0, 0))
    in_specs = [qspec, qspec,
                pl.BlockSpec((None, gps, n16, 128), lambda b, g, i: (b, g, 0, 0)),
                pl.BlockSpec((None, gps, 128, n16), lambda b, g, i: (b, g, 0, 0)),
                kvspec, kvtspec, kvspec, kvtspec,
                pl.BlockSpec((None, gps, 12, t), lambda b, g, i: (b, g, 0, i))]
    return pl.pallas_call(
        functools.partial(_nsa_kernel, n_cmp=n_cmp, n_top=n_top), grid=(B, G // gps, nt), in_specs=in_specs,
        out_specs=pl.BlockSpec((None, t, gps * HG * 64), lambda b, g, i: (b, i, g)),
        out_shape=jax.ShapeDtypeStruct((B, S, H * 64), BF16),
        scratch_shapes=[pltpu.VMEM((gps, n_sel, t), F32), pltpu.VMEM((LANES, gps * HG * t), F32)]
                       + _flash_scratch(gps * HG, t, 2 * t),
        compiler_params=_params(("parallel", "parallel", "parallel")), name="nsa_attention",
    )(qc, qr, kvc, kvct, kvs, kvst, kvw, kvwt, gates_t)


def _rope_freq_row(period, rot):
    half = rot // 2
    inv_freq = ROPE_THETA ** (-(jnp.arange(half, dtype=F32) * 2.0 / rot))
    lane = jnp.arange(LANES) % period
    f = jnp.where(lane < rot, inv_freq[lane % half], 0.0)
    return f.reshape(1, LANES).astype(F32)


def _norm_matrices():
    r = jnp.arange(LANES)
    same = (r[:, None] // 64) == (r[None, :] // 64)
    nq = jnp.where(same, 1.0 / 64, 0.0).astype(BF16)
    nk = jnp.where(same & (r[:, None] < 64), 1.0 / 64, 0.0).astype(BF16)
    return nq, nk


def _q_gain(g):
    return (jnp.tile(g.astype(F32), 2) * Q_SCALE).reshape(1, LANES)


def _k_gain(g):
    return jnp.concatenate([g.astype(F32), jnp.ones((64,), F32)]).reshape(1, LANES)


def _interleave_kv(wk, wv, n_heads):
    d = wk.shape[0]
    wk = wk.reshape(d, n_heads, 64)
    wv = wv.reshape(d, n_heads, 64)
    return jnp.concatenate([wk, wv], axis=2).reshape(d, n_heads * 128)


def _split_cols(w, sizes):
    out, start = [], 0
    for n in sizes:
        out.append(w[:, start:start + n])
        start += n
    return out


def _mixer_layer0(x2, trig, B, S, gmix, w_in, w_out, a_q_norm, a_k_norm, b_q_norm, b_k_norm):
    sizes = (512, 64, 64, 256, 32, 8, 512, 512, 512)
    waq, wak, wav, wiq, wik, wiw, wbq, wbk, wbv = _split_cols(w_in, sizes)
    pad = jnp.zeros((D_MODEL, LANES - 40), w_in.dtype)
    w = jnp.concatenate([waq, wbq, _interleave_kv(wbk, wbv, 8), wak, wav, wiq, wik, wiw, pad],
                        axis=1).astype(BF16)
    nq, nk = _norm_matrices()
    tabs = (nq, nk, _q_gain(a_q_norm), _q_gain(b_q_norm), _k_gain(a_k_norm), _k_gain(b_k_norm))
    aq, bq, bkv, bkvt, akv, akvt, iq, ik, iwt, km = _ab_prep(x2, trig, gmix, w, tabs, B, S)
    n_blk = S // MOBA_BLOCK
    kmean = km.reshape(B, n_blk, 8, 128).transpose(0, 2, 1, 3)
    o_a = _dsa_attention(iq, iwt, ik, aq, akv, akvt).reshape(B * S, 512)
    o_b = _moba_attention(bq, bkv, bkvt, kmean).reshape(B * S, 512)
    w_out = w_out.astype(BF16)
    return [o_a, o_b], [w_out[:512], w_out[512:]]


def _mixer_layer1(x2, trig, B, S, gmix, w_in, w_out, q_norm, kcmp_norm, ksel_norm, kwin_norm,
                  pos_k, pos_v, w1_k, w2_k, w1_v, w2_v):
    G = NSA_GROUPS
    sizes = (1024,) + (256,) * 6 + (48,)
    wq, wkc, wvc, wks, wvs, wkw, wvw, wgt = _split_cols(w_in, sizes)
    pad = jnp.zeros((D_MODEL, LANES - 48), w_in.dtype)
    w = jnp.concatenate([wq, _interleave_kv(wks, wvs, G), _interleave_kv(wkw, wvw, G),
                         wkc, wvc, wgt, pad], axis=1).astype(BF16)
    nq, nk = _norm_matrices()
    tabs = (nq, nk, _q_gain(q_norm), _k_gain(ksel_norm), _k_gain(kwin_norm))
    qc, qr, kvs, kvst, kvw, kvwt, kc_raw, vc_raw, gates_t = _nsa_prep(x2, trig, gmix, w, tabs, B, S)

    n16 = S // NSA_CMP_STRIDE

    def blocks16(t):
        return (t.reshape(B, n16, NSA_CMP_STRIDE, G, HEAD_DIM).transpose(0, 3, 1, 2, 4)
                .reshape(B, G, n16, NSA_CMP_STRIDE * HEAD_DIM))

    def pos_rows(p):
        return p.astype(F32).reshape(2, NSA_CMP_STRIDE * HEAD_DIM)

    kvc, kvct = _compress(blocks16(kc_raw), blocks16(vc_raw), pos_rows(pos_k), pos_rows(pos_v),
                          w1_k.astype(BF16), w1_v.astype(BF16), w2_k.astype(BF16), w2_v.astype(BF16),
                          kcmp_norm.astype(F32).reshape(1, HEAD_DIM))
    o = _nsa_attention(qc, qr, kvc, kvct, kvs, kvst, kvw, kvwt, gates_t)
    return [o.reshape(B * S, NSA_HEADS * HEAD_DIM)], [w_out.astype(BF16)]


def _finish_layer(parts, weights, x2, mem, S, g_mem, g_src, w_q, w_kv, w_o, q_norm, k_norm,
                  g_ffn, ffn_w_in, ffn_w_out):
    row = lambda v: v.astype(F32).reshape(1, -1)
    kv = _mem_kv(mem, row(g_src), w_kv.astype(BF16), row(k_norm))
    wg = ffn_w_in[:, :D_FF].astype(BF16)
    wu = ffn_w_in[:, D_FF:].astype(BF16)
    return _post_mixer(parts, weights, x2, row(g_mem), w_q.astype(BF16), row(q_norm), kv, w_o.astype(BF16),
                       row(g_ffn), wg, wu, ffn_w_out.astype(BF16), S)


def kernel(x, mem, positions, norm_mix, norm_mem, norm_mem_src, norm_ffn, ab_w_in, ab_w_out, dsa_q_norm, dsa_k_norm, moba_q_norm, moba_k_norm, nsa_w_in, nsa_w_out, nsa_q_norm, nsa_kcmp_norm, nsa_ksel_norm, nsa_kwin_norm, nsa_cmp_pos_k, nsa_cmp_pos_v, nsa_cmp_w1_k, nsa_cmp_w2_k, nsa_cmp_w1_v, nsa_cmp_w2_v, mem_w_q, mem_w_kv, mem_w_o, mem_q_norm, mem_k_norm, ffn_w_in, ffn_w_out):
    B, S, D = x.shape
    depth = norm_mix.shape[0]
    x2 = x.reshape(B * S, D)
    trig = _rope_trig(positions.astype(F32).reshape(B * S, 1), _rope_freq_row(64, 16), _rope_freq_row(32, 8))
    row = lambda v: v.astype(F32).reshape(1, -1)
    for i in range(depth):
        j = i // 2
        if i % 2 == 0:
            parts, weights = _mixer_layer0(x2, trig, B, S, row(norm_mix[i]), ab_w_in[j], ab_w_out[j],
                               dsa_q_norm[j], dsa_k_norm[j], moba_q_norm[j], moba_k_norm[j])
        else:
            parts, weights = _mixer_layer1(x2, trig, B, S, row(norm_mix[i]), nsa_w_in[j], nsa_w_out[j],
                               nsa_q_norm[j], nsa_kcmp_norm[j], nsa_ksel_norm[j], nsa_kwin_norm[j],
                               nsa_cmp_pos_k[j], nsa_cmp_pos_v[j], nsa_cmp_w1_k[j], nsa_cmp_w2_k[j],
                               nsa_cmp_w1_v[j], nsa_cmp_w2_v[j])
        x2 = _finish_layer(parts, weights, x2, mem, S, norm_mem[i], norm_mem_src[i], mem_w_q[i], mem_w_kv[i],
                           mem_w_o[i], mem_q_norm[i], mem_k_norm[i], norm_ffn[i], ffn_w_in[i], ffn_w_out[i])
    return x2.reshape(B, S, D)
```

```python
import functools
import math

import jax
import jax.numpy as jnp
from jax import lax
from jax.experimental import pallas as pl
from jax.experimental.pallas import tpu as pltpu

F32 = jnp.float32
BF16 = jnp.bfloat16
I32 = jnp.int32
I16 = jnp.int16

D_MODEL = 1024
N_MEM = 256
HEAD_DIM = 64
ROPE_THETA = 500000.0
RMS_EPS = 1e-6
NEG_INF = -1e30
TINY = 1e-20

DSA_HEADS = 8
DSA_IDX_HEADS = 8
DSA_IDX_DIM = 32
DSA_TOPK = 256
MOBA_HEADS = 8
MOBA_BLOCK = 256
MOBA_TOPK = 3
NSA_HEADS = 16
NSA_GROUPS = 4
NSA_CMP_LEN = 32
NSA_CMP_STRIDE = 16
NSA_SEL_LEN = 64
NSA_SEL_TOPK = 16
NSA_WINDOW = 512
NSA_FORCE = 1e4
MEM_HEADS = 4
MEM_HEAD_DIM = 128
D_FF = ((8 * D_MODEL + 3 * 256 - 1) // (3 * 256)) * 256

LANES = 128
SUBLANES = 8
INT_MIN = -(2 ** 31)
VMEM_LIMIT = 56 * 1024 * 1024

PROJ_GROUP = 4
ATT_T = 256
MASK_BIAS = -1e30
M_FLOOR = -1e29
LOG2E = math.log2(math.e)
Q_SCALE = HEAD_DIM ** -0.5 * LOG2E

NT_DIMS = (((1,), (1,)), ((), ()))


def _dot(a, b):
    return jnp.dot(a, b, preferred_element_type=F32)


def _dot_nt(a, b):
    return lax.dot_general(a, b, NT_DIMS, preferred_element_type=F32)


def _split_bf16(a):
    hi = a.astype(BF16)
    return hi, (a - hi.astype(F32)).astype(BF16)


def _split_dot(a, b):
    hi, lo = _split_bf16(a)
    return _dot(hi, b) + _dot(lo, b)


def _rms_rows(x, gain):
    ms = jnp.mean(x * x, axis=-1, keepdims=True)
    return x * lax.rsqrt(ms + RMS_EPS) * gain


def _params(sem):
    return pltpu.CompilerParams(dimension_semantics=sem, vmem_limit_bytes=VMEM_LIMIT)


def _head_norm(y, norm_m, gain):
    ms = _split_dot(y * y, norm_m)
    return y * lax.rsqrt(ms + RMS_EPS) * gain


def _rope(y, c, s, lo_mask, half):
    sw = jnp.where(lo_mask, pltpu.roll(y, LANES - half, 1), pltpu.roll(y, half, 1))
    return y * c + sw * s


def _lane_iota(shape):
    return lax.broadcasted_iota(I32, shape, 1)


def _row_iota(shape):
    return lax.broadcasted_iota(I32, shape, 0)


def _rope_tables(pos, ftab, period, half):
    ang = pos * ftab
    lane = _lane_iota(ang.shape) % period
    c = jnp.cos(ang)
    s = jnp.sin(ang) * jnp.where(lane < half, -1.0, 1.0)
    return c, s


class _ColumnProjector:
    def __init__(self, xn, w_ref):
        self.xn, self.w_ref, self.groups = xn, w_ref, {}

    def __call__(self, j):
        g, u = divmod(j, PROJ_GROUP)
        if g not in self.groups:
            width = PROJ_GROUP * LANES
            lo = g * width
            hi = min(lo + width, self.w_ref.shape[1])
            self.groups[g] = _dot(self.xn, self.w_ref[:, lo:hi])
        return self.groups[g][:, u * LANES:(u + 1) * LANES]


def _kv_column(yc, nk, gain, c64k, s64k, lo64, first64):
    kn = jnp.where(first64, _head_norm(yc, nk, gain), yc)
    return _rope(kn, c64k, s64k, lo64, 8)


def _rope_trig_kernel(pos_ref, f64_ref, f32_ref, o_ref):
    pos = pos_ref[...]
    c64, s64 = _rope_tables(pos, f64_ref[...], 64, 8)
    c32, s32 = _rope_tables(pos, f32_ref[...], 32, 4)
    o_ref[...] = jnp.concatenate([c64, s64, c32, s32], axis=1)


def _rope_trig(pos2, f64, f32t, tm=1024):
    T = pos2.shape[0]
    return pl.pallas_call(
        _rope_trig_kernel, grid=(T // tm,),
        in_specs=[pl.BlockSpec((tm, 1), lambda i: (i, 0)),
                  pl.BlockSpec(f64.shape, lambda i: (0, 0)), pl.BlockSpec(f32t.shape, lambda i: (0, 0))],
        out_specs=pl.BlockSpec((tm, 4 * LANES), lambda i: (i, 0)),
        out_shape=jax.ShapeDtypeStruct((T, 4 * LANES), F32),
        compiler_params=_params(("parallel",)), name="rope_trig",
    )(pos2, f64, f32t)


def _ab_prep_kernel(x_ref, trig_ref, gmix_ref, w_ref, nq_ref, nk_ref,
                    gaq_ref, gbq_ref, gak_ref, gbk_ref,
                    aq_ref, bq_ref, bkv_ref, bkvt_ref, akv_ref, akvt_ref, iq_ref, ik_ref, iwt_ref, km_ref,
                    *, n_tiles):
    xn = _rms_rows(x_ref[...], gmix_ref[...]).astype(BF16)
    c64, s64, c32, s32 = [trig_ref[:, j * LANES:(j + 1) * LANES] for j in range(4)]
    lane = _lane_iota(c64.shape)
    lo64 = (lane % 64) < 8
    lo32 = (lane % 32) < 4
    first64 = lane < 64
    c64k = jnp.where(first64, c64, 1.0)
    s64k = jnp.where(first64, s64, 0.0)
    first32 = lane < 32
    c32k = jnp.where(first32, c32, 1.0)
    s32k = jnp.where(first32, s32, 0.0)
    nq = nq_ref[...]
    nk = nk_ref[...]

    col = _ColumnProjector(xn, w_ref)

    for j in range(4):
        q = _rope(_head_norm(col(j), nq, gaq_ref[...]), c64, s64, lo64, 8)
        aq_ref[2 * j] = q[:, :64].astype(BF16)
        aq_ref[2 * j + 1] = q[:, 64:].astype(BF16)
    for j in range(4):
        q = _rope(_head_norm(col(4 + j), nq, gbq_ref[...]), c64, s64, lo64, 8)
        bq_ref[2 * j] = q[:, :64].astype(BF16)
        bq_ref[2 * j + 1] = q[:, 64:].astype(BF16)
    blk_onehot = jnp.where(lane == HEAD_DIM + pl.program_id(0) % n_tiles, 1.0, 0.0)
    for h in range(8):
        kv = _kv_column(col(8 + h), nk, gbk_ref[...], c64k, s64k, lo64, first64)
        bkv_ref[h] = jnp.where(first64, kv, blk_onehot).astype(BF16)
        bkvt_ref[h] = kv.T.astype(BF16)
        km_ref[h:h + 1, :] = jnp.mean(kv, axis=0, keepdims=True)
    kv = _kv_column(col(16), nk, gak_ref[...], c64k, s64k, lo64, first64)
    akv_ref[...] = kv.astype(BF16)
    akvt_ref[...] = kv.T.astype(BF16)
    for j in range(2):
        q = _rope(col(17 + j), c32, s32, lo32, 4)
        for u in range(4):
            iq_ref[4 * j + u] = q[:, 32 * u:32 * (u + 1)].astype(BF16)
    yc = col(19)
    ik_ref[...] = _rope(yc, c32k, s32k, lo32, 4)[:, :32].astype(BF16)
    iwt_ref[...] = yc.T[32:40, :]


def _ab_prep(x2, trig, gmix, w, tabs, B, S):
    T = x2.shape[0]
    tm = ATT_T
    nt = S // tm
    n_cols = w.shape[1]
    nq, nk, gaq, gbq, gak, gbk = tabs

    def full(a):
        return pl.BlockSpec(a.shape, lambda i: (0,) * a.ndim)

    def hm(width, heads=8):
        return pl.BlockSpec((None, heads, tm, width), lambda i: (i // nt, 0, i % nt, 0))

    def tokm(width):
        return pl.BlockSpec((None, tm, width), lambda i: (i // nt, i % nt, 0))

    out_shape = (
        jax.ShapeDtypeStruct((B, 8, S, 64), BF16),
        jax.ShapeDtypeStruct((B, 8, S, 64), BF16),
        jax.ShapeDtypeStruct((B, 8, S, 128), BF16),
        jax.ShapeDtypeStruct((B, 8, nt, 128, tm), BF16),
        jax.ShapeDtypeStruct((B, S, 128), BF16),
        jax.ShapeDtypeStruct((B, nt, 128, tm), BF16),
        jax.ShapeDtypeStruct((B, 8, S, 32), BF16),
        jax.ShapeDtypeStruct((B, S, 32), BF16),
        jax.ShapeDtypeStruct((B, 8, S), F32),
        jax.ShapeDtypeStruct((T // tm, 8, 128), F32),
    )
    out_specs = (hm(64), hm(64), hm(128),
                 pl.BlockSpec((None, 8, None, 128, tm), lambda i: (i // nt, 0, i % nt, 0, 0)),
                 tokm(128),
                 pl.BlockSpec((None, None, 128, tm), lambda i: (i // nt, i % nt, 0, 0)),
                 hm(32), tokm(32),
                 pl.BlockSpec((None, 8, tm), lambda i: (i // nt, 0, i % nt)),
                 pl.BlockSpec((None, 8, 128), lambda i: (i, 0, 0)))
    in_specs = [pl.BlockSpec((tm, D_MODEL), lambda i: (i, 0)),
                pl.BlockSpec((tm, 4 * LANES), lambda i: (i, 0)),
                full(gmix), pl.BlockSpec((D_MODEL, n_cols), lambda i: (0, 0)),
                full(nq), full(nk), full(gaq), full(gbq), full(gak), full(gbk)]
    return pl.pallas_call(
        functools.partial(_ab_prep_kernel, n_tiles=nt), grid=(T // tm,), in_specs=in_specs, out_specs=out_specs,
        out_shape=out_shape, compiler_params=_params(("parallel",)), name="ab_prep",
    )(x2, trig, gmix, w, nq, nk, gaq, gbq, gak, gbk)


def _pad_q(q):
    return jnp.concatenate([q, jnp.zeros_like(q)], axis=1)


def _bias_lanes(q, rows):
    n, tq = rows.shape
    parts = [jnp.zeros((HEAD_DIM, tq), F32), rows]
    if n < HEAD_DIM:
        parts.append(jnp.zeros((HEAD_DIM - n, tq), F32))
    lanes = jnp.concatenate(parts, axis=0).T.astype(BF16)
    return jnp.where(_lane_iota(q.shape) < HEAD_DIM, q, lanes)


class _Flash:
    def __init__(self, m_ref, l_ref, acc_ref, s_ref, cmax_ref, p_ref, tq):
        self.m_ref, self.l_ref, self.acc_ref, self.tq = m_ref, l_ref, acc_ref, tq
        self.s_ref, self.cmax_ref, self.p_ref = s_ref, cmax_ref, p_ref

    def reset(self):
        self.m_ref[...] = jnp.full(self.m_ref.shape, M_FLOOR, F32)
        self.l_ref[...] = jnp.zeros(self.l_ref.shape, F32)
        self.acc_ref[...] = jnp.zeros(self.acc_ref.shape, F32)

    def _scores(self, buf, qs, kvs, biases):
        tq = self.tq
        for i in range(len(qs)):
            s = _dot_nt(kvs[i], qs[i])
            if biases[i] is not None:
                s = s + biases[i]
            self.s_ref[buf, i, :s.shape[0], :] = s
            self.cmax_ref[buf, :, i * tq:(i + 1) * tq] = jnp.max(s, axis=0, keepdims=True)

    def update(self, qs, kvs, kvts, biases):
        self._scores(0, qs, kvs, biases)
        self._finish(0, kvts)

    def run(self, qs, count, operands):
        def scores(c, buf):
            kvs, _, biases = operands(c)
            self._scores(buf, qs, kvs, biases)

        def finish(c, buf):
            self._finish(buf, operands(c)[1])

        last = jnp.maximum(count - 1, 0)
        scores(0, 0)

        def two_chunks(pp, carry):
            c = 2 * pp
            scores(c + 1, 1)
            finish(c, 0)
            scores(jnp.minimum(c + 2, last), 0)
            finish(c + 1, 1)
            return carry

        lax.fori_loop(0, count // 2, two_chunks, 0)

        @pl.when(count % 2 == 1)
        def _():
            finish(count - 1, 0)

    def _finish(self, buf, kvts):
        n = len(kvts)
        tq = self.tq
        kc = sum(kvt.shape[1] for kvt in kvts[0])
        alphas = []
        for i in range(n):
            cols = slice(i * tq, (i + 1) * tq)
            m = self.m_ref[:, cols]
            m_new = jnp.maximum(m, self.cmax_ref[buf, :, cols])
            p = jnp.exp2(self.s_ref[buf, i, :kc, :] - m_new)
            alpha = jnp.exp2(m - m_new)
            self.m_ref[:, cols] = m_new
            self.l_ref[:, cols] = alpha * self.l_ref[:, cols] + p.reshape(-1, SUBLANES, tq).sum(axis=0)
            self.p_ref[i, :kc, :] = p.astype(BF16)
            alphas.append(alpha)
        for i in range(n):
            cols = slice(i * tq, (i + 1) * tq)
            pv, r0 = None, 0
            for kvt in kvts[i]:
                part = _dot(kvt, self.p_ref[i, r0:r0 + kvt.shape[1], :])
                pv = part if pv is None else pv + part
                r0 += kvt.shape[1]
            self.acc_ref[:, cols] = alphas[i] * self.acc_ref[:, cols] + pv

    def result(self, slot):
        cols = slice(slot * self.tq, (slot + 1) * self.tq)
        l = jnp.sum(self.l_ref[:, cols], axis=0, keepdims=True)
        return self.acc_ref[:, cols] / jnp.maximum(l, TINY)


def _flash_scratch(n_slots, tq, kc):
    return [pltpu.VMEM((1, n_slots * tq), F32), pltpu.VMEM((SUBLANES, n_slots * tq), F32),
            pltpu.VMEM((LANES, n_slots * tq), F32),
            pltpu.VMEM((2, n_slots, kc, tq), F32), pltpu.VMEM((2, 1, n_slots * tq), F32),
            pltpu.VMEM((n_slots, kc, tq), BF16)]


def _softmax_direct(qs, kv, kvts, bias):
    scores = [_dot_nt(kv, q) for q in qs]
    probs, inv_ls = [], []
    for s in scores:
        s = s + bias
        m = jnp.maximum(jnp.max(s, axis=0, keepdims=True), M_FLOOR)
        p = jnp.exp2(s - m)
        inv_ls.append(1.0 / jnp.maximum(jnp.sum(p, axis=0, keepdims=True), TINY))
        probs.append(p)
    outs = []
    for p in probs:
        pb = p.astype(BF16)
        o, r0 = None, 0
        for kvt in kvts:
            part = _dot(kvt, pb[r0:r0 + kvt.shape[1]])
            o = part if o is None else o + part
            r0 += kvt.shape[1]
        outs.append(o)
    return probs, inv_ls, outs


def _causal_bias(t):
    return jnp.where(_row_iota((t, t)) <= _lane_iota((t, t)), 0.0, MASK_BIAS)


def _store_heads(o_ref, heads_t):
    tq = heads_t[0].shape[1]
    lane = _lane_iota((tq, LANES))
    for u in range(len(heads_t) // 2):
        even = pltpu.roll(heads_t[2 * u].T, 64, 1)
        odd = heads_t[2 * u + 1].T
        o_ref[:, u * LANES:(u + 1) * LANES] = jnp.where(lane < 64, even, odd).astype(o_ref.dtype)


def _rank_select_t(v, n_valid, n_top):
    n = v.shape[0]
    row = _row_iota(v.shape)
    rank = jnp.zeros(v.shape, F32)
    for m in range(n):
        vm = v[m:m + 1, :]
        ahead = (vm > v) | ((vm == v) & (m < row))
        if n_valid is not None:
            ahead = ahead & (m < n_valid)
        rank = rank + jnp.where(ahead, 1.0, 0.0)
    sel = rank < n_top
    if n_valid is not None:
        sel = sel & (row < n_valid)
    return jnp.where(sel, 1.0, 0.0)


def _dsa_kernel(iq_ref, iwt_ref, ik_ref, aq_ref, akv_ref, akvt_ref, o_ref,
                sk_ref, half_ref, bias_ref, xcut_ref, *flash_refs, k_top, index_bits):
    t = ATT_T
    i = pl.program_id(1)
    n_ch = i + 1
    kio = _row_iota((t, t))
    qio = _lane_iota((t, t))

    def causal(c):
        return (c - i) * t + kio <= qio

    def score_chunk(c):
        k0 = pl.multiple_of(c * t, t)
        ikc = ik_ref[pl.ds(k0, t), :]
        sc = jnp.zeros((t, t), F32)
        for h in range(DSA_IDX_HEADS):
            logit = _dot_nt(ikc, iq_ref[h])
            sc = sc + iwt_ref[h:h + 1, :] * jnp.maximum(logit, 0.0)
        sc = jnp.where(sc == 0.0, 0.0, sc)
        bits = pltpu.bitcast(sc, I32)
        key = bits ^ ((bits >> 31) & 0x7FFFFFFF)
        key = jnp.where(causal(c), key, INT_MIN)
        sk_ref[c] = key
        half_ref[c] = (key >> 16).astype(I16)

    def score_pair(cc, carry):
        score_chunk(2 * cc)
        score_chunk(2 * cc + 1)
        return carry

    lax.fori_loop(0, (n_ch + 1) // 2, score_pair, 0)

    def count(pred):
        def body(c, acc8):
            ind = jnp.where(pred(sk_ref[c], c), 1.0, 0.0)
            return acc8 + ind.reshape(-1, SUBLANES, t).sum(axis=0)
        acc8 = lax.fori_loop(0, n_ch, body, jnp.zeros((SUBLANES, t), F32))
        return jnp.sum(acc8, axis=0, keepdims=True)

    def count_half(cand):
        rows = 2 * SUBLANES

        def body(cc, acc):
            parts = []
            for c in (2 * cc, 2 * cc + 1):
                ind = jnp.where(half_ref[c] >= cand, jnp.bfloat16(1), jnp.bfloat16(0))
                parts += [ind[rows * j:rows * (j + 1), :] for j in range(t // rows)]
            while len(parts) > 1:
                parts = [parts[2 * j] + parts[2 * j + 1] for j in range(len(parts) // 2)]
            return acc + parts[0].astype(F32)
        acc = lax.fori_loop(0, (n_ch + 1) // 2, body, jnp.zeros((rows, t), F32))
        return jnp.sum(acc, axis=0, keepdims=True)

    def half_search():
        def bit_step(b, v):
            cand = v + lax.shift_left(jnp.int32(1), 15 - b)
            return jnp.where(count_half(cand.astype(I16)) >= k_top, cand, v)
        return lax.fori_loop(0, 16, bit_step, jnp.full((1, t), -(2 ** 15), I32))

    thr_hi = half_search()

    def low_half_chunk(c, carry):
        key = sk_ref[c]
        hi = key >> 16
        lo = (key & 0xFFFF) - 2 ** 15
        half_ref[c] = jnp.where(hi > thr_hi, 2 ** 15 - 1, jnp.where(hi < thr_hi, -(2 ** 15), lo)).astype(I16)
        return carry

    lax.fori_loop(0, n_ch, low_half_chunk, 0)
    thr = lax.shift_left(thr_hi, 16) + (half_search() + 2 ** 15)

    need = k_top - count(lambda blk, c: blk > thr)
    n_ge = count(lambda blk, c: blk >= thr)
    xcut_ref[...] = jnp.full((1, t), 2 ** 30, I32)

    @pl.when(jnp.max(n_ge) > k_top)
    def _():
        def x_step(b, x):
            cand = x + lax.shift_left(jnp.int32(1), index_bits - 1 - b)
            ties_below = count(lambda blk, c: (blk == thr) & (c * t + kio < cand))
            return jnp.where(ties_below <= need, cand, x)
        xcut_ref[...] = lax.fori_loop(0, index_bits, x_step, jnp.zeros((1, t), I32))

    xcut = xcut_ref[...]

    n_pairs = (n_ch + 1) // 2

    def bias_chunk(c, carry):
        blk = sk_ref[jnp.minimum(c, i)]
        keep = (blk > thr) | ((blk == thr) & (c * t + kio < xcut))
        bias_ref[c] = jnp.where(keep & causal(c), 0.0, MASK_BIAS)
        return carry

    lax.fori_loop(0, 2 * n_pairs, bias_chunk, 0)

    flash = _Flash(*flash_refs, t)
    flash.reset()
    qs = [_pad_q(aq_ref[h]) for h in range(DSA_HEADS)]

    n = DSA_HEADS

    def att_pair(cc):
        c0 = 2 * cc
        k0 = pl.multiple_of(c0 * t, 2 * t)
        kv = akv_ref[pl.ds(k0, 2 * t), :]
        bias = jnp.concatenate([bias_ref[c0], bias_ref[c0 + 1]], axis=0)
        return [kv] * n, [[akvt_ref[c0], akvt_ref[c0 + 1]]] * n, [bias] * n

    flash.run(qs, n_pairs, att_pair)
    _store_heads(o_ref, [flash.result(h) for h in range(DSA_HEADS)])


def _dsa_attention(iq, iwt, ik, aq, akv, akvt):
    B, _, S, _ = aq.shape
    t = ATT_T
    nt = S // t
    k_top = min(DSA_TOPK, S // 4)
    in_specs = [
        pl.BlockSpec((None, 8, t, 32), lambda b, i: (b, 0, i, 0)),
        pl.BlockSpec((None, 8, t), lambda b, i: (b, 0, i)),
        pl.BlockSpec((None, S, 32), lambda b, i: (b, 0, 0)),
        pl.BlockSpec((None, 8, t, 64), lambda b, i: (b, 0, i, 0)),
        pl.BlockSpec((None, S, 128), lambda b, i: (b, 0, 0)),
        pl.BlockSpec((None, nt, 128, t), lambda b, i: (b, 0, 0, 0)),
    ]
    return pl.pallas_call(
        functools.partial(_dsa_kernel, k_top=k_top, index_bits=S.bit_length()),
        grid=(B, nt), in_specs=in_specs,
        out_specs=pl.BlockSpec((None, t, 512), lambda b, i: (b, i, 0)),
        out_shape=jax.ShapeDtypeStruct((B, S, 512), BF16),
        scratch_shapes=[pltpu.VMEM((nt, t, t), I32), pltpu.VMEM((nt, t, t), I16), pltpu.VMEM((nt, t, t), F32),
                        pltpu.VMEM((1, t), I32)] + _flash_scratch(DSA_HEADS, t, 2 * t),
        compiler_params=_params(("parallel", "parallel")), name="dsa_attention",
    )(iq, iwt, ik, aq, akv, akvt)


MOBA_HPS = 8


def _moba_kernel(q_ref, kv_ref, kvt_ref, km_ref, o_ref, *flash_refs, n_top):
    t = ATT_T
    own = pl.program_id(2)
    causal = _causal_bias(t)
    flash = _Flash(*flash_refs, t)
    flash.reset()
    qs = []
    for hh in range(MOBA_HPS):
        q = _pad_q(q_ref[hh])
        km_hi, km_lo = _split_bf16(km_ref[hh])
        gate = _dot_nt(km_hi, q) + _dot_nt(km_lo, q)
        keep = _rank_select_t(gate, own, n_top)
        keep = jnp.where(_row_iota(keep.shape) == own, 1.0, keep)
        qs.append(_bias_lanes(q, (keep - 1.0) * (-MASK_BIAS)))

    def operands(cc):
        n0 = 2 * cc
        k0 = pl.multiple_of(n0 * t, 2 * t)
        heads = range(MOBA_HPS)
        return (n0, [kv_ref[hh, pl.ds(k0, 2 * t), :] for hh in heads],
                [[kvt_ref[hh, n0], kvt_ref[hh, n0 + 1]] for hh in heads])

    def past_pair(cc):
        _, kvs, kvts = operands(cc)
        return kvs, kvts, [None] * MOBA_HPS

    flash.run(qs, own // 2, past_pair)
    heads = range(MOBA_HPS)

    @pl.when(own % 2 == 0)
    def _():
        k0 = pl.multiple_of(own * t, t)
        flash.update(qs, [kv_ref[hh, pl.ds(k0, t), :] for hh in heads],
                     [[kvt_ref[hh, own]] for hh in heads], [causal] * MOBA_HPS)

    @pl.when(own % 2 == 1)
    def _():
        _, kvs, kvts = operands(own // 2)
        bias = jnp.concatenate([jnp.zeros((t, t), F32), causal], axis=0)
        flash.update(qs, kvs, kvts, [bias] * MOBA_HPS)

    _store_heads(o_ref, [flash.result(hh) for hh in range(MOBA_HPS)])


def _moba_attention(bq, bkv, bkvt, kmean):
    B, H, S, _ = bq.shape
    t = ATT_T
    hps = MOBA_HPS
    n_blk = S // MOBA_BLOCK
    assert t == MOBA_BLOCK and n_blk % 2 == 0 and H % hps == 0
    n_top = max(1, min(MOBA_TOPK, n_blk - 1))
    in_specs = [
        pl.BlockSpec((None, hps, t, 64), lambda b, h, i: (b, h, i, 0)),
        pl.BlockSpec((None, hps, S, 128), lambda b, h, i: (b, h, 0, 0)),
        pl.BlockSpec((None, hps, n_blk, 128, t), lambda b, h, i: (b, h, 0, 0, 0)),
        pl.BlockSpec((None, hps, n_blk, 128), lambda b, h, i: (b, h, 0, 0)),
    ]
    return pl.pallas_call(
        functools.partial(_moba_kernel, n_top=n_top), grid=(B, H // hps, S // t), in_specs=in_specs,
        out_specs=pl.BlockSpec((None, t, hps * 64), lambda b, h, i: (b, i, h)),
        out_shape=jax.ShapeDtypeStruct((B, S, H * 64), BF16),
        scratch_shapes=_flash_scratch(hps, t, 2 * t),
        compiler_params=_params(("parallel", "parallel", "parallel")), name="moba_attention",
    )(bq, bkv, bkvt, kmean)


def _lane_group_norm(y, gain, width):
    outs = []
    for j in range(y.shape[1] // width):
        yc = y[:, j * width:(j + 1) * width]
        outs.append(_rms_rows(yc, gain))
    return jnp.concatenate(outs, axis=1)


def _mem_kv_kernel(m_ref, g_ref, w_ref, gk_ref, o_ref):
    mn = _rms_rows(m_ref[...], g_ref[...]).astype(BF16)
    y = _dot(mn, w_ref[...])
    hw = MEM_HEADS * MEM_HEAD_DIM
    k = _lane_group_norm(y[:, :hw], gk_ref[...], MEM_HEAD_DIM)
    o_ref[...] = jnp.concatenate([k, y[:, hw:]], axis=1).astype(BF16)


def _mem_kv(mem, g, w, gk):
    B, M, _ = mem.shape
    n = w.shape[1]
    return pl.pallas_call(
        _mem_kv_kernel, grid=(B,),
        in_specs=[pl.BlockSpec((None, M, D_MODEL), lambda b: (b, 0, 0)),
                  pl.BlockSpec(g.shape, lambda b: (0, 0)),
                  pl.BlockSpec(w.shape, lambda b: (0, 0)),
                  pl.BlockSpec(gk.shape, lambda b: (0, 0))],
        out_specs=pl.BlockSpec((None, M, n), lambda b: (b, 0, 0)),
        out_shape=jax.ShapeDtypeStruct((B, M, n), BF16),
        compiler_params=_params(("parallel",)), name="mem_kv",
    )(mem, g, w, gk)


def _mem_attend(x, g_ref, wq_ref, gq_ref, kv_ref, wo_ref):
    xn = _rms_rows(x, g_ref[...]).astype(BF16)
    q = _lane_group_norm(_dot(xn, wq_ref[...]), gq_ref[...], MEM_HEAD_DIM).astype(BF16)
    hw = MEM_HEADS * MEM_HEAD_DIM
    scale = MEM_HEAD_DIM ** -0.5
    outs = []
    for h in range(MEM_HEADS):
        cols = slice(h * MEM_HEAD_DIM, (h + 1) * MEM_HEAD_DIM)
        k = kv_ref[:, cols]
        v = kv_ref[:, hw + h * MEM_HEAD_DIM:hw + (h + 1) * MEM_HEAD_DIM]
        s = _dot_nt(q[:, cols], k) * scale
        p = jnp.exp(s - jnp.max(s, axis=-1, keepdims=True))
        p = p / jnp.sum(p, axis=-1, keepdims=True)
        outs.append(_dot(p.astype(BF16), v))
    o = jnp.concatenate(outs, axis=1).astype(BF16)
    return x + _dot(o, wo_ref[...])


def _post_mixer_kernel(*refs, n_in):
    a_refs = refs[:n_in]
    w_refs = refs[n_in:2 * n_in]
    (x_ref, gm_ref, wq_ref, gq_ref, kv_ref, wo_ref, gf_ref, wg_ref, wu_ref, wd_ref,
     o_ref, xn_ref, acc_ref) = refs[2 * n_in:]
    j = pl.program_id(1)

    @pl.when(j == 0)
    def _():
        x = x_ref[...]
        for a_ref, w_ref in zip(a_refs, w_refs):
            x = x + _dot(a_ref[...], w_ref[...])
        x = _mem_attend(x, gm_ref, wq_ref, gq_ref, kv_ref, wo_ref)
        xn_ref[...] = _rms_rows(x, gf_ref[...]).astype(BF16)
        acc_ref[...] = x

    xn = xn_ref[...]
    gate = _dot(xn, wg_ref[...])
    up = _dot(xn, wu_ref[...])
    act = (gate * jax.nn.sigmoid(gate) * up).astype(BF16)
    acc_ref[...] += _dot(act, wd_ref[...])

    @pl.when(j == pl.num_programs(1) - 1)
    def _():
        o_ref[...] = acc_ref[...]


def _post_mixer(parts, weights, x2, g_mem, wq, gq, kv, wo, g_ffn, wg, wu, wd, S, tm=512, n_split=2):
    T = x2.shape[0]
    nt = S // tm
    tf = D_FF // n_split
    n_in = len(parts)
    M, n = kv.shape[1], kv.shape[2]

    def const(a):
        return pl.BlockSpec(a.shape, lambda i, j: (0,) * a.ndim)

    in_specs = ([pl.BlockSpec((tm, p.shape[1]), lambda i, j: (i, 0)) for p in parts]
                + [const(w) for w in weights]
                + [pl.BlockSpec((tm, D_MODEL), lambda i, j: (i, 0)),
                   const(g_mem), const(wq), const(gq),
                   pl.BlockSpec((None, M, n), lambda i, j: (i // nt, 0, 0)),
                   const(wo), const(g_ffn),
                   pl.BlockSpec((D_MODEL, tf), lambda i, j: (0, j)),
                   pl.BlockSpec((D_MODEL, tf), lambda i, j: (0, j)),
                   pl.BlockSpec((tf, D_MODEL), lambda i, j: (j, 0))])
    return pl.pallas_call(
        functools.partial(_post_mixer_kernel, n_in=n_in), grid=(T // tm, n_split), in_specs=in_specs,
        out_specs=pl.BlockSpec((tm, D_MODEL), lambda i, j: (i, 0)),
        out_shape=jax.ShapeDtypeStruct((T, D_MODEL), F32),
        scratch_shapes=[pltpu.VMEM((tm, D_MODEL), BF16), pltpu.VMEM((tm, D_MODEL), F32)],
        compiler_params=_params(("parallel", "arbitrary")), name="post_mixer",
    )(*parts, *weights, x2, g_mem, wq, gq, kv, wo, g_ffn, wg, wu, wd)


def _nsa_prep_kernel(x_ref, trig_ref, gmix_ref, w_ref, nq_ref, nk_ref,
                     gq_ref, gks_ref, gkw_ref,
                     qc_ref, qr_ref, kvs_ref, kvst_ref, kvw_ref, kvwt_ref, kc_ref, vc_ref, gtt_ref,
                     *, n_tiles):
    xn = _rms_rows(x_ref[...], gmix_ref[...]).astype(BF16)
    c64, s64 = trig_ref[:, :LANES], trig_ref[:, LANES:]
    lane = _lane_iota(c64.shape)
    lo64 = (lane % 64) < 8
    first64 = lane < 64
    c64k = jnp.where(first64, c64, 1.0)
    s64k = jnp.where(first64, s64, 0.0)
    nq = nq_ref[...]
    nk = nk_ref[...]

    col = _ColumnProjector(xn, w_ref)

    for j in range(8):
        qn = _head_norm(col(j), nq, gq_ref[...])
        qr = _rope(qn, c64, s64, lo64, 8)
        qc_ref[2 * j] = qn[:, :64].astype(BF16)
        qc_ref[2 * j + 1] = qn[:, 64:].astype(BF16)
        qr_ref[2 * j] = qr[:, :64].astype(BF16)
        qr_ref[2 * j + 1] = qr[:, 64:].astype(BF16)
    tile = pl.program_id(0) % n_tiles
    sel_blk = tile * (ATT_T // NSA_SEL_LEN) + lax.shift_right_logical(
        _row_iota(c64.shape), NSA_SEL_LEN.bit_length() - 1)
    blk_onehot = jnp.where(lane == HEAD_DIM + sel_blk, 1.0, 0.0)
    for g in range(NSA_GROUPS):
        kv = _kv_column(col(8 + g), nk, gks_ref[...], c64k, s64k, lo64, first64)
        kvs_ref[g] = jnp.where(first64, kv, blk_onehot).astype(BF16)
        kvst_ref[g] = kv.T.astype(BF16)
        kv = _kv_column(col(12 + g), nk, gkw_ref[...], c64k, s64k, lo64, first64)
        kvw_ref[g] = kv.astype(BF16)
        kvwt_ref[g] = kv.T.astype(BF16)
    kc_ref[...] = jnp.concatenate([col(16), col(17)], axis=1)
    vc_ref[...] = jnp.concatenate([col(18), col(19)], axis=1)
    gates_t = jax.nn.sigmoid(col(20)).T
    for g in range(NSA_GROUPS):
        gtt_ref[g] = gates_t[12 * g:12 * (g + 1), :]


def _nsa_prep(x2, trig, gmix, w, tabs, B, S):
    T = x2.shape[0]
    tm = ATT_T
    nt = S // tm
    nq, nk, gq, gks, gkw = tabs

    def full(a):
        return pl.BlockSpec(a.shape, lambda i: (0,) * a.ndim)

    def hm(width, heads):
        return pl.BlockSpec((None, heads, tm, width), lambda i: (i // nt, 0, i % nt, 0))

    def hmt(heads):
        return pl.BlockSpec((None, heads, None, 128, tm), lambda i: (i // nt, 0, i % nt, 0, 0))

    def tokm(width):
        return pl.BlockSpec((None, tm, width), lambda i: (i // nt, i % nt, 0))

    out_shape = (
        jax.ShapeDtypeStruct((B, 16, S, 64), BF16),
        jax.ShapeDtypeStruct((B, 16, S, 64), BF16),
        jax.ShapeDtypeStruct((B, 4, S, 128), BF16),
        jax.ShapeDtypeStruct((B, 4, nt, 128, tm), BF16),
        jax.ShapeDtypeStruct((B, 4, S, 128), BF16),
        jax.ShapeDtypeStruct((B, 4, nt, 128, tm), BF16),
        jax.ShapeDtypeStruct((B, S, 256), F32),
        jax.ShapeDtypeStruct((B, S, 256), F32),
        jax.ShapeDtypeStruct((B, 4, 12, S), F32),
    )
    out_specs = (hm(64, 16), hm(64, 16), hm(128, 4), hmt(4), hm(128, 4), hmt(4), tokm(256), tokm(256),
                 pl.BlockSpec((None, 4, 12, tm), lambda i: (i // nt, 0, 0, i % nt)))
    in_specs = [pl.BlockSpec((tm, D_MODEL), lambda i: (i, 0)),
                pl.BlockSpec((tm, 2 * LANES), lambda i: (i, 0)),
                full(gmix), full(w), full(nq), full(nk), full(gq), full(gks), full(gkw)]
    return pl.pallas_call(
        functools.partial(_nsa_prep_kernel, n_tiles=nt), grid=(T // tm,), in_specs=in_specs, out_specs=out_specs,
        out_shape=out_shape, compiler_params=_params(("parallel",)), name="nsa_prep",
    )(x2, trig, gmix, w, nq, nk, gq, gks, gkw)


def _compress_one(x16, pa, pb, w1a, w1b, w2):
    n16 = x16.shape[0]
    h_a = _dot((x16 + pa).astype(BF16), w1a)
    h_b = _dot((x16 + pb).astype(BF16), w1b)
    pre = h_a + pltpu.roll(h_b, n16 - 1, 0)
    act = pre * jax.nn.sigmoid(pre)
    return _dot(act.astype(BF16), w2)


def _compress_kernel(xk_ref, xv_ref, pk_ref, pv_ref, w1k_ref, w1v_ref, w2k_ref, w2v_ref, gk_ref,
                     o_ref, ot_ref):
    half = w1k_ref.shape[0] // 2
    k = _compress_one(xk_ref[...], pk_ref[0:1, :], pk_ref[1:2, :],
                      w1k_ref[:half, :], w1k_ref[half:, :], w2k_ref[...])
    k = _rms_rows(k, gk_ref[...])
    v = _compress_one(xv_ref[...], pv_ref[0:1, :], pv_ref[1:2, :],
                      w1v_ref[:half, :], w1v_ref[half:, :], w2v_ref[...])
    kv = jnp.concatenate([k, v], axis=1)
    o_ref[...] = kv.astype(BF16)
    ot_ref[...] = kv.T.astype(BF16)


def _compress(xk16, xv16, pk, pv, w1k, w1v, w2k, w2v, gk):
    B, G, n16, width = xk16.shape

    def full(a):
        return pl.BlockSpec(a.shape, lambda b, g: (0,) * a.ndim)

    xspec = pl.BlockSpec((None, None, n16, width), lambda b, g: (b, g, 0, 0))
    return pl.pallas_call(
        _compress_kernel, grid=(B, G),
        in_specs=[xspec, xspec, full(pk), full(pv), full(w1k), full(w1v), full(w2k), full(w2v), full(gk)],
        out_specs=(pl.BlockSpec((None, None, n16, 128), lambda b, g: (b, g, 0, 0)),
                   pl.BlockSpec((None, None, 128, n16), lambda b, g: (b, g, 0, 0))),
        out_shape=(jax.ShapeDtypeStruct((B, G, n16, 128), BF16),
                   jax.ShapeDtypeStruct((B, G, 128, n16), BF16)),
        compiler_params=_params(("parallel", "parallel")), name="nsa_compress",
    )(xk16, xv16, pk, pv, w1k, w1v, w2k, w2v, gk)


NSA_GPS = 2


def _nsa_front(qc_ref, qr_ref, kvc_ref, kvct_ref, kvw_ref, kvwt_ref, gtt_ref, sel_ref, part_ref,
               i, n_cmp, n_top):
    t = ATT_T
    HG = NSA_HEADS // NSA_GROUPS
    n_sel = sel_ref.shape[0]
    n16 = kvc_ref.shape[0]
    t0 = i * t

    kvc = kvc_ref[...]
    kvct = kvct_ref[...]
    n_id = _row_iota((n16, t))
    q_id = t0 + _lane_iota((n16, t))
    visible = (n_id < n_cmp) & (n_id * NSA_CMP_STRIDE + (NSA_CMP_LEN - 1) <= q_id)
    bias_c = jnp.where(visible, 0.0, MASK_BIAS)
    p_sum = jnp.zeros((n16, t), F32)
    o_c = []
    probs, inv_ls, outs = _softmax_direct([_pad_q(qc_ref[j]) for j in range(HG)], kvc, [kvct], bias_c)
    for j in range(HG):
        p_sum = p_sum + probs[j] * inv_ls[j]
        o_c.append(outs[j] * inv_ls[j])

    b_id = _row_iota((n_sel, n16)) * NSA_SEL_LEN
    r_id = _lane_iota((n_sel, n16)) * NSA_CMP_STRIDE
    cover_t = ((r_id < b_id + NSA_SEL_LEN) & (r_id + NSA_CMP_LEN > b_id)
               & (_lane_iota((n_sel, n16)) < n_cmp))
    cover_t = jnp.where(cover_t, 1.0, 0.0).astype(BF16)
    p_hi, p_lo = _split_bf16(p_sum)
    imp = _dot(cover_t, p_hi) + _dot(cover_t, p_lo)

    qs = [_pad_q(qr_ref[j]) for j in range(HG)]
    n_wc = NSA_WINDOW // t + 1
    cw = jnp.maximum(i - (n_wc - 1), 0)
    kw0 = pl.multiple_of(cw * t, t)
    dist = (i - cw) * t + _lane_iota((n_wc * t, t)) - _row_iota((n_wc * t, t))
    bias_w = jnp.where((dist >= 0) & (dist < NSA_WINDOW), 0.0, MASK_BIAS)
    _, inv_lw, out_w = _softmax_direct(qs, kvw_ref[pl.ds(kw0, n_wc * t), :],
                                       [kvwt_ref[cw + u] for u in range(n_wc)], bias_w)
    for j in range(HG):
        part_ref[:, j * t:(j + 1) * t] = (gtt_ref[3 * j:3 * j + 1, :] * o_c[j]
                                          + gtt_ref[3 * j + 2:3 * j + 3, :] * (out_w[j] * inv_lw[j]))

    blk = _row_iota((n_sel, t))
    cur = lax.shift_right_logical(t0 + _lane_iota((n_sel, t)), NSA_SEL_LEN.bit_length() - 1)
    forced = (blk == 0) | (blk == cur) | (blk == cur - 1)
    imp = jnp.where(forced, NSA_FORCE, imp)
    visible_blk = blk <= cur
    imp = jnp.where(visible_blk, imp, NEG_INF)
    n_larger = jnp.zeros((n_sel, t), F32)
    for m in range(n_sel):
        n_larger = n_larger + jnp.where(imp[m:m + 1, :] > imp, 1.0, 0.0)
    sel_fast = n_larger < n_top
    n_picked = jnp.sum(jnp.where(sel_fast & visible_blk, 1.0, 0.0), axis=0, keepdims=True)
    n_wanted = jnp.minimum(cur[0:1, :] + 1, n_top).astype(F32)
    sel_ref[...] = jnp.where(sel_fast, 0.0, MASK_BIAS)

    @pl.when(jnp.max(jnp.abs(n_picked - n_wanted)) > 0.0)
    def _():
        sel_ref[...] = (_rank_select_t(imp, None, n_top) - 1.0) * (-MASK_BIAS)

    return qs


def _nsa_kernel(qc_ref, qr_ref, kvc_ref, kvct_ref, kvs_ref, kvst_ref, kvw_ref, kvwt_ref, gtt_ref,
                o_ref, sel_ref, part_ref, *flash_refs, n_cmp, n_top):
    t = ATT_T
    HG = NSA_HEADS // NSA_GROUPS
    n_slots = NSA_GPS * HG
    i = pl.program_id(2)

    qs = []
    for g in range(NSA_GPS):
        heads_g = pl.ds(g * HG, HG)
        qs += _nsa_front(qc_ref.at[heads_g], qr_ref.at[heads_g], kvc_ref.at[g], kvct_ref.at[g],
                         kvw_ref.at[g], kvwt_ref.at[g], gtt_ref.at[g], sel_ref.at[g],
                         part_ref.at[:, pl.ds(g * HG * t, HG * t)], i, n_cmp, n_top)

    qs_sel = [_bias_lanes(qs[s], sel_ref[s // HG]) for s in range(n_slots)]
    flash = _Flash(*flash_refs, t)
    flash.reset()

    def sel_operands(cc):
        c0 = 2 * cc
        k0 = pl.multiple_of(c0 * t, 2 * t)
        kvs = [kvs_ref[s // HG, pl.ds(k0, 2 * t), :] for s in range(n_slots)]
        kvts = [[kvst_ref[s // HG, c0], kvst_ref[s // HG, c0 + 1]] for s in range(n_slots)]
        return kvs, kvts

    def past_pair(cc):
        kvs, kvts = sel_operands(cc)
        return kvs, kvts, [None] * n_slots

    flash.run(qs_sel, i // 2, past_pair)
    causal = _causal_bias(t)

    @pl.when(i % 2 == 0)
    def _():
        k0 = pl.multiple_of(i * t, t)
        flash.update(qs_sel, [kvs_ref[s // HG, pl.ds(k0, t), :] for s in range(n_slots)],
                     [[kvst_ref[s // HG, i]] for s in range(n_slots)], [causal] * n_slots)

    @pl.when(i % 2 == 1)
    def _():
        kvs, kvts = sel_operands(i // 2)
        bias = jnp.concatenate([jnp.zeros((t, t), F32), causal], axis=0)
        flash.update(qs_sel, kvs, kvts, [bias] * n_slots)

    heads = []
    for s in range(n_slots):
        g, j = divmod(s, HG)
        heads.append(part_ref[:, s * t:(s + 1) * t] + gtt_ref[g, 3 * j + 1:3 * j + 2, :] * flash.result(s))
    _store_heads(o_ref, heads)


def _nsa_attention(qc, qr, kvc, kvct, kvs, kvst, kvw, kvwt, gates_t):
    B, H, S, _ = qc.shape
    G = NSA_GROUPS
    HG = H // G
    t = ATT_T
    nt = S // t
    n16 = kvc.shape[2]
    n_cmp = (S - NSA_CMP_LEN) // NSA_CMP_STRIDE + 1
    n_sel = S // NSA_SEL_LEN
    n_top = min(NSA_SEL_TOPK, n_sel)
    gps = NSA_GPS
    assert G % gps == 0 and nt % 2 == 0 and n_sel <= HEAD_DIM and S >= (NSA_WINDOW // t + 1) * t
    qspec = pl.BlockSpec((None, gps * HG, t, 64), lambda b, g, i: (b, g, i, 0))
    kvspec = pl.BlockSpec((None, gps, S, 128), lambda b, g, i: (b, g, 0, 0))
    kvtspec = pl.BlockSpec((None, gps, nt, 128, t), lambda b, g, i: (b, g, 0, 0, 0))
    in_specs = [qspec, qspec,
                pl.BlockSpec((None, gps, n16, 128), lambda b, g, i: (b, g, 0, 0)),
                pl.BlockSpec((None, gps, 128, n16), lambda b, g, i: (b, g, 0, 0)),
                kvspec, kvtspec, kvspec, kvtspec,
                pl.BlockSpec((None, gps, 12, t), lambda b, g, i: (b, g, 0, i))]
    return pl.pallas_call(
        functools.partial(_nsa_kernel, n_cmp=n_cmp, n_top=n_top), grid=(B, G // gps, nt), in_specs=in_specs,
        out_specs=pl.BlockSpec((None, t, gps * HG * 64), lambda b, g, i: (b, i, g)),
        out_shape=jax.ShapeDtypeStruct((B, S, H * 64), BF16),
        scratch_shapes=[pltpu.VMEM((gps, n_sel, t), F32), pltpu.VMEM((LANES, gps * HG * t), F32)]
                       + _flash_scratch(gps * HG, t, 2 * t),
        compiler_params=_params(("parallel", "parallel", "parallel")), name="nsa_attention",
    )(qc, qr, kvc, kvct, kvs, kvst, kvw, kvwt, gates_t)


def _rope_freq_row(period, rot):
    half = rot // 2
    inv_freq = ROPE_THETA ** (-(jnp.arange(half, dtype=F32) * 2.0 / rot))
    lane = jnp.arange(LANES) % period
    f = jnp.where(lane < rot, inv_freq[lane % half], 0.0)
    return f.reshape(1, LANES).astype(F32)


def _norm_matrices():
    r = jnp.arange(LANES)
    same = (r[:, None] // 64) == (r[None, :] // 64)
    nq = jnp.where(same, 1.0 / 64, 0.0).astype(BF16)
    nk = jnp.where(same & (r[:, None] < 64), 1.0 / 64, 0.0).astype(BF16)
    return nq, nk


def _q_gain(g):
    return (jnp.tile(g.astype(F32), 2) * Q_SCALE).reshape(1, LANES)


def _k_gain(g):
    return jnp.concatenate([g.astype(F32), jnp.ones((64,), F32)]).reshape(1, LANES)


def _interleave_kv(wk, wv, n_heads):
    d = wk.shape[0]
    wk = wk.reshape(d, n_heads, 64)
    wv = wv.reshape(d, n_heads, 64)
    return jnp.concatenate([wk, wv], axis=2).reshape(d, n_heads * 128)


def _split_cols(w, sizes):
    out, start = [], 0
    for n in sizes:
        out.append(w[:, start:start + n])
        start += n
    return out


def _mixer_layer0(x2, trig, B, S, gmix, w_in, w_out, a_q_norm, a_k_norm, b_q_norm, b_k_norm):
    sizes = (512, 64, 64, 256, 32, 8, 512, 512, 512)
    waq, wak, wav, wiq, wik, wiw, wbq, wbk, wbv = _split_cols(w_in, sizes)
    pad = jnp.zeros((D_MODEL, LANES - 40), w_in.dtype)
    w = jnp.concatenate([waq, wbq, _interleave_kv(wbk, wbv, 8), wak, wav, wiq, wik, wiw, pad],
                        axis=1).astype(BF16)
    nq, nk = _norm_matrices()
    tabs = (nq, nk, _q_gain(a_q_norm), _q_gain(b_q_norm), _k_gain(a_k_norm), _k_gain(b_k_norm))
    aq, bq, bkv, bkvt, akv, akvt, iq, ik, iwt, km = _ab_prep(x2, trig, gmix, w, tabs, B, S)
    n_blk = S // MOBA_BLOCK
    kmean = km.reshape(B, n_blk, 8, 128).transpose(0, 2, 1, 3)
    o_a = _dsa_attention(iq, iwt, ik, aq, akv, akvt).reshape(B * S, 512)
    o_b = _moba_attention(bq, bkv, bkvt, kmean).reshape(B * S, 512)
    w_out = w_out.astype(BF16)
    return [o_a, o_b], [w_out[:512], w_out[512:]]


def _mixer_layer1(x2, trig, B, S, gmix, w_in, w_out, q_norm, kcmp_norm, ksel_norm, kwin_norm,
                  pos_k, pos_v, w1_k, w2_k, w1_v, w2_v):
    G = NSA_GROUPS
    sizes = (1024,) + (256,) * 6 + (48,)
    wq, wkc, wvc, wks, wvs, wkw, wvw, wgt = _split_cols(w_in, sizes)
    pad = jnp.zeros((D_MODEL, LANES - 48), w_in.dtype)
    w = jnp.concatenate([wq, _interleave_kv(wks, wvs, G), _interleave_kv(wkw, wvw, G),
                         wkc, wvc, wgt, pad], axis=1).astype(BF16)
    nq, nk = _norm_matrices()
    tabs = (nq, nk, _q_gain(q_norm), _k_gain(ksel_norm), _k_gain(kwin_norm))
    qc, qr, kvs, kvst, kvw, kvwt, kc_raw, vc_raw, gates_t = _nsa_prep(x2, trig, gmix, w, tabs, B, S)

    n16 = S // NSA_CMP_STRIDE

    def blocks16(t):
        return (t.reshape(B, n16, NSA_CMP_STRIDE, G, HEAD_DIM).transpose(0, 3, 1, 2, 4)
                .reshape(B, G, n16, NSA_CMP_STRIDE * HEAD_DIM))

    def pos_rows(p):
        return p.astype(F32).reshape(2, NSA_CMP_STRIDE * HEAD_DIM)

    kvc, kvct = _compress(blocks16(kc_raw), blocks16(vc_raw), pos_rows(pos_k), pos_rows(pos_v),
                          w1_k.astype(BF16), w1_v.astype(BF16), w2_k.astype(BF16), w2_v.astype(BF16),
                          kcmp_norm.astype(F32).reshape(1, HEAD_DIM))
    o = _nsa_attention(qc, qr, kvc, kvct, kvs, kvst, kvw, kvwt, gates_t)
    return [o.reshape(B * S, NSA_HEADS * HEAD_DIM)], [w_out.astype(BF16)]


def _finish_layer(parts, weights, x2, mem, S, g_mem, g_src, w_q, w_kv, w_o, q_norm, k_norm,
                  g_ffn, ffn_w_in, ffn_w_out):
    row = lambda v: v.astype(F32).reshape(1, -1)
    kv = _mem_kv(mem, row(g_src), w_kv.astype(BF16), row(k_norm))
    wg = ffn_w_in[:, :D_FF].astype(BF16)
    wu = ffn_w_in[:, D_FF:].astype(BF16)
    return _post_mixer(parts, weights, x2, row(g_mem), w_q.astype(BF16), row(q_norm), kv, w_o.astype(BF16),
                       row(g_ffn), wg, wu, ffn_w_out.astype(BF16), S)


def kernel(x, mem, positions, norm_mix, norm_mem, norm_mem_src, norm_ffn, ab_w_in, ab_w_out, dsa_q_norm, dsa_k_norm, moba_q_norm, moba_k_norm, nsa_w_in, nsa_w_out, nsa_q_norm, nsa_kcmp_norm, nsa_ksel_norm, nsa_kwin_norm, nsa_cmp_pos_k, nsa_cmp_pos_v, nsa_cmp_w1_k, nsa_cmp_w2_k, nsa_cmp_w1_v, nsa_cmp_w2_v, mem_w_q, mem_w_kv, mem_w_o, mem_q_norm, mem_k_norm, ffn_w_in, ffn_w_out):
    B, S, D = x.shape
    depth = norm_mix.shape[0]
    x2 = x.reshape(B * S, D)
    trig = _rope_trig(positions.astype(F32).reshape(B * S, 1), _rope_freq_row(64, 16), _rope_freq_row(32, 8))
    row = lambda v: v.astype(F32).reshape(1, -1)
    for i in range(depth):
        j = i // 2
        if i % 2 == 0:
            parts, weights = _mixer_layer0(x2, trig, B, S, row(norm_mix[i]), ab_w_in[j], ab_w_out[j],
                               dsa_q_norm[j], dsa_k_norm[j], moba_q_norm[j], moba_k_norm[j])
        else:
            parts, weights = _mixer_layer1(x2, trig, B, S, row(norm_mix[i]), nsa_w_in[j], nsa_w_out[j],
                               nsa_q_norm[j], nsa_kcmp_norm[j], nsa_ksel_norm[j], nsa_kwin_norm[j],
                               nsa_cmp_pos_k[j], nsa_cmp_pos_v[j], nsa_cmp_w1_k[j], nsa_cmp_w2_k[j],
                               nsa_cmp_w1_v[j], nsa_cmp_w2_v[j])
        x2 = _finish_layer(parts, weights, x2, mem, S, norm_mem[i], norm_mem_src[i], mem_w_q[i], mem_w_kv[i],
                           mem_w_o[i], mem_q_norm[i], mem_k_norm[i], norm_ffn[i], ffn_w_in[i], ffn_w_out[i])
    return x2.reshape(B, S, D)
```

```python
import functools
import math

import jax
import jax.numpy as jnp
from jax import lax
from jax.experimental import pallas as pl
from jax.experimental.pallas import tpu as pltpu

F32 = jnp.float32
BF16 = jnp.bfloat16
I32 = jnp.int32
I16 = jnp.int16

D_MODEL = 1024
N_MEM = 256
HEAD_DIM = 64
ROPE_THETA = 500000.0
RMS_EPS = 1e-6
NEG_INF = -1e30
TINY = 1e-20

DSA_HEADS = 8
DSA_IDX_HEADS = 8
DSA_IDX_DIM = 32
DSA_TOPK = 256
MOBA_HEADS = 8
MOBA_BLOCK = 256
MOBA_TOPK = 3
NSA_HEADS = 16
NSA_GROUPS = 4
NSA_CMP_LEN = 32
NSA_CMP_STRIDE = 16
NSA_SEL_LEN = 64
NSA_SEL_TOPK = 16
NSA_WINDOW = 512
NSA_FORCE = 1e4
MEM_HEADS = 4
MEM_HEAD_DIM = 128
D_FF = ((8 * D_MODEL + 3 * 256 - 1) // (3 * 256)) * 256

LANES = 128
SUBLANES = 8
INT_MIN = -(2 ** 31)
VMEM_LIMIT = 60 * 1024 * 1024

PROJ_GROUP = 4
ATT_T = 256
MASK_BIAS = -1e30
M_FLOOR = -1e29
LOG2E = math.log2(math.e)
Q_SCALE = HEAD_DIM ** -0.5 * LOG2E

NT_DIMS = (((1,), (1,)), ((), ()))


def _dot(a, b):
    return jnp.dot(a, b, preferred_element_type=F32)


def _dot_nt(a, b):
    return lax.dot_general(a, b, NT_DIMS, preferred_element_type=F32)


def _split_bf16(a):
    hi = a.astype(BF16)
    return hi, (a - hi.astype(F32)).astype(BF16)


def _split_dot(a, b):
    hi, lo = _split_bf16(a)
    return _dot(hi, b) + _dot(lo, b)


def _rms_rows(x, gain):
    ms = jnp.mean(x * x, axis=-1, keepdims=True)
    return x * lax.rsqrt(ms + RMS_EPS) * gain


def _params(sem):
    return pltpu.CompilerParams(dimension_semantics=sem, vmem_limit_bytes=VMEM_LIMIT)


def _head_norm(y, norm_m, gain):
    ms = _split_dot(y * y, norm_m)
    return y * lax.rsqrt(ms + RMS_EPS) * gain


def _rope(y, c, s, lo_mask, half):
    sw = jnp.where(lo_mask, pltpu.roll(y, LANES - half, 1), pltpu.roll(y, half, 1))
    return y * c + sw * s


def _lane_iota(shape):
    return lax.broadcasted_iota(I32, shape, 1)


def _row_iota(shape):
    return lax.broadcasted_iota(I32, shape, 0)


def _rope_tables(pos, ftab, period, half):
    ang = pos * ftab
    lane = _lane_iota(ang.shape) % period
    c = jnp.cos(ang)
    s = jnp.sin(ang) * jnp.where(lane < half, -1.0, 1.0)
    return c, s


class _ColumnProjector:
    def __init__(self, xn, w_ref):
        self.xn, self.w_ref, self.groups = xn, w_ref, {}

    def __call__(self, j):
        g, u = divmod(j, PROJ_GROUP)
        if g not in self.groups:
            width = PROJ_GROUP * LANES
            lo = g * width
            hi = min(lo + width, self.w_ref.shape[1])
            self.groups[g] = _dot(self.xn, self.w_ref[:, lo:hi])
        return self.groups[g][:, u * LANES:(u + 1) * LANES]


def _kv_column(yc, nk, gain, c64k, s64k, lo64, first64):
    kn = jnp.where(first64, _head_norm(yc, nk, gain), yc)
    return _rope(kn, c64k, s64k, lo64, 8)


def _rope_trig_kernel(pos_ref, f64_ref, f32_ref, o_ref):
    pos = pos_ref[...]
    c64, s64 = _rope_tables(pos, f64_ref[...], 64, 8)
    c32, s32 = _rope_tables(pos, f32_ref[...], 32, 4)
    o_ref[...] = jnp.concatenate([c64, s64, c32, s32], axis=1)


def _rope_trig(pos2, f64, f32t, tm=1024):
    T = pos2.shape[0]
    return pl.pallas_call(
        _rope_trig_kernel, grid=(T // tm,),
        in_specs=[pl.BlockSpec((tm, 1), lambda i: (i, 0)),
                  pl.BlockSpec(f64.shape, lambda i: (0, 0)), pl.BlockSpec(f32t.shape, lambda i: (0, 0))],
        out_specs=pl.BlockSpec((tm, 4 * LANES), lambda i: (i, 0)),
        out_shape=jax.ShapeDtypeStruct((T, 4 * LANES), F32),
        compiler_params=_params(("parallel",)), name="rope_trig",
    )(pos2, f64, f32t)


def _ab_prep_kernel(x_ref, trig_ref, gmix_ref, w_ref, nq_ref, nk_ref,
                    gaq_ref, gbq_ref, gak_ref, gbk_ref,
                    aq_ref, bq_ref, bkv_ref, bkvt_ref, akv_ref, akvt_ref, iq_ref, ik_ref, iwt_ref, km_ref,
                    *, n_tiles):
    xn = _rms_rows(x_ref[...], gmix_ref[...]).astype(BF16)
    c64, s64, c32, s32 = [trig_ref[:, j * LANES:(j + 1) * LANES] for j in range(4)]
    lane = _lane_iota(c64.shape)
    lo64 = (lane % 64) < 8
    lo32 = (lane % 32) < 4
    first64 = lane < 64
    c64k = jnp.where(first64, c64, 1.0)
    s64k = jnp.where(first64, s64, 0.0)
    first32 = lane < 32
    c32k = jnp.where(first32, c32, 1.0)
    s32k = jnp.where(first32, s32, 0.0)
    nq = nq_ref[...]
    nk = nk_ref[...]

    col = _ColumnProjector(xn, w_ref)

    for j in range(4):
        q = _rope(_head_norm(col(j), nq, gaq_ref[...]), c64, s64, lo64, 8)
        aq_ref[2 * j] = q[:, :64].astype(BF16)
        aq_ref[2 * j + 1] = q[:, 64:].astype(BF16)
    for j in range(4):
        q = _rope(_head_norm(col(4 + j), nq, gbq_ref[...]), c64, s64, lo64, 8)
        bq_ref[2 * j] = q[:, :64].astype(BF16)
        bq_ref[2 * j + 1] = q[:, 64:].astype(BF16)
    blk_onehot = jnp.where(lane == HEAD_DIM + pl.program_id(0) % n_tiles, 1.0, 0.0)
    for h in range(8):
        kv = _kv_column(col(8 + h), nk, gbk_ref[...], c64k, s64k, lo64, first64)
        bkv_ref[h] = jnp.where(first64, kv, blk_onehot).astype(BF16)
        bkvt_ref[h] = kv.T.astype(BF16)
        km_ref[h:h + 1, :] = jnp.mean(kv, axis=0, keepdims=True)
    kv = _kv_column(col(16), nk, gak_ref[...], c64k, s64k, lo64, first64)
    akv_ref[...] = kv.astype(BF16)
    akvt_ref[...] = kv.T.astype(BF16)
    for j in range(2):
        q = _rope(col(17 + j), c32, s32, lo32, 4)
        for u in range(4):
            iq_ref[4 * j + u] = q[:, 32 * u:32 * (u + 1)].astype(BF16)
    yc = col(19)
    ik_ref[...] = _rope(yc, c32k, s32k, lo32, 4)[:, :32].astype(BF16)
    iwt_ref[...] = yc.T[32:40, :]


def _ab_prep(x2, trig, gmix, w, tabs, B, S):
    T = x2.shape[0]
    tm = ATT_T
    nt = S // tm
    n_cols = w.shape[1]
    nq, nk, gaq, gbq, gak, gbk = tabs

    def full(a):
        return pl.BlockSpec(a.shape, lambda i: (0,) * a.ndim)

    def hm(width, heads=8):
        return pl.BlockSpec((None, heads, tm, width), lambda i: (i // nt, 0, i % nt, 0))

    def tokm(width):
        return pl.BlockSpec((None, tm, width), lambda i: (i // nt, i % nt, 0))

    out_shape = (
        jax.ShapeDtypeStruct((B, 8, S, 64), BF16),
        jax.ShapeDtypeStruct((B, 8, S, 64), BF16),
        jax.ShapeDtypeStruct((B, 8, S, 128), BF16),
        jax.ShapeDtypeStruct((B, 8, nt, 128, tm), BF16),
        jax.ShapeDtypeStruct((B, S, 128), BF16),
        jax.ShapeDtypeStruct((B, nt, 128, tm), BF16),
        jax.ShapeDtypeStruct((B, 8, S, 32), BF16),
        jax.ShapeDtypeStruct((B, S, 32), BF16),
        jax.ShapeDtypeStruct((B, 8, S), F32),
        jax.ShapeDtypeStruct((T // tm, 8, 128), F32),
    )
    out_specs = (hm(64), hm(64), hm(128),
                 pl.BlockSpec((None, 8, None, 128, tm), lambda i: (i // nt, 0, i % nt, 0, 0)),
                 tokm(128),
                 pl.BlockSpec((None, None, 128, tm), lambda i: (i // nt, i % nt, 0, 0)),
                 hm(32), tokm(32),
                 pl.BlockSpec((None, 8, tm), lambda i: (i // nt, 0, i % nt)),
                 pl.BlockSpec((None, 8, 128), lambda i: (i, 0, 0)))
    in_specs = [pl.BlockSpec((tm, D_MODEL), lambda i: (i, 0)),
                pl.BlockSpec((tm, 4 * LANES), lambda i: (i, 0)),
                full(gmix), pl.BlockSpec((D_MODEL, n_cols), lambda i: (0, 0)),
                full(nq), full(nk), full(gaq), full(gbq), full(gak), full(gbk)]
    return pl.pallas_call(
        functools.partial(_ab_prep_kernel, n_tiles=nt), grid=(T // tm,), in_specs=in_specs, out_specs=out_specs,
        out_shape=out_shape, compiler_params=_params(("parallel",)), name="ab_prep",
    )(x2, trig, gmix, w, nq, nk, gaq, gbq, gak, gbk)


def _pad_q(q):
    return jnp.concatenate([q, jnp.zeros_like(q)], axis=1)


def _bias_lanes(q, rows):
    n, tq = rows.shape
    parts = [jnp.zeros((HEAD_DIM, tq), F32), rows]
    if n < HEAD_DIM:
        parts.append(jnp.zeros((HEAD_DIM - n, tq), F32))
    lanes = jnp.concatenate(parts, axis=0).T.astype(BF16)
    return jnp.where(_lane_iota(q.shape) < HEAD_DIM, q, lanes)


class _Flash:
    def __init__(self, m_ref, l_ref, acc_ref, s_ref, cmax_ref, p_ref, tq):
        self.m_ref, self.l_ref, self.acc_ref, self.tq = m_ref, l_ref, acc_ref, tq
        self.s_ref, self.cmax_ref, self.p_ref = s_ref, cmax_ref, p_ref

    def reset(self):
        self.m_ref[...] = jnp.full(self.m_ref.shape, M_FLOOR, F32)
        self.l_ref[...] = jnp.zeros(self.l_ref.shape, F32)
        self.acc_ref[...] = jnp.zeros(self.acc_ref.shape, F32)

    def _scores(self, buf, qs, kvs, biases):
        tq = self.tq
        for i in range(len(qs)):
            s = _dot_nt(kvs[i], qs[i])
            if biases[i] is not None:
                s = s + biases[i]
            self.s_ref[buf, i, :s.shape[0], :] = s
            self.cmax_ref[buf, :, i * tq:(i + 1) * tq] = jnp.max(s, axis=0, keepdims=True)

    def update(self, qs, kvs, kvts, biases):
        self._scores(0, qs, kvs, biases)
        self._finish(0, kvts)

    def run(self, qs, count, operands):
        def scores(c, buf):
            kvs, _, biases = operands(c)
            self._scores(buf, qs, kvs, biases)

        def finish(c, buf):
            self._finish(buf, operands(c)[1])

        last = jnp.maximum(count - 1, 0)
        scores(0, 0)

        def two_chunks(pp, carry):
            c = 2 * pp
            scores(c + 1, 1)
            finish(c, 0)
            scores(jnp.minimum(c + 2, last), 0)
            finish(c + 1, 1)
            return carry

        lax.fori_loop(0, count // 2, two_chunks, 0)

        @pl.when(count % 2 == 1)
        def _():
            finish(count - 1, 0)

    def _finish(self, buf, kvts):
        n = len(kvts)
        tq = self.tq
        kc = sum(kvt.shape[1] for kvt in kvts[0])
        alphas = []
        for i in range(n):
            cols = slice(i * tq, (i + 1) * tq)
            m = self.m_ref[:, cols]
            m_new = jnp.maximum(m, self.cmax_ref[buf, :, cols])
            p = jnp.exp2(self.s_ref[buf, i, :kc, :] - m_new)
            alpha = jnp.exp2(m - m_new)
            self.m_ref[:, cols] = m_new
            self.l_ref[:, cols] = alpha * self.l_ref[:, cols] + p.reshape(-1, SUBLANES, tq).sum(axis=0)
            self.p_ref[i, :kc, :] = p.astype(BF16)
            alphas.append(alpha)
        for i in range(n):
            cols = slice(i * tq, (i + 1) * tq)
            pv, r0 = None, 0
            for kvt in kvts[i]:
                part = _dot(kvt, self.p_ref[i, r0:r0 + kvt.shape[1], :])
                pv = part if pv is None else pv + part
                r0 += kvt.shape[1]
            self.acc_ref[:, cols] = alphas[i] * self.acc_ref[:, cols] + pv

    def result(self, slot):
        cols = slice(slot * self.tq, (slot + 1) * self.tq)
        l = jnp.sum(self.l_ref[:, cols], axis=0, keepdims=True)
        return self.acc_ref[:, cols] / jnp.maximum(l, TINY)


def _flash_scratch(n_slots, tq, kc):
    return [pltpu.VMEM((1, n_slots * tq), F32), pltpu.VMEM((SUBLANES, n_slots * tq), F32),
            pltpu.VMEM((LANES, n_slots * tq), F32),
            pltpu.VMEM((2, n_slots, kc, tq), F32), pltpu.VMEM((2, 1, n_slots * tq), F32),
            pltpu.VMEM((n_slots, kc, tq), BF16)]


def _softmax_direct(qs, kv, kvts, bias):
    scores = [_dot_nt(kv, q) for q in qs]
    probs, inv_ls = [], []
    for s in scores:
        s = s + bias
        m = jnp.maximum(jnp.max(s, axis=0, keepdims=True), M_FLOOR)
        p = jnp.exp2(s - m)
        inv_ls.append(1.0 / jnp.maximum(jnp.sum(p, axis=0, keepdims=True), TINY))
        probs.append(p)
    outs = []
    for p in probs:
        pb = p.astype(BF16)
        o, r0 = None, 0
        for kvt in kvts:
            part = _dot(kvt, pb[r0:r0 + kvt.shape[1]])
            o = part if o is None else o + part
            r0 += kvt.shape[1]
        outs.append(o)
    return probs, inv_ls, outs


def _causal_bias(t):
    return jnp.where(_row_iota((t, t)) <= _lane_iota((t, t)), 0.0, MASK_BIAS)


def _store_heads(o_ref, heads_t):
    tq = heads_t[0].shape[1]
    lane = _lane_iota((tq, LANES))
    for u in range(len(heads_t) // 2):
        even = pltpu.roll(heads_t[2 * u].T, 64, 1)
        odd = heads_t[2 * u + 1].T
        o_ref[:, u * LANES:(u + 1) * LANES] = jnp.where(lane < 64, even, odd).astype(o_ref.dtype)


def _rank_select_t(v, n_valid, n_top):
    n = v.shape[0]
    row = _row_iota(v.shape)
    rank = jnp.zeros(v.shape, F32)
    for m in range(n):
        vm = v[m:m + 1, :]
        ahead = (vm > v) | ((vm == v) & (m < row))
        if n_valid is not None:
            ahead = ahead & (m < n_valid)
        rank = rank + jnp.where(ahead, 1.0, 0.0)
    sel = rank < n_top
    if n_valid is not None:
        sel = sel & (row < n_valid)
    return jnp.where(sel, 1.0, 0.0)


def _dsa_kernel(iq_ref, iwt_ref, ik_ref, aq_ref, akv_ref, akvt_ref, o_ref,
                sk_ref, half_ref, bias_ref, xcut_ref, *flash_refs, k_top, index_bits):
    t = ATT_T
    i = pl.program_id(1)
    n_ch = i + 1
    kio = _row_iota((t, t))
    qio = _lane_iota((t, t))

    def causal(c):
        return (c - i) * t + kio <= qio

    def score_chunk(c):
        k0 = pl.multiple_of(c * t, t)
        ikc = ik_ref[pl.ds(k0, t), :]
        sc = jnp.zeros((t, t), F32)
        for h in range(DSA_IDX_HEADS):
            logit = _dot_nt(ikc, iq_ref[h])
            sc = sc + iwt_ref[h:h + 1, :] * jnp.maximum(logit, 0.0)
        sc = jnp.where(sc == 0.0, 0.0, sc)
        bits = pltpu.bitcast(sc, I32)
        key = bits ^ ((bits >> 31) & 0x7FFFFFFF)
        key = jnp.where(causal(c), key, INT_MIN)
        sk_ref[c] = key
        half_ref[c] = (key >> 16).astype(I16)

    def score_pair(cc, carry):
        score_chunk(2 * cc)
        score_chunk(2 * cc + 1)
        return carry

    lax.fori_loop(0, (n_ch + 1) // 2, score_pair, 0)

    def count(pred):
        def body(c, acc8):
            ind = jnp.where(pred(sk_ref[c], c), 1.0, 0.0)
            return acc8 + ind.reshape(-1, SUBLANES, t).sum(axis=0)
        acc8 = lax.fori_loop(0, n_ch, body, jnp.zeros((SUBLANES, t), F32))
        return jnp.sum(acc8, axis=0, keepdims=True)

    def count_half(cand):
        rows = 2 * SUBLANES

        def body(cc, acc):
            parts = []
            for c in (2 * cc, 2 * cc + 1):
                ind = jnp.where(half_ref[c] >= cand, jnp.bfloat16(1), jnp.bfloat16(0))
                parts += [ind[rows * j:rows * (j + 1), :] for j in range(t // rows)]
            while len(parts) > 1:
                parts = [parts[2 * j] + parts[2 * j + 1] for j in range(len(parts) // 2)]
            return acc + parts[0].astype(F32)
        acc = lax.fori_loop(0, (n_ch + 1) // 2, body, jnp.zeros((rows, t), F32))
        return jnp.sum(acc, axis=0, keepdims=True)

    def half_search():
        def bit_step(b, v):
            cand = v + lax.shift_left(jnp.int32(1), 15 - b)
            return jnp.where(count_half(cand.astype(I16)) >= k_top, cand, v)
        return lax.fori_loop(0, 16, bit_step, jnp.full((1, t), -(2 ** 15), I32))

    thr_hi = half_search()

    def low_half_chunk(c, carry):
        key = sk_ref[c]
        hi = key >> 16
        lo = (key & 0xFFFF) - 2 ** 15
        half_ref[c] = jnp.where(hi > thr_hi, 2 ** 15 - 1, jnp.where(hi < thr_hi, -(2 ** 15), lo)).astype(I16)
        return carry

    lax.fori_loop(0, n_ch, low_half_chunk, 0)
    thr = lax.shift_left(thr_hi, 16) + (half_search() + 2 ** 15)

    need = k_top - count(lambda blk, c: blk > thr)
    n_ge = count(lambda blk, c: blk >= thr)
    xcut_ref[...] = jnp.full((1, t), 2 ** 30, I32)

    @pl.when(jnp.max(n_ge) > k_top)
    def _():
        def x_step(b, x):
            cand = x + lax.shift_left(jnp.int32(1), index_bits - 1 - b)
            ties_below = count(lambda blk, c: (blk == thr) & (c * t + kio < cand))
            return jnp.where(ties_below <= need, cand, x)
        xcut_ref[...] = lax.fori_loop(0, index_bits, x_step, jnp.zeros((1, t), I32))

    xcut = xcut_ref[...]

    n_pairs = (n_ch + 1) // 2

    def bias_chunk(c, carry):
        blk = sk_ref[jnp.minimum(c, i)]
        keep = (blk > thr) | ((blk == thr) & (c * t + kio < xcut))
        bias_ref[c] = jnp.where(keep & causal(c), 0.0, MASK_BIAS)
        return carry

    lax.fori_loop(0, 2 * n_pairs, bias_chunk, 0)

    flash = _Flash(*flash_refs, t)
    flash.reset()
    qs = [_pad_q(aq_ref[h]) for h in range(DSA_HEADS)]

    n = DSA_HEADS

    def att_pair(cc):
        c0 = 2 * cc
        k0 = pl.multiple_of(c0 * t, 2 * t)
        kv = akv_ref[pl.ds(k0, 2 * t), :]
        bias = jnp.concatenate([bias_ref[c0], bias_ref[c0 + 1]], axis=0)
        return [kv] * n, [[akvt_ref[c0], akvt_ref[c0 + 1]]] * n, [bias] * n

    flash.run(qs, n_pairs, att_pair)
    _store_heads(o_ref, [flash.result(h) for h in range(DSA_HEADS)])


def _dsa_attention(iq, iwt, ik, aq, akv, akvt):
    B, _, S, _ = aq.shape
    t = ATT_T
    nt = S // t
    k_top = min(DSA_TOPK, S // 4)
    in_specs = [
        pl.BlockSpec((None, 8, t, 32), lambda b, i: (b, 0, i, 0)),
        pl.BlockSpec((None, 8, t), lambda b, i: (b, 0, i)),
        pl.BlockSpec((None, S, 32), lambda b, i: (b, 0, 0)),
        pl.BlockSpec((None, 8, t, 64), lambda b, i: (b, 0, i, 0)),
        pl.BlockSpec((None, S, 128), lambda b, i: (b, 0, 0)),
        pl.BlockSpec((None, nt, 128, t), lambda b, i: (b, 0, 0, 0)),
    ]
    return pl.pallas_call(
        functools.partial(_dsa_kernel, k_top=k_top, index_bits=S.bit_length()),
        grid=(B, nt), in_specs=in_specs,
        out_specs=pl.BlockSpec((None, t, 512), lambda b, i: (b, i, 0)),
        out_shape=jax.ShapeDtypeStruct((B, S, 512), BF16),
        scratch_shapes=[pltpu.VMEM((nt, t, t), I32), pltpu.VMEM((nt, t, t), I16), pltpu.VMEM((nt, t, t), F32),
                        pltpu.VMEM((1, t), I32)] + _flash_scratch(DSA_HEADS, t, 2 * t),
        compiler_params=_params(("parallel", "parallel")), name="dsa_attention",
    )(iq, iwt, ik, aq, akv, akvt)


MOBA_HPS = 8


def _moba_kernel(q_ref, kv_ref, kvt_ref, km_ref, o_ref, *flash_refs, n_top):
    t = ATT_T
    own = pl.program_id(2)
    causal = _causal_bias(t)
    flash = _Flash(*flash_refs, t)
    flash.reset()
    qs = []
    for hh in range(MOBA_HPS):
        q = _pad_q(q_ref[hh])
        km_hi, km_lo = _split_bf16(km_ref[hh])
        gate = _dot_nt(km_hi, q) + _dot_nt(km_lo, q)
        keep = _rank_select_t(gate, own, n_top)
        keep = jnp.where(_row_iota(keep.shape) == own, 1.0, keep)
        qs.append(_bias_lanes(q, (keep - 1.0) * (-MASK_BIAS)))

    def operands(cc):
        n0 = 2 * cc
        k0 = pl.multiple_of(n0 * t, 2 * t)
        heads = range(MOBA_HPS)
        return (n0, [kv_ref[hh, pl.ds(k0, 2 * t), :] for hh in heads],
                [[kvt_ref[hh, n0], kvt_ref[hh, n0 + 1]] for hh in heads])

    def past_pair(cc):
        _, kvs, kvts = operands(cc)
        return kvs, kvts, [None] * MOBA_HPS

    flash.run(qs, own // 2, past_pair)
    heads = range(MOBA_HPS)

    @pl.when(own % 2 == 0)
    def _():
        k0 = pl.multiple_of(own * t, t)
        flash.update(qs, [kv_ref[hh, pl.ds(k0, t), :] for hh in heads],
                     [[kvt_ref[hh, own]] for hh in heads], [causal] * MOBA_HPS)

    @pl.when(own % 2 == 1)
    def _():
        _, kvs, kvts = operands(own // 2)
        bias = jnp.concatenate([jnp.zeros((t, t), F32), causal], axis=0)
        flash.update(qs, kvs, kvts, [bias] * MOBA_HPS)

    _store_heads(o_ref, [flash.result(hh) for hh in range(MOBA_HPS)])


def _moba_attention(bq, bkv, bkvt, kmean):
    B, H, S, _ = bq.shape
    t = ATT_T
    hps = MOBA_HPS
    n_blk = S // MOBA_BLOCK
    assert t == MOBA_BLOCK and n_blk % 2 == 0 and H % hps == 0
    n_top = max(1, min(MOBA_TOPK, n_blk - 1))
    in_specs = [
        pl.BlockSpec((None, hps, t, 64), lambda b, h, i: (b, h, i, 0)),
        pl.BlockSpec((None, hps, S, 128), lambda b, h, i: (b, h, 0, 0)),
        pl.BlockSpec((None, hps, n_blk, 128, t), lambda b, h, i: (b, h, 0, 0, 0)),
        pl.BlockSpec((None, hps, n_blk, 128), lambda b, h, i: (b, h, 0, 0)),
    ]
    return pl.pallas_call(
        functools.partial(_moba_kernel, n_top=n_top), grid=(B, H // hps, S // t), in_specs=in_specs,
        out_specs=pl.BlockSpec((None, t, hps * 64), lambda b, h, i: (b, i, h)),
        out_shape=jax.ShapeDtypeStruct((B, S, H * 64), BF16),
        scratch_shapes=_flash_scratch(hps, t, 2 * t),
        compiler_params=_params(("parallel", "parallel", "parallel")), name="moba_attention",
    )(bq, bkv, bkvt, kmean)


def _lane_group_norm(y, gain, width):
    outs = []
    for j in range(y.shape[1] // width):
        yc = y[:, j * width:(j + 1) * width]
        outs.append(_rms_rows(yc, gain))
    return jnp.concatenate(outs, axis=1)


def _mem_kv_kernel(m_ref, g_ref, w_ref, gk_ref, o_ref):
    mn = _rms_rows(m_ref[...], g_ref[...]).astype(BF16)
    y = _dot(mn, w_ref[...])
    hw = MEM_HEADS * MEM_HEAD_DIM
    k = _lane_group_norm(y[:, :hw], gk_ref[...], MEM_HEAD_DIM)
    o_ref[...] = jnp.concatenate([k, y[:, hw:]], axis=1).astype(BF16)


def _mem_kv(mem, g, w, gk):
    B, M, _ = mem.shape
    n = w.shape[1]
    return pl.pallas_call(
        _mem_kv_kernel, grid=(B,),
        in_specs=[pl.BlockSpec((None, M, D_MODEL), lambda b: (b, 0, 0)),
                  pl.BlockSpec(g.shape, lambda b: (0, 0)),
                  pl.BlockSpec(w.shape, lambda b: (0, 0)),
                  pl.BlockSpec(gk.shape, lambda b: (0, 0))],
        out_specs=pl.BlockSpec((None, M, n), lambda b: (b, 0, 0)),
        out_shape=jax.ShapeDtypeStruct((B, M, n), BF16),
        compiler_params=_params(("parallel",)), name="mem_kv",
    )(mem, g, w, gk)


def _mem_attend(x, g_ref, wq_ref, gq_ref, kv_ref, wo_ref):
    xn = _rms_rows(x, g_ref[...]).astype(BF16)
    q = _lane_group_norm(_dot(xn, wq_ref[...]), gq_ref[...], MEM_HEAD_DIM).astype(BF16)
    hw = MEM_HEADS * MEM_HEAD_DIM
    scale = MEM_HEAD_DIM ** -0.5
    outs = []
    for h in range(MEM_HEADS):
        cols = slice(h * MEM_HEAD_DIM, (h + 1) * MEM_HEAD_DIM)
        k = kv_ref[:, cols]
        v = kv_ref[:, hw + h * MEM_HEAD_DIM:hw + (h + 1) * MEM_HEAD_DIM]
        s = _dot_nt(q[:, cols], k) * scale
        p = jnp.exp(s - jnp.max(s, axis=-1, keepdims=True))
        p = p / jnp.sum(p, axis=-1, keepdims=True)
        outs.append(_dot(p.astype(BF16), v))
    o = jnp.concatenate(outs, axis=1).astype(BF16)
    return x + _dot(o, wo_ref[...])


def _post_mixer_kernel(*refs, n_in):
    a_refs = refs[:n_in]
    w_refs = refs[n_in:2 * n_in]
    (x_ref, gm_ref, wq_ref, gq_ref, kv_ref, wo_ref, gf_ref, wg_ref, wu_ref, wd_ref,
     o_ref, xn_ref, acc_ref) = refs[2 * n_in:]
    j = pl.program_id(1)

    @pl.when(j == 0)
    def _():
        x = x_ref[...]
        for a_ref, w_ref in zip(a_refs, w_refs):
            x = x + _dot(a_ref[...], w_ref[...])
        x = _mem_attend(x, gm_ref, wq_ref, gq_ref, kv_ref, wo_ref)
        xn_ref[...] = _rms_rows(x, gf_ref[...]).astype(BF16)
        acc_ref[...] = x

    xn = xn_ref[...]
    gate = _dot(xn, wg_ref[...])
    up = _dot(xn, wu_ref[...])
    act = (gate * jax.nn.sigmoid(gate) * up).astype(BF16)
    acc_ref[...] += _dot(act, wd_ref[...])

    @pl.when(j == pl.num_programs(1) - 1)
    def _():
        o_ref[...] = acc_ref[...]


def _post_mixer(parts, weights, x2, g_mem, wq, gq, kv, wo, g_ffn, wg, wu, wd, S, tm=512, n_split=2):
    T = x2.shape[0]
    nt = S // tm
    tf = D_FF // n_split
    n_in = len(parts)
    M, n = kv.shape[1], kv.shape[2]

    def const(a):
        return pl.BlockSpec(a.shape, lambda i, j: (0,) * a.ndim)

    in_specs = ([pl.BlockSpec((tm, p.shape[1]), lambda i, j: (i, 0)) for p in parts]
                + [const(w) for w in weights]
                + [pl.BlockSpec((tm, D_MODEL), lambda i, j: (i, 0)),
                   const(g_mem), const(wq), const(gq),
                   pl.BlockSpec((None, M, n), lambda i, j: (i // nt, 0, 0)),
                   const(wo), const(g_ffn),
                   pl.BlockSpec((D_MODEL, tf), lambda i, j: (0, j)),
                   pl.BlockSpec((D_MODEL, tf), lambda i, j: (0, j)),
                   pl.BlockSpec((tf, D_MODEL), lambda i, j: (j, 0))])
    return pl.pallas_call(
        functools.partial(_post_mixer_kernel, n_in=n_in), grid=(T // tm, n_split), in_specs=in_specs,
        out_specs=pl.BlockSpec((tm, D_MODEL), lambda i, j: (i, 0)),
        out_shape=jax.ShapeDtypeStruct((T, D_MODEL), F32),
        scratch_shapes=[pltpu.VMEM((tm, D_MODEL), BF16), pltpu.VMEM((tm, D_MODEL), F32)],
        compiler_params=_params(("parallel", "arbitrary")), name="post_mixer",
    )(*parts, *weights, x2, g_mem, wq, gq, kv, wo, g_ffn, wg, wu, wd)


def _nsa_prep_kernel(x_ref, trig_ref, gmix_ref, w_ref, nq_ref, nk_ref,
                     gq_ref, gks_ref, gkw_ref,
                     qc_ref, qr_ref, kvs_ref, kvst_ref, kvw_ref, kvwt_ref, kc_ref, vc_ref, gtt_ref,
                     *, n_tiles):
    xn = _rms_rows(x_ref[...], gmix_ref[...]).astype(BF16)
    c64, s64 = trig_ref[:, :LANES], trig_ref[:, LANES:]
    lane = _lane_iota(c64.shape)
    lo64 = (lane % 64) < 8
    first64 = lane < 64
    c64k = jnp.where(first64, c64, 1.0)
    s64k = jnp.where(first64, s64, 0.0)
    nq = nq_ref[...]
    nk = nk_ref[...]

    col = _ColumnProjector(xn, w_ref)

    for j in range(8):
        qn = _head_norm(col(j), nq, gq_ref[...])
        qr = _rope(qn, c64, s64, lo64, 8)
        qc_ref[2 * j] = qn[:, :64].astype(BF16)
        qc_ref[2 * j + 1] = qn[:, 64:].astype(BF16)
        qr_ref[2 * j] = qr[:, :64].astype(BF16)
        qr_ref[2 * j + 1] = qr[:, 64:].astype(BF16)
    tile = pl.program_id(0) % n_tiles
    sel_blk = tile * (ATT_T // NSA_SEL_LEN) + lax.shift_right_logical(
        _row_iota(c64.shape), NSA_SEL_LEN.bit_length() - 1)
    blk_onehot = jnp.where(lane == HEAD_DIM + sel_blk, 1.0, 0.0)
    for g in range(NSA_GROUPS):
        kv = _kv_column(col(8 + g), nk, gks_ref[...], c64k, s64k, lo64, first64)
        kvs_ref[g] = jnp.where(first64, kv, blk_onehot).astype(BF16)
        kvst_ref[g] = kv.T.astype(BF16)
        kv = _kv_column(col(12 + g), nk, gkw_ref[...], c64k, s64k, lo64, first64)
        kvw_ref[g] = kv.astype(BF16)
        kvwt_ref[g] = kv.T.astype(BF16)
    kc_ref[...] = jnp.concatenate([col(16), col(17)], axis=1)
    vc_ref[...] = jnp.concatenate([col(18), col(19)], axis=1)
    gates_t = jax.nn.sigmoid(col(20)).T
    for g in range(NSA_GROUPS):
        gtt_ref[g] = gates_t[12 * g:12 * (g + 1), :]


def _nsa_prep(x2, trig, gmix, w, tabs, B, S):
    T = x2.shape[0]
    tm = ATT_T
    nt = S // tm
    nq, nk, gq, gks, gkw = tabs

    def full(a):
        return pl.BlockSpec(a.shape, lambda i: (0,) * a.ndim)

    def hm(width, heads):
        return pl.BlockSpec((None, heads, tm, width), lambda i: (i // nt, 0, i % nt, 0))

    def hmt(heads):
        return pl.BlockSpec((None, heads, None, 128, tm), lambda i: (i // nt, 0, i % nt, 0, 0))

    def tokm(width):
        return pl.BlockSpec((None, tm, width), lambda i: (i // nt, i % nt, 0))

    out_shape = (
        jax.ShapeDtypeStruct((B, 16, S, 64), BF16),
        jax.ShapeDtypeStruct((B, 16, S, 64), BF16),
        jax.ShapeDtypeStruct((B, 4, S, 128), BF16),
        jax.ShapeDtypeStruct((B, 4, nt, 128, tm), BF16),
        jax.ShapeDtypeStruct((B, 4, S, 128), BF16),
        jax.ShapeDtypeStruct((B, 4, nt, 128, tm), BF16),
        jax.ShapeDtypeStruct((B, S, 256), F32),
        jax.ShapeDtypeStruct((B, S, 256), F32),
        jax.ShapeDtypeStruct((B, 4, 12, S), F32),
    )
    out_specs = (hm(64, 16), hm(64, 16), hm(128, 4), hmt(4), hm(128, 4), hmt(4), tokm(256), tokm(256),
                 pl.BlockSpec((None, 4, 12, tm), lambda i: (i // nt, 0, 0, i % nt)))
    in_specs = [pl.BlockSpec((tm, D_MODEL), lambda i: (i, 0)),
                pl.BlockSpec((tm, 2 * LANES), lambda i: (i, 0)),
                full(gmix), full(w), full(nq), full(nk), full(gq), full(gks), full(gkw)]
    return pl.pallas_call(
        functools.partial(_nsa_prep_kernel, n_tiles=nt), grid=(T // tm,), in_specs=in_specs, out_specs=out_specs,
        out_shape=out_shape, compiler_params=_params(("parallel",)), name="nsa_prep",
    )(x2, trig, gmix, w, nq, nk, gq, gks, gkw)


def _compress_one(x16, pa, pb, w1a, w1b, w2):
    n16 = x16.shape[0]
    h_a = _dot((x16 + pa).astype(BF16), w1a)
    h_b = _dot((x16 + pb).astype(BF16), w1b)
    pre = h_a + pltpu.roll(h_b, n16 - 1, 0)
    act = pre * jax.nn.sigmoid(pre)
    return _dot(act.astype(BF16), w2)


def _compress_kernel(xk_ref, xv_ref, pk_ref, pv_ref, w1k_ref, w1v_ref, w2k_ref, w2v_ref, gk_ref,
                     o_ref, ot_ref):
    half = w1k_ref.shape[0] // 2
    k = _compress_one(xk_ref[...], pk_ref[0:1, :], pk_ref[1:2, :],
                      w1k_ref[:half, :], w1k_ref[half:, :], w2k_ref[...])
    k = _rms_rows(k, gk_ref[...])
    v = _compress_one(xv_ref[...], pv_ref[0:1, :], pv_ref[1:2, :],
                      w1v_ref[:half, :], w1v_ref[half:, :], w2v_ref[...])
    kv = jnp.concatenate([k, v], axis=1)
    o_ref[...] = kv.astype(BF16)
    ot_ref[...] = kv.T.astype(BF16)


def _compress(xk16, xv16, pk, pv, w1k, w1v, w2k, w2v, gk):
    B, G, n16, width = xk16.shape

    def full(a):
        return pl.BlockSpec(a.shape, lambda b, g: (0,) * a.ndim)

    xspec = pl.BlockSpec((None, None, n16, width), lambda b, g: (b, g, 0, 0))
    return pl.pallas_call(
        _compress_kernel, grid=(B, G),
        in_specs=[xspec, xspec, full(pk), full(pv), full(w1k), full(w1v), full(w2k), full(w2v), full(gk)],
        out_specs=(pl.BlockSpec((None, None, n16, 128), lambda b, g: (b, g, 0, 0)),
                   pl.BlockSpec((None, None, 128, n16), lambda b, g: (b, g, 0, 0))),
        out_shape=(jax.ShapeDtypeStruct((B, G, n16, 128), BF16),
                   jax.ShapeDtypeStruct((B, G, 128, n16), BF16)),
        compiler_params=_params(("parallel", "parallel")), name="nsa_compress",
    )(xk16, xv16, pk, pv, w1k, w1v, w2k, w2v, gk)


NSA_GPS = 4


def _nsa_front(qc_ref, qr_ref, kvc_ref, kvct_ref, kvw_ref, kvwt_ref, gtt_ref, sel_ref, part_ref,
               i, n_cmp, n_top):
    t = ATT_T
    HG = NSA_HEADS // NSA_GROUPS
    n_sel = sel_ref.shape[0]
    n16 = kvc_ref.shape[0]
    t0 = i * t

    kvc = kvc_ref[...]
    kvct = kvct_ref[...]
    n_id = _row_iota((n16, t))
    q_id = t0 + _lane_iota((n16, t))
    visible = (n_id < n_cmp) & (n_id * NSA_CMP_STRIDE + (NSA_CMP_LEN - 1) <= q_id)
    bias_c = jnp.where(visible, 0.0, MASK_BIAS)
    p_sum = jnp.zeros((n16, t), F32)
    o_c = []
    probs, inv_ls, outs = _softmax_direct([_pad_q(qc_ref[j]) for j in range(HG)], kvc, [kvct], bias_c)
    for j in range(HG):
        p_sum = p_sum + probs[j] * inv_ls[j]
        o_c.append(outs[j] * inv_ls[j])

    b_id = _row_iota((n_sel, n16)) * NSA_SEL_LEN
    r_id = _lane_iota((n_sel, n16)) * NSA_CMP_STRIDE
    cover_t = ((r_id < b_id + NSA_SEL_LEN) & (r_id + NSA_CMP_LEN > b_id)
               & (_lane_iota((n_sel, n16)) < n_cmp))
    cover_t = jnp.where(cover_t, 1.0, 0.0).astype(BF16)
    p_hi, p_lo = _split_bf16(p_sum)
    imp = _dot(cover_t, p_hi) + _dot(cover_t, p_lo)

    qs = [_pad_q(qr_ref[j]) for j in range(HG)]
    n_wc = NSA_WINDOW // t + 1
    cw = jnp.maximum(i - (n_wc - 1), 0)
    kw0 = pl.multiple_of(cw * t, t)
    dist = (i - cw) * t + _lane_iota((n_wc * t, t)) - _row_iota((n_wc * t, t))
    bias_w = jnp.where((dist >= 0) & (dist < NSA_WINDOW), 0.0, MASK_BIAS)
    _, inv_lw, out_w = _softmax_direct(qs, kvw_ref[pl.ds(kw0, n_wc * t), :],
                                       [kvwt_ref[cw + u] for u in range(n_wc)], bias_w)
    for j in range(HG):
        part_ref[:, j * t:(j + 1) * t] = (gtt_ref[3 * j:3 * j + 1, :] * o_c[j]
                                          + gtt_ref[3 * j + 2:3 * j + 3, :] * (out_w[j] * inv_lw[j]))

    blk = _row_iota((n_sel, t))
    cur = lax.shift_right_logical(t0 + _lane_iota((n_sel, t)), NSA_SEL_LEN.bit_length() - 1)
    forced = (blk == 0) | (blk == cur) | (blk == cur - 1)
    imp = jnp.where(forced, NSA_FORCE, imp)
    visible_blk = blk <= cur
    imp = jnp.where(visible_blk, imp, NEG_INF)
    n_larger = jnp.zeros((n_sel, t), F32)
    for m in range(n_sel):
        n_larger = n_larger + jnp.where(imp[m:m + 1, :] > imp, 1.0, 0.0)
    sel_fast = n_larger < n_top
    n_picked = jnp.sum(jnp.where(sel_fast & visible_blk, 1.0, 0.0), axis=0, keepdims=True)
    n_wanted = jnp.minimum(cur[0:1, :] + 1, n_top).astype(F32)
    sel_ref[...] = jnp.where(sel_fast, 0.0, MASK_BIAS)

    @pl.when(jnp.max(jnp.abs(n_picked - n_wanted)) > 0.0)
    def _():
        sel_ref[...] = (_rank_select_t(imp, None, n_top) - 1.0) * (-MASK_BIAS)

    return qs


def _nsa_kernel(qc_ref, qr_ref, kvc_ref, kvct_ref, kvs_ref, kvst_ref, kvw_ref, kvwt_ref, gtt_ref,
                o_ref, sel_ref, part_ref, *flash_refs, n_cmp, n_top):
    t = ATT_T
    HG = NSA_HEADS // NSA_GROUPS
    n_slots = NSA_GPS * HG
    i = pl.program_id(2)

    qs = []
    for g in range(NSA_GPS):
        heads_g = pl.ds(g * HG, HG)
        qs += _nsa_front(qc_ref.at[heads_g], qr_ref.at[heads_g], kvc_ref.at[g], kvct_ref.at[g],
                         kvw_ref.at[g], kvwt_ref.at[g], gtt_ref.at[g], sel_ref.at[g],
                         part_ref.at[:, pl.ds(g * HG * t, HG * t)], i, n_cmp, n_top)

    qs_sel = [_bias_lanes(qs[s], sel_ref[s // HG]) for s in range(n_slots)]
    flash = _Flash(*flash_refs, t)
    flash.reset()

    def sel_operands(cc):
        c0 = 2 * cc
        k0 = pl.multiple_of(c0 * t, 2 * t)
        kvs = [kvs_ref[s // HG, pl.ds(k0, 2 * t), :] for s in range(n_slots)]
        kvts = [[kvst_ref[s // HG, c0], kvst_ref[s // HG, c0 + 1]] for s in range(n_slots)]
        return kvs, kvts

    def past_pair(cc):
        kvs, kvts = sel_operands(cc)
        return kvs, kvts, [None] * n_slots

    flash.run(qs_sel, i // 2, past_pair)
    causal = _causal_bias(t)

    @pl.when(i % 2 == 0)
    def _():
        k0 = pl.multiple_of(i * t, t)
        flash.update(qs_sel, [kvs_ref[s // HG, pl.ds(k0, t), :] for s in range(n_slots)],
                     [[kvst_ref[s // HG, i]] for s in range(n_slots)], [causal] * n_slots)

    @pl.when(i % 2 == 1)
    def _():
        kvs, kvts = sel_operands(i // 2)
        bias = jnp.concatenate([jnp.zeros((t, t), F32), causal], axis=0)
        flash.update(qs_sel, kvs, kvts, [bias] * n_slots)

    heads = []
    for s in range(n_slots):
        g, j = divmod(s, HG)
        heads.append(part_ref[:, s * t:(s + 1) * t] + gtt_ref[g, 3 * j + 1:3 * j + 2, :] * flash.result(s))
    _store_heads(o_ref, heads)


def _nsa_attention(qc, qr, kvc, kvct, kvs, kvst, kvw, kvwt, gates_t):
    B, H, S, _ = qc.shape
    G = NSA_GROUPS
    HG = H // G
    t = ATT_T
    nt = S // t
    n16 = kvc.shape[2]
    n_cmp = (S - NSA_CMP_LEN) // NSA_CMP_STRIDE + 1
    n_sel = S // NSA_SEL_LEN
    n_top = min(NSA_SEL_TOPK, n_sel)
    gps = NSA_GPS
    assert G % gps == 0 and nt % 2 == 0 and n_sel <= HEAD_DIM and S >= (NSA_WINDOW // t + 1) * t
    qspec = pl.BlockSpec((None, gps * HG, t, 64), lambda b, g, i: (b, g, i, 0))
    once = pl.Buffered(1)
    kvspec = pl.BlockSpec((None, gps, S, 128), lambda b, g, i: (b, g, 0, 0), pipeline_mode=once)
    kvtspec = pl.BlockSpec((None, gps, nt, 128, t), lambda b, g, i: (b, g, 0, 0, 0), pipeline_mode=once)
    in_specs = [qspec, qspec,
                pl.BlockSpec((None, gps, n16, 128), lambda b, g, i: (b, g, 0, 0)),
                pl.BlockSpec((None, gps, 128, n16), lambda b, g, i: (b, g, 0, 0)),
                kvspec, kvtspec, kvspec, kvtspec,
                pl.BlockSpec((None, gps, 12, t), lambda b, g, i: (b, g, 0, i))]
    return pl.pallas_call(
        functools.partial(_nsa_kernel, n_cmp=n_cmp, n_top=n_top), grid=(B, G // gps, nt), in_specs=in_specs,
        out_specs=pl.BlockSpec((None, t, gps * HG * 64), lambda b, g, i: (b, i, g)),
        out_shape=jax.ShapeDtypeStruct((B, S, H * 64), BF16),
        scratch_shapes=[pltpu.VMEM((gps, n_sel, t), F32), pltpu.VMEM((LANES, gps * HG * t), F32)]
                       + _flash_scratch(gps * HG, t, 2 * t),
        compiler_params=_params(("parallel", "parallel", "parallel")), name="nsa_attention",
    )(qc, qr, kvc, kvct, kvs, kvst, kvw, kvwt, gates_t)


def _rope_freq_row(period, rot):
    half = rot // 2
    inv_freq = ROPE_THETA ** (-(jnp.arange(half, dtype=F32) * 2.0 / rot))
    lane = jnp.arange(LANES) % period
    f = jnp.where(lane < rot, inv_freq[lane % half], 0.0)
    return f.reshape(1, LANES).astype(F32)


def _norm_matrices():
    r = jnp.arange(LANES)
    same = (r[:, None] // 64) == (r[None, :] // 64)
    nq = jnp.where(same, 1.0 / 64, 0.0).astype(BF16)
    nk = jnp.where(same & (r[:, None] < 64), 1.0 / 64, 0.0).astype(BF16)
    return nq, nk


def _q_gain(g):
    return (jnp.tile(g.astype(F32), 2) * Q_SCALE).reshape(1, LANES)


def _k_gain(g):
    return jnp.concatenate([g.astype(F32), jnp.ones((64,), F32)]).reshape(1, LANES)


def _interleave_kv(wk, wv, n_heads):
    d = wk.shape[0]
    wk = wk.reshape(d, n_heads, 64)
    wv = wv.reshape(d, n_heads, 64)
    return jnp.concatenate([wk, wv], axis=2).reshape(d, n_heads * 128)


def _split_cols(w, sizes):
    out, start = [], 0
    for n in sizes:
        out.append(w[:, start:start + n])
        start += n
    return out


def _mixer_layer0(x2, trig, B, S, gmix, w_in, w_out, a_q_norm, a_k_norm, b_q_norm, b_k_norm):
    sizes = (512, 64, 64, 256, 32, 8, 512, 512, 512)
    waq, wak, wav, wiq, wik, wiw, wbq, wbk, wbv = _split_cols(w_in, sizes)
    pad = jnp.zeros((D_MODEL, LANES - 40), w_in.dtype)
    w = jnp.concatenate([waq, wbq, _interleave_kv(wbk, wbv, 8), wak, wav, wiq, wik, wiw, pad],
                        axis=1).astype(BF16)
    nq, nk = _norm_matrices()
    tabs = (nq, nk, _q_gain(a_q_norm), _q_gain(b_q_norm), _k_gain(a_k_norm), _k_gain(b_k_norm))
    aq, bq, bkv, bkvt, akv, akvt, iq, ik, iwt, km = _ab_prep(x2, trig, gmix, w, tabs, B, S)
    n_blk = S // MOBA_BLOCK
    kmean = km.reshape(B, n_blk, 8, 128).transpose(0, 2, 1, 3)
    o_a = _dsa_attention(iq, iwt, ik, aq, akv, akvt).reshape(B * S, 512)
    o_b = _moba_attention(bq, bkv, bkvt, kmean).reshape(B * S, 512)
    w_out = w_out.astype(BF16)
    return [o_a, o_b], [w_out[:512], w_out[512:]]


def _mixer_layer1(x2, trig, B, S, gmix, w_in, w_out, q_norm, kcmp_norm, ksel_norm, kwin_norm,
                  pos_k, pos_v, w1_k, w2_k, w1_v, w2_v):
    G = NSA_GROUPS
    sizes = (1024,) + (256,) * 6 + (48,)
    wq, wkc, wvc, wks, wvs, wkw, wvw, wgt = _split_cols(w_in, sizes)
    pad = jnp.zeros((D_MODEL, LANES - 48), w_in.dtype)
    w = jnp.concatenate([wq, _interleave_kv(wks, wvs, G), _interleave_kv(wkw, wvw, G),
                         wkc, wvc, wgt, pad], axis=1).astype(BF16)
    nq, nk = _norm_matrices()
    tabs = (nq, nk, _q_gain(q_norm), _k_gain(ksel_norm), _k_gain(kwin_norm))
    qc, qr, kvs, kvst, kvw, kvwt, kc_raw, vc_raw, gates_t = _nsa_prep(x2, trig, gmix, w, tabs, B, S)

    n16 = S // NSA_CMP_STRIDE

    def blocks16(t):
        return (t.reshape(B, n16, NSA_CMP_STRIDE, G, HEAD_DIM).transpose(0, 3, 1, 2, 4)
                .reshape(B, G, n16, NSA_CMP_STRIDE * HEAD_DIM))

    def pos_rows(p):
        return p.astype(F32).reshape(2, NSA_CMP_STRIDE * HEAD_DIM)

    kvc, kvct = _compress(blocks16(kc_raw), blocks16(vc_raw), pos_rows(pos_k), pos_rows(pos_v),
                          w1_k.astype(BF16), w1_v.astype(BF16), w2_k.astype(BF16), w2_v.astype(BF16),
                          kcmp_norm.astype(F32).reshape(1, HEAD_DIM))
    o = _nsa_attention(qc, qr, kvc, kvct, kvs, kvst, kvw, kvwt, gates_t)
    return [o.reshape(B * S, NSA_HEADS * HEAD_DIM)], [w_out.astype(BF16)]


def _finish_layer(parts, weights, x2, mem, S, g_mem, g_src, w_q, w_kv, w_o, q_norm, k_norm,
                  g_ffn, ffn_w_in, ffn_w_out):
    row = lambda v: v.astype(F32).reshape(1, -1)
    kv = _mem_kv(mem, row(g_src), w_kv.astype(BF16), row(k_norm))
    wg = ffn_w_in[:, :D_FF].astype(BF16)
    wu = ffn_w_in[:, D_FF:].astype(BF16)
    return _post_mixer(parts, weights, x2, row(g_mem), w_q.astype(BF16), row(q_norm), kv, w_o.astype(BF16),
                       row(g_ffn), wg, wu, ffn_w_out.astype(BF16), S)


def kernel(x, mem, positions, norm_mix, norm_mem, norm_mem_src, norm_ffn, ab_w_in, ab_w_out, dsa_q_norm, dsa_k_norm, moba_q_norm, moba_k_norm, nsa_w_in, nsa_w_out, nsa_q_norm, nsa_kcmp_norm, nsa_ksel_norm, nsa_kwin_norm, nsa_cmp_pos_k, nsa_cmp_pos_v, nsa_cmp_w1_k, nsa_cmp_w2_k, nsa_cmp_w1_v, nsa_cmp_w2_v, mem_w_q, mem_w_kv, mem_w_o, mem_q_norm, mem_k_norm, ffn_w_in, ffn_w_out):
    B, S, D = x.shape
    depth = norm_mix.shape[0]
    x2 = x.reshape(B * S, D)
    trig = _rope_trig(positions.astype(F32).reshape(B * S, 1), _rope_freq_row(64, 16), _rope_freq_row(32, 8))
    row = lambda v: v.astype(F32).reshape(1, -1)
    for i in range(depth):
        j = i // 2
        if i % 2 == 0:
            parts, weights = _mixer_layer0(x2, trig, B, S, row(norm_mix[i]), ab_w_in[j], ab_w_out[j],
                               dsa_q_norm[j], dsa_k_norm[j], moba_q_norm[j], moba_k_norm[j])
        else:
            parts, weights = _mixer_layer1(x2, trig, B, S, row(norm_mix[i]), nsa_w_in[j], nsa_w_out[j],
                               nsa_q_norm[j], nsa_kcmp_norm[j], nsa_ksel_norm[j], nsa_kwin_norm[j],
                               nsa_cmp_pos_k[j], nsa_cmp_pos_v[j], nsa_cmp_w1_k[j], nsa_cmp_w2_k[j],
                               nsa_cmp_w1_v[j], nsa_cmp_w2_v[j])
        x2 = _finish_layer(parts, weights, x2, mem, S, norm_mem[i], norm_mem_src[i], mem_w_q[i], mem_w_kv[i],
                           mem_w_o[i], mem_q_norm[i], mem_k_norm[i], norm_ffn[i], ffn_w_in[i], ffn_w_out[i])
    return x2.reshape(B, S, D)
```

```python
import functools
import math

import jax
import jax.numpy as jnp
from jax import lax
from jax.experimental import pallas as pl
from jax.experimental.pallas import tpu as pltpu

F32 = jnp.float32
BF16 = jnp.bfloat16
I32 = jnp.int32
I16 = jnp.int16

D_MODEL = 1024
N_MEM = 256
HEAD_DIM = 64
ROPE_THETA = 500000.0
RMS_EPS = 1e-6
NEG_INF = -1e30
TINY = 1e-20

DSA_HEADS = 8
DSA_IDX_HEADS = 8
DSA_IDX_DIM = 32
DSA_TOPK = 256
MOBA_HEADS = 8
MOBA_BLOCK = 256
MOBA_TOPK = 3
NSA_HEADS = 16
NSA_GROUPS = 4
NSA_CMP_LEN = 32
NSA_CMP_STRIDE = 16
NSA_SEL_LEN = 64
NSA_SEL_TOPK = 16
NSA_WINDOW = 512
NSA_FORCE = 1e4
MEM_HEADS = 4
MEM_HEAD_DIM = 128
D_FF = ((8 * D_MODEL + 3 * 256 - 1) // (3 * 256)) * 256

LANES = 128
SUBLANES = 8
INT_MIN = -(2 ** 31)
VMEM_LIMIT = 60 * 1024 * 1024

PROJ_GROUP = 4
ATT_T = 256
MASK_BIAS = -1e30
M_FLOOR = -1e29
LOG2E = math.log2(math.e)
Q_SCALE = HEAD_DIM ** -0.5 * LOG2E

NT_DIMS = (((1,), (1,)), ((), ()))


def _dot(a, b):
    return jnp.dot(a, b, preferred_element_type=F32)


def _dot_nt(a, b):
    return lax.dot_general(a, b, NT_DIMS, preferred_element_type=F32)


def _split_bf16(a):
    hi = a.astype(BF16)
    return hi, (a - hi.astype(F32)).astype(BF16)


def _split_dot(a, b):
    hi, lo = _split_bf16(a)
    return _dot(hi, b) + _dot(lo, b)


def _rms_rows(x, gain):
    ms = jnp.mean(x * x, axis=-1, keepdims=True)
    return x * lax.rsqrt(ms + RMS_EPS) * gain


def _params(sem):
    return pltpu.CompilerParams(dimension_semantics=sem, vmem_limit_bytes=VMEM_LIMIT)


def _head_norm(y, norm_m, gain):
    ms = _split_dot(y * y, norm_m)
    return y * lax.rsqrt(ms + RMS_EPS) * gain


def _rope(y, c, s, lo_mask, half):
    sw = jnp.where(lo_mask, pltpu.roll(y, LANES - half, 1), pltpu.roll(y, half, 1))
    return y * c + sw * s


def _lane_iota(shape):
    return lax.broadcasted_iota(I32, shape, 1)


def _row_iota(shape):
    return lax.broadcasted_iota(I32, shape, 0)


def _rope_tables(pos, ftab, period, half):
    ang = pos * ftab
    lane = _lane_iota(ang.shape) % period
    c = jnp.cos(ang)
    s = jnp.sin(ang) * jnp.where(lane < half, -1.0, 1.0)
    return c, s


class _ColumnProjector:
    def __init__(self, xn, w_ref):
        self.xn, self.w_ref, self.groups = xn, w_ref, {}

    def __call__(self, j):
        g, u = divmod(j, PROJ_GROUP)
        if g not in self.groups:
            width = PROJ_GROUP * LANES
            lo = g * width
            hi = min(lo + width, self.w_ref.shape[1])
            self.groups[g] = _dot(self.xn, self.w_ref[:, lo:hi])
        return self.groups[g][:, u * LANES:(u + 1) * LANES]


def _kv_column(yc, nk, gain, c64k, s64k, lo64, first64):
    kn = jnp.where(first64, _head_norm(yc, nk, gain), yc)
    return _rope(kn, c64k, s64k, lo64, 8)


def _rope_trig_kernel(pos_ref, f64_ref, f32_ref, o_ref):
    pos = pos_ref[...]
    c64, s64 = _rope_tables(pos, f64_ref[...], 64, 8)
    c32, s32 = _rope_tables(pos, f32_ref[...], 32, 4)
    o_ref[...] = jnp.concatenate([c64, s64, c32, s32], axis=1)


def _rope_trig(pos2, f64, f32t, tm=1024):
    T = pos2.shape[0]
    return pl.pallas_call(
        _rope_trig_kernel, grid=(T // tm,),
        in_specs=[pl.BlockSpec((tm, 1), lambda i: (i, 0)),
                  pl.BlockSpec(f64.shape, lambda i: (0, 0)), pl.BlockSpec(f32t.shape, lambda i: (0, 0))],
        out_specs=pl.BlockSpec((tm, 4 * LANES), lambda i: (i, 0)),
        out_shape=jax.ShapeDtypeStruct((T, 4 * LANES), F32),
        compiler_params=_params(("parallel",)), name="rope_trig",
    )(pos2, f64, f32t)


def _ab_prep_kernel(x_ref, trig_ref, gmix_ref, w_ref, nq_ref, nk_ref,
                    gaq_ref, gbq_ref, gak_ref, gbk_ref,
                    aq_ref, bq_ref, bkv_ref, bkvt_ref, akv_ref, akvt_ref, iq_ref, ik_ref, iwt_ref, km_ref,
                    *, n_tiles):
    xn = _rms_rows(x_ref[...], gmix_ref[...]).astype(BF16)
    c64, s64, c32, s32 = [trig_ref[:, j * LANES:(j + 1) * LANES] for j in range(4)]
    lane = _lane_iota(c64.shape)
    lo64 = (lane % 64) < 8
    lo32 = (lane % 32) < 4
    first64 = lane < 64
    c64k = jnp.where(first64, c64, 1.0)
    s64k = jnp.where(first64, s64, 0.0)
    first32 = lane < 32
    c32k = jnp.where(first32, c32, 1.0)
    s32k = jnp.where(first32, s32, 0.0)
    nq = nq_ref[...]
    nk = nk_ref[...]

    col = _ColumnProjector(xn, w_ref)

    for j in range(4):
        q = _rope(_head_norm(col(j), nq, gaq_ref[...]), c64, s64, lo64, 8)
        aq_ref[2 * j] = q[:, :64].astype(BF16)
        aq_ref[2 * j + 1] = q[:, 64:].astype(BF16)
    for j in range(4):
        q = _rope(_head_norm(col(4 + j), nq, gbq_ref[...]), c64, s64, lo64, 8)
        bq_ref[2 * j] = q[:, :64].astype(BF16)
        bq_ref[2 * j + 1] = q[:, 64:].astype(BF16)
    blk_onehot = jnp.where(lane == HEAD_DIM + pl.program_id(0) % n_tiles, 1.0, 0.0)
    for h in range(8):
        kv = _kv_column(col(8 + h), nk, gbk_ref[...], c64k, s64k, lo64, first64)
        bkv_ref[h] = jnp.where(first64, kv, blk_onehot).astype(BF16)
        bkvt_ref[h] = kv.T.astype(BF16)
        km_ref[h:h + 1, :] = jnp.mean(kv, axis=0, keepdims=True)
    kv = _kv_column(col(16), nk, gak_ref[...], c64k, s64k, lo64, first64)
    akv_ref[...] = kv.astype(BF16)
    akvt_ref[...] = kv.T.astype(BF16)
    for j in range(2):
        q = _rope(col(17 + j), c32, s32, lo32, 4)
        for u in range(4):
            iq_ref[4 * j + u] = q[:, 32 * u:32 * (u + 1)].astype(BF16)
    yc = col(19)
    ik_ref[...] = _rope(yc, c32k, s32k, lo32, 4)[:, :32].astype(BF16)
    iwt_ref[...] = yc.T[32:40, :]


def _ab_prep(x2, trig, gmix, w, tabs, B, S):
    T = x2.shape[0]
    tm = ATT_T
    nt = S // tm
    n_cols = w.shape[1]
    nq, nk, gaq, gbq, gak, gbk = tabs

    def full(a):
        return pl.BlockSpec(a.shape, lambda i: (0,) * a.ndim)

    def hm(width, heads=8):
        return pl.BlockSpec((None, heads, tm, width), lambda i: (i // nt, 0, i % nt, 0))

    def tokm(width):
        return pl.BlockSpec((None, tm, width), lambda i: (i // nt, i % nt, 0))

    out_shape = (
        jax.ShapeDtypeStruct((B, 8, S, 64), BF16),
        jax.ShapeDtypeStruct((B, 8, S, 64), BF16),
        jax.ShapeDtypeStruct((B, 8, S, 128), BF16),
        jax.ShapeDtypeStruct((B, 8, nt, 128, tm), BF16),
        jax.ShapeDtypeStruct((B, S, 128), BF16),
        jax.ShapeDtypeStruct((B, nt, 128, tm), BF16),
        jax.ShapeDtypeStruct((B, 8, S, 32), BF16),
        jax.ShapeDtypeStruct((B, S, 32), BF16),
        jax.ShapeDtypeStruct((B, 8, S), F32),
        jax.ShapeDtypeStruct((T // tm, 8, 128), F32),
    )
    out_specs = (hm(64), hm(64), hm(128),
                 pl.BlockSpec((None, 8, None, 128, tm), lambda i: (i // nt, 0, i % nt, 0, 0)),
                 tokm(128),
                 pl.BlockSpec((None, None, 128, tm), lambda i: (i // nt, i % nt, 0, 0)),
                 hm(32), tokm(32),
                 pl.BlockSpec((None, 8, tm), lambda i: (i // nt, 0, i % nt)),
                 pl.BlockSpec((None, 8, 128), lambda i: (i, 0, 0)))
    in_specs = [pl.BlockSpec((tm, D_MODEL), lambda i: (i, 0)),
                pl.BlockSpec((tm, 4 * LANES), lambda i: (i, 0)),
                full(gmix), pl.BlockSpec((D_MODEL, n_cols), lambda i: (0, 0)),
                full(nq), full(nk), full(gaq), full(gbq), full(gak), full(gbk)]
    return pl.pallas_call(
        functools.partial(_ab_prep_kernel, n_tiles=nt), grid=(T // tm,), in_specs=in_specs, out_specs=out_specs,
        out_shape=out_shape, compiler_params=_params(("parallel",)), name="ab_prep",
    )(x2, trig, gmix, w, nq, nk, gaq, gbq, gak, gbk)


def _pad_q(q):
    return jnp.concatenate([q, jnp.zeros_like(q)], axis=1)


def _bias_lanes(q, rows):
    n, tq = rows.shape
    parts = [jnp.zeros((HEAD_DIM, tq), F32), rows]
    if n < HEAD_DIM:
        parts.append(jnp.zeros((HEAD_DIM - n, tq), F32))
    lanes = jnp.concatenate(parts, axis=0).T.astype(BF16)
    return jnp.where(_lane_iota(q.shape) < HEAD_DIM, q, lanes)


class _Flash:
    def __init__(self, m_ref, l_ref, acc_ref, s_ref, cmax_ref, p_ref, tq):
        self.m_ref, self.l_ref, self.acc_ref, self.tq = m_ref, l_ref, acc_ref, tq
        self.s_ref, self.cmax_ref, self.p_ref = s_ref, cmax_ref, p_ref

    def reset(self):
        self.m_ref[...] = jnp.full(self.m_ref.shape, M_FLOOR, F32)
        self.l_ref[...] = jnp.zeros(self.l_ref.shape, F32)
        self.acc_ref[...] = jnp.zeros(self.acc_ref.shape, F32)

    def _scores(self, buf, qs, kvs, biases):
        tq = self.tq
        for i in range(len(qs)):
            s = _dot_nt(kvs[i], qs[i])
            if biases[i] is not None:
                s = s + biases[i]
            self.s_ref[buf, i, :s.shape[0], :] = s
            self.cmax_ref[buf, :, i * tq:(i + 1) * tq] = jnp.max(s, axis=0, keepdims=True)

    def update(self, qs, kvs, kvts, biases):
        self._scores(0, qs, kvs, biases)
        self._finish(0, kvts)

    def run(self, qs, count, operands):
        def scores(c, buf):
            kvs, _, biases = operands(c)
            self._scores(buf, qs, kvs, biases)

        def finish(c, buf):
            self._finish(buf, operands(c)[1])

        last = jnp.maximum(count - 1, 0)
        scores(0, 0)

        def two_chunks(pp, carry):
            c = 2 * pp
            scores(c + 1, 1)
            finish(c, 0)
            scores(jnp.minimum(c + 2, last), 0)
            finish(c + 1, 1)
            return carry

        lax.fori_loop(0, count // 2, two_chunks, 0)

        @pl.when(count % 2 == 1)
        def _():
            finish(count - 1, 0)

    def _finish(self, buf, kvts):
        n = len(kvts)
        tq = self.tq
        kc = sum(kvt.shape[1] for kvt in kvts[0])
        alphas = []
        for i in range(n):
            cols = slice(i * tq, (i + 1) * tq)
            m = self.m_ref[:, cols]
            m_new = jnp.maximum(m, self.cmax_ref[buf, :, cols])
            p = jnp.exp2(self.s_ref[buf, i, :kc, :] - m_new)
            alpha = jnp.exp2(m - m_new)
            self.m_ref[:, cols] = m_new
            self.l_ref[:, cols] = alpha * self.l_ref[:, cols] + p.reshape(-1, SUBLANES, tq).sum(axis=0)
            self.p_ref[i, :kc, :] = p.astype(BF16)
            alphas.append(alpha)
        for i in range(n):
            cols = slice(i * tq, (i + 1) * tq)
            pv, r0 = None, 0
            for kvt in kvts[i]:
                part = _dot(kvt, self.p_ref[i, r0:r0 + kvt.shape[1], :])
                pv = part if pv is None else pv + part
                r0 += kvt.shape[1]
            self.acc_ref[:, cols] = alphas[i] * self.acc_ref[:, cols] + pv

    def result(self, slot):
        cols = slice(slot * self.tq, (slot + 1) * self.tq)
        l = jnp.sum(self.l_ref[:, cols], axis=0, keepdims=True)
        return self.acc_ref[:, cols] / jnp.maximum(l, TINY)


def _flash_scratch(n_slots, tq, kc):
    return [pltpu.VMEM((1, n_slots * tq), F32), pltpu.VMEM((SUBLANES, n_slots * tq), F32),
            pltpu.VMEM((LANES, n_slots * tq), F32),
            pltpu.VMEM((2, n_slots, kc, tq), F32), pltpu.VMEM((2, 1, n_slots * tq), F32),
            pltpu.VMEM((n_slots, kc, tq), BF16)]


def _softmax_direct(qs, kv, kvts, bias):
    scores = [_dot_nt(kv, q) for q in qs]
    probs, inv_ls = [], []
    for s in scores:
        s = s + bias
        m = jnp.maximum(jnp.max(s, axis=0, keepdims=True), M_FLOOR)
        p = jnp.exp2(s - m)
        inv_ls.append(1.0 / jnp.maximum(jnp.sum(p, axis=0, keepdims=True), TINY))
        probs.append(p)
    outs = []
    for p in probs:
        pb = p.astype(BF16)
        o, r0 = None, 0
        for kvt in kvts:
            part = _dot(kvt, pb[r0:r0 + kvt.shape[1]])
            o = part if o is None else o + part
            r0 += kvt.shape[1]
        outs.append(o)
    return probs, inv_ls, outs


def _causal_bias(t):
    return jnp.where(_row_iota((t, t)) <= _lane_iota((t, t)), 0.0, MASK_BIAS)


def _store_heads(o_ref, heads_t):
    tq = heads_t[0].shape[1]
    lane = _lane_iota((tq, LANES))
    for u in range(len(heads_t) // 2):
        even = pltpu.roll(heads_t[2 * u].T, 64, 1)
        odd = heads_t[2 * u + 1].T
        o_ref[:, u * LANES:(u + 1) * LANES] = jnp.where(lane < 64, even, odd).astype(o_ref.dtype)


def _rank_select_t(v, n_valid, n_top):
    n = v.shape[0]
    row = _row_iota(v.shape)
    rank = jnp.zeros(v.shape, F32)
    for m in range(n):
        vm = v[m:m + 1, :]
        ahead = (vm > v) | ((vm == v) & (m < row))
        if n_valid is not None:
            ahead = ahead & (m < n_valid)
        rank = rank + jnp.where(ahead, 1.0, 0.0)
    sel = rank < n_top
    if n_valid is not None:
        sel = sel & (row < n_valid)
    return jnp.where(sel, 1.0, 0.0)


def _dsa_kernel(iq_ref, iwt_ref, ik_ref, aq_ref, akv_ref, akvt_ref, o_ref,
                sk_ref, half_ref, bias_ref, xcut_ref, *flash_refs, k_top, index_bits):
    t = ATT_T
    i = pl.program_id(1)
    n_ch = i + 1
    kio = _row_iota((t, t))
    qio = _lane_iota((t, t))

    def causal(c):
        return (c - i) * t + kio <= qio

    def score_chunk(c):
        k0 = pl.multiple_of(c * t, t)
        ikc = ik_ref[pl.ds(k0, t), :]
        sc = jnp.zeros((t, t), F32)
        for h in range(DSA_IDX_HEADS):
            logit = _dot_nt(ikc, iq_ref[h])
            sc = sc + iwt_ref[h:h + 1, :] * jnp.maximum(logit, 0.0)
        sc = jnp.where(sc == 0.0, 0.0, sc)
        bits = pltpu.bitcast(sc, I32)
        key = bits ^ ((bits >> 31) & 0x7FFFFFFF)
        key = jnp.where(causal(c), key, INT_MIN)
        sk_ref[c] = key
        half_ref[c] = (key >> 16).astype(I16)

    def score_pair(cc, carry):
        score_chunk(2 * cc)
        score_chunk(2 * cc + 1)
        return carry

    lax.fori_loop(0, (n_ch + 1) // 2, score_pair, 0)

    def count(pred):
        def body(c, acc8):
            ind = jnp.where(pred(sk_ref[c], c), 1.0, 0.0)
            return acc8 + ind.reshape(-1, SUBLANES, t).sum(axis=0)
        acc8 = lax.fori_loop(0, n_ch, body, jnp.zeros((SUBLANES, t), F32))
        return jnp.sum(acc8, axis=0, keepdims=True)

    def count_half(cand):
        rows = 2 * SUBLANES

        def body(cc, acc):
            parts = []
            for c in (2 * cc, 2 * cc + 1):
                ind = jnp.where(half_ref[c] >= cand, jnp.bfloat16(1), jnp.bfloat16(0))
                parts += [ind[rows * j:rows * (j + 1), :] for j in range(t // rows)]
            while len(parts) > 1:
                parts = [parts[2 * j] + parts[2 * j + 1] for j in range(len(parts) // 2)]
            return acc + parts[0].astype(F32)
        acc = lax.fori_loop(0, (n_ch + 1) // 2, body, jnp.zeros((rows, t), F32))
        return jnp.sum(acc, axis=0, keepdims=True)

    def half_search():
        def bit_step(b, v):
            cand = v + lax.shift_left(jnp.int32(1), 15 - b)
            return jnp.where(count_half(cand.astype(I16)) >= k_top, cand, v)
        return lax.fori_loop(0, 16, bit_step, jnp.full((1, t), -(2 ** 15), I32))

    thr_hi = half_search()

    def low_half_chunk(c, carry):
        key = sk_ref[c]
        hi = key >> 16
        lo = (key & 0xFFFF) - 2 ** 15
        half_ref[c] = jnp.where(hi > thr_hi, 2 ** 15 - 1, jnp.where(hi < thr_hi, -(2 ** 15), lo)).astype(I16)
        return carry

    lax.fori_loop(0, n_ch, low_half_chunk, 0)
    thr = lax.shift_left(thr_hi, 16) + (half_search() + 2 ** 15)

    need = k_top - count(lambda blk, c: blk > thr)
    n_ge = count(lambda blk, c: blk >= thr)
    xcut_ref[...] = jnp.full((1, t), 2 ** 30, I32)

    @pl.when(jnp.max(n_ge) > k_top)
    def _():
        def x_step(b, x):
            cand = x + lax.shift_left(jnp.int32(1), index_bits - 1 - b)
            ties_below = count(lambda blk, c: (blk == thr) & (c * t + kio < cand))
            return jnp.where(ties_below <= need, cand, x)
        xcut_ref[...] = lax.fori_loop(0, index_bits, x_step, jnp.zeros((1, t), I32))

    xcut = xcut_ref[...]

    n_pairs = (n_ch + 1) // 2

    def bias_chunk(c, carry):
        blk = sk_ref[jnp.minimum(c, i)]
        keep = (blk > thr) | ((blk == thr) & (c * t + kio < xcut))
        bias_ref[c] = jnp.where(keep & causal(c), 0.0, MASK_BIAS)
        return carry

    lax.fori_loop(0, 2 * n_pairs, bias_chunk, 0)

    flash = _Flash(*flash_refs, t)
    flash.reset()
    qs = [_pad_q(aq_ref[h]) for h in range(DSA_HEADS)]

    n = DSA_HEADS

    def att_pair(cc):
        c0 = 2 * cc
        k0 = pl.multiple_of(c0 * t, 2 * t)
        kv = akv_ref[pl.ds(k0, 2 * t), :]
        bias = jnp.concatenate([bias_ref[c0], bias_ref[c0 + 1]], axis=0)
        return [kv] * n, [[akvt_ref[c0], akvt_ref[c0 + 1]]] * n, [bias] * n

    flash.run(qs, n_pairs, att_pair)
    _store_heads(o_ref, [flash.result(h) for h in range(DSA_HEADS)])


def _dsa_attention(iq, iwt, ik, aq, akv, akvt):
    B, _, S, _ = aq.shape
    t = ATT_T
    nt = S // t
    k_top = min(DSA_TOPK, S // 4)
    in_specs = [
        pl.BlockSpec((None, 8, t, 32), lambda b, i: (b, 0, i, 0)),
        pl.BlockSpec((None, 8, t), lambda b, i: (b, 0, i)),
        pl.BlockSpec((None, S, 32), lambda b, i: (b, 0, 0)),
        pl.BlockSpec((None, 8, t, 64), lambda b, i: (b, 0, i, 0)),
        pl.BlockSpec((None, S, 128), lambda b, i: (b, 0, 0)),
        pl.BlockSpec((None, nt, 128, t), lambda b, i: (b, 0, 0, 0)),
    ]
    return pl.pallas_call(
        functools.partial(_dsa_kernel, k_top=k_top, index_bits=S.bit_length()),
        grid=(B, nt), in_specs=in_specs,
        out_specs=pl.BlockSpec((None, t, 512), lambda b, i: (b, i, 0)),
        out_shape=jax.ShapeDtypeStruct((B, S, 512), BF16),
        scratch_shapes=[pltpu.VMEM((nt, t, t), I32), pltpu.VMEM((nt, t, t), I16), pltpu.VMEM((nt, t, t), F32),
                        pltpu.VMEM((1, t), I32)] + _flash_scratch(DSA_HEADS, t, 2 * t),
        compiler_params=_params(("parallel", "parallel")), name="dsa_attention",
    )(iq, iwt, ik, aq, akv, akvt)


MOBA_HPS = 8


def _moba_kernel(q_ref, kv_ref, kvt_ref, km_ref, o_ref, *flash_refs, n_top):
    t = ATT_T
    own = pl.program_id(2)
    causal = _causal_bias(t)
    flash = _Flash(*flash_refs, t)
    flash.reset()
    qs = []
    for hh in range(MOBA_HPS):
        q = _pad_q(q_ref[hh])
        km_hi, km_lo = _split_bf16(km_ref[hh])
        gate = _dot_nt(km_hi, q) + _dot_nt(km_lo, q)
        keep = _rank_select_t(gate, own, n_top)
        keep = jnp.where(_row_iota(keep.shape) == own, 1.0, keep)
        qs.append(_bias_lanes(q, (keep - 1.0) * (-MASK_BIAS)))

    def operands(cc):
        n0 = 2 * cc
        k0 = pl.multiple_of(n0 * t, 2 * t)
        heads = range(MOBA_HPS)
        return (n0, [kv_ref[hh, pl.ds(k0, 2 * t), :] for hh in heads],
                [[kvt_ref[hh, n0], kvt_ref[hh, n0 + 1]] for hh in heads])

    def past_pair(cc):
        _, kvs, kvts = operands(cc)
        return kvs, kvts, [None] * MOBA_HPS

    flash.run(qs, own // 2, past_pair)
    heads = range(MOBA_HPS)

    @pl.when(own % 2 == 0)
    def _():
        k0 = pl.multiple_of(own * t, t)
        flash.update(qs, [kv_ref[hh, pl.ds(k0, t), :] for hh in heads],
                     [[kvt_ref[hh, own]] for hh in heads], [causal] * MOBA_HPS)

    @pl.when(own % 2 == 1)
    def _():
        _, kvs, kvts = operands(own // 2)
        bias = jnp.concatenate([jnp.zeros((t, t), F32), causal], axis=0)
        flash.update(qs, kvs, kvts, [bias] * MOBA_HPS)

    _store_heads(o_ref, [flash.result(hh) for hh in range(MOBA_HPS)])


def _moba_attention(bq, bkv, bkvt, kmean):
    B, H, S, _ = bq.shape
    t = ATT_T
    hps = MOBA_HPS
    n_blk = S // MOBA_BLOCK
    assert t == MOBA_BLOCK and n_blk % 2 == 0 and H % hps == 0
    n_top = max(1, min(MOBA_TOPK, n_blk - 1))
    in_specs = [
        pl.BlockSpec((None, hps, t, 64), lambda b, h, i: (b, h, i, 0)),
        pl.BlockSpec((None, hps, S, 128), lambda b, h, i: (b, h, 0, 0)),
        pl.BlockSpec((None, hps, n_blk, 128, t), lambda b, h, i: (b, h, 0, 0, 0)),
        pl.BlockSpec((None, hps, n_blk, 128), lambda b, h, i: (b, h, 0, 0)),
    ]
    return pl.pallas_call(
        functools.partial(_moba_kernel, n_top=n_top), grid=(B, H // hps, S // t), in_specs=in_specs,
        out_specs=pl.BlockSpec((None, t, hps * 64), lambda b, h, i: (b, i, h)),
        out_shape=jax.ShapeDtypeStruct((B, S, H * 64), BF16),
        scratch_shapes=_flash_scratch(hps, t, 2 * t),
        compiler_params=_params(("parallel", "parallel", "parallel")), name="moba_attention",
    )(bq, bkv, bkvt, kmean)


def _lane_group_norm(y, gain, width):
    outs = []
    for j in range(y.shape[1] // width):
        yc = y[:, j * width:(j + 1) * width]
        outs.append(_rms_rows(yc, gain))
    return jnp.concatenate(outs, axis=1)


def _mem_kv_kernel(m_ref, g_ref, w_ref, gk_ref, o_ref):
    mn = _rms_rows(m_ref[...], g_ref[...]).astype(BF16)
    y = _dot(mn, w_ref[...])
    hw = MEM_HEADS * MEM_HEAD_DIM
    k = _lane_group_norm(y[:, :hw], gk_ref[...], MEM_HEAD_DIM)
    o_ref[...] = jnp.concatenate([k, y[:, hw:]], axis=1).astype(BF16)


def _mem_kv(mem, g, w, gk):
    B, M, _ = mem.shape
    n = w.shape[1]
    return pl.pallas_call(
        _mem_kv_kernel, grid=(B,),
        in_specs=[pl.BlockSpec((None, M, D_MODEL), lambda b: (b, 0, 0)),
                  pl.BlockSpec(g.shape, lambda b: (0, 0)),
                  pl.BlockSpec(w.shape, lambda b: (0, 0)),
                  pl.BlockSpec(gk.shape, lambda b: (0, 0))],
        out_specs=pl.BlockSpec((None, M, n), lambda b: (b, 0, 0)),
        out_shape=jax.ShapeDtypeStruct((B, M, n), BF16),
        compiler_params=_params(("parallel",)), name="mem_kv",
    )(mem, g, w, gk)


def _mem_attend(x, g_ref, wq_ref, gq_ref, kv_ref, wo_ref):
    xn = _rms_rows(x, g_ref[...]).astype(BF16)
    q = _lane_group_norm(_dot(xn, wq_ref[...]), gq_ref[...], MEM_HEAD_DIM).astype(BF16)
    hw = MEM_HEADS * MEM_HEAD_DIM
    scale = MEM_HEAD_DIM ** -0.5
    outs = []
    for h in range(MEM_HEADS):
        cols = slice(h * MEM_HEAD_DIM, (h + 1) * MEM_HEAD_DIM)
        k = kv_ref[:, cols]
        v = kv_ref[:, hw + h * MEM_HEAD_DIM:hw + (h + 1) * MEM_HEAD_DIM]
        s = _dot_nt(q[:, cols], k) * scale
        p = jnp.exp(s - jnp.max(s, axis=-1, keepdims=True))
        p = p / jnp.sum(p, axis=-1, keepdims=True)
        outs.append(_dot(p.astype(BF16), v))
    o = jnp.concatenate(outs, axis=1).astype(BF16)
    return x + _dot(o, wo_ref[...])


def _post_mixer_kernel(*refs, n_in):
    a_refs = refs[:n_in]
    w_refs = refs[n_in:2 * n_in]
    (x_ref, gm_ref, wq_ref, gq_ref, kv_ref, wo_ref, gf_ref, wg_ref, wu_ref, wd_ref,
     o_ref, xn_ref, acc_ref) = refs[2 * n_in:]
    j = pl.program_id(1)

    @pl.when(j == 0)
    def _():
        x = x_ref[...]
        for a_ref, w_ref in zip(a_refs, w_refs):
            x = x + _dot(a_ref[...], w_ref[...])
        x = _mem_attend(x, gm_ref, wq_ref, gq_ref, kv_ref, wo_ref)
        xn_ref[...] = _rms_rows(x, gf_ref[...]).astype(BF16)
        acc_ref[...] = x

    xn = xn_ref[...]
    gate = _dot(xn, wg_ref[...])
    up = _dot(xn, wu_ref[...])
    act = (gate * jax.nn.sigmoid(gate) * up).astype(BF16)
    acc_ref[...] += _dot(act, wd_ref[...])

    @pl.when(j == pl.num_programs(1) - 1)
    def _():
        o_ref[...] = acc_ref[...]


def _post_mixer(parts, weights, x2, g_mem, wq, gq, kv, wo, g_ffn, wg, wu, wd, S, tm=512, n_split=2):
    T = x2.shape[0]
    nt = S // tm
    tf = D_FF // n_split
    n_in = len(parts)
    M, n = kv.shape[1], kv.shape[2]

    def const(a):
        return pl.BlockSpec(a.shape, lambda i, j: (0,) * a.ndim)

    in_specs = ([pl.BlockSpec((tm, p.shape[1]), lambda i, j: (i, 0)) for p in parts]
                + [const(w) for w in weights]
                + [pl.BlockSpec((tm, D_MODEL), lambda i, j: (i, 0)),
                   const(g_mem), const(wq), const(gq),
                   pl.BlockSpec((None, M, n), lambda i, j: (i // nt, 0, 0)),
                   const(wo), const(g_ffn),
                   pl.BlockSpec((D_MODEL, tf), lambda i, j: (0, j)),
                   pl.BlockSpec((D_MODEL, tf), lambda i, j: (0, j)),
                   pl.BlockSpec((tf, D_MODEL), lambda i, j: (j, 0))])
    return pl.pallas_call(
        functools.partial(_post_mixer_kernel, n_in=n_in), grid=(T // tm, n_split), in_specs=in_specs,
        out_specs=pl.BlockSpec((tm, D_MODEL), lambda i, j: (i, 0)),
        out_shape=jax.ShapeDtypeStruct((T, D_MODEL), F32),
        scratch_shapes=[pltpu.VMEM((tm, D_MODEL), BF16), pltpu.VMEM((tm, D_MODEL), F32)],
        compiler_params=_params(("parallel", "arbitrary")), name="post_mixer",
    )(*parts, *weights, x2, g_mem, wq, gq, kv, wo, g_ffn, wg, wu, wd)


def _nsa_prep_kernel(x_ref, trig_ref, gmix_ref, w_ref, nq_ref, nk_ref,
                     gq_ref, gks_ref, gkw_ref,
                     qc_ref, qr_ref, kvs_ref, kvst_ref, kvw_ref, kvwt_ref, kc_ref, vc_ref, gtt_ref,
                     stage_ref, *, n_tiles):
    xn = _rms_rows(x_ref[...], gmix_ref[...]).astype(BF16)
    c64, s64 = trig_ref[:, :LANES], trig_ref[:, LANES:]
    lane = _lane_iota(c64.shape)
    lo64 = (lane % 64) < 8
    first64 = lane < 64
    c64k = jnp.where(first64, c64, 1.0)
    s64k = jnp.where(first64, s64, 0.0)
    nq = nq_ref[...]
    nk = nk_ref[...]

    col = _ColumnProjector(xn, w_ref)

    for j in range(8):
        qn = _head_norm(col(j), nq, gq_ref[...])
        qr = _rope(qn, c64, s64, lo64, 8)
        qc_ref[2 * j] = qn[:, :64].astype(BF16)
        qc_ref[2 * j + 1] = qn[:, 64:].astype(BF16)
        qr_ref[2 * j] = qr[:, :64].astype(BF16)
        qr_ref[2 * j + 1] = qr[:, 64:].astype(BF16)
    tile = pl.program_id(0) % n_tiles
    sel_blk = tile * (ATT_T // NSA_SEL_LEN) + lax.shift_right_logical(
        _row_iota(c64.shape), NSA_SEL_LEN.bit_length() - 1)
    blk_onehot = jnp.where(lane == HEAD_DIM + sel_blk, 1.0, 0.0)
    for g in range(NSA_GROUPS):
        kv = _kv_column(col(8 + g), nk, gks_ref[...], c64k, s64k, lo64, first64)
        kvs_ref[g] = jnp.where(first64, kv, blk_onehot).astype(BF16)
        kvst_ref[g] = kv.T.astype(BF16)
        kv = _kv_column(col(12 + g), nk, gkw_ref[...], c64k, s64k, lo64, first64)
        kvw_ref[g] = kv.astype(BF16)
        kvwt_ref[g] = kv.T.astype(BF16)
    stride = NSA_CMP_STRIDE
    rows = stage_ref.shape[0] // stride
    for out_ref, first in ((kc_ref, 16), (vc_ref, 18)):
        for c in range(2):
            stage_ref[...] = col(first + c)
            for u in range(0, stride, 2):
                pair = [stage_ref[pl.ds(u + v, rows, stride=stride), :] for v in range(2)]
                for h in range(2):
                    halves = [p[:, h * HEAD_DIM:(h + 1) * HEAD_DIM] for p in pair]
                    out_ref[2 * c + h, :, u * HEAD_DIM:(u + 2) * HEAD_DIM] = jnp.concatenate(halves, axis=1)
    gates_t = jax.nn.sigmoid(col(20)).T
    for g in range(NSA_GROUPS):
        gtt_ref[g] = gates_t[12 * g:12 * (g + 1), :]


def _nsa_prep(x2, trig, gmix, w, tabs, B, S):
    T = x2.shape[0]
    tm = ATT_T
    nt = S // tm
    nq, nk, gq, gks, gkw = tabs

    def full(a):
        return pl.BlockSpec(a.shape, lambda i: (0,) * a.ndim)

    def hm(width, heads):
        return pl.BlockSpec((None, heads, tm, width), lambda i: (i // nt, 0, i % nt, 0))

    def hmt(heads):
        return pl.BlockSpec((None, heads, None, 128, tm), lambda i: (i // nt, 0, i % nt, 0, 0))

    def tokm(width):
        return pl.BlockSpec((None, tm, width), lambda i: (i // nt, i % nt, 0))

    out_shape = (
        jax.ShapeDtypeStruct((B, 16, S, 64), BF16),
        jax.ShapeDtypeStruct((B, 16, S, 64), BF16),
        jax.ShapeDtypeStruct((B, 4, S, 128), BF16),
        jax.ShapeDtypeStruct((B, 4, nt, 128, tm), BF16),
        jax.ShapeDtypeStruct((B, 4, S, 128), BF16),
        jax.ShapeDtypeStruct((B, 4, nt, 128, tm), BF16),
        jax.ShapeDtypeStruct((B, 4, S // 16, 1024), F32),
        jax.ShapeDtypeStruct((B, 4, S // 16, 1024), F32),
        jax.ShapeDtypeStruct((B, 4, 12, S), F32),
    )
    rows16 = pl.BlockSpec((None, 4, tm // 16, 1024), lambda i: (i // nt, 0, i % nt, 0))
    out_specs = (hm(64, 16), hm(64, 16), hm(128, 4), hmt(4), hm(128, 4), hmt(4), rows16, rows16,
                 pl.BlockSpec((None, 4, 12, tm), lambda i: (i // nt, 0, 0, i % nt)))
    in_specs = [pl.BlockSpec((tm, D_MODEL), lambda i: (i, 0)),
                pl.BlockSpec((tm, 2 * LANES), lambda i: (i, 0)),
                full(gmix), full(w), full(nq), full(nk), full(gq), full(gks), full(gkw)]
    return pl.pallas_call(
        functools.partial(_nsa_prep_kernel, n_tiles=nt), grid=(T // tm,), in_specs=in_specs, out_specs=out_specs,
        out_shape=out_shape, scratch_shapes=[pltpu.VMEM((tm, LANES), F32)],
        compiler_params=_params(("parallel",)), name="nsa_prep",
    )(x2, trig, gmix, w, nq, nk, gq, gks, gkw)


def _compress_one(x16, pa, pb, w1a, w1b, w2):
    n16 = x16.shape[0]
    h_a = _dot((x16 + pa).astype(BF16), w1a)
    h_b = _dot((x16 + pb).astype(BF16), w1b)
    pre = h_a + pltpu.roll(h_b, n16 - 1, 0)
    act = pre * jax.nn.sigmoid(pre)
    return _dot(act.astype(BF16), w2)


def _compress_kernel(xk_ref, xv_ref, pk_ref, pv_ref, w1k_ref, w1v_ref, w2k_ref, w2v_ref, gk_ref,
                     o_ref, ot_ref):
    half = w1k_ref.shape[0] // 2
    k = _compress_one(xk_ref[...], pk_ref[0:1, :], pk_ref[1:2, :],
                      w1k_ref[:half, :], w1k_ref[half:, :], w2k_ref[...])
    k = _rms_rows(k, gk_ref[...])
    v = _compress_one(xv_ref[...], pv_ref[0:1, :], pv_ref[1:2, :],
                      w1v_ref[:half, :], w1v_ref[half:, :], w2v_ref[...])
    kv = jnp.concatenate([k, v], axis=1)
    o_ref[...] = kv.astype(BF16)
    ot_ref[...] = kv.T.astype(BF16)


def _compress(xk16, xv16, pk, pv, w1k, w1v, w2k, w2v, gk):
    B, G, n16, width = xk16.shape

    def full(a):
        return pl.BlockSpec(a.shape, lambda b, g: (0,) * a.ndim)

    xspec = pl.BlockSpec((None, None, n16, width), lambda b, g: (b, g, 0, 0))
    return pl.pallas_call(
        _compress_kernel, grid=(B, G),
        in_specs=[xspec, xspec, full(pk), full(pv), full(w1k), full(w1v), full(w2k), full(w2v), full(gk)],
        out_specs=(pl.BlockSpec((None, None, n16, 128), lambda b, g: (b, g, 0, 0)),
                   pl.BlockSpec((None, None, 128, n16), lambda b, g: (b, g, 0, 0))),
        out_shape=(jax.ShapeDtypeStruct((B, G, n16, 128), BF16),
                   jax.ShapeDtypeStruct((B, G, 128, n16), BF16)),
        compiler_params=_params(("parallel", "parallel")), name="nsa_compress",
    )(xk16, xv16, pk, pv, w1k, w1v, w2k, w2v, gk)


NSA_GPS = 4


def _nsa_front(qc_ref, qr_ref, kvc_ref, kvct_ref, kvw_ref, kvwt_ref, gtt_ref, sel_ref, part_ref,
               i, n_cmp, n_top):
    t = ATT_T
    HG = NSA_HEADS // NSA_GROUPS
    n_sel = sel_ref.shape[0]
    n16 = kvc_ref.shape[0]
    t0 = i * t

    kvc = kvc_ref[...]
    kvct = kvct_ref[...]
    n_id = _row_iota((n16, t))
    q_id = t0 + _lane_iota((n16, t))
    visible = (n_id < n_cmp) & (n_id * NSA_CMP_STRIDE + (NSA_CMP_LEN - 1) <= q_id)
    bias_c = jnp.where(visible, 0.0, MASK_BIAS)
    p_sum = jnp.zeros((n16, t), F32)
    o_c = []
    probs, inv_ls, outs = _softmax_direct([_pad_q(qc_ref[j]) for j in range(HG)], kvc, [kvct], bias_c)
    for j in range(HG):
        p_sum = p_sum + probs[j] * inv_ls[j]
        o_c.append(outs[j] * inv_ls[j])

    b_id = _row_iota((n_sel, n16)) * NSA_SEL_LEN
    r_id = _lane_iota((n_sel, n16)) * NSA_CMP_STRIDE
    cover_t = ((r_id < b_id + NSA_SEL_LEN) & (r_id + NSA_CMP_LEN > b_id)
               & (_lane_iota((n_sel, n16)) < n_cmp))
    cover_t = jnp.where(cover_t, 1.0, 0.0).astype(BF16)
    p_hi, p_lo = _split_bf16(p_sum)
    imp = _dot(cover_t, p_hi) + _dot(cover_t, p_lo)

    qs = [_pad_q(qr_ref[j]) for j in range(HG)]
    n_wc = NSA_WINDOW // t + 1
    cw = jnp.maximum(i - (n_wc - 1), 0)
    kw0 = pl.multiple_of(cw * t, t)
    dist = (i - cw) * t + _lane_iota((n_wc * t, t)) - _row_iota((n_wc * t, t))
    bias_w = jnp.where((dist >= 0) & (dist < NSA_WINDOW), 0.0, MASK_BIAS)
    _, inv_lw, out_w = _softmax_direct(qs, kvw_ref[pl.ds(kw0, n_wc * t), :],
                                       [kvwt_ref[cw + u] for u in range(n_wc)], bias_w)
    for j in range(HG):
        part_ref[:, j * t:(j + 1) * t] = (gtt_ref[3 * j:3 * j + 1, :] * o_c[j]
                                          + gtt_ref[3 * j + 2:3 * j + 3, :] * (out_w[j] * inv_lw[j]))

    blk = _row_iota((n_sel, t))
    cur = lax.shift_right_logical(t0 + _lane_iota((n_sel, t)), NSA_SEL_LEN.bit_length() - 1)
    forced = (blk == 0) | (blk == cur) | (blk == cur - 1)
    imp = jnp.where(forced, NSA_FORCE, imp)
    visible_blk = blk <= cur
    imp = jnp.where(visible_blk, imp, NEG_INF)
    n_larger = jnp.zeros((n_sel, t), F32)
    for m in range(n_sel):
        n_larger = n_larger + jnp.where(imp[m:m + 1, :] > imp, 1.0, 0.0)
    sel_fast = n_larger < n_top
    n_picked = jnp.sum(jnp.where(sel_fast & visible_blk, 1.0, 0.0), axis=0, keepdims=True)
    n_wanted = jnp.minimum(cur[0:1, :] + 1, n_top).astype(F32)
    sel_ref[...] = jnp.where(sel_fast, 0.0, MASK_BIAS)

    @pl.when(jnp.max(jnp.abs(n_picked - n_wanted)) > 0.0)
    def _():
        sel_ref[...] = (_rank_select_t(imp, None, n_top) - 1.0) * (-MASK_BIAS)

    return qs


def _nsa_kernel(qc_ref, qr_ref, kvc_ref, kvct_ref, kvs_ref, kvst_ref, kvw_ref, kvwt_ref, gtt_ref,
                o_ref, sel_ref, part_ref, *flash_refs, n_cmp, n_top):
    t = ATT_T
    HG = NSA_HEADS // NSA_GROUPS
    n_slots = NSA_GPS * HG
    i = pl.program_id(2)

    qs = []
    for g in range(NSA_GPS):
        heads_g = pl.ds(g * HG, HG)
        qs += _nsa_front(qc_ref.at[heads_g], qr_ref.at[heads_g], kvc_ref.at[g], kvct_ref.at[g],
                         kvw_ref.at[g], kvwt_ref.at[g], gtt_ref.at[g], sel_ref.at[g],
                         part_ref.at[:, pl.ds(g * HG * t, HG * t)], i, n_cmp, n_top)

    qs_sel = [_bias_lanes(qs[s], sel_ref[s // HG]) for s in range(n_slots)]
    flash = _Flash(*flash_refs, t)
    flash.reset()

    def sel_operands(cc):
        c0 = 2 * cc
        k0 = pl.multiple_of(c0 * t, 2 * t)
        kvs = [kvs_ref[s // HG, pl.ds(k0, 2 * t), :] for s in range(n_slots)]
        kvts = [[kvst_ref[s // HG, c0], kvst_ref[s // HG, c0 + 1]] for s in range(n_slots)]
        return kvs, kvts

    def past_pair(cc):
        kvs, kvts = sel_operands(cc)
        return kvs, kvts, [None] * n_slots

    flash.run(qs_sel, i // 2, past_pair)
    causal = _causal_bias(t)

    @pl.when(i % 2 == 0)
    def _():
        k0 = pl.multiple_of(i * t, t)
        flash.update(qs_sel, [kvs_ref[s // HG, pl.ds(k0, t), :] for s in range(n_slots)],
                     [[kvst_ref[s // HG, i]] for s in range(n_slots)], [causal] * n_slots)

    @pl.when(i % 2 == 1)
    def _():
        kvs, kvts = sel_operands(i // 2)
        bias = jnp.concatenate([jnp.zeros((t, t), F32), causal], axis=0)
        flash.update(qs_sel, kvs, kvts, [bias] * n_slots)

    heads = []
    for s in range(n_slots):
        g, j = divmod(s, HG)
        heads.append(part_ref[:, s * t:(s + 1) * t] + gtt_ref[g, 3 * j + 1:3 * j + 2, :] * flash.result(s))
    _store_heads(o_ref, heads)


def _nsa_attention(qc, qr, kvc, kvct, kvs, kvst, kvw, kvwt, gates_t):
    B, H, S, _ = qc.shape
    G = NSA_GROUPS
    HG = H // G
    t = ATT_T
    nt = S // t
    n16 = kvc.shape[2]
    n_cmp = (S - NSA_CMP_LEN) // NSA_CMP_STRIDE + 1
    n_sel = S // NSA_SEL_LEN
    n_top = min(NSA_SEL_TOPK, n_sel)
    gps = NSA_GPS
    assert G % gps == 0 and nt % 2 == 0 and n_sel <= HEAD_DIM and S >= (NSA_WINDOW // t + 1) * t
    qspec = pl.BlockSpec((None, gps * HG, t, 64), lambda b, g, i: (b, g, i, 0))
    once = pl.Buffered(1)
    kvspec = pl.BlockSpec((None, gps, S, 128), lambda b, g, i: (b, g, 0, 0), pipeline_mode=once)
    kvtspec = pl.BlockSpec((None, gps, nt, 128, t), lambda b, g, i: (b, g, 0, 0, 0), pipeline_mode=once)
    in_specs = [qspec, qspec,
                pl.BlockSpec((None, gps, n16, 128), lambda b, g, i: (b, g, 0, 0)),
                pl.BlockSpec((None, gps, 128, n16), lambda b, g, i: (b, g, 0, 0)),
                kvspec, kvtspec, kvspec, kvtspec,
                pl.BlockSpec((None, gps, 12, t), lambda b, g, i: (b, g, 0, i))]
    return pl.pallas_call(
        functools.partial(_nsa_kernel, n_cmp=n_cmp, n_top=n_top), grid=(B, G // gps, nt), in_specs=in_specs,
        out_specs=pl.BlockSpec((None, t, gps * HG * 64), lambda b, g, i: (b, i, g)),
        out_shape=jax.ShapeDtypeStruct((B, S, H * 64), BF16),
        scratch_shapes=[pltpu.VMEM((gps, n_sel, t), F32), pltpu.VMEM((LANES, gps * HG * t), F32)]
                       + _flash_scratch(gps * HG, t, 2 * t),
        compiler_params=_params(("parallel", "parallel", "parallel")), name="nsa_attention",
    )(qc, qr, kvc, kvct, kvs, kvst, kvw, kvwt, gates_t)


def _rope_freq_row(period, rot):
    half = rot // 2
    inv_freq = ROPE_THETA ** (-(jnp.arange(half, dtype=F32) * 2.0 / rot))
    lane = jnp.arange(LANES) % period
    f = jnp.where(lane < rot, inv_freq[lane % half], 0.0)
    return f.reshape(1, LANES).astype(F32)


def _norm_matrices():
    r = jnp.arange(LANES)
    same = (r[:, None] // 64) == (r[None, :] // 64)
    nq = jnp.where(same, 1.0 / 64, 0.0).astype(BF16)
    nk = jnp.where(same & (r[:, None] < 64), 1.0 / 64, 0.0).astype(BF16)
    return nq, nk


def _q_gain(g):
    return (jnp.tile(g.astype(F32), 2) * Q_SCALE).reshape(1, LANES)


def _k_gain(g):
    return jnp.concatenate([g.astype(F32), jnp.ones((64,), F32)]).reshape(1, LANES)


def _interleave_kv(wk, wv, n_heads):
    d = wk.shape[0]
    wk = wk.reshape(d, n_heads, 64)
    wv = wv.reshape(d, n_heads, 64)
    return jnp.concatenate([wk, wv], axis=2).reshape(d, n_heads * 128)


def _split_cols(w, sizes):
    out, start = [], 0
    for n in sizes:
        out.append(w[:, start:start + n])
        start += n
    return out


def _mixer_layer0(x2, trig, B, S, gmix, w_in, w_out, a_q_norm, a_k_norm, b_q_norm, b_k_norm):
    sizes = (512, 64, 64, 256, 32, 8, 512, 512, 512)
    waq, wak, wav, wiq, wik, wiw, wbq, wbk, wbv = _split_cols(w_in, sizes)
    pad = jnp.zeros((D_MODEL, LANES - 40), w_in.dtype)
    w = jnp.concatenate([waq, wbq, _interleave_kv(wbk, wbv, 8), wak, wav, wiq, wik, wiw, pad],
                        axis=1).astype(BF16)
    nq, nk = _norm_matrices()
    tabs = (nq, nk, _q_gain(a_q_norm), _q_gain(b_q_norm), _k_gain(a_k_norm), _k_gain(b_k_norm))
    aq, bq, bkv, bkvt, akv, akvt, iq, ik, iwt, km = _ab_prep(x2, trig, gmix, w, tabs, B, S)
    n_blk = S // MOBA_BLOCK
    kmean = km.reshape(B, n_blk, 8, 128).transpose(0, 2, 1, 3)
    o_a = _dsa_attention(iq, iwt, ik, aq, akv, akvt).reshape(B * S, 512)
    o_b = _moba_attention(bq, bkv, bkvt, kmean).reshape(B * S, 512)
    w_out = w_out.astype(BF16)
    return [o_a, o_b], [w_out[:512], w_out[512:]]


def _mixer_layer1(x2, trig, B, S, gmix, w_in, w_out, q_norm, kcmp_norm, ksel_norm, kwin_norm,
                  pos_k, pos_v, w1_k, w2_k, w1_v, w2_v):
    G = NSA_GROUPS
    sizes = (1024,) + (256,) * 6 + (48,)
    wq, wkc, wvc, wks, wvs, wkw, wvw, wgt = _split_cols(w_in, sizes)
    pad = jnp.zeros((D_MODEL, LANES - 48), w_in.dtype)
    w = jnp.concatenate([wq, _interleave_kv(wks, wvs, G), _interleave_kv(wkw, wvw, G),
                         wkc, wvc, wgt, pad], axis=1).astype(BF16)
    nq, nk = _norm_matrices()
    tabs = (nq, nk, _q_gain(q_norm), _k_gain(ksel_norm), _k_gain(kwin_norm))
    qc, qr, kvs, kvst, kvw, kvwt, kc16, vc16, gates_t = _nsa_prep(x2, trig, gmix, w, tabs, B, S)

    def pos_rows(p):
        return p.astype(F32).reshape(2, NSA_CMP_STRIDE * HEAD_DIM)

    kvc, kvct = _compress(kc16, vc16, pos_rows(pos_k), pos_rows(pos_v),
                          w1_k.astype(BF16), w1_v.astype(BF16), w2_k.astype(BF16), w2_v.astype(BF16),
                          kcmp_norm.astype(F32).reshape(1, HEAD_DIM))
    o = _nsa_attention(qc, qr, kvc, kvct, kvs, kvst, kvw, kvwt, gates_t)
    return [o.reshape(B * S, NSA_HEADS * HEAD_DIM)], [w_out.astype(BF16)]


def _finish_layer(parts, weights, x2, mem, S, g_mem, g_src, w_q, w_kv, w_o, q_norm, k_norm,
                  g_ffn, ffn_w_in, ffn_w_out):
    row = lambda v: v.astype(F32).reshape(1, -1)
    kv = _mem_kv(mem, row(g_src), w_kv.astype(BF16), row(k_norm))
    wg = ffn_w_in[:, :D_FF].astype(BF16)
    wu = ffn_w_in[:, D_FF:].astype(BF16)
    return _post_mixer(parts, weights, x2, row(g_mem), w_q.astype(BF16), row(q_norm), kv, w_o.astype(BF16),
                       row(g_ffn), wg, wu, ffn_w_out.astype(BF16), S)


def kernel(x, mem, positions, norm_mix, norm_mem, norm_mem_src, norm_ffn, ab_w_in, ab_w_out, dsa_q_norm, dsa_k_norm, moba_q_norm, moba_k_norm, nsa_w_in, nsa_w_out, nsa_q_norm, nsa_kcmp_norm, nsa_ksel_norm, nsa_kwin_norm, nsa_cmp_pos_k, nsa_cmp_pos_v, nsa_cmp_w1_k, nsa_cmp_w2_k, nsa_cmp_w1_v, nsa_cmp_w2_v, mem_w_q, mem_w_kv, mem_w_o, mem_q_norm, mem_k_norm, ffn_w_in, ffn_w_out):
    B, S, D = x.shape
    depth = norm_mix.shape[0]
    x2 = x.reshape(B * S, D)
    trig = _rope_trig(positions.astype(F32).reshape(B * S, 1), _rope_freq_row(64, 16), _rope_freq_row(32, 8))
    row = lambda v: v.astype(F32).reshape(1, -1)
    for i in range(depth):
        j = i // 2
        if i % 2 == 0:
            parts, weights = _mixer_layer0(x2, trig, B, S, row(norm_mix[i]), ab_w_in[j], ab_w_out[j],
                               dsa_q_norm[j], dsa_k_norm[j], moba_q_norm[j], moba_k_norm[j])
        else:
            parts, weights = _mixer_layer1(x2, trig, B, S, row(norm_mix[i]), nsa_w_in[j], nsa_w_out[j],
                               nsa_q_norm[j], nsa_kcmp_norm[j], nsa_ksel_norm[j], nsa_kwin_norm[j],
                               nsa_cmp_pos_k[j], nsa_cmp_pos_v[j], nsa_cmp_w1_k[j], nsa_cmp_w2_k[j],
                               nsa_cmp_w1_v[j], nsa_cmp_w2_v[j])
        x2 = _finish_layer(parts, weights, x2, mem, S, norm_mem[i], norm_mem_src[i], mem_w_q[i], mem_w_kv[i],
                           mem_w_o[i], mem_q_norm[i], mem_k_norm[i], norm_ffn[i], ffn_w_in[i], ffn_w_out[i])
    return x2.reshape(B, S, D)
```

```python
import functools
import math

import jax
import jax.numpy as jnp
from jax import lax
from jax.experimental import pallas as pl
from jax.experimental.pallas import tpu as pltpu

F32 = jnp.float32
BF16 = jnp.bfloat16
I32 = jnp.int32
I16 = jnp.int16

D_MODEL = 1024
N_MEM = 256
HEAD_DIM = 64
ROPE_THETA = 500000.0
RMS_EPS = 1e-6
NEG_INF = -1e30
TINY = 1e-20

DSA_HEADS = 8
DSA_IDX_HEADS = 8
DSA_IDX_DIM = 32
DSA_TOPK = 256
MOBA_HEADS = 8
MOBA_BLOCK = 256
MOBA_TOPK = 3
NSA_HEADS = 16
NSA_GROUPS = 4
NSA_CMP_LEN = 32
NSA_CMP_STRIDE = 16
NSA_SEL_LEN = 64
NSA_SEL_TOPK = 16
NSA_WINDOW = 512
NSA_FORCE = 1e4
MEM_HEADS = 4
MEM_HEAD_DIM = 128
D_FF = ((8 * D_MODEL + 3 * 256 - 1) // (3 * 256)) * 256

LANES = 128
SUBLANES = 8
INT_MIN = -(2 ** 31)
VMEM_LIMIT = 60 * 1024 * 1024

PROJ_GROUP = 4
ATT_T = 256
MASK_BIAS = -1e30
M_FLOOR = -1e29
LOG2E = math.log2(math.e)
Q_SCALE = HEAD_DIM ** -0.5 * LOG2E

NT_DIMS = (((1,), (1,)), ((), ()))


def _dot(a, b):
    return jnp.dot(a, b, preferred_element_type=F32)


def _dot_nt(a, b):
    return lax.dot_general(a, b, NT_DIMS, preferred_element_type=F32)


def _split_bf16(a):
    hi = a.astype(BF16)
    return hi, (a - hi.astype(F32)).astype(BF16)


def _split_dot(a, b):
    hi, lo = _split_bf16(a)
    return _dot(hi, b) + _dot(lo, b)


def _rms_rows(x, gain):
    ms = jnp.mean(x * x, axis=-1, keepdims=True)
    return x * lax.rsqrt(ms + RMS_EPS) * gain


def _params(sem):
    return pltpu.CompilerParams(dimension_semantics=sem, vmem_limit_bytes=VMEM_LIMIT)


def _head_norm(y, norm_m, gain):
    ms = _split_dot(y * y, norm_m)
    return y * lax.rsqrt(ms + RMS_EPS) * gain


def _rope(y, c, s, lo_mask, half):
    sw = jnp.where(lo_mask, pltpu.roll(y, LANES - half, 1), pltpu.roll(y, half, 1))
    return y * c + sw * s


def _lane_iota(shape):
    return lax.broadcasted_iota(I32, shape, 1)


def _row_iota(shape):
    return lax.broadcasted_iota(I32, shape, 0)


def _rope_tables(pos, ftab, period, half):
    ang = pos * ftab
    lane = _lane_iota(ang.shape) % period
    c = jnp.cos(ang)
    s = jnp.sin(ang) * jnp.where(lane < half, -1.0, 1.0)
    return c, s


class _ColumnProjector:
    def __init__(self, xn, w_ref):
        self.xn, self.w_ref, self.groups = xn, w_ref, {}

    def __call__(self, j):
        g, u = divmod(j, PROJ_GROUP)
        if g not in self.groups:
            width = PROJ_GROUP * LANES
            lo = g * width
            hi = min(lo + width, self.w_ref.shape[1])
            self.groups[g] = _dot(self.xn, self.w_ref[:, lo:hi])
        return self.groups[g][:, u * LANES:(u + 1) * LANES]


def _kv_column(yc, nk, gain, c64k, s64k, lo64, first64):
    kn = jnp.where(first64, _head_norm(yc, nk, gain), yc)
    return _rope(kn, c64k, s64k, lo64, 8)


def _rope_trig_kernel(pos_ref, f64_ref, f32_ref, o_ref):
    pos = pos_ref[...]
    c64, s64 = _rope_tables(pos, f64_ref[...], 64, 8)
    c32, s32 = _rope_tables(pos, f32_ref[...], 32, 4)
    o_ref[...] = jnp.concatenate([c64, s64, c32, s32], axis=1)


def _rope_trig(pos2, f64, f32t, tm=1024):
    T = pos2.shape[0]
    return pl.pallas_call(
        _rope_trig_kernel, grid=(T // tm,),
        in_specs=[pl.BlockSpec((tm, 1), lambda i: (i, 0)),
                  pl.BlockSpec(f64.shape, lambda i: (0, 0)), pl.BlockSpec(f32t.shape, lambda i: (0, 0))],
        out_specs=pl.BlockSpec((tm, 4 * LANES), lambda i: (i, 0)),
        out_shape=jax.ShapeDtypeStruct((T, 4 * LANES), F32),
        compiler_params=_params(("parallel",)), name="rope_trig",
    )(pos2, f64, f32t)


def _ab_prep_kernel(x_ref, trig_ref, gmix_ref, w_ref, nq_ref, nk_ref,
                    gaq_ref, gbq_ref, gak_ref, gbk_ref,
                    aq_ref, bq_ref, bkv_ref, bkvt_ref, akv_ref, akvt_ref, iq_ref, ik_ref, iwt_ref, km_ref,
                    *, n_tiles):
    xn = _rms_rows(x_ref[...], gmix_ref[...]).astype(BF16)
    c64, s64, c32, s32 = [trig_ref[:, j * LANES:(j + 1) * LANES] for j in range(4)]
    lane = _lane_iota(c64.shape)
    lo64 = (lane % 64) < 8
    lo32 = (lane % 32) < 4
    first64 = lane < 64
    c64k = jnp.where(first64, c64, 1.0)
    s64k = jnp.where(first64, s64, 0.0)
    first32 = lane < 32
    c32k = jnp.where(first32, c32, 1.0)
    s32k = jnp.where(first32, s32, 0.0)
    nq = nq_ref[...]
    nk = nk_ref[...]

    col = _ColumnProjector(xn, w_ref)

    for j in range(4):
        q = _rope(_head_norm(col(j), nq, gaq_ref[...]), c64, s64, lo64, 8)
        aq_ref[2 * j] = q[:, :64].astype(BF16)
        aq_ref[2 * j + 1] = q[:, 64:].astype(BF16)
    for j in range(4):
        q = _rope(_head_norm(col(4 + j), nq, gbq_ref[...]), c64, s64, lo64, 8)
        bq_ref[2 * j] = q[:, :64].astype(BF16)
        bq_ref[2 * j + 1] = q[:, 64:].astype(BF16)
    blk_onehot = jnp.where(lane == HEAD_DIM + pl.program_id(0) % n_tiles, 1.0, 0.0)
    for h in range(8):
        kv = _kv_column(col(8 + h), nk, gbk_ref[...], c64k, s64k, lo64, first64)
        bkv_ref[h] = jnp.where(first64, kv, blk_onehot).astype(BF16)
        bkvt_ref[h] = kv.T.astype(BF16)
        km_ref[h:h + 1, :] = jnp.mean(kv, axis=0, keepdims=True)
    kv = _kv_column(col(16), nk, gak_ref[...], c64k, s64k, lo64, first64)
    akv_ref[...] = kv.astype(BF16)
    akvt_ref[...] = kv.T.astype(BF16)
    for j in range(2):
        q = _rope(col(17 + j), c32, s32, lo32, 4)
        for u in range(4):
            iq_ref[4 * j + u] = q[:, 32 * u:32 * (u + 1)].astype(BF16)
    yc = col(19)
    ik_ref[...] = _rope(yc, c32k, s32k, lo32, 4)[:, :32].astype(BF16)
    iwt_ref[...] = yc.T[32:40, :]


def _ab_prep(x2, trig, gmix, w, tabs, B, S):
    T = x2.shape[0]
    tm = ATT_T
    nt = S // tm
    n_cols = w.shape[1]
    nq, nk, gaq, gbq, gak, gbk = tabs

    def full(a):
        return pl.BlockSpec(a.shape, lambda i: (0,) * a.ndim)

    def hm(width, heads=8):
        return pl.BlockSpec((None, heads, tm, width), lambda i: (i // nt, 0, i % nt, 0))

    def tokm(width):
        return pl.BlockSpec((None, tm, width), lambda i: (i // nt, i % nt, 0))

    out_shape = (
        jax.ShapeDtypeStruct((B, 8, S, 64), BF16),
        jax.ShapeDtypeStruct((B, 8, S, 64), BF16),
        jax.ShapeDtypeStruct((B, 8, S, 128), BF16),
        jax.ShapeDtypeStruct((B, 8, nt, 128, tm), BF16),
        jax.ShapeDtypeStruct((B, S, 128), BF16),
        jax.ShapeDtypeStruct((B, nt, 128, tm), BF16),
        jax.ShapeDtypeStruct((B, 8, S, 32), BF16),
        jax.ShapeDtypeStruct((B, S, 32), BF16),
        jax.ShapeDtypeStruct((B, 8, S), F32),
        jax.ShapeDtypeStruct((T // tm, 8, 128), F32),
    )
    out_specs = (hm(64), hm(64), hm(128),
                 pl.BlockSpec((None, 8, None, 128, tm), lambda i: (i // nt, 0, i % nt, 0, 0)),
                 tokm(128),
                 pl.BlockSpec((None, None, 128, tm), lambda i: (i // nt, i % nt, 0, 0)),
                 hm(32), tokm(32),
                 pl.BlockSpec((None, 8, tm), lambda i: (i // nt, 0, i % nt)),
                 pl.BlockSpec((None, 8, 128), lambda i: (i, 0, 0)))
    in_specs = [pl.BlockSpec((tm, D_MODEL), lambda i: (i, 0)),
                pl.BlockSpec((tm, 4 * LANES), lambda i: (i, 0)),
                full(gmix), pl.BlockSpec((D_MODEL, n_cols), lambda i: (0, 0)),
                full(nq), full(nk), full(gaq), full(gbq), full(gak), full(gbk)]
    return pl.pallas_call(
        functools.partial(_ab_prep_kernel, n_tiles=nt), grid=(T // tm,), in_specs=in_specs, out_specs=out_specs,
        out_shape=out_shape, compiler_params=_params(("parallel",)), name="ab_prep",
    )(x2, trig, gmix, w, nq, nk, gaq, gbq, gak, gbk)


def _pad_q(q):
    return jnp.concatenate([q, jnp.zeros_like(q)], axis=1)


def _bias_lanes(q, rows):
    n, tq = rows.shape
    parts = [jnp.zeros((HEAD_DIM, tq), F32), rows]
    if n < HEAD_DIM:
        parts.append(jnp.zeros((HEAD_DIM - n, tq), F32))
    lanes = jnp.concatenate(parts, axis=0).T.astype(BF16)
    return jnp.where(_lane_iota(q.shape) < HEAD_DIM, q, lanes)


class _Flash:
    def __init__(self, m_ref, l_ref, acc_ref, s_ref, cmax_ref, p_ref, tq):
        self.m_ref, self.l_ref, self.acc_ref, self.tq = m_ref, l_ref, acc_ref, tq
        self.s_ref, self.cmax_ref, self.p_ref = s_ref, cmax_ref, p_ref

    def reset(self):
        self.m_ref[...] = jnp.full(self.m_ref.shape, M_FLOOR, F32)
        self.l_ref[...] = jnp.zeros(self.l_ref.shape, F32)
        self.acc_ref[...] = jnp.zeros(self.acc_ref.shape, F32)

    def _scores(self, buf, qs, kvs, biases):
        tq = self.tq
        for i in range(len(qs)):
            s = _dot_nt(kvs[i], qs[i])
            if biases[i] is not None:
                s = s + biases[i]
            self.s_ref[buf, i, :s.shape[0], :] = s
            self.cmax_ref[buf, :, i * tq:(i + 1) * tq] = jnp.max(s, axis=0, keepdims=True)

    def update(self, qs, kvs, kvts, biases):
        self._scores(0, qs, kvs, biases)
        self._finish(0, kvts)

    def run(self, qs, count, operands):
        def scores(c, buf):
            kvs, _, biases = operands(c)
            self._scores(buf, qs, kvs, biases)

        def finish(c, buf):
            self._finish(buf, operands(c)[1])

        last = jnp.maximum(count - 1, 0)
        scores(0, 0)

        def two_chunks(pp, carry):
            c = 2 * pp
            scores(c + 1, 1)
            finish(c, 0)
            scores(jnp.minimum(c + 2, last), 0)
            finish(c + 1, 1)
            return carry

        lax.fori_loop(0, count // 2, two_chunks, 0)

        @pl.when(count % 2 == 1)
        def _():
            finish(count - 1, 0)

    def _finish(self, buf, kvts):
        n = len(kvts)
        tq = self.tq
        kc = sum(kvt.shape[1] for kvt in kvts[0])
        alphas = []
        for i in range(n):
            cols = slice(i * tq, (i + 1) * tq)
            m = self.m_ref[:, cols]
            m_new = jnp.maximum(m, self.cmax_ref[buf, :, cols])
            p = jnp.exp2(self.s_ref[buf, i, :kc, :] - m_new)
            alpha = jnp.exp2(m - m_new)
            self.m_ref[:, cols] = m_new
            self.l_ref[:, cols] = alpha * self.l_ref[:, cols] + p.reshape(-1, SUBLANES, tq).sum(axis=0)
            self.p_ref[i, :kc, :] = p.astype(BF16)
            alphas.append(alpha)
        for i in range(n):
            cols = slice(i * tq, (i + 1) * tq)
            pv, r0 = None, 0
            for kvt in kvts[i]:
                part = _dot(kvt, self.p_ref[i, r0:r0 + kvt.shape[1], :])
                pv = part if pv is None else pv + part
                r0 += kvt.shape[1]
            self.acc_ref[:, cols] = alphas[i] * self.acc_ref[:, cols] + pv

    def result(self, slot):
        cols = slice(slot * self.tq, (slot + 1) * self.tq)
        l = jnp.sum(self.l_ref[:, cols], axis=0, keepdims=True)
        return self.acc_ref[:, cols] / jnp.maximum(l, TINY)


def _flash_scratch(n_slots, tq, kc):
    return [pltpu.VMEM((1, n_slots * tq), F32), pltpu.VMEM((SUBLANES, n_slots * tq), F32),
            pltpu.VMEM((LANES, n_slots * tq), F32),
            pltpu.VMEM((2, n_slots, kc, tq), F32), pltpu.VMEM((2, 1, n_slots * tq), F32),
            pltpu.VMEM((n_slots, kc, tq), BF16)]


def _softmax_direct(qs, kv, kvts, bias):
    scores = [_dot_nt(kv, q) for q in qs]
    probs, inv_ls = [], []
    for s in scores:
        s = s + bias
        m = jnp.maximum(jnp.max(s, axis=0, keepdims=True), M_FLOOR)
        p = jnp.exp2(s - m)
        inv_ls.append(1.0 / jnp.maximum(jnp.sum(p, axis=0, keepdims=True), TINY))
        probs.append(p)
    outs = []
    for p in probs:
        pb = p.astype(BF16)
        o, r0 = None, 0
        for kvt in kvts:
            part = _dot(kvt, pb[r0:r0 + kvt.shape[1]])
            o = part if o is None else o + part
            r0 += kvt.shape[1]
        outs.append(o)
    return probs, inv_ls, outs


def _causal_bias(t):
    return jnp.where(_row_iota((t, t)) <= _lane_iota((t, t)), 0.0, MASK_BIAS)


def _store_heads(o_ref, heads_t):
    tq = heads_t[0].shape[1]
    lane = _lane_iota((tq, LANES))
    for u in range(len(heads_t) // 2):
        even = pltpu.roll(heads_t[2 * u].T, 64, 1)
        odd = heads_t[2 * u + 1].T
        o_ref[:, u * LANES:(u + 1) * LANES] = jnp.where(lane < 64, even, odd).astype(o_ref.dtype)


def _rank_select_t(v, n_valid, n_top):
    n = v.shape[0]
    row = _row_iota(v.shape)
    rank = jnp.zeros(v.shape, F32)
    for m in range(n):
        vm = v[m:m + 1, :]
        ahead = (vm > v) | ((vm == v) & (m < row))
        if n_valid is not None:
            ahead = ahead & (m < n_valid)
        rank = rank + jnp.where(ahead, 1.0, 0.0)
    sel = rank < n_top
    if n_valid is not None:
        sel = sel & (row < n_valid)
    return jnp.where(sel, 1.0, 0.0)


def _dsa_kernel(iq_ref, iwt_ref, ik_ref, aq_ref, akv_ref, akvt_ref, o_ref,
                sk_ref, half_ref, bias_ref, xcut_ref, *flash_refs, k_top, index_bits):
    t = ATT_T
    i = pl.program_id(1)
    n_ch = i + 1
    kio = _row_iota((t, t))
    qio = _lane_iota((t, t))

    def causal(c):
        return (c - i) * t + kio <= qio

    def score_chunk(c):
        k0 = pl.multiple_of(c * t, t)
        ikc = ik_ref[pl.ds(k0, t), :]
        sc = jnp.zeros((t, t), F32)
        for h in range(DSA_IDX_HEADS):
            logit = _dot_nt(ikc, iq_ref[h])
            sc = sc + iwt_ref[h:h + 1, :] * jnp.maximum(logit, 0.0)
        sc = jnp.where(sc == 0.0, 0.0, sc)
        bits = pltpu.bitcast(sc, I32)
        key = bits ^ ((bits >> 31) & 0x7FFFFFFF)
        key = jnp.where(causal(c), key, INT_MIN)
        sk_ref[c] = key
        half_ref[c] = (key >> 16).astype(I16)

    def score_pair(cc, carry):
        score_chunk(2 * cc)
        score_chunk(2 * cc + 1)
        return carry

    lax.fori_loop(0, (n_ch + 1) // 2, score_pair, 0)

    def count(pred):
        def body(c, acc8):
            ind = jnp.where(pred(sk_ref[c], c), 1.0, 0.0)
            return acc8 + ind.reshape(-1, SUBLANES, t).sum(axis=0)
        acc8 = lax.fori_loop(0, n_ch, body, jnp.zeros((SUBLANES, t), F32))
        return jnp.sum(acc8, axis=0, keepdims=True)

    def count_half(cand):
        rows = 2 * SUBLANES

        def body(cc, acc):
            parts = []
            for c in (2 * cc, 2 * cc + 1):
                ind = jnp.where(half_ref[c] >= cand, jnp.bfloat16(1), jnp.bfloat16(0))
                parts += [ind[rows * j:rows * (j + 1), :] for j in range(t // rows)]
            while len(parts) > 1:
                parts = [parts[2 * j] + parts[2 * j + 1] for j in range(len(parts) // 2)]
            return acc + parts[0].astype(F32)
        acc = lax.fori_loop(0, (n_ch + 1) // 2, body, jnp.zeros((rows, t), F32))
        return jnp.sum(acc, axis=0, keepdims=True)

    def half_search():
        def bit_step(b, v):
            cand = v + lax.shift_left(jnp.int32(1), 15 - b)
            return jnp.where(count_half(cand.astype(I16)) >= k_top, cand, v)
        return lax.fori_loop(0, 16, bit_step, jnp.full((1, t), -(2 ** 15), I32))

    thr_hi = half_search()

    def low_half_chunk(c, carry):
        key = sk_ref[c]
        hi = key >> 16
        lo = (key & 0xFFFF) - 2 ** 15
        half_ref[c] = jnp.where(hi > thr_hi, 2 ** 15 - 1, jnp.where(hi < thr_hi, -(2 ** 15), lo)).astype(I16)
        return carry

    lax.fori_loop(0, n_ch, low_half_chunk, 0)
    thr = lax.shift_left(thr_hi, 16) + (half_search() + 2 ** 15)

    need = k_top - count(lambda blk, c: blk > thr)
    n_ge = count(lambda blk, c: blk >= thr)
    xcut_ref[...] = jnp.full((1, t), 2 ** 30, I32)

    @pl.when(jnp.max(n_ge) > k_top)
    def _():
        def x_step(b, x):
            cand = x + lax.shift_left(jnp.int32(1), index_bits - 1 - b)
            ties_below = count(lambda blk, c: (blk == thr) & (c * t + kio < cand))
            return jnp.where(ties_below <= need, cand, x)
        xcut_ref[...] = lax.fori_loop(0, index_bits, x_step, jnp.zeros((1, t), I32))

    xcut = xcut_ref[...]

    n_pairs = (n_ch + 1) // 2

    def bias_chunk(c, carry):
        blk = sk_ref[jnp.minimum(c, i)]
        keep = (blk > thr) | ((blk == thr) & (c * t + kio < xcut))
        bias_ref[c] = jnp.where(keep & causal(c), 0.0, MASK_BIAS)
        return carry

    lax.fori_loop(0, 2 * n_pairs, bias_chunk, 0)

    flash = _Flash(*flash_refs, t)
    flash.reset()
    qs = [_pad_q(aq_ref[h]) for h in range(DSA_HEADS)]

    n = DSA_HEADS

    def att_pair(cc):
        c0 = 2 * cc
        k0 = pl.multiple_of(c0 * t, 2 * t)
        kv = akv_ref[pl.ds(k0, 2 * t), :]
        bias = jnp.concatenate([bias_ref[c0], bias_ref[c0 + 1]], axis=0)
        return [kv] * n, [[akvt_ref[c0], akvt_ref[c0 + 1]]] * n, [bias] * n

    flash.run(qs, n_pairs, att_pair)
    _store_heads(o_ref, [flash.result(h) for h in range(DSA_HEADS)])


def _dsa_attention(iq, iwt, ik, aq, akv, akvt):
    B, _, S, _ = aq.shape
    t = ATT_T
    nt = S // t
    k_top = min(DSA_TOPK, S // 4)
    in_specs = [
        pl.BlockSpec((None, 8, t, 32), lambda b, i: (b, 0, i, 0)),
        pl.BlockSpec((None, 8, t), lambda b, i: (b, 0, i)),
        pl.BlockSpec((None, S, 32), lambda b, i: (b, 0, 0)),
        pl.BlockSpec((None, 8, t, 64), lambda b, i: (b, 0, i, 0)),
        pl.BlockSpec((None, S, 128), lambda b, i: (b, 0, 0)),
        pl.BlockSpec((None, nt, 128, t), lambda b, i: (b, 0, 0, 0)),
    ]
    return pl.pallas_call(
        functools.partial(_dsa_kernel, k_top=k_top, index_bits=S.bit_length()),
        grid=(B, nt), in_specs=in_specs,
        out_specs=pl.BlockSpec((None, t, 512), lambda b, i: (b, i, 0)),
        out_shape=jax.ShapeDtypeStruct((B, S, 512), BF16),
        scratch_shapes=[pltpu.VMEM((nt, t, t), I32), pltpu.VMEM((nt, t, t), I16), pltpu.VMEM((nt, t, t), F32),
                        pltpu.VMEM((1, t), I32)] + _flash_scratch(DSA_HEADS, t, 2 * t),
        compiler_params=_params(("parallel", "parallel")), name="dsa_attention",
    )(iq, iwt, ik, aq, akv, akvt)


MOBA_HPS = 8


def _moba_kernel(q_ref, kv_ref, kvt_ref, km_ref, o_ref, *flash_refs, n_top):
    t = ATT_T
    own = pl.program_id(2)
    causal = _causal_bias(t)
    flash = _Flash(*flash_refs, t)
    flash.reset()
    qs = []
    for hh in range(MOBA_HPS):
        q = _pad_q(q_ref[hh])
        km_hi, km_lo = _split_bf16(km_ref[hh])
        gate = _dot_nt(km_hi, q) + _dot_nt(km_lo, q)
        keep = _rank_select_t(gate, own, n_top)
        keep = jnp.where(_row_iota(keep.shape) == own, 1.0, keep)
        qs.append(_bias_lanes(q, (keep - 1.0) * (-MASK_BIAS)))

    def operands(cc):
        n0 = 2 * cc
        k0 = pl.multiple_of(n0 * t, 2 * t)
        heads = range(MOBA_HPS)
        return (n0, [kv_ref[hh, pl.ds(k0, 2 * t), :] for hh in heads],
                [[kvt_ref[hh, n0], kvt_ref[hh, n0 + 1]] for hh in heads])

    def past_pair(cc):
        _, kvs, kvts = operands(cc)
        return kvs, kvts, [None] * MOBA_HPS

    flash.run(qs, own // 2, past_pair)
    heads = range(MOBA_HPS)

    @pl.when(own % 2 == 0)
    def _():
        k0 = pl.multiple_of(own * t, t)
        flash.update(qs, [kv_ref[hh, pl.ds(k0, t), :] for hh in heads],
                     [[kvt_ref[hh, own]] for hh in heads], [causal] * MOBA_HPS)

    @pl.when(own % 2 == 1)
    def _():
        _, kvs, kvts = operands(own // 2)
        bias = jnp.concatenate([jnp.zeros((t, t), F32), causal], axis=0)
        flash.update(qs, kvs, kvts, [bias] * MOBA_HPS)

    _store_heads(o_ref, [flash.result(hh) for hh in range(MOBA_HPS)])


def _moba_attention(bq, bkv, bkvt, kmean):
    B, H, S, _ = bq.shape
    t = ATT_T
    hps = MOBA_HPS
    n_blk = S // MOBA_BLOCK
    assert t == MOBA_BLOCK and n_blk % 2 == 0 and H % hps == 0
    n_top = max(1, min(MOBA_TOPK, n_blk - 1))
    in_specs = [
        pl.BlockSpec((None, hps, t, 64), lambda b, h, i: (b, h, i, 0)),
        pl.BlockSpec((None, hps, S, 128), lambda b, h, i: (b, h, 0, 0)),
        pl.BlockSpec((None, hps, n_blk, 128, t), lambda b, h, i: (b, h, 0, 0, 0)),
        pl.BlockSpec((None, hps, n_blk, 128), lambda b, h, i: (b, h, 0, 0)),
    ]
    return pl.pallas_call(
        functools.partial(_moba_kernel, n_top=n_top), grid=(B, H // hps, S // t), in_specs=in_specs,
        out_specs=pl.BlockSpec((None, t, hps * 64), lambda b, h, i: (b, i, h)),
        out_shape=jax.ShapeDtypeStruct((B, S, H * 64), BF16),
        scratch_shapes=_flash_scratch(hps, t, 2 * t),
        compiler_params=_params(("parallel", "parallel", "parallel")), name="moba_attention",
    )(bq, bkv, bkvt, kmean)


def _lane_group_norm(y, gain, width):
    outs = []
    for j in range(y.shape[1] // width):
        yc = y[:, j * width:(j + 1) * width]
        outs.append(_rms_rows(yc, gain))
    return jnp.concatenate(outs, axis=1)


def _mem_kv_kernel(m_ref, g_ref, w_ref, gk_ref, o_ref):
    mn = _rms_rows(m_ref[...], g_ref[...]).astype(BF16)
    y = _dot(mn, w_ref[...])
    hw = MEM_HEADS * MEM_HEAD_DIM
    k = _lane_group_norm(y[:, :hw], gk_ref[...], MEM_HEAD_DIM)
    o_ref[...] = jnp.concatenate([k, y[:, hw:]], axis=1).astype(BF16)


def _mem_kv(mem, g, w, gk):
    B, M, _ = mem.shape
    n = w.shape[1]
    return pl.pallas_call(
        _mem_kv_kernel, grid=(B,),
        in_specs=[pl.BlockSpec((None, M, D_MODEL), lambda b: (b, 0, 0)),
                  pl.BlockSpec(g.shape, lambda b: (0, 0)),
                  pl.BlockSpec(w.shape, lambda b: (0, 0)),
                  pl.BlockSpec(gk.shape, lambda b: (0, 0))],
        out_specs=pl.BlockSpec((None, M, n), lambda b: (b, 0, 0)),
        out_shape=jax.ShapeDtypeStruct((B, M, n), BF16),
        compiler_params=_params(("parallel",)), name="mem_kv",
    )(mem, g, w, gk)


def _mem_attend(x, g_ref, wq_ref, gq_ref, kv_ref, wo_ref):
    xn = _rms_rows(x, g_ref[...]).astype(BF16)
    q = _lane_group_norm(_dot(xn, wq_ref[...]), gq_ref[...], MEM_HEAD_DIM).astype(BF16)
    hw = MEM_HEADS * MEM_HEAD_DIM
    scale = MEM_HEAD_DIM ** -0.5
    outs = []
    for h in range(MEM_HEADS):
        cols = slice(h * MEM_HEAD_DIM, (h + 1) * MEM_HEAD_DIM)
        k = kv_ref[:, cols]
        v = kv_ref[:, hw + h * MEM_HEAD_DIM:hw + (h + 1) * MEM_HEAD_DIM]
        s = _dot_nt(q[:, cols], k) * scale
        p = jnp.exp(s - jnp.max(s, axis=-1, keepdims=True))
        p = p / jnp.sum(p, axis=-1, keepdims=True)
        outs.append(_dot(p.astype(BF16), v))
    o = jnp.concatenate(outs, axis=1).astype(BF16)
    return x + _dot(o, wo_ref[...])


def _post_mixer_kernel(*refs, n_in):
    a_refs = refs[:n_in]
    w_refs = refs[n_in:2 * n_in]
    (x_ref, gm_ref, wq_ref, gq_ref, kv_ref, wo_ref, gf_ref, wg_ref, wu_ref, wd_ref,
     o_ref, xn_ref, acc_ref) = refs[2 * n_in:]
    j = pl.program_id(1)

    @pl.when(j == 0)
    def _():
        x = x_ref[...]
        for a_ref, w_ref in zip(a_refs, w_refs):
            x = x + _dot(a_ref[...], w_ref[...])
        x = _mem_attend(x, gm_ref, wq_ref, gq_ref, kv_ref, wo_ref)
        xn_ref[...] = _rms_rows(x, gf_ref[...]).astype(BF16)
        acc_ref[...] = x

    xn = xn_ref[...]
    gate = _dot(xn, wg_ref[...])
    up = _dot(xn, wu_ref[...])
    act = (gate * jax.nn.sigmoid(gate) * up).astype(BF16)
    acc_ref[...] += _dot(act, wd_ref[...])

    @pl.when(j == pl.num_programs(1) - 1)
    def _():
        o_ref[...] = acc_ref[...]


def _post_mixer(parts, weights, x2, g_mem, wq, gq, kv, wo, g_ffn, wg, wu, wd, S, tm=512, n_split=2):
    T = x2.shape[0]
    nt = S // tm
    tf = D_FF // n_split
    n_in = len(parts)
    M, n = kv.shape[1], kv.shape[2]

    def const(a):
        return pl.BlockSpec(a.shape, lambda i, j: (0,) * a.ndim)

    in_specs = ([pl.BlockSpec((tm, p.shape[1]), lambda i, j: (i, 0)) for p in parts]
                + [const(w) for w in weights]
                + [pl.BlockSpec((tm, D_MODEL), lambda i, j: (i, 0)),
                   const(g_mem), const(wq), const(gq),
                   pl.BlockSpec((None, M, n), lambda i, j: (i // nt, 0, 0)),
                   const(wo), const(g_ffn),
                   pl.BlockSpec((D_MODEL, tf), lambda i, j: (0, j)),
                   pl.BlockSpec((D_MODEL, tf), lambda i, j: (0, j)),
                   pl.BlockSpec((tf, D_MODEL), lambda i, j: (j, 0))])
    return pl.pallas_call(
        functools.partial(_post_mixer_kernel, n_in=n_in), grid=(T // tm, n_split), in_specs=in_specs,
        out_specs=pl.BlockSpec((tm, D_MODEL), lambda i, j: (i, 0)),
        out_shape=jax.ShapeDtypeStruct((T, D_MODEL), F32),
        scratch_shapes=[pltpu.VMEM((tm, D_MODEL), BF16), pltpu.VMEM((tm, D_MODEL), F32)],
        compiler_params=_params(("parallel", "arbitrary")), name="post_mixer",
    )(*parts, *weights, x2, g_mem, wq, gq, kv, wo, g_ffn, wg, wu, wd)


def _nsa_prep_kernel(x_ref, trig_ref, gmix_ref, w_ref, nq_ref, nk_ref,
                     gq_ref, gks_ref, gkw_ref,
                     qc_ref, qr_ref, kvs_ref, kvst_ref, kvw_ref, kvwt_ref, kc_ref, vc_ref, gtt_ref,
                     stage_ref, *, n_tiles):
    xn = _rms_rows(x_ref[...], gmix_ref[...]).astype(BF16)
    c64, s64 = trig_ref[:, :LANES], trig_ref[:, LANES:]
    lane = _lane_iota(c64.shape)
    lo64 = (lane % 64) < 8
    first64 = lane < 64
    c64k = jnp.where(first64, c64, 1.0)
    s64k = jnp.where(first64, s64, 0.0)
    nq = nq_ref[...]
    nk = nk_ref[...]

    col = _ColumnProjector(xn, w_ref)

    for j in range(8):
        qn = _head_norm(col(j), nq, gq_ref[...])
        qr = _rope(qn, c64, s64, lo64, 8)
        qc_ref[2 * j] = qn[:, :64].astype(BF16)
        qc_ref[2 * j + 1] = qn[:, 64:].astype(BF16)
        qr_ref[2 * j] = qr[:, :64].astype(BF16)
        qr_ref[2 * j + 1] = qr[:, 64:].astype(BF16)
    tile = pl.program_id(0) % n_tiles
    sel_blk = tile * (ATT_T // NSA_SEL_LEN) + lax.shift_right_logical(
        _row_iota(c64.shape), NSA_SEL_LEN.bit_length() - 1)
    blk_onehot = jnp.where(lane == HEAD_DIM + sel_blk, 1.0, 0.0)
    for g in range(NSA_GROUPS):
        kv = _kv_column(col(8 + g), nk, gks_ref[...], c64k, s64k, lo64, first64)
        kvs_ref[g] = jnp.where(first64, kv, blk_onehot).astype(BF16)
        kvst_ref[g] = kv.T.astype(BF16)
        kv = _kv_column(col(12 + g), nk, gkw_ref[...], c64k, s64k, lo64, first64)
        kvw_ref[g] = kv.astype(BF16)
        kvwt_ref[g] = kv.T.astype(BF16)
    stride = NSA_CMP_STRIDE
    rows = stage_ref.shape[0] // stride
    for out_ref, first in ((kc_ref, 16), (vc_ref, 18)):
        for c in range(2):
            stage_ref[...] = col(first + c)
            for u in range(0, stride, 2):
                pair = [stage_ref[pl.ds(u + v, rows, stride=stride), :] for v in range(2)]
                for h in range(2):
                    halves = [p[:, h * HEAD_DIM:(h + 1) * HEAD_DIM] for p in pair]
                    out_ref[2 * c + h, :, u * HEAD_DIM:(u + 2) * HEAD_DIM] = jnp.concatenate(halves, axis=1)
    gates_t = jax.nn.sigmoid(col(20)).T
    for g in range(NSA_GROUPS):
        gtt_ref[g] = gates_t[12 * g:12 * (g + 1), :]


def _nsa_prep(x2, trig, gmix, w, tabs, B, S):
    T = x2.shape[0]
    tm = ATT_T
    nt = S // tm
    nq, nk, gq, gks, gkw = tabs

    def full(a):
        return pl.BlockSpec(a.shape, lambda i: (0,) * a.ndim)

    def hm(width, heads):
        return pl.BlockSpec((None, heads, tm, width), lambda i: (i // nt, 0, i % nt, 0))

    def hmt(heads):
        return pl.BlockSpec((None, heads, None, 128, tm), lambda i: (i // nt, 0, i % nt, 0, 0))

    def tokm(width):
        return pl.BlockSpec((None, tm, width), lambda i: (i // nt, i % nt, 0))

    out_shape = (
        jax.ShapeDtypeStruct((B, 16, S, 64), BF16),
        jax.ShapeDtypeStruct((B, 16, S, 64), BF16),
        jax.ShapeDtypeStruct((B, 4, S, 128), BF16),
        jax.ShapeDtypeStruct((B, 4, nt, 128, tm), BF16),
        jax.ShapeDtypeStruct((B, 4, S, 128), BF16),
        jax.ShapeDtypeStruct((B, 4, nt, 128, tm), BF16),
        jax.ShapeDtypeStruct((B, 4, S // 16, 1024), F32),
        jax.ShapeDtypeStruct((B, 4, S // 16, 1024), F32),
        jax.ShapeDtypeStruct((B, 4, 12, S), F32),
    )
    rows16 = pl.BlockSpec((None, 4, tm // 16, 1024), lambda i: (i // nt, 0, i % nt, 0))
    out_specs = (hm(64, 16), hm(64, 16), hm(128, 4), hmt(4), hm(128, 4), hmt(4), rows16, rows16,
                 pl.BlockSpec((None, 4, 12, tm), lambda i: (i // nt, 0, 0, i % nt)))
    in_specs = [pl.BlockSpec((tm, D_MODEL), lambda i: (i, 0)),
                pl.BlockSpec((tm, 2 * LANES), lambda i: (i, 0)),
                full(gmix), full(w), full(nq), full(nk), full(gq), full(gks), full(gkw)]
    return pl.pallas_call(
        functools.partial(_nsa_prep_kernel, n_tiles=nt), grid=(T // tm,), in_specs=in_specs, out_specs=out_specs,
        out_shape=out_shape, scratch_shapes=[pltpu.VMEM((tm, LANES), F32)],
        compiler_params=_params(("parallel",)), name="nsa_prep",
    )(x2, trig, gmix, w, nq, nk, gq, gks, gkw)


def _compress_one(x16, pa, pb, w1a, w1b, w2):
    n16 = x16.shape[0]
    h_a = _dot((x16 + pa).astype(BF16), w1a)
    h_b = _dot((x16 + pb).astype(BF16), w1b)
    pre = h_a + pltpu.roll(h_b, n16 - 1, 0)
    act = pre * jax.nn.sigmoid(pre)
    return _dot(act.astype(BF16), w2)


def _compress_kernel(xk_ref, xv_ref, pk_ref, pv_ref, w1k_ref, w1v_ref, w2k_ref, w2v_ref, gk_ref,
                     o_ref, ot_ref):
    half = w1k_ref.shape[0] // 2
    k = _compress_one(xk_ref[...], pk_ref[0:1, :], pk_ref[1:2, :],
                      w1k_ref[:half, :], w1k_ref[half:, :], w2k_ref[...])
    k = _rms_rows(k, gk_ref[...])
    v = _compress_one(xv_ref[...], pv_ref[0:1, :], pv_ref[1:2, :],
                      w1v_ref[:half, :], w1v_ref[half:, :], w2v_ref[...])
    kv = jnp.concatenate([k, v], axis=1)
    o_ref[...] = kv.astype(BF16)
    ot_ref[...] = kv.T.astype(BF16)


def _compress(xk16, xv16, pk, pv, w1k, w1v, w2k, w2v, gk):
    B, G, n16, width = xk16.shape

    def full(a):
        return pl.BlockSpec(a.shape, lambda b, g: (0,) * a.ndim)

    xspec = pl.BlockSpec((None, None, n16, width), lambda b, g: (b, g, 0, 0))
    return pl.pallas_call(
        _compress_kernel, grid=(B, G),
        in_specs=[xspec, xspec, full(pk), full(pv), full(w1k), full(w1v), full(w2k), full(w2v), full(gk)],
        out_specs=(pl.BlockSpec((None, None, n16, 128), lambda b, g: (b, g, 0, 0)),
                   pl.BlockSpec((None, None, 128, n16), lambda b, g: (b, g, 0, 0))),
        out_shape=(jax.ShapeDtypeStruct((B, G, n16, 128), BF16),
                   jax.ShapeDtypeStruct((B, G, 128, n16), BF16)),
        compiler_params=_params(("parallel", "parallel")), name="nsa_compress",
    )(xk16, xv16, pk, pv, w1k, w1v, w2k, w2v, gk)


NSA_GPS = 4


def _nsa_tile_masks(i, n16, n_sel, n_cmp, n_top):
    t = ATT_T
    t0 = i * t
    n_id = _row_iota((n16, t))
    q_id = t0 + _lane_iota((n16, t))
    cmp_visible = (n_id < n_cmp) & (n_id * NSA_CMP_STRIDE + (NSA_CMP_LEN - 1) <= q_id)
    b_id = _row_iota((n_sel, n16)) * NSA_SEL_LEN
    r_id = _lane_iota((n_sel, n16)) * NSA_CMP_STRIDE
    cover_t = ((r_id < b_id + NSA_SEL_LEN) & (r_id + NSA_CMP_LEN > b_id)
               & (_lane_iota((n_sel, n16)) < n_cmp))
    n_wc = NSA_WINDOW // t + 1
    cw = jnp.maximum(i - (n_wc - 1), 0)
    dist = (i - cw) * t + _lane_iota((n_wc * t, t)) - _row_iota((n_wc * t, t))
    blk = _row_iota((n_sel, t))
    cur = lax.shift_right_logical(t0 + _lane_iota((n_sel, t)), NSA_SEL_LEN.bit_length() - 1)
    return dict(
        bias_c=jnp.where(cmp_visible, 0.0, MASK_BIAS),
        cover_t=jnp.where(cover_t, 1.0, 0.0).astype(BF16),
        n_wc=n_wc, cw=cw, bias_w=jnp.where((dist >= 0) & (dist < NSA_WINDOW), 0.0, MASK_BIAS),
        forced=(blk == 0) | (blk == cur) | (blk == cur - 1), visible_blk=blk <= cur,
        n_wanted=jnp.minimum(cur[0:1, :] + 1, n_top).astype(F32))


def _nsa_front(qc_ref, qr_ref, kvc_ref, kvct_ref, kvw_ref, kvwt_ref, gtt_ref, sel_ref, part_ref,
               masks, n_top):
    t = ATT_T
    HG = NSA_HEADS // NSA_GROUPS
    n_sel = sel_ref.shape[0]
    n16 = kvc_ref.shape[0]

    p_sum = jnp.zeros((n16, t), F32)
    o_c = []
    probs, inv_ls, outs = _softmax_direct([_pad_q(qc_ref[j]) for j in range(HG)], kvc_ref[...],
                                          [kvct_ref[...]], masks["bias_c"])
    for j in range(HG):
        p_sum = p_sum + probs[j] * inv_ls[j]
        o_c.append(outs[j] * inv_ls[j])

    p_hi, p_lo = _split_bf16(p_sum)
    imp = _dot(masks["cover_t"], p_hi) + _dot(masks["cover_t"], p_lo)

    qs = [_pad_q(qr_ref[j]) for j in range(HG)]
    n_wc, cw = masks["n_wc"], masks["cw"]
    kw0 = pl.multiple_of(cw * t, t)
    _, inv_lw, out_w = _softmax_direct(qs, kvw_ref[pl.ds(kw0, n_wc * t), :],
                                       [kvwt_ref[cw + u] for u in range(n_wc)], masks["bias_w"])
    for j in range(HG):
        part_ref[:, j * t:(j + 1) * t] = (gtt_ref[3 * j:3 * j + 1, :] * o_c[j]
                                          + gtt_ref[3 * j + 2:3 * j + 3, :] * (out_w[j] * inv_lw[j]))

    imp = jnp.where(masks["forced"], NSA_FORCE, imp)
    imp = jnp.where(masks["visible_blk"], imp, NEG_INF)
    n_larger = jnp.zeros((n_sel, t), F32)
    for m in range(n_sel):
        n_larger = n_larger + jnp.where(imp[m:m + 1, :] > imp, 1.0, 0.0)
    sel_fast = n_larger < n_top
    n_picked = jnp.sum(jnp.where(sel_fast & masks["visible_blk"], 1.0, 0.0), axis=0, keepdims=True)
    sel_ref[...] = jnp.where(sel_fast, 0.0, MASK_BIAS)
    return qs, imp, jnp.abs(n_picked - masks["n_wanted"])


def _nsa_kernel(qc_ref, qr_ref, kvc_ref, kvct_ref, kvs_ref, kvst_ref, kvw_ref, kvwt_ref, gtt_ref,
                o_ref, sel_ref, part_ref, *flash_refs, n_cmp, n_top):
    t = ATT_T
    HG = NSA_HEADS // NSA_GROUPS
    n_slots = NSA_GPS * HG
    i = pl.program_id(2)

    masks = _nsa_tile_masks(i, kvc_ref.shape[1], sel_ref.shape[1], n_cmp, n_top)
    qs, imps, miss = [], [], None
    for g in range(NSA_GPS):
        heads_g = pl.ds(g * HG, HG)
        q_g, imp_g, miss_g = _nsa_front(
            qc_ref.at[heads_g], qr_ref.at[heads_g], kvc_ref.at[g], kvct_ref.at[g], kvw_ref.at[g],
            kvwt_ref.at[g], gtt_ref.at[g], sel_ref.at[g], part_ref.at[:, pl.ds(g * HG * t, HG * t)],
            masks, n_top)
        qs += q_g
        imps.append(imp_g)
        miss = miss_g if miss is None else jnp.maximum(miss, miss_g)

    @pl.when(jnp.max(miss) > 0.0)
    def _():
        for g in range(NSA_GPS):
            sel_ref[g] = (_rank_select_t(imps[g], None, n_top) - 1.0) * (-MASK_BIAS)

    qs_sel = [_bias_lanes(qs[s], sel_ref[s // HG]) for s in range(n_slots)]
    flash = _Flash(*flash_refs, t)
    flash.reset()

    def sel_operands(c):
        k0 = pl.multiple_of(c * t, t)
        kvs = [kvs_ref[s // HG, pl.ds(k0, t), :] for s in range(n_slots)]
        kvts = [[kvst_ref[s // HG, c]] for s in range(n_slots)]
        return kvs, kvts

    def past_chunk(c):
        kvs, kvts = sel_operands(c)
        return kvs, kvts, [None] * n_slots

    flash.run(qs_sel, i, past_chunk)
    kvs, kvts = sel_operands(i)
    flash.update(qs_sel, kvs, kvts, [_causal_bias(t)] * n_slots)

    heads = []
    for s in range(n_slots):
        g, j = divmod(s, HG)
        heads.append(part_ref[:, s * t:(s + 1) * t] + gtt_ref[g, 3 * j + 1:3 * j + 2, :] * flash.result(s))
    _store_heads(o_ref, heads)


def _nsa_attention(qc, qr, kvc, kvct, kvs, kvst, kvw, kvwt, gates_t):
    B, H, S, _ = qc.shape
    G = NSA_GROUPS
    HG = H // G
    t = ATT_T
    nt = S // t
    n16 = kvc.shape[2]
    n_cmp = (S - NSA_CMP_LEN) // NSA_CMP_STRIDE + 1
    n_sel = S // NSA_SEL_LEN
    n_top = min(NSA_SEL_TOPK, n_sel)
    gps = NSA_GPS
    assert G % gps == 0 and nt % 2 == 0 and n_sel <= HEAD_DIM and S >= (NSA_WINDOW // t + 1) * t
    qspec = pl.BlockSpec((None, gps * HG, t, 64), lambda b, g, i: (b, g, i, 0))
    once = pl.Buffered(1)
    kvspec = pl.BlockSpec((None, gps, S, 128), lambda b, g, i: (b, g, 0, 0), pipeline_mode=once)
    kvtspec = pl.BlockSpec((None, gps, nt, 128, t), lambda b, g, i: (b, g, 0, 0, 0), pipeline_mode=once)
    in_specs = [qspec, qspec,
                pl.BlockSpec((None, gps, n16, 128), lambda b, g, i: (b, g, 0, 0)),
                pl.BlockSpec((None, gps, 128, n16), lambda b, g, i: (b, g, 0, 0)),
                kvspec, kvtspec, kvspec, kvtspec,
                pl.BlockSpec((None, gps, 12, t), lambda b, g, i: (b, g, 0, i))]
    return pl.pallas_call(
        functools.partial(_nsa_kernel, n_cmp=n_cmp, n_top=n_top), grid=(B, G // gps, nt), in_specs=in_specs,
        out_specs=pl.BlockSpec((None, t, gps * HG * 64), lambda b, g, i: (b, i, g)),
        out_shape=jax.ShapeDtypeStruct((B, S, H * 64), BF16),
        scratch_shapes=[pltpu.VMEM((gps, n_sel, t), F32), pltpu.VMEM((LANES, gps * HG * t), F32)]
                       + _flash_scratch(gps * HG, t, t),
        compiler_params=_params(("parallel", "parallel", "parallel")), name="nsa_attention",
    )(qc, qr, kvc, kvct, kvs, kvst, kvw, kvwt, gates_t)


def _rope_freq_row(period, rot):
    half = rot // 2
    inv_freq = ROPE_THETA ** (-(jnp.arange(half, dtype=F32) * 2.0 / rot))
    lane = jnp.arange(LANES) % period
    f = jnp.where(lane < rot, inv_freq[lane % half], 0.0)
    return f.reshape(1, LANES).astype(F32)


def _norm_matrices():
    r = jnp.arange(LANES)
    same = (r[:, None] // 64) == (r[None, :] // 64)
    nq = jnp.where(same, 1.0 / 64, 0.0).astype(BF16)
    nk = jnp.where(same & (r[:, None] < 64), 1.0 / 64, 0.0).astype(BF16)
    return nq, nk


def _q_gain(g):
    return (jnp.tile(g.astype(F32), 2) * Q_SCALE).reshape(1, LANES)


def _k_gain(g):
    return jnp.concatenate([g.astype(F32), jnp.ones((64,), F32)]).reshape(1, LANES)


def _interleave_kv(wk, wv, n_heads):
    d = wk.shape[0]
    wk = wk.reshape(d, n_heads, 64)
    wv = wv.reshape(d, n_heads, 64)
    return jnp.concatenate([wk, wv], axis=2).reshape(d, n_heads * 128)


def _split_cols(w, sizes):
    out, start = [], 0
    for n in sizes:
        out.append(w[:, start:start + n])
        start += n
    return out


def _mixer_layer0(x2, trig, B, S, gmix, w_in, w_out, a_q_norm, a_k_norm, b_q_norm, b_k_norm):
    sizes = (512, 64, 64, 256, 32, 8, 512, 512, 512)
    waq, wak, wav, wiq, wik, wiw, wbq, wbk, wbv = _split_cols(w_in, sizes)
    pad = jnp.zeros((D_MODEL, LANES - 40), w_in.dtype)
    w = jnp.concatenate([waq, wbq, _interleave_kv(wbk, wbv, 8), wak, wav, wiq, wik, wiw, pad],
                        axis=1).astype(BF16)
    nq, nk = _norm_matrices()
    tabs = (nq, nk, _q_gain(a_q_norm), _q_gain(b_q_norm), _k_gain(a_k_norm), _k_gain(b_k_norm))
    aq, bq, bkv, bkvt, akv, akvt, iq, ik, iwt, km = _ab_prep(x2, trig, gmix, w, tabs, B, S)
    n_blk = S // MOBA_BLOCK
    kmean = km.reshape(B, n_blk, 8, 128).transpose(0, 2, 1, 3)
    o_a = _dsa_attention(iq, iwt, ik, aq, akv, akvt).reshape(B * S, 512)
    o_b = _moba_attention(bq, bkv, bkvt, kmean).reshape(B * S, 512)
    w_out = w_out.astype(BF16)
    return [o_a, o_b], [w_out[:512], w_out[512:]]


def _mixer_layer1(x2, trig, B, S, gmix, w_in, w_out, q_norm, kcmp_norm, ksel_norm, kwin_norm,
                  pos_k, pos_v, w1_k, w2_k, w1_v, w2_v):
    G = NSA_GROUPS
    sizes = (1024,) + (256,) * 6 + (48,)
    wq, wkc, wvc, wks, wvs, wkw, wvw, wgt = _split_cols(w_in, sizes)
    pad = jnp.zeros((D_MODEL, LANES - 48), w_in.dtype)
    w = jnp.concatenate([wq, _interleave_kv(wks, wvs, G), _interleave_kv(wkw, wvw, G),
                         wkc, wvc, wgt, pad], axis=1).astype(BF16)
    nq, nk = _norm_matrices()
    tabs = (nq, nk, _q_gain(q_norm), _k_gain(ksel_norm), _k_gain(kwin_norm))
    qc, qr, kvs, kvst, kvw, kvwt, kc16, vc16, gates_t = _nsa_prep(x2, trig, gmix, w, tabs, B, S)

    def pos_rows(p):
        return p.astype(F32).reshape(2, NSA_CMP_STRIDE * HEAD_DIM)

    kvc, kvct = _compress(kc16, vc16, pos_rows(pos_k), pos_rows(pos_v),
                          w1_k.astype(BF16), w1_v.astype(BF16), w2_k.astype(BF16), w2_v.astype(BF16),
                          kcmp_norm.astype(F32).reshape(1, HEAD_DIM))
    o = _nsa_attention(qc, qr, kvc, kvct, kvs, kvst, kvw, kvwt, gates_t)
    return [o.reshape(B * S, NSA_HEADS * HEAD_DIM)], [w_out.astype(BF16)]


def _finish_layer(parts, weights, x2, mem, S, g_mem, g_src, w_q, w_kv, w_o, q_norm, k_norm,
                  g_ffn, ffn_w_in, ffn_w_out):
    row = lambda v: v.astype(F32).reshape(1, -1)
    kv = _mem_kv(mem, row(g_src), w_kv.astype(BF16), row(k_norm))
    wg = ffn_w_in[:, :D_FF].astype(BF16)
    wu = ffn_w_in[:, D_FF:].astype(BF16)
    return _post_mixer(parts, weights, x2, row(g_mem), w_q.astype(BF16), row(q_norm), kv, w_o.astype(BF16),
                       row(g_ffn), wg, wu, ffn_w_out.astype(BF16), S)


def kernel(x, mem, positions, norm_mix, norm_mem, norm_mem_src, norm_ffn, ab_w_in, ab_w_out, dsa_q_norm, dsa_k_norm, moba_q_norm, moba_k_norm, nsa_w_in, nsa_w_out, nsa_q_norm, nsa_kcmp_norm, nsa_ksel_norm, nsa_kwin_norm, nsa_cmp_pos_k, nsa_cmp_pos_v, nsa_cmp_w1_k, nsa_cmp_w2_k, nsa_cmp_w1_v, nsa_cmp_w2_v, mem_w_q, mem_w_kv, mem_w_o, mem_q_norm, mem_k_norm, ffn_w_in, ffn_w_out):
    B, S, D = x.shape
    depth = norm_mix.shape[0]
    x2 = x.reshape(B * S, D)
    trig = _rope_trig(positions.astype(F32).reshape(B * S, 1), _rope_freq_row(64, 16), _rope_freq_row(32, 8))
    row = lambda v: v.astype(F32).reshape(1, -1)
    for i in range(depth):
        j = i // 2
        if i % 2 == 0:
            parts, weights = _mixer_layer0(x2, trig, B, S, row(norm_mix[i]), ab_w_in[j], ab_w_out[j],
                               dsa_q_norm[j], dsa_k_norm[j], moba_q_norm[j], moba_k_norm[j])
        else:
            parts, weights = _mixer_layer1(x2, trig, B, S, row(norm_mix[i]), nsa_w_in[j], nsa_w_out[j],
                               nsa_q_norm[j], nsa_kcmp_norm[j], nsa_ksel_norm[j], nsa_kwin_norm[j],
                               nsa_cmp_pos_k[j], nsa_cmp_pos_v[j], nsa_cmp_w1_k[j], nsa_cmp_w2_k[j],
                               nsa_cmp_w1_v[j], nsa_cmp_w2_v[j])
        x2 = _finish_layer(parts, weights, x2, mem, S, norm_mem[i], norm_mem_src[i], mem_w_q[i], mem_w_kv[i],
                           mem_w_o[i], mem_q_norm[i], mem_k_norm[i], norm_ffn[i], ffn_w_in[i], ffn_w_out[i])
    return x2.reshape(B, S, D)
```

```python
import functools
import math

import jax
import jax.numpy as jnp
from jax import lax
from jax.experimental import pallas as pl
from jax.experimental.pallas import tpu as pltpu

F32 = jnp.float32
BF16 = jnp.bfloat16
I32 = jnp.int32
I16 = jnp.int16

D_MODEL = 1024
N_MEM = 256
HEAD_DIM = 64
ROPE_THETA = 500000.0
RMS_EPS = 1e-6
NEG_INF = -1e30
TINY = 1e-20

DSA_HEADS = 8
DSA_IDX_HEADS = 8
DSA_IDX_DIM = 32
DSA_TOPK = 256
MOBA_HEADS = 8
MOBA_BLOCK = 256
MOBA_TOPK = 3
NSA_HEADS = 16
NSA_GROUPS = 4
NSA_CMP_LEN = 32
NSA_CMP_STRIDE = 16
NSA_SEL_LEN = 64
NSA_SEL_TOPK = 16
NSA_WINDOW = 512
NSA_FORCE = 1e4
MEM_HEADS = 4
MEM_HEAD_DIM = 128
D_FF = ((8 * D_MODEL + 3 * 256 - 1) // (3 * 256)) * 256

LANES = 128
SUBLANES = 8
INT_MIN = -(2 ** 31)
VMEM_LIMIT = 60 * 1024 * 1024

PROJ_GROUP = 4
ATT_T = 256
MASK_BIAS = -1e30
M_FLOOR = -1e29
LOG2E = math.log2(math.e)
Q_SCALE = HEAD_DIM ** -0.5 * LOG2E

NT_DIMS = (((1,), (1,)), ((), ()))


def _dot(a, b):
    return jnp.dot(a, b, preferred_element_type=F32)


def _dot_nt(a, b):
    return lax.dot_general(a, b, NT_DIMS, preferred_element_type=F32)


def _split_bf16(a):
    hi = a.astype(BF16)
    return hi, (a - hi.astype(F32)).astype(BF16)


def _split_dot(a, b):
    hi, lo = _split_bf16(a)
    return _dot(hi, b) + _dot(lo, b)


def _rms_rows(x, gain):
    ms = jnp.mean(x * x, axis=-1, keepdims=True)
    return x * lax.rsqrt(ms + RMS_EPS) * gain


def _params(sem):
    return pltpu.CompilerParams(dimension_semantics=sem, vmem_limit_bytes=VMEM_LIMIT)


def _head_norm(y, norm_m, gain):
    ms = _split_dot(y * y, norm_m)
    return y * lax.rsqrt(ms + RMS_EPS) * gain


def _rope(y, c, s, lo_mask, half):
    sw = jnp.where(lo_mask, pltpu.roll(y, LANES - half, 1), pltpu.roll(y, half, 1))
    return y * c + sw * s


def _lane_iota(shape):
    return lax.broadcasted_iota(I32, shape, 1)


def _row_iota(shape):
    return lax.broadcasted_iota(I32, shape, 0)


def _rope_tables(pos, ftab, period, half):
    ang = pos * ftab
    lane = _lane_iota(ang.shape) % period
    c = jnp.cos(ang)
    s = jnp.sin(ang) * jnp.where(lane < half, -1.0, 1.0)
    return c, s


def _pv_operand(kv):
    lane = _lane_iota(kv.shape)
    src = jnp.where(lane >= HEAD_DIM, kv, jnp.where(lane == 0, 1.0, 0.0))
    return src.T.astype(BF16)


class _ColumnProjector:
    def __init__(self, xn, w_ref):
        self.xn, self.w_ref, self.groups = xn, w_ref, {}

    def __call__(self, j):
        g, u = divmod(j, PROJ_GROUP)
        if g not in self.groups:
            width = PROJ_GROUP * LANES
            lo = g * width
            hi = min(lo + width, self.w_ref.shape[1])
            self.groups[g] = _dot(self.xn, self.w_ref[:, lo:hi])
        return self.groups[g][:, u * LANES:(u + 1) * LANES]


def _kv_column(yc, nk, gain, c64k, s64k, lo64, first64):
    kn = jnp.where(first64, _head_norm(yc, nk, gain), yc)
    return _rope(kn, c64k, s64k, lo64, 8)


def _rope_trig_kernel(pos_ref, f64_ref, f32_ref, o_ref):
    pos = pos_ref[...]
    c64, s64 = _rope_tables(pos, f64_ref[...], 64, 8)
    c32, s32 = _rope_tables(pos, f32_ref[...], 32, 4)
    o_ref[...] = jnp.concatenate([c64, s64, c32, s32], axis=1)


def _rope_trig(pos2, f64, f32t, tm=1024):
    T = pos2.shape[0]
    return pl.pallas_call(
        _rope_trig_kernel, grid=(T // tm,),
        in_specs=[pl.BlockSpec((tm, 1), lambda i: (i, 0)),
                  pl.BlockSpec(f64.shape, lambda i: (0, 0)), pl.BlockSpec(f32t.shape, lambda i: (0, 0))],
        out_specs=pl.BlockSpec((tm, 4 * LANES), lambda i: (i, 0)),
        out_shape=jax.ShapeDtypeStruct((T, 4 * LANES), F32),
        compiler_params=_params(("parallel",)), name="rope_trig",
    )(pos2, f64, f32t)


def _ab_prep_kernel(x_ref, trig_ref, gmix_ref, w_ref, nq_ref, nk_ref,
                    gaq_ref, gbq_ref, gak_ref, gbk_ref,
                    aq_ref, bq_ref, bkv_ref, bkvt_ref, akv_ref, akvt_ref, iq_ref, ik_ref, iwt_ref, km_ref,
                    *, n_tiles):
    xn = _rms_rows(x_ref[...], gmix_ref[...]).astype(BF16)
    c64, s64, c32, s32 = [trig_ref[:, j * LANES:(j + 1) * LANES] for j in range(4)]
    lane = _lane_iota(c64.shape)
    lo64 = (lane % 64) < 8
    lo32 = (lane % 32) < 4
    first64 = lane < 64
    c64k = jnp.where(first64, c64, 1.0)
    s64k = jnp.where(first64, s64, 0.0)
    first32 = lane < 32
    c32k = jnp.where(first32, c32, 1.0)
    s32k = jnp.where(first32, s32, 0.0)
    nq = nq_ref[...]
    nk = nk_ref[...]

    col = _ColumnProjector(xn, w_ref)

    for j in range(4):
        q = _rope(_head_norm(col(j), nq, gaq_ref[...]), c64, s64, lo64, 8)
        aq_ref[2 * j] = q[:, :64].astype(BF16)
        aq_ref[2 * j + 1] = q[:, 64:].astype(BF16)
    for j in range(4):
        q = _rope(_head_norm(col(4 + j), nq, gbq_ref[...]), c64, s64, lo64, 8)
        bq_ref[2 * j] = q[:, :64].astype(BF16)
        bq_ref[2 * j + 1] = q[:, 64:].astype(BF16)
    blk_onehot = jnp.where(lane == HEAD_DIM + pl.program_id(0) % n_tiles, 1.0, 0.0)
    for h in range(8):
        kv = _kv_column(col(8 + h), nk, gbk_ref[...], c64k, s64k, lo64, first64)
        bkv_ref[h] = jnp.where(first64, kv, blk_onehot).astype(BF16)
        bkvt_ref[h] = _pv_operand(kv)
        km_ref[h:h + 1, :] = jnp.mean(kv, axis=0, keepdims=True)
    kv = _kv_column(col(16), nk, gak_ref[...], c64k, s64k, lo64, first64)
    akv_ref[...] = kv.astype(BF16)
    akvt_ref[...] = _pv_operand(kv)
    for j in range(2):
        q = _rope(col(17 + j), c32, s32, lo32, 4)
        for u in range(4):
            iq_ref[4 * j + u] = q[:, 32 * u:32 * (u + 1)].astype(BF16)
    yc = col(19)
    ik_ref[...] = _rope(yc, c32k, s32k, lo32, 4)[:, :32].astype(BF16)
    iwt_ref[...] = yc.T[32:40, :]


def _ab_prep(x2, trig, gmix, w, tabs, B, S):
    T = x2.shape[0]
    tm = ATT_T
    nt = S // tm
    n_cols = w.shape[1]
    nq, nk, gaq, gbq, gak, gbk = tabs

    def full(a):
        return pl.BlockSpec(a.shape, lambda i: (0,) * a.ndim)

    def hm(width, heads=8):
        return pl.BlockSpec((None, heads, tm, width), lambda i: (i // nt, 0, i % nt, 0))

    def tokm(width):
        return pl.BlockSpec((None, tm, width), lambda i: (i // nt, i % nt, 0))

    out_shape = (
        jax.ShapeDtypeStruct((B, 8, S, 64), BF16),
        jax.ShapeDtypeStruct((B, 8, S, 64), BF16),
        jax.ShapeDtypeStruct((B, 8, S, 128), BF16),
        jax.ShapeDtypeStruct((B, 8, nt, 128, tm), BF16),
        jax.ShapeDtypeStruct((B, S, 128), BF16),
        jax.ShapeDtypeStruct((B, nt, 128, tm), BF16),
        jax.ShapeDtypeStruct((B, 8, S, 32), BF16),
        jax.ShapeDtypeStruct((B, S, 32), BF16),
        jax.ShapeDtypeStruct((B, 8, S), F32),
        jax.ShapeDtypeStruct((T // tm, 8, 128), F32),
    )
    out_specs = (hm(64), hm(64), hm(128),
                 pl.BlockSpec((None, 8, None, 128, tm), lambda i: (i // nt, 0, i % nt, 0, 0)),
                 tokm(128),
                 pl.BlockSpec((None, None, 128, tm), lambda i: (i // nt, i % nt, 0, 0)),
                 hm(32), tokm(32),
                 pl.BlockSpec((None, 8, tm), lambda i: (i // nt, 0, i % nt)),
                 pl.BlockSpec((None, 8, 128), lambda i: (i, 0, 0)))
    in_specs = [pl.BlockSpec((tm, D_MODEL), lambda i: (i, 0)),
                pl.BlockSpec((tm, 4 * LANES), lambda i: (i, 0)),
                full(gmix), pl.BlockSpec((D_MODEL, n_cols), lambda i: (0, 0)),
                full(nq), full(nk), full(gaq), full(gbq), full(gak), full(gbk)]
    return pl.pallas_call(
        functools.partial(_ab_prep_kernel, n_tiles=nt), grid=(T // tm,), in_specs=in_specs, out_specs=out_specs,
        out_shape=out_shape, compiler_params=_params(("parallel",)), name="ab_prep",
    )(x2, trig, gmix, w, nq, nk, gaq, gbq, gak, gbk)


def _pad_q(q):
    return jnp.concatenate([q, jnp.zeros_like(q)], axis=1)


def _bias_lanes(q, rows):
    n, tq = rows.shape
    parts = [jnp.zeros((HEAD_DIM, tq), F32), rows]
    if n < HEAD_DIM:
        parts.append(jnp.zeros((HEAD_DIM - n, tq), F32))
    lanes = jnp.concatenate(parts, axis=0).T.astype(BF16)
    return jnp.where(_lane_iota(q.shape) < HEAD_DIM, q, lanes)


class _Flash:
    def __init__(self, m_ref, acc_ref, s_ref, cmax_ref, p_ref, tq):
        self.m_ref, self.acc_ref, self.tq = m_ref, acc_ref, tq
        self.s_ref, self.cmax_ref, self.p_ref = s_ref, cmax_ref, p_ref

    def reset(self):
        self.m_ref[...] = jnp.full(self.m_ref.shape, M_FLOOR, F32)
        self.acc_ref[...] = jnp.zeros(self.acc_ref.shape, F32)

    def _scores(self, buf, qs, kvs, biases):
        tq = self.tq
        for i in range(len(qs)):
            s = _dot_nt(kvs[i], qs[i])
            if biases[i] is not None:
                s = s + biases[i]
            self.s_ref[buf, i, :s.shape[0], :] = s
            self.cmax_ref[buf, :, i * tq:(i + 1) * tq] = jnp.max(s, axis=0, keepdims=True)

    def update(self, qs, kvs, kvts, biases):
        self._scores(0, qs, kvs, biases)
        self._finish(0, kvts)

    def run(self, qs, count, operands):
        def scores(c, buf):
            kvs, _, biases = operands(c)
            self._scores(buf, qs, kvs, biases)

        def finish(c, buf):
            self._finish(buf, operands(c)[1])

        last = jnp.maximum(count - 1, 0)
        scores(0, 0)

        def two_chunks(pp, carry):
            c = 2 * pp
            scores(c + 1, 1)
            finish(c, 0)
            scores(jnp.minimum(c + 2, last), 0)
            finish(c + 1, 1)
            return carry

        lax.fori_loop(0, count // 2, two_chunks, 0)

        @pl.when(count % 2 == 1)
        def _():
            finish(count - 1, 0)

    def _finish(self, buf, kvts):
        n = len(kvts)
        tq = self.tq
        kc = sum(kvt.shape[1] for kvt in kvts[0])
        alphas = []
        for i in range(n):
            cols = slice(i * tq, (i + 1) * tq)
            m = self.m_ref[:, cols]
            m_new = jnp.maximum(m, self.cmax_ref[buf, :, cols])
            p = jnp.exp2(self.s_ref[buf, i, :kc, :] - m_new)
            alpha = jnp.exp2(m - m_new)
            self.m_ref[:, cols] = m_new
            self.p_ref[i, :kc, :] = p.astype(BF16)
            alphas.append(alpha)
        for i in range(n):
            cols = slice(i * tq, (i + 1) * tq)
            pv, r0 = None, 0
            for kvt in kvts[i]:
                part = _dot(kvt, self.p_ref[i, r0:r0 + kvt.shape[1], :])
                pv = part if pv is None else pv + part
                r0 += kvt.shape[1]
            self.acc_ref[:, cols] = alphas[i] * self.acc_ref[:, cols] + pv

    def result(self, slot):
        cols = slice(slot * self.tq, (slot + 1) * self.tq)
        acc = self.acc_ref[:, cols]
        return acc / jnp.maximum(acc[0:1, :], TINY)


def _flash_scratch(n_slots, tq, kc):
    return [pltpu.VMEM((1, n_slots * tq), F32), pltpu.VMEM((LANES, n_slots * tq), F32),
            pltpu.VMEM((2, n_slots, kc, tq), F32), pltpu.VMEM((2, 1, n_slots * tq), F32),
            pltpu.VMEM((n_slots, kc, tq), BF16)]


def _softmax_direct(qs, kv, kvts, bias):
    scores = [_dot_nt(kv, q) for q in qs]
    probs = []
    for s in scores:
        s = s + bias
        m = jnp.maximum(jnp.max(s, axis=0, keepdims=True), M_FLOOR)
        probs.append(jnp.exp2(s - m))
    inv_ls, outs = [], []
    for p in probs:
        pb = p.astype(BF16)
        o, r0 = None, 0
        for kvt in kvts:
            part = _dot(kvt, pb[r0:r0 + kvt.shape[1]])
            o = part if o is None else o + part
            r0 += kvt.shape[1]
        outs.append(o)
        inv_ls.append(1.0 / jnp.maximum(o[0:1, :], TINY))
    return probs, inv_ls, outs


def _causal_bias(t):
    return jnp.where(_row_iota((t, t)) <= _lane_iota((t, t)), 0.0, MASK_BIAS)


def _store_heads(o_ref, heads_t):
    tq = heads_t[0].shape[1]
    lane = _lane_iota((tq, LANES))
    for u in range(len(heads_t) // 2):
        even = pltpu.roll(heads_t[2 * u].T, 64, 1)
        odd = heads_t[2 * u + 1].T
        o_ref[:, u * LANES:(u + 1) * LANES] = jnp.where(lane < 64, even, odd).astype(o_ref.dtype)


def _rank_select_t(v, n_valid, n_top):
    n = v.shape[0]
    row = _row_iota(v.shape)
    rank = jnp.zeros(v.shape, F32)
    for m in range(n):
        vm = v[m:m + 1, :]
        ahead = (vm > v) | ((vm == v) & (m < row))
        if n_valid is not None:
            ahead = ahead & (m < n_valid)
        rank = rank + jnp.where(ahead, 1.0, 0.0)
    sel = rank < n_top
    if n_valid is not None:
        sel = sel & (row < n_valid)
    return jnp.where(sel, 1.0, 0.0)


def _dsa_kernel(iq_ref, iwt_ref, ik_ref, aq_ref, akv_ref, akvt_ref, o_ref,
                sk_ref, half_ref, bias_ref, xcut_ref, *flash_refs, k_top, index_bits):
    t = ATT_T
    i = pl.program_id(1)
    n_ch = i + 1
    kio = _row_iota((t, t))
    qio = _lane_iota((t, t))

    def causal(c):
        return (c - i) * t + kio <= qio

    def score_chunk(c):
        k0 = pl.multiple_of(c * t, t)
        ikc = ik_ref[pl.ds(k0, t), :]
        sc = jnp.zeros((t, t), F32)
        for h in range(DSA_IDX_HEADS):
            logit = _dot_nt(ikc, iq_ref[h])
            sc = sc + iwt_ref[h:h + 1, :] * jnp.maximum(logit, 0.0)
        sc = jnp.where(sc == 0.0, 0.0, sc)
        bits = pltpu.bitcast(sc, I32)
        key = bits ^ ((bits >> 31) & 0x7FFFFFFF)
        key = jnp.where(causal(c), key, INT_MIN)
        sk_ref[c] = key
        half_ref[c] = (key >> 16).astype(I16)

    def score_pair(cc, carry):
        score_chunk(2 * cc)
        score_chunk(2 * cc + 1)
        return carry

    lax.fori_loop(0, (n_ch + 1) // 2, score_pair, 0)

    def count(pred):
        def body(c, acc8):
            ind = jnp.where(pred(sk_ref[c], c), 1.0, 0.0)
            return acc8 + ind.reshape(-1, SUBLANES, t).sum(axis=0)
        acc8 = lax.fori_loop(0, n_ch, body, jnp.zeros((SUBLANES, t), F32))
        return jnp.sum(acc8, axis=0, keepdims=True)

    def count_half(cand):
        rows = 2 * SUBLANES

        def body(cc, acc):
            parts = []
            for c in (2 * cc, 2 * cc + 1):
                ind = jnp.where(half_ref[c] >= cand, jnp.bfloat16(1), jnp.bfloat16(0))
                parts += [ind[rows * j:rows * (j + 1), :] for j in range(t // rows)]
            while len(parts) > 1:
                parts = [parts[2 * j] + parts[2 * j + 1] for j in range(len(parts) // 2)]
            return acc + parts[0].astype(F32)
        acc = lax.fori_loop(0, (n_ch + 1) // 2, body, jnp.zeros((rows, t), F32))
        return jnp.sum(acc, axis=0, keepdims=True)

    def half_search():
        def bit_step(b, v):
            cand = v + lax.shift_left(jnp.int32(1), 15 - b)
            return jnp.where(count_half(cand.astype(I16)) >= k_top, cand, v)
        return lax.fori_loop(0, 16, bit_step, jnp.full((1, t), -(2 ** 15), I32))

    thr_hi = half_search()

    def low_half_chunk(c, carry):
        key = sk_ref[c]
        hi = key >> 16
        lo = (key & 0xFFFF) - 2 ** 15
        half_ref[c] = jnp.where(hi > thr_hi, 2 ** 15 - 1, jnp.where(hi < thr_hi, -(2 ** 15), lo)).astype(I16)
        return carry

    lax.fori_loop(0, n_ch, low_half_chunk, 0)
    thr = lax.shift_left(thr_hi, 16) + (half_search() + 2 ** 15)

    need = k_top - count(lambda blk, c: blk > thr)
    n_ge = count(lambda blk, c: blk >= thr)
    xcut_ref[...] = jnp.full((1, t), 2 ** 30, I32)

    @pl.when(jnp.max(n_ge) > k_top)
    def _():
        def x_step(b, x):
            cand = x + lax.shift_left(jnp.int32(1), index_bits - 1 - b)
            ties_below = count(lambda blk, c: (blk == thr) & (c * t + kio < cand))
            return jnp.where(ties_below <= need, cand, x)
        xcut_ref[...] = lax.fori_loop(0, index_bits, x_step, jnp.zeros((1, t), I32))

    xcut = xcut_ref[...]

    n_pairs = (n_ch + 1) // 2

    def bias_chunk(c, carry):
        blk = sk_ref[jnp.minimum(c, i)]
        keep = (blk > thr) | ((blk == thr) & (c * t + kio < xcut))
        bias_ref[c] = jnp.where(keep & causal(c), 0.0, MASK_BIAS)
        return carry

    lax.fori_loop(0, 2 * n_pairs, bias_chunk, 0)

    flash = _Flash(*flash_refs, t)
    flash.reset()
    qs = [_pad_q(aq_ref[h]) for h in range(DSA_HEADS)]

    n = DSA_HEADS

    def att_pair(cc):
        c0 = 2 * cc
        k0 = pl.multiple_of(c0 * t, 2 * t)
        kv = akv_ref[pl.ds(k0, 2 * t), :]
        bias = jnp.concatenate([bias_ref[c0], bias_ref[c0 + 1]], axis=0)
        return [kv] * n, [[akvt_ref[c0], akvt_ref[c0 + 1]]] * n, [bias] * n

    flash.run(qs, n_pairs, att_pair)
    _store_heads(o_ref, [flash.result(h) for h in range(DSA_HEADS)])


def _dsa_attention(iq, iwt, ik, aq, akv, akvt):
    B, _, S, _ = aq.shape
    t = ATT_T
    nt = S // t
    k_top = min(DSA_TOPK, S // 4)
    in_specs = [
        pl.BlockSpec((None, 8, t, 32), lambda b, i: (b, 0, i, 0)),
        pl.BlockSpec((None, 8, t), lambda b, i: (b, 0, i)),
        pl.BlockSpec((None, S, 32), lambda b, i: (b, 0, 0)),
        pl.BlockSpec((None, 8, t, 64), lambda b, i: (b, 0, i, 0)),
        pl.BlockSpec((None, S, 128), lambda b, i: (b, 0, 0)),
        pl.BlockSpec((None, nt, 128, t), lambda b, i: (b, 0, 0, 0)),
    ]
    return pl.pallas_call(
        functools.partial(_dsa_kernel, k_top=k_top, index_bits=S.bit_length()),
        grid=(B, nt), in_specs=in_specs,
        out_specs=pl.BlockSpec((None, t, 512), lambda b, i: (b, i, 0)),
        out_shape=jax.ShapeDtypeStruct((B, S, 512), BF16),
        scratch_shapes=[pltpu.VMEM((nt, t, t), I32), pltpu.VMEM((nt, t, t), I16), pltpu.VMEM((nt, t, t), F32),
                        pltpu.VMEM((1, t), I32)] + _flash_scratch(DSA_HEADS, t, 2 * t),
        compiler_params=_params(("parallel", "parallel")), name="dsa_attention",
    )(iq, iwt, ik, aq, akv, akvt)


MOBA_HPS = 8


def _moba_kernel(q_ref, kv_ref, kvt_ref, km_ref, o_ref, *flash_refs, n_top):
    t = ATT_T
    own = pl.program_id(2)
    causal = _causal_bias(t)
    flash = _Flash(*flash_refs, t)
    flash.reset()
    qs = []
    for hh in range(MOBA_HPS):
        q = _pad_q(q_ref[hh])
        km_hi, km_lo = _split_bf16(km_ref[hh])
        gate = _dot_nt(km_hi, q) + _dot_nt(km_lo, q)
        keep = _rank_select_t(gate, own, n_top)
        keep = jnp.where(_row_iota(keep.shape) == own, 1.0, keep)
        qs.append(_bias_lanes(q, (keep - 1.0) * (-MASK_BIAS)))

    def operands(cc):
        n0 = 2 * cc
        k0 = pl.multiple_of(n0 * t, 2 * t)
        heads = range(MOBA_HPS)
        return (n0, [kv_ref[hh, pl.ds(k0, 2 * t), :] for hh in heads],
                [[kvt_ref[hh, n0], kvt_ref[hh, n0 + 1]] for hh in heads])

    def past_pair(cc):
        _, kvs, kvts = operands(cc)
        return kvs, kvts, [None] * MOBA_HPS

    flash.run(qs, own // 2, past_pair)
    heads = range(MOBA_HPS)

    @pl.when(own % 2 == 0)
    def _():
        k0 = pl.multiple_of(own * t, t)
        flash.update(qs, [kv_ref[hh, pl.ds(k0, t), :] for hh in heads],
                     [[kvt_ref[hh, own]] for hh in heads], [causal] * MOBA_HPS)

    @pl.when(own % 2 == 1)
    def _():
        _, kvs, kvts = operands(own // 2)
        bias = jnp.concatenate([jnp.zeros((t, t), F32), causal], axis=0)
        flash.update(qs, kvs, kvts, [bias] * MOBA_HPS)

    _store_heads(o_ref, [flash.result(hh) for hh in range(MOBA_HPS)])


def _moba_attention(bq, bkv, bkvt, kmean):
    B, H, S, _ = bq.shape
    t = ATT_T
    hps = MOBA_HPS
    n_blk = S // MOBA_BLOCK
    assert t == MOBA_BLOCK and n_blk % 2 == 0 and H % hps == 0
    n_top = max(1, min(MOBA_TOPK, n_blk - 1))
    in_specs = [
        pl.BlockSpec((None, hps, t, 64), lambda b, h, i: (b, h, i, 0)),
        pl.BlockSpec((None, hps, S, 128), lambda b, h, i: (b, h, 0, 0)),
        pl.BlockSpec((None, hps, n_blk, 128, t), lambda b, h, i: (b, h, 0, 0, 0)),
        pl.BlockSpec((None, hps, n_blk, 128), lambda b, h, i: (b, h, 0, 0)),
    ]
    return pl.pallas_call(
        functools.partial(_moba_kernel, n_top=n_top), grid=(B, H // hps, S // t), in_specs=in_specs,
        out_specs=pl.BlockSpec((None, t, hps * 64), lambda b, h, i: (b, i, h)),
        out_shape=jax.ShapeDtypeStruct((B, S, H * 64), BF16),
        scratch_shapes=_flash_scratch(hps, t, 2 * t),
        compiler_params=_params(("parallel", "parallel", "parallel")), name="moba_attention",
    )(bq, bkv, bkvt, kmean)


def _lane_group_norm(y, gain, width):
    outs = []
    for j in range(y.shape[1] // width):
        yc = y[:, j * width:(j + 1) * width]
        outs.append(_rms_rows(yc, gain))
    return jnp.concatenate(outs, axis=1)


def _mem_kv_kernel(m_ref, g_ref, w_ref, gk_ref, o_ref):
    mn = _rms_rows(m_ref[...], g_ref[...]).astype(BF16)
    y = _dot(mn, w_ref[...])
    hw = MEM_HEADS * MEM_HEAD_DIM
    k = _lane_group_norm(y[:, :hw], gk_ref[...], MEM_HEAD_DIM)
    o_ref[...] = jnp.concatenate([k, y[:, hw:]], axis=1).astype(BF16)


def _mem_kv(mem, g, w, gk):
    B, M, _ = mem.shape
    n = w.shape[1]
    return pl.pallas_call(
        _mem_kv_kernel, grid=(B,),
        in_specs=[pl.BlockSpec((None, M, D_MODEL), lambda b: (b, 0, 0)),
                  pl.BlockSpec(g.shape, lambda b: (0, 0)),
                  pl.BlockSpec(w.shape, lambda b: (0, 0)),
                  pl.BlockSpec(gk.shape, lambda b: (0, 0))],
        out_specs=pl.BlockSpec((None, M, n), lambda b: (b, 0, 0)),
        out_shape=jax.ShapeDtypeStruct((B, M, n), BF16),
        compiler_params=_params(("parallel",)), name="mem_kv",
    )(mem, g, w, gk)


def _mem_attend(x, g_ref, wq_ref, gq_ref, kv_ref, wo_ref):
    xn = _rms_rows(x, g_ref[...]).astype(BF16)
    q = _lane_group_norm(_dot(xn, wq_ref[...]), gq_ref[...], MEM_HEAD_DIM).astype(BF16)
    hw = MEM_HEADS * MEM_HEAD_DIM
    scale = MEM_HEAD_DIM ** -0.5
    outs = []
    for h in range(MEM_HEADS):
        cols = slice(h * MEM_HEAD_DIM, (h + 1) * MEM_HEAD_DIM)
        k = kv_ref[:, cols]
        v = kv_ref[:, hw + h * MEM_HEAD_DIM:hw + (h + 1) * MEM_HEAD_DIM]
        s = _dot_nt(q[:, cols], k) * scale
        p = jnp.exp(s - jnp.max(s, axis=-1, keepdims=True))
        p = p / jnp.sum(p, axis=-1, keepdims=True)
        outs.append(_dot(p.astype(BF16), v))
    o = jnp.concatenate(outs, axis=1).astype(BF16)
    return x + _dot(o, wo_ref[...])


def _post_mixer_kernel(*refs, n_in):
    a_refs = refs[:n_in]
    w_refs = refs[n_in:2 * n_in]
    (x_ref, gm_ref, wq_ref, gq_ref, kv_ref, wo_ref, gf_ref, wg_ref, wu_ref, wd_ref,
     o_ref, xn_ref, acc_ref) = refs[2 * n_in:]
    j = pl.program_id(1)

    @pl.when(j == 0)
    def _():
        x = x_ref[...]
        for a_ref, w_ref in zip(a_refs, w_refs):
            x = x + _dot(a_ref[...], w_ref[...])
        x = _mem_attend(x, gm_ref, wq_ref, gq_ref, kv_ref, wo_ref)
        xn_ref[...] = _rms_rows(x, gf_ref[...]).astype(BF16)
        acc_ref[...] = x

    xn = xn_ref[...]
    gate = _dot(xn, wg_ref[...])
    up = _dot(xn, wu_ref[...])
    act = (gate * jax.nn.sigmoid(gate) * up).astype(BF16)
    acc_ref[...] += _dot(act, wd_ref[...])

    @pl.when(j == pl.num_programs(1) - 1)
    def _():
        o_ref[...] = acc_ref[...]


def _post_mixer(parts, weights, x2, g_mem, wq, gq, kv, wo, g_ffn, wg, wu, wd, S, tm=512, n_split=2):
    T = x2.shape[0]
    nt = S // tm
    tf = D_FF // n_split
    n_in = len(parts)
    M, n = kv.shape[1], kv.shape[2]

    def const(a):
        return pl.BlockSpec(a.shape, lambda i, j: (0,) * a.ndim)

    in_specs = ([pl.BlockSpec((tm, p.shape[1]), lambda i, j: (i, 0)) for p in parts]
                + [const(w) for w in weights]
                + [pl.BlockSpec((tm, D_MODEL), lambda i, j: (i, 0)),
                   const(g_mem), const(wq), const(gq),
                   pl.BlockSpec((None, M, n), lambda i, j: (i // nt, 0, 0)),
                   const(wo), const(g_ffn),
                   pl.BlockSpec((D_MODEL, tf), lambda i, j: (0, j)),
                   pl.BlockSpec((D_MODEL, tf), lambda i, j: (0, j)),
                   pl.BlockSpec((tf, D_MODEL), lambda i, j: (j, 0))])
    return pl.pallas_call(
        functools.partial(_post_mixer_kernel, n_in=n_in), grid=(T // tm, n_split), in_specs=in_specs,
        out_specs=pl.BlockSpec((tm, D_MODEL), lambda i, j: (i, 0)),
        out_shape=jax.ShapeDtypeStruct((T, D_MODEL), F32),
        scratch_shapes=[pltpu.VMEM((tm, D_MODEL), BF16), pltpu.VMEM((tm, D_MODEL), F32)],
        compiler_params=_params(("parallel", "arbitrary")), name="post_mixer",
    )(*parts, *weights, x2, g_mem, wq, gq, kv, wo, g_ffn, wg, wu, wd)


def _nsa_prep_kernel(x_ref, trig_ref, gmix_ref, w_ref, nq_ref, nk_ref,
                     gq_ref, gks_ref, gkw_ref,
                     qc_ref, qr_ref, kvs_ref, kvst_ref, kvw_ref, kvwt_ref, kc_ref, vc_ref, gtt_ref,
                     stage_ref, *, n_tiles):
    xn = _rms_rows(x_ref[...], gmix_ref[...]).astype(BF16)
    c64, s64 = trig_ref[:, :LANES], trig_ref[:, LANES:]
    lane = _lane_iota(c64.shape)
    lo64 = (lane % 64) < 8
    first64 = lane < 64
    c64k = jnp.where(first64, c64, 1.0)
    s64k = jnp.where(first64, s64, 0.0)
    nq = nq_ref[...]
    nk = nk_ref[...]

    col = _ColumnProjector(xn, w_ref)

    for j in range(8):
        qn = _head_norm(col(j), nq, gq_ref[...])
        qr = _rope(qn, c64, s64, lo64, 8)
        qc_ref[2 * j] = qn[:, :64].astype(BF16)
        qc_ref[2 * j + 1] = qn[:, 64:].astype(BF16)
        qr_ref[2 * j] = qr[:, :64].astype(BF16)
        qr_ref[2 * j + 1] = qr[:, 64:].astype(BF16)
    tile = pl.program_id(0) % n_tiles
    sel_blk = tile * (ATT_T // NSA_SEL_LEN) + lax.shift_right_logical(
        _row_iota(c64.shape), NSA_SEL_LEN.bit_length() - 1)
    blk_onehot = jnp.where(lane == HEAD_DIM + sel_blk, 1.0, 0.0)
    for g in range(NSA_GROUPS):
        kv = _kv_column(col(8 + g), nk, gks_ref[...], c64k, s64k, lo64, first64)
        kvs_ref[g] = jnp.where(first64, kv, blk_onehot).astype(BF16)
        kvst_ref[g] = _pv_operand(kv)
        kv = _kv_column(col(12 + g), nk, gkw_ref[...], c64k, s64k, lo64, first64)
        kvw_ref[g] = kv.astype(BF16)
        kvwt_ref[g] = _pv_operand(kv)
    stride = NSA_CMP_STRIDE
    rows = stage_ref.shape[0] // stride
    for out_ref, first in ((kc_ref, 16), (vc_ref, 18)):
        for c in range(2):
            stage_ref[...] = col(first + c)
            for u in range(0, stride, 2):
                pair = [stage_ref[pl.ds(u + v, rows, stride=stride), :] for v in range(2)]
                for h in range(2):
                    halves = [p[:, h * HEAD_DIM:(h + 1) * HEAD_DIM] for p in pair]
                    out_ref[2 * c + h, :, u * HEAD_DIM:(u + 2) * HEAD_DIM] = jnp.concatenate(halves, axis=1)
    gates_t = jax.nn.sigmoid(col(20)).T
    for g in range(NSA_GROUPS):
        gtt_ref[g] = gates_t[12 * g:12 * (g + 1), :]


def _nsa_prep(x2, trig, gmix, w, tabs, B, S):
    T = x2.shape[0]
    tm = ATT_T
    nt = S // tm
    nq, nk, gq, gks, gkw = tabs

    def full(a):
        return pl.BlockSpec(a.shape, lambda i: (0,) * a.ndim)

    def hm(width, heads):
        return pl.BlockSpec((None, heads, tm, width), lambda i: (i // nt, 0, i % nt, 0))

    def hmt(heads):
        return pl.BlockSpec((None, heads, None, 128, tm), lambda i: (i // nt, 0, i % nt, 0, 0))

    def tokm(width):
        return pl.BlockSpec((None, tm, width), lambda i: (i // nt, i % nt, 0))

    out_shape = (
        jax.ShapeDtypeStruct((B, 16, S, 64), BF16),
        jax.ShapeDtypeStruct((B, 16, S, 64), BF16),
        jax.ShapeDtypeStruct((B, 4, S, 128), BF16),
        jax.ShapeDtypeStruct((B, 4, nt, 128, tm), BF16),
        jax.ShapeDtypeStruct((B, 4, S, 128), BF16),
        jax.ShapeDtypeStruct((B, 4, nt, 128, tm), BF16),
        jax.ShapeDtypeStruct((B, 4, S // 16, 1024), F32),
        jax.ShapeDtypeStruct((B, 4, S // 16, 1024), F32),
        jax.ShapeDtypeStruct((B, 4, 12, S), F32),
    )
    rows16 = pl.BlockSpec((None, 4, tm // 16, 1024), lambda i: (i // nt, 0, i % nt, 0))
    out_specs = (hm(64, 16), hm(64, 16), hm(128, 4), hmt(4), hm(128, 4), hmt(4), rows16, rows16,
                 pl.BlockSpec((None, 4, 12, tm), lambda i: (i // nt, 0, 0, i % nt)))
    in_specs = [pl.BlockSpec((tm, D_MODEL), lambda i: (i, 0)),
                pl.BlockSpec((tm, 2 * LANES), lambda i: (i, 0)),
                full(gmix), full(w), full(nq), full(nk), full(gq), full(gks), full(gkw)]
    return pl.pallas_call(
        functools.partial(_nsa_prep_kernel, n_tiles=nt), grid=(T // tm,), in_specs=in_specs, out_specs=out_specs,
        out_shape=out_shape, scratch_shapes=[pltpu.VMEM((tm, LANES), F32)],
        compiler_params=_params(("parallel",)), name="nsa_prep",
    )(x2, trig, gmix, w, nq, nk, gq, gks, gkw)


def _compress_one(x16, pa, pb, w1a, w1b, w2):
    n16 = x16.shape[0]
    h_a = _dot((x16 + pa).astype(BF16), w1a)
    h_b = _dot((x16 + pb).astype(BF16), w1b)
    pre = h_a + pltpu.roll(h_b, n16 - 1, 0)
    act = pre * jax.nn.sigmoid(pre)
    return _dot(act.astype(BF16), w2)


def _compress_kernel(xk_ref, xv_ref, pk_ref, pv_ref, w1k_ref, w1v_ref, w2k_ref, w2v_ref, gk_ref,
                     o_ref, ot_ref):
    half = w1k_ref.shape[0] // 2
    k = _compress_one(xk_ref[...], pk_ref[0:1, :], pk_ref[1:2, :],
                      w1k_ref[:half, :], w1k_ref[half:, :], w2k_ref[...])
    k = _rms_rows(k, gk_ref[...])
    v = _compress_one(xv_ref[...], pv_ref[0:1, :], pv_ref[1:2, :],
                      w1v_ref[:half, :], w1v_ref[half:, :], w2v_ref[...])
    kv = jnp.concatenate([k, v], axis=1)
    o_ref[...] = kv.astype(BF16)
    ot_ref[...] = _pv_operand(kv)


def _compress(xk16, xv16, pk, pv, w1k, w1v, w2k, w2v, gk):
    B, G, n16, width = xk16.shape

    def full(a):
        return pl.BlockSpec(a.shape, lambda b, g: (0,) * a.ndim)

    xspec = pl.BlockSpec((None, None, n16, width), lambda b, g: (b, g, 0, 0))
    return pl.pallas_call(
        _compress_kernel, grid=(B, G),
        in_specs=[xspec, xspec, full(pk), full(pv), full(w1k), full(w1v), full(w2k), full(w2v), full(gk)],
        out_specs=(pl.BlockSpec((None, None, n16, 128), lambda b, g: (b, g, 0, 0)),
                   pl.BlockSpec((None, None, 128, n16), lambda b, g: (b, g, 0, 0))),
        out_shape=(jax.ShapeDtypeStruct((B, G, n16, 128), BF16),
                   jax.ShapeDtypeStruct((B, G, 128, n16), BF16)),
        compiler_params=_params(("parallel", "parallel")), name="nsa_compress",
    )(xk16, xv16, pk, pv, w1k, w1v, w2k, w2v, gk)


NSA_GPS = 4


def _nsa_tile_masks(i, n16, n_sel, n_cmp, n_top):
    t = ATT_T
    t0 = i * t
    n_id = _row_iota((n16, t))
    q_id = t0 + _lane_iota((n16, t))
    cmp_visible = (n_id < n_cmp) & (n_id * NSA_CMP_STRIDE + (NSA_CMP_LEN - 1) <= q_id)
    b_id = _row_iota((n_sel, n16)) * NSA_SEL_LEN
    r_id = _lane_iota((n_sel, n16)) * NSA_CMP_STRIDE
    cover_t = ((r_id < b_id + NSA_SEL_LEN) & (r_id + NSA_CMP_LEN > b_id)
               & (_lane_iota((n_sel, n16)) < n_cmp))
    n_wc = NSA_WINDOW // t + 1
    cw = jnp.maximum(i - (n_wc - 1), 0)
    dist = (i - cw) * t + _lane_iota((n_wc * t, t)) - _row_iota((n_wc * t, t))
    blk = _row_iota((n_sel, t))
    cur = lax.shift_right_logical(t0 + _lane_iota((n_sel, t)), NSA_SEL_LEN.bit_length() - 1)
    return dict(
        bias_c=jnp.where(cmp_visible, 0.0, MASK_BIAS),
        cover_t=jnp.where(cover_t, 1.0, 0.0).astype(BF16),
        n_wc=n_wc, cw=cw, bias_w=jnp.where((dist >= 0) & (dist < NSA_WINDOW), 0.0, MASK_BIAS),
        forced=(blk == 0) | (blk == cur) | (blk == cur - 1), visible_blk=blk <= cur,
        n_wanted=jnp.minimum(cur[0:1, :] + 1, n_top).astype(F32))


def _nsa_front(qc_ref, qr_ref, kvc_ref, kvct_ref, kvw_ref, kvwt_ref, gtt_ref, sel_ref, part_ref,
               masks, n_top):
    t = ATT_T
    HG = NSA_HEADS // NSA_GROUPS
    n_sel = sel_ref.shape[0]
    n16 = kvc_ref.shape[0]

    p_sum = jnp.zeros((n16, t), F32)
    o_c = []
    probs, inv_ls, outs = _softmax_direct([_pad_q(qc_ref[j]) for j in range(HG)], kvc_ref[...],
                                          [kvct_ref[...]], masks["bias_c"])
    for j in range(HG):
        p_sum = p_sum + probs[j] * inv_ls[j]
        o_c.append(outs[j] * inv_ls[j])

    p_hi, p_lo = _split_bf16(p_sum)
    imp = _dot(masks["cover_t"], p_hi) + _dot(masks["cover_t"], p_lo)

    qs = [_pad_q(qr_ref[j]) for j in range(HG)]
    n_wc, cw = masks["n_wc"], masks["cw"]
    kw0 = pl.multiple_of(cw * t, t)
    _, inv_lw, out_w = _softmax_direct(qs, kvw_ref[pl.ds(kw0, n_wc * t), :],
                                       [kvwt_ref[cw + u] for u in range(n_wc)], masks["bias_w"])
    for j in range(HG):
        part_ref[:, j * t:(j + 1) * t] = (gtt_ref[3 * j:3 * j + 1, :] * o_c[j]
                                          + gtt_ref[3 * j + 2:3 * j + 3, :] * (out_w[j] * inv_lw[j]))

    imp = jnp.where(masks["forced"], NSA_FORCE, imp)
    imp = jnp.where(masks["visible_blk"], imp, NEG_INF)
    n_larger = jnp.zeros((n_sel, t), F32)
    for m in range(n_sel):
        n_larger = n_larger + jnp.where(imp[m:m + 1, :] > imp, 1.0, 0.0)
    sel_fast = n_larger < n_top
    n_picked = jnp.sum(jnp.where(sel_fast & masks["visible_blk"], 1.0, 0.0), axis=0, keepdims=True)
    sel_ref[...] = jnp.where(sel_fast, 0.0, MASK_BIAS)
    return qs, imp, jnp.abs(n_picked - masks["n_wanted"])


def _nsa_kernel(qc_ref, qr_ref, kvc_ref, kvct_ref, kvs_ref, kvst_ref, kvw_ref, kvwt_ref, gtt_ref,
                o_ref, sel_ref, part_ref, *flash_refs, n_cmp, n_top):
    t = ATT_T
    HG = NSA_HEADS // NSA_GROUPS
    n_slots = NSA_GPS * HG
    i = pl.program_id(2)

    masks = _nsa_tile_masks(i, kvc_ref.shape[1], sel_ref.shape[1], n_cmp, n_top)
    qs, imps, miss = [], [], None
    for g in range(NSA_GPS):
        heads_g = pl.ds(g * HG, HG)
        q_g, imp_g, miss_g = _nsa_front(
            qc_ref.at[heads_g], qr_ref.at[heads_g], kvc_ref.at[g], kvct_ref.at[g], kvw_ref.at[g],
            kvwt_ref.at[g], gtt_ref.at[g], sel_ref.at[g], part_ref.at[:, pl.ds(g * HG * t, HG * t)],
            masks, n_top)
        qs += q_g
        imps.append(imp_g)
        miss = miss_g if miss is None else jnp.maximum(miss, miss_g)

    @pl.when(jnp.max(miss) > 0.0)
    def _():
        for g in range(NSA_GPS):
            sel_ref[g] = (_rank_select_t(imps[g], None, n_top) - 1.0) * (-MASK_BIAS)

    qs_sel = [_bias_lanes(qs[s], sel_ref[s // HG]) for s in range(n_slots)]
    flash = _Flash(*flash_refs, t)
    flash.reset()

    def sel_operands(c):
        k0 = pl.multiple_of(c * t, t)
        kvs = [kvs_ref[s // HG, pl.ds(k0, t), :] for s in range(n_slots)]
        kvts = [[kvst_ref[s // HG, c]] for s in range(n_slots)]
        return kvs, kvts

    def past_chunk(c):
        kvs, kvts = sel_operands(c)
        return kvs, kvts, [None] * n_slots

    flash.run(qs_sel, i, past_chunk)
    kvs, kvts = sel_operands(i)
    flash.update(qs_sel, kvs, kvts, [_causal_bias(t)] * n_slots)

    heads = []
    for s in range(n_slots):
        g, j = divmod(s, HG)
        heads.append(part_ref[:, s * t:(s + 1) * t] + gtt_ref[g, 3 * j + 1:3 * j + 2, :] * flash.result(s))
    _store_heads(o_ref, heads)


def _nsa_attention(qc, qr, kvc, kvct, kvs, kvst, kvw, kvwt, gates_t):
    B, H, S, _ = qc.shape
    G = NSA_GROUPS
    HG = H // G
    t = ATT_T
    nt = S // t
    n16 = kvc.shape[2]
    n_cmp = (S - NSA_CMP_LEN) // NSA_CMP_STRIDE + 1
    n_sel = S // NSA_SEL_LEN
    n_top = min(NSA_SEL_TOPK, n_sel)
    gps = NSA_GPS
    assert G % gps == 0 and nt % 2 == 0 and n_sel <= HEAD_DIM and S >= (NSA_WINDOW // t + 1) * t
    qspec = pl.BlockSpec((None, gps * HG, t, 64), lambda b, g, i: (b, g, i, 0))
    once = pl.Buffered(1)
    kvspec = pl.BlockSpec((None, gps, S, 128), lambda b, g, i: (b, g, 0, 0), pipeline_mode=once)
    kvtspec = pl.BlockSpec((None, gps, nt, 128, t), lambda b, g, i: (b, g, 0, 0, 0), pipeline_mode=once)
    in_specs = [qspec, qspec,
                pl.BlockSpec((None, gps, n16, 128), lambda b, g, i: (b, g, 0, 0)),
                pl.BlockSpec((None, gps, 128, n16), lambda b, g, i: (b, g, 0, 0)),
                kvspec, kvtspec, kvspec, kvtspec,
                pl.BlockSpec((None, gps, 12, t), lambda b, g, i: (b, g, 0, i))]
    return pl.pallas_call(
        functools.partial(_nsa_kernel, n_cmp=n_cmp, n_top=n_top), grid=(B, G // gps, nt), in_specs=in_specs,
        out_specs=pl.BlockSpec((None, t, gps * HG * 64), lambda b, g, i: (b, i, g)),
        out_shape=jax.ShapeDtypeStruct((B, S, H * 64), BF16),
        scratch_shapes=[pltpu.VMEM((gps, n_sel, t), F32), pltpu.VMEM((LANES, gps * HG * t), F32)]
                       + _flash_scratch(gps * HG, t, t),
        compiler_params=_params(("parallel", "parallel", "parallel")), name="nsa_attention",
    )(qc, qr, kvc, kvct, kvs, kvst, kvw, kvwt, gates_t)


def _rope_freq_row(period, rot):
    half = rot // 2
    inv_freq = ROPE_THETA ** (-(jnp.arange(half, dtype=F32) * 2.0 / rot))
    lane = jnp.arange(LANES) % period
    f = jnp.where(lane < rot, inv_freq[lane % half], 0.0)
    return f.reshape(1, LANES).astype(F32)


def _norm_matrices():
    r = jnp.arange(LANES)
    same = (r[:, None] // 64) == (r[None, :] // 64)
    nq = jnp.where(same, 1.0 / 64, 0.0).astype(BF16)
    nk = jnp.where(same & (r[:, None] < 64), 1.0 / 64, 0.0).astype(BF16)
    return nq, nk


def _q_gain(g):
    return (jnp.tile(g.astype(F32), 2) * Q_SCALE).reshape(1, LANES)


def _k_gain(g):
    return jnp.concatenate([g.astype(F32), jnp.ones((64,), F32)]).reshape(1, LANES)


def _interleave_kv(wk, wv, n_heads):
    d = wk.shape[0]
    wk = wk.reshape(d, n_heads, 64)
    wv = wv.reshape(d, n_heads, 64)
    return jnp.concatenate([wk, wv], axis=2).reshape(d, n_heads * 128)


def _split_cols(w, sizes):
    out, start = [], 0
    for n in sizes:
        out.append(w[:, start:start + n])
        start += n
    return out


def _mixer_layer0(x2, trig, B, S, gmix, w_in, w_out, a_q_norm, a_k_norm, b_q_norm, b_k_norm):
    sizes = (512, 64, 64, 256, 32, 8, 512, 512, 512)
    waq, wak, wav, wiq, wik, wiw, wbq, wbk, wbv = _split_cols(w_in, sizes)
    pad = jnp.zeros((D_MODEL, LANES - 40), w_in.dtype)
    w = jnp.concatenate([waq, wbq, _interleave_kv(wbk, wbv, 8), wak, wav, wiq, wik, wiw, pad],
                        axis=1).astype(BF16)
    nq, nk = _norm_matrices()
    tabs = (nq, nk, _q_gain(a_q_norm), _q_gain(b_q_norm), _k_gain(a_k_norm), _k_gain(b_k_norm))
    aq, bq, bkv, bkvt, akv, akvt, iq, ik, iwt, km = _ab_prep(x2, trig, gmix, w, tabs, B, S)
    n_blk = S // MOBA_BLOCK
    kmean = km.reshape(B, n_blk, 8, 128).transpose(0, 2, 1, 3)
    o_a = _dsa_attention(iq, iwt, ik, aq, akv, akvt).reshape(B * S, 512)
    o_b = _moba_attention(bq, bkv, bkvt, kmean).reshape(B * S, 512)
    w_out = w_out.astype(BF16)
    return [o_a, o_b], [w_out[:512], w_out[512:]]


def _mixer_layer1(x2, trig, B, S, gmix, w_in, w_out, q_norm, kcmp_norm, ksel_norm, kwin_norm,
                  pos_k, pos_v, w1_k, w2_k, w1_v, w2_v):
    G = NSA_GROUPS
    sizes = (1024,) + (256,) * 6 + (48,)
    wq, wkc, wvc, wks, wvs, wkw, wvw, wgt = _split_cols(w_in, sizes)
    pad = jnp.zeros((D_MODEL, LANES - 48), w_in.dtype)
    w = jnp.concatenate([wq, _interleave_kv(wks, wvs, G), _interleave_kv(wkw, wvw, G),
                         wkc, wvc, wgt, pad], axis=1).astype(BF16)
    nq, nk = _norm_matrices()
    tabs = (nq, nk, _q_gain(q_norm), _k_gain(ksel_norm), _k_gain(kwin_norm))
    qc, qr, kvs, kvst, kvw, kvwt, kc16, vc16, gates_t = _nsa_prep(x2, trig, gmix, w, tabs, B, S)

    def pos_rows(p):
        return p.astype(F32).reshape(2, NSA_CMP_STRIDE * HEAD_DIM)

    kvc, kvct = _compress(kc16, vc16, pos_rows(pos_k), pos_rows(pos_v),
                          w1_k.astype(BF16), w1_v.astype(BF16), w2_k.astype(BF16), w2_v.astype(BF16),
                          kcmp_norm.astype(F32).reshape(1, HEAD_DIM))
    o = _nsa_attention(qc, qr, kvc, kvct, kvs, kvst, kvw, kvwt, gates_t)
    return [o.reshape(B * S, NSA_HEADS * HEAD_DIM)], [w_out.astype(BF16)]


def _finish_layer(parts, weights, x2, mem, S, g_mem, g_src, w_q, w_kv, w_o, q_norm, k_norm,
                  g_ffn, ffn_w_in, ffn_w_out):
    row = lambda v: v.astype(F32).reshape(1, -1)
    kv = _mem_kv(mem, row(g_src), w_kv.astype(BF16), row(k_norm))
    wg = ffn_w_in[:, :D_FF].astype(BF16)
    wu = ffn_w_in[:, D_FF:].astype(BF16)
    return _post_mixer(parts, weights, x2, row(g_mem), w_q.astype(BF16), row(q_norm), kv, w_o.astype(BF16),
                       row(g_ffn), wg, wu, ffn_w_out.astype(BF16), S)


def kernel(x, mem, positions, norm_mix, norm_mem, norm_mem_src, norm_ffn, ab_w_in, ab_w_out, dsa_q_norm, dsa_k_norm, moba_q_norm, moba_k_norm, nsa_w_in, nsa_w_out, nsa_q_norm, nsa_kcmp_norm, nsa_ksel_norm, nsa_kwin_norm, nsa_cmp_pos_k, nsa_cmp_pos_v, nsa_cmp_w1_k, nsa_cmp_w2_k, nsa_cmp_w1_v, nsa_cmp_w2_v, mem_w_q, mem_w_kv, mem_w_o, mem_q_norm, mem_k_norm, ffn_w_in, ffn_w_out):
    B, S, D = x.shape
    depth = norm_mix.shape[0]
    x2 = x.reshape(B * S, D)
    trig = _rope_trig(positions.astype(F32).reshape(B * S, 1), _rope_freq_row(64, 16), _rope_freq_row(32, 8))
    row = lambda v: v.astype(F32).reshape(1, -1)
    for i in range(depth):
        j = i // 2
        if i % 2 == 0:
            parts, weights = _mixer_layer0(x2, trig, B, S, row(norm_mix[i]), ab_w_in[j], ab_w_out[j],
                               dsa_q_norm[j], dsa_k_norm[j], moba_q_norm[j], moba_k_norm[j])
        else:
            parts, weights = _mixer_layer1(x2, trig, B, S, row(norm_mix[i]), nsa_w_in[j], nsa_w_out[j],
                               nsa_q_norm[j], nsa_kcmp_norm[j], nsa_ksel_norm[j], nsa_kwin_norm[j],
                               nsa_cmp_pos_k[j], nsa_cmp_pos_v[j], nsa_cmp_w1_k[j], nsa_cmp_w2_k[j],
                               nsa_cmp_w1_v[j], nsa_cmp_w2_v[j])
        x2 = _finish_layer(parts, weights, x2, mem, S, norm_mem[i], norm_mem_src[i], mem_w_q[i], mem_w_kv[i],
                           mem_w_o[i], mem_q_norm[i], mem_k_norm[i], norm_ffn[i], ffn_w_in[i], ffn_w_out[i])
    return x2.reshape(B, S, D)
```

```python
import functools
import math

import jax
import jax.numpy as jnp
from jax import lax
from jax.experimental import pallas as pl
from jax.experimental.pallas import tpu as pltpu

F32 = jnp.float32
BF16 = jnp.bfloat16
I32 = jnp.int32
I16 = jnp.int16

D_MODEL = 1024
N_MEM = 256
HEAD_DIM = 64
ROPE_THETA = 500000.0
RMS_EPS = 1e-6
NEG_INF = -1e30
TINY = 1e-20

DSA_HEADS = 8
DSA_IDX_HEADS = 8
DSA_IDX_DIM = 32
DSA_TOPK = 256
MOBA_HEADS = 8
MOBA_BLOCK = 256
MOBA_TOPK = 3
NSA_HEADS = 16
NSA_GROUPS = 4
NSA_CMP_LEN = 32
NSA_CMP_STRIDE = 16
NSA_SEL_LEN = 64
NSA_SEL_TOPK = 16
NSA_WINDOW = 512
NSA_FORCE = 1e4
MEM_HEADS = 4
MEM_HEAD_DIM = 128
D_FF = ((8 * D_MODEL + 3 * 256 - 1) // (3 * 256)) * 256

LANES = 128
SUBLANES = 8
INT_MIN = -(2 ** 31)
VMEM_LIMIT = 60 * 1024 * 1024

PV_HEAD_ROWS = 16
PV_ROWS = PV_HEAD_ROWS + HEAD_DIM
PROJ_GROUP = 4
ATT_T = 256
MASK_BIAS = -1e30
M_FLOOR = -1e29
LOG2E = math.log2(math.e)
Q_SCALE = HEAD_DIM ** -0.5 * LOG2E

NT_DIMS = (((1,), (1,)), ((), ()))


def _dot(a, b):
    return jnp.dot(a, b, preferred_element_type=F32)


def _dot_nt(a, b):
    return lax.dot_general(a, b, NT_DIMS, preferred_element_type=F32)


def _split_bf16(a):
    hi = a.astype(BF16)
    return hi, (a - hi.astype(F32)).astype(BF16)


def _split_dot(a, b):
    hi, lo = _split_bf16(a)
    return _dot(hi, b) + _dot(lo, b)


def _rms_rows(x, gain):
    ms = jnp.mean(x * x, axis=-1, keepdims=True)
    return x * lax.rsqrt(ms + RMS_EPS) * gain


def _params(sem):
    return pltpu.CompilerParams(dimension_semantics=sem, vmem_limit_bytes=VMEM_LIMIT)


def _head_norm(y, norm_m, gain):
    ms = _split_dot(y * y, norm_m)
    return y * lax.rsqrt(ms + RMS_EPS) * gain


def _rope(y, c, s, lo_mask, half):
    sw = jnp.where(lo_mask, pltpu.roll(y, LANES - half, 1), pltpu.roll(y, half, 1))
    return y * c + sw * s


def _lane_iota(shape):
    return lax.broadcasted_iota(I32, shape, 1)


def _row_iota(shape):
    return lax.broadcasted_iota(I32, shape, 0)


def _rope_tables(pos, ftab, period, half):
    ang = pos * ftab
    lane = _lane_iota(ang.shape) % period
    c = jnp.cos(ang)
    s = jnp.sin(ang) * jnp.where(lane < half, -1.0, 1.0)
    return c, s


def _pv_operand(kv):
    head = jnp.where(_row_iota((PV_HEAD_ROWS, kv.shape[0])) == 0, 1.0, 0.0)
    return jnp.concatenate([head, kv.T[HEAD_DIM:, :]], axis=0).astype(BF16)


class _ColumnProjector:
    def __init__(self, xn, w_ref):
        self.xn, self.w_ref, self.groups = xn, w_ref, {}

    def __call__(self, j):
        g, u = divmod(j, PROJ_GROUP)
        if g not in self.groups:
            width = PROJ_GROUP * LANES
            lo = g * width
            hi = min(lo + width, self.w_ref.shape[1])
            self.groups[g] = _dot(self.xn, self.w_ref[:, lo:hi])
        return self.groups[g][:, u * LANES:(u + 1) * LANES]


def _kv_column(yc, nk, gain, c64k, s64k, lo64, first64):
    kn = jnp.where(first64, _head_norm(yc, nk, gain), yc)
    return _rope(kn, c64k, s64k, lo64, 8)


def _rope_trig_kernel(pos_ref, f64_ref, f32_ref, o_ref):
    pos = pos_ref[...]
    c64, s64 = _rope_tables(pos, f64_ref[...], 64, 8)
    c32, s32 = _rope_tables(pos, f32_ref[...], 32, 4)
    o_ref[...] = jnp.concatenate([c64, s64, c32, s32], axis=1)


def _rope_trig(pos2, f64, f32t, tm=1024):
    T = pos2.shape[0]
    return pl.pallas_call(
        _rope_trig_kernel, grid=(T // tm,),
        in_specs=[pl.BlockSpec((tm, 1), lambda i: (i, 0)),
                  pl.BlockSpec(f64.shape, lambda i: (0, 0)), pl.BlockSpec(f32t.shape, lambda i: (0, 0))],
        out_specs=pl.BlockSpec((tm, 4 * LANES), lambda i: (i, 0)),
        out_shape=jax.ShapeDtypeStruct((T, 4 * LANES), F32),
        compiler_params=_params(("parallel",)), name="rope_trig",
    )(pos2, f64, f32t)


def _ab_prep_kernel(x_ref, trig_ref, gmix_ref, w_ref, nq_ref, nk_ref,
                    gaq_ref, gbq_ref, gak_ref, gbk_ref,
                    aq_ref, bq_ref, bkv_ref, bkvt_ref, akv_ref, akvt_ref, iq_ref, ik_ref, iwt_ref, km_ref,
                    *, n_tiles):
    xn = _rms_rows(x_ref[...], gmix_ref[...]).astype(BF16)
    c64, s64, c32, s32 = [trig_ref[:, j * LANES:(j + 1) * LANES] for j in range(4)]
    lane = _lane_iota(c64.shape)
    lo64 = (lane % 64) < 8
    lo32 = (lane % 32) < 4
    first64 = lane < 64
    c64k = jnp.where(first64, c64, 1.0)
    s64k = jnp.where(first64, s64, 0.0)
    first32 = lane < 32
    c32k = jnp.where(first32, c32, 1.0)
    s32k = jnp.where(first32, s32, 0.0)
    nq = nq_ref[...]
    nk = nk_ref[...]

    col = _ColumnProjector(xn, w_ref)

    for j in range(4):
        q = _rope(_head_norm(col(j), nq, gaq_ref[...]), c64, s64, lo64, 8)
        aq_ref[2 * j] = q[:, :64].astype(BF16)
        aq_ref[2 * j + 1] = q[:, 64:].astype(BF16)
    for j in range(4):
        q = _rope(_head_norm(col(4 + j), nq, gbq_ref[...]), c64, s64, lo64, 8)
        bq_ref[2 * j] = q[:, :64].astype(BF16)
        bq_ref[2 * j + 1] = q[:, 64:].astype(BF16)
    blk_onehot = jnp.where(lane == HEAD_DIM + pl.program_id(0) % n_tiles, 1.0, 0.0)
    for h in range(8):
        kv = _kv_column(col(8 + h), nk, gbk_ref[...], c64k, s64k, lo64, first64)
        bkv_ref[h] = jnp.where(first64, kv, blk_onehot).astype(BF16)
        bkvt_ref[h] = _pv_operand(kv)
        km_ref[h:h + 1, :] = jnp.mean(kv, axis=0, keepdims=True)
    kv = _kv_column(col(16), nk, gak_ref[...], c64k, s64k, lo64, first64)
    akv_ref[...] = kv.astype(BF16)
    akvt_ref[...] = _pv_operand(kv)
    for j in range(2):
        q = _rope(col(17 + j), c32, s32, lo32, 4)
        for u in range(4):
            iq_ref[4 * j + u] = q[:, 32 * u:32 * (u + 1)].astype(BF16)
    yc = col(19)
    ik_ref[...] = _rope(yc, c32k, s32k, lo32, 4)[:, :32].astype(BF16)
    iwt_ref[...] = yc.T[32:40, :]


def _ab_prep(x2, trig, gmix, w, tabs, B, S):
    T = x2.shape[0]
    tm = ATT_T
    nt = S // tm
    n_cols = w.shape[1]
    nq, nk, gaq, gbq, gak, gbk = tabs

    def full(a):
        return pl.BlockSpec(a.shape, lambda i: (0,) * a.ndim)

    def hm(width, heads=8):
        return pl.BlockSpec((None, heads, tm, width), lambda i: (i // nt, 0, i % nt, 0))

    def tokm(width):
        return pl.BlockSpec((None, tm, width), lambda i: (i // nt, i % nt, 0))

    out_shape = (
        jax.ShapeDtypeStruct((B, 8, S, 64), BF16),
        jax.ShapeDtypeStruct((B, 8, S, 64), BF16),
        jax.ShapeDtypeStruct((B, 8, S, 128), BF16),
        jax.ShapeDtypeStruct((B, 8, nt, PV_ROWS, tm), BF16),
        jax.ShapeDtypeStruct((B, S, 128), BF16),
        jax.ShapeDtypeStruct((B, nt, PV_ROWS, tm), BF16),
        jax.ShapeDtypeStruct((B, 8, S, 32), BF16),
        jax.ShapeDtypeStruct((B, S, 32), BF16),
        jax.ShapeDtypeStruct((B, 8, S), F32),
        jax.ShapeDtypeStruct((T // tm, 8, 128), F32),
    )
    out_specs = (hm(64), hm(64), hm(128),
                 pl.BlockSpec((None, 8, None, PV_ROWS, tm), lambda i: (i // nt, 0, i % nt, 0, 0)),
                 tokm(128),
                 pl.BlockSpec((None, None, PV_ROWS, tm), lambda i: (i // nt, i % nt, 0, 0)),
                 hm(32), tokm(32),
                 pl.BlockSpec((None, 8, tm), lambda i: (i // nt, 0, i % nt)),
                 pl.BlockSpec((None, 8, 128), lambda i: (i, 0, 0)))
    in_specs = [pl.BlockSpec((tm, D_MODEL), lambda i: (i, 0)),
                pl.BlockSpec((tm, 4 * LANES), lambda i: (i, 0)),
                full(gmix), pl.BlockSpec((D_MODEL, n_cols), lambda i: (0, 0)),
                full(nq), full(nk), full(gaq), full(gbq), full(gak), full(gbk)]
    return pl.pallas_call(
        functools.partial(_ab_prep_kernel, n_tiles=nt), grid=(T // tm,), in_specs=in_specs, out_specs=out_specs,
        out_shape=out_shape, compiler_params=_params(("parallel",)), name="ab_prep",
    )(x2, trig, gmix, w, nq, nk, gaq, gbq, gak, gbk)


def _pad_q(q):
    return jnp.concatenate([q, jnp.zeros_like(q)], axis=1)


def _bias_lanes(q, rows):
    n, tq = rows.shape
    parts = [jnp.zeros((HEAD_DIM, tq), F32), rows]
    if n < HEAD_DIM:
        parts.append(jnp.zeros((HEAD_DIM - n, tq), F32))
    lanes = jnp.concatenate(parts, axis=0).T.astype(BF16)
    return jnp.where(_lane_iota(q.shape) < HEAD_DIM, q, lanes)


class _Flash:
    def __init__(self, m_ref, acc_ref, s_ref, cmax_ref, p_ref, tq):
        self.m_ref, self.acc_ref, self.tq = m_ref, acc_ref, tq
        self.s_ref, self.cmax_ref, self.p_ref = s_ref, cmax_ref, p_ref

    def reset(self):
        self.m_ref[...] = jnp.full(self.m_ref.shape, M_FLOOR, F32)
        self.acc_ref[...] = jnp.zeros(self.acc_ref.shape, F32)

    def _scores(self, buf, qs, kvs, biases):
        tq = self.tq
        for i in range(len(qs)):
            s = _dot_nt(kvs[i], qs[i])
            if biases[i] is not None:
                s = s + biases[i]
            self.s_ref[buf, i, :s.shape[0], :] = s
            self.cmax_ref[buf, :, i * tq:(i + 1) * tq] = jnp.max(s, axis=0, keepdims=True)

    def update(self, qs, kvs, kvts, biases):
        self._scores(0, qs, kvs, biases)
        self._finish(0, kvts)

    def run(self, qs, count, operands):
        def scores(c, buf):
            kvs, _, biases = operands(c)
            self._scores(buf, qs, kvs, biases)

        def finish(c, buf):
            self._finish(buf, operands(c)[1])

        last = jnp.maximum(count - 1, 0)
        scores(0, 0)

        def two_chunks(pp, carry):
            c = 2 * pp
            scores(c + 1, 1)
            finish(c, 0)
            scores(jnp.minimum(c + 2, last), 0)
            finish(c + 1, 1)
            return carry

        lax.fori_loop(0, count // 2, two_chunks, 0)

        @pl.when(count % 2 == 1)
        def _():
            finish(count - 1, 0)

    def _finish(self, buf, kvts):
        n = len(kvts)
        tq = self.tq
        kc = sum(kvt.shape[1] for kvt in kvts[0])
        alphas = []
        for i in range(n):
            cols = slice(i * tq, (i + 1) * tq)
            m = self.m_ref[:, cols]
            m_new = jnp.maximum(m, self.cmax_ref[buf, :, cols])
            p = jnp.exp2(self.s_ref[buf, i, :kc, :] - m_new)
            alpha = jnp.exp2(m - m_new)
            self.m_ref[:, cols] = m_new
            self.p_ref[i, :kc, :] = p.astype(BF16)
            alphas.append(alpha)
        for i in range(n):
            cols = slice(i * tq, (i + 1) * tq)
            pv, r0 = None, 0
            for kvt in kvts[i]:
                part = _dot(kvt, self.p_ref[i, r0:r0 + kvt.shape[1], :])
                pv = part if pv is None else pv + part
                r0 += kvt.shape[1]
            self.acc_ref[:, cols] = alphas[i] * self.acc_ref[:, cols] + pv

    def result(self, slot):
        cols = slice(slot * self.tq, (slot + 1) * self.tq)
        acc = self.acc_ref[:, cols]
        return acc / jnp.maximum(acc[0:1, :], TINY)


def _flash_scratch(n_slots, tq, kc):
    return [pltpu.VMEM((1, n_slots * tq), F32), pltpu.VMEM((PV_ROWS, n_slots * tq), F32),
            pltpu.VMEM((2, n_slots, kc, tq), F32), pltpu.VMEM((2, 1, n_slots * tq), F32),
            pltpu.VMEM((n_slots, kc, tq), BF16)]


def _softmax_direct(qs, kv, kvts, bias):
    scores = [_dot_nt(kv, q) for q in qs]
    probs = []
    for s in scores:
        s = s + bias
        m = jnp.maximum(jnp.max(s, axis=0, keepdims=True), M_FLOOR)
        probs.append(jnp.exp2(s - m))
    inv_ls, outs = [], []
    for p in probs:
        pb = p.astype(BF16)
        o, r0 = None, 0
        for kvt in kvts:
            part = _dot(kvt, pb[r0:r0 + kvt.shape[1]])
            o = part if o is None else o + part
            r0 += kvt.shape[1]
        outs.append(o)
        inv_ls.append(1.0 / jnp.maximum(o[0:1, :], TINY))
    return probs, inv_ls, outs


def _causal_bias(t):
    return jnp.where(_row_iota((t, t)) <= _lane_iota((t, t)), 0.0, MASK_BIAS)


def _store_heads(o_ref, heads_t):
    for u in range(len(heads_t) // 2):
        pair = jnp.concatenate([heads_t[2 * u][PV_HEAD_ROWS:, :], heads_t[2 * u + 1][PV_HEAD_ROWS:, :]], axis=0)
        o_ref[:, u * LANES:(u + 1) * LANES] = pair.T.astype(o_ref.dtype)


def _rank_select_t(v, n_valid, n_top):
    n = v.shape[0]
    row = _row_iota(v.shape)
    rank = jnp.zeros(v.shape, F32)
    for m in range(n):
        vm = v[m:m + 1, :]
        ahead = (vm > v) | ((vm == v) & (m < row))
        if n_valid is not None:
            ahead = ahead & (m < n_valid)
        rank = rank + jnp.where(ahead, 1.0, 0.0)
    sel = rank < n_top
    if n_valid is not None:
        sel = sel & (row < n_valid)
    return jnp.where(sel, 1.0, 0.0)


def _dsa_kernel(iq_ref, iwt_ref, ik_ref, aq_ref, akv_ref, akvt_ref, o_ref,
                sk_ref, half_ref, bias_ref, xcut_ref, *flash_refs, k_top, index_bits):
    t = ATT_T
    i = pl.program_id(1)
    n_ch = i + 1
    kio = _row_iota((t, t))
    qio = _lane_iota((t, t))

    def causal(c):
        return (c - i) * t + kio <= qio

    def score_chunk(c):
        k0 = pl.multiple_of(c * t, t)
        ikc = ik_ref[pl.ds(k0, t), :]
        sc = jnp.zeros((t, t), F32)
        for h in range(DSA_IDX_HEADS):
            logit = _dot_nt(ikc, iq_ref[h])
            sc = sc + iwt_ref[h:h + 1, :] * jnp.maximum(logit, 0.0)
        sc = jnp.where(sc == 0.0, 0.0, sc)
        bits = pltpu.bitcast(sc, I32)
        key = bits ^ ((bits >> 31) & 0x7FFFFFFF)
        key = jnp.where(causal(c), key, INT_MIN)
        sk_ref[c] = key
        half_ref[c] = (key >> 16).astype(I16)

    def score_pair(cc, carry):
        score_chunk(2 * cc)
        score_chunk(2 * cc + 1)
        return carry

    lax.fori_loop(0, (n_ch + 1) // 2, score_pair, 0)

    def count(pred):
        def body(c, acc8):
            ind = jnp.where(pred(sk_ref[c], c), 1.0, 0.0)
            return acc8 + ind.reshape(-1, SUBLANES, t).sum(axis=0)
        acc8 = lax.fori_loop(0, n_ch, body, jnp.zeros((SUBLANES, t), F32))
        return jnp.sum(acc8, axis=0, keepdims=True)

    def count_half(cand):
        rows = 2 * SUBLANES

        def body(cc, acc):
            parts = []
            for c in (2 * cc, 2 * cc + 1):
                ind = jnp.where(half_ref[c] >= cand, jnp.bfloat16(1), jnp.bfloat16(0))
                parts += [ind[rows * j:rows * (j + 1), :] for j in range(t // rows)]
            while len(parts) > 1:
                parts = [parts[2 * j] + parts[2 * j + 1] for j in range(len(parts) // 2)]
            return acc + parts[0].astype(F32)
        acc = lax.fori_loop(0, (n_ch + 1) // 2, body, jnp.zeros((rows, t), F32))
        return jnp.sum(acc, axis=0, keepdims=True)

    def half_search():
        def bit_step(b, v):
            cand = v + lax.shift_left(jnp.int32(1), 15 - b)
            return jnp.where(count_half(cand.astype(I16)) >= k_top, cand, v)
        return lax.fori_loop(0, 16, bit_step, jnp.full((1, t), -(2 ** 15), I32))

    thr_hi = half_search()

    def low_half_chunk(c, carry):
        key = sk_ref[c]
        hi = key >> 16
        lo = (key & 0xFFFF) - 2 ** 15
        half_ref[c] = jnp.where(hi > thr_hi, 2 ** 15 - 1, jnp.where(hi < thr_hi, -(2 ** 15), lo)).astype(I16)
        return carry

    lax.fori_loop(0, n_ch, low_half_chunk, 0)
    thr = lax.shift_left(thr_hi, 16) + (half_search() + 2 ** 15)

    need = k_top - count(lambda blk, c: blk > thr)
    n_ge = count(lambda blk, c: blk >= thr)
    xcut_ref[...] = jnp.full((1, t), 2 ** 30, I32)

    @pl.when(jnp.max(n_ge) > k_top)
    def _():
        def x_step(b, x):
            cand = x + lax.shift_left(jnp.int32(1), index_bits - 1 - b)
            ties_below = count(lambda blk, c: (blk == thr) & (c * t + kio < cand))
            return jnp.where(ties_below <= need, cand, x)
        xcut_ref[...] = lax.fori_loop(0, index_bits, x_step, jnp.zeros((1, t), I32))

    xcut = xcut_ref[...]

    n_pairs = (n_ch + 1) // 2

    def bias_chunk(c, carry):
        blk = sk_ref[jnp.minimum(c, i)]
        keep = (blk > thr) | ((blk == thr) & (c * t + kio < xcut))
        bias_ref[c] = jnp.where(keep & causal(c), 0.0, MASK_BIAS)
        return carry

    lax.fori_loop(0, 2 * n_pairs, bias_chunk, 0)

    flash = _Flash(*flash_refs, t)
    qs = [_pad_q(aq_ref[h]) for h in range(DSA_HEADS)]
    flash.reset()

    n = DSA_HEADS

    def att_pair(cc):
        c0 = 2 * cc
        k0 = pl.multiple_of(c0 * t, 2 * t)
        kv = akv_ref[pl.ds(k0, 2 * t), :]
        bias = jnp.concatenate([bias_ref[c0], bias_ref[c0 + 1]], axis=0)
        return [kv] * n, [[akvt_ref[c0], akvt_ref[c0 + 1]]] * n, [bias] * n

    flash.run(qs, n_pairs, att_pair)
    _store_heads(o_ref, [flash.result(h) for h in range(DSA_HEADS)])


def _dsa_attention(iq, iwt, ik, aq, akv, akvt):
    B, _, S, _ = aq.shape
    t = ATT_T
    nt = S // t
    k_top = min(DSA_TOPK, S // 4)
    in_specs = [
        pl.BlockSpec((None, 8, t, 32), lambda b, i: (b, 0, i, 0)),
        pl.BlockSpec((None, 8, t), lambda b, i: (b, 0, i)),
        pl.BlockSpec((None, S, 32), lambda b, i: (b, 0, 0)),
        pl.BlockSpec((None, 8, t, 64), lambda b, i: (b, 0, i, 0)),
        pl.BlockSpec((None, S, 128), lambda b, i: (b, 0, 0)),
        pl.BlockSpec((None, nt, PV_ROWS, t), lambda b, i: (b, 0, 0, 0)),
    ]
    return pl.pallas_call(
        functools.partial(_dsa_kernel, k_top=k_top, index_bits=S.bit_length()),
        grid=(B, nt), in_specs=in_specs,
        out_specs=pl.BlockSpec((None, t, 512), lambda b, i: (b, i, 0)),
        out_shape=jax.ShapeDtypeStruct((B, S, 512), BF16),
        scratch_shapes=[pltpu.VMEM((nt, t, t), I32), pltpu.VMEM((nt, t, t), I16), pltpu.VMEM((nt, t, t), F32),
                        pltpu.VMEM((1, t), I32)] + _flash_scratch(DSA_HEADS, t, 2 * t),
        compiler_params=_params(("parallel", "parallel")), name="dsa_attention",
    )(iq, iwt, ik, aq, akv, akvt)


MOBA_HPS = 8


def _moba_kernel(q_ref, kv_ref, kvt_ref, km_ref, o_ref, *flash_refs, n_top):
    t = ATT_T
    own = pl.program_id(2)
    causal = _causal_bias(t)
    flash = _Flash(*flash_refs, t)
    qs = []
    for hh in range(MOBA_HPS):
        q = _pad_q(q_ref[hh])
        km_hi, km_lo = _split_bf16(km_ref[hh])
        gate = _dot_nt(km_hi, q) + _dot_nt(km_lo, q)
        keep = _rank_select_t(gate, own, n_top)
        keep = jnp.where(_row_iota(keep.shape) == own, 1.0, keep)
        qs.append(_bias_lanes(q, (keep - 1.0) * (-MASK_BIAS)))
    flash.reset()

    def operands(cc):
        n0 = 2 * cc
        k0 = pl.multiple_of(n0 * t, 2 * t)
        heads = range(MOBA_HPS)
        return (n0, [kv_ref[hh, pl.ds(k0, 2 * t), :] for hh in heads],
                [[kvt_ref[hh, n0], kvt_ref[hh, n0 + 1]] for hh in heads])

    def past_pair(cc):
        _, kvs, kvts = operands(cc)
        return kvs, kvts, [None] * MOBA_HPS

    flash.run(qs, own // 2, past_pair)
    heads = range(MOBA_HPS)

    @pl.when(own % 2 == 0)
    def _():
        k0 = pl.multiple_of(own * t, t)
        flash.update(qs, [kv_ref[hh, pl.ds(k0, t), :] for hh in heads],
                     [[kvt_ref[hh, own]] for hh in heads], [causal] * MOBA_HPS)

    @pl.when(own % 2 == 1)
    def _():
        _, kvs, kvts = operands(own // 2)
        bias = jnp.concatenate([jnp.zeros((t, t), F32), causal], axis=0)
        flash.update(qs, kvs, kvts, [bias] * MOBA_HPS)

    _store_heads(o_ref, [flash.result(hh) for hh in range(MOBA_HPS)])


def _moba_attention(bq, bkv, bkvt, kmean):
    B, H, S, _ = bq.shape
    t = ATT_T
    hps = MOBA_HPS
    n_blk = S // MOBA_BLOCK
    assert t == MOBA_BLOCK and n_blk % 2 == 0 and H % hps == 0
    n_top = max(1, min(MOBA_TOPK, n_blk - 1))
    in_specs = [
        pl.BlockSpec((None, hps, t, 64), lambda b, h, i: (b, h, i, 0)),
        pl.BlockSpec((None, hps, S, 128), lambda b, h, i: (b, h, 0, 0)),
        pl.BlockSpec((None, hps, n_blk, PV_ROWS, t), lambda b, h, i: (b, h, 0, 0, 0)),
        pl.BlockSpec((None, hps, n_blk, 128), lambda b, h, i: (b, h, 0, 0)),
    ]
    return pl.pallas_call(
        functools.partial(_moba_kernel, n_top=n_top), grid=(B, H // hps, S // t), in_specs=in_specs,
        out_specs=pl.BlockSpec((None, t, hps * 64), lambda b, h, i: (b, i, h)),
        out_shape=jax.ShapeDtypeStruct((B, S, H * 64), BF16),
        scratch_shapes=_flash_scratch(hps, t, 2 * t),
        compiler_params=_params(("parallel", "parallel", "parallel")), name="moba_attention",
    )(bq, bkv, bkvt, kmean)


def _lane_group_norm(y, gain, width):
    outs = []
    for j in range(y.shape[1] // width):
        yc = y[:, j * width:(j + 1) * width]
        outs.append(_rms_rows(yc, gain))
    return jnp.concatenate(outs, axis=1)


def _mem_kv_kernel(m_ref, g_ref, w_ref, gk_ref, o_ref):
    mn = _rms_rows(m_ref[...], g_ref[...]).astype(BF16)
    y = _dot(mn, w_ref[...])
    hw = MEM_HEADS * MEM_HEAD_DIM
    k = _lane_group_norm(y[:, :hw], gk_ref[...], MEM_HEAD_DIM)
    o_ref[...] = jnp.concatenate([k, y[:, hw:]], axis=1).astype(BF16)


def _mem_kv(mem, g, w, gk):
    B, M, _ = mem.shape
    n = w.shape[1]
    return pl.pallas_call(
        _mem_kv_kernel, grid=(B,),
        in_specs=[pl.BlockSpec((None, M, D_MODEL), lambda b: (b, 0, 0)),
                  pl.BlockSpec(g.shape, lambda b: (0, 0)),
                  pl.BlockSpec(w.shape, lambda b: (0, 0)),
                  pl.BlockSpec(gk.shape, lambda b: (0, 0))],
        out_specs=pl.BlockSpec((None, M, n), lambda b: (b, 0, 0)),
        out_shape=jax.ShapeDtypeStruct((B, M, n), BF16),
        compiler_params=_params(("parallel",)), name="mem_kv",
    )(mem, g, w, gk)


def _mem_attend(x, g_ref, wq_ref, gq_ref, kv_ref, wo_ref):
    xn = _rms_rows(x, g_ref[...]).astype(BF16)
    q = _lane_group_norm(_dot(xn, wq_ref[...]), gq_ref[...], MEM_HEAD_DIM).astype(BF16)
    hw = MEM_HEADS * MEM_HEAD_DIM
    scale = MEM_HEAD_DIM ** -0.5
    outs = []
    for h in range(MEM_HEADS):
        cols = slice(h * MEM_HEAD_DIM, (h + 1) * MEM_HEAD_DIM)
        k = kv_ref[:, cols]
        v = kv_ref[:, hw + h * MEM_HEAD_DIM:hw + (h + 1) * MEM_HEAD_DIM]
        s = _dot_nt(q[:, cols], k) * scale
        p = jnp.exp(s - jnp.max(s, axis=-1, keepdims=True))
        p = p / jnp.sum(p, axis=-1, keepdims=True)
        outs.append(_dot(p.astype(BF16), v))
    o = jnp.concatenate(outs, axis=1).astype(BF16)
    return x + _dot(o, wo_ref[...])


def _post_mixer_kernel(*refs, n_in):
    a_refs = refs[:n_in]
    w_refs = refs[n_in:2 * n_in]
    (x_ref, gm_ref, wq_ref, gq_ref, kv_ref, wo_ref, gf_ref, wg_ref, wu_ref, wd_ref,
     o_ref, xn_ref, acc_ref) = refs[2 * n_in:]
    j = pl.program_id(1)

    @pl.when(j == 0)
    def _():
        x = x_ref[...]
        for a_ref, w_ref in zip(a_refs, w_refs):
            x = x + _dot(a_ref[...], w_ref[...])
        x = _mem_attend(x, gm_ref, wq_ref, gq_ref, kv_ref, wo_ref)
        xn_ref[...] = _rms_rows(x, gf_ref[...]).astype(BF16)
        acc_ref[...] = x

    xn = xn_ref[...]
    gate = _dot(xn, wg_ref[...])
    up = _dot(xn, wu_ref[...])
    act = (gate * jax.nn.sigmoid(gate) * up).astype(BF16)
    acc_ref[...] += _dot(act, wd_ref[...])

    @pl.when(j == pl.num_programs(1) - 1)
    def _():
        o_ref[...] = acc_ref[...]


def _post_mixer(parts, weights, x2, g_mem, wq, gq, kv, wo, g_ffn, wg, wu, wd, S, tm=512, n_split=2):
    T = x2.shape[0]
    nt = S // tm
    tf = D_FF // n_split
    n_in = len(parts)
    M, n = kv.shape[1], kv.shape[2]

    def const(a):
        return pl.BlockSpec(a.shape, lambda i, j: (0,) * a.ndim)

    in_specs = ([pl.BlockSpec((tm, p.shape[1]), lambda i, j: (i, 0)) for p in parts]
                + [const(w) for w in weights]
                + [pl.BlockSpec((tm, D_MODEL), lambda i, j: (i, 0)),
                   const(g_mem), const(wq), const(gq),
                   pl.BlockSpec((None, M, n), lambda i, j: (i // nt, 0, 0)),
                   const(wo), const(g_ffn),
                   pl.BlockSpec((D_MODEL, tf), lambda i, j: (0, j)),
                   pl.BlockSpec((D_MODEL, tf), lambda i, j: (0, j)),
                   pl.BlockSpec((tf, D_MODEL), lambda i, j: (j, 0))])
    return pl.pallas_call(
        functools.partial(_post_mixer_kernel, n_in=n_in), grid=(T // tm, n_split), in_specs=in_specs,
        out_specs=pl.BlockSpec((tm, D_MODEL), lambda i, j: (i, 0)),
        out_shape=jax.ShapeDtypeStruct((T, D_MODEL), F32),
        scratch_shapes=[pltpu.VMEM((tm, D_MODEL), BF16), pltpu.VMEM((tm, D_MODEL), F32)],
        compiler_params=_params(("parallel", "arbitrary")), name="post_mixer",
    )(*parts, *weights, x2, g_mem, wq, gq, kv, wo, g_ffn, wg, wu, wd)


def _nsa_prep_kernel(x_ref, trig_ref, gmix_ref, w_ref, nq_ref, nk_ref,
                     gq_ref, gks_ref, gkw_ref,
                     qc_ref, qr_ref, kvs_ref, kvst_ref, kvw_ref, kvwt_ref, kc_ref, vc_ref, gtt_ref,
                     stage_ref, *, n_tiles):
    xn = _rms_rows(x_ref[...], gmix_ref[...]).astype(BF16)
    c64, s64 = trig_ref[:, :LANES], trig_ref[:, LANES:]
    lane = _lane_iota(c64.shape)
    lo64 = (lane % 64) < 8
    first64 = lane < 64
    c64k = jnp.where(first64, c64, 1.0)
    s64k = jnp.where(first64, s64, 0.0)
    nq = nq_ref[...]
    nk = nk_ref[...]

    col = _ColumnProjector(xn, w_ref)

    for j in range(8):
        qn = _head_norm(col(j), nq, gq_ref[...])
        qr = _rope(qn, c64, s64, lo64, 8)
        qc_ref[2 * j] = qn[:, :64].astype(BF16)
        qc_ref[2 * j + 1] = qn[:, 64:].astype(BF16)
        qr_ref[2 * j] = qr[:, :64].astype(BF16)
        qr_ref[2 * j + 1] = qr[:, 64:].astype(BF16)
    tile = pl.program_id(0) % n_tiles
    sel_blk = tile * (ATT_T // NSA_SEL_LEN) + lax.shift_right_logical(
        _row_iota(c64.shape), NSA_SEL_LEN.bit_length() - 1)
    blk_onehot = jnp.where(lane == HEAD_DIM + sel_blk, 1.0, 0.0)
    for g in range(NSA_GROUPS):
        kv = _kv_column(col(8 + g), nk, gks_ref[...], c64k, s64k, lo64, first64)
        kvs_ref[g] = jnp.where(first64, kv, blk_onehot).astype(BF16)
        kvst_ref[g] = _pv_operand(kv)
        kv = _kv_column(col(12 + g), nk, gkw_ref[...], c64k, s64k, lo64, first64)
        kvw_ref[g] = kv.astype(BF16)
        kvwt_ref[g] = _pv_operand(kv)
    stride = NSA_CMP_STRIDE
    rows = stage_ref.shape[0] // stride
    for out_ref, first in ((kc_ref, 16), (vc_ref, 18)):
        for c in range(2):
            stage_ref[...] = col(first + c)
            for u in range(0, stride, 2):
                pair = [stage_ref[pl.ds(u + v, rows, stride=stride), :] for v in range(2)]
                for h in range(2):
                    halves = [p[:, h * HEAD_DIM:(h + 1) * HEAD_DIM] for p in pair]
                    out_ref[2 * c + h, :, u * HEAD_DIM:(u + 2) * HEAD_DIM] = jnp.concatenate(halves, axis=1)
    gates_t = jax.nn.sigmoid(col(20)).T
    for g in range(NSA_GROUPS):
        gtt_ref[g] = gates_t[12 * g:12 * (g + 1), :]


def _nsa_prep(x2, trig, gmix, w, tabs, B, S):
    T = x2.shape[0]
    tm = ATT_T
    nt = S // tm
    nq, nk, gq, gks, gkw = tabs

    def full(a):
        return pl.BlockSpec(a.shape, lambda i: (0,) * a.ndim)

    def hm(width, heads):
        return pl.BlockSpec((None, heads, tm, width), lambda i: (i // nt, 0, i % nt, 0))

    def hmt(heads):
        return pl.BlockSpec((None, heads, None, PV_ROWS, tm), lambda i: (i // nt, 0, i % nt, 0, 0))

    def tokm(width):
        return pl.BlockSpec((None, tm, width), lambda i: (i // nt, i % nt, 0))

    out_shape = (
        jax.ShapeDtypeStruct((B, 16, S, 64), BF16),
        jax.ShapeDtypeStruct((B, 16, S, 64), BF16),
        jax.ShapeDtypeStruct((B, 4, S, 128), BF16),
        jax.ShapeDtypeStruct((B, 4, nt, PV_ROWS, tm), BF16),
        jax.ShapeDtypeStruct((B, 4, S, 128), BF16),
        jax.ShapeDtypeStruct((B, 4, nt, PV_ROWS, tm), BF16),
        jax.ShapeDtypeStruct((B, 4, S // 16, 1024), F32),
        jax.ShapeDtypeStruct((B, 4, S // 16, 1024), F32),
        jax.ShapeDtypeStruct((B, 4, 12, S), F32),
    )
    rows16 = pl.BlockSpec((None, 4, tm // 16, 1024), lambda i: (i // nt, 0, i % nt, 0))
    out_specs = (hm(64, 16), hm(64, 16), hm(128, 4), hmt(4), hm(128, 4), hmt(4), rows16, rows16,
                 pl.BlockSpec((None, 4, 12, tm), lambda i: (i // nt, 0, 0, i % nt)))
    in_specs = [pl.BlockSpec((tm, D_MODEL), lambda i: (i, 0)),
                pl.BlockSpec((tm, 2 * LANES), lambda i: (i, 0)),
                full(gmix), full(w), full(nq), full(nk), full(gq), full(gks), full(gkw)]
    return pl.pallas_call(
        functools.partial(_nsa_prep_kernel, n_tiles=nt), grid=(T // tm,), in_specs=in_specs, out_specs=out_specs,
        out_shape=out_shape, scratch_shapes=[pltpu.VMEM((tm, LANES), F32)],
        compiler_params=_params(("parallel",)), name="nsa_prep",
    )(x2, trig, gmix, w, nq, nk, gq, gks, gkw)


def _compress_one(x16, pa, pb, w1a, w1b, w2):
    n16 = x16.shape[0]
    h_a = _dot((x16 + pa).astype(BF16), w1a)
    h_b = _dot((x16 + pb).astype(BF16), w1b)
    pre = h_a + pltpu.roll(h_b, n16 - 1, 0)
    act = pre * jax.nn.sigmoid(pre)
    return _dot(act.astype(BF16), w2)


def _compress_kernel(xk_ref, xv_ref, pk_ref, pv_ref, w1k_ref, w1v_ref, w2k_ref, w2v_ref, gk_ref,
                     o_ref, ot_ref):
    half = w1k_ref.shape[0] // 2
    k = _compress_one(xk_ref[...], pk_ref[0:1, :], pk_ref[1:2, :],
                      w1k_ref[:half, :], w1k_ref[half:, :], w2k_ref[...])
    k = _rms_rows(k, gk_ref[...])
    v = _compress_one(xv_ref[...], pv_ref[0:1, :], pv_ref[1:2, :],
                      w1v_ref[:half, :], w1v_ref[half:, :], w2v_ref[...])
    kv = jnp.concatenate([k, v], axis=1)
    o_ref[...] = kv.astype(BF16)
    ot_ref[...] = _pv_operand(kv)


def _compress(xk16, xv16, pk, pv, w1k, w1v, w2k, w2v, gk):
    B, G, n16, width = xk16.shape

    def full(a):
        return pl.BlockSpec(a.shape, lambda b, g: (0,) * a.ndim)

    xspec = pl.BlockSpec((None, None, n16, width), lambda b, g: (b, g, 0, 0))
    return pl.pallas_call(
        _compress_kernel, grid=(B, G),
        in_specs=[xspec, xspec, full(pk), full(pv), full(w1k), full(w1v), full(w2k), full(w2v), full(gk)],
        out_specs=(pl.BlockSpec((None, None, n16, 128), lambda b, g: (b, g, 0, 0)),
                   pl.BlockSpec((None, None, PV_ROWS, n16), lambda b, g: (b, g, 0, 0))),
        out_shape=(jax.ShapeDtypeStruct((B, G, n16, 128), BF16),
                   jax.ShapeDtypeStruct((B, G, PV_ROWS, n16), BF16)),
        compiler_params=_params(("parallel", "parallel")), name="nsa_compress",
    )(xk16, xv16, pk, pv, w1k, w1v, w2k, w2v, gk)


NSA_GPS = 4


def _nsa_tile_masks(i, n16, n_sel, n_cmp, n_top):
    t = ATT_T
    t0 = i * t
    n_id = _row_iota((n16, t))
    q_id = t0 + _lane_iota((n16, t))
    cmp_visible = (n_id < n_cmp) & (n_id * NSA_CMP_STRIDE + (NSA_CMP_LEN - 1) <= q_id)
    b_id = _row_iota((n_sel, n16)) * NSA_SEL_LEN
    r_id = _lane_iota((n_sel, n16)) * NSA_CMP_STRIDE
    cover_t = ((r_id < b_id + NSA_SEL_LEN) & (r_id + NSA_CMP_LEN > b_id)
               & (_lane_iota((n_sel, n16)) < n_cmp))
    n_wc = NSA_WINDOW // t + 1
    cw = jnp.maximum(i - (n_wc - 1), 0)
    dist = (i - cw) * t + _lane_iota((n_wc * t, t)) - _row_iota((n_wc * t, t))
    blk = _row_iota((n_sel, t))
    cur = lax.shift_right_logical(t0 + _lane_iota((n_sel, t)), NSA_SEL_LEN.bit_length() - 1)
    return dict(
        bias_c=jnp.where(cmp_visible, 0.0, MASK_BIAS),
        cover_t=jnp.where(cover_t, 1.0, 0.0).astype(BF16),
        n_wc=n_wc, cw=cw, bias_w=jnp.where((dist >= 0) & (dist < NSA_WINDOW), 0.0, MASK_BIAS),
        forced=(blk == 0) | (blk == cur) | (blk == cur - 1), visible_blk=blk <= cur,
        n_wanted=jnp.minimum(cur[0:1, :] + 1, n_top).astype(F32))


def _nsa_front(qc_ref, qr_ref, kvc_ref, kvct_ref, kvw_ref, kvwt_ref, gtt_ref, sel_ref, part_ref,
               masks, n_top):
    t = ATT_T
    HG = NSA_HEADS // NSA_GROUPS
    n_sel = sel_ref.shape[0]
    n16 = kvc_ref.shape[0]

    p_sum = jnp.zeros((n16, t), F32)
    o_c = []
    probs, inv_ls, outs = _softmax_direct([_pad_q(qc_ref[j]) for j in range(HG)], kvc_ref[...],
                                          [kvct_ref[...]], masks["bias_c"])
    for j in range(HG):
        p_sum = p_sum + probs[j] * inv_ls[j]
        o_c.append(outs[j] * inv_ls[j])

    p_hi, p_lo = _split_bf16(p_sum)
    imp = _dot(masks["cover_t"], p_hi) + _dot(masks["cover_t"], p_lo)

    qs = [_pad_q(qr_ref[j]) for j in range(HG)]
    n_wc, cw = masks["n_wc"], masks["cw"]
    kw0 = pl.multiple_of(cw * t, t)
    _, inv_lw, out_w = _softmax_direct(qs, kvw_ref[pl.ds(kw0, n_wc * t), :],
                                       [kvwt_ref[cw + u] for u in range(n_wc)], masks["bias_w"])
    for j in range(HG):
        part_ref[:, j * t:(j + 1) * t] = (gtt_ref[3 * j:3 * j + 1, :] * o_c[j]
                                          + gtt_ref[3 * j + 2:3 * j + 3, :] * (out_w[j] * inv_lw[j]))

    imp = jnp.where(masks["forced"], NSA_FORCE, imp)
    imp = jnp.where(masks["visible_blk"], imp, NEG_INF)
    n_larger = jnp.zeros((n_sel, t), F32)
    for m in range(n_sel):
        n_larger = n_larger + jnp.where(imp[m:m + 1, :] > imp, 1.0, 0.0)
    sel_fast = n_larger < n_top
    n_picked = jnp.sum(jnp.where(sel_fast & masks["visible_blk"], 1.0, 0.0), axis=0, keepdims=True)
    sel_ref[...] = jnp.where(sel_fast, 0.0, MASK_BIAS)
    return qs, imp, jnp.abs(n_picked - masks["n_wanted"])


def _nsa_kernel(qc_ref, qr_ref, kvc_ref, kvct_ref, kvs_ref, kvst_ref, kvw_ref, kvwt_ref, gtt_ref,
                o_ref, sel_ref, part_ref, *flash_refs, n_cmp, n_top):
    t = ATT_T
    HG = NSA_HEADS // NSA_GROUPS
    n_slots = NSA_GPS * HG
    i = pl.program_id(2)

    masks = _nsa_tile_masks(i, kvc_ref.shape[1], sel_ref.shape[1], n_cmp, n_top)
    qs, imps, miss = [], [], None
    for g in range(NSA_GPS):
        heads_g = pl.ds(g * HG, HG)
        q_g, imp_g, miss_g = _nsa_front(
            qc_ref.at[heads_g], qr_ref.at[heads_g], kvc_ref.at[g], kvct_ref.at[g], kvw_ref.at[g],
            kvwt_ref.at[g], gtt_ref.at[g], sel_ref.at[g], part_ref.at[:, pl.ds(g * HG * t, HG * t)],
            masks, n_top)
        qs += q_g
        imps.append(imp_g)
        miss = miss_g if miss is None else jnp.maximum(miss, miss_g)

    @pl.when(jnp.max(miss) > 0.0)
    def _():
        for g in range(NSA_GPS):
            sel_ref[g] = (_rank_select_t(imps[g], None, n_top) - 1.0) * (-MASK_BIAS)

    qs_sel = [_bias_lanes(qs[s], sel_ref[s // HG]) for s in range(n_slots)]
    flash = _Flash(*flash_refs, t)
    flash.reset()

    def sel_operands(c):
        k0 = pl.multiple_of(c * t, t)
        kvs = [kvs_ref[s // HG, pl.ds(k0, t), :] for s in range(n_slots)]
        kvts = [[kvst_ref[s // HG, c]] for s in range(n_slots)]
        return kvs, kvts

    def past_chunk(c):
        kvs, kvts = sel_operands(c)
        return kvs, kvts, [None] * n_slots

    flash.run(qs_sel, i, past_chunk)
    kvs, kvts = sel_operands(i)
    flash.update(qs_sel, kvs, kvts, [_causal_bias(t)] * n_slots)

    heads = []
    for s in range(n_slots):
        g, j = divmod(s, HG)
        heads.append(part_ref[:, s * t:(s + 1) * t] + gtt_ref[g, 3 * j + 1:3 * j + 2, :] * flash.result(s))
    _store_heads(o_ref, heads)


def _nsa_attention(qc, qr, kvc, kvct, kvs, kvst, kvw, kvwt, gates_t):
    B, H, S, _ = qc.shape
    G = NSA_GROUPS
    HG = H // G
    t = ATT_T
    nt = S // t
    n16 = kvc.shape[2]
    n_cmp = (S - NSA_CMP_LEN) // NSA_CMP_STRIDE + 1
    n_sel = S // NSA_SEL_LEN
    n_top = min(NSA_SEL_TOPK, n_sel)
    gps = NSA_GPS
    assert G % gps == 0 and nt % 2 == 0 and n_sel <= HEAD_DIM and S >= (NSA_WINDOW // t + 1) * t
    qspec = pl.BlockSpec((None, gps * HG, t, 64), lambda b, g, i: (b, g, i, 0))
    once = pl.Buffered(1)
    kvspec = pl.BlockSpec((None, gps, S, 128), lambda b, g, i: (b, g, 0, 0), pipeline_mode=once)
    kvtspec = pl.BlockSpec((None, gps, nt, PV_ROWS, t), lambda b, g, i: (b, g, 0, 0, 0), pipeline_mode=once)
    in_specs = [qspec, qspec,
                pl.BlockSpec((None, gps, n16, 128), lambda b, g, i: (b, g, 0, 0)),
                pl.BlockSpec((None, gps, PV_ROWS, n16), lambda b, g, i: (b, g, 0, 0)),
                kvspec, kvtspec, kvspec, kvtspec,
                pl.BlockSpec((None, gps, 12, t), lambda b, g, i: (b, g, 0, i))]
    return pl.pallas_call(
        functools.partial(_nsa_kernel, n_cmp=n_cmp, n_top=n_top), grid=(B, G // gps, nt), in_specs=in_specs,
        out_specs=pl.BlockSpec((None, t, gps * HG * 64), lambda b, g, i: (b, i, g)),
        out_shape=jax.ShapeDtypeStruct((B, S, H * 64), BF16),
        scratch_shapes=[pltpu.VMEM((gps, n_sel, t), F32), pltpu.VMEM((PV_ROWS, gps * HG * t), F32)]
                       + _flash_scratch(gps * HG, t, t),
        compiler_params=_params(("parallel", "parallel", "parallel")), name="nsa_attention",
    )(qc, qr, kvc, kvct, kvs, kvst, kvw, kvwt, gates_t)


def _rope_freq_row(period, rot):
    half = rot // 2
    inv_freq = ROPE_THETA ** (-(jnp.arange(half, dtype=F32) * 2.0 / rot))
    lane = jnp.arange(LANES) % period
    f = jnp.where(lane < rot, inv_freq[lane % half], 0.0)
    return f.reshape(1, LANES).astype(F32)


def _norm_matrices():
    r = jnp.arange(LANES)
    same = (r[:, None] // 64) == (r[None, :] // 64)
    nq = jnp.where(same, 1.0 / 64, 0.0).astype(BF16)
    nk = jnp.where(same & (r[:, None] < 64), 1.0 / 64, 0.0).astype(BF16)
    return nq, nk


def _q_gain(g):
    return (jnp.tile(g.astype(F32), 2) * Q_SCALE).reshape(1, LANES)


def _k_gain(g):
    return jnp.concatenate([g.astype(F32), jnp.ones((64,), F32)]).reshape(1, LANES)


def _interleave_kv(wk, wv, n_heads):
    d = wk.shape[0]
    wk = wk.reshape(d, n_heads, 64)
    wv = wv.reshape(d, n_heads, 64)
    return jnp.concatenate([wk, wv], axis=2).reshape(d, n_heads * 128)


def _split_cols(w, sizes):
    out, start = [], 0
    for n in sizes:
        out.append(w[:, start:start + n])
        start += n
    return out


def _mixer_layer0(x2, trig, B, S, gmix, w_in, w_out, a_q_norm, a_k_norm, b_q_norm, b_k_norm):
    sizes = (512, 64, 64, 256, 32, 8, 512, 512, 512)
    waq, wak, wav, wiq, wik, wiw, wbq, wbk, wbv = _split_cols(w_in, sizes)
    pad = jnp.zeros((D_MODEL, LANES - 40), w_in.dtype)
    w = jnp.concatenate([waq, wbq, _interleave_kv(wbk, wbv, 8), wak, wav, wiq, wik, wiw, pad],
                        axis=1).astype(BF16)
    nq, nk = _norm_matrices()
    tabs = (nq, nk, _q_gain(a_q_norm), _q_gain(b_q_norm), _k_gain(a_k_norm), _k_gain(b_k_norm))
    aq, bq, bkv, bkvt, akv, akvt, iq, ik, iwt, km = _ab_prep(x2, trig, gmix, w, tabs, B, S)
    n_blk = S // MOBA_BLOCK
    kmean = km.reshape(B, n_blk, 8, 128).transpose(0, 2, 1, 3)
    o_a = _dsa_attention(iq, iwt, ik, aq, akv, akvt).reshape(B * S, 512)
    o_b = _moba_attention(bq, bkv, bkvt, kmean).reshape(B * S, 512)
    w_out = w_out.astype(BF16)
    return [o_a, o_b], [w_out[:512], w_out[512:]]


def _mixer_layer1(x2, trig, B, S, gmix, w_in, w_out, q_norm, kcmp_norm, ksel_norm, kwin_norm,
                  pos_k, pos_v, w1_k, w2_k, w1_v, w2_v):
    G = NSA_GROUPS
    sizes = (1024,) + (256,) * 6 + (48,)
    wq, wkc, wvc, wks, wvs, wkw, wvw, wgt = _split_cols(w_in, sizes)
    pad = jnp.zeros((D_MODEL, LANES - 48), w_in.dtype)
    w = jnp.concatenate([wq, _interleave_kv(wks, wvs, G), _interleave_kv(wkw, wvw, G),
                         wkc, wvc, wgt, pad], axis=1).astype(BF16)
    nq, nk = _norm_matrices()
    tabs = (nq, nk, _q_gain(q_norm), _k_gain(ksel_norm), _k_gain(kwin_norm))
    qc, qr, kvs, kvst, kvw, kvwt, kc16, vc16, gates_t = _nsa_prep(x2, trig, gmix, w, tabs, B, S)

    def pos_rows(p):
        return p.astype(F32).reshape(2, NSA_CMP_STRIDE * HEAD_DIM)

    kvc, kvct = _compress(kc16, vc16, pos_rows(pos_k), pos_rows(pos_v),
                          w1_k.astype(BF16), w1_v.astype(BF16), w2_k.astype(BF16), w2_v.astype(BF16),
                          kcmp_norm.astype(F32).reshape(1, HEAD_DIM))
    o = _nsa_attention(qc, qr, kvc, kvct, kvs, kvst, kvw, kvwt, gates_t)
    return [o.reshape(B * S, NSA_HEADS * HEAD_DIM)], [w_out.astype(BF16)]


def _finish_layer(parts, weights, x2, mem, S, g_mem, g_src, w_q, w_kv, w_o, q_norm, k_norm,
                  g_ffn, ffn_w_in, ffn_w_out):
    row = lambda v: v.astype(F32).reshape(1, -1)
    kv = _mem_kv(mem, row(g_src), w_kv.astype(BF16), row(k_norm))
    wg = ffn_w_in[:, :D_FF].astype(BF16)
    wu = ffn_w_in[:, D_FF:].astype(BF16)
    return _post_mixer(parts, weights, x2, row(g_mem), w_q.astype(BF16), row(q_norm), kv, w_o.astype(BF16),
                       row(g_ffn), wg, wu, ffn_w_out.astype(BF16), S)


def kernel(x, mem, positions, norm_mix, norm_mem, norm_mem_src, norm_ffn, ab_w_in, ab_w_out, dsa_q_norm, dsa_k_norm, moba_q_norm, moba_k_norm, nsa_w_in, nsa_w_out, nsa_q_norm, nsa_kcmp_norm, nsa_ksel_norm, nsa_kwin_norm, nsa_cmp_pos_k, nsa_cmp_pos_v, nsa_cmp_w1_k, nsa_cmp_w2_k, nsa_cmp_w1_v, nsa_cmp_w2_v, mem_w_q, mem_w_kv, mem_w_o, mem_q_norm, mem_k_norm, ffn_w_in, ffn_w_out):
    B, S, D = x.shape
    depth = norm_mix.shape[0]
    x2 = x.reshape(B * S, D)
    trig = _rope_trig(positions.astype(F32).reshape(B * S, 1), _rope_freq_row(64, 16), _rope_freq_row(32, 8))
    row = lambda v: v.astype(F32).reshape(1, -1)
    for i in range(depth):
        j = i // 2
        if i % 2 == 0:
            parts, weights = _mixer_layer0(x2, trig, B, S, row(norm_mix[i]), ab_w_in[j], ab_w_out[j],
                               dsa_q_norm[j], dsa_k_norm[j], moba_q_norm[j], moba_k_norm[j])
        else:
            parts, weights = _mixer_layer1(x2, trig, B, S, row(norm_mix[i]), nsa_w_in[j], nsa_w_out[j],
                               nsa_q_norm[j], nsa_kcmp_norm[j], nsa_ksel_norm[j], nsa_kwin_norm[j],
                               nsa_cmp_pos_k[j], nsa_cmp_pos_v[j], nsa_cmp_w1_k[j], nsa_cmp_w2_k[j],
                               nsa_cmp_w1_v[j], nsa_cmp_w2_v[j])
        x2 = _finish_layer(parts, weights, x2, mem, S, norm_mem[i], norm_mem_src[i], mem_w_q[i], mem_w_kv[i],
                           mem_w_o[i], mem_q_norm[i], mem_k_norm[i], norm_ffn[i], ffn_w_in[i], ffn_w_out[i])
    return x2.reshape(B, S, D)
```

```python
import functools
import math

import jax
import jax.numpy as jnp
from jax import lax
from jax.experimental import pallas as pl
from jax.experimental.pallas import tpu as pltpu

F32 = jnp.float32
BF16 = jnp.bfloat16
I32 = jnp.int32
I16 = jnp.int16

D_MODEL = 1024
N_MEM = 256
HEAD_DIM = 64
ROPE_THETA = 500000.0
RMS_EPS = 1e-6
NEG_INF = -1e30
TINY = 1e-20

DSA_HEADS = 8
DSA_IDX_HEADS = 8
DSA_IDX_DIM = 32
DSA_TOPK = 256
MOBA_HEADS = 8
MOBA_BLOCK = 256
MOBA_TOPK = 3
NSA_HEADS = 16
NSA_GROUPS = 4
NSA_CMP_LEN = 32
NSA_CMP_STRIDE = 16
NSA_SEL_LEN = 64
NSA_SEL_TOPK = 16
NSA_WINDOW = 512
NSA_FORCE = 1e4
MEM_HEADS = 4
MEM_HEAD_DIM = 128
D_FF = ((8 * D_MODEL + 3 * 256 - 1) // (3 * 256)) * 256

LANES = 128
SUBLANES = 8
INT_MIN = -(2 ** 31)
VMEM_LIMIT = 60 * 1024 * 1024

PV_HEAD_ROWS = 16
PV_ROWS = PV_HEAD_ROWS + HEAD_DIM
PROJ_GROUP = 4
ATT_T = 256
MASK_BIAS = -1e30
M_FLOOR = -1e29
LOG2E = math.log2(math.e)
Q_SCALE = HEAD_DIM ** -0.5 * LOG2E

NT_DIMS = (((1,), (1,)), ((), ()))


def _dot(a, b):
    return jnp.dot(a, b, preferred_element_type=F32)


def _dot_nt(a, b):
    return lax.dot_general(a, b, NT_DIMS, preferred_element_type=F32)


def _split_bf16(a):
    hi = a.astype(BF16)
    return hi, (a - hi.astype(F32)).astype(BF16)


def _split_dot(a, b):
    hi, lo = _split_bf16(a)
    return _dot(hi, b) + _dot(lo, b)


def _rms_rows(x, gain):
    ms = jnp.mean(x * x, axis=-1, keepdims=True)
    return x * lax.rsqrt(ms + RMS_EPS) * gain


def _params(sem):
    return pltpu.CompilerParams(dimension_semantics=sem, vmem_limit_bytes=VMEM_LIMIT)


def _head_norm(y, norm_m, gain):
    ms = _split_dot(y * y, norm_m)
    return y * lax.rsqrt(ms + RMS_EPS) * gain


def _rope(y, c, s, lo_mask, half):
    sw = jnp.where(lo_mask, pltpu.roll(y, LANES - half, 1), pltpu.roll(y, half, 1))
    return y * c + sw * s


def _lane_iota(shape):
    return lax.broadcasted_iota(I32, shape, 1)


def _row_iota(shape):
    return lax.broadcasted_iota(I32, shape, 0)


def _rope_tables(pos, ftab, period, half):
    ang = pos * ftab
    lane = _lane_iota(ang.shape) % period
    c = jnp.cos(ang)
    s = jnp.sin(ang) * jnp.where(lane < half, -1.0, 1.0)
    return c, s


def _pv_operand(kv):
    head = jnp.where(_row_iota((PV_HEAD_ROWS, kv.shape[0])) == 0, 1.0, 0.0)
    return jnp.concatenate([head, kv.T[HEAD_DIM:, :]], axis=0).astype(BF16)


class _ColumnProjector:
    def __init__(self, xn, w_ref):
        self.xn, self.w_ref, self.groups = xn, w_ref, {}

    def __call__(self, j):
        g, u = divmod(j, PROJ_GROUP)
        if g not in self.groups:
            width = PROJ_GROUP * LANES
            lo = g * width
            hi = min(lo + width, self.w_ref.shape[1])
            self.groups[g] = _dot(self.xn, self.w_ref[:, lo:hi])
        return self.groups[g][:, u * LANES:(u + 1) * LANES]


def _kv_column(yc, nk, gain, c64k, s64k, lo64, first64):
    kn = jnp.where(first64, _head_norm(yc, nk, gain), yc)
    return _rope(kn, c64k, s64k, lo64, 8)


def _rope_trig_kernel(pos_ref, f64_ref, f32_ref, o_ref):
    pos = pos_ref[...]
    c64, s64 = _rope_tables(pos, f64_ref[...], 64, 8)
    c32, s32 = _rope_tables(pos, f32_ref[...], 32, 4)
    o_ref[...] = jnp.concatenate([c64, s64, c32, s32], axis=1)


def _rope_trig(pos2, f64, f32t, tm=1024):
    T = pos2.shape[0]
    return pl.pallas_call(
        _rope_trig_kernel, grid=(T // tm,),
        in_specs=[pl.BlockSpec((tm, 1), lambda i: (i, 0)),
                  pl.BlockSpec(f64.shape, lambda i: (0, 0)), pl.BlockSpec(f32t.shape, lambda i: (0, 0))],
        out_specs=pl.BlockSpec((tm, 4 * LANES), lambda i: (i, 0)),
        out_shape=jax.ShapeDtypeStruct((T, 4 * LANES), F32),
        compiler_params=_params(("parallel",)), name="rope_trig",
    )(pos2, f64, f32t)


def _ab_prep_kernel(x_ref, trig_ref, gmix_ref, w_ref, nq_ref, nk_ref,
                    gaq_ref, gbq_ref, gak_ref, gbk_ref,
                    aq_ref, bq_ref, bkv_ref, bkvt_ref, akv_ref, akvt_ref, iq_ref, ik_ref, iwt_ref, km_ref,
                    *, n_tiles):
    xn = _rms_rows(x_ref[...], gmix_ref[...]).astype(BF16)
    c64, s64, c32, s32 = [trig_ref[:, j * LANES:(j + 1) * LANES] for j in range(4)]
    lane = _lane_iota(c64.shape)
    lo64 = (lane % 64) < 8
    lo32 = (lane % 32) < 4
    first64 = lane < 64
    c64k = jnp.where(first64, c64, 1.0)
    s64k = jnp.where(first64, s64, 0.0)
    first32 = lane < 32
    c32k = jnp.where(first32, c32, 1.0)
    s32k = jnp.where(first32, s32, 0.0)
    nq = nq_ref[...]
    nk = nk_ref[...]

    col = _ColumnProjector(xn, w_ref)

    for j in range(4):
        q = _rope(_head_norm(col(j), nq, gaq_ref[...]), c64, s64, lo64, 8)
        aq_ref[2 * j] = q[:, :64].astype(BF16)
        aq_ref[2 * j + 1] = q[:, 64:].astype(BF16)
    for j in range(4):
        q = _rope(_head_norm(col(4 + j), nq, gbq_ref[...]), c64, s64, lo64, 8)
        bq_ref[2 * j] = q[:, :64].astype(BF16)
        bq_ref[2 * j + 1] = q[:, 64:].astype(BF16)
    blk_onehot = jnp.where(lane == HEAD_DIM + pl.program_id(0) % n_tiles, 1.0, 0.0)
    for h in range(8):
        kv = _kv_column(col(8 + h), nk, gbk_ref[...], c64k, s64k, lo64, first64)
        bkv_ref[h] = jnp.where(first64, kv, blk_onehot).astype(BF16)
        bkvt_ref[h] = _pv_operand(kv)
        km_ref[h:h + 1, :] = jnp.mean(kv, axis=0, keepdims=True)
    kv = _kv_column(col(16), nk, gak_ref[...], c64k, s64k, lo64, first64)
    akv_ref[...] = kv.astype(BF16)
    akvt_ref[...] = _pv_operand(kv)
    for j in range(2):
        q = _rope(col(17 + j), c32, s32, lo32, 4)
        for u in range(4):
            iq_ref[4 * j + u] = q[:, 32 * u:32 * (u + 1)].astype(BF16)
    yc = col(19)
    ik_ref[...] = _rope(yc, c32k, s32k, lo32, 4)[:, :32].astype(BF16)
    iwt_ref[...] = yc.T[32:40, :]


def _ab_prep(x2, trig, gmix, w, tabs, B, S):
    T = x2.shape[0]
    tm = ATT_T
    nt = S // tm
    n_cols = w.shape[1]
    nq, nk, gaq, gbq, gak, gbk = tabs

    def full(a):
        return pl.BlockSpec(a.shape, lambda i: (0,) * a.ndim)

    def hm(width, heads=8):
        return pl.BlockSpec((None, heads, tm, width), lambda i: (i // nt, 0, i % nt, 0))

    def tokm(width):
        return pl.BlockSpec((None, tm, width), lambda i: (i // nt, i % nt, 0))

    out_shape = (
        jax.ShapeDtypeStruct((B, 8, S, 64), BF16),
        jax.ShapeDtypeStruct((B, 8, S, 64), BF16),
        jax.ShapeDtypeStruct((B, 8, S, 128), BF16),
        jax.ShapeDtypeStruct((B, 8, nt, PV_ROWS, tm), BF16),
        jax.ShapeDtypeStruct((B, S, 128), BF16),
        jax.ShapeDtypeStruct((B, nt, PV_ROWS, tm), BF16),
        jax.ShapeDtypeStruct((B, 8, S, 32), BF16),
        jax.ShapeDtypeStruct((B, S, 32), BF16),
        jax.ShapeDtypeStruct((B, 8, S), F32),
        jax.ShapeDtypeStruct((T // tm, 8, 128), F32),
    )
    out_specs = (hm(64), hm(64), hm(128),
                 pl.BlockSpec((None, 8, None, PV_ROWS, tm), lambda i: (i // nt, 0, i % nt, 0, 0)),
                 tokm(128),
                 pl.BlockSpec((None, None, PV_ROWS, tm), lambda i: (i // nt, i % nt, 0, 0)),
                 hm(32), tokm(32),
                 pl.BlockSpec((None, 8, tm), lambda i: (i // nt, 0, i % nt)),
                 pl.BlockSpec((None, 8, 128), lambda i: (i, 0, 0)))
    in_specs = [pl.BlockSpec((tm, D_MODEL), lambda i: (i, 0)),
                pl.BlockSpec((tm, 4 * LANES), lambda i: (i, 0)),
                full(gmix), pl.BlockSpec((D_MODEL, n_cols), lambda i: (0, 0)),
                full(nq), full(nk), full(gaq), full(gbq), full(gak), full(gbk)]
    return pl.pallas_call(
        functools.partial(_ab_prep_kernel, n_tiles=nt), grid=(T // tm,), in_specs=in_specs, out_specs=out_specs,
        out_shape=out_shape, compiler_params=_params(("parallel",)), name="ab_prep",
    )(x2, trig, gmix, w, nq, nk, gaq, gbq, gak, gbk)


def _pad_q(q):
    return jnp.concatenate([q, jnp.zeros_like(q)], axis=1)


def _bias_lanes(q, rows):
    n, tq = rows.shape
    parts = [jnp.zeros((HEAD_DIM, tq), F32), rows]
    if n < HEAD_DIM:
        parts.append(jnp.zeros((HEAD_DIM - n, tq), F32))
    lanes = jnp.concatenate(parts, axis=0).T.astype(BF16)
    return jnp.where(_lane_iota(q.shape) < HEAD_DIM, q, lanes)


class _Flash:
    def __init__(self, m_ref, acc_ref, s_ref, cmax_ref, p_ref, tq):
        self.m_ref, self.acc_ref, self.tq = m_ref, acc_ref, tq
        self.s_ref, self.cmax_ref, self.p_ref = s_ref, cmax_ref, p_ref

    def reset(self):
        self.m_ref[...] = jnp.full(self.m_ref.shape, M_FLOOR, F32)
        self.acc_ref[...] = jnp.zeros(self.acc_ref.shape, F32)

    def _scores(self, buf, qs, kvs, biases):
        tq = self.tq
        for i in range(len(qs)):
            s = _dot_nt(kvs[i], qs[i])
            if biases[i] is not None:
                s = s + biases[i]
            self.s_ref[buf, i, :s.shape[0], :] = s
            self.cmax_ref[buf, :, i * tq:(i + 1) * tq] = jnp.max(s, axis=0, keepdims=True)

    def update(self, qs, kvs, kvts, biases):
        self._scores(0, qs, kvs, biases)
        self._finish(0, kvts)

    def run(self, qs, count, operands):
        def scores(c, buf):
            kvs, _, biases = operands(c)
            self._scores(buf, qs, kvs, biases)

        def finish(c, buf):
            self._finish(buf, operands(c)[1])

        last = jnp.maximum(count - 1, 0)
        scores(0, 0)

        def two_chunks(pp, carry):
            c = 2 * pp
            scores(c + 1, 1)
            finish(c, 0)
            scores(jnp.minimum(c + 2, last), 0)
            finish(c + 1, 1)
            return carry

        lax.fori_loop(0, count // 2, two_chunks, 0)

        @pl.when(count % 2 == 1)
        def _():
            finish(count - 1, 0)

    def _finish(self, buf, kvts):
        n = len(kvts)
        tq = self.tq
        kc = sum(kvt.shape[1] for kvt in kvts[0])
        alphas = []
        for i in range(n):
            cols = slice(i * tq, (i + 1) * tq)
            m = self.m_ref[:, cols]
            m_new = jnp.maximum(m, self.cmax_ref[buf, :, cols])
            p = jnp.exp2(self.s_ref[buf, i, :kc, :] - m_new)
            alpha = jnp.exp2(m - m_new)
            self.m_ref[:, cols] = m_new
            self.p_ref[i, :kc, :] = p.astype(BF16)
            alphas.append(alpha)
        for i in range(n):
            cols = slice(i * tq, (i + 1) * tq)
            pv, r0 = None, 0
            for kvt in kvts[i]:
                part = _dot(kvt, self.p_ref[i, r0:r0 + kvt.shape[1], :])
                pv = part if pv is None else pv + part
                r0 += kvt.shape[1]
            self.acc_ref[:, cols] = alphas[i] * self.acc_ref[:, cols] + pv

    def result(self, slot):
        cols = slice(slot * self.tq, (slot + 1) * self.tq)
        acc = self.acc_ref[:, cols]
        return acc / jnp.maximum(acc[0:1, :], TINY)


def _flash_scratch(n_slots, tq, kc):
    return [pltpu.VMEM((1, n_slots * tq), F32), pltpu.VMEM((PV_ROWS, n_slots * tq), F32),
            pltpu.VMEM((2, n_slots, kc, tq), F32), pltpu.VMEM((2, 1, n_slots * tq), F32),
            pltpu.VMEM((n_slots, kc, tq), BF16)]


def _softmax_direct(qs, kv, kvts, bias):
    scores = [_dot_nt(kv, q) for q in qs]
    probs = []
    for s in scores:
        s = s + bias
        m = jnp.maximum(jnp.max(s, axis=0, keepdims=True), M_FLOOR)
        probs.append(jnp.exp2(s - m))
    inv_ls, outs = [], []
    for p in probs:
        pb = p.astype(BF16)
        o, r0 = None, 0
        for kvt in kvts:
            part = _dot(kvt, pb[r0:r0 + kvt.shape[1]])
            o = part if o is None else o + part
            r0 += kvt.shape[1]
        outs.append(o)
        inv_ls.append(1.0 / jnp.maximum(o[0:1, :], TINY))
    return probs, inv_ls, outs


def _causal_bias(t):
    return jnp.where(_row_iota((t, t)) <= _lane_iota((t, t)), 0.0, MASK_BIAS)


def _store_heads(o_ref, heads_t):
    for u in range(len(heads_t) // 2):
        pair = jnp.concatenate([heads_t[2 * u][PV_HEAD_ROWS:, :], heads_t[2 * u + 1][PV_HEAD_ROWS:, :]], axis=0)
        o_ref[:, u * LANES:(u + 1) * LANES] = pair.T.astype(o_ref.dtype)


def _rank_select_t(v, n_valid, n_top):
    n = v.shape[0]
    row = _row_iota(v.shape)
    rank = jnp.zeros(v.shape, F32)
    for m in range(n):
        vm = v[m:m + 1, :]
        ahead = (vm > v) | ((vm == v) & (m < row))
        if n_valid is not None:
            ahead = ahead & (m < n_valid)
        rank = rank + jnp.where(ahead, 1.0, 0.0)
    sel = rank < n_top
    if n_valid is not None:
        sel = sel & (row < n_valid)
    return jnp.where(sel, 1.0, 0.0)


def _dsa_kernel(iq_ref, iwt_ref, ik_ref, aq_ref, akv_ref, akvt_ref, o_ref,
                sk_ref, half_ref, bias_ref, xcut_ref, *flash_refs, k_top, index_bits):
    t = ATT_T
    i = pl.program_id(1)
    n_ch = i + 1
    kio = _row_iota((t, t))
    qio = _lane_iota((t, t))

    def causal(c):
        return (c - i) * t + kio <= qio

    def score_chunk(c):
        k0 = pl.multiple_of(c * t, t)
        ikc = ik_ref[pl.ds(k0, t), :]
        sc = jnp.zeros((t, t), F32)
        for h in range(DSA_IDX_HEADS):
            logit = _dot_nt(ikc, iq_ref[h])
            sc = sc + iwt_ref[h:h + 1, :] * jnp.maximum(logit, 0.0)
        sc = jnp.where(sc == 0.0, 0.0, sc)
        bits = pltpu.bitcast(sc, I32)
        key = bits ^ ((bits >> 31) & 0x7FFFFFFF)
        key = jnp.where(causal(c), key, INT_MIN)
        sk_ref[c] = key
        half_ref[c] = (key >> 16).astype(I16)

    def score_pair(cc, carry):
        score_chunk(2 * cc)
        score_chunk(2 * cc + 1)
        return carry

    lax.fori_loop(0, (n_ch + 1) // 2, score_pair, 0)

    def count(pred):
        def body(c, acc8):
            ind = jnp.where(pred(sk_ref[c], c), 1.0, 0.0)
            return acc8 + ind.reshape(-1, SUBLANES, t).sum(axis=0)
        acc8 = lax.fori_loop(0, n_ch, body, jnp.zeros((SUBLANES, t), F32))
        return jnp.sum(acc8, axis=0, keepdims=True)

    def count_half(cand):
        rows = 2 * SUBLANES

        def body(cc, acc):
            parts = []
            for c in (2 * cc, 2 * cc + 1):
                ind = jnp.where(half_ref[c] >= cand, jnp.bfloat16(1), jnp.bfloat16(0))
                parts += [ind[rows * j:rows * (j + 1), :] for j in range(t // rows)]
            while len(parts) > 1:
                parts = [parts[2 * j] + parts[2 * j + 1] for j in range(len(parts) // 2)]
            return acc + parts[0].astype(F32)
        acc = lax.fori_loop(0, (n_ch + 1) // 2, body, jnp.zeros((rows, t), F32))
        return jnp.sum(acc, axis=0, keepdims=True)

    def half_search(n_all):
        def bit_step(b, carry):
            v, n_ge_v = carry
            cand = v + lax.shift_left(jnp.int32(1), 15 - b)
            n_ge_cand = count_half(cand.astype(I16))
            ok = n_ge_cand >= k_top
            return jnp.where(ok, cand, v), jnp.where(ok, n_ge_cand, n_ge_v)
        return lax.fori_loop(0, 16, bit_step, (jnp.full((1, t), -(2 ** 15), I32), n_all))

    thr_hi, n_ge_hi = half_search(jnp.full((1, t), t * n_ch, I32).astype(F32))

    def low_half_chunk(c, carry):
        key = sk_ref[c]
        hi = key >> 16
        lo = (key & 0xFFFF) - 2 ** 15
        half_ref[c] = jnp.where(hi > thr_hi, 2 ** 15 - 1, jnp.where(hi < thr_hi, -(2 ** 15), lo)).astype(I16)
        return carry

    lax.fori_loop(0, n_ch, low_half_chunk, 0)
    thr_lo, n_ge = half_search(n_ge_hi)
    thr = lax.shift_left(thr_hi, 16) + (thr_lo + 2 ** 15)

    xcut_ref[...] = jnp.full((1, t), 2 ** 30, I32)

    @pl.when(jnp.max(n_ge) > k_top)
    def _():
        need = k_top - count(lambda blk, c: blk > thr)

        def x_step(b, x):
            cand = x + lax.shift_left(jnp.int32(1), index_bits - 1 - b)
            ties_below = count(lambda blk, c: (blk == thr) & (c * t + kio < cand))
            return jnp.where(ties_below <= need, cand, x)
        xcut_ref[...] = lax.fori_loop(0, index_bits, x_step, jnp.zeros((1, t), I32))

    xcut = xcut_ref[...]

    n_pairs = (n_ch + 1) // 2

    def bias_chunk(c, carry):
        blk = sk_ref[jnp.minimum(c, i)]
        keep = (blk > thr) | ((blk == thr) & (c * t + kio < xcut))
        bias_ref[c] = jnp.where(keep & causal(c), 0.0, MASK_BIAS)
        return carry

    lax.fori_loop(0, 2 * n_pairs, bias_chunk, 0)

    flash = _Flash(*flash_refs, t)
    qs = [_pad_q(aq_ref[h]) for h in range(DSA_HEADS)]
    flash.reset()

    n = DSA_HEADS

    def att_pair(cc):
        c0 = 2 * cc
        k0 = pl.multiple_of(c0 * t, 2 * t)
        kv = akv_ref[pl.ds(k0, 2 * t), :]
        bias = jnp.concatenate([bias_ref[c0], bias_ref[c0 + 1]], axis=0)
        return [kv] * n, [[akvt_ref[c0], akvt_ref[c0 + 1]]] * n, [bias] * n

    flash.run(qs, n_pairs, att_pair)
    _store_heads(o_ref, [flash.result(h) for h in range(DSA_HEADS)])


def _dsa_attention(iq, iwt, ik, aq, akv, akvt):
    B, _, S, _ = aq.shape
    t = ATT_T
    nt = S // t
    k_top = min(DSA_TOPK, S // 4)
    in_specs = [
        pl.BlockSpec((None, 8, t, 32), lambda b, i: (b, 0, i, 0)),
        pl.BlockSpec((None, 8, t), lambda b, i: (b, 0, i)),
        pl.BlockSpec((None, S, 32), lambda b, i: (b, 0, 0)),
        pl.BlockSpec((None, 8, t, 64), lambda b, i: (b, 0, i, 0)),
        pl.BlockSpec((None, S, 128), lambda b, i: (b, 0, 0)),
        pl.BlockSpec((None, nt, PV_ROWS, t), lambda b, i: (b, 0, 0, 0)),
    ]
    return pl.pallas_call(
        functools.partial(_dsa_kernel, k_top=k_top, index_bits=S.bit_length()),
        grid=(B, nt), in_specs=in_specs,
        out_specs=pl.BlockSpec((None, t, 512), lambda b, i: (b, i, 0)),
        out_shape=jax.ShapeDtypeStruct((B, S, 512), BF16),
        scratch_shapes=[pltpu.VMEM((nt, t, t), I32), pltpu.VMEM((nt, t, t), I16), pltpu.VMEM((nt, t, t), F32),
                        pltpu.VMEM((1, t), I32)] + _flash_scratch(DSA_HEADS, t, 2 * t),
        compiler_params=_params(("parallel", "parallel")), name="dsa_attention",
    )(iq, iwt, ik, aq, akv, akvt)


MOBA_HPS = 8


def _moba_kernel(q_ref, kv_ref, kvt_ref, km_ref, o_ref, *flash_refs, n_top):
    t = ATT_T
    own = pl.program_id(2)
    causal = _causal_bias(t)
    flash = _Flash(*flash_refs, t)
    qs = []
    for hh in range(MOBA_HPS):
        q = _pad_q(q_ref[hh])
        km_hi, km_lo = _split_bf16(km_ref[hh])
        gate = _dot_nt(km_hi, q) + _dot_nt(km_lo, q)
        keep = _rank_select_t(gate, own, n_top)
        keep = jnp.where(_row_iota(keep.shape) == own, 1.0, keep)
        qs.append(_bias_lanes(q, (keep - 1.0) * (-MASK_BIAS)))
    flash.reset()

    def operands(cc):
        n0 = 2 * cc
        k0 = pl.multiple_of(n0 * t, 2 * t)
        heads = range(MOBA_HPS)
        return (n0, [kv_ref[hh, pl.ds(k0, 2 * t), :] for hh in heads],
                [[kvt_ref[hh, n0], kvt_ref[hh, n0 + 1]] for hh in heads])

    def past_pair(cc):
        _, kvs, kvts = operands(cc)
        return kvs, kvts, [None] * MOBA_HPS

    flash.run(qs, own // 2, past_pair)
    heads = range(MOBA_HPS)

    @pl.when(own % 2 == 0)
    def _():
        k0 = pl.multiple_of(own * t, t)
        flash.update(qs, [kv_ref[hh, pl.ds(k0, t), :] for hh in heads],
                     [[kvt_ref[hh, own]] for hh in heads], [causal] * MOBA_HPS)

    @pl.when(own % 2 == 1)
    def _():
        _, kvs, kvts = operands(own // 2)
        bias = jnp.concatenate([jnp.zeros((t, t), F32), causal], axis=0)
        flash.update(qs, kvs, kvts, [bias] * MOBA_HPS)

    _store_heads(o_ref, [flash.result(hh) for hh in range(MOBA_HPS)])


def _moba_attention(bq, bkv, bkvt, kmean):
    B, H, S, _ = bq.shape
    t = ATT_T
    hps = MOBA_HPS
    n_blk = S // MOBA_BLOCK
    assert t == MOBA_BLOCK and n_blk % 2 == 0 and H % hps == 0
    n_top = max(1, min(MOBA_TOPK, n_blk - 1))
    in_specs = [
        pl.BlockSpec((None, hps, t, 64), lambda b, h, i: (b, h, i, 0)),
        pl.BlockSpec((None, hps, S, 128), lambda b, h, i: (b, h, 0, 0)),
        pl.BlockSpec((None, hps, n_blk, PV_ROWS, t), lambda b, h, i: (b, h, 0, 0, 0)),
        pl.BlockSpec((None, hps, n_blk, 128), lambda b, h, i: (b, h, 0, 0)),
    ]
    return pl.pallas_call(
        functools.partial(_moba_kernel, n_top=n_top), grid=(B, H // hps, S // t), in_specs=in_specs,
        out_specs=pl.BlockSpec((None, t, hps * 64), lambda b, h, i: (b, i, h)),
        out_shape=jax.ShapeDtypeStruct((B, S, H * 64), BF16),
        scratch_shapes=_flash_scratch(hps, t, 2 * t),
        compiler_params=_params(("parallel", "parallel", "parallel")), name="moba_attention",
    )(bq, bkv, bkvt, kmean)


def _lane_group_norm(y, gain, width):
    outs = []
    for j in range(y.shape[1] // width):
        yc = y[:, j * width:(j + 1) * width]
        outs.append(_rms_rows(yc, gain))
    return jnp.concatenate(outs, axis=1)


def _mem_kv_kernel(m_ref, g_ref, w_ref, gk_ref, o_ref):
    mn = _rms_rows(m_ref[...], g_ref[...]).astype(BF16)
    y = _dot(mn, w_ref[...])
    hw = MEM_HEADS * MEM_HEAD_DIM
    k = _lane_group_norm(y[:, :hw], gk_ref[...], MEM_HEAD_DIM)
    o_ref[...] = jnp.concatenate([k, y[:, hw:]], axis=1).astype(BF16)


def _mem_kv(mem, g, w, gk):
    B, M, _ = mem.shape
    n = w.shape[1]
    return pl.pallas_call(
        _mem_kv_kernel, grid=(B,),
        in_specs=[pl.BlockSpec((None, M, D_MODEL), lambda b: (b, 0, 0)),
                  pl.BlockSpec(g.shape, lambda b: (0, 0)),
                  pl.BlockSpec(w.shape, lambda b: (0, 0)),
                  pl.BlockSpec(gk.shape, lambda b: (0, 0))],
        out_specs=pl.BlockSpec((None, M, n), lambda b: (b, 0, 0)),
        out_shape=jax.ShapeDtypeStruct((B, M, n), BF16),
        compiler_params=_params(("parallel",)), name="mem_kv",
    )(mem, g, w, gk)


def _mem_attend(x, g_ref, wq_ref, gq_ref, kv_ref, wo_ref):
    xn = _rms_rows(x, g_ref[...]).astype(BF16)
    q = _lane_group_norm(_dot(xn, wq_ref[...]), gq_ref[...], MEM_HEAD_DIM).astype(BF16)
    hw = MEM_HEADS * MEM_HEAD_DIM
    scale = MEM_HEAD_DIM ** -0.5
    outs = []
    for h in range(MEM_HEADS):
        cols = slice(h * MEM_HEAD_DIM, (h + 1) * MEM_HEAD_DIM)
        k = kv_ref[:, cols]
        v = kv_ref[:, hw + h * MEM_HEAD_DIM:hw + (h + 1) * MEM_HEAD_DIM]
        s = _dot_nt(q[:, cols], k) * scale
        p = jnp.exp(s - jnp.max(s, axis=-1, keepdims=True))
        p = p / jnp.sum(p, axis=-1, keepdims=True)
        outs.append(_dot(p.astype(BF16), v))
    o = jnp.concatenate(outs, axis=1).astype(BF16)
    return x + _dot(o, wo_ref[...])


def _post_mixer_kernel(*refs, n_in):
    a_refs = refs[:n_in]
    w_refs = refs[n_in:2 * n_in]
    (x_ref, gm_ref, wq_ref, gq_ref, kv_ref, wo_ref, gf_ref, wg_ref, wu_ref, wd_ref,
     o_ref, xn_ref, acc_ref) = refs[2 * n_in:]
    j = pl.program_id(1)

    @pl.when(j == 0)
    def _():
        x = x_ref[...]
        for a_ref, w_ref in zip(a_refs, w_refs):
            x = x + _dot(a_ref[...], w_ref[...])
        x = _mem_attend(x, gm_ref, wq_ref, gq_ref, kv_ref, wo_ref)
        xn_ref[...] = _rms_rows(x, gf_ref[...]).astype(BF16)
        acc_ref[...] = x

    xn = xn_ref[...]
    gate = _dot(xn, wg_ref[...])
    up = _dot(xn, wu_ref[...])
    act = (gate * jax.nn.sigmoid(gate) * up).astype(BF16)
    acc_ref[...] += _dot(act, wd_ref[...])

    @pl.when(j == pl.num_programs(1) - 1)
    def _():
        o_ref[...] = acc_ref[...]


def _post_mixer(parts, weights, x2, g_mem, wq, gq, kv, wo, g_ffn, wg, wu, wd, S, tm=512, n_split=2):
    T = x2.shape[0]
    nt = S // tm
    tf = D_FF // n_split
    n_in = len(parts)
    M, n = kv.shape[1], kv.shape[2]

    def const(a):
        return pl.BlockSpec(a.shape, lambda i, j: (0,) * a.ndim)

    in_specs = ([pl.BlockSpec((tm, p.shape[1]), lambda i, j: (i, 0)) for p in parts]
                + [const(w) for w in weights]
                + [pl.BlockSpec((tm, D_MODEL), lambda i, j: (i, 0)),
                   const(g_mem), const(wq), const(gq),
                   pl.BlockSpec((None, M, n), lambda i, j: (i // nt, 0, 0)),
                   const(wo), const(g_ffn),
                   pl.BlockSpec((D_MODEL, tf), lambda i, j: (0, j)),
                   pl.BlockSpec((D_MODEL, tf), lambda i, j: (0, j)),
                   pl.BlockSpec((tf, D_MODEL), lambda i, j: (j, 0))])
    return pl.pallas_call(
        functools.partial(_post_mixer_kernel, n_in=n_in), grid=(T // tm, n_split), in_specs=in_specs,
        out_specs=pl.BlockSpec((tm, D_MODEL), lambda i, j: (i, 0)),
        out_shape=jax.ShapeDtypeStruct((T, D_MODEL), F32),
        scratch_shapes=[pltpu.VMEM((tm, D_MODEL), BF16), pltpu.VMEM((tm, D_MODEL), F32)],
        compiler_params=_params(("parallel", "arbitrary")), name="post_mixer",
    )(*parts, *weights, x2, g_mem, wq, gq, kv, wo, g_ffn, wg, wu, wd)


def _nsa_prep_kernel(x_ref, trig_ref, gmix_ref, w_ref, nq_ref, nk_ref,
                     gq_ref, gks_ref, gkw_ref,
                     qc_ref, qr_ref, kvs_ref, kvst_ref, kvw_ref, kvwt_ref, kc_ref, vc_ref, gtt_ref,
                     stage_ref, *, n_tiles):
    xn = _rms_rows(x_ref[...], gmix_ref[...]).astype(BF16)
    c64, s64 = trig_ref[:, :LANES], trig_ref[:, LANES:]
    lane = _lane_iota(c64.shape)
    lo64 = (lane % 64) < 8
    first64 = lane < 64
    c64k = jnp.where(first64, c64, 1.0)
    s64k = jnp.where(first64, s64, 0.0)
    nq = nq_ref[...]
    nk = nk_ref[...]

    col = _ColumnProjector(xn, w_ref)

    for j in range(8):
        qn = _head_norm(col(j), nq, gq_ref[...])
        qr = _rope(qn, c64, s64, lo64, 8)
        qc_ref[2 * j] = qn[:, :64].astype(BF16)
        qc_ref[2 * j + 1] = qn[:, 64:].astype(BF16)
        qr_ref[2 * j] = qr[:, :64].astype(BF16)
        qr_ref[2 * j + 1] = qr[:, 64:].astype(BF16)
    tile = pl.program_id(0) % n_tiles
    sel_blk = tile * (ATT_T // NSA_SEL_LEN) + lax.shift_right_logical(
        _row_iota(c64.shape), NSA_SEL_LEN.bit_length() - 1)
    blk_onehot = jnp.where(lane == HEAD_DIM + sel_blk, 1.0, 0.0)
    for g in range(NSA_GROUPS):
        kv = _kv_column(col(8 + g), nk, gks_ref[...], c64k, s64k, lo64, first64)
        kvs_ref[g] = jnp.where(first64, kv, blk_onehot).astype(BF16)
        kvst_ref[g] = _pv_operand(kv)
        kv = _kv_column(col(12 + g), nk, gkw_ref[...], c64k, s64k, lo64, first64)
        kvw_ref[g] = kv.astype(BF16)
        kvwt_ref[g] = _pv_operand(kv)
    stride = NSA_CMP_STRIDE
    rows = stage_ref.shape[0] // stride
    for out_ref, first in ((kc_ref, 16), (vc_ref, 18)):
        for c in range(2):
            stage_ref[...] = col(first + c)
            for u in range(0, stride, 2):
                pair = [stage_ref[pl.ds(u + v, rows, stride=stride), :] for v in range(2)]
                for h in range(2):
                    halves = [p[:, h * HEAD_DIM:(h + 1) * HEAD_DIM] for p in pair]
                    out_ref[2 * c + h, :, u * HEAD_DIM:(u + 2) * HEAD_DIM] = jnp.concatenate(halves, axis=1)
    gates_t = jax.nn.sigmoid(col(20)).T
    for g in range(NSA_GROUPS):
        gtt_ref[g] = gates_t[12 * g:12 * (g + 1), :]


def _nsa_prep(x2, trig, gmix, w, tabs, B, S):
    T = x2.shape[0]
    tm = ATT_T
    nt = S // tm
    nq, nk, gq, gks, gkw = tabs

    def full(a):
        return pl.BlockSpec(a.shape, lambda i: (0,) * a.ndim)

    def hm(width, heads):
        return pl.BlockSpec((None, heads, tm, width), lambda i: (i // nt, 0, i % nt, 0))

    def hmt(heads):
        return pl.BlockSpec((None, heads, None, PV_ROWS, tm), lambda i: (i // nt, 0, i % nt, 0, 0))

    def tokm(width):
        return pl.BlockSpec((None, tm, width), lambda i: (i // nt, i % nt, 0))

    out_shape = (
        jax.ShapeDtypeStruct((B, 16, S, 64), BF16),
        jax.ShapeDtypeStruct((B, 16, S, 64), BF16),
        jax.ShapeDtypeStruct((B, 4, S, 128), BF16),
        jax.ShapeDtypeStruct((B, 4, nt, PV_ROWS, tm), BF16),
        jax.ShapeDtypeStruct((B, 4, S, 128), BF16),
        jax.ShapeDtypeStruct((B, 4, nt, PV_ROWS, tm), BF16),
        jax.ShapeDtypeStruct((B, 4, S // 16, 1024), F32),
        jax.ShapeDtypeStruct((B, 4, S // 16, 1024), F32),
        jax.ShapeDtypeStruct((B, 4, 12, S), F32),
    )
    rows16 = pl.BlockSpec((None, 4, tm // 16, 1024), lambda i: (i // nt, 0, i % nt, 0))
    out_specs = (hm(64, 16), hm(64, 16), hm(128, 4), hmt(4), hm(128, 4), hmt(4), rows16, rows16,
                 pl.BlockSpec((None, 4, 12, tm), lambda i: (i // nt, 0, 0, i % nt)))
    in_specs = [pl.BlockSpec((tm, D_MODEL), lambda i: (i, 0)),
                pl.BlockSpec((tm, 2 * LANES), lambda i: (i, 0)),
                full(gmix), full(w), full(nq), full(nk), full(gq), full(gks), full(gkw)]
    return pl.pallas_call(
        functools.partial(_nsa_prep_kernel, n_tiles=nt), grid=(T // tm,), in_specs=in_specs, out_specs=out_specs,
        out_shape=out_shape, scratch_shapes=[pltpu.VMEM((tm, LANES), F32)],
        compiler_params=_params(("parallel",)), name="nsa_prep",
    )(x2, trig, gmix, w, nq, nk, gq, gks, gkw)


def _compress_one(x16, pa, pb, w1a, w1b, w2):
    n16 = x16.shape[0]
    h_a = _dot((x16 + pa).astype(BF16), w1a)
    h_b = _dot((x16 + pb).astype(BF16), w1b)
    pre = h_a + pltpu.roll(h_b, n16 - 1, 0)
    act = pre * jax.nn.sigmoid(pre)
    return _dot(act.astype(BF16), w2)


def _compress_kernel(xk_ref, xv_ref, pk_ref, pv_ref, w1k_ref, w1v_ref, w2k_ref, w2v_ref, gk_ref,
                     o_ref, ot_ref):
    half = w1k_ref.shape[0] // 2
    k = _compress_one(xk_ref[...], pk_ref[0:1, :], pk_ref[1:2, :],
                      w1k_ref[:half, :], w1k_ref[half:, :], w2k_ref[...])
    k = _rms_rows(k, gk_ref[...])
    v = _compress_one(xv_ref[...], pv_ref[0:1, :], pv_ref[1:2, :],
                      w1v_ref[:half, :], w1v_ref[half:, :], w2v_ref[...])
    kv = jnp.concatenate([k, v], axis=1)
    o_ref[...] = kv.astype(BF16)
    ot_ref[...] = _pv_operand(kv)


def _compress(xk16, xv16, pk, pv, w1k, w1v, w2k, w2v, gk):
    B, G, n16, width = xk16.shape

    def full(a):
        return pl.BlockSpec(a.shape, lambda b, g: (0,) * a.ndim)

    xspec = pl.BlockSpec((None, None, n16, width), lambda b, g: (b, g, 0, 0))
    return pl.pallas_call(
        _compress_kernel, grid=(B, G),
        in_specs=[xspec, xspec, full(pk), full(pv), full(w1k), full(w1v), full(w2k), full(w2v), full(gk)],
        out_specs=(pl.BlockSpec((None, None, n16, 128), lambda b, g: (b, g, 0, 0)),
                   pl.BlockSpec((None, None, PV_ROWS, n16), lambda b, g: (b, g, 0, 0))),
        out_shape=(jax.ShapeDtypeStruct((B, G, n16, 128), BF16),
                   jax.ShapeDtypeStruct((B, G, PV_ROWS, n16), BF16)),
        compiler_params=_params(("parallel", "parallel")), name="nsa_compress",
    )(xk16, xv16, pk, pv, w1k, w1v, w2k, w2v, gk)


NSA_GPS = 4


def _nsa_tile_masks(i, n16, n_sel, n_cmp, n_top):
    t = ATT_T
    t0 = i * t
    n_id = _row_iota((n16, t))
    q_id = t0 + _lane_iota((n16, t))
    cmp_visible = (n_id < n_cmp) & (n_id * NSA_CMP_STRIDE + (NSA_CMP_LEN - 1) <= q_id)
    b_id = _row_iota((n_sel, n16)) * NSA_SEL_LEN
    r_id = _lane_iota((n_sel, n16)) * NSA_CMP_STRIDE
    cover_t = ((r_id < b_id + NSA_SEL_LEN) & (r_id + NSA_CMP_LEN > b_id)
               & (_lane_iota((n_sel, n16)) < n_cmp))
    n_wc = NSA_WINDOW // t + 1
    cw = jnp.maximum(i - (n_wc - 1), 0)
    dist = (i - cw) * t + _lane_iota((n_wc * t, t)) - _row_iota((n_wc * t, t))
    blk = _row_iota((n_sel, t))
    cur = lax.shift_right_logical(t0 + _lane_iota((n_sel, t)), NSA_SEL_LEN.bit_length() - 1)
    return dict(
        bias_c=jnp.where(cmp_visible, 0.0, MASK_BIAS),
        cover_t=jnp.where(cover_t, 1.0, 0.0).astype(BF16),
        n_wc=n_wc, cw=cw, bias_w=jnp.where((dist >= 0) & (dist < NSA_WINDOW), 0.0, MASK_BIAS),
        forced=(blk == 0) | (blk == cur) | (blk == cur - 1), visible_blk=blk <= cur,
        n_wanted=jnp.minimum(cur[0:1, :] + 1, n_top).astype(F32))


def _nsa_front(qc_ref, qr_ref, kvc_ref, kvct_ref, kvw_ref, kvwt_ref, gtt_ref, part_ref, masks):
    t = ATT_T
    HG = NSA_HEADS // NSA_GROUPS
    n16 = kvc_ref.shape[0]

    p_sum = jnp.zeros((n16, t), F32)
    o_c = []
    probs, inv_ls, outs = _softmax_direct([_pad_q(qc_ref[j]) for j in range(HG)], kvc_ref[...],
                                          [kvct_ref[...]], masks["bias_c"])
    for j in range(HG):
        p_sum = p_sum + probs[j] * inv_ls[j]
        o_c.append(outs[j] * inv_ls[j])

    p_hi, p_lo = _split_bf16(p_sum)
    imp = _dot(masks["cover_t"], p_hi) + _dot(masks["cover_t"], p_lo)

    qs = [_pad_q(qr_ref[j]) for j in range(HG)]
    n_wc, cw = masks["n_wc"], masks["cw"]
    kw0 = pl.multiple_of(cw * t, t)
    _, inv_lw, out_w = _softmax_direct(qs, kvw_ref[pl.ds(kw0, n_wc * t), :],
                                       [kvwt_ref[cw + u] for u in range(n_wc)], masks["bias_w"])
    for j in range(HG):
        part_ref[:, j * t:(j + 1) * t] = (gtt_ref[3 * j:3 * j + 1, :] * o_c[j]
                                          + gtt_ref[3 * j + 2:3 * j + 3, :] * (out_w[j] * inv_lw[j]))

    imp = jnp.where(masks["forced"], NSA_FORCE, imp)
    return qs, jnp.where(masks["visible_blk"], imp, NEG_INF)


RANK_SEGMENT = 16


def _count_larger(imp, acc, rows):
    for m in rows:
        acc = acc + jnp.where(imp[m:m + 1, :] > imp, 1.0, 0.0)
    return acc


def _nsa_kernel(qc_ref, qr_ref, kvc_ref, kvct_ref, kvs_ref, kvst_ref, kvw_ref, kvwt_ref, gtt_ref,
                o_ref, sel_ref, part_ref, *flash_refs, n_cmp, n_top):
    t = ATT_T
    HG = NSA_HEADS // NSA_GROUPS
    n_slots = NSA_GPS * HG
    i = pl.program_id(2)

    masks = _nsa_tile_masks(i, kvc_ref.shape[1], sel_ref.shape[1], n_cmp, n_top)
    n_sel = sel_ref.shape[1]
    qs, imps = [], []
    for g in range(NSA_GPS):
        heads_g = pl.ds(g * HG, HG)
        q_g, imp_g = _nsa_front(
            qc_ref.at[heads_g], qr_ref.at[heads_g], kvc_ref.at[g], kvct_ref.at[g], kvw_ref.at[g],
            kvwt_ref.at[g], gtt_ref.at[g], part_ref.at[:, pl.ds(g * HG * t, HG * t)], masks)
        qs += q_g
        imps.append(imp_g)
        sel_ref[g] = _count_larger(imp_g, jnp.zeros((n_sel, t), F32), range(RANK_SEGMENT))

    for k in range(1, n_sel // RANK_SEGMENT):
        @pl.when((i + 1) * (t // NSA_SEL_LEN) > k * RANK_SEGMENT)
        def _():
            for g in range(NSA_GPS):
                sel_ref[g] = _count_larger(imps[g], sel_ref[g],
                                           range(k * RANK_SEGMENT, (k + 1) * RANK_SEGMENT))

    miss = None
    for g in range(NSA_GPS):
        sel_fast = sel_ref[g] < n_top
        n_picked = jnp.sum(jnp.where(sel_fast & masks["visible_blk"], 1.0, 0.0), axis=0, keepdims=True)
        sel_ref[g] = jnp.where(sel_fast, 0.0, MASK_BIAS)
        miss_g = jnp.abs(n_picked - masks["n_wanted"])
        miss = miss_g if miss is None else jnp.maximum(miss, miss_g)

    @pl.when(jnp.max(miss) > 0.0)
    def _():
        for g in range(NSA_GPS):
            sel_ref[g] = (_rank_select_t(imps[g], None, n_top) - 1.0) * (-MASK_BIAS)

    qs_sel = [_bias_lanes(qs[s], sel_ref[s // HG]) for s in range(n_slots)]
    flash = _Flash(*flash_refs, t)
    flash.reset()

    def sel_operands(c):
        k0 = pl.multiple_of(c * t, t)
        kvs = [kvs_ref[s // HG, pl.ds(k0, t), :] for s in range(n_slots)]
        kvts = [[kvst_ref[s // HG, c]] for s in range(n_slots)]
        return kvs, kvts

    def past_chunk(c):
        kvs, kvts = sel_operands(c)
        return kvs, kvts, [None] * n_slots

    flash.run(qs_sel, i, past_chunk)
    kvs, kvts = sel_operands(i)
    flash.update(qs_sel, kvs, kvts, [_causal_bias(t)] * n_slots)

    heads = []
    for s in range(n_slots):
        g, j = divmod(s, HG)
        heads.append(part_ref[:, s * t:(s + 1) * t] + gtt_ref[g, 3 * j + 1:3 * j + 2, :] * flash.result(s))
    _store_heads(o_ref, heads)


def _nsa_attention(qc, qr, kvc, kvct, kvs, kvst, kvw, kvwt, gates_t):
    B, H, S, _ = qc.shape
    G = NSA_GROUPS
    HG = H // G
    t = ATT_T
    nt = S // t
    n16 = kvc.shape[2]
    n_cmp = (S - NSA_CMP_LEN) // NSA_CMP_STRIDE + 1
    n_sel = S // NSA_SEL_LEN
    n_top = min(NSA_SEL_TOPK, n_sel)
    gps = NSA_GPS
    assert G % gps == 0 and nt % 2 == 0 and n_sel <= HEAD_DIM and S >= (NSA_WINDOW // t + 1) * t
    qspec = pl.BlockSpec((None, gps * HG, t, 64), lambda b, g, i: (b, g, i, 0))
    once = pl.Buffered(1)
    kvspec = pl.BlockSpec((None, gps, S, 128), lambda b, g, i: (b, g, 0, 0), pipeline_mode=once)
    kvtspec = pl.BlockSpec((None, gps, nt, PV_ROWS, t), lambda b, g, i: (b, g, 0, 0, 0), pipeline_mode=once)
    in_specs = [qspec, qspec,
                pl.BlockSpec((None, gps, n16, 128), lambda b, g, i: (b, g, 0, 0)),
                pl.BlockSpec((None, gps, PV_ROWS, n16), lambda b, g, i: (b, g, 0, 0)),
                kvspec, kvtspec, kvspec, kvtspec,
                pl.BlockSpec((None, gps, 12, t), lambda b, g, i: (b, g, 0, i))]
    return pl.pallas_call(
        functools.partial(_nsa_kernel, n_cmp=n_cmp, n_top=n_top), grid=(B, G // gps, nt), in_specs=in_specs,
        out_specs=pl.BlockSpec((None, t, gps * HG * 64), lambda b, g, i: (b, i, g)),
        out_shape=jax.ShapeDtypeStruct((B, S, H * 64), BF16),
        scratch_shapes=[pltpu.VMEM((gps, n_sel, t), F32), pltpu.VMEM((PV_ROWS, gps * HG * t), F32)]
                       + _flash_scratch(gps * HG, t, t),
        compiler_params=_params(("parallel", "parallel", "parallel")), name="nsa_attention",
    )(qc, qr, kvc, kvct, kvs, kvst, kvw, kvwt, gates_t)


def _rope_freq_row(period, rot):
    half = rot // 2
    inv_freq = ROPE_THETA ** (-(jnp.arange(half, dtype=F32) * 2.0 / rot))
    lane = jnp.arange(LANES) % period
    f = jnp.where(lane < rot, inv_freq[lane % half], 0.0)
    return f.reshape(1, LANES).astype(F32)


def _norm_matrices():
    r = jnp.arange(LANES)
    same = (r[:, None] // 64) == (r[None, :] // 64)
    nq = jnp.where(same, 1.0 / 64, 0.0).astype(BF16)
    nk = jnp.where(same & (r[:, None] < 64), 1.0 / 64, 0.0).astype(BF16)
    return nq, nk


def _q_gain(g):
    return (jnp.tile(g.astype(F32), 2) * Q_SCALE).reshape(1, LANES)


def _k_gain(g):
    return jnp.concatenate([g.astype(F32), jnp.ones((64,), F32)]).reshape(1, LANES)


def _interleave_kv(wk, wv, n_heads):
    d = wk.shape[0]
    wk = wk.reshape(d, n_heads, 64)
    wv = wv.reshape(d, n_heads, 64)
    return jnp.concatenate([wk, wv], axis=2).reshape(d, n_heads * 128)


def _split_cols(w, sizes):
    out, start = [], 0
    for n in sizes:
        out.append(w[:, start:start + n])
        start += n
    return out


def _mixer_layer0(x2, trig, B, S, gmix, w_in, w_out, a_q_norm, a_k_norm, b_q_norm, b_k_norm):
    sizes = (512, 64, 64, 256, 32, 8, 512, 512, 512)
    waq, wak, wav, wiq, wik, wiw, wbq, wbk, wbv = _split_cols(w_in, sizes)
    pad = jnp.zeros((D_MODEL, LANES - 40), w_in.dtype)
    w = jnp.concatenate([waq, wbq, _interleave_kv(wbk, wbv, 8), wak, wav, wiq, wik, wiw, pad],
                        axis=1).astype(BF16)
    nq, nk = _norm_matrices()
    tabs = (nq, nk, _q_gain(a_q_norm), _q_gain(b_q_norm), _k_gain(a_k_norm), _k_gain(b_k_norm))
    aq, bq, bkv, bkvt, akv, akvt, iq, ik, iwt, km = _ab_prep(x2, trig, gmix, w, tabs, B, S)
    n_blk = S // MOBA_BLOCK
    kmean = km.reshape(B, n_blk, 8, 128).transpose(0, 2, 1, 3)
    o_a = _dsa_attention(iq, iwt, ik, aq, akv, akvt).reshape(B * S, 512)
    o_b = _moba_attention(bq, bkv, bkvt, kmean).reshape(B * S, 512)
    w_out = w_out.astype(BF16)
    return [o_a, o_b], [w_out[:512], w_out[512:]]


def _mixer_layer1(x2, trig, B, S, gmix, w_in, w_out, q_norm, kcmp_norm, ksel_norm, kwin_norm,
                  pos_k, pos_v, w1_k, w2_k, w1_v, w2_v):
    G = NSA_GROUPS
    sizes = (1024,) + (256,) * 6 + (48,)
    wq, wkc, wvc, wks, wvs, wkw, wvw, wgt = _split_cols(w_in, sizes)
    pad = jnp.zeros((D_MODEL, LANES - 48), w_in.dtype)
    w = jnp.concatenate([wq, _interleave_kv(wks, wvs, G), _interleave_kv(wkw, wvw, G),
                         wkc, wvc, wgt, pad], axis=1).astype(BF16)
    nq, nk = _norm_matrices()
    tabs = (nq, nk, _q_gain(q_norm), _k_gain(ksel_norm), _k_gain(kwin_norm))
    qc, qr, kvs, kvst, kvw, kvwt, kc16, vc16, gates_t = _nsa_prep(x2, trig, gmix, w, tabs, B, S)

    def pos_rows(p):
        return p.astype(F32).reshape(2, NSA_CMP_STRIDE * HEAD_DIM)

    kvc, kvct = _compress(kc16, vc16, pos_rows(pos_k), pos_rows(pos_v),
                          w1_k.astype(BF16), w1_v.astype(BF16), w2_k.astype(BF16), w2_v.astype(BF16),
                          kcmp_norm.astype(F32).reshape(1, HEAD_DIM))
    o = _nsa_attention(qc, qr, kvc, kvct, kvs, kvst, kvw, kvwt, gates_t)
    return [o.reshape(B * S, NSA_HEADS * HEAD_DIM)], [w_out.astype(BF16)]


def _finish_layer(parts, weights, x2, mem, S, g_mem, g_src, w_q, w_kv, w_o, q_norm, k_norm,
                  g_ffn, ffn_w_in, ffn_w_out):
    row = lambda v: v.astype(F32).reshape(1, -1)
    kv = _mem_kv(mem, row(g_src), w_kv.astype(BF16), row(k_norm))
    wg = ffn_w_in[:, :D_FF].astype(BF16)
    wu = ffn_w_in[:, D_FF:].astype(BF16)
    return _post_mixer(parts, weights, x2, row(g_mem), w_q.astype(BF16), row(q_norm), kv, w_o.astype(BF16),
                       row(g_ffn), wg, wu, ffn_w_out.astype(BF16), S)


def kernel(x, mem, positions, norm_mix, norm_mem, norm_mem_src, norm_ffn, ab_w_in, ab_w_out, dsa_q_norm, dsa_k_norm, moba_q_norm, moba_k_norm, nsa_w_in, nsa_w_out, nsa_q_norm, nsa_kcmp_norm, nsa_ksel_norm, nsa_kwin_norm, nsa_cmp_pos_k, nsa_cmp_pos_v, nsa_cmp_w1_k, nsa_cmp_w2_k, nsa_cmp_w1_v, nsa_cmp_w2_v, mem_w_q, mem_w_kv, mem_w_o, mem_q_norm, mem_k_norm, ffn_w_in, ffn_w_out):
    B, S, D = x.shape
    depth = norm_mix.shape[0]
    x2 = x.reshape(B * S, D)
    trig = _rope_trig(positions.astype(F32).reshape(B * S, 1), _rope_freq_row(64, 16), _rope_freq_row(32, 8))
    row = lambda v: v.astype(F32).reshape(1, -1)
    for i in range(depth):
        j = i // 2
        if i % 2 == 0:
            parts, weights = _mixer_layer0(x2, trig, B, S, row(norm_mix[i]), ab_w_in[j], ab_w_out[j],
                               dsa_q_norm[j], dsa_k_norm[j], moba_q_norm[j], moba_k_norm[j])
        else:
            parts, weights = _mixer_layer1(x2, trig, B, S, row(norm_mix[i]), nsa_w_in[j], nsa_w_out[j],
                               nsa_q_norm[j], nsa_kcmp_norm[j], nsa_ksel_norm[j], nsa_kwin_norm[j],
                               nsa_cmp_pos_k[j], nsa_cmp_pos_v[j], nsa_cmp_w1_k[j], nsa_cmp_w2_k[j],
                               nsa_cmp_w1_v[j], nsa_cmp_w2_v[j])
        x2 = _finish_layer(parts, weights, x2, mem, S, norm_mem[i], norm_mem_src[i], mem_w_q[i], mem_w_kv[i],
                           mem_w_o[i], mem_q_norm[i], mem_k_norm[i], norm_ffn[i], ffn_w_in[i], ffn_w_out[i])
    return x2.reshape(B, S, D)
```

```python
import functools
import math

import jax
import jax.numpy as jnp
from jax import lax
from jax.experimental import pallas as pl
from jax.experimental.pallas import tpu as pltpu

F32 = jnp.float32
BF16 = jnp.bfloat16
I32 = jnp.int32
I16 = jnp.int16

D_MODEL = 1024
N_MEM = 256
HEAD_DIM = 64
ROPE_THETA = 500000.0
RMS_EPS = 1e-6
NEG_INF = -1e30
TINY = 1e-20

DSA_HEADS = 8
DSA_IDX_HEADS = 8
DSA_IDX_DIM = 32
DSA_TOPK = 256
MOBA_HEADS = 8
MOBA_BLOCK = 256
MOBA_TOPK = 3
NSA_HEADS = 16
NSA_GROUPS = 4
NSA_CMP_LEN = 32
NSA_CMP_STRIDE = 16
NSA_SEL_LEN = 64
NSA_SEL_TOPK = 16
NSA_WINDOW = 512
NSA_FORCE = 1e4
MEM_HEADS = 4
MEM_HEAD_DIM = 128
D_FF = ((8 * D_MODEL + 3 * 256 - 1) // (3 * 256)) * 256

LANES = 128
SUBLANES = 8
INT_MIN = -(2 ** 31)
VMEM_LIMIT = 60 * 1024 * 1024

PV_HEAD_ROWS = 16
PV_ROWS = PV_HEAD_ROWS + HEAD_DIM
PROJ_GROUP = 4
ATT_T = 256
MASK_BIAS = -1e30
M_FLOOR = -1e29
LOG2E = math.log2(math.e)
Q_SCALE = HEAD_DIM ** -0.5 * LOG2E

NT_DIMS = (((1,), (1,)), ((), ()))


def _dot(a, b):
    return jnp.dot(a, b, preferred_element_type=F32)


def _dot_nt(a, b):
    return lax.dot_general(a, b, NT_DIMS, preferred_element_type=F32)


def _split_bf16(a):
    hi = a.astype(BF16)
    return hi, (a - hi.astype(F32)).astype(BF16)


def _split_dot(a, b):
    hi, lo = _split_bf16(a)
    return _dot(hi, b) + _dot(lo, b)


def _rms_rows(x, gain):
    ms = jnp.mean(x * x, axis=-1, keepdims=True)
    return x * lax.rsqrt(ms + RMS_EPS) * gain


def _params(sem):
    return pltpu.CompilerParams(dimension_semantics=sem, vmem_limit_bytes=VMEM_LIMIT)


def _head_norm(y, norm_m, gain):
    ms = _split_dot(y * y, norm_m)
    return y * lax.rsqrt(ms + RMS_EPS) * gain


def _rope(y, c, s, lo_mask, half):
    sw = jnp.where(lo_mask, pltpu.roll(y, LANES - half, 1), pltpu.roll(y, half, 1))
    return y * c + sw * s


def _lane_iota(shape):
    return lax.broadcasted_iota(I32, shape, 1)


def _row_iota(shape):
    return lax.broadcasted_iota(I32, shape, 0)


def _rope_tables(pos, ftab, period, half):
    ang = pos * ftab
    lane = _lane_iota(ang.shape) % period
    c = jnp.cos(ang)
    s = jnp.sin(ang) * jnp.where(lane < half, -1.0, 1.0)
    return c, s


def _pv_operand(kv):
    head = jnp.where(_row_iota((PV_HEAD_ROWS, kv.shape[0])) == 0, 1.0, 0.0)
    return jnp.concatenate([head, kv.T[HEAD_DIM:, :]], axis=0).astype(BF16)


class _ColumnProjector:
    def __init__(self, xn, w_ref):
        self.xn, self.w_ref, self.groups = xn, w_ref, {}

    def __call__(self, j):
        g, u = divmod(j, PROJ_GROUP)
        if g not in self.groups:
            width = PROJ_GROUP * LANES
            lo = g * width
            hi = min(lo + width, self.w_ref.shape[1])
            self.groups[g] = _dot(self.xn, self.w_ref[:, lo:hi])
        return self.groups[g][:, u * LANES:(u + 1) * LANES]


def _kv_column(yc, nk, gain, c64k, s64k, lo64, first64):
    kn = jnp.where(first64, _head_norm(yc, nk, gain), yc)
    return _rope(kn, c64k, s64k, lo64, 8)


def _rope_trig_kernel(pos_ref, f64_ref, f32_ref, o_ref):
    pos = pos_ref[...]
    c64, s64 = _rope_tables(pos, f64_ref[...], 64, 8)
    c32, s32 = _rope_tables(pos, f32_ref[...], 32, 4)
    o_ref[...] = jnp.concatenate([c64, s64, c32, s32], axis=1)


def _rope_trig(pos2, f64, f32t, tm=1024):
    T = pos2.shape[0]
    return pl.pallas_call(
        _rope_trig_kernel, grid=(T // tm,),
        in_specs=[pl.BlockSpec((tm, 1), lambda i: (i, 0)),
                  pl.BlockSpec(f64.shape, lambda i: (0, 0)), pl.BlockSpec(f32t.shape, lambda i: (0, 0))],
        out_specs=pl.BlockSpec((tm, 4 * LANES), lambda i: (i, 0)),
        out_shape=jax.ShapeDtypeStruct((T, 4 * LANES), F32),
        compiler_params=_params(("parallel",)), name="rope_trig",
    )(pos2, f64, f32t)


def _ab_prep_kernel(x_ref, trig_ref, gmix_ref, w_ref, nq_ref, nk_ref,
                    gaq_ref, gbq_ref, gak_ref, gbk_ref,
                    aq_ref, bq_ref, bkv_ref, bkvt_ref, akv_ref, akvt_ref, iq_ref, ik_ref, iwt_ref, km_ref,
                    *, n_tiles):
    xn = _rms_rows(x_ref[...], gmix_ref[...]).astype(BF16)
    c64, s64, c32, s32 = [trig_ref[:, j * LANES:(j + 1) * LANES] for j in range(4)]
    lane = _lane_iota(c64.shape)
    lo64 = (lane % 64) < 8
    lo32 = (lane % 32) < 4
    first64 = lane < 64
    c64k = jnp.where(first64, c64, 1.0)
    s64k = jnp.where(first64, s64, 0.0)
    first32 = lane < 32
    c32k = jnp.where(first32, c32, 1.0)
    s32k = jnp.where(first32, s32, 0.0)
    nq = nq_ref[...]
    nk = nk_ref[...]

    col = _ColumnProjector(xn, w_ref)

    for j in range(4):
        q = _rope(_head_norm(col(j), nq, gaq_ref[...]), c64, s64, lo64, 8)
        aq_ref[2 * j] = q[:, :64].astype(BF16)
        aq_ref[2 * j + 1] = q[:, 64:].astype(BF16)
    for j in range(4):
        q = _rope(_head_norm(col(4 + j), nq, gbq_ref[...]), c64, s64, lo64, 8)
        bq_ref[2 * j] = q[:, :64].astype(BF16)
        bq_ref[2 * j + 1] = q[:, 64:].astype(BF16)
    blk_onehot = jnp.where(lane == HEAD_DIM + pl.program_id(0) % n_tiles, 1.0, 0.0)
    for h in range(8):
        kv = _kv_column(col(8 + h), nk, gbk_ref[...], c64k, s64k, lo64, first64)
        bkv_ref[h] = jnp.where(first64, kv, blk_onehot).astype(BF16)
        bkvt_ref[h] = _pv_operand(kv)
        km_ref[h:h + 1, :] = jnp.mean(kv, axis=0, keepdims=True)
    kv = _kv_column(col(16), nk, gak_ref[...], c64k, s64k, lo64, first64)
    akv_ref[...] = kv.astype(BF16)
    akvt_ref[...] = _pv_operand(kv)
    for j in range(2):
        q = _rope(col(17 + j), c32, s32, lo32, 4)
        for u in range(4):
            iq_ref[4 * j + u] = q[:, 32 * u:32 * (u + 1)].astype(BF16)
    yc = col(19)
    ik_ref[...] = _rope(yc, c32k, s32k, lo32, 4)[:, :32].astype(BF16)
    iwt_ref[...] = yc.T[32:40, :]


def _ab_prep(x2, trig, gmix, w, tabs, B, S):
    T = x2.shape[0]
    tm = ATT_T
    nt = S // tm
    n_cols = w.shape[1]
    nq, nk, gaq, gbq, gak, gbk = tabs

    def full(a):
        return pl.BlockSpec(a.shape, lambda i: (0,) * a.ndim)

    def hm(width, heads=8):
        return pl.BlockSpec((None, heads, tm, width), lambda i: (i // nt, 0, i % nt, 0))

    def tokm(width):
        return pl.BlockSpec((None, tm, width), lambda i: (i // nt, i % nt, 0))

    out_shape = (
        jax.ShapeDtypeStruct((B, 8, S, 64), BF16),
        jax.ShapeDtypeStruct((B, 8, S, 64), BF16),
        jax.ShapeDtypeStruct((B, 8, S, 128), BF16),
        jax.ShapeDtypeStruct((B, 8, nt, PV_ROWS, tm), BF16),
        jax.ShapeDtypeStruct((B, S, 128), BF16),
        jax.ShapeDtypeStruct((B, nt, PV_ROWS, tm), BF16),
        jax.ShapeDtypeStruct((B, 8, S, 32), BF16),
        jax.ShapeDtypeStruct((B, S, 32), BF16),
        jax.ShapeDtypeStruct((B, 8, S), F32),
        jax.ShapeDtypeStruct((T // tm, 8, 128), F32),
    )
    out_specs = (hm(64), hm(64), hm(128),
                 pl.BlockSpec((None, 8, None, PV_ROWS, tm), lambda i: (i // nt, 0, i % nt, 0, 0)),
                 tokm(128),
                 pl.BlockSpec((None, None, PV_ROWS, tm), lambda i: (i // nt, i % nt, 0, 0)),
                 hm(32), tokm(32),
                 pl.BlockSpec((None, 8, tm), lambda i: (i // nt, 0, i % nt)),
                 pl.BlockSpec((None, 8, 128), lambda i: (i, 0, 0)))
    in_specs = [pl.BlockSpec((tm, D_MODEL), lambda i: (i, 0)),
                pl.BlockSpec((tm, 4 * LANES), lambda i: (i, 0)),
                full(gmix), pl.BlockSpec((D_MODEL, n_cols), lambda i: (0, 0)),
                full(nq), full(nk), full(gaq), full(gbq), full(gak), full(gbk)]
    return pl.pallas_call(
        functools.partial(_ab_prep_kernel, n_tiles=nt), grid=(T // tm,), in_specs=in_specs, out_specs=out_specs,
        out_shape=out_shape, compiler_params=_params(("parallel",)), name="ab_prep",
    )(x2, trig, gmix, w, nq, nk, gaq, gbq, gak, gbk)


def _pad_q(q):
    return jnp.concatenate([q, jnp.zeros_like(q)], axis=1)


def _bias_lanes(q, rows):
    n, tq = rows.shape
    parts = [jnp.zeros((HEAD_DIM, tq), F32), rows]
    if n < HEAD_DIM:
        parts.append(jnp.zeros((HEAD_DIM - n, tq), F32))
    lanes = jnp.concatenate(parts, axis=0).T.astype(BF16)
    return jnp.where(_lane_iota(q.shape) < HEAD_DIM, q, lanes)


class _Flash:
    def __init__(self, m_ref, acc_ref, s_ref, cmax_ref, p_ref, tq):
        self.m_ref, self.acc_ref, self.tq = m_ref, acc_ref, tq
        self.s_ref, self.cmax_ref, self.p_ref = s_ref, cmax_ref, p_ref

    def reset(self):
        self.m_ref[...] = jnp.full(self.m_ref.shape, M_FLOOR, F32)
        self.acc_ref[...] = jnp.zeros(self.acc_ref.shape, F32)

    def _scores(self, buf, qs, kvs, biases):
        tq = self.tq
        for i in range(len(qs)):
            s = _dot_nt(kvs[i], qs[i])
            if biases[i] is not None:
                s = s + biases[i]
            self.s_ref[buf, i, :s.shape[0], :] = s
            self.cmax_ref[buf, :, i * tq:(i + 1) * tq] = jnp.max(s, axis=0, keepdims=True)

    def update(self, qs, kvs, kvts, biases):
        self._scores(0, qs, kvs, biases)
        self._finish(0, kvts)

    def run(self, qs, count, operands):
        def scores(c, buf):
            kvs, _, biases = operands(c)
            self._scores(buf, qs, kvs, biases)

        def finish(c, buf):
            self._finish(buf, operands(c)[1])

        last = jnp.maximum(count - 1, 0)
        scores(0, 0)

        def two_chunks(pp, carry):
            c = 2 * pp
            scores(c + 1, 1)
            finish(c, 0)
            scores(jnp.minimum(c + 2, last), 0)
            finish(c + 1, 1)
            return carry

        lax.fori_loop(0, count // 2, two_chunks, 0)

        @pl.when(count % 2 == 1)
        def _():
            finish(count - 1, 0)

    def _finish(self, buf, kvts):
        n = len(kvts)
        tq = self.tq
        kc = sum(kvt.shape[1] for kvt in kvts[0])
        alphas = []
        for i in range(n):
            cols = slice(i * tq, (i + 1) * tq)
            m = self.m_ref[:, cols]
            m_new = jnp.maximum(m, self.cmax_ref[buf, :, cols])
            p = jnp.exp2(self.s_ref[buf, i, :kc, :] - m_new)
            alpha = jnp.exp2(m - m_new)
            self.m_ref[:, cols] = m_new
            self.p_ref[i, :kc, :] = p.astype(BF16)
            alphas.append(alpha)
        for i in range(n):
            cols = slice(i * tq, (i + 1) * tq)
            pv, r0 = None, 0
            for kvt in kvts[i]:
                part = _dot(kvt, self.p_ref[i, r0:r0 + kvt.shape[1], :])
                pv = part if pv is None else pv + part
                r0 += kvt.shape[1]
            self.acc_ref[:, cols] = alphas[i] * self.acc_ref[:, cols] + pv

    def result(self, slot):
        cols = slice(slot * self.tq, (slot + 1) * self.tq)
        acc = self.acc_ref[:, cols]
        return acc / jnp.maximum(acc[0:1, :], TINY)


def _flash_scratch(n_slots, tq, kc):
    return [pltpu.VMEM((1, n_slots * tq), F32), pltpu.VMEM((PV_ROWS, n_slots * tq), F32),
            pltpu.VMEM((2, n_slots, kc, tq), F32), pltpu.VMEM((2, 1, n_slots * tq), F32),
            pltpu.VMEM((n_slots, kc, tq), BF16)]


def _softmax_direct(qs, kv, kvts, bias):
    scores = [_dot_nt(kv, q) for q in qs]
    probs = []
    for s in scores:
        s = s + bias
        m = jnp.maximum(jnp.max(s, axis=0, keepdims=True), M_FLOOR)
        probs.append(jnp.exp2(s - m))
    inv_ls, outs = [], []
    for p in probs:
        pb = p.astype(BF16)
        o, r0 = None, 0
        for kvt in kvts:
            part = _dot(kvt, pb[r0:r0 + kvt.shape[1]])
            o = part if o is None else o + part
            r0 += kvt.shape[1]
        outs.append(o)
        inv_ls.append(1.0 / jnp.maximum(o[0:1, :], TINY))
    return probs, inv_ls, outs


def _causal_bias(t):
    return jnp.where(_row_iota((t, t)) <= _lane_iota((t, t)), 0.0, MASK_BIAS)


def _store_heads(o_ref, heads_t):
    for u in range(len(heads_t) // 2):
        pair = jnp.concatenate([heads_t[2 * u][PV_HEAD_ROWS:, :], heads_t[2 * u + 1][PV_HEAD_ROWS:, :]], axis=0)
        o_ref[:, u * LANES:(u + 1) * LANES] = pair.T.astype(o_ref.dtype)


def _rank_select_t(v, n_valid, n_top):
    n = v.shape[0]
    row = _row_iota(v.shape)
    rank = jnp.zeros(v.shape, F32)
    for m in range(n):
        vm = v[m:m + 1, :]
        ahead = (vm > v) | ((vm == v) & (m < row))
        if n_valid is not None:
            ahead = ahead & (m < n_valid)
        rank = rank + jnp.where(ahead, 1.0, 0.0)
    sel = rank < n_top
    if n_valid is not None:
        sel = sel & (row < n_valid)
    return jnp.where(sel, 1.0, 0.0)


def _dsa_kernel(iq_ref, iwt_ref, ik_ref, aq_ref, akv_ref, akvt_ref, o_ref,
                sk_ref, half_ref, bias_ref, xcut_ref, *flash_refs, k_top, index_bits):
    t = ATT_T
    i = pl.program_id(1)
    n_ch = i + 1
    kio = _row_iota((t, t))
    qio = _lane_iota((t, t))

    def causal(c):
        return (c - i) * t + kio <= qio

    def score_chunk(c):
        k0 = pl.multiple_of(c * t, t)
        ikc = ik_ref[pl.ds(k0, t), :]
        sc = jnp.zeros((t, t), F32)
        for h in range(DSA_IDX_HEADS):
            logit = _dot_nt(ikc, iq_ref[h])
            sc = sc + iwt_ref[h:h + 1, :] * jnp.maximum(logit, 0.0)
        sc = jnp.where(sc == 0.0, 0.0, sc)
        bits = pltpu.bitcast(sc, I32)
        key = bits ^ ((bits >> 31) & 0x7FFFFFFF)
        key = jnp.where(causal(c), key, INT_MIN)
        sk_ref[c] = key
        half_ref[c] = (key >> 16).astype(I16)

    def score_pair(cc, carry):
        score_chunk(2 * cc)
        score_chunk(2 * cc + 1)
        return carry

    lax.fori_loop(0, (n_ch + 1) // 2, score_pair, 0)

    def count(pred):
        def body(c, acc8):
            ind = jnp.where(pred(sk_ref[c], c), 1.0, 0.0)
            return acc8 + ind.reshape(-1, SUBLANES, t).sum(axis=0)
        acc8 = lax.fori_loop(0, n_ch, body, jnp.zeros((SUBLANES, t), F32))
        return jnp.sum(acc8, axis=0, keepdims=True)

    def count_half(cand):
        rows = 2 * SUBLANES

        def body(cc, acc):
            parts = []
            for c in (2 * cc, 2 * cc + 1):
                ind = jnp.where(half_ref[c] >= cand, jnp.bfloat16(1), jnp.bfloat16(0))
                parts += [ind[rows * j:rows * (j + 1), :] for j in range(t // rows)]
            while len(parts) > 1:
                parts = [parts[2 * j] + parts[2 * j + 1] for j in range(len(parts) // 2)]
            return acc + parts[0].astype(F32)
        acc = lax.fori_loop(0, (n_ch + 1) // 2, body, jnp.zeros((rows, t), F32))
        return jnp.sum(acc, axis=0, keepdims=True)

    def half_search(n_all):
        def bit_step(b, carry):
            v, n_ge_v = carry
            cand = v + lax.shift_left(jnp.int32(1), 15 - b)
            n_ge_cand = count_half(cand.astype(I16))
            ok = n_ge_cand >= k_top
            return jnp.where(ok, cand, v), jnp.where(ok, n_ge_cand, n_ge_v)
        return lax.fori_loop(0, 16, bit_step, (jnp.full((1, t), -(2 ** 15), I32), n_all))

    thr_hi, n_ge_hi = half_search(jnp.full((1, t), t * n_ch, I32).astype(F32))

    def low_half_chunk(c, carry):
        key = sk_ref[c]
        hi = key >> 16
        lo = (key & 0xFFFF) - 2 ** 15
        half_ref[c] = jnp.where(hi > thr_hi, 2 ** 15 - 1, jnp.where(hi < thr_hi, -(2 ** 15), lo)).astype(I16)
        return carry

    lax.fori_loop(0, n_ch, low_half_chunk, 0)
    thr_lo, n_ge = half_search(n_ge_hi)
    thr = lax.shift_left(thr_hi, 16) + (thr_lo + 2 ** 15)

    xcut_ref[...] = jnp.full((1, t), 2 ** 30, I32)

    @pl.when(jnp.max(n_ge) > k_top)
    def _():
        need = k_top - count(lambda blk, c: blk > thr)

        def x_step(b, x):
            cand = x + lax.shift_left(jnp.int32(1), index_bits - 1 - b)
            ties_below = count(lambda blk, c: (blk == thr) & (c * t + kio < cand))
            return jnp.where(ties_below <= need, cand, x)
        xcut_ref[...] = lax.fori_loop(0, index_bits, x_step, jnp.zeros((1, t), I32))

    xcut = xcut_ref[...]

    n_pairs = (n_ch + 1) // 2

    def bias_chunk(c, carry):
        blk = sk_ref[jnp.minimum(c, i)]
        keep = (blk > thr) | ((blk == thr) & (c * t + kio < xcut))
        bias_ref[c] = jnp.where(keep & causal(c), 0.0, MASK_BIAS)
        return carry

    lax.fori_loop(0, 2 * n_pairs, bias_chunk, 0)

    flash = _Flash(*flash_refs, t)
    qs = [_pad_q(aq_ref[h]) for h in range(DSA_HEADS)]
    flash.reset()

    n = DSA_HEADS

    def att_pair(cc):
        c0 = 2 * cc
        k0 = pl.multiple_of(c0 * t, 2 * t)
        kv = akv_ref[pl.ds(k0, 2 * t), :]
        bias = jnp.concatenate([bias_ref[c0], bias_ref[c0 + 1]], axis=0)
        return [kv] * n, [[akvt_ref[c0], akvt_ref[c0 + 1]]] * n, [bias] * n

    flash.run(qs, n_pairs, att_pair)
    _store_heads(o_ref, [flash.result(h) for h in range(DSA_HEADS)])


def _dsa_attention(iq, iwt, ik, aq, akv, akvt):
    B, _, S, _ = aq.shape
    t = ATT_T
    nt = S // t
    k_top = min(DSA_TOPK, S // 4)
    in_specs = [
        pl.BlockSpec((None, 8, t, 32), lambda b, i: (b, 0, i, 0)),
        pl.BlockSpec((None, 8, t), lambda b, i: (b, 0, i)),
        pl.BlockSpec((None, S, 32), lambda b, i: (b, 0, 0)),
        pl.BlockSpec((None, 8, t, 64), lambda b, i: (b, 0, i, 0)),
        pl.BlockSpec((None, S, 128), lambda b, i: (b, 0, 0)),
        pl.BlockSpec((None, nt, PV_ROWS, t), lambda b, i: (b, 0, 0, 0)),
    ]
    return pl.pallas_call(
        functools.partial(_dsa_kernel, k_top=k_top, index_bits=S.bit_length()),
        grid=(B, nt), in_specs=in_specs,
        out_specs=pl.BlockSpec((None, t, 512), lambda b, i: (b, i, 0)),
        out_shape=jax.ShapeDtypeStruct((B, S, 512), BF16),
        scratch_shapes=[pltpu.VMEM((nt, t, t), I32), pltpu.VMEM((nt, t, t), I16), pltpu.VMEM((nt, t, t), F32),
                        pltpu.VMEM((1, t), I32)] + _flash_scratch(DSA_HEADS, t, 2 * t),
        compiler_params=_params(("parallel", "parallel")), name="dsa_attention",
    )(iq, iwt, ik, aq, akv, akvt)


MOBA_HPS = 8


def _moba_kernel(q_ref, kv_ref, kvt_ref, km_ref, o_ref, *flash_refs, n_top):
    t = ATT_T
    own = pl.program_id(2)
    causal = _causal_bias(t)
    flash = _Flash(*flash_refs, t)
    qs = []
    for hh in range(MOBA_HPS):
        q = _pad_q(q_ref[hh])
        km_hi, km_lo = _split_bf16(km_ref[hh])
        gate = _dot_nt(km_hi, q) + _dot_nt(km_lo, q)
        keep = _rank_select_t(gate, own, n_top)
        keep = jnp.where(_row_iota(keep.shape) == own, 1.0, keep)
        qs.append(_bias_lanes(q, (keep - 1.0) * (-MASK_BIAS)))
    flash.reset()

    def operands(cc):
        n0 = 2 * cc
        k0 = pl.multiple_of(n0 * t, 2 * t)
        heads = range(MOBA_HPS)
        return (n0, [kv_ref[hh, pl.ds(k0, 2 * t), :] for hh in heads],
                [[kvt_ref[hh, n0], kvt_ref[hh, n0 + 1]] for hh in heads])

    def past_pair(cc):
        _, kvs, kvts = operands(cc)
        return kvs, kvts, [None] * MOBA_HPS

    flash.run(qs, own // 2, past_pair)
    heads = range(MOBA_HPS)

    @pl.when(own % 2 == 0)
    def _():
        k0 = pl.multiple_of(own * t, t)
        flash.update(qs, [kv_ref[hh, pl.ds(k0, t), :] for hh in heads],
                     [[kvt_ref[hh, own]] for hh in heads], [causal] * MOBA_HPS)

    @pl.when(own % 2 == 1)
    def _():
        _, kvs, kvts = operands(own // 2)
        bias = jnp.concatenate([jnp.zeros((t, t), F32), causal], axis=0)
        flash.update(qs, kvs, kvts, [bias] * MOBA_HPS)

    _store_heads(o_ref, [flash.result(hh) for hh in range(MOBA_HPS)])


def _moba_attention(bq, bkv, bkvt, kmean):
    B, H, S, _ = bq.shape
    t = ATT_T
    hps = MOBA_HPS
    n_blk = S // MOBA_BLOCK
    assert t == MOBA_BLOCK and n_blk % 2 == 0 and H % hps == 0
    n_top = max(1, min(MOBA_TOPK, n_blk - 1))
    in_specs = [
        pl.BlockSpec((None, hps, t, 64), lambda b, h, i: (b, h, i, 0)),
        pl.BlockSpec((None, hps, S, 128), lambda b, h, i: (b, h, 0, 0)),
        pl.BlockSpec((None, hps, n_blk, PV_ROWS, t), lambda b, h, i: (b, h, 0, 0, 0)),
        pl.BlockSpec((None, hps, n_blk, 128), lambda b, h, i: (b, h, 0, 0)),
    ]
    return pl.pallas_call(
        functools.partial(_moba_kernel, n_top=n_top), grid=(B, H // hps, S // t), in_specs=in_specs,
        out_specs=pl.BlockSpec((None, t, hps * 64), lambda b, h, i: (b, i, h)),
        out_shape=jax.ShapeDtypeStruct((B, S, H * 64), BF16),
        scratch_shapes=_flash_scratch(hps, t, 2 * t),
        compiler_params=_params(("parallel", "parallel", "parallel")), name="moba_attention",
    )(bq, bkv, bkvt, kmean)


def _lane_group_norm(y, gain, width):
    outs = []
    for j in range(y.shape[1] // width):
        yc = y[:, j * width:(j + 1) * width]
        outs.append(_rms_rows(yc, gain))
    return jnp.concatenate(outs, axis=1)


def _mem_kv_kernel(m_ref, g_ref, w_ref, gk_ref, o_ref):
    mn = _rms_rows(m_ref[...], g_ref[...]).astype(BF16)
    y = _dot(mn, w_ref[...])
    hw = MEM_HEADS * MEM_HEAD_DIM
    k = _lane_group_norm(y[:, :hw], gk_ref[...], MEM_HEAD_DIM)
    o_ref[...] = jnp.concatenate([k, y[:, hw:]], axis=1).astype(BF16)


def _mem_kv(mem, g, w, gk):
    B, M, _ = mem.shape
    n = w.shape[1]
    return pl.pallas_call(
        _mem_kv_kernel, grid=(B,),
        in_specs=[pl.BlockSpec((None, M, D_MODEL), lambda b: (b, 0, 0)),
                  pl.BlockSpec(g.shape, lambda b: (0, 0)),
                  pl.BlockSpec(w.shape, lambda b: (0, 0)),
                  pl.BlockSpec(gk.shape, lambda b: (0, 0))],
        out_specs=pl.BlockSpec((None, M, n), lambda b: (b, 0, 0)),
        out_shape=jax.ShapeDtypeStruct((B, M, n), BF16),
        compiler_params=_params(("parallel",)), name="mem_kv",
    )(mem, g, w, gk)


def _mem_attend(x, g_ref, wq_ref, gq_ref, kv_ref, wo_ref):
    xn = _rms_rows(x, g_ref[...]).astype(BF16)
    q = _lane_group_norm(_dot(xn, wq_ref[...]), gq_ref[...], MEM_HEAD_DIM).astype(BF16)
    hw = MEM_HEADS * MEM_HEAD_DIM
    scale = MEM_HEAD_DIM ** -0.5
    outs = []
    for h in range(MEM_HEADS):
        cols = slice(h * MEM_HEAD_DIM, (h + 1) * MEM_HEAD_DIM)
        k = kv_ref[:, cols]
        v = kv_ref[:, hw + h * MEM_HEAD_DIM:hw + (h + 1) * MEM_HEAD_DIM]
        s = _dot_nt(q[:, cols], k) * scale
        p = jnp.exp(s - jnp.max(s, axis=-1, keepdims=True))
        p = p / jnp.sum(p, axis=-1, keepdims=True)
        outs.append(_dot(p.astype(BF16), v))
    o = jnp.concatenate(outs, axis=1).astype(BF16)
    return x + _dot(o, wo_ref[...])


def _post_mixer_kernel(*refs, n_in):
    a_refs = refs[:n_in]
    w_refs = refs[n_in:2 * n_in]
    (x_ref, gm_ref, wq_ref, gq_ref, kv_ref, wo_ref, gf_ref, wg_ref, wu_ref, wd_ref,
     o_ref, xn_ref, acc_ref) = refs[2 * n_in:]
    j = pl.program_id(1)

    @pl.when(j == 0)
    def _():
        x = x_ref[...]
        for a_ref, w_ref in zip(a_refs, w_refs):
            x = x + _dot(a_ref[...], w_ref[...])
        x = _mem_attend(x, gm_ref, wq_ref, gq_ref, kv_ref, wo_ref)
        xn_ref[...] = _rms_rows(x, gf_ref[...]).astype(BF16)
        acc_ref[...] = x

    xn = xn_ref[...]
    gate = _dot(xn, wg_ref[...])
    up = _dot(xn, wu_ref[...])
    act = (gate * jax.nn.sigmoid(gate) * up).astype(BF16)
    acc_ref[...] += _dot(act, wd_ref[...])

    @pl.when(j == pl.num_programs(1) - 1)
    def _():
        o_ref[...] = acc_ref[...]


def _post_mixer(parts, weights, x2, g_mem, wq, gq, kv, wo, g_ffn, wg, wu, wd, S, tm=512, n_split=1):
    T = x2.shape[0]
    nt = S // tm
    tf = D_FF // n_split
    n_in = len(parts)
    M, n = kv.shape[1], kv.shape[2]
    once = pl.Buffered(1)
    ffn_mode = once if n_split == 1 else None

    def const(a):
        return pl.BlockSpec(a.shape, lambda i, j: (0,) * a.ndim, pipeline_mode=once)

    in_specs = ([pl.BlockSpec((tm, p.shape[1]), lambda i, j: (i, 0)) for p in parts]
                + [const(w) for w in weights]
                + [pl.BlockSpec((tm, D_MODEL), lambda i, j: (i, 0)),
                   const(g_mem), const(wq), const(gq),
                   pl.BlockSpec((None, M, n), lambda i, j: (i // nt, 0, 0)),
                   const(wo), const(g_ffn),
                   pl.BlockSpec((D_MODEL, tf), lambda i, j: (0, j), pipeline_mode=ffn_mode),
                   pl.BlockSpec((D_MODEL, tf), lambda i, j: (0, j), pipeline_mode=ffn_mode),
                   pl.BlockSpec((tf, D_MODEL), lambda i, j: (j, 0), pipeline_mode=ffn_mode)])
    return pl.pallas_call(
        functools.partial(_post_mixer_kernel, n_in=n_in), grid=(T // tm, n_split), in_specs=in_specs,
        out_specs=pl.BlockSpec((tm, D_MODEL), lambda i, j: (i, 0)),
        out_shape=jax.ShapeDtypeStruct((T, D_MODEL), F32),
        scratch_shapes=[pltpu.VMEM((tm, D_MODEL), BF16), pltpu.VMEM((tm, D_MODEL), F32)],
        compiler_params=_params(("parallel", "arbitrary")), name="post_mixer",
    )(*parts, *weights, x2, g_mem, wq, gq, kv, wo, g_ffn, wg, wu, wd)


def _nsa_prep_kernel(x_ref, trig_ref, gmix_ref, w_ref, nq_ref, nk_ref,
                     gq_ref, gks_ref, gkw_ref,
                     qc_ref, qr_ref, kvs_ref, kvst_ref, kvw_ref, kvwt_ref, kc_ref, vc_ref, gtt_ref,
                     stage_ref, *, n_tiles):
    xn = _rms_rows(x_ref[...], gmix_ref[...]).astype(BF16)
    c64, s64 = trig_ref[:, :LANES], trig_ref[:, LANES:]
    lane = _lane_iota(c64.shape)
    lo64 = (lane % 64) < 8
    first64 = lane < 64
    c64k = jnp.where(first64, c64, 1.0)
    s64k = jnp.where(first64, s64, 0.0)
    nq = nq_ref[...]
    nk = nk_ref[...]

    col = _ColumnProjector(xn, w_ref)

    for j in range(8):
        qn = _head_norm(col(j), nq, gq_ref[...])
        qr = _rope(qn, c64, s64, lo64, 8)
        qc_ref[2 * j] = qn[:, :64].astype(BF16)
        qc_ref[2 * j + 1] = qn[:, 64:].astype(BF16)
        qr_ref[2 * j] = qr[:, :64].astype(BF16)
        qr_ref[2 * j + 1] = qr[:, 64:].astype(BF16)
    tile = pl.program_id(0) % n_tiles
    sel_blk = tile * (ATT_T // NSA_SEL_LEN) + lax.shift_right_logical(
        _row_iota(c64.shape), NSA_SEL_LEN.bit_length() - 1)
    blk_onehot = jnp.where(lane == HEAD_DIM + sel_blk, 1.0, 0.0)
    for g in range(NSA_GROUPS):
        kv = _kv_column(col(8 + g), nk, gks_ref[...], c64k, s64k, lo64, first64)
        kvs_ref[g] = jnp.where(first64, kv, blk_onehot).astype(BF16)
        kvst_ref[g] = _pv_operand(kv)
        kv = _kv_column(col(12 + g), nk, gkw_ref[...], c64k, s64k, lo64, first64)
        kvw_ref[g] = kv.astype(BF16)
        kvwt_ref[g] = _pv_operand(kv)
    stride = NSA_CMP_STRIDE
    rows = stage_ref.shape[0] // stride
    for out_ref, first in ((kc_ref, 16), (vc_ref, 18)):
        for c in range(2):
            stage_ref[...] = col(first + c)
            for u in range(0, stride, 2):
                pair = [stage_ref[pl.ds(u + v, rows, stride=stride), :] for v in range(2)]
                for h in range(2):
                    halves = [p[:, h * HEAD_DIM:(h + 1) * HEAD_DIM] for p in pair]
                    out_ref[2 * c + h, :, u * HEAD_DIM:(u + 2) * HEAD_DIM] = jnp.concatenate(halves, axis=1)
    gates_t = jax.nn.sigmoid(col(20)).T
    for g in range(NSA_GROUPS):
        gtt_ref[g] = gates_t[12 * g:12 * (g + 1), :]


def _nsa_prep(x2, trig, gmix, w, tabs, B, S):
    T = x2.shape[0]
    tm = ATT_T
    nt = S // tm
    nq, nk, gq, gks, gkw = tabs

    def full(a):
        return pl.BlockSpec(a.shape, lambda i: (0,) * a.ndim)

    def hm(width, heads):
        return pl.BlockSpec((None, heads, tm, width), lambda i: (i // nt, 0, i % nt, 0))

    def hmt(heads):
        return pl.BlockSpec((None, heads, None, PV_ROWS, tm), lambda i: (i // nt, 0, i % nt, 0, 0))

    def tokm(width):
        return pl.BlockSpec((None, tm, width), lambda i: (i // nt, i % nt, 0))

    out_shape = (
        jax.ShapeDtypeStruct((B, 16, S, 64), BF16),
        jax.ShapeDtypeStruct((B, 16, S, 64), BF16),
        jax.ShapeDtypeStruct((B, 4, S, 128), BF16),
        jax.ShapeDtypeStruct((B, 4, nt, PV_ROWS, tm), BF16),
        jax.ShapeDtypeStruct((B, 4, S, 128), BF16),
        jax.ShapeDtypeStruct((B, 4, nt, PV_ROWS, tm), BF16),
        jax.ShapeDtypeStruct((B, 4, S // 16, 1024), F32),
        jax.ShapeDtypeStruct((B, 4, S // 16, 1024), F32),
        jax.ShapeDtypeStruct((B, 4, 12, S), F32),
    )
    rows16 = pl.BlockSpec((None, 4, tm // 16, 1024), lambda i: (i // nt, 0, i % nt, 0))
    out_specs = (hm(64, 16), hm(64, 16), hm(128, 4), hmt(4), hm(128, 4), hmt(4), rows16, rows16,
                 pl.BlockSpec((None, 4, 12, tm), lambda i: (i // nt, 0, 0, i % nt)))
    in_specs = [pl.BlockSpec((tm, D_MODEL), lambda i: (i, 0)),
                pl.BlockSpec((tm, 2 * LANES), lambda i: (i, 0)),
                full(gmix), full(w), full(nq), full(nk), full(gq), full(gks), full(gkw)]
    return pl.pallas_call(
        functools.partial(_nsa_prep_kernel, n_tiles=nt), grid=(T // tm,), in_specs=in_specs, out_specs=out_specs,
        out_shape=out_shape, scratch_shapes=[pltpu.VMEM((tm, LANES), F32)],
        compiler_params=_params(("parallel",)), name="nsa_prep",
    )(x2, trig, gmix, w, nq, nk, gq, gks, gkw)


def _compress_one(x16, pa, pb, w1a, w1b, w2):
    n16 = x16.shape[0]
    h_a = _dot((x16 + pa).astype(BF16), w1a)
    h_b = _dot((x16 + pb).astype(BF16), w1b)
    pre = h_a + pltpu.roll(h_b, n16 - 1, 0)
    act = pre * jax.nn.sigmoid(pre)
    return _dot(act.astype(BF16), w2)


def _compress_kernel(xk_ref, xv_ref, pk_ref, pv_ref, w1k_ref, w1v_ref, w2k_ref, w2v_ref, gk_ref,
                     o_ref, ot_ref):
    half = w1k_ref.shape[0] // 2
    k = _compress_one(xk_ref[...], pk_ref[0:1, :], pk_ref[1:2, :],
                      w1k_ref[:half, :], w1k_ref[half:, :], w2k_ref[...])
    k = _rms_rows(k, gk_ref[...])
    v = _compress_one(xv_ref[...], pv_ref[0:1, :], pv_ref[1:2, :],
                      w1v_ref[:half, :], w1v_ref[half:, :], w2v_ref[...])
    kv = jnp.concatenate([k, v], axis=1)
    o_ref[...] = kv.astype(BF16)
    ot_ref[...] = _pv_operand(kv)


def _compress(xk16, xv16, pk, pv, w1k, w1v, w2k, w2v, gk):
    B, G, n16, width = xk16.shape

    def full(a):
        return pl.BlockSpec(a.shape, lambda b, g: (0,) * a.ndim)

    xspec = pl.BlockSpec((None, None, n16, width), lambda b, g: (b, g, 0, 0))
    return pl.pallas_call(
        _compress_kernel, grid=(B, G),
        in_specs=[xspec, xspec, full(pk), full(pv), full(w1k), full(w1v), full(w2k), full(w2v), full(gk)],
        out_specs=(pl.BlockSpec((None, None, n16, 128), lambda b, g: (b, g, 0, 0)),
                   pl.BlockSpec((None, None, PV_ROWS, n16), lambda b, g: (b, g, 0, 0))),
        out_shape=(jax.ShapeDtypeStruct((B, G, n16, 128), BF16),
                   jax.ShapeDtypeStruct((B, G, PV_ROWS, n16), BF16)),
        compiler_params=_params(("parallel", "parallel")), name="nsa_compress",
    )(xk16, xv16, pk, pv, w1k, w1v, w2k, w2v, gk)


NSA_GPS = 4


def _nsa_tile_masks(i, n16, n_sel, n_cmp, n_top):
    t = ATT_T
    t0 = i * t
    n_id = _row_iota((n16, t))
    q_id = t0 + _lane_iota((n16, t))
    cmp_visible = (n_id < n_cmp) & (n_id * NSA_CMP_STRIDE + (NSA_CMP_LEN - 1) <= q_id)
    b_id = _row_iota((n_sel, n16)) * NSA_SEL_LEN
    r_id = _lane_iota((n_sel, n16)) * NSA_CMP_STRIDE
    cover_t = ((r_id < b_id + NSA_SEL_LEN) & (r_id + NSA_CMP_LEN > b_id)
               & (_lane_iota((n_sel, n16)) < n_cmp))
    n_wc = NSA_WINDOW // t + 1
    cw = jnp.maximum(i - (n_wc - 1), 0)
    dist = (i - cw) * t + _lane_iota((n_wc * t, t)) - _row_iota((n_wc * t, t))
    blk = _row_iota((n_sel, t))
    cur = lax.shift_right_logical(t0 + _lane_iota((n_sel, t)), NSA_SEL_LEN.bit_length() - 1)
    return dict(
        bias_c=jnp.where(cmp_visible, 0.0, MASK_BIAS),
        cover_t=jnp.where(cover_t, 1.0, 0.0).astype(BF16),
        n_wc=n_wc, cw=cw, bias_w=jnp.where((dist >= 0) & (dist < NSA_WINDOW), 0.0, MASK_BIAS),
        forced=(blk == 0) | (blk == cur) | (blk == cur - 1), visible_blk=blk <= cur,
        n_wanted=jnp.minimum(cur[0:1, :] + 1, n_top).astype(F32))


def _nsa_front(qc_ref, qr_ref, kvc_ref, kvct_ref, kvw_ref, kvwt_ref, gtt_ref, part_ref, masks):
    t = ATT_T
    HG = NSA_HEADS // NSA_GROUPS
    n16 = kvc_ref.shape[0]

    p_sum = jnp.zeros((n16, t), F32)
    o_c = []
    probs, inv_ls, outs = _softmax_direct([_pad_q(qc_ref[j]) for j in range(HG)], kvc_ref[...],
                                          [kvct_ref[...]], masks["bias_c"])
    for j in range(HG):
        p_sum = p_sum + probs[j] * inv_ls[j]
        o_c.append(outs[j] * inv_ls[j])

    p_hi, p_lo = _split_bf16(p_sum)
    imp = _dot(masks["cover_t"], p_hi) + _dot(masks["cover_t"], p_lo)

    qs = [_pad_q(qr_ref[j]) for j in range(HG)]
    n_wc, cw = masks["n_wc"], masks["cw"]
    kw0 = pl.multiple_of(cw * t, t)
    _, inv_lw, out_w = _softmax_direct(qs, kvw_ref[pl.ds(kw0, n_wc * t), :],
                                       [kvwt_ref[cw + u] for u in range(n_wc)], masks["bias_w"])
    for j in range(HG):
        part_ref[:, j * t:(j + 1) * t] = (gtt_ref[3 * j:3 * j + 1, :] * o_c[j]
                                          + gtt_ref[3 * j + 2:3 * j + 3, :] * (out_w[j] * inv_lw[j]))

    imp = jnp.where(masks["forced"], NSA_FORCE, imp)
    return qs, jnp.where(masks["visible_blk"], imp, NEG_INF)


RANK_SEGMENT = 16


def _count_larger(imp, acc, rows):
    for m in rows:
        acc = acc + jnp.where(imp[m:m + 1, :] > imp, 1.0, 0.0)
    return acc


def _nsa_kernel(qc_ref, qr_ref, kvc_ref, kvct_ref, kvs_ref, kvst_ref, kvw_ref, kvwt_ref, gtt_ref,
                o_ref, sel_ref, part_ref, *flash_refs, n_cmp, n_top):
    t = ATT_T
    HG = NSA_HEADS // NSA_GROUPS
    n_slots = NSA_GPS * HG
    i = pl.program_id(2)

    masks = _nsa_tile_masks(i, kvc_ref.shape[1], sel_ref.shape[1], n_cmp, n_top)
    n_sel = sel_ref.shape[1]
    qs, imps = [], []
    for g in range(NSA_GPS):
        heads_g = pl.ds(g * HG, HG)
        q_g, imp_g = _nsa_front(
            qc_ref.at[heads_g], qr_ref.at[heads_g], kvc_ref.at[g], kvct_ref.at[g], kvw_ref.at[g],
            kvwt_ref.at[g], gtt_ref.at[g], part_ref.at[:, pl.ds(g * HG * t, HG * t)], masks)
        qs += q_g
        imps.append(imp_g)
        sel_ref[g] = _count_larger(imp_g, jnp.zeros((n_sel, t), F32), range(RANK_SEGMENT))

    for k in range(1, n_sel // RANK_SEGMENT):
        @pl.when((i + 1) * (t // NSA_SEL_LEN) > k * RANK_SEGMENT)
        def _():
            for g in range(NSA_GPS):
                sel_ref[g] = _count_larger(imps[g], sel_ref[g],
                                           range(k * RANK_SEGMENT, (k + 1) * RANK_SEGMENT))

    miss = None
    for g in range(NSA_GPS):
        sel_fast = sel_ref[g] < n_top
        n_picked = jnp.sum(jnp.where(sel_fast & masks["visible_blk"], 1.0, 0.0), axis=0, keepdims=True)
        sel_ref[g] = jnp.where(sel_fast, 0.0, MASK_BIAS)
        miss_g = jnp.abs(n_picked - masks["n_wanted"])
        miss = miss_g if miss is None else jnp.maximum(miss, miss_g)

    @pl.when(jnp.max(miss) > 0.0)
    def _():
        for g in range(NSA_GPS):
            sel_ref[g] = (_rank_select_t(imps[g], None, n_top) - 1.0) * (-MASK_BIAS)

    qs_sel = [_bias_lanes(qs[s], sel_ref[s // HG]) for s in range(n_slots)]
    flash = _Flash(*flash_refs, t)
    flash.reset()

    def sel_operands(c):
        k0 = pl.multiple_of(c * t, t)
        kvs = [kvs_ref[s // HG, pl.ds(k0, t), :] for s in range(n_slots)]
        kvts = [[kvst_ref[s // HG, c]] for s in range(n_slots)]
        return kvs, kvts

    def past_chunk(c):
        kvs, kvts = sel_operands(c)
        return kvs, kvts, [None] * n_slots

    flash.run(qs_sel, i, past_chunk)
    kvs, kvts = sel_operands(i)
    flash.update(qs_sel, kvs, kvts, [_causal_bias(t)] * n_slots)

    heads = []
    for s in range(n_slots):
        g, j = divmod(s, HG)
        heads.append(part_ref[:, s * t:(s + 1) * t] + gtt_ref[g, 3 * j + 1:3 * j + 2, :] * flash.result(s))
    _store_heads(o_ref, heads)


def _nsa_attention(qc, qr, kvc, kvct, kvs, kvst, kvw, kvwt, gates_t):
    B, H, S, _ = qc.shape
    G = NSA_GROUPS
    HG = H // G
    t = ATT_T
    nt = S // t
    n16 = kvc.shape[2]
    n_cmp = (S - NSA_CMP_LEN) // NSA_CMP_STRIDE + 1
    n_sel = S // NSA_SEL_LEN
    n_top = min(NSA_SEL_TOPK, n_sel)
    gps = NSA_GPS
    assert G % gps == 0 and nt % 2 == 0 and n_sel <= HEAD_DIM and S >= (NSA_WINDOW // t + 1) * t
    qspec = pl.BlockSpec((None, gps * HG, t, 64), lambda b, g, i: (b, g, i, 0))
    once = pl.Buffered(1)
    kvspec = pl.BlockSpec((None, gps, S, 128), lambda b, g, i: (b, g, 0, 0), pipeline_mode=once)
    kvtspec = pl.BlockSpec((None, gps, nt, PV_ROWS, t), lambda b, g, i: (b, g, 0, 0, 0), pipeline_mode=once)
    in_specs = [qspec, qspec,
                pl.BlockSpec((None, gps, n16, 128), lambda b, g, i: (b, g, 0, 0)),
                pl.BlockSpec((None, gps, PV_ROWS, n16), lambda b, g, i: (b, g, 0, 0)),
                kvspec, kvtspec, kvspec, kvtspec,
                pl.BlockSpec((None, gps, 12, t), lambda b, g, i: (b, g, 0, i))]
    return pl.pallas_call(
        functools.partial(_nsa_kernel, n_cmp=n_cmp, n_top=n_top), grid=(B, G // gps, nt), in_specs=in_specs,
        out_specs=pl.BlockSpec((None, t, gps * HG * 64), lambda b, g, i: (b, i, g)),
        out_shape=jax.ShapeDtypeStruct((B, S, H * 64), BF16),
        scratch_shapes=[pltpu.VMEM((gps, n_sel, t), F32), pltpu.VMEM((PV_ROWS, gps * HG * t), F32)]
                       + _flash_scratch(gps * HG, t, t),
        compiler_params=_params(("parallel", "parallel", "parallel")), name="nsa_attention",
    )(qc, qr, kvc, kvct, kvs, kvst, kvw, kvwt, gates_t)


def _rope_freq_row(period, rot):
    half = rot // 2
    inv_freq = ROPE_THETA ** (-(jnp.arange(half, dtype=F32) * 2.0 / rot))
    lane = jnp.arange(LANES) % period
    f = jnp.where(lane < rot, inv_freq[lane % half], 0.0)
    return f.reshape(1, LANES).astype(F32)


def _norm_matrices():
    r = jnp.arange(LANES)
    same = (r[:, None] // 64) == (r[None, :] // 64)
    nq = jnp.where(same, 1.0 / 64, 0.0).astype(BF16)
    nk = jnp.where(same & (r[:, None] < 64), 1.0 / 64, 0.0).astype(BF16)
    return nq, nk


def _q_gain(g):
    return (jnp.tile(g.astype(F32), 2) * Q_SCALE).reshape(1, LANES)


def _k_gain(g):
    return jnp.concatenate([g.astype(F32), jnp.ones((64,), F32)]).reshape(1, LANES)


def _interleave_kv(wk, wv, n_heads):
    d = wk.shape[0]
    wk = wk.reshape(d, n_heads, 64)
    wv = wv.reshape(d, n_heads, 64)
    return jnp.concatenate([wk, wv], axis=2).reshape(d, n_heads * 128)


def _split_cols(w, sizes):
    out, start = [], 0
    for n in sizes:
        out.append(w[:, start:start + n])
        start += n
    return out


def _mixer_layer0(x2, trig, B, S, gmix, w_in, w_out, a_q_norm, a_k_norm, b_q_norm, b_k_norm):
    sizes = (512, 64, 64, 256, 32, 8, 512, 512, 512)
    waq, wak, wav, wiq, wik, wiw, wbq, wbk, wbv = _split_cols(w_in, sizes)
    pad = jnp.zeros((D_MODEL, LANES - 40), w_in.dtype)
    w = jnp.concatenate([waq, wbq, _interleave_kv(wbk, wbv, 8), wak, wav, wiq, wik, wiw, pad],
                        axis=1).astype(BF16)
    nq, nk = _norm_matrices()
    tabs = (nq, nk, _q_gain(a_q_norm), _q_gain(b_q_norm), _k_gain(a_k_norm), _k_gain(b_k_norm))
    aq, bq, bkv, bkvt, akv, akvt, iq, ik, iwt, km = _ab_prep(x2, trig, gmix, w, tabs, B, S)
    n_blk = S // MOBA_BLOCK
    kmean = km.reshape(B, n_blk, 8, 128).transpose(0, 2, 1, 3)
    o_a = _dsa_attention(iq, iwt, ik, aq, akv, akvt).reshape(B * S, 512)
    o_b = _moba_attention(bq, bkv, bkvt, kmean).reshape(B * S, 512)
    w_out = w_out.astype(BF16)
    return [o_a, o_b], [w_out[:512], w_out[512:]]


def _mixer_layer1(x2, trig, B, S, gmix, w_in, w_out, q_norm, kcmp_norm, ksel_norm, kwin_norm,
                  pos_k, pos_v, w1_k, w2_k, w1_v, w2_v):
    G = NSA_GROUPS
    sizes = (1024,) + (256,) * 6 + (48,)
    wq, wkc, wvc, wks, wvs, wkw, wvw, wgt = _split_cols(w_in, sizes)
    pad = jnp.zeros((D_MODEL, LANES - 48), w_in.dtype)
    w = jnp.concatenate([wq, _interleave_kv(wks, wvs, G), _interleave_kv(wkw, wvw, G),
                         wkc, wvc, wgt, pad], axis=1).astype(BF16)
    nq, nk = _norm_matrices()
    tabs = (nq, nk, _q_gain(q_norm), _k_gain(ksel_norm), _k_gain(kwin_norm))
    qc, qr, kvs, kvst, kvw, kvwt, kc16, vc16, gates_t = _nsa_prep(x2, trig, gmix, w, tabs, B, S)

    def pos_rows(p):
        return p.astype(F32).reshape(2, NSA_CMP_STRIDE * HEAD_DIM)

    kvc, kvct = _compress(kc16, vc16, pos_rows(pos_k), pos_rows(pos_v),
                          w1_k.astype(BF16), w1_v.astype(BF16), w2_k.astype(BF16), w2_v.astype(BF16),
                          kcmp_norm.astype(F32).reshape(1, HEAD_DIM))
    o = _nsa_attention(qc, qr, kvc, kvct, kvs, kvst, kvw, kvwt, gates_t)
    return [o.reshape(B * S, NSA_HEADS * HEAD_DIM)], [w_out.astype(BF16)]


def _finish_layer(parts, weights, x2, mem, S, g_mem, g_src, w_q, w_kv, w_o, q_norm, k_norm,
                  g_ffn, ffn_w_in, ffn_w_out):
    row = lambda v: v.astype(F32).reshape(1, -1)
    kv = _mem_kv(mem, row(g_src), w_kv.astype(BF16), row(k_norm))
    wg = ffn_w_in[:, :D_FF].astype(BF16)
    wu = ffn_w_in[:, D_FF:].astype(BF16)
    return _post_mixer(parts, weights, x2, row(g_mem), w_q.astype(BF16), row(q_norm), kv, w_o.astype(BF16),
                       row(g_ffn), wg, wu, ffn_w_out.astype(BF16), S)


def kernel(x, mem, positions, norm_mix, norm_mem, norm_mem_src, norm_ffn, ab_w_in, ab_w_out, dsa_q_norm, dsa_k_norm, moba_q_norm, moba_k_norm, nsa_w_in, nsa_w_out, nsa_q_norm, nsa_kcmp_norm, nsa_ksel_norm, nsa_kwin_norm, nsa_cmp_pos_k, nsa_cmp_pos_v, nsa_cmp_w1_k, nsa_cmp_w2_k, nsa_cmp_w1_v, nsa_cmp_w2_v, mem_w_q, mem_w_kv, mem_w_o, mem_q_norm, mem_k_norm, ffn_w_in, ffn_w_out):
    B, S, D = x.shape
    depth = norm_mix.shape[0]
    x2 = x.reshape(B * S, D)
    trig = _rope_trig(positions.astype(F32).reshape(B * S, 1), _rope_freq_row(64, 16), _rope_freq_row(32, 8))
    row = lambda v: v.astype(F32).reshape(1, -1)
    for i in range(depth):
        j = i // 2
        if i % 2 == 0:
            parts, weights = _mixer_layer0(x2, trig, B, S, row(norm_mix[i]), ab_w_in[j], ab_w_out[j],
                               dsa_q_norm[j], dsa_k_norm[j], moba_q_norm[j], moba_k_norm[j])
        else:
            parts, weights = _mixer_layer1(x2, trig, B, S, row(norm_mix[i]), nsa_w_in[j], nsa_w_out[j],
                               nsa_q_norm[j], nsa_kcmp_norm[j], nsa_ksel_norm[j], nsa_kwin_norm[j],
                               nsa_cmp_pos_k[j], nsa_cmp_pos_v[j], nsa_cmp_w1_k[j], nsa_cmp_w2_k[j],
                               nsa_cmp_w1_v[j], nsa_cmp_w2_v[j])
        x2 = _finish_layer(parts, weights, x2, mem, S, norm_mem[i], norm_mem_src[i], mem_w_q[i], mem_w_kv[i],
                           mem_w_o[i], mem_q_norm[i], mem_k_norm[i], norm_ffn[i], ffn_w_in[i], ffn_w_out[i])
    return x2.reshape(B, S, D)
```

```python
import functools
import math

import jax
import jax.numpy as jnp
from jax import lax
from jax.experimental import pallas as pl
from jax.experimental.pallas import tpu as pltpu

F32 = jnp.float32
BF16 = jnp.bfloat16
I32 = jnp.int32
I16 = jnp.int16

D_MODEL = 1024
N_MEM = 256
HEAD_DIM = 64
ROPE_THETA = 500000.0
RMS_EPS = 1e-6
NEG_INF = -1e30
TINY = 1e-20

DSA_HEADS = 8
DSA_IDX_HEADS = 8
DSA_IDX_DIM = 32
DSA_TOPK = 256
MOBA_HEADS = 8
MOBA_BLOCK = 256
MOBA_TOPK = 3
NSA_HEADS = 16
NSA_GROUPS = 4
NSA_CMP_LEN = 32
NSA_CMP_STRIDE = 16
NSA_SEL_LEN = 64
NSA_SEL_TOPK = 16
NSA_WINDOW = 512
NSA_FORCE = 1e4
MEM_HEADS = 4
MEM_HEAD_DIM = 128
D_FF = ((8 * D_MODEL + 3 * 256 - 1) // (3 * 256)) * 256

LANES = 128
SUBLANES = 8
INT_MIN = -(2 ** 31)
VMEM_LIMIT = 60 * 1024 * 1024

PV_HEAD_ROWS = 16
PV_ROWS = PV_HEAD_ROWS + HEAD_DIM
PROJ_GROUP = 4
ATT_T = 256
MASK_BIAS = -1e30
M_FLOOR = -1e29
LOG2E = math.log2(math.e)
Q_SCALE = HEAD_DIM ** -0.5 * LOG2E

NT_DIMS = (((1,), (1,)), ((), ()))


def _dot(a, b):
    return jnp.dot(a, b, preferred_element_type=F32)


def _dot_nt(a, b):
    return lax.dot_general(a, b, NT_DIMS, preferred_element_type=F32)


def _split_bf16(a):
    hi = a.astype(BF16)
    return hi, (a - hi.astype(F32)).astype(BF16)


def _split_dot(a, b):
    hi, lo = _split_bf16(a)
    return _dot(hi, b) + _dot(lo, b)


def _rms_rows(x, gain):
    ms = jnp.mean(x * x, axis=-1, keepdims=True)
    return x * lax.rsqrt(ms + RMS_EPS) * gain


def _params(sem):
    return pltpu.CompilerParams(dimension_semantics=sem, vmem_limit_bytes=VMEM_LIMIT)


def _head_norm(y, norm_m, gain):
    ms = _split_dot(y * y, norm_m)
    return y * lax.rsqrt(ms + RMS_EPS) * gain


def _rope(y, c, s, lo_mask, half):
    sw = jnp.where(lo_mask, pltpu.roll(y, LANES - half, 1), pltpu.roll(y, half, 1))
    return y * c + sw * s


def _lane_iota(shape):
    return lax.broadcasted_iota(I32, shape, 1)


def _row_iota(shape):
    return lax.broadcasted_iota(I32, shape, 0)


def _rope_tables(pos, ftab, period, half):
    ang = pos * ftab
    lane = _lane_iota(ang.shape) % period
    c = jnp.cos(ang)
    s = jnp.sin(ang) * jnp.where(lane < half, -1.0, 1.0)
    return c, s


def _pv_operand(kv):
    head = jnp.where(_row_iota((PV_HEAD_ROWS, kv.shape[0])) == 0, 1.0, 0.0)
    return jnp.concatenate([head, kv.T[HEAD_DIM:, :]], axis=0).astype(BF16)


class _ColumnProjector:
    def __init__(self, xn, w_ref):
        self.xn, self.w_ref, self.groups = xn, w_ref, {}

    def __call__(self, j):
        g, u = divmod(j, PROJ_GROUP)
        if g not in self.groups:
            width = PROJ_GROUP * LANES
            lo = g * width
            hi = min(lo + width, self.w_ref.shape[1])
            self.groups[g] = _dot(self.xn, self.w_ref[:, lo:hi])
        return self.groups[g][:, u * LANES:(u + 1) * LANES]


def _kv_column(yc, nk, gain, c64k, s64k, lo64, first64):
    kn = jnp.where(first64, _head_norm(yc, nk, gain), yc)
    return _rope(kn, c64k, s64k, lo64, 8)


def _rope_trig_kernel(pos_ref, f64_ref, f32_ref, o_ref):
    pos = pos_ref[...]
    c64, s64 = _rope_tables(pos, f64_ref[...], 64, 8)
    c32, s32 = _rope_tables(pos, f32_ref[...], 32, 4)
    o_ref[...] = jnp.concatenate([c64, s64, c32, s32], axis=1)


def _rope_trig(pos2, f64, f32t, tm=1024):
    T = pos2.shape[0]
    return pl.pallas_call(
        _rope_trig_kernel, grid=(T // tm,),
        in_specs=[pl.BlockSpec((tm, 1), lambda i: (i, 0)),
                  pl.BlockSpec(f64.shape, lambda i: (0, 0)), pl.BlockSpec(f32t.shape, lambda i: (0, 0))],
        out_specs=pl.BlockSpec((tm, 4 * LANES), lambda i: (i, 0)),
        out_shape=jax.ShapeDtypeStruct((T, 4 * LANES), F32),
        compiler_params=_params(("parallel",)), name="rope_trig",
    )(pos2, f64, f32t)


def _ab_prep_kernel(x_ref, trig_ref, gmix_ref, w_ref, nq_ref, nk_ref,
                    gaq_ref, gbq_ref, gak_ref, gbk_ref,
                    aq_ref, bq_ref, bkv_ref, bkvt_ref, akv_ref, akvt_ref, iq_ref, ik_ref, iwt_ref, km_ref,
                    *, n_tiles):
    xn = _rms_rows(x_ref[...], gmix_ref[...]).astype(BF16)
    c64, s64, c32, s32 = [trig_ref[:, j * LANES:(j + 1) * LANES] for j in range(4)]
    lane = _lane_iota(c64.shape)
    lo64 = (lane % 64) < 8
    lo32 = (lane % 32) < 4
    first64 = lane < 64
    c64k = jnp.where(first64, c64, 1.0)
    s64k = jnp.where(first64, s64, 0.0)
    first32 = lane < 32
    c32k = jnp.where(first32, c32, 1.0)
    s32k = jnp.where(first32, s32, 0.0)
    nq = nq_ref[...]
    nk = nk_ref[...]

    col = _ColumnProjector(xn, w_ref)

    for j in range(4):
        q = _rope(_head_norm(col(j), nq, gaq_ref[...]), c64, s64, lo64, 8)
        aq_ref[2 * j] = q[:, :64].astype(BF16)
        aq_ref[2 * j + 1] = q[:, 64:].astype(BF16)
    for j in range(4):
        q = _rope(_head_norm(col(4 + j), nq, gbq_ref[...]), c64, s64, lo64, 8)
        bq_ref[2 * j] = q[:, :64].astype(BF16)
        bq_ref[2 * j + 1] = q[:, 64:].astype(BF16)
    blk_onehot = jnp.where(lane == HEAD_DIM + pl.program_id(0) % n_tiles, 1.0, 0.0)
    for h in range(8):
        kv = _kv_column(col(8 + h), nk, gbk_ref[...], c64k, s64k, lo64, first64)
        bkv_ref[h] = jnp.where(first64, kv, blk_onehot).astype(BF16)
        bkvt_ref[h] = _pv_operand(kv)
        km_ref[h:h + 1, :] = jnp.mean(kv, axis=0, keepdims=True)
    kv = _kv_column(col(16), nk, gak_ref[...], c64k, s64k, lo64, first64)
    akv_ref[...] = kv.astype(BF16)
    akvt_ref[...] = _pv_operand(kv)
    for j in range(2):
        q = _rope(col(17 + j), c32, s32, lo32, 4)
        for u in range(4):
            iq_ref[4 * j + u] = q[:, 32 * u:32 * (u + 1)].astype(BF16)
    yc = col(19)
    ik_ref[...] = _rope(yc, c32k, s32k, lo32, 4)[:, :32].astype(BF16)
    iwt_ref[...] = yc.T[32:40, :]


def _ab_prep(x2, trig, gmix, w, tabs, B, S):
    T = x2.shape[0]
    tm = ATT_T
    nt = S // tm
    n_cols = w.shape[1]
    nq, nk, gaq, gbq, gak, gbk = tabs

    def full(a):
        return pl.BlockSpec(a.shape, lambda i: (0,) * a.ndim)

    def hm(width, heads=8):
        return pl.BlockSpec((None, heads, tm, width), lambda i: (i // nt, 0, i % nt, 0))

    def tokm(width):
        return pl.BlockSpec((None, tm, width), lambda i: (i // nt, i % nt, 0))

    out_shape = (
        jax.ShapeDtypeStruct((B, 8, S, 64), BF16),
        jax.ShapeDtypeStruct((B, 8, S, 64), BF16),
        jax.ShapeDtypeStruct((B, 8, S, 128), BF16),
        jax.ShapeDtypeStruct((B, 8, nt, PV_ROWS, tm), BF16),
        jax.ShapeDtypeStruct((B, S, 128), BF16),
        jax.ShapeDtypeStruct((B, nt, PV_ROWS, tm), BF16),
        jax.ShapeDtypeStruct((B, 8, S, 32), BF16),
        jax.ShapeDtypeStruct((B, S, 32), BF16),
        jax.ShapeDtypeStruct((B, 8, S), F32),
        jax.ShapeDtypeStruct((T // tm, 8, 128), F32),
    )
    out_specs = (hm(64), hm(64), hm(128),
                 pl.BlockSpec((None, 8, None, PV_ROWS, tm), lambda i: (i // nt, 0, i % nt, 0, 0)),
                 tokm(128),
                 pl.BlockSpec((None, None, PV_ROWS, tm), lambda i: (i // nt, i % nt, 0, 0)),
                 hm(32), tokm(32),
                 pl.BlockSpec((None, 8, tm), lambda i: (i // nt, 0, i % nt)),
                 pl.BlockSpec((None, 8, 128), lambda i: (i, 0, 0)))
    in_specs = [pl.BlockSpec((tm, D_MODEL), lambda i: (i, 0)),
                pl.BlockSpec((tm, 4 * LANES), lambda i: (i, 0)),
                full(gmix), pl.BlockSpec((D_MODEL, n_cols), lambda i: (0, 0)),
                full(nq), full(nk), full(gaq), full(gbq), full(gak), full(gbk)]
    return pl.pallas_call(
        functools.partial(_ab_prep_kernel, n_tiles=nt), grid=(T // tm,), in_specs=in_specs, out_specs=out_specs,
        out_shape=out_shape, compiler_params=_params(("parallel",)), name="ab_prep",
    )(x2, trig, gmix, w, nq, nk, gaq, gbq, gak, gbk)


def _pad_q(q):
    return jnp.concatenate([q, jnp.zeros_like(q)], axis=1)


def _bias_lanes(q, rows):
    n, tq = rows.shape
    parts = [jnp.zeros((HEAD_DIM, tq), F32), rows]
    if n < HEAD_DIM:
        parts.append(jnp.zeros((HEAD_DIM - n, tq), F32))
    lanes = jnp.concatenate(parts, axis=0).T.astype(BF16)
    return jnp.where(_lane_iota(q.shape) < HEAD_DIM, q, lanes)


class _Flash:
    def __init__(self, m_ref, acc_ref, s_ref, cmax_ref, p_ref, tq):
        self.m_ref, self.acc_ref, self.tq = m_ref, acc_ref, tq
        self.s_ref, self.cmax_ref, self.p_ref = s_ref, cmax_ref, p_ref

    def reset(self):
        self.m_ref[...] = jnp.full(self.m_ref.shape, M_FLOOR, F32)
        self.acc_ref[...] = jnp.zeros(self.acc_ref.shape, F32)

    def _scores(self, buf, qs, kvs, biases):
        tq = self.tq
        for i in range(len(qs)):
            s = _dot_nt(kvs[i], qs[i])
            if biases[i] is not None:
                s = s + biases[i]
            self.s_ref[buf, i, :s.shape[0], :] = s
            self.cmax_ref[buf, :, i * tq:(i + 1) * tq] = jnp.max(s, axis=0, keepdims=True)

    def update(self, qs, kvs, kvts, biases):
        self._scores(0, qs, kvs, biases)
        self._finish(0, kvts)

    def run(self, qs, count, operands):
        def scores(c, buf):
            kvs, _, biases = operands(c)
            self._scores(buf, qs, kvs, biases)

        def finish(c, buf):
            self._finish(buf, operands(c)[1])

        last = jnp.maximum(count - 1, 0)
        scores(0, 0)

        def two_chunks(pp, carry):
            c = 2 * pp
            scores(c + 1, 1)
            finish(c, 0)
            scores(jnp.minimum(c + 2, last), 0)
            finish(c + 1, 1)
            return carry

        lax.fori_loop(0, count // 2, two_chunks, 0)

        @pl.when(count % 2 == 1)
        def _():
            finish(count - 1, 0)

    def _finish(self, buf, kvts):
        n = len(kvts)
        tq = self.tq
        kc = sum(kvt.shape[1] for kvt in kvts[0])
        alphas = []
        for i in range(n):
            cols = slice(i * tq, (i + 1) * tq)
            m = self.m_ref[:, cols]
            m_new = jnp.maximum(m, self.cmax_ref[buf, :, cols])
            p = jnp.exp2(self.s_ref[buf, i, :kc, :] - m_new)
            alpha = jnp.exp2(m - m_new)
            self.m_ref[:, cols] = m_new
            self.p_ref[i, :kc, :] = p.astype(BF16)
            alphas.append(alpha)
        for i in range(n):
            cols = slice(i * tq, (i + 1) * tq)
            pv, r0 = None, 0
            for kvt in kvts[i]:
                part = _dot(kvt, self.p_ref[i, r0:r0 + kvt.shape[1], :])
                pv = part if pv is None else pv + part
                r0 += kvt.shape[1]
            self.acc_ref[:, cols] = alphas[i] * self.acc_ref[:, cols] + pv

    def result(self, slot):
        cols = slice(slot * self.tq, (slot + 1) * self.tq)
        acc = self.acc_ref[:, cols]
        return acc / jnp.maximum(acc[0:1, :], TINY)


def _flash_scratch(n_slots, tq, kc):
    return [pltpu.VMEM((1, n_slots * tq), F32), pltpu.VMEM((PV_ROWS, n_slots * tq), F32),
            pltpu.VMEM((2, n_slots, kc, tq), F32), pltpu.VMEM((2, 1, n_slots * tq), F32),
            pltpu.VMEM((n_slots, kc, tq), BF16)]


def _softmax_direct(qs, kv, kvts, bias):
    scores = [_dot_nt(kv, q) for q in qs]
    probs = []
    for s in scores:
        s = s + bias
        m = jnp.maximum(jnp.max(s, axis=0, keepdims=True), M_FLOOR)
        probs.append(jnp.exp2(s - m))
    inv_ls, outs = [], []
    for p in probs:
        pb = p.astype(BF16)
        o, r0 = None, 0
        for kvt in kvts:
            part = _dot(kvt, pb[r0:r0 + kvt.shape[1]])
            o = part if o is None else o + part
            r0 += kvt.shape[1]
        outs.append(o)
        inv_ls.append(1.0 / jnp.maximum(o[0:1, :], TINY))
    return probs, inv_ls, outs


def _causal_bias(t):
    return jnp.where(_row_iota((t, t)) <= _lane_iota((t, t)), 0.0, MASK_BIAS)


def _store_heads(o_ref, heads_t):
    for u in range(len(heads_t) // 2):
        pair = jnp.concatenate([heads_t[2 * u][PV_HEAD_ROWS:, :], heads_t[2 * u + 1][PV_HEAD_ROWS:, :]], axis=0)
        o_ref[:, u * LANES:(u + 1) * LANES] = pair.T.astype(o_ref.dtype)


def _rank_select_t(v, n_valid, n_top):
    n = v.shape[0]
    row = _row_iota(v.shape)
    rank = jnp.zeros(v.shape, F32)
    for m in range(n):
        vm = v[m:m + 1, :]
        ahead = (vm > v) | ((vm == v) & (m < row))
        if n_valid is not None:
            ahead = ahead & (m < n_valid)
        rank = rank + jnp.where(ahead, 1.0, 0.0)
    sel = rank < n_top
    if n_valid is not None:
        sel = sel & (row < n_valid)
    return jnp.where(sel, 1.0, 0.0)


def _dsa_kernel(iq_ref, iwt_ref, ik_ref, aq_ref, akv_ref, akvt_ref, o_ref,
                sk_ref, half_ref, bias_ref, xcut_ref, *flash_refs, k_top, index_bits):
    t = ATT_T
    i = pl.program_id(1)
    n_ch = i + 1
    kio = _row_iota((t, t))
    qio = _lane_iota((t, t))

    def causal(c):
        return (c - i) * t + kio <= qio

    def score_chunk(c):
        k0 = pl.multiple_of(c * t, t)
        ikc = ik_ref[pl.ds(k0, t), :]
        sc = jnp.zeros((t, t), F32)
        for h in range(DSA_IDX_HEADS):
            logit = _dot_nt(ikc, iq_ref[h])
            sc = sc + iwt_ref[h:h + 1, :] * jnp.maximum(logit, 0.0)
        sc = jnp.where(sc == 0.0, 0.0, sc)
        bits = pltpu.bitcast(sc, I32)
        key = bits ^ ((bits >> 31) & 0x7FFFFFFF)
        key = jnp.where(causal(c), key, INT_MIN)
        sk_ref[c] = key
        half_ref[c] = (key >> 16).astype(I16)

    def score_pair(cc, carry):
        score_chunk(2 * cc)
        score_chunk(2 * cc + 1)
        return carry

    lax.fori_loop(0, (n_ch + 1) // 2, score_pair, 0)

    def count(pred):
        def body(c, acc8):
            ind = jnp.where(pred(sk_ref[c], c), 1.0, 0.0)
            return acc8 + ind.reshape(-1, SUBLANES, t).sum(axis=0)
        acc8 = lax.fori_loop(0, n_ch, body, jnp.zeros((SUBLANES, t), F32))
        return jnp.sum(acc8, axis=0, keepdims=True)

    def count_half(cand):
        rows = 2 * SUBLANES

        def body(cc, acc):
            parts = []
            for c in (2 * cc, 2 * cc + 1):
                ind = jnp.where(half_ref[c] >= cand, jnp.bfloat16(1), jnp.bfloat16(0))
                parts += [ind[rows * j:rows * (j + 1), :] for j in range(t // rows)]
            while len(parts) > 1:
                parts = [parts[2 * j] + parts[2 * j + 1] for j in range(len(parts) // 2)]
            return acc + parts[0].astype(F32)
        acc = lax.fori_loop(0, (n_ch + 1) // 2, body, jnp.zeros((rows, t), F32))
        return jnp.sum(acc, axis=0, keepdims=True)

    def half_search(n_all):
        def bit_step(b, carry):
            v, n_ge_v = carry
            cand = v + lax.shift_left(jnp.int32(1), 15 - b)
            n_ge_cand = count_half(cand.astype(I16))
            ok = n_ge_cand >= k_top
            return jnp.where(ok, cand, v), jnp.where(ok, n_ge_cand, n_ge_v)
        return lax.fori_loop(0, 16, bit_step, (jnp.full((1, t), -(2 ** 15), I32), n_all))

    thr_hi, n_ge_hi = half_search(jnp.full((1, t), t * n_ch, I32).astype(F32))

    def low_half_chunk(c, carry):
        key = sk_ref[c]
        hi = key >> 16
        lo = (key & 0xFFFF) - 2 ** 15
        half_ref[c] = jnp.where(hi > thr_hi, 2 ** 15 - 1, jnp.where(hi < thr_hi, -(2 ** 15), lo)).astype(I16)
        return carry

    lax.fori_loop(0, n_ch, low_half_chunk, 0)
    thr_lo, n_ge = half_search(n_ge_hi)
    thr = lax.shift_left(thr_hi, 16) + (thr_lo + 2 ** 15)

    xcut_ref[...] = jnp.full((1, t), 2 ** 30, I32)

    @pl.when(jnp.max(n_ge) > k_top)
    def _():
        need = k_top - count(lambda blk, c: blk > thr)

        def x_step(b, x):
            cand = x + lax.shift_left(jnp.int32(1), index_bits - 1 - b)
            ties_below = count(lambda blk, c: (blk == thr) & (c * t + kio < cand))
            return jnp.where(ties_below <= need, cand, x)
        xcut_ref[...] = lax.fori_loop(0, index_bits, x_step, jnp.zeros((1, t), I32))

    xcut = xcut_ref[...]

    n_pairs = (n_ch + 1) // 2

    def bias_chunk(c, carry):
        blk = sk_ref[jnp.minimum(c, i)]
        keep = (blk > thr) | ((blk == thr) & (c * t + kio < xcut))
        bias_ref[c] = jnp.where(keep & causal(c), 0.0, MASK_BIAS)
        return carry

    lax.fori_loop(0, 2 * n_pairs, bias_chunk, 0)

    flash = _Flash(*flash_refs, t)
    qs = [_pad_q(aq_ref[h]) for h in range(DSA_HEADS)]
    flash.reset()

    n = DSA_HEADS

    def att_pair(cc):
        c0 = 2 * cc
        k0 = pl.multiple_of(c0 * t, 2 * t)
        kv = akv_ref[pl.ds(k0, 2 * t), :]
        bias = jnp.concatenate([bias_ref[c0], bias_ref[c0 + 1]], axis=0)
        return [kv] * n, [[akvt_ref[c0], akvt_ref[c0 + 1]]] * n, [bias] * n

    flash.run(qs, n_pairs, att_pair)
    _store_heads(o_ref, [flash.result(h) for h in range(DSA_HEADS)])


def _dsa_attention(iq, iwt, ik, aq, akv, akvt):
    B, _, S, _ = aq.shape
    t = ATT_T
    nt = S // t
    k_top = min(DSA_TOPK, S // 4)
    in_specs = [
        pl.BlockSpec((None, 8, t, 32), lambda b, i: (b, 0, i, 0)),
        pl.BlockSpec((None, 8, t), lambda b, i: (b, 0, i)),
        pl.BlockSpec((None, S, 32), lambda b, i: (b, 0, 0)),
        pl.BlockSpec((None, 8, t, 64), lambda b, i: (b, 0, i, 0)),
        pl.BlockSpec((None, S, 128), lambda b, i: (b, 0, 0)),
        pl.BlockSpec((None, nt, PV_ROWS, t), lambda b, i: (b, 0, 0, 0)),
    ]
    return pl.pallas_call(
        functools.partial(_dsa_kernel, k_top=k_top, index_bits=S.bit_length()),
        grid=(B, nt), in_specs=in_specs,
        out_specs=pl.BlockSpec((None, t, 512), lambda b, i: (b, i, 0)),
        out_shape=jax.ShapeDtypeStruct((B, S, 512), BF16),
        scratch_shapes=[pltpu.VMEM((nt, t, t), I32), pltpu.VMEM((nt, t, t), I16), pltpu.VMEM((nt, t, t), F32),
                        pltpu.VMEM((1, t), I32)] + _flash_scratch(DSA_HEADS, t, 2 * t),
        compiler_params=_params(("parallel", "parallel")), name="dsa_attention",
    )(iq, iwt, ik, aq, akv, akvt)


MOBA_HPS = 8


def _moba_kernel(q_ref, kv_ref, kvt_ref, km_ref, o_ref, *flash_refs, n_top):
    t = ATT_T
    own = pl.program_id(2)
    causal = _causal_bias(t)
    flash = _Flash(*flash_refs, t)
    qs = []
    for hh in range(MOBA_HPS):
        q = _pad_q(q_ref[hh])
        km_hi, km_lo = _split_bf16(km_ref[hh])
        gate = _dot_nt(km_hi, q) + _dot_nt(km_lo, q)
        keep = _rank_select_t(gate, own, n_top)
        keep = jnp.where(_row_iota(keep.shape) == own, 1.0, keep)
        qs.append(_bias_lanes(q, (keep - 1.0) * (-MASK_BIAS)))
    flash.reset()

    def operands(cc):
        n0 = 2 * cc
        k0 = pl.multiple_of(n0 * t, 2 * t)
        heads = range(MOBA_HPS)
        return (n0, [kv_ref[hh, pl.ds(k0, 2 * t), :] for hh in heads],
                [[kvt_ref[hh, n0], kvt_ref[hh, n0 + 1]] for hh in heads])

    def past_pair(cc):
        _, kvs, kvts = operands(cc)
        return kvs, kvts, [None] * MOBA_HPS

    flash.run(qs, own // 2, past_pair)
    heads = range(MOBA_HPS)

    @pl.when(own % 2 == 0)
    def _():
        k0 = pl.multiple_of(own * t, t)
        flash.update(qs, [kv_ref[hh, pl.ds(k0, t), :] for hh in heads],
                     [[kvt_ref[hh, own]] for hh in heads], [causal] * MOBA_HPS)

    @pl.when(own % 2 == 1)
    def _():
        _, kvs, kvts = operands(own // 2)
        bias = jnp.concatenate([jnp.zeros((t, t), F32), causal], axis=0)
        flash.update(qs, kvs, kvts, [bias] * MOBA_HPS)

    _store_heads(o_ref, [flash.result(hh) for hh in range(MOBA_HPS)])


def _moba_attention(bq, bkv, bkvt, kmean):
    B, H, S, _ = bq.shape
    t = ATT_T
    hps = MOBA_HPS
    n_blk = S // MOBA_BLOCK
    assert t == MOBA_BLOCK and n_blk % 2 == 0 and H % hps == 0
    n_top = max(1, min(MOBA_TOPK, n_blk - 1))
    in_specs = [
        pl.BlockSpec((None, hps, t, 64), lambda b, h, i: (b, h, i, 0)),
        pl.BlockSpec((None, hps, S, 128), lambda b, h, i: (b, h, 0, 0)),
        pl.BlockSpec((None, hps, n_blk, PV_ROWS, t), lambda b, h, i: (b, h, 0, 0, 0)),
        pl.BlockSpec((None, hps, n_blk, 128), lambda b, h, i: (b, h, 0, 0)),
    ]
    return pl.pallas_call(
        functools.partial(_moba_kernel, n_top=n_top), grid=(B, H // hps, S // t), in_specs=in_specs,
        out_specs=pl.BlockSpec((None, t, hps * 64), lambda b, h, i: (b, i, h)),
        out_shape=jax.ShapeDtypeStruct((B, S, H * 64), BF16),
        scratch_shapes=_flash_scratch(hps, t, 2 * t),
        compiler_params=_params(("parallel", "parallel", "parallel")), name="moba_attention",
    )(bq, bkv, bkvt, kmean)


def _lane_group_norm(y, gain, width):
    outs = []
    for j in range(y.shape[1] // width):
        yc = y[:, j * width:(j + 1) * width]
        outs.append(_rms_rows(yc, gain))
    return jnp.concatenate(outs, axis=1)


def _mem_kv_kernel(m_ref, g_ref, w_ref, gk_ref, o_ref):
    mn = _rms_rows(m_ref[...], g_ref[...]).astype(BF16)
    y = _dot(mn, w_ref[...])
    hw = MEM_HEADS * MEM_HEAD_DIM
    k = _lane_group_norm(y[:, :hw], gk_ref[...], MEM_HEAD_DIM)
    o_ref[...] = jnp.concatenate([k, y[:, hw:]], axis=1).astype(BF16)


def _mem_kv(mem, g, w, gk):
    B, M, _ = mem.shape
    n = w.shape[1]
    return pl.pallas_call(
        _mem_kv_kernel, grid=(B,),
        in_specs=[pl.BlockSpec((None, M, D_MODEL), lambda b: (b, 0, 0)),
                  pl.BlockSpec(g.shape, lambda b: (0, 0)),
                  pl.BlockSpec(w.shape, lambda b: (0, 0)),
                  pl.BlockSpec(gk.shape, lambda b: (0, 0))],
        out_specs=pl.BlockSpec((None, M, n), lambda b: (b, 0, 0)),
        out_shape=jax.ShapeDtypeStruct((B, M, n), BF16),
        compiler_params=_params(("parallel",)), name="mem_kv",
    )(mem, g, w, gk)


def _mem_attend(x, g_ref, wq_ref, gq_ref, kv_ref, wo_ref):
    xn = _rms_rows(x, g_ref[...]).astype(BF16)
    q = _lane_group_norm(_dot(xn, wq_ref[...]), gq_ref[...], MEM_HEAD_DIM).astype(BF16)
    hw = MEM_HEADS * MEM_HEAD_DIM
    scale = MEM_HEAD_DIM ** -0.5
    outs = []
    for h in range(MEM_HEADS):
        cols = slice(h * MEM_HEAD_DIM, (h + 1) * MEM_HEAD_DIM)
        k = kv_ref[:, cols]
        v = kv_ref[:, hw + h * MEM_HEAD_DIM:hw + (h + 1) * MEM_HEAD_DIM]
        s = _dot_nt(q[:, cols], k) * scale
        p = jnp.exp(s - jnp.max(s, axis=-1, keepdims=True))
        p = p / jnp.sum(p, axis=-1, keepdims=True)
        outs.append(_dot(p.astype(BF16), v))
    o = jnp.concatenate(outs, axis=1).astype(BF16)
    return x + _dot(o, wo_ref[...])


def _post_mixer_kernel(*refs, n_in):
    a_refs = refs[:n_in]
    w_refs = refs[n_in:2 * n_in]
    x_ref, gm_ref, wq_ref, gq_ref, kv_ref, wo_ref, gf_ref, wg_ref, wu_ref, wd_ref, o_ref = refs[2 * n_in:]
    x = x_ref[...]
    for a_ref, w_ref in zip(a_refs, w_refs):
        x = x + _dot(a_ref[...], w_ref[...])
    x = _mem_attend(x, gm_ref, wq_ref, gq_ref, kv_ref, wo_ref)
    xn = _rms_rows(x, gf_ref[...]).astype(BF16)
    gate = _dot(xn, wg_ref[...])
    up = _dot(xn, wu_ref[...])
    act = (gate * jax.nn.sigmoid(gate) * up).astype(BF16)
    o_ref[...] = x + _dot(act, wd_ref[...])


def _post_mixer(parts, weights, x2, g_mem, wq, gq, kv, wo, g_ffn, wg, wu, wd, S, tm=512):
    T = x2.shape[0]
    nt = S // tm
    n_in = len(parts)
    M, n = kv.shape[1], kv.shape[2]

    def const(a):
        return pl.BlockSpec(a.shape, lambda i: (0,) * a.ndim, pipeline_mode=pl.Buffered(1))

    in_specs = ([pl.BlockSpec((tm, p.shape[1]), lambda i: (i, 0)) for p in parts]
                + [const(w) for w in weights]
                + [pl.BlockSpec((tm, D_MODEL), lambda i: (i, 0)),
                   const(g_mem), const(wq), const(gq),
                   pl.BlockSpec((None, M, n), lambda i: (i // nt, 0, 0)),
                   const(wo), const(g_ffn), const(wg), const(wu), const(wd)])
    return pl.pallas_call(
        functools.partial(_post_mixer_kernel, n_in=n_in), grid=(T // tm,), in_specs=in_specs,
        out_specs=pl.BlockSpec((tm, D_MODEL), lambda i: (i, 0)),
        out_shape=jax.ShapeDtypeStruct((T, D_MODEL), F32),
        compiler_params=_params(("parallel",)), name="post_mixer",
    )(*parts, *weights, x2, g_mem, wq, gq, kv, wo, g_ffn, wg, wu, wd)


def _nsa_prep_kernel(x_ref, trig_ref, gmix_ref, w_ref, nq_ref, nk_ref,
                     gq_ref, gks_ref, gkw_ref,
                     qc_ref, qr_ref, kvs_ref, kvst_ref, kvw_ref, kvwt_ref, kc_ref, vc_ref, gtt_ref,
                     stage_ref, *, n_tiles):
    xn = _rms_rows(x_ref[...], gmix_ref[...]).astype(BF16)
    c64, s64 = trig_ref[:, :LANES], trig_ref[:, LANES:]
    lane = _lane_iota(c64.shape)
    lo64 = (lane % 64) < 8
    first64 = lane < 64
    c64k = jnp.where(first64, c64, 1.0)
    s64k = jnp.where(first64, s64, 0.0)
    nq = nq_ref[...]
    nk = nk_ref[...]

    col = _ColumnProjector(xn, w_ref)

    for j in range(8):
        qn = _head_norm(col(j), nq, gq_ref[...])
        qr = _rope(qn, c64, s64, lo64, 8)
        qc_ref[2 * j] = qn[:, :64].astype(BF16)
        qc_ref[2 * j + 1] = qn[:, 64:].astype(BF16)
        qr_ref[2 * j] = qr[:, :64].astype(BF16)
        qr_ref[2 * j + 1] = qr[:, 64:].astype(BF16)
    tile = pl.program_id(0) % n_tiles
    sel_blk = tile * (ATT_T // NSA_SEL_LEN) + lax.shift_right_logical(
        _row_iota(c64.shape), NSA_SEL_LEN.bit_length() - 1)
    blk_onehot = jnp.where(lane == HEAD_DIM + sel_blk, 1.0, 0.0)
    for g in range(NSA_GROUPS):
        kv = _kv_column(col(8 + g), nk, gks_ref[...], c64k, s64k, lo64, first64)
        kvs_ref[g] = jnp.where(first64, kv, blk_onehot).astype(BF16)
        kvst_ref[g] = _pv_operand(kv)
        kv = _kv_column(col(12 + g), nk, gkw_ref[...], c64k, s64k, lo64, first64)
        kvw_ref[g] = kv.astype(BF16)
        kvwt_ref[g] = _pv_operand(kv)
    stride = NSA_CMP_STRIDE
    rows = stage_ref.shape[0] // stride
    for out_ref, first in ((kc_ref, 16), (vc_ref, 18)):
        for c in range(2):
            stage_ref[...] = col(first + c)
            for u in range(0, stride, 2):
                pair = [stage_ref[pl.ds(u + v, rows, stride=stride), :] for v in range(2)]
                for h in range(2):
                    halves = [p[:, h * HEAD_DIM:(h + 1) * HEAD_DIM] for p in pair]
                    out_ref[2 * c + h, :, u * HEAD_DIM:(u + 2) * HEAD_DIM] = jnp.concatenate(halves, axis=1)
    gates_t = jax.nn.sigmoid(col(20)).T
    for g in range(NSA_GROUPS):
        gtt_ref[g] = gates_t[12 * g:12 * (g + 1), :]


def _nsa_prep(x2, trig, gmix, w, tabs, B, S):
    T = x2.shape[0]
    tm = ATT_T
    nt = S // tm
    nq, nk, gq, gks, gkw = tabs

    def full(a):
        return pl.BlockSpec(a.shape, lambda i: (0,) * a.ndim)

    def hm(width, heads):
        return pl.BlockSpec((None, heads, tm, width), lambda i: (i // nt, 0, i % nt, 0))

    def hmt(heads):
        return pl.BlockSpec((None, heads, None, PV_ROWS, tm), lambda i: (i // nt, 0, i % nt, 0, 0))

    def tokm(width):
        return pl.BlockSpec((None, tm, width), lambda i: (i // nt, i % nt, 0))

    out_shape = (
        jax.ShapeDtypeStruct((B, 16, S, 64), BF16),
        jax.ShapeDtypeStruct((B, 16, S, 64), BF16),
        jax.ShapeDtypeStruct((B, 4, S, 128), BF16),
        jax.ShapeDtypeStruct((B, 4, nt, PV_ROWS, tm), BF16),
        jax.ShapeDtypeStruct((B, 4, S, 128), BF16),
        jax.ShapeDtypeStruct((B, 4, nt, PV_ROWS, tm), BF16),
        jax.ShapeDtypeStruct((B, 4, S // 16, 1024), F32),
        jax.ShapeDtypeStruct((B, 4, S // 16, 1024), F32),
        jax.ShapeDtypeStruct((B, 4, 12, S), F32),
    )
    rows16 = pl.BlockSpec((None, 4, tm // 16, 1024), lambda i: (i // nt, 0, i % nt, 0))
    out_specs = (hm(64, 16), hm(64, 16), hm(128, 4), hmt(4), hm(128, 4), hmt(4), rows16, rows16,
                 pl.BlockSpec((None, 4, 12, tm), lambda i: (i // nt, 0, 0, i % nt)))
    in_specs = [pl.BlockSpec((tm, D_MODEL), lambda i: (i, 0)),
                pl.BlockSpec((tm, 2 * LANES), lambda i: (i, 0)),
                full(gmix), full(w), full(nq), full(nk), full(gq), full(gks), full(gkw)]
    return pl.pallas_call(
        functools.partial(_nsa_prep_kernel, n_tiles=nt), grid=(T // tm,), in_specs=in_specs, out_specs=out_specs,
        out_shape=out_shape, scratch_shapes=[pltpu.VMEM((tm, LANES), F32)],
        compiler_params=_params(("parallel",)), name="nsa_prep",
    )(x2, trig, gmix, w, nq, nk, gq, gks, gkw)


def _compress_one(x16, pa, pb, w1a, w1b, w2):
    n16 = x16.shape[0]
    h_a = _dot((x16 + pa).astype(BF16), w1a)
    h_b = _dot((x16 + pb).astype(BF16), w1b)
    pre = h_a + pltpu.roll(h_b, n16 - 1, 0)
    act = pre * jax.nn.sigmoid(pre)
    return _dot(act.astype(BF16), w2)


def _compress_kernel(xk_ref, xv_ref, pk_ref, pv_ref, w1k_ref, w1v_ref, w2k_ref, w2v_ref, gk_ref,
                     o_ref, ot_ref):
    half = w1k_ref.shape[0] // 2
    k = _compress_one(xk_ref[...], pk_ref[0:1, :], pk_ref[1:2, :],
                      w1k_ref[:half, :], w1k_ref[half:, :], w2k_ref[...])
    k = _rms_rows(k, gk_ref[...])
    v = _compress_one(xv_ref[...], pv_ref[0:1, :], pv_ref[1:2, :],
                      w1v_ref[:half, :], w1v_ref[half:, :], w2v_ref[...])
    kv = jnp.concatenate([k, v], axis=1)
    o_ref[...] = kv.astype(BF16)
    ot_ref[...] = _pv_operand(kv)


def _compress(xk16, xv16, pk, pv, w1k, w1v, w2k, w2v, gk):
    B, G, n16, width = xk16.shape

    def full(a):
        return pl.BlockSpec(a.shape, lambda b, g: (0,) * a.ndim)

    xspec = pl.BlockSpec((None, None, n16, width), lambda b, g: (b, g, 0, 0))
    return pl.pallas_call(
        _compress_kernel, grid=(B, G),
        in_specs=[xspec, xspec, full(pk), full(pv), full(w1k), full(w1v), full(w2k), full(w2v), full(gk)],
        out_specs=(pl.BlockSpec((None, None, n16, 128), lambda b, g: (b, g, 0, 0)),
                   pl.BlockSpec((None, None, PV_ROWS, n16), lambda b, g: (b, g, 0, 0))),
        out_shape=(jax.ShapeDtypeStruct((B, G, n16, 128), BF16),
                   jax.ShapeDtypeStruct((B, G, PV_ROWS, n16), BF16)),
        compiler_params=_params(("parallel", "parallel")), name="nsa_compress",
    )(xk16, xv16, pk, pv, w1k, w1v, w2k, w2v, gk)


NSA_GPS = 4


def _nsa_tile_masks(i, n16, n_sel, n_cmp, n_top):
    t = ATT_T
    t0 = i * t
    n_id = _row_iota((n16, t))
    q_id = t0 + _lane_iota((n16, t))
    cmp_visible = (n_id < n_cmp) & (n_id * NSA_CMP_STRIDE + (NSA_CMP_LEN - 1) <= q_id)
    b_id = _row_iota((n_sel, n16)) * NSA_SEL_LEN
    r_id = _lane_iota((n_sel, n16)) * NSA_CMP_STRIDE
    cover_t = ((r_id < b_id + NSA_SEL_LEN) & (r_id + NSA_CMP_LEN > b_id)
               & (_lane_iota((n_sel, n16)) < n_cmp))
    n_wc = NSA_WINDOW // t + 1
    cw = jnp.maximum(i - (n_wc - 1), 0)
    dist = (i - cw) * t + _lane_iota((n_wc * t, t)) - _row_iota((n_wc * t, t))
    blk = _row_iota((n_sel, t))
    cur = lax.shift_right_logical(t0 + _lane_iota((n_sel, t)), NSA_SEL_LEN.bit_length() - 1)
    return dict(
        bias_c=jnp.where(cmp_visible, 0.0, MASK_BIAS),
        cover_t=jnp.where(cover_t, 1.0, 0.0).astype(BF16),
        n_wc=n_wc, cw=cw, bias_w=jnp.where((dist >= 0) & (dist < NSA_WINDOW), 0.0, MASK_BIAS),
        forced=(blk == 0) | (blk == cur) | (blk == cur - 1), visible_blk=blk <= cur,
        n_wanted=jnp.minimum(cur[0:1, :] + 1, n_top).astype(F32))


def _nsa_front(qc_ref, qr_ref, kvc_ref, kvct_ref, kvw_ref, kvwt_ref, gtt_ref, part_ref, masks):
    t = ATT_T
    HG = NSA_HEADS // NSA_GROUPS
    n16 = kvc_ref.shape[0]

    p_sum = jnp.zeros((n16, t), F32)
    o_c = []
    probs, inv_ls, outs = _softmax_direct([_pad_q(qc_ref[j]) for j in range(HG)], kvc_ref[...],
                                          [kvct_ref[...]], masks["bias_c"])
    for j in range(HG):
        p_sum = p_sum + probs[j] * inv_ls[j]
        o_c.append(outs[j] * inv_ls[j])

    p_hi, p_lo = _split_bf16(p_sum)
    imp = _dot(masks["cover_t"], p_hi) + _dot(masks["cover_t"], p_lo)

    qs = [_pad_q(qr_ref[j]) for j in range(HG)]
    n_wc, cw = masks["n_wc"], masks["cw"]
    kw0 = pl.multiple_of(cw * t, t)
    _, inv_lw, out_w = _softmax_direct(qs, kvw_ref[pl.ds(kw0, n_wc * t), :],
                                       [kvwt_ref[cw + u] for u in range(n_wc)], masks["bias_w"])
    for j in range(HG):
        part_ref[:, j * t:(j + 1) * t] = (gtt_ref[3 * j:3 * j + 1, :] * o_c[j]
                                          + gtt_ref[3 * j + 2:3 * j + 3, :] * (out_w[j] * inv_lw[j]))

    imp = jnp.where(masks["forced"], NSA_FORCE, imp)
    return qs, jnp.where(masks["visible_blk"], imp, NEG_INF)


RANK_SEGMENT = 16


def _count_larger(imp, acc, rows):
    for m in rows:
        acc = acc + jnp.where(imp[m:m + 1, :] > imp, 1.0, 0.0)
    return acc


def _nsa_kernel(qc_ref, qr_ref, kvc_ref, kvct_ref, kvs_ref, kvst_ref, kvw_ref, kvwt_ref, gtt_ref,
                o_ref, sel_ref, part_ref, *flash_refs, n_cmp, n_top):
    t = ATT_T
    HG = NSA_HEADS // NSA_GROUPS
    n_slots = NSA_GPS * HG
    i = pl.program_id(2)

    masks = _nsa_tile_masks(i, kvc_ref.shape[1], sel_ref.shape[1], n_cmp, n_top)
    n_sel = sel_ref.shape[1]
    qs, imps = [], []
    for g in range(NSA_GPS):
        heads_g = pl.ds(g * HG, HG)
        q_g, imp_g = _nsa_front(
            qc_ref.at[heads_g], qr_ref.at[heads_g], kvc_ref.at[g], kvct_ref.at[g], kvw_ref.at[g],
            kvwt_ref.at[g], gtt_ref.at[g], part_ref.at[:, pl.ds(g * HG * t, HG * t)], masks)
        qs += q_g
        imps.append(imp_g)
        sel_ref[g] = _count_larger(imp_g, jnp.zeros((n_sel, t), F32), range(RANK_SEGMENT))

    for k in range(1, n_sel // RANK_SEGMENT):
        @pl.when((i + 1) * (t // NSA_SEL_LEN) > k * RANK_SEGMENT)
        def _():
            for g in range(NSA_GPS):
                sel_ref[g] = _count_larger(imps[g], sel_ref[g],
                                           range(k * RANK_SEGMENT, (k + 1) * RANK_SEGMENT))

    miss = None
    for g in range(NSA_GPS):
        sel_fast = sel_ref[g] < n_top
        n_picked = jnp.sum(jnp.where(sel_fast & masks["visible_blk"], 1.0, 0.0), axis=0, keepdims=True)
        sel_ref[g] = jnp.where(sel_fast, 0.0, MASK_BIAS)
        miss_g = jnp.abs(n_picked - masks["n_wanted"])
        miss = miss_g if miss is None else jnp.maximum(miss, miss_g)

    @pl.when(jnp.max(miss) > 0.0)
    def _():
        for g in range(NSA_GPS):
            sel_ref[g] = (_rank_select_t(imps[g], None, n_top) - 1.0) * (-MASK_BIAS)

    qs_sel = [_bias_lanes(qs[s], sel_ref[s // HG]) for s in range(n_slots)]
    flash = _Flash(*flash_refs, t)
    flash.reset()

    def sel_operands(c):
        k0 = pl.multiple_of(c * t, t)
        kvs = [kvs_ref[s // HG, pl.ds(k0, t), :] for s in range(n_slots)]
        kvts = [[kvst_ref[s // HG, c]] for s in range(n_slots)]
        return kvs, kvts

    def past_chunk(c):
        kvs, kvts = sel_operands(c)
        return kvs, kvts, [None] * n_slots

    flash.run(qs_sel, i, past_chunk)
    kvs, kvts = sel_operands(i)
    flash.update(qs_sel, kvs, kvts, [_causal_bias(t)] * n_slots)

    heads = []
    for s in range(n_slots):
        g, j = divmod(s, HG)
        heads.append(part_ref[:, s * t:(s + 1) * t] + gtt_ref[g, 3 * j + 1:3 * j + 2, :] * flash.result(s))
    _store_heads(o_ref, heads)


def _nsa_attention(qc, qr, kvc, kvct, kvs, kvst, kvw, kvwt, gates_t):
    B, H, S, _ = qc.shape
    G = NSA_GROUPS
    HG = H // G
    t = ATT_T
    nt = S // t
    n16 = kvc.shape[2]
    n_cmp = (S - NSA_CMP_LEN) // NSA_CMP_STRIDE + 1
    n_sel = S // NSA_SEL_LEN
    n_top = min(NSA_SEL_TOPK, n_sel)
    gps = NSA_GPS
    assert G % gps == 0 and nt % 2 == 0 and n_sel <= HEAD_DIM and S >= (NSA_WINDOW // t + 1) * t
    qspec = pl.BlockSpec((None, gps * HG, t, 64), lambda b, g, i: (b, g, i, 0))
    once = pl.Buffered(1)
    kvspec = pl.BlockSpec((None, gps, S, 128), lambda b, g, i: (b, g, 0, 0), pipeline_mode=once)
    kvtspec = pl.BlockSpec((None, gps, nt, PV_ROWS, t), lambda b, g, i: (b, g, 0, 0, 0), pipeline_mode=once)
    in_specs = [qspec, qspec,
                pl.BlockSpec((None, gps, n16, 128), lambda b, g, i: (b, g, 0, 0)),
                pl.BlockSpec((None, gps, PV_ROWS, n16), lambda b, g, i: (b, g, 0, 0)),
                kvspec, kvtspec, kvspec, kvtspec,
                pl.BlockSpec((None, gps, 12, t), lambda b, g, i: (b, g, 0, i))]
    return pl.pallas_call(
        functools.partial(_nsa_kernel, n_cmp=n_cmp, n_top=n_top), grid=(B, G // gps, nt), in_specs=in_specs,
        out_specs=pl.BlockSpec((None, t, gps * HG * 64), lambda b, g, i: (b, i, g)),
        out_shape=jax.ShapeDtypeStruct((B, S, H * 64), BF16),
        scratch_shapes=[pltpu.VMEM((gps, n_sel, t), F32), pltpu.VMEM((PV_ROWS, gps * HG * t), F32)]
                       + _flash_scratch(gps * HG, t, t),
        compiler_params=_params(("parallel", "parallel", "parallel")), name="nsa_attention",
    )(qc, qr, kvc, kvct, kvs, kvst, kvw, kvwt, gates_t)


def _rope_freq_row(period, rot):
    half = rot // 2
    inv_freq = ROPE_THETA ** (-(jnp.arange(half, dtype=F32) * 2.0 / rot))
    lane = jnp.arange(LANES) % period
    f = jnp.where(lane < rot, inv_freq[lane % half], 0.0)
    return f.reshape(1, LANES).astype(F32)


def _norm_matrices():
    r = jnp.arange(LANES)
    same = (r[:, None] // 64) == (r[None, :] // 64)
    nq = jnp.where(same, 1.0 / 64, 0.0).astype(BF16)
    nk = jnp.where(same & (r[:, None] < 64), 1.0 / 64, 0.0).astype(BF16)
    return nq, nk


def _q_gain(g):
    return (jnp.tile(g.astype(F32), 2) * Q_SCALE).reshape(1, LANES)


def _k_gain(g):
    return jnp.concatenate([g.astype(F32), jnp.ones((64,), F32)]).reshape(1, LANES)


def _interleave_kv(wk, wv, n_heads):
    d = wk.shape[0]
    wk = wk.reshape(d, n_heads, 64)
    wv = wv.reshape(d, n_heads, 64)
    return jnp.concatenate([wk, wv], axis=2).reshape(d, n_heads * 128)


def _split_cols(w, sizes):
    out, start = [], 0
    for n in sizes:
        out.append(w[:, start:start + n])
        start += n
    return out


def _mixer_layer0(x2, trig, B, S, gmix, w_in, w_out, a_q_norm, a_k_norm, b_q_norm, b_k_norm):
    sizes = (512, 64, 64, 256, 32, 8, 512, 512, 512)
    waq, wak, wav, wiq, wik, wiw, wbq, wbk, wbv = _split_cols(w_in, sizes)
    pad = jnp.zeros((D_MODEL, LANES - 40), w_in.dtype)
    w = jnp.concatenate([waq, wbq, _interleave_kv(wbk, wbv, 8), wak, wav, wiq, wik, wiw, pad],
                        axis=1).astype(BF16)
    nq, nk = _norm_matrices()
    tabs = (nq, nk, _q_gain(a_q_norm), _q_gain(b_q_norm), _k_gain(a_k_norm), _k_gain(b_k_norm))
    aq, bq, bkv, bkvt, akv, akvt, iq, ik, iwt, km = _ab_prep(x2, trig, gmix, w, tabs, B, S)
    n_blk = S // MOBA_BLOCK
    kmean = km.reshape(B, n_blk, 8, 128).transpose(0, 2, 1, 3)
    o_a = _dsa_attention(iq, iwt, ik, aq, akv, akvt).reshape(B * S, 512)
    o_b = _moba_attention(bq, bkv, bkvt, kmean).reshape(B * S, 512)
    w_out = w_out.astype(BF16)
    return [o_a, o_b], [w_out[:512], w_out[512:]]


def _mixer_layer1(x2, trig, B, S, gmix, w_in, w_out, q_norm, kcmp_norm, ksel_norm, kwin_norm,
                  pos_k, pos_v, w1_k, w2_k, w1_v, w2_v):
    G = NSA_GROUPS
    sizes = (1024,) + (256,) * 6 + (48,)
    wq, wkc, wvc, wks, wvs, wkw, wvw, wgt = _split_cols(w_in, sizes)
    pad = jnp.zeros((D_MODEL, LANES - 48), w_in.dtype)
    w = jnp.concatenate([wq, _interleave_kv(wks, wvs, G), _interleave_kv(wkw, wvw, G),
                         wkc, wvc, wgt, pad], axis=1).astype(BF16)
    nq, nk = _norm_matrices()
    tabs = (nq, nk, _q_gain(q_norm), _k_gain(ksel_norm), _k_gain(kwin_norm))
    qc, qr, kvs, kvst, kvw, kvwt, kc16, vc16, gates_t = _nsa_prep(x2, trig, gmix, w, tabs, B, S)

    def pos_rows(p):
        return p.astype(F32).reshape(2, NSA_CMP_STRIDE * HEAD_DIM)

    kvc, kvct = _compress(kc16, vc16, pos_rows(pos_k), pos_rows(pos_v),
                          w1_k.astype(BF16), w1_v.astype(BF16), w2_k.astype(BF16), w2_v.astype(BF16),
                          kcmp_norm.astype(F32).reshape(1, HEAD_DIM))
    o = _nsa_attention(qc, qr, kvc, kvct, kvs, kvst, kvw, kvwt, gates_t)
    return [o.reshape(B * S, NSA_HEADS * HEAD_DIM)], [w_out.astype(BF16)]


def _finish_layer(parts, weights, x2, mem, S, g_mem, g_src, w_q, w_kv, w_o, q_norm, k_norm,
                  g_ffn, ffn_w_in, ffn_w_out):
    row = lambda v: v.astype(F32).reshape(1, -1)
    kv = _mem_kv(mem, row(g_src), w_kv.astype(BF16), row(k_norm))
    wg = ffn_w_in[:, :D_FF].astype(BF16)
    wu = ffn_w_in[:, D_FF:].astype(BF16)
    return _post_mixer(parts, weights, x2, row(g_mem), w_q.astype(BF16), row(q_norm), kv, w_o.astype(BF16),
                       row(g_ffn), wg, wu, ffn_w_out.astype(BF16), S)


def kernel(x, mem, positions, norm_mix, norm_mem, norm_mem_src, norm_ffn, ab_w_in, ab_w_out, dsa_q_norm, dsa_k_norm, moba_q_norm, moba_k_norm, nsa_w_in, nsa_w_out, nsa_q_norm, nsa_kcmp_norm, nsa_ksel_norm, nsa_kwin_norm, nsa_cmp_pos_k, nsa_cmp_pos_v, nsa_cmp_w1_k, nsa_cmp_w2_k, nsa_cmp_w1_v, nsa_cmp_w2_v, mem_w_q, mem_w_kv, mem_w_o, mem_q_norm, mem_k_norm, ffn_w_in, ffn_w_out):
    B, S, D = x.shape
    depth = norm_mix.shape[0]
    x2 = x.reshape(B * S, D)
    trig = _rope_trig(positions.astype(F32).reshape(B * S, 1), _rope_freq_row(64, 16), _rope_freq_row(32, 8))
    row = lambda v: v.astype(F32).reshape(1, -1)
    for i in range(depth):
        j = i // 2
        if i % 2 == 0:
            parts, weights = _mixer_layer0(x2, trig, B, S, row(norm_mix[i]), ab_w_in[j], ab_w_out[j],
                               dsa_q_norm[j], dsa_k_norm[j], moba_q_norm[j], moba_k_norm[j])
        else:
            parts, weights = _mixer_layer1(x2, trig, B, S, row(norm_mix[i]), nsa_w_in[j], nsa_w_out[j],
                               nsa_q_norm[j], nsa_kcmp_norm[j], nsa_ksel_norm[j], nsa_kwin_norm[j],
                               nsa_cmp_pos_k[j], nsa_cmp_pos_v[j], nsa_cmp_w1_k[j], nsa_cmp_w2_k[j],
                               nsa_cmp_w1_v[j], nsa_cmp_w2_v[j])
        x2 = _finish_layer(parts, weights, x2, mem, S, norm_mem[i], norm_mem_src[i], mem_w_q[i], mem_w_kv[i],
                           mem_w_o[i], mem_q_norm[i], mem_k_norm[i], norm_ffn[i], ffn_w_in[i], ffn_w_out[i])
    return x2.reshape(B, S, D)
```

```python
import functools
import math

import jax
import jax.numpy as jnp
from jax import lax
from jax.experimental import pallas as pl
from jax.experimental.pallas import tpu as pltpu

F32 = jnp.float32
BF16 = jnp.bfloat16
I32 = jnp.int32
I16 = jnp.int16

D_MODEL = 1024
N_MEM = 256
HEAD_DIM = 64
ROPE_THETA = 500000.0
RMS_EPS = 1e-6
NEG_INF = -1e30
TINY = 1e-20

DSA_HEADS = 8
DSA_IDX_HEADS = 8
DSA_IDX_DIM = 32
DSA_TOPK = 256
MOBA_HEADS = 8
MOBA_BLOCK = 256
MOBA_TOPK = 3
NSA_HEADS = 16
NSA_GROUPS = 4
NSA_CMP_LEN = 32
NSA_CMP_STRIDE = 16
NSA_SEL_LEN = 64
NSA_SEL_TOPK = 16
NSA_WINDOW = 512
NSA_FORCE = 1e4
MEM_HEADS = 4
MEM_HEAD_DIM = 128
D_FF = ((8 * D_MODEL + 3 * 256 - 1) // (3 * 256)) * 256

LANES = 128
SUBLANES = 8
INT_MIN = -(2 ** 31)
VMEM_LIMIT = 60 * 1024 * 1024

PV_HEAD_ROWS = 16
PV_ROWS = PV_HEAD_ROWS + HEAD_DIM
PROJ_GROUP = 4
ATT_T = 256
MASK_BIAS = -1e30
M_FLOOR = -1e29
LOG2E = math.log2(math.e)
Q_SCALE = HEAD_DIM ** -0.5 * LOG2E

NT_DIMS = (((1,), (1,)), ((), ()))


def _dot(a, b):
    return jnp.dot(a, b, preferred_element_type=F32)


def _dot_nt(a, b):
    return lax.dot_general(a, b, NT_DIMS, preferred_element_type=F32)


def _split_bf16(a):
    hi = a.astype(BF16)
    return hi, (a - hi.astype(F32)).astype(BF16)


def _split_dot(a, b):
    hi, lo = _split_bf16(a)
    return _dot(hi, b) + _dot(lo, b)


def _rms_rows(x, gain):
    ms = jnp.mean(x * x, axis=-1, keepdims=True)
    return x * lax.rsqrt(ms + RMS_EPS) * gain


def _params(sem):
    return pltpu.CompilerParams(dimension_semantics=sem, vmem_limit_bytes=VMEM_LIMIT)


def _head_norm(y, norm_m, gain):
    ms = _split_dot(y * y, norm_m)
    return y * lax.rsqrt(ms + RMS_EPS) * gain


def _rope(y, c, s, lo_mask, half):
    sw = jnp.where(lo_mask, pltpu.roll(y, LANES - half, 1), pltpu.roll(y, half, 1))
    return y * c + sw * s


def _lane_iota(shape):
    return lax.broadcasted_iota(I32, shape, 1)


def _row_iota(shape):
    return lax.broadcasted_iota(I32, shape, 0)


def _rope_tables(pos, ftab, period, half):
    ang = pos * ftab
    lane = _lane_iota(ang.shape) % period
    c = jnp.cos(ang)
    s = jnp.sin(ang) * jnp.where(lane < half, -1.0, 1.0)
    return c, s


def _pv_operand(kv):
    head = jnp.where(_row_iota((PV_HEAD_ROWS, kv.shape[0])) == 0, 1.0, 0.0)
    return jnp.concatenate([head, kv.T[HEAD_DIM:, :]], axis=0).astype(BF16)


class _ColumnProjector:
    def __init__(self, xn, w_ref):
        self.xn, self.w_ref, self.groups = xn, w_ref, {}

    def __call__(self, j):
        g, u = divmod(j, PROJ_GROUP)
        if g not in self.groups:
            width = PROJ_GROUP * LANES
            lo = g * width
            hi = min(lo + width, self.w_ref.shape[1])
            self.groups[g] = _dot(self.xn, self.w_ref[:, lo:hi])
        return self.groups[g][:, u * LANES:(u + 1) * LANES]


def _kv_column(yc, nk, gain, c64k, s64k, lo64, first64):
    kn = jnp.where(first64, _head_norm(yc, nk, gain), yc)
    return _rope(kn, c64k, s64k, lo64, 8)


def _rope_trig_kernel(pos_ref, f64_ref, f32_ref, o_ref):
    pos = pos_ref[...]
    c64, s64 = _rope_tables(pos, f64_ref[...], 64, 8)
    c32, s32 = _rope_tables(pos, f32_ref[...], 32, 4)
    o_ref[...] = jnp.concatenate([c64, s64, c32, s32], axis=1)


def _rope_trig(pos2, f64, f32t, tm=1024):
    T = pos2.shape[0]
    return pl.pallas_call(
        _rope_trig_kernel, grid=(T // tm,),
        in_specs=[pl.BlockSpec((tm, 1), lambda i: (i, 0)),
                  pl.BlockSpec(f64.shape, lambda i: (0, 0)), pl.BlockSpec(f32t.shape, lambda i: (0, 0))],
        out_specs=pl.BlockSpec((tm, 4 * LANES), lambda i: (i, 0)),
        out_shape=jax.ShapeDtypeStruct((T, 4 * LANES), F32),
        compiler_params=_params(("parallel",)), name="rope_trig",
    )(pos2, f64, f32t)


def _ab_prep_kernel(x_ref, trig_ref, gmix_ref, w_ref, nq_ref, nk_ref,
                    gaq_ref, gbq_ref, gak_ref, gbk_ref,
                    aq_ref, bq_ref, bkv_ref, bkvt_ref, akv_ref, akvt_ref, iq_ref, ik_ref, iwt_ref, km_ref,
                    *, n_tiles):
    xn = _rms_rows(x_ref[...], gmix_ref[...]).astype(BF16)
    c64, s64, c32, s32 = [trig_ref[:, j * LANES:(j + 1) * LANES] for j in range(4)]
    lane = _lane_iota(c64.shape)
    lo64 = (lane % 64) < 8
    lo32 = (lane % 32) < 4
    first64 = lane < 64
    c64k = jnp.where(first64, c64, 1.0)
    s64k = jnp.where(first64, s64, 0.0)
    first32 = lane < 32
    c32k = jnp.where(first32, c32, 1.0)
    s32k = jnp.where(first32, s32, 0.0)
    nq = nq_ref[...]
    nk = nk_ref[...]

    col = _ColumnProjector(xn, w_ref)

    for j in range(4):
        q = _rope(_head_norm(col(j), nq, gaq_ref[...]), c64, s64, lo64, 8)
        aq_ref[2 * j] = q[:, :64].astype(BF16)
        aq_ref[2 * j + 1] = q[:, 64:].astype(BF16)
    for j in range(4):
        q = _rope(_head_norm(col(4 + j), nq, gbq_ref[...]), c64, s64, lo64, 8)
        bq_ref[2 * j] = q[:, :64].astype(BF16)
        bq_ref[2 * j + 1] = q[:, 64:].astype(BF16)
    blk_onehot = jnp.where(lane == HEAD_DIM + pl.program_id(0) % n_tiles, 1.0, 0.0)
    for h in range(8):
        kv = _kv_column(col(8 + h), nk, gbk_ref[...], c64k, s64k, lo64, first64)
        bkv_ref[h] = jnp.where(first64, kv, blk_onehot).astype(BF16)
        bkvt_ref[h] = _pv_operand(kv)
        km_ref[h:h + 1, :] = jnp.mean(kv, axis=0, keepdims=True)
    kv = _kv_column(col(16), nk, gak_ref[...], c64k, s64k, lo64, first64)
    akv_ref[...] = kv.astype(BF16)
    akvt_ref[...] = _pv_operand(kv)
    for j in range(2):
        q = _rope(col(17 + j), c32, s32, lo32, 4)
        for u in range(4):
            iq_ref[4 * j + u] = q[:, 32 * u:32 * (u + 1)].astype(BF16)
    yc = col(19)
    ik_ref[...] = _rope(yc, c32k, s32k, lo32, 4)[:, :32].astype(BF16)
    iwt_ref[...] = yc.T[32:40, :]


def _ab_prep(x2, trig, gmix, w, tabs, B, S):
    T = x2.shape[0]
    tm = ATT_T
    nt = S // tm
    n_cols = w.shape[1]
    nq, nk, gaq, gbq, gak, gbk = tabs

    def full(a):
        return pl.BlockSpec(a.shape, lambda i: (0,) * a.ndim)

    def hm(width, heads=8):
        return pl.BlockSpec((None, heads, tm, width), lambda i: (i // nt, 0, i % nt, 0))

    def tokm(width):
        return pl.BlockSpec((None, tm, width), lambda i: (i // nt, i % nt, 0))

    out_shape = (
        jax.ShapeDtypeStruct((B, 8, S, 64), BF16),
        jax.ShapeDtypeStruct((B, 8, S, 64), BF16),
        jax.ShapeDtypeStruct((B, 8, S, 128), BF16),
        jax.ShapeDtypeStruct((B, 8, nt, PV_ROWS, tm), BF16),
        jax.ShapeDtypeStruct((B, S, 128), BF16),
        jax.ShapeDtypeStruct((B, nt, PV_ROWS, tm), BF16),
        jax.ShapeDtypeStruct((B, 8, S, 32), BF16),
        jax.ShapeDtypeStruct((B, S, 32), BF16),
        jax.ShapeDtypeStruct((B, 8, S), F32),
        jax.ShapeDtypeStruct((T // tm, 8, 128), F32),
    )
    out_specs = (hm(64), hm(64), hm(128),
                 pl.BlockSpec((None, 8, None, PV_ROWS, tm), lambda i: (i // nt, 0, i % nt, 0, 0)),
                 tokm(128),
                 pl.BlockSpec((None, None, PV_ROWS, tm), lambda i: (i // nt, i % nt, 0, 0)),
                 hm(32), tokm(32),
                 pl.BlockSpec((None, 8, tm), lambda i: (i // nt, 0, i % nt)),
                 pl.BlockSpec((None, 8, 128), lambda i: (i, 0, 0)))
    in_specs = [pl.BlockSpec((tm, D_MODEL), lambda i: (i, 0)),
                pl.BlockSpec((tm, 4 * LANES), lambda i: (i, 0)),
                full(gmix), pl.BlockSpec((D_MODEL, n_cols), lambda i: (0, 0)),
                full(nq), full(nk), full(gaq), full(gbq), full(gak), full(gbk)]
    return pl.pallas_call(
        functools.partial(_ab_prep_kernel, n_tiles=nt), grid=(T // tm,), in_specs=in_specs, out_specs=out_specs,
        out_shape=out_shape, compiler_params=_params(("parallel",)), name="ab_prep",
    )(x2, trig, gmix, w, nq, nk, gaq, gbq, gak, gbk)


def _pad_q(q):
    return jnp.concatenate([q, jnp.zeros_like(q)], axis=1)


def _bias_lanes(q, rows):
    n, tq = rows.shape
    parts = [jnp.zeros((HEAD_DIM, tq), F32), rows]
    if n < HEAD_DIM:
        parts.append(jnp.zeros((HEAD_DIM - n, tq), F32))
    lanes = jnp.concatenate(parts, axis=0).T.astype(BF16)
    return jnp.where(_lane_iota(q.shape) < HEAD_DIM, q, lanes)


class _Flash:
    def __init__(self, m_ref, acc_ref, s_ref, cmax_ref, p_ref, tq):
        self.m_ref, self.acc_ref, self.tq = m_ref, acc_ref, tq
        self.s_ref, self.cmax_ref, self.p_ref = s_ref, cmax_ref, p_ref

    def reset(self):
        self.m_ref[...] = jnp.full(self.m_ref.shape, M_FLOOR, F32)
        self.acc_ref[...] = jnp.zeros(self.acc_ref.shape, F32)

    def _scores(self, buf, qs, kvs, biases):
        tq = self.tq
        for i in range(len(qs)):
            s = _dot_nt(kvs[i], qs[i])
            if biases[i] is not None:
                s = s + biases[i]
            self.s_ref[buf, i, :s.shape[0], :] = s
            self.cmax_ref[buf, :, i * tq:(i + 1) * tq] = jnp.max(s, axis=0, keepdims=True)

    def update(self, qs, kvs, kvts, biases):
        self._scores(0, qs, kvs, biases)
        self._finish(0, kvts)

    def run(self, qs, count, operands):
        def scores(c, buf):
            kvs, _, biases = operands(c)
            self._scores(buf, qs, kvs, biases)

        def finish(c, buf):
            self._finish(buf, operands(c)[1])

        last = jnp.maximum(count - 1, 0)
        scores(0, 0)

        def two_chunks(pp, carry):
            c = 2 * pp
            scores(c + 1, 1)
            finish(c, 0)
            scores(jnp.minimum(c + 2, last), 0)
            finish(c + 1, 1)
            return carry

        lax.fori_loop(0, count // 2, two_chunks, 0)

        @pl.when(count % 2 == 1)
        def _():
            finish(count - 1, 0)

    def _finish(self, buf, kvts):
        n = len(kvts)
        tq = self.tq
        kc = sum(kvt.shape[1] for kvt in kvts[0])
        alphas = []
        for i in range(n):
            cols = slice(i * tq, (i + 1) * tq)
            m = self.m_ref[:, cols]
            m_new = jnp.maximum(m, self.cmax_ref[buf, :, cols])
            p = jnp.exp2(self.s_ref[buf, i, :kc, :] - m_new)
            alpha = jnp.exp2(m - m_new)
            self.m_ref[:, cols] = m_new
            self.p_ref[i, :kc, :] = p.astype(BF16)
            alphas.append(alpha)
        for i in range(n):
            cols = slice(i * tq, (i + 1) * tq)
            pv, r0 = None, 0
            for kvt in kvts[i]:
                part = _dot(kvt, self.p_ref[i, r0:r0 + kvt.shape[1], :])
                pv = part if pv is None else pv + part
                r0 += kvt.shape[1]
            self.acc_ref[:, cols] = alphas[i] * self.acc_ref[:, cols] + pv

    def result(self, slot):
        cols = slice(slot * self.tq, (slot + 1) * self.tq)
        acc = self.acc_ref[:, cols]
        return acc / jnp.maximum(acc[0:1, :], TINY)


def _flash_scratch(n_slots, tq, kc):
    return [pltpu.VMEM((1, n_slots * tq), F32), pltpu.VMEM((PV_ROWS, n_slots * tq), F32),
            pltpu.VMEM((2, n_slots, kc, tq), F32), pltpu.VMEM((2, 1, n_slots * tq), F32),
            pltpu.VMEM((n_slots, kc, tq), BF16)]


def _softmax_direct(qs, kv, kvts, bias):
    scores = [_dot_nt(kv, q) for q in qs]
    probs = []
    for s in scores:
        s = s + bias
        m = jnp.maximum(jnp.max(s, axis=0, keepdims=True), M_FLOOR)
        probs.append(jnp.exp2(s - m))
    inv_ls, outs = [], []
    for p in probs:
        pb = p.astype(BF16)
        o, r0 = None, 0
        for kvt in kvts:
            part = _dot(kvt, pb[r0:r0 + kvt.shape[1]])
            o = part if o is None else o + part
            r0 += kvt.shape[1]
        outs.append(o)
        inv_ls.append(1.0 / jnp.maximum(o[0:1, :], TINY))
    return probs, inv_ls, outs


def _causal_bias(t):
    return jnp.where(_row_iota((t, t)) <= _lane_iota((t, t)), 0.0, MASK_BIAS)


def _store_heads(o_ref, heads_t):
    for u in range(len(heads_t) // 2):
        pair = jnp.concatenate([heads_t[2 * u][PV_HEAD_ROWS:, :], heads_t[2 * u + 1][PV_HEAD_ROWS:, :]], axis=0)
        o_ref[:, u * LANES:(u + 1) * LANES] = pair.T.astype(o_ref.dtype)


def _rank_select_t(v, n_valid, n_top):
    n = v.shape[0]
    row = _row_iota(v.shape)
    rank = jnp.zeros(v.shape, F32)
    for m in range(n):
        vm = v[m:m + 1, :]
        ahead = (vm > v) | ((vm == v) & (m < row))
        if n_valid is not None:
            ahead = ahead & (m < n_valid)
        rank = rank + jnp.where(ahead, 1.0, 0.0)
    sel = rank < n_top
    if n_valid is not None:
        sel = sel & (row < n_valid)
    return jnp.where(sel, 1.0, 0.0)


def _dsa_kernel(iq_ref, iwt_ref, ik_ref, aq_ref, akv_ref, akvt_ref, o_ref,
                sk_ref, half_ref, xcut_ref, *flash_refs, k_top, index_bits):
    t = ATT_T
    i = pl.program_id(1)
    n_ch = i + 1
    kio = _row_iota((t, t))
    qio = _lane_iota((t, t))

    def causal(c):
        return (c - i) * t + kio <= qio

    def score_chunk(c):
        k0 = pl.multiple_of(c * t, t)
        ikc = ik_ref[pl.ds(k0, t), :]
        sc = jnp.zeros((t, t), F32)
        for h in range(DSA_IDX_HEADS):
            logit = _dot_nt(ikc, iq_ref[h])
            sc = sc + iwt_ref[h:h + 1, :] * jnp.maximum(logit, 0.0)
        sc = jnp.where(sc == 0.0, 0.0, sc)
        bits = pltpu.bitcast(sc, I32)
        key = bits ^ ((bits >> 31) & 0x7FFFFFFF)
        key = jnp.where(causal(c), key, INT_MIN)
        sk_ref[c] = key
        half_ref[c] = (key >> 16).astype(I16)

    def score_pair(cc, carry):
        score_chunk(2 * cc)
        score_chunk(2 * cc + 1)
        return carry

    lax.fori_loop(0, (n_ch + 1) // 2, score_pair, 0)

    def count(pred):
        def body(c, acc8):
            ind = jnp.where(pred(sk_ref[c], c), 1.0, 0.0)
            return acc8 + ind.reshape(-1, SUBLANES, t).sum(axis=0)
        acc8 = lax.fori_loop(0, n_ch, body, jnp.zeros((SUBLANES, t), F32))
        return jnp.sum(acc8, axis=0, keepdims=True)

    def count_half(cand):
        rows = 2 * SUBLANES

        def body(cc, acc):
            parts = []
            for c in (2 * cc, 2 * cc + 1):
                ind = jnp.where(half_ref[c] >= cand, jnp.bfloat16(1), jnp.bfloat16(0))
                parts += [ind[rows * j:rows * (j + 1), :] for j in range(t // rows)]
            while len(parts) > 1:
                parts = [parts[2 * j] + parts[2 * j + 1] for j in range(len(parts) // 2)]
            return acc + parts[0].astype(F32)
        acc = lax.fori_loop(0, (n_ch + 1) // 2, body, jnp.zeros((rows, t), F32))
        return jnp.sum(acc, axis=0, keepdims=True)

    def half_search(n_all):
        def bit_step(b, carry):
            v, n_ge_v = carry
            cand = v + lax.shift_left(jnp.int32(1), 15 - b)
            n_ge_cand = count_half(cand.astype(I16))
            ok = n_ge_cand >= k_top
            return jnp.where(ok, cand, v), jnp.where(ok, n_ge_cand, n_ge_v)
        return lax.fori_loop(0, 16, bit_step, (jnp.full((1, t), -(2 ** 15), I32), n_all))

    thr_hi, n_ge_hi = half_search(jnp.full((1, t), t * n_ch, I32).astype(F32))

    def low_half_chunk(c, carry):
        key = sk_ref[c]
        hi = key >> 16
        lo = (key & 0xFFFF) - 2 ** 15
        half_ref[c] = jnp.where(hi > thr_hi, 2 ** 15 - 1, jnp.where(hi < thr_hi, -(2 ** 15), lo)).astype(I16)
        return carry

    lax.fori_loop(0, n_ch, low_half_chunk, 0)
    thr_lo, n_ge = half_search(n_ge_hi)
    thr = lax.shift_left(thr_hi, 16) + (thr_lo + 2 ** 15)

    xcut_ref[...] = jnp.full((1, t), 2 ** 30, I32)

    @pl.when(jnp.max(n_ge) > k_top)
    def _():
        need = k_top - count(lambda blk, c: blk > thr)

        def x_step(b, x):
            cand = x + lax.shift_left(jnp.int32(1), index_bits - 1 - b)
            ties_below = count(lambda blk, c: (blk == thr) & (c * t + kio < cand))
            return jnp.where(ties_below <= need, cand, x)
        xcut_ref[...] = lax.fori_loop(0, index_bits, x_step, jnp.zeros((1, t), I32))

    xcut = xcut_ref[...]

    n_pairs = (n_ch + 1) // 2

    def chunk_bias(c):
        blk = sk_ref[c]
        keep = (blk > thr) | ((blk == thr) & (c * t + kio < xcut))
        return jnp.where(keep & causal(c), 0.0, MASK_BIAS)

    flash = _Flash(*flash_refs, t)
    qs = [_pad_q(aq_ref[h]) for h in range(DSA_HEADS)]
    flash.reset()

    n = DSA_HEADS

    def att_pair(cc):
        c0 = 2 * cc
        k0 = pl.multiple_of(c0 * t, 2 * t)
        kv = akv_ref[pl.ds(k0, 2 * t), :]
        bias = jnp.concatenate([chunk_bias(c0), chunk_bias(c0 + 1)], axis=0)
        return [kv] * n, [[akvt_ref[c0], akvt_ref[c0 + 1]]] * n, [bias] * n

    flash.run(qs, n_pairs, att_pair)
    _store_heads(o_ref, [flash.result(h) for h in range(DSA_HEADS)])


def _dsa_attention(iq, iwt, ik, aq, akv, akvt):
    B, _, S, _ = aq.shape
    t = ATT_T
    nt = S // t
    k_top = min(DSA_TOPK, S // 4)
    in_specs = [
        pl.BlockSpec((None, 8, t, 32), lambda b, i: (b, 0, i, 0)),
        pl.BlockSpec((None, 8, t), lambda b, i: (b, 0, i)),
        pl.BlockSpec((None, S, 32), lambda b, i: (b, 0, 0)),
        pl.BlockSpec((None, 8, t, 64), lambda b, i: (b, 0, i, 0)),
        pl.BlockSpec((None, S, 128), lambda b, i: (b, 0, 0)),
        pl.BlockSpec((None, nt, PV_ROWS, t), lambda b, i: (b, 0, 0, 0)),
    ]
    return pl.pallas_call(
        functools.partial(_dsa_kernel, k_top=k_top, index_bits=S.bit_length()),
        grid=(B, nt), in_specs=in_specs,
        out_specs=pl.BlockSpec((None, t, 512), lambda b, i: (b, i, 0)),
        out_shape=jax.ShapeDtypeStruct((B, S, 512), BF16),
        scratch_shapes=[pltpu.VMEM((nt, t, t), I32), pltpu.VMEM((nt, t, t), I16),
                        pltpu.VMEM((1, t), I32)] + _flash_scratch(DSA_HEADS, t, 2 * t),
        compiler_params=_params(("parallel", "parallel")), name="dsa_attention",
    )(iq, iwt, ik, aq, akv, akvt)


MOBA_HPS = 8


def _moba_kernel(q_ref, kv_ref, kvt_ref, km_ref, o_ref, *flash_refs, n_top):
    t = ATT_T
    own = pl.program_id(2)
    causal = _causal_bias(t)
    flash = _Flash(*flash_refs, t)
    qs = []
    for hh in range(MOBA_HPS):
        q = _pad_q(q_ref[hh])
        km_hi, km_lo = _split_bf16(km_ref[hh])
        gate = _dot_nt(km_hi, q) + _dot_nt(km_lo, q)
        keep = _rank_select_t(gate, own, n_top)
        keep = jnp.where(_row_iota(keep.shape) == own, 1.0, keep)
        qs.append(_bias_lanes(q, (keep - 1.0) * (-MASK_BIAS)))
    flash.reset()

    def operands(cc):
        n0 = 2 * cc
        k0 = pl.multiple_of(n0 * t, 2 * t)
        heads = range(MOBA_HPS)
        return (n0, [kv_ref[hh, pl.ds(k0, 2 * t), :] for hh in heads],
                [[kvt_ref[hh, n0], kvt_ref[hh, n0 + 1]] for hh in heads])

    def past_pair(cc):
        _, kvs, kvts = operands(cc)
        return kvs, kvts, [None] * MOBA_HPS

    flash.run(qs, own // 2, past_pair)
    heads = range(MOBA_HPS)

    @pl.when(own % 2 == 0)
    def _():
        k0 = pl.multiple_of(own * t, t)
        flash.update(qs, [kv_ref[hh, pl.ds(k0, t), :] for hh in heads],
                     [[kvt_ref[hh, own]] for hh in heads], [causal] * MOBA_HPS)

    @pl.when(own % 2 == 1)
    def _():
        _, kvs, kvts = operands(own // 2)
        bias = jnp.concatenate([jnp.zeros((t, t), F32), causal], axis=0)
        flash.update(qs, kvs, kvts, [bias] * MOBA_HPS)

    _store_heads(o_ref, [flash.result(hh) for hh in range(MOBA_HPS)])


def _moba_attention(bq, bkv, bkvt, kmean):
    B, H, S, _ = bq.shape
    t = ATT_T
    hps = MOBA_HPS
    n_blk = S // MOBA_BLOCK
    assert t == MOBA_BLOCK and n_blk % 2 == 0 and H % hps == 0
    n_top = max(1, min(MOBA_TOPK, n_blk - 1))
    in_specs = [
        pl.BlockSpec((None, hps, t, 64), lambda b, h, i: (b, h, i, 0)),
        pl.BlockSpec((None, hps, S, 128), lambda b, h, i: (b, h, 0, 0)),
        pl.BlockSpec((None, hps, n_blk, PV_ROWS, t), lambda b, h, i: (b, h, 0, 0, 0)),
        pl.BlockSpec((None, hps, n_blk, 128), lambda b, h, i: (b, h, 0, 0)),
    ]
    return pl.pallas_call(
        functools.partial(_moba_kernel, n_top=n_top), grid=(B, H // hps, S // t), in_specs=in_specs,
        out_specs=pl.BlockSpec((None, t, hps * 64), lambda b, h, i: (b, i, h)),
        out_shape=jax.ShapeDtypeStruct((B, S, H * 64), BF16),
        scratch_shapes=_flash_scratch(hps, t, 2 * t),
        compiler_params=_params(("parallel", "parallel", "parallel")), name="moba_attention",
    )(bq, bkv, bkvt, kmean)


def _lane_group_norm(y, gain, width):
    outs = []
    for j in range(y.shape[1] // width):
        yc = y[:, j * width:(j + 1) * width]
        outs.append(_rms_rows(yc, gain))
    return jnp.concatenate(outs, axis=1)


def _mem_kv_kernel(m_ref, g_ref, w_ref, gk_ref, o_ref):
    mn = _rms_rows(m_ref[...], g_ref[...]).astype(BF16)
    y = _dot(mn, w_ref[...])
    hw = MEM_HEADS * MEM_HEAD_DIM
    k = _lane_group_norm(y[:, :hw], gk_ref[...], MEM_HEAD_DIM)
    o_ref[...] = jnp.concatenate([k, y[:, hw:]], axis=1).astype(BF16)


def _mem_kv(mem, g, w, gk):
    B, M, _ = mem.shape
    n = w.shape[1]
    return pl.pallas_call(
        _mem_kv_kernel, grid=(B,),
        in_specs=[pl.BlockSpec((None, M, D_MODEL), lambda b: (b, 0, 0)),
                  pl.BlockSpec(g.shape, lambda b: (0, 0)),
                  pl.BlockSpec(w.shape, lambda b: (0, 0)),
                  pl.BlockSpec(gk.shape, lambda b: (0, 0))],
        out_specs=pl.BlockSpec((None, M, n), lambda b: (b, 0, 0)),
        out_shape=jax.ShapeDtypeStruct((B, M, n), BF16),
        compiler_params=_params(("parallel",)), name="mem_kv",
    )(mem, g, w, gk)


def _mem_attend(x, g_ref, wq_ref, gq_ref, kv_ref, wo_ref):
    xn = _rms_rows(x, g_ref[...]).astype(BF16)
    q = _lane_group_norm(_dot(xn, wq_ref[...]), gq_ref[...], MEM_HEAD_DIM).astype(BF16)
    hw = MEM_HEADS * MEM_HEAD_DIM
    scale = MEM_HEAD_DIM ** -0.5
    outs = []
    for h in range(MEM_HEADS):
        cols = slice(h * MEM_HEAD_DIM, (h + 1) * MEM_HEAD_DIM)
        k = kv_ref[:, cols]
        v = kv_ref[:, hw + h * MEM_HEAD_DIM:hw + (h + 1) * MEM_HEAD_DIM]
        s = _dot_nt(q[:, cols], k) * scale
        p = jnp.exp(s - jnp.max(s, axis=-1, keepdims=True))
        p = p / jnp.sum(p, axis=-1, keepdims=True)
        outs.append(_dot(p.astype(BF16), v))
    o = jnp.concatenate(outs, axis=1).astype(BF16)
    return x + _dot(o, wo_ref[...])


def _post_mixer_kernel(*refs, n_in):
    a_refs = refs[:n_in]
    w_refs = refs[n_in:2 * n_in]
    x_ref, gm_ref, wq_ref, gq_ref, kv_ref, wo_ref, gf_ref, wg_ref, wu_ref, wd_ref, o_ref = refs[2 * n_in:]
    x = x_ref[...]
    for a_ref, w_ref in zip(a_refs, w_refs):
        x = x + _dot(a_ref[...], w_ref[...])
    x = _mem_attend(x, gm_ref, wq_ref, gq_ref, kv_ref, wo_ref)
    xn = _rms_rows(x, gf_ref[...]).astype(BF16)
    gate = _dot(xn, wg_ref[...])
    up = _dot(xn, wu_ref[...])
    act = (gate * jax.nn.sigmoid(gate) * up).astype(BF16)
    o_ref[...] = x + _dot(act, wd_ref[...])


def _post_mixer(parts, weights, x2, g_mem, wq, gq, kv, wo, g_ffn, wg, wu, wd, S, tm=512):
    T = x2.shape[0]
    nt = S // tm
    n_in = len(parts)
    M, n = kv.shape[1], kv.shape[2]

    def const(a):
        return pl.BlockSpec(a.shape, lambda i: (0,) * a.ndim, pipeline_mode=pl.Buffered(1))

    in_specs = ([pl.BlockSpec((tm, p.shape[1]), lambda i: (i, 0)) for p in parts]
                + [const(w) for w in weights]
                + [pl.BlockSpec((tm, D_MODEL), lambda i: (i, 0)),
                   const(g_mem), const(wq), const(gq),
                   pl.BlockSpec((None, M, n), lambda i: (i // nt, 0, 0)),
                   const(wo), const(g_ffn), const(wg), const(wu), const(wd)])
    return pl.pallas_call(
        functools.partial(_post_mixer_kernel, n_in=n_in), grid=(T // tm,), in_specs=in_specs,
        out_specs=pl.BlockSpec((tm, D_MODEL), lambda i: (i, 0)),
        out_shape=jax.ShapeDtypeStruct((T, D_MODEL), F32),
        compiler_params=_params(("parallel",)), name="post_mixer",
    )(*parts, *weights, x2, g_mem, wq, gq, kv, wo, g_ffn, wg, wu, wd)


def _nsa_prep_kernel(x_ref, trig_ref, gmix_ref, w_ref, nq_ref, nk_ref,
                     gq_ref, gks_ref, gkw_ref,
                     qc_ref, qr_ref, kvs_ref, kvst_ref, kvw_ref, kvwt_ref, kc_ref, vc_ref, gtt_ref,
                     stage_ref, *, n_tiles):
    xn = _rms_rows(x_ref[...], gmix_ref[...]).astype(BF16)
    c64, s64 = trig_ref[:, :LANES], trig_ref[:, LANES:]
    lane = _lane_iota(c64.shape)
    lo64 = (lane % 64) < 8
    first64 = lane < 64
    c64k = jnp.where(first64, c64, 1.0)
    s64k = jnp.where(first64, s64, 0.0)
    nq = nq_ref[...]
    nk = nk_ref[...]

    col = _ColumnProjector(xn, w_ref)

    for j in range(8):
        qn = _head_norm(col(j), nq, gq_ref[...])
        qr = _rope(qn, c64, s64, lo64, 8)
        qc_ref[2 * j] = qn[:, :64].astype(BF16)
        qc_ref[2 * j + 1] = qn[:, 64:].astype(BF16)
        qr_ref[2 * j] = qr[:, :64].astype(BF16)
        qr_ref[2 * j + 1] = qr[:, 64:].astype(BF16)
    tile = pl.program_id(0) % n_tiles
    sel_blk = tile * (ATT_T // NSA_SEL_LEN) + lax.shift_right_logical(
        _row_iota(c64.shape), NSA_SEL_LEN.bit_length() - 1)
    blk_onehot = jnp.where(lane == HEAD_DIM + sel_blk, 1.0, 0.0)
    for g in range(NSA_GROUPS):
        kv = _kv_column(col(8 + g), nk, gks_ref[...], c64k, s64k, lo64, first64)
        kvs_ref[g] = jnp.where(first64, kv, blk_onehot).astype(BF16)
        kvst_ref[g] = _pv_operand(kv)
        kv = _kv_column(col(12 + g), nk, gkw_ref[...], c64k, s64k, lo64, first64)
        kvw_ref[g] = kv.astype(BF16)
        kvwt_ref[g] = _pv_operand(kv)
    stride = NSA_CMP_STRIDE
    rows = stage_ref.shape[0] // stride
    for out_ref, first in ((kc_ref, 16), (vc_ref, 18)):
        for c in range(2):
            stage_ref[...] = col(first + c)
            for u in range(0, stride, 2):
                pair = [stage_ref[pl.ds(u + v, rows, stride=stride), :] for v in range(2)]
                for h in range(2):
                    halves = [p[:, h * HEAD_DIM:(h + 1) * HEAD_DIM] for p in pair]
                    out_ref[2 * c + h, :, u * HEAD_DIM:(u + 2) * HEAD_DIM] = jnp.concatenate(halves, axis=1)
    gates_t = jax.nn.sigmoid(col(20)).T
    for g in range(NSA_GROUPS):
        gtt_ref[g] = gates_t[12 * g:12 * (g + 1), :]


def _nsa_prep(x2, trig, gmix, w, tabs, B, S):
    T = x2.shape[0]
    tm = ATT_T
    nt = S // tm
    nq, nk, gq, gks, gkw = tabs

    def full(a):
        return pl.BlockSpec(a.shape, lambda i: (0,) * a.ndim)

    def hm(width, heads):
        return pl.BlockSpec((None, heads, tm, width), lambda i: (i // nt, 0, i % nt, 0))

    def hmt(heads):
        return pl.BlockSpec((None, heads, None, PV_ROWS, tm), lambda i: (i // nt, 0, i % nt, 0, 0))

    def tokm(width):
        return pl.BlockSpec((None, tm, width), lambda i: (i // nt, i % nt, 0))

    out_shape = (
        jax.ShapeDtypeStruct((B, 16, S, 64), BF16),
        jax.ShapeDtypeStruct((B, 16, S, 64), BF16),
        jax.ShapeDtypeStruct((B, 4, S, 128), BF16),
        jax.ShapeDtypeStruct((B, 4, nt, PV_ROWS, tm), BF16),
        jax.ShapeDtypeStruct((B, 4, S, 128), BF16),
        jax.ShapeDtypeStruct((B, 4, nt, PV_ROWS, tm), BF16),
        jax.ShapeDtypeStruct((B, 4, S // 16, 1024), F32),
        jax.ShapeDtypeStruct((B, 4, S // 16, 1024), F32),
        jax.ShapeDtypeStruct((B, 4, 12, S), F32),
    )
    rows16 = pl.BlockSpec((None, 4, tm // 16, 1024), lambda i: (i // nt, 0, i % nt, 0))
    out_specs = (hm(64, 16), hm(64, 16), hm(128, 4), hmt(4), hm(128, 4), hmt(4), rows16, rows16,
                 pl.BlockSpec((None, 4, 12, tm), lambda i: (i // nt, 0, 0, i % nt)))
    in_specs = [pl.BlockSpec((tm, D_MODEL), lambda i: (i, 0)),
                pl.BlockSpec((tm, 2 * LANES), lambda i: (i, 0)),
                full(gmix), full(w), full(nq), full(nk), full(gq), full(gks), full(gkw)]
    return pl.pallas_call(
        functools.partial(_nsa_prep_kernel, n_tiles=nt), grid=(T // tm,), in_specs=in_specs, out_specs=out_specs,
        out_shape=out_shape, scratch_shapes=[pltpu.VMEM((tm, LANES), F32)],
        compiler_params=_params(("parallel",)), name="nsa_prep",
    )(x2, trig, gmix, w, nq, nk, gq, gks, gkw)


def _compress_one(x16, pa, pb, w1a, w1b, w2):
    n16 = x16.shape[0]
    h_a = _dot((x16 + pa).astype(BF16), w1a)
    h_b = _dot((x16 + pb).astype(BF16), w1b)
    pre = h_a + pltpu.roll(h_b, n16 - 1, 0)
    act = pre * jax.nn.sigmoid(pre)
    return _dot(act.astype(BF16), w2)


def _compress_kernel(xk_ref, xv_ref, pk_ref, pv_ref, w1k_ref, w1v_ref, w2k_ref, w2v_ref, gk_ref,
                     o_ref, ot_ref):
    half = w1k_ref.shape[0] // 2
    k = _compress_one(xk_ref[...], pk_ref[0:1, :], pk_ref[1:2, :],
                      w1k_ref[:half, :], w1k_ref[half:, :], w2k_ref[...])
    k = _rms_rows(k, gk_ref[...])
    v = _compress_one(xv_ref[...], pv_ref[0:1, :], pv_ref[1:2, :],
                      w1v_ref[:half, :], w1v_ref[half:, :], w2v_ref[...])
    kv = jnp.concatenate([k, v], axis=1)
    o_ref[...] = kv.astype(BF16)
    ot_ref[...] = _pv_operand(kv)


def _compress(xk16, xv16, pk, pv, w1k, w1v, w2k, w2v, gk):
    B, G, n16, width = xk16.shape

    def full(a):
        return pl.BlockSpec(a.shape, lambda b, g: (0,) * a.ndim)

    xspec = pl.BlockSpec((None, None, n16, width), lambda b, g: (b, g, 0, 0))
    return pl.pallas_call(
        _compress_kernel, grid=(B, G),
        in_specs=[xspec, xspec, full(pk), full(pv), full(w1k), full(w1v), full(w2k), full(w2v), full(gk)],
        out_specs=(pl.BlockSpec((None, None, n16, 128), lambda b, g: (b, g, 0, 0)),
                   pl.BlockSpec((None, None, PV_ROWS, n16), lambda b, g: (b, g, 0, 0))),
        out_shape=(jax.ShapeDtypeStruct((B, G, n16, 128), BF16),
                   jax.ShapeDtypeStruct((B, G, PV_ROWS, n16), BF16)),
        compiler_params=_params(("parallel", "parallel")), name="nsa_compress",
    )(xk16, xv16, pk, pv, w1k, w1v, w2k, w2v, gk)


NSA_GPS = 4


def _nsa_tile_masks(i, n16, n_sel, n_cmp, n_top):
    t = ATT_T
    t0 = i * t
    n_id = _row_iota((n16, t))
    q_id = t0 + _lane_iota((n16, t))
    cmp_visible = (n_id < n_cmp) & (n_id * NSA_CMP_STRIDE + (NSA_CMP_LEN - 1) <= q_id)
    b_id = _row_iota((n_sel, n16)) * NSA_SEL_LEN
    r_id = _lane_iota((n_sel, n16)) * NSA_CMP_STRIDE
    cover_t = ((r_id < b_id + NSA_SEL_LEN) & (r_id + NSA_CMP_LEN > b_id)
               & (_lane_iota((n_sel, n16)) < n_cmp))
    n_wc = NSA_WINDOW // t + 1
    cw = jnp.maximum(i - (n_wc - 1), 0)
    dist = (i - cw) * t + _lane_iota((n_wc * t, t)) - _row_iota((n_wc * t, t))
    blk = _row_iota((n_sel, t))
    cur = lax.shift_right_logical(t0 + _lane_iota((n_sel, t)), NSA_SEL_LEN.bit_length() - 1)
    return dict(
        bias_c=jnp.where(cmp_visible, 0.0, MASK_BIAS),
        cover_t=jnp.where(cover_t, 1.0, 0.0).astype(BF16),
        n_wc=n_wc, cw=cw, bias_w=jnp.where((dist >= 0) & (dist < NSA_WINDOW), 0.0, MASK_BIAS),
        forced=(blk == 0) | (blk == cur) | (blk == cur - 1), visible_blk=blk <= cur,
        n_wanted=jnp.minimum(cur[0:1, :] + 1, n_top).astype(F32))


def _nsa_front(qc_ref, qr_ref, kvc_ref, kvct_ref, kvw_ref, kvwt_ref, gtt_ref, part_ref, masks):
    t = ATT_T
    HG = NSA_HEADS // NSA_GROUPS
    n16 = kvc_ref.shape[0]

    p_sum = jnp.zeros((n16, t), F32)
    o_c = []
    probs, inv_ls, outs = _softmax_direct([_pad_q(qc_ref[j]) for j in range(HG)], kvc_ref[...],
                                          [kvct_ref[...]], masks["bias_c"])
    for j in range(HG):
        p_sum = p_sum + probs[j] * inv_ls[j]
        o_c.append(outs[j] * inv_ls[j])

    p_hi, p_lo = _split_bf16(p_sum)
    imp = _dot(masks["cover_t"], p_hi) + _dot(masks["cover_t"], p_lo)

    qs = [_pad_q(qr_ref[j]) for j in range(HG)]
    n_wc, cw = masks["n_wc"], masks["cw"]
    kw0 = pl.multiple_of(cw * t, t)
    _, inv_lw, out_w = _softmax_direct(qs, kvw_ref[pl.ds(kw0, n_wc * t), :],
                                       [kvwt_ref[cw + u] for u in range(n_wc)], masks["bias_w"])
    for j in range(HG):
        part_ref[:, j * t:(j + 1) * t] = (gtt_ref[3 * j:3 * j + 1, :] * o_c[j]
                                          + gtt_ref[3 * j + 2:3 * j + 3, :] * (out_w[j] * inv_lw[j]))

    imp = jnp.where(masks["forced"], NSA_FORCE, imp)
    return qs, jnp.where(masks["visible_blk"], imp, NEG_INF)


RANK_SEGMENT = 16


def _count_larger(imp, acc, rows):
    for m in rows:
        acc = acc + jnp.where(imp[m:m + 1, :] > imp, 1.0, 0.0)
    return acc


def _nsa_kernel(qc_ref, qr_ref, kvc_ref, kvct_ref, kvs_ref, kvst_ref, kvw_ref, kvwt_ref, gtt_ref,
                o_ref, sel_ref, part_ref, *flash_refs, n_cmp, n_top):
    t = ATT_T
    HG = NSA_HEADS // NSA_GROUPS
    n_slots = NSA_GPS * HG
    i = pl.program_id(2)

    masks = _nsa_tile_masks(i, kvc_ref.shape[1], sel_ref.shape[1], n_cmp, n_top)
    n_sel = sel_ref.shape[1]
    qs, imps = [], []
    for g in range(NSA_GPS):
        heads_g = pl.ds(g * HG, HG)
        q_g, imp_g = _nsa_front(
            qc_ref.at[heads_g], qr_ref.at[heads_g], kvc_ref.at[g], kvct_ref.at[g], kvw_ref.at[g],
            kvwt_ref.at[g], gtt_ref.at[g], part_ref.at[:, pl.ds(g * HG * t, HG * t)], masks)
        qs += q_g
        imps.append(imp_g)
        sel_ref[g] = _count_larger(imp_g, jnp.zeros((n_sel, t), F32), range(RANK_SEGMENT))

    for k in range(1, n_sel // RANK_SEGMENT):
        @pl.when((i + 1) * (t // NSA_SEL_LEN) > k * RANK_SEGMENT)
        def _():
            for g in range(NSA_GPS):
                sel_ref[g] = _count_larger(imps[g], sel_ref[g],
                                           range(k * RANK_SEGMENT, (k + 1) * RANK_SEGMENT))

    miss = None
    for g in range(NSA_GPS):
        sel_fast = sel_ref[g] < n_top
        n_picked = jnp.sum(jnp.where(sel_fast & masks["visible_blk"], 1.0, 0.0), axis=0, keepdims=True)
        sel_ref[g] = jnp.where(sel_fast, 0.0, MASK_BIAS)
        miss_g = jnp.abs(n_picked - masks["n_wanted"])
        miss = miss_g if miss is None else jnp.maximum(miss, miss_g)

    @pl.when(jnp.max(miss) > 0.0)
    def _():
        for g in range(NSA_GPS):
            sel_ref[g] = (_rank_select_t(imps[g], None, n_top) - 1.0) * (-MASK_BIAS)

    qs_sel = [_bias_lanes(qs[s], sel_ref[s // HG]) for s in range(n_slots)]
    flash = _Flash(*flash_refs, t)
    flash.reset()

    def sel_operands(c):
        k0 = pl.multiple_of(c * t, t)
        kvs = [kvs_ref[s // HG, pl.ds(k0, t), :] for s in range(n_slots)]
        kvts = [[kvst_ref[s // HG, c]] for s in range(n_slots)]
        return kvs, kvts

    def past_chunk(c):
        kvs, kvts = sel_operands(c)
        return kvs, kvts, [None] * n_slots

    flash.run(qs_sel, i, past_chunk)
    kvs, kvts = sel_operands(i)
    flash.update(qs_sel, kvs, kvts, [_causal_bias(t)] * n_slots)

    heads = []
    for s in range(n_slots):
        g, j = divmod(s, HG)
        heads.append(part_ref[:, s * t:(s + 1) * t] + gtt_ref[g, 3 * j + 1:3 * j + 2, :] * flash.result(s))
    _store_heads(o_ref, heads)


def _nsa_attention(qc, qr, kvc, kvct, kvs, kvst, kvw, kvwt, gates_t):
    B, H, S, _ = qc.shape
    G = NSA_GROUPS
    HG = H // G
    t = ATT_T
    nt = S // t
    n16 = kvc.shape[2]
    n_cmp = (S - NSA_CMP_LEN) // NSA_CMP_STRIDE + 1
    n_sel = S // NSA_SEL_LEN
    n_top = min(NSA_SEL_TOPK, n_sel)
    gps = NSA_GPS
    assert G % gps == 0 and nt % 2 == 0 and n_sel <= HEAD_DIM and S >= (NSA_WINDOW // t + 1) * t
    qspec = pl.BlockSpec((None, gps * HG, t, 64), lambda b, g, i: (b, g, i, 0))
    once = pl.Buffered(1)
    kvspec = pl.BlockSpec((None, gps, S, 128), lambda b, g, i: (b, g, 0, 0), pipeline_mode=once)
    kvtspec = pl.BlockSpec((None, gps, nt, PV_ROWS, t), lambda b, g, i: (b, g, 0, 0, 0), pipeline_mode=once)
    in_specs = [qspec, qspec,
                pl.BlockSpec((None, gps, n16, 128), lambda b, g, i: (b, g, 0, 0)),
                pl.BlockSpec((None, gps, PV_ROWS, n16), lambda b, g, i: (b, g, 0, 0)),
                kvspec, kvtspec, kvspec, kvtspec,
                pl.BlockSpec((None, gps, 12, t), lambda b, g, i: (b, g, 0, i))]
    return pl.pallas_call(
        functools.partial(_nsa_kernel, n_cmp=n_cmp, n_top=n_top), grid=(B, G // gps, nt), in_specs=in_specs,
        out_specs=pl.BlockSpec((None, t, gps * HG * 64), lambda b, g, i: (b, i, g)),
        out_shape=jax.ShapeDtypeStruct((B, S, H * 64), BF16),
        scratch_shapes=[pltpu.VMEM((gps, n_sel, t), F32), pltpu.VMEM((PV_ROWS, gps * HG * t), F32)]
                       + _flash_scratch(gps * HG, t, t),
        compiler_params=_params(("parallel", "parallel", "parallel")), name="nsa_attention",
    )(qc, qr, kvc, kvct, kvs, kvst, kvw, kvwt, gates_t)


def _rope_freq_row(period, rot):
    half = rot // 2
    inv_freq = ROPE_THETA ** (-(jnp.arange(half, dtype=F32) * 2.0 / rot))
    lane = jnp.arange(LANES) % period
    f = jnp.where(lane < rot, inv_freq[lane % half], 0.0)
    return f.reshape(1, LANES).astype(F32)


def _norm_matrices():
    r = jnp.arange(LANES)
    same = (r[:, None] // 64) == (r[None, :] // 64)
    nq = jnp.where(same, 1.0 / 64, 0.0).astype(BF16)
    nk = jnp.where(same & (r[:, None] < 64), 1.0 / 64, 0.0).astype(BF16)
    return nq, nk


def _q_gain(g):
    return (jnp.tile(g.astype(F32), 2) * Q_SCALE).reshape(1, LANES)


def _k_gain(g):
    return jnp.concatenate([g.astype(F32), jnp.ones((64,), F32)]).reshape(1, LANES)


def _interleave_kv(wk, wv, n_heads):
    d = wk.shape[0]
    wk = wk.reshape(d, n_heads, 64)
    wv = wv.reshape(d, n_heads, 64)
    return jnp.concatenate([wk, wv], axis=2).reshape(d, n_heads * 128)


def _split_cols(w, sizes):
    out, start = [], 0
    for n in sizes:
        out.append(w[:, start:start + n])
        start += n
    return out


def _mixer_layer0(x2, trig, B, S, gmix, w_in, w_out, a_q_norm, a_k_norm, b_q_norm, b_k_norm):
    sizes = (512, 64, 64, 256, 32, 8, 512, 512, 512)
    waq, wak, wav, wiq, wik, wiw, wbq, wbk, wbv = _split_cols(w_in, sizes)
    pad = jnp.zeros((D_MODEL, LANES - 40), w_in.dtype)
    w = jnp.concatenate([waq, wbq, _interleave_kv(wbk, wbv, 8), wak, wav, wiq, wik, wiw, pad],
                        axis=1).astype(BF16)
    nq, nk = _norm_matrices()
    tabs = (nq, nk, _q_gain(a_q_norm), _q_gain(b_q_norm), _k_gain(a_k_norm), _k_gain(b_k_norm))
    aq, bq, bkv, bkvt, akv, akvt, iq, ik, iwt, km = _ab_prep(x2, trig, gmix, w, tabs, B, S)
    n_blk = S // MOBA_BLOCK
    kmean = km.reshape(B, n_blk, 8, 128).transpose(0, 2, 1, 3)
    o_a = _dsa_attention(iq, iwt, ik, aq, akv, akvt).reshape(B * S, 512)
    o_b = _moba_attention(bq, bkv, bkvt, kmean).reshape(B * S, 512)
    w_out = w_out.astype(BF16)
    return [o_a, o_b], [w_out[:512], w_out[512:]]


def _mixer_layer1(x2, trig, B, S, gmix, w_in, w_out, q_norm, kcmp_norm, ksel_norm, kwin_norm,
                  pos_k, pos_v, w1_k, w2_k, w1_v, w2_v):
    G = NSA_GROUPS
    sizes = (1024,) + (256,) * 6 + (48,)
    wq, wkc, wvc, wks, wvs, wkw, wvw, wgt = _split_cols(w_in, sizes)
    pad = jnp.zeros((D_MODEL, LANES - 48), w_in.dtype)
    w = jnp.concatenate([wq, _interleave_kv(wks, wvs, G), _interleave_kv(wkw, wvw, G),
                         wkc, wvc, wgt, pad], axis=1).astype(BF16)
    nq, nk = _norm_matrices()
    tabs = (nq, nk, _q_gain(q_norm), _k_gain(ksel_norm), _k_gain(kwin_norm))
    qc, qr, kvs, kvst, kvw, kvwt, kc16, vc16, gates_t = _nsa_prep(x2, trig, gmix, w, tabs, B, S)

    def pos_rows(p):
        return p.astype(F32).reshape(2, NSA_CMP_STRIDE * HEAD_DIM)

    kvc, kvct = _compress(kc16, vc16, pos_rows(pos_k), pos_rows(pos_v),
                          w1_k.astype(BF16), w1_v.astype(BF16), w2_k.astype(BF16), w2_v.astype(BF16),
                          kcmp_norm.astype(F32).reshape(1, HEAD_DIM))
    o = _nsa_attention(qc, qr, kvc, kvct, kvs, kvst, kvw, kvwt, gates_t)
    return [o.reshape(B * S, NSA_HEADS * HEAD_DIM)], [w_out.astype(BF16)]


def _finish_layer(parts, weights, x2, mem, S, g_mem, g_src, w_q, w_kv, w_o, q_norm, k_norm,
                  g_ffn, ffn_w_in, ffn_w_out):
    row = lambda v: v.astype(F32).reshape(1, -1)
    kv = _mem_kv(mem, row(g_src), w_kv.astype(BF16), row(k_norm))
    wg = ffn_w_in[:, :D_FF].astype(BF16)
    wu = ffn_w_in[:, D_FF:].astype(BF16)
    return _post_mixer(parts, weights, x2, row(g_mem), w_q.astype(BF16), row(q_norm), kv, w_o.astype(BF16),
                       row(g_ffn), wg, wu, ffn_w_out.astype(BF16), S)


def kernel(x, mem, positions, norm_mix, norm_mem, norm_mem_src, norm_ffn, ab_w_in, ab_w_out, dsa_q_norm, dsa_k_norm, moba_q_norm, moba_k_norm, nsa_w_in, nsa_w_out, nsa_q_norm, nsa_kcmp_norm, nsa_ksel_norm, nsa_kwin_norm, nsa_cmp_pos_k, nsa_cmp_pos_v, nsa_cmp_w1_k, nsa_cmp_w2_k, nsa_cmp_w1_v, nsa_cmp_w2_v, mem_w_q, mem_w_kv, mem_w_o, mem_q_norm, mem_k_norm, ffn_w_in, ffn_w_out):
    B, S, D = x.shape
    depth = norm_mix.shape[0]
    x2 = x.reshape(B * S, D)
    trig = _rope_trig(positions.astype(F32).reshape(B * S, 1), _rope_freq_row(64, 16), _rope_freq_row(32, 8))
    row = lambda v: v.astype(F32).reshape(1, -1)
    for i in range(depth):
        j = i // 2
        if i % 2 == 0:
            parts, weights = _mixer_layer0(x2, trig, B, S, row(norm_mix[i]), ab_w_in[j], ab_w_out[j],
                               dsa_q_norm[j], dsa_k_norm[j], moba_q_norm[j], moba_k_norm[j])
        else:
            parts, weights = _mixer_layer1(x2, trig, B, S, row(norm_mix[i]), nsa_w_in[j], nsa_w_out[j],
                               nsa_q_norm[j], nsa_kcmp_norm[j], nsa_ksel_norm[j], nsa_kwin_norm[j],
                               nsa_cmp_pos_k[j], nsa_cmp_pos_v[j], nsa_cmp_w1_k[j], nsa_cmp_w2_k[j],
                               nsa_cmp_w1_v[j], nsa_cmp_w2_v[j])
        x2 = _finish_layer(parts, weights, x2, mem, S, norm_mem[i], norm_mem_src[i], mem_w_q[i], mem_w_kv[i],
                           mem_w_o[i], mem_q_norm[i], mem_k_norm[i], norm_ffn[i], ffn_w_in[i], ffn_w_out[i])
    return x2.reshape(B, S, D)
```

```python
import functools
import math

import jax
import jax.numpy as jnp
from jax import lax
from jax.experimental import pallas as pl
from jax.experimental.pallas import tpu as pltpu

F32 = jnp.float32
BF16 = jnp.bfloat16
I32 = jnp.int32
I16 = jnp.int16

D_MODEL = 1024
N_MEM = 256
HEAD_DIM = 64
ROPE_THETA = 500000.0
RMS_EPS = 1e-6
NEG_INF = -1e30
TINY = 1e-20

DSA_HEADS = 8
DSA_IDX_HEADS = 8
DSA_IDX_DIM = 32
DSA_TOPK = 256
MOBA_HEADS = 8
MOBA_BLOCK = 256
MOBA_TOPK = 3
NSA_HEADS = 16
NSA_GROUPS = 4
NSA_CMP_LEN = 32
NSA_CMP_STRIDE = 16
NSA_SEL_LEN = 64
NSA_SEL_TOPK = 16
NSA_WINDOW = 512
NSA_FORCE = 1e4
NSA_GATE_ROWS = 3 * (NSA_HEADS // NSA_GROUPS)
MEM_HEADS = 4
MEM_HEAD_DIM = 128
D_FF = ((8 * D_MODEL + 3 * 256 - 1) // (3 * 256)) * 256

LANES = 128
SUBLANES = 8
INT_MIN = -(2 ** 31)
VMEM_LIMIT = 60 * 1024 * 1024

PV_HEAD_ROWS = 16
PV_ROWS = PV_HEAD_ROWS + HEAD_DIM
PROJ_GROUP = 4
ATT_T = 256
MASK_BIAS = -1e30
M_FLOOR = -1e29
LOG2E = math.log2(math.e)
Q_SCALE = HEAD_DIM ** -0.5 * LOG2E

NT_DIMS = (((1,), (1,)), ((), ()))


def _dot(a, b):
    return jnp.dot(a, b, preferred_element_type=F32)


def _dot_nt(a, b):
    return lax.dot_general(a, b, NT_DIMS, preferred_element_type=F32)


def _split_bf16(a):
    hi = a.astype(BF16)
    return hi, (a - hi.astype(F32)).astype(BF16)


def _split_dot(a, b):
    hi, lo = _split_bf16(a)
    return _dot(hi, b) + _dot(lo, b)


def _rms_rows(x, gain):
    ms = jnp.mean(x * x, axis=-1, keepdims=True)
    return x * lax.rsqrt(ms + RMS_EPS) * gain


def _params(sem):
    return pltpu.CompilerParams(dimension_semantics=sem, vmem_limit_bytes=VMEM_LIMIT)


def _head_norm(y, norm_m, gain):
    ms = _split_dot(y * y, norm_m)
    return y * lax.rsqrt(ms + RMS_EPS) * gain


def _rope(y, c, s, lo_mask, half):
    sw = jnp.where(lo_mask, pltpu.roll(y, LANES - half, 1), pltpu.roll(y, half, 1))
    return y * c + sw * s


def _lane_iota(shape):
    return lax.broadcasted_iota(I32, shape, 1)


def _row_iota(shape):
    return lax.broadcasted_iota(I32, shape, 0)


def _rope_tables(pos, ftab, period, half):
    ang = pos * ftab
    lane = _lane_iota(ang.shape) % period
    c = jnp.cos(ang)
    s = jnp.sin(ang) * jnp.where(lane < half, -1.0, 1.0)
    return c, s


def _pv_operand(kv):
    head = jnp.where(_row_iota((PV_HEAD_ROWS, kv.shape[0])) == 0, 1.0, 0.0)
    return jnp.concatenate([head, kv.T[HEAD_DIM:, :]], axis=0).astype(BF16)


class _ColumnProjector:
    def __init__(self, xn, w_ref):
        self.xn, self.w_ref, self.groups = xn, w_ref, {}

    def __call__(self, j):
        g, u = divmod(j, PROJ_GROUP)
        if g not in self.groups:
            width = PROJ_GROUP * LANES
            lo = g * width
            hi = min(lo + width, self.w_ref.shape[1])
            self.groups[g] = _dot(self.xn, self.w_ref[:, lo:hi])
        return self.groups[g][:, u * LANES:(u + 1) * LANES]


def _kv_column(yc, nk, gain, c64k, s64k, lo64, first64):
    kn = jnp.where(first64, _head_norm(yc, nk, gain), yc)
    return _rope(kn, c64k, s64k, lo64, 8)


def _rope_trig_kernel(pos_ref, f64_ref, f32_ref, o_ref):
    pos = pos_ref[...]
    c64, s64 = _rope_tables(pos, f64_ref[...], 64, 8)
    c32, s32 = _rope_tables(pos, f32_ref[...], 32, 4)
    o_ref[...] = jnp.concatenate([c64, s64, c32, s32], axis=1)


def _rope_trig(pos2, f64, f32t, tm=1024):
    T = pos2.shape[0]
    return pl.pallas_call(
        _rope_trig_kernel, grid=(T // tm,),
        in_specs=[pl.BlockSpec((tm, 1), lambda i: (i, 0)),
                  pl.BlockSpec(f64.shape, lambda i: (0, 0)), pl.BlockSpec(f32t.shape, lambda i: (0, 0))],
        out_specs=pl.BlockSpec((tm, 4 * LANES), lambda i: (i, 0)),
        out_shape=jax.ShapeDtypeStruct((T, 4 * LANES), F32),
        compiler_params=_params(("parallel",)), name="rope_trig",
    )(pos2, f64, f32t)


def _ab_prep_kernel(x_ref, trig_ref, gmix_ref, w_ref, nq_ref, nk_ref,
                    gaq_ref, gbq_ref, gak_ref, gbk_ref,
                    aq_ref, bq_ref, bkv_ref, bkvt_ref, akv_ref, akvt_ref, iq_ref, ik_ref, iwt_ref, km_ref,
                    *, n_tiles):
    xn = _rms_rows(x_ref[...], gmix_ref[...]).astype(BF16)
    c64, s64, c32, s32 = [trig_ref[:, j * LANES:(j + 1) * LANES] for j in range(4)]
    lane = _lane_iota(c64.shape)
    lo64 = (lane % 64) < 8
    lo32 = (lane % 32) < 4
    first64 = lane < 64
    c64k = jnp.where(first64, c64, 1.0)
    s64k = jnp.where(first64, s64, 0.0)
    first32 = lane < 32
    c32k = jnp.where(first32, c32, 1.0)
    s32k = jnp.where(first32, s32, 0.0)
    nq = nq_ref[...]
    nk = nk_ref[...]

    col = _ColumnProjector(xn, w_ref)

    for j in range(4):
        q = _rope(_head_norm(col(j), nq, gaq_ref[...]), c64, s64, lo64, 8)
        aq_ref[2 * j] = q[:, :64].astype(BF16)
        aq_ref[2 * j + 1] = q[:, 64:].astype(BF16)
    for j in range(4):
        q = _rope(_head_norm(col(4 + j), nq, gbq_ref[...]), c64, s64, lo64, 8)
        bq_ref[2 * j] = q[:, :64].astype(BF16)
        bq_ref[2 * j + 1] = q[:, 64:].astype(BF16)
    blk_onehot = jnp.where(lane == HEAD_DIM + pl.program_id(0) % n_tiles, 1.0, 0.0)
    for h in range(8):
        kv = _kv_column(col(8 + h), nk, gbk_ref[...], c64k, s64k, lo64, first64)
        bkv_ref[h] = jnp.where(first64, kv, blk_onehot).astype(BF16)
        bkvt_ref[h] = _pv_operand(kv)
        km_ref[h:h + 1, :] = jnp.mean(kv, axis=0, keepdims=True)
    kv = _kv_column(col(16), nk, gak_ref[...], c64k, s64k, lo64, first64)
    akv_ref[...] = kv.astype(BF16)
    akvt_ref[...] = _pv_operand(kv)
    for j in range(2):
        q = _rope(col(17 + j), c32, s32, lo32, 4)
        for u in range(4):
            iq_ref[4 * j + u] = q[:, 32 * u:32 * (u + 1)].astype(BF16)
    yc = col(19)
    ik_ref[...] = _rope(yc, c32k, s32k, lo32, 4)[:, :32].astype(BF16)
    iwt_ref[...] = yc.T[32:40, :]


def _ab_prep(x2, trig, gmix, w, tabs, B, S):
    T = x2.shape[0]
    tm = ATT_T
    nt = S // tm
    n_cols = w.shape[1]
    nq, nk, gaq, gbq, gak, gbk = tabs

    def full(a):
        return pl.BlockSpec(a.shape, lambda i: (0,) * a.ndim)

    def hm(width, heads=8):
        return pl.BlockSpec((None, heads, tm, width), lambda i: (i // nt, 0, i % nt, 0))

    def tokm(width):
        return pl.BlockSpec((None, tm, width), lambda i: (i // nt, i % nt, 0))

    out_shape = (
        jax.ShapeDtypeStruct((B, 8, S, 64), BF16),
        jax.ShapeDtypeStruct((B, 8, S, 64), BF16),
        jax.ShapeDtypeStruct((B, 8, S, 128), BF16),
        jax.ShapeDtypeStruct((B, 8, nt, PV_ROWS, tm), BF16),
        jax.ShapeDtypeStruct((B, S, 128), BF16),
        jax.ShapeDtypeStruct((B, nt, PV_ROWS, tm), BF16),
        jax.ShapeDtypeStruct((B, 8, S, 32), BF16),
        jax.ShapeDtypeStruct((B, S, 32), BF16),
        jax.ShapeDtypeStruct((B, 8, S), F32),
        jax.ShapeDtypeStruct((T // tm, 8, 128), F32),
    )
    out_specs = (hm(64), hm(64), hm(128),
                 pl.BlockSpec((None, 8, None, PV_ROWS, tm), lambda i: (i // nt, 0, i % nt, 0, 0)),
                 tokm(128),
                 pl.BlockSpec((None, None, PV_ROWS, tm), lambda i: (i // nt, i % nt, 0, 0)),
                 hm(32), tokm(32),
                 pl.BlockSpec((None, 8, tm), lambda i: (i // nt, 0, i % nt)),
                 pl.BlockSpec((None, 8, 128), lambda i: (i, 0, 0)))
    in_specs = [pl.BlockSpec((tm, D_MODEL), lambda i: (i, 0)),
                pl.BlockSpec((tm, 4 * LANES), lambda i: (i, 0)),
                full(gmix), pl.BlockSpec((D_MODEL, n_cols), lambda i: (0, 0)),
                full(nq), full(nk), full(gaq), full(gbq), full(gak), full(gbk)]
    return pl.pallas_call(
        functools.partial(_ab_prep_kernel, n_tiles=nt), grid=(T // tm,), in_specs=in_specs, out_specs=out_specs,
        out_shape=out_shape, compiler_params=_params(("parallel",)), name="ab_prep",
    )(x2, trig, gmix, w, nq, nk, gaq, gbq, gak, gbk)


def _pad_q(q):
    return jnp.concatenate([q, jnp.zeros_like(q)], axis=1)


def _bias_lanes(q, rows):
    n, tq = rows.shape
    parts = [jnp.zeros((HEAD_DIM, tq), F32), rows]
    if n < HEAD_DIM:
        parts.append(jnp.zeros((HEAD_DIM - n, tq), F32))
    lanes = jnp.concatenate(parts, axis=0).T.astype(BF16)
    return jnp.where(_lane_iota(q.shape) < HEAD_DIM, q, lanes)


class _Flash:
    def __init__(self, m_ref, acc_ref, s_ref, cmax_ref, p_ref, tq):
        self.m_ref, self.acc_ref, self.tq = m_ref, acc_ref, tq
        self.s_ref, self.cmax_ref, self.p_ref = s_ref, cmax_ref, p_ref

    def reset(self):
        self.m_ref[...] = jnp.full(self.m_ref.shape, M_FLOOR, F32)
        self.acc_ref[...] = jnp.zeros(self.acc_ref.shape, F32)

    def _scores(self, buf, qs, kvs, biases):
        tq = self.tq
        for i in range(len(qs)):
            s = _dot_nt(kvs[i], qs[i])
            if biases[i] is not None:
                s = s + biases[i]
            self.s_ref[buf, i, :s.shape[0], :] = s
            self.cmax_ref[buf, :, i * tq:(i + 1) * tq] = jnp.max(s, axis=0, keepdims=True)

    def update(self, qs, kvs, kvts, biases):
        self._scores(0, qs, kvs, biases)
        self._finish(0, kvts)

    def run(self, qs, count, operands):
        def scores(c, buf):
            kvs, _, biases = operands(c)
            self._scores(buf, qs, kvs, biases)

        def finish(c, buf):
            self._finish(buf, operands(c)[1])

        last = jnp.maximum(count - 1, 0)
        scores(0, 0)

        def two_chunks(pp, carry):
            c = 2 * pp
            scores(c + 1, 1)
            finish(c, 0)
            scores(jnp.minimum(c + 2, last), 0)
            finish(c + 1, 1)
            return carry

        lax.fori_loop(0, count // 2, two_chunks, 0)

        @pl.when(count % 2 == 1)
        def _():
            finish(count - 1, 0)

    def _finish(self, buf, kvts):
        n = len(kvts)
        tq = self.tq
        kc = sum(kvt.shape[1] for kvt in kvts[0])
        alphas = []
        for i in range(n):
            cols = slice(i * tq, (i + 1) * tq)
            m = self.m_ref[:, cols]
            m_new = jnp.maximum(m, self.cmax_ref[buf, :, cols])
            p = jnp.exp2(self.s_ref[buf, i, :kc, :] - m_new)
            alpha = jnp.exp2(m - m_new)
            self.m_ref[:, cols] = m_new
            self.p_ref[i, :kc, :] = p.astype(BF16)
            alphas.append(alpha)
        for i in range(n):
            cols = slice(i * tq, (i + 1) * tq)
            pv, r0 = None, 0
            for kvt in kvts[i]:
                part = _dot(kvt, self.p_ref[i, r0:r0 + kvt.shape[1], :])
                pv = part if pv is None else pv + part
                r0 += kvt.shape[1]
            self.acc_ref[:, cols] = alphas[i] * self.acc_ref[:, cols] + pv

    def result(self, slot):
        cols = slice(slot * self.tq, (slot + 1) * self.tq)
        acc = self.acc_ref[:, cols]
        return acc / jnp.maximum(acc[0:1, :], TINY)


def _flash_scratch(n_slots, tq, kc):
    return [pltpu.VMEM((1, n_slots * tq), F32), pltpu.VMEM((PV_ROWS, n_slots * tq), F32),
            pltpu.VMEM((2, n_slots, kc, tq), F32), pltpu.VMEM((2, 1, n_slots * tq), F32),
            pltpu.VMEM((n_slots, kc, tq), BF16)]


def _softmax_direct(qs, kv, kvts, bias):
    scores = [_dot_nt(kv, q) for q in qs]
    probs = []
    for s in scores:
        s = s + bias
        m = jnp.maximum(jnp.max(s, axis=0, keepdims=True), M_FLOOR)
        probs.append(jnp.exp2(s - m))
    inv_ls, outs = [], []
    for p in probs:
        pb = p.astype(BF16)
        o, r0 = None, 0
        for kvt in kvts:
            part = _dot(kvt, pb[r0:r0 + kvt.shape[1]])
            o = part if o is None else o + part
            r0 += kvt.shape[1]
        outs.append(o)
        inv_ls.append(1.0 / jnp.maximum(o[0:1, :], TINY))
    return probs, inv_ls, outs


def _causal_bias(t):
    return jnp.where(_row_iota((t, t)) <= _lane_iota((t, t)), 0.0, MASK_BIAS)


def _store_heads(o_ref, heads_t):
    for u in range(len(heads_t) // 2):
        pair = jnp.concatenate([heads_t[2 * u][PV_HEAD_ROWS:, :], heads_t[2 * u + 1][PV_HEAD_ROWS:, :]], axis=0)
        o_ref[:, u * LANES:(u + 1) * LANES] = pair.T.astype(o_ref.dtype)


def _rank_select_t(v, n_valid, n_top):
    n = v.shape[0]
    row = _row_iota(v.shape)
    rank = jnp.zeros(v.shape, F32)
    for m in range(n):
        vm = v[m:m + 1, :]
        ahead = (vm > v) | ((vm == v) & (m < row))
        if n_valid is not None:
            ahead = ahead & (m < n_valid)
        rank = rank + jnp.where(ahead, 1.0, 0.0)
    sel = rank < n_top
    if n_valid is not None:
        sel = sel & (row < n_valid)
    return jnp.where(sel, 1.0, 0.0)


def _dsa_kernel(iq_ref, iwt_ref, ik_ref, aq_ref, akv_ref, akvt_ref, o_ref,
                sk_ref, half_ref, xcut_ref, *flash_refs, k_top, index_bits):
    t = ATT_T
    i = pl.program_id(1)
    n_ch = i + 1
    kio = _row_iota((t, t))
    qio = _lane_iota((t, t))

    def causal(c):
        return (c - i) * t + kio <= qio

    def score_chunk(c):
        k0 = pl.multiple_of(c * t, t)
        ikc = ik_ref[pl.ds(k0, t), :]
        sc = jnp.zeros((t, t), F32)
        for h in range(DSA_IDX_HEADS):
            logit = _dot_nt(ikc, iq_ref[h])
            sc = sc + iwt_ref[h:h + 1, :] * jnp.maximum(logit, 0.0)
        sc = jnp.where(sc == 0.0, 0.0, sc)
        bits = pltpu.bitcast(sc, I32)
        key = bits ^ ((bits >> 31) & 0x7FFFFFFF)
        key = jnp.where(causal(c), key, INT_MIN)
        sk_ref[c] = key
        half_ref[c] = (key >> 16).astype(I16)

    def score_pair(cc, carry):
        score_chunk(2 * cc)
        score_chunk(2 * cc + 1)
        return carry

    lax.fori_loop(0, (n_ch + 1) // 2, score_pair, 0)

    def count(pred):
        def body(c, acc8):
            ind = jnp.where(pred(sk_ref[c], c), 1.0, 0.0)
            return acc8 + ind.reshape(-1, SUBLANES, t).sum(axis=0)
        acc8 = lax.fori_loop(0, n_ch, body, jnp.zeros((SUBLANES, t), F32))
        return jnp.sum(acc8, axis=0, keepdims=True)

    def count_half(cand):
        rows = 2 * SUBLANES

        def body(cc, acc):
            parts = []
            for c in (2 * cc, 2 * cc + 1):
                ind = jnp.where(half_ref[c] >= cand, jnp.bfloat16(1), jnp.bfloat16(0))
                parts += [ind[rows * j:rows * (j + 1), :] for j in range(t // rows)]
            while len(parts) > 1:
                parts = [parts[2 * j] + parts[2 * j + 1] for j in range(len(parts) // 2)]
            return acc + parts[0].astype(F32)
        acc = lax.fori_loop(0, (n_ch + 1) // 2, body, jnp.zeros((rows, t), F32))
        return jnp.sum(acc, axis=0, keepdims=True)

    def half_search(n_all):
        def bit_step(b, carry):
            v, n_ge_v = carry
            cand = v + lax.shift_left(jnp.int32(1), 15 - b)
            n_ge_cand = count_half(cand.astype(I16))
            ok = n_ge_cand >= k_top
            return jnp.where(ok, cand, v), jnp.where(ok, n_ge_cand, n_ge_v)
        return lax.fori_loop(0, 16, bit_step, (jnp.full((1, t), -(2 ** 15), I32), n_all))

    thr_hi, n_ge_hi = half_search(jnp.full((1, t), t * n_ch, I32).astype(F32))

    def low_half_pair(cc, carry):
        for c in (2 * cc, 2 * cc + 1):
            key = sk_ref[c]
            hi = key >> 16
            lo = (key & 0xFFFF) - 2 ** 15
            half_ref[c] = jnp.where(hi > thr_hi, 2 ** 15 - 1,
                                    jnp.where(hi < thr_hi, -(2 ** 15), lo)).astype(I16)
        return carry

    lax.fori_loop(0, (n_ch + 1) // 2, low_half_pair, 0)
    thr_lo, n_ge = half_search(n_ge_hi)
    thr = lax.shift_left(thr_hi, 16) + (thr_lo + 2 ** 15)

    xcut_ref[...] = jnp.full((1, t), 2 ** 30, I32)

    @pl.when(jnp.max(n_ge) > k_top)
    def _():
        need = k_top - count(lambda blk, c: blk > thr)

        def x_step(b, x):
            cand = x + lax.shift_left(jnp.int32(1), index_bits - 1 - b)
            ties_below = count(lambda blk, c: (blk == thr) & (c * t + kio < cand))
            return jnp.where(ties_below <= need, cand, x)
        xcut_ref[...] = lax.fori_loop(0, index_bits, x_step, jnp.zeros((1, t), I32))

    xcut = xcut_ref[...]

    n_pairs = (n_ch + 1) // 2

    def chunk_bias(c):
        blk = sk_ref[c]
        keep = (blk > thr) | ((blk == thr) & (c * t + kio < xcut))
        return jnp.where(keep & causal(c), 0.0, MASK_BIAS)

    flash = _Flash(*flash_refs, t)
    qs = [_pad_q(aq_ref[h]) for h in range(DSA_HEADS)]
    flash.reset()

    n = DSA_HEADS

    def att_pair(cc):
        c0 = 2 * cc
        k0 = pl.multiple_of(c0 * t, 2 * t)
        kv = akv_ref[pl.ds(k0, 2 * t), :]
        bias = jnp.concatenate([chunk_bias(c0), chunk_bias(c0 + 1)], axis=0)
        return [kv] * n, [[akvt_ref[c0], akvt_ref[c0 + 1]]] * n, [bias] * n

    flash.run(qs, n_pairs, att_pair)
    _store_heads(o_ref, [flash.result(h) for h in range(DSA_HEADS)])


def _dsa_attention(iq, iwt, ik, aq, akv, akvt):
    B, _, S, _ = aq.shape
    t = ATT_T
    nt = S // t
    k_top = min(DSA_TOPK, S // 4)
    assert nt % 2 == 0
    in_specs = [
        pl.BlockSpec((None, 8, t, 32), lambda b, i: (b, 0, i, 0)),
        pl.BlockSpec((None, 8, t), lambda b, i: (b, 0, i)),
        pl.BlockSpec((None, S, 32), lambda b, i: (b, 0, 0)),
        pl.BlockSpec((None, 8, t, 64), lambda b, i: (b, 0, i, 0)),
        pl.BlockSpec((None, S, 128), lambda b, i: (b, 0, 0)),
        pl.BlockSpec((None, nt, PV_ROWS, t), lambda b, i: (b, 0, 0, 0)),
    ]
    return pl.pallas_call(
        functools.partial(_dsa_kernel, k_top=k_top, index_bits=S.bit_length()),
        grid=(B, nt), in_specs=in_specs,
        out_specs=pl.BlockSpec((None, t, 512), lambda b, i: (b, i, 0)),
        out_shape=jax.ShapeDtypeStruct((B, S, 512), BF16),
        scratch_shapes=[pltpu.VMEM((nt, t, t), I32), pltpu.VMEM((nt, t, t), I16),
                        pltpu.VMEM((1, t), I32)] + _flash_scratch(DSA_HEADS, t, 2 * t),
        compiler_params=_params(("parallel", "parallel")), name="dsa_attention",
    )(iq, iwt, ik, aq, akv, akvt)


MOBA_HPS = 8


def _moba_kernel(q_ref, kv_ref, kvt_ref, km_ref, o_ref, *flash_refs, n_top):
    t = ATT_T
    own = pl.program_id(2)
    causal = _causal_bias(t)
    flash = _Flash(*flash_refs, t)
    qs = []
    for hh in range(MOBA_HPS):
        q = _pad_q(q_ref[hh])
        km_hi, km_lo = _split_bf16(km_ref[hh])
        gate = _dot_nt(km_hi, q) + _dot_nt(km_lo, q)
        keep = _rank_select_t(gate, own, n_top)
        keep = jnp.where(_row_iota(keep.shape) == own, 1.0, keep)
        qs.append(_bias_lanes(q, (keep - 1.0) * (-MASK_BIAS)))
    flash.reset()

    def operands(cc):
        n0 = 2 * cc
        k0 = pl.multiple_of(n0 * t, 2 * t)
        heads = range(MOBA_HPS)
        return (n0, [kv_ref[hh, pl.ds(k0, 2 * t), :] for hh in heads],
                [[kvt_ref[hh, n0], kvt_ref[hh, n0 + 1]] for hh in heads])

    def past_pair(cc):
        _, kvs, kvts = operands(cc)
        return kvs, kvts, [None] * MOBA_HPS

    flash.run(qs, own // 2, past_pair)
    heads = range(MOBA_HPS)

    @pl.when(own % 2 == 0)
    def _():
        k0 = pl.multiple_of(own * t, t)
        flash.update(qs, [kv_ref[hh, pl.ds(k0, t), :] for hh in heads],
                     [[kvt_ref[hh, own]] for hh in heads], [causal] * MOBA_HPS)

    @pl.when(own % 2 == 1)
    def _():
        _, kvs, kvts = operands(own // 2)
        bias = jnp.concatenate([jnp.zeros((t, t), F32), causal], axis=0)
        flash.update(qs, kvs, kvts, [bias] * MOBA_HPS)

    _store_heads(o_ref, [flash.result(hh) for hh in range(MOBA_HPS)])


def _moba_attention(bq, bkv, bkvt, kmean):
    B, H, S, _ = bq.shape
    t = ATT_T
    hps = MOBA_HPS
    n_blk = S // MOBA_BLOCK
    assert t == MOBA_BLOCK and n_blk % 2 == 0 and H % hps == 0
    n_top = max(1, min(MOBA_TOPK, n_blk - 1))
    in_specs = [
        pl.BlockSpec((None, hps, t, 64), lambda b, h, i: (b, h, i, 0)),
        pl.BlockSpec((None, hps, S, 128), lambda b, h, i: (b, h, 0, 0)),
        pl.BlockSpec((None, hps, n_blk, PV_ROWS, t), lambda b, h, i: (b, h, 0, 0, 0)),
        pl.BlockSpec((None, hps, n_blk, 128), lambda b, h, i: (b, h, 0, 0)),
    ]
    return pl.pallas_call(
        functools.partial(_moba_kernel, n_top=n_top), grid=(B, H // hps, S // t), in_specs=in_specs,
        out_specs=pl.BlockSpec((None, t, hps * 64), lambda b, h, i: (b, i, h)),
        out_shape=jax.ShapeDtypeStruct((B, S, H * 64), BF16),
        scratch_shapes=_flash_scratch(hps, t, 2 * t),
        compiler_params=_params(("parallel", "parallel", "parallel")), name="moba_attention",
    )(bq, bkv, bkvt, kmean)


def _lane_group_norm(y, gain, width):
    outs = []
    for j in range(y.shape[1] // width):
        yc = y[:, j * width:(j + 1) * width]
        outs.append(_rms_rows(yc, gain))
    return jnp.concatenate(outs, axis=1)


def _mem_kv_kernel(m_ref, g_ref, w_ref, gk_ref, o_ref):
    mn = _rms_rows(m_ref[...], g_ref[...]).astype(BF16)
    y = _dot(mn, w_ref[...])
    hw = MEM_HEADS * MEM_HEAD_DIM
    k = _lane_group_norm(y[:, :hw], gk_ref[...], MEM_HEAD_DIM)
    o_ref[...] = jnp.concatenate([k, y[:, hw:]], axis=1).astype(BF16)


def _mem_kv(mem, g, w, gk):
    B, M, _ = mem.shape
    n = w.shape[1]
    return pl.pallas_call(
        _mem_kv_kernel, grid=(B,),
        in_specs=[pl.BlockSpec((None, M, D_MODEL), lambda b: (b, 0, 0)),
                  pl.BlockSpec(g.shape, lambda b: (0, 0)),
                  pl.BlockSpec(w.shape, lambda b: (0, 0)),
                  pl.BlockSpec(gk.shape, lambda b: (0, 0))],
        out_specs=pl.BlockSpec((None, M, n), lambda b: (b, 0, 0)),
        out_shape=jax.ShapeDtypeStruct((B, M, n), BF16),
        compiler_params=_params(("parallel",)), name="mem_kv",
    )(mem, g, w, gk)


def _mem_attend(x, g_ref, wq_ref, gq_ref, kv_ref, wo_ref):
    xn = _rms_rows(x, g_ref[...]).astype(BF16)
    q = _lane_group_norm(_dot(xn, wq_ref[...]), gq_ref[...], MEM_HEAD_DIM).astype(BF16)
    hw = MEM_HEADS * MEM_HEAD_DIM
    scale = MEM_HEAD_DIM ** -0.5
    outs = []
    for h in range(MEM_HEADS):
        cols = slice(h * MEM_HEAD_DIM, (h + 1) * MEM_HEAD_DIM)
        k = kv_ref[:, cols]
        v = kv_ref[:, hw + h * MEM_HEAD_DIM:hw + (h + 1) * MEM_HEAD_DIM]
        s = _dot_nt(q[:, cols], k) * scale
        p = jnp.exp(s - jnp.max(s, axis=-1, keepdims=True))
        p = p / jnp.sum(p, axis=-1, keepdims=True)
        outs.append(_dot(p.astype(BF16), v))
    o = jnp.concatenate(outs, axis=1).astype(BF16)
    return x + _dot(o, wo_ref[...])


def _post_mixer_kernel(*refs, n_in):
    a_refs = refs[:n_in]
    w_refs = refs[n_in:2 * n_in]
    x_ref, gm_ref, wq_ref, gq_ref, kv_ref, wo_ref, gf_ref, wg_ref, wu_ref, wd_ref, o_ref = refs[2 * n_in:]
    x = x_ref[...]
    for a_ref, w_ref in zip(a_refs, w_refs):
        x = x + _dot(a_ref[...], w_ref[...])
    x = _mem_attend(x, gm_ref, wq_ref, gq_ref, kv_ref, wo_ref)
    xn = _rms_rows(x, gf_ref[...]).astype(BF16)
    gate = _dot(xn, wg_ref[...])
    up = _dot(xn, wu_ref[...])
    act = (gate * jax.nn.sigmoid(gate) * up).astype(BF16)
    o_ref[...] = x + _dot(act, wd_ref[...])


def _post_mixer(parts, weights, x2, g_mem, wq, gq, kv, wo, g_ffn, wg, wu, wd, S, tm=512):
    T = x2.shape[0]
    nt = S // tm
    n_in = len(parts)
    M, n = kv.shape[1], kv.shape[2]

    def const(a):
        return pl.BlockSpec(a.shape, lambda i: (0,) * a.ndim, pipeline_mode=pl.Buffered(1))

    in_specs = ([pl.BlockSpec((tm, p.shape[1]), lambda i: (i, 0)) for p in parts]
                + [const(w) for w in weights]
                + [pl.BlockSpec((tm, D_MODEL), lambda i: (i, 0)),
                   const(g_mem), const(wq), const(gq),
                   pl.BlockSpec((None, M, n), lambda i: (i // nt, 0, 0)),
                   const(wo), const(g_ffn), const(wg), const(wu), const(wd)])
    return pl.pallas_call(
        functools.partial(_post_mixer_kernel, n_in=n_in), grid=(T // tm,), in_specs=in_specs,
        out_specs=pl.BlockSpec((tm, D_MODEL), lambda i: (i, 0)),
        out_shape=jax.ShapeDtypeStruct((T, D_MODEL), F32),
        compiler_params=_params(("parallel",)), name="post_mixer",
    )(*parts, *weights, x2, g_mem, wq, gq, kv, wo, g_ffn, wg, wu, wd)


def _nsa_prep_kernel(x_ref, trig_ref, gmix_ref, w_ref, nq_ref, nk_ref,
                     gq_ref, gks_ref, gkw_ref,
                     qc_ref, qr_ref, kvs_ref, kvst_ref, kvw_ref, kvwt_ref, kc_ref, vc_ref, gtt_ref,
                     stage_ref, *, n_tiles):
    xn = _rms_rows(x_ref[...], gmix_ref[...]).astype(BF16)
    c64, s64 = trig_ref[:, :LANES], trig_ref[:, LANES:]
    lane = _lane_iota(c64.shape)
    lo64 = (lane % 64) < 8
    first64 = lane < 64
    c64k = jnp.where(first64, c64, 1.0)
    s64k = jnp.where(first64, s64, 0.0)
    nq = nq_ref[...]
    nk = nk_ref[...]

    col = _ColumnProjector(xn, w_ref)

    for j in range(8):
        qn = _head_norm(col(j), nq, gq_ref[...])
        qr = _rope(qn, c64, s64, lo64, 8)
        qc_ref[2 * j] = qn[:, :64].astype(BF16)
        qc_ref[2 * j + 1] = qn[:, 64:].astype(BF16)
        qr_ref[2 * j] = qr[:, :64].astype(BF16)
        qr_ref[2 * j + 1] = qr[:, 64:].astype(BF16)
    tile = pl.program_id(0) % n_tiles
    sel_blk = tile * (ATT_T // NSA_SEL_LEN) + lax.shift_right_logical(
        _row_iota(c64.shape), NSA_SEL_LEN.bit_length() - 1)
    blk_onehot = jnp.where(lane == HEAD_DIM + sel_blk, 1.0, 0.0)
    for g in range(NSA_GROUPS):
        kv = _kv_column(col(8 + g), nk, gks_ref[...], c64k, s64k, lo64, first64)
        kvs_ref[g] = jnp.where(first64, kv, blk_onehot).astype(BF16)
        kvst_ref[g] = _pv_operand(kv)
        kv = _kv_column(col(12 + g), nk, gkw_ref[...], c64k, s64k, lo64, first64)
        kvw_ref[g] = kv.astype(BF16)
        kvwt_ref[g] = _pv_operand(kv)
    stride = NSA_CMP_STRIDE
    rows = stage_ref.shape[0] // stride
    for out_ref, first in ((kc_ref, 16), (vc_ref, 18)):
        for c in range(2):
            stage_ref[...] = col(first + c)
            for u in range(0, stride, 2):
                pair = [stage_ref[pl.ds(u + v, rows, stride=stride), :] for v in range(2)]
                for h in range(2):
                    halves = [p[:, h * HEAD_DIM:(h + 1) * HEAD_DIM] for p in pair]
                    out_ref[2 * c + h, :, u * HEAD_DIM:(u + 2) * HEAD_DIM] = jnp.concatenate(halves, axis=1)
    gates_t = jax.nn.sigmoid(col(20)).T
    for g in range(NSA_GROUPS):
        gtt_ref[g] = gates_t[NSA_GATE_ROWS * g:NSA_GATE_ROWS * (g + 1), :]


def _nsa_prep(x2, trig, gmix, w, tabs, B, S):
    T = x2.shape[0]
    tm = ATT_T
    nt = S // tm
    nq, nk, gq, gks, gkw = tabs

    def full(a):
        return pl.BlockSpec(a.shape, lambda i: (0,) * a.ndim)

    def hm(width, heads):
        return pl.BlockSpec((None, heads, tm, width), lambda i: (i // nt, 0, i % nt, 0))

    def hmt(heads):
        return pl.BlockSpec((None, heads, None, PV_ROWS, tm), lambda i: (i // nt, 0, i % nt, 0, 0))

    def tokm(width):
        return pl.BlockSpec((None, tm, width), lambda i: (i // nt, i % nt, 0))

    out_shape = (
        jax.ShapeDtypeStruct((B, 16, S, 64), BF16),
        jax.ShapeDtypeStruct((B, 16, S, 64), BF16),
        jax.ShapeDtypeStruct((B, 4, S, 128), BF16),
        jax.ShapeDtypeStruct((B, 4, nt, PV_ROWS, tm), BF16),
        jax.ShapeDtypeStruct((B, 4, S, 128), BF16),
        jax.ShapeDtypeStruct((B, 4, nt, PV_ROWS, tm), BF16),
        jax.ShapeDtypeStruct((B, 4, S // 16, 1024), F32),
        jax.ShapeDtypeStruct((B, 4, S // 16, 1024), F32),
        jax.ShapeDtypeStruct((B, 4, NSA_GATE_ROWS, S), F32),
    )
    rows16 = pl.BlockSpec((None, 4, tm // 16, 1024), lambda i: (i // nt, 0, i % nt, 0))
    out_specs = (hm(64, 16), hm(64, 16), hm(128, 4), hmt(4), hm(128, 4), hmt(4), rows16, rows16,
                 pl.BlockSpec((None, 4, NSA_GATE_ROWS, tm), lambda i: (i // nt, 0, 0, i % nt)))
    in_specs = [pl.BlockSpec((tm, D_MODEL), lambda i: (i, 0)),
                pl.BlockSpec((tm, 2 * LANES), lambda i: (i, 0)),
                full(gmix), full(w), full(nq), full(nk), full(gq), full(gks), full(gkw)]
    return pl.pallas_call(
        functools.partial(_nsa_prep_kernel, n_tiles=nt), grid=(T // tm,), in_specs=in_specs, out_specs=out_specs,
        out_shape=out_shape, scratch_shapes=[pltpu.VMEM((tm, LANES), F32)],
        compiler_params=_params(("parallel",)), name="nsa_prep",
    )(x2, trig, gmix, w, nq, nk, gq, gks, gkw)


def _compress_one(x16, pa, pb, w1a, w1b, w2):
    n16 = x16.shape[0]
    h_a = _dot((x16 + pa).astype(BF16), w1a)
    h_b = _dot((x16 + pb).astype(BF16), w1b)
    pre = h_a + pltpu.roll(h_b, n16 - 1, 0)
    act = pre * jax.nn.sigmoid(pre)
    return _dot(act.astype(BF16), w2)


def _compress_kernel(xk_ref, xv_ref, pk_ref, pv_ref, w1k_ref, w1v_ref, w2k_ref, w2v_ref, gk_ref,
                     o_ref, ot_ref):
    half = w1k_ref.shape[0] // 2
    k = _compress_one(xk_ref[...], pk_ref[0:1, :], pk_ref[1:2, :],
                      w1k_ref[:half, :], w1k_ref[half:, :], w2k_ref[...])
    k = _rms_rows(k, gk_ref[...])
    v = _compress_one(xv_ref[...], pv_ref[0:1, :], pv_ref[1:2, :],
                      w1v_ref[:half, :], w1v_ref[half:, :], w2v_ref[...])
    kv = jnp.concatenate([k, v], axis=1)
    o_ref[...] = kv.astype(BF16)
    ot_ref[...] = _pv_operand(kv)


def _compress(xk16, xv16, pk, pv, w1k, w1v, w2k, w2v, gk):
    B, G, n16, width = xk16.shape

    def full(a):
        return pl.BlockSpec(a.shape, lambda b, g: (0,) * a.ndim)

    xspec = pl.BlockSpec((None, None, n16, width), lambda b, g: (b, g, 0, 0))
    return pl.pallas_call(
        _compress_kernel, grid=(B, G),
        in_specs=[xspec, xspec, full(pk), full(pv), full(w1k), full(w1v), full(w2k), full(w2v), full(gk)],
        out_specs=(pl.BlockSpec((None, None, n16, 128), lambda b, g: (b, g, 0, 0)),
                   pl.BlockSpec((None, None, PV_ROWS, n16), lambda b, g: (b, g, 0, 0))),
        out_shape=(jax.ShapeDtypeStruct((B, G, n16, 128), BF16),
                   jax.ShapeDtypeStruct((B, G, PV_ROWS, n16), BF16)),
        compiler_params=_params(("parallel", "parallel")), name="nsa_compress",
    )(xk16, xv16, pk, pv, w1k, w1v, w2k, w2v, gk)


NSA_GPS = 4


def _nsa_tile_masks(i, n16, n_sel, n_cmp, n_top):
    t = ATT_T
    t0 = i * t
    n_id = _row_iota((n16, t))
    q_id = t0 + _lane_iota((n16, t))
    cmp_visible = (n_id < n_cmp) & (n_id * NSA_CMP_STRIDE + (NSA_CMP_LEN - 1) <= q_id)
    b_id = _row_iota((n_sel, n16)) * NSA_SEL_LEN
    r_id = _lane_iota((n_sel, n16)) * NSA_CMP_STRIDE
    cover_t = ((r_id < b_id + NSA_SEL_LEN) & (r_id + NSA_CMP_LEN > b_id)
               & (_lane_iota((n_sel, n16)) < n_cmp))
    n_wc = NSA_WINDOW // t + 1
    cw = jnp.maximum(i - (n_wc - 1), 0)
    dist = (i - cw) * t + _lane_iota((n_wc * t, t)) - _row_iota((n_wc * t, t))
    blk = _row_iota((n_sel, t))
    cur = lax.shift_right_logical(t0 + _lane_iota((n_sel, t)), NSA_SEL_LEN.bit_length() - 1)
    return dict(
        bias_c=jnp.where(cmp_visible, 0.0, MASK_BIAS),
        cover_t=jnp.where(cover_t, 1.0, 0.0).astype(BF16),
        n_wc=n_wc, cw=cw, bias_w=jnp.where((dist >= 0) & (dist < NSA_WINDOW), 0.0, MASK_BIAS),
        forced=(blk == 0) | (blk == cur) | (blk == cur - 1), visible_blk=blk <= cur,
        n_wanted=jnp.minimum(cur[0:1, :] + 1, n_top).astype(F32))


def _nsa_front(qc_ref, qr_ref, kvc_ref, kvct_ref, kvw_ref, kvwt_ref, gtt_ref, part_ref, masks):
    t = ATT_T
    HG = NSA_HEADS // NSA_GROUPS
    n16 = kvc_ref.shape[0]

    p_sum = jnp.zeros((n16, t), F32)
    o_c = []
    probs, inv_ls, outs = _softmax_direct([_pad_q(qc_ref[j]) for j in range(HG)], kvc_ref[...],
                                          [kvct_ref[...]], masks["bias_c"])
    for j in range(HG):
        p_sum = p_sum + probs[j] * inv_ls[j]
        o_c.append(outs[j] * inv_ls[j])

    p_hi, p_lo = _split_bf16(p_sum)
    imp = _dot(masks["cover_t"], p_hi) + _dot(masks["cover_t"], p_lo)

    qs = [_pad_q(qr_ref[j]) for j in range(HG)]
    n_wc, cw = masks["n_wc"], masks["cw"]
    kw0 = pl.multiple_of(cw * t, t)
    _, inv_lw, out_w = _softmax_direct(qs, kvw_ref[pl.ds(kw0, n_wc * t), :],
                                       [kvwt_ref[cw + u] for u in range(n_wc)], masks["bias_w"])
    for j in range(HG):
        part_ref[:, j * t:(j + 1) * t] = (gtt_ref[3 * j:3 * j + 1, :] * o_c[j]
                                          + gtt_ref[3 * j + 2:3 * j + 3, :] * (out_w[j] * inv_lw[j]))

    imp = jnp.where(masks["forced"], NSA_FORCE, imp)
    return qs, jnp.where(masks["visible_blk"], imp, NEG_INF)


RANK_SEGMENT = 16


def _count_larger(imp, acc, rows):
    for m in rows:
        acc = acc + jnp.where(imp[m:m + 1, :] > imp, 1.0, 0.0)
    return acc


def _nsa_kernel(qc_ref, qr_ref, kvc_ref, kvct_ref, kvs_ref, kvst_ref, kvw_ref, kvwt_ref, gtt_ref,
                o_ref, sel_ref, part_ref, *flash_refs, n_cmp, n_top):
    t = ATT_T
    HG = NSA_HEADS // NSA_GROUPS
    n_slots = NSA_GPS * HG
    i = pl.program_id(2)

    masks = _nsa_tile_masks(i, kvc_ref.shape[1], sel_ref.shape[1], n_cmp, n_top)
    n_sel = sel_ref.shape[1]
    qs, imps = [], []
    for g in range(NSA_GPS):
        heads_g = pl.ds(g * HG, HG)
        q_g, imp_g = _nsa_front(
            qc_ref.at[heads_g], qr_ref.at[heads_g], kvc_ref.at[g], kvct_ref.at[g], kvw_ref.at[g],
            kvwt_ref.at[g], gtt_ref.at[g], part_ref.at[:, pl.ds(g * HG * t, HG * t)], masks)
        qs += q_g
        imps.append(imp_g)
        sel_ref[g] = _count_larger(imp_g, jnp.zeros((n_sel, t), F32), range(RANK_SEGMENT))

    for k in range(1, n_sel // RANK_SEGMENT):
        @pl.when((i + 1) * (t // NSA_SEL_LEN) > k * RANK_SEGMENT)
        def _():
            for g in range(NSA_GPS):
                sel_ref[g] = _count_larger(imps[g], sel_ref[g],
                                           range(k * RANK_SEGMENT, (k + 1) * RANK_SEGMENT))

    miss = None
    for g in range(NSA_GPS):
        sel_fast = sel_ref[g] < n_top
        n_picked = jnp.sum(jnp.where(sel_fast & masks["visible_blk"], 1.0, 0.0), axis=0, keepdims=True)
        sel_ref[g] = jnp.where(sel_fast, 0.0, MASK_BIAS)
        miss_g = jnp.abs(n_picked - masks["n_wanted"])
        miss = miss_g if miss is None else jnp.maximum(miss, miss_g)

    @pl.when(jnp.max(miss) > 0.0)
    def _():
        for g in range(NSA_GPS):
            sel_ref[g] = (_rank_select_t(imps[g], None, n_top) - 1.0) * (-MASK_BIAS)

    qs_sel = [_bias_lanes(qs[s], sel_ref[s // HG]) for s in range(n_slots)]
    flash = _Flash(*flash_refs, t)
    flash.reset()

    def sel_operands(c):
        k0 = pl.multiple_of(c * t, t)
        kvs = [kvs_ref[s // HG, pl.ds(k0, t), :] for s in range(n_slots)]
        kvts = [[kvst_ref[s // HG, c]] for s in range(n_slots)]
        return kvs, kvts

    def past_chunk(c):
        kvs, kvts = sel_operands(c)
        return kvs, kvts, [None] * n_slots

    flash.run(qs_sel, i, past_chunk)
    kvs, kvts = sel_operands(i)
    flash.update(qs_sel, kvs, kvts, [_causal_bias(t)] * n_slots)

    heads = []
    for s in range(n_slots):
        g, j = divmod(s, HG)
        heads.append(part_ref[:, s * t:(s + 1) * t] + gtt_ref[g, 3 * j + 1:3 * j + 2, :] * flash.result(s))
    _store_heads(o_ref, heads)


def _nsa_attention(qc, qr, kvc, kvct, kvs, kvst, kvw, kvwt, gates_t):
    B, H, S, _ = qc.shape
    G = NSA_GROUPS
    HG = H // G
    t = ATT_T
    nt = S // t
    n16 = kvc.shape[2]
    n_cmp = (S - NSA_CMP_LEN) // NSA_CMP_STRIDE + 1
    n_sel = S // NSA_SEL_LEN
    n_top = min(NSA_SEL_TOPK, n_sel)
    gps = NSA_GPS
    assert G % gps == 0 and nt % 2 == 0 and n_sel <= HEAD_DIM and S >= (NSA_WINDOW // t + 1) * t
    qspec = pl.BlockSpec((None, gps * HG, t, 64), lambda b, g, i: (b, g, i, 0))
    once = pl.Buffered(1)
    kvspec = pl.BlockSpec((None, gps, S, 128), lambda b, g, i: (b, g, 0, 0), pipeline_mode=once)
    kvtspec = pl.BlockSpec((None, gps, nt, PV_ROWS, t), lambda b, g, i: (b, g, 0, 0, 0), pipeline_mode=once)
    in_specs = [qspec, qspec,
                pl.BlockSpec((None, gps, n16, 128), lambda b, g, i: (b, g, 0, 0)),
                pl.BlockSpec((None, gps, PV_ROWS, n16), lambda b, g, i: (b, g, 0, 0)),
                kvspec, kvtspec, kvspec, kvtspec,
                pl.BlockSpec((None, gps, NSA_GATE_ROWS, t), lambda b, g, i: (b, g, 0, i))]
    return pl.pallas_call(
        functools.partial(_nsa_kernel, n_cmp=n_cmp, n_top=n_top), grid=(B, G // gps, nt), in_specs=in_specs,
        out_specs=pl.BlockSpec((None, t, gps * HG * 64), lambda b, g, i: (b, i, g)),
        out_shape=jax.ShapeDtypeStruct((B, S, H * 64), BF16),
        scratch_shapes=[pltpu.VMEM((gps, n_sel, t), F32), pltpu.VMEM((PV_ROWS, gps * HG * t), F32)]
                       + _flash_scratch(gps * HG, t, t),
        compiler_params=_params(("parallel", "parallel", "parallel")), name="nsa_attention",
    )(qc, qr, kvc, kvct, kvs, kvst, kvw, kvwt, gates_t)


def _rope_freq_row(period, rot):
    half = rot // 2
    inv_freq = ROPE_THETA ** (-(jnp.arange(half, dtype=F32) * 2.0 / rot))
    lane = jnp.arange(LANES) % period
    f = jnp.where(lane < rot, inv_freq[lane % half], 0.0)
    return f.reshape(1, LANES).astype(F32)


def _norm_matrices():
    r = jnp.arange(LANES)
    same = (r[:, None] // 64) == (r[None, :] // 64)
    nq = jnp.where(same, 1.0 / 64, 0.0).astype(BF16)
    nk = jnp.where(same & (r[:, None] < 64), 1.0 / 64, 0.0).astype(BF16)
    return nq, nk


def _q_gain(g):
    return (jnp.tile(g.astype(F32), 2) * Q_SCALE).reshape(1, LANES)


def _k_gain(g):
    return jnp.concatenate([g.astype(F32), jnp.ones((64,), F32)]).reshape(1, LANES)


def _interleave_kv(wk, wv, n_heads):
    d = wk.shape[0]
    wk = wk.reshape(d, n_heads, 64)
    wv = wv.reshape(d, n_heads, 64)
    return jnp.concatenate([wk, wv], axis=2).reshape(d, n_heads * 128)


def _split_cols(w, sizes):
    out, start = [], 0
    for n in sizes:
        out.append(w[:, start:start + n])
        start += n
    return out


def _mixer_layer0(x2, trig, B, S, gmix, w_in, w_out, a_q_norm, a_k_norm, b_q_norm, b_k_norm):
    sizes = (512, 64, 64, 256, 32, 8, 512, 512, 512)
    waq, wak, wav, wiq, wik, wiw, wbq, wbk, wbv = _split_cols(w_in, sizes)
    pad = jnp.zeros((D_MODEL, LANES - 40), w_in.dtype)
    w = jnp.concatenate([waq, wbq, _interleave_kv(wbk, wbv, 8), wak, wav, wiq, wik, wiw, pad],
                        axis=1).astype(BF16)
    nq, nk = _norm_matrices()
    tabs = (nq, nk, _q_gain(a_q_norm), _q_gain(b_q_norm), _k_gain(a_k_norm), _k_gain(b_k_norm))
    aq, bq, bkv, bkvt, akv, akvt, iq, ik, iwt, km = _ab_prep(x2, trig, gmix, w, tabs, B, S)
    n_blk = S // MOBA_BLOCK
    kmean = km.reshape(B, n_blk, 8, 128).transpose(0, 2, 1, 3)
    o_a = _dsa_attention(iq, iwt, ik, aq, akv, akvt).reshape(B * S, 512)
    o_b = _moba_attention(bq, bkv, bkvt, kmean).reshape(B * S, 512)
    w_out = w_out.astype(BF16)
    return [o_a, o_b], [w_out[:512], w_out[512:]]


def _mixer_layer1(x2, trig, B, S, gmix, w_in, w_out, q_norm, kcmp_norm, ksel_norm, kwin_norm,
                  pos_k, pos_v, w1_k, w2_k, w1_v, w2_v):
    G = NSA_GROUPS
    sizes = (1024,) + (256,) * 6 + (48,)
    wq, wkc, wvc, wks, wvs, wkw, wvw, wgt = _split_cols(w_in, sizes)
    pad = jnp.zeros((D_MODEL, LANES - 48), w_in.dtype)
    w = jnp.concatenate([wq, _interleave_kv(wks, wvs, G), _interleave_kv(wkw, wvw, G),
                         wkc, wvc, wgt, pad], axis=1).astype(BF16)
    nq, nk = _norm_matrices()
    tabs = (nq, nk, _q_gain(q_norm), _k_gain(ksel_norm), _k_gain(kwin_norm))
    qc, qr, kvs, kvst, kvw, kvwt, kc16, vc16, gates_t = _nsa_prep(x2, trig, gmix, w, tabs, B, S)

    def pos_rows(p):
        return p.astype(F32).reshape(2, NSA_CMP_STRIDE * HEAD_DIM)

    kvc, kvct = _compress(kc16, vc16, pos_rows(pos_k), pos_rows(pos_v),
                          w1_k.astype(BF16), w1_v.astype(BF16), w2_k.astype(BF16), w2_v.astype(BF16),
                          kcmp_norm.astype(F32).reshape(1, HEAD_DIM))
    o = _nsa_attention(qc, qr, kvc, kvct, kvs, kvst, kvw, kvwt, gates_t)
    return [o.reshape(B * S, NSA_HEADS * HEAD_DIM)], [w_out.astype(BF16)]


def _finish_layer(parts, weights, x2, mem, S, g_mem, g_src, w_q, w_kv, w_o, q_norm, k_norm,
                  g_ffn, ffn_w_in, ffn_w_out):
    row = lambda v: v.astype(F32).reshape(1, -1)
    kv = _mem_kv(mem, row(g_src), w_kv.astype(BF16), row(k_norm))
    wg = ffn_w_in[:, :D_FF].astype(BF16)
    wu = ffn_w_in[:, D_FF:].astype(BF16)
    return _post_mixer(parts, weights, x2, row(g_mem), w_q.astype(BF16), row(q_norm), kv, w_o.astype(BF16),
                       row(g_ffn), wg, wu, ffn_w_out.astype(BF16), S)


def kernel(x, mem, positions, norm_mix, norm_mem, norm_mem_src, norm_ffn, ab_w_in, ab_w_out, dsa_q_norm, dsa_k_norm, moba_q_norm, moba_k_norm, nsa_w_in, nsa_w_out, nsa_q_norm, nsa_kcmp_norm, nsa_ksel_norm, nsa_kwin_norm, nsa_cmp_pos_k, nsa_cmp_pos_v, nsa_cmp_w1_k, nsa_cmp_w2_k, nsa_cmp_w1_v, nsa_cmp_w2_v, mem_w_q, mem_w_kv, mem_w_o, mem_q_norm, mem_k_norm, ffn_w_in, ffn_w_out):
    B, S, D = x.shape
    depth = norm_mix.shape[0]
    x2 = x.reshape(B * S, D)
    trig = _rope_trig(positions.astype(F32).reshape(B * S, 1), _rope_freq_row(64, 16), _rope_freq_row(32, 8))
    row = lambda v: v.astype(F32).reshape(1, -1)
    for i in range(depth):
        j = i // 2
        if i % 2 == 0:
            parts, weights = _mixer_layer0(x2, trig, B, S, row(norm_mix[i]), ab_w_in[j], ab_w_out[j],
                               dsa_q_norm[j], dsa_k_norm[j], moba_q_norm[j], moba_k_norm[j])
        else:
            parts, weights = _mixer_layer1(x2, trig, B, S, row(norm_mix[i]), nsa_w_in[j], nsa_w_out[j],
                               nsa_q_norm[j], nsa_kcmp_norm[j], nsa_ksel_norm[j], nsa_kwin_norm[j],
                               nsa_cmp_pos_k[j], nsa_cmp_pos_v[j], nsa_cmp_w1_k[j], nsa_cmp_w2_k[j],
                               nsa_cmp_w1_v[j], nsa_cmp_w2_v[j])
        x2 = _finish_layer(parts, weights, x2, mem, S, norm_mem[i], norm_mem_src[i], mem_w_q[i], mem_w_kv[i],
                           mem_w_o[i], mem_q_norm[i], mem_k_norm[i], norm_ffn[i], ffn_w_in[i], ffn_w_out[i])
    return x2.reshape(B, S, D)
```

```python
import functools
import math

import jax
import jax.numpy as jnp
from jax import lax
from jax.experimental import pallas as pl
from jax.experimental.pallas import tpu as pltpu

F32 = jnp.float32
BF16 = jnp.bfloat16
I32 = jnp.int32
I16 = jnp.int16

D_MODEL = 1024
N_MEM = 256
HEAD_DIM = 64
ROPE_THETA = 500000.0
RMS_EPS = 1e-6
NEG_INF = -1e30
TINY = 1e-20

DSA_HEADS = 8
DSA_IDX_HEADS = 8
DSA_IDX_DIM = 32
DSA_TOPK = 256
MOBA_HEADS = 8
MOBA_BLOCK = 256
MOBA_TOPK = 3
NSA_HEADS = 16
NSA_GROUPS = 4
NSA_CMP_LEN = 32
NSA_CMP_STRIDE = 16
NSA_SEL_LEN = 64
NSA_SEL_TOPK = 16
NSA_WINDOW = 512
NSA_FORCE = 1e4
NSA_GATE_ROWS = 3 * (NSA_HEADS // NSA_GROUPS)
MEM_HEADS = 4
MEM_HEAD_DIM = 128
D_FF = ((8 * D_MODEL + 3 * 256 - 1) // (3 * 256)) * 256

LANES = 128
SUBLANES = 8
INT_MIN = -(2 ** 31)
VMEM_LIMIT = 60 * 1024 * 1024

PV_HEAD_ROWS = 16
PV_ROWS = PV_HEAD_ROWS + HEAD_DIM
PROJ_GROUP = 4
ATT_T = 256
MASK_BIAS = -1e30
M_FLOOR = -1e29
LOG2E = math.log2(math.e)
Q_SCALE = HEAD_DIM ** -0.5 * LOG2E

NT_DIMS = (((1,), (1,)), ((), ()))


def _dot(a, b):
    return jnp.dot(a, b, preferred_element_type=F32)


def _dot_nt(a, b):
    return lax.dot_general(a, b, NT_DIMS, preferred_element_type=F32)


def _split_bf16(a):
    hi = a.astype(BF16)
    return hi, (a - hi.astype(F32)).astype(BF16)


def _split_dot(a, b):
    hi, lo = _split_bf16(a)
    return _dot(hi, b) + _dot(lo, b)


def _rms_rows(x, gain):
    ms = jnp.mean(x * x, axis=-1, keepdims=True)
    return x * lax.rsqrt(ms + RMS_EPS) * gain


def _params(sem):
    return pltpu.CompilerParams(dimension_semantics=sem, vmem_limit_bytes=VMEM_LIMIT)


def _head_norm(y, norm_m, gain):
    ms = _split_dot(y * y, norm_m)
    return y * lax.rsqrt(ms + RMS_EPS) * gain


def _rope(y, c, s, lo_mask, half):
    sw = jnp.where(lo_mask, pltpu.roll(y, LANES - half, 1), pltpu.roll(y, half, 1))
    return y * c + sw * s


def _lane_iota(shape):
    return lax.broadcasted_iota(I32, shape, 1)


def _row_iota(shape):
    return lax.broadcasted_iota(I32, shape, 0)


def _rope_tables(pos, ftab, period, half):
    ang = pos * ftab
    lane = _lane_iota(ang.shape) % period
    c = jnp.cos(ang)
    s = jnp.sin(ang) * jnp.where(lane < half, -1.0, 1.0)
    return c, s


def _pv_operand(kv):
    head = jnp.where(_row_iota((PV_HEAD_ROWS, kv.shape[0])) == 0, 1.0, 0.0)
    return jnp.concatenate([head, kv.T[HEAD_DIM:, :]], axis=0).astype(BF16)


class _ColumnProjector:
    def __init__(self, xn, w_ref):
        self.xn, self.w_ref, self.groups = xn, w_ref, {}

    def __call__(self, j):
        g, u = divmod(j, PROJ_GROUP)
        if g not in self.groups:
            width = PROJ_GROUP * LANES
            lo = g * width
            hi = min(lo + width, self.w_ref.shape[1])
            self.groups[g] = _dot(self.xn, self.w_ref[:, lo:hi])
        return self.groups[g][:, u * LANES:(u + 1) * LANES]


def _kv_column(yc, nk, gain, c64k, s64k, lo64, first64):
    kn = jnp.where(first64, _head_norm(yc, nk, gain), yc)
    return _rope(kn, c64k, s64k, lo64, 8)


def _rope_trig_kernel(pos_ref, f64_ref, f32_ref, o_ref):
    pos = pos_ref[...]
    c64, s64 = _rope_tables(pos, f64_ref[...], 64, 8)
    c32, s32 = _rope_tables(pos, f32_ref[...], 32, 4)
    o_ref[...] = jnp.concatenate([c64, s64, c32, s32], axis=1)


def _rope_trig(pos2, f64, f32t, tm=1024):
    T = pos2.shape[0]
    return pl.pallas_call(
        _rope_trig_kernel, grid=(T // tm,),
        in_specs=[pl.BlockSpec((tm, 1), lambda i: (i, 0)),
                  pl.BlockSpec(f64.shape, lambda i: (0, 0)), pl.BlockSpec(f32t.shape, lambda i: (0, 0))],
        out_specs=pl.BlockSpec((tm, 4 * LANES), lambda i: (i, 0)),
        out_shape=jax.ShapeDtypeStruct((T, 4 * LANES), F32),
        compiler_params=_params(("parallel",)), name="rope_trig",
    )(pos2, f64, f32t)


def _ab_prep_kernel(x_ref, trig_ref, gmix_ref, w_ref, nq_ref, nk_ref,
                    gaq_ref, gbq_ref, gak_ref, gbk_ref,
                    aq_ref, bq_ref, bkv_ref, bkvt_ref, akv_ref, akvt_ref, iq_ref, ik_ref, iwt_ref, km_ref,
                    *, n_tiles):
    xn = _rms_rows(x_ref[...], gmix_ref[...]).astype(BF16)
    c64, s64, c32, s32 = [trig_ref[:, j * LANES:(j + 1) * LANES] for j in range(4)]
    lane = _lane_iota(c64.shape)
    lo64 = (lane % 64) < 8
    lo32 = (lane % 32) < 4
    first64 = lane < 64
    c64k = jnp.where(first64, c64, 1.0)
    s64k = jnp.where(first64, s64, 0.0)
    first32 = lane < 32
    c32k = jnp.where(first32, c32, 1.0)
    s32k = jnp.where(first32, s32, 0.0)
    nq = nq_ref[...]
    nk = nk_ref[...]

    col = _ColumnProjector(xn, w_ref)

    for j in range(4):
        q = _rope(_head_norm(col(j), nq, gaq_ref[...]), c64, s64, lo64, 8)
        aq_ref[2 * j] = q[:, :64].astype(BF16)
        aq_ref[2 * j + 1] = q[:, 64:].astype(BF16)
    for j in range(4):
        q = _rope(_head_norm(col(4 + j), nq, gbq_ref[...]), c64, s64, lo64, 8)
        bq_ref[2 * j] = q[:, :64].astype(BF16)
        bq_ref[2 * j + 1] = q[:, 64:].astype(BF16)
    blk_onehot = jnp.where(lane == HEAD_DIM + pl.program_id(0) % n_tiles, 1.0, 0.0)
    for h in range(8):
        kv = _kv_column(col(8 + h), nk, gbk_ref[...], c64k, s64k, lo64, first64)
        bkv_ref[h] = jnp.where(first64, kv, blk_onehot).astype(BF16)
        bkvt_ref[h] = _pv_operand(kv)
        km_ref[h:h + 1, :] = jnp.mean(kv, axis=0, keepdims=True)
    kv = _kv_column(col(16), nk, gak_ref[...], c64k, s64k, lo64, first64)
    akv_ref[...] = kv.astype(BF16)
    akvt_ref[...] = _pv_operand(kv)
    for j in range(2):
        q = _rope(col(17 + j), c32, s32, lo32, 4)
        for u in range(4):
            iq_ref[4 * j + u] = q[:, 32 * u:32 * (u + 1)].astype(BF16)
    yc = col(19)
    ik_ref[...] = _rope(yc, c32k, s32k, lo32, 4)[:, :32].astype(BF16)
    iwt_ref[...] = yc.T[32:40, :]


def _ab_prep(x2, trig, gmix, w, tabs, B, S):
    T = x2.shape[0]
    tm = ATT_T
    nt = S // tm
    n_cols = w.shape[1]
    nq, nk, gaq, gbq, gak, gbk = tabs

    def full(a):
        return pl.BlockSpec(a.shape, lambda i: (0,) * a.ndim)

    def hm(width, heads=8):
        return pl.BlockSpec((None, heads, tm, width), lambda i: (i // nt, 0, i % nt, 0))

    def tokm(width):
        return pl.BlockSpec((None, tm, width), lambda i: (i // nt, i % nt, 0))

    out_shape = (
        jax.ShapeDtypeStruct((B, 8, S, 64), BF16),
        jax.ShapeDtypeStruct((B, 8, S, 64), BF16),
        jax.ShapeDtypeStruct((B, 8, S, 128), BF16),
        jax.ShapeDtypeStruct((B, 8, nt, PV_ROWS, tm), BF16),
        jax.ShapeDtypeStruct((B, S, 128), BF16),
        jax.ShapeDtypeStruct((B, nt, PV_ROWS, tm), BF16),
        jax.ShapeDtypeStruct((B, 8, S, 32), BF16),
        jax.ShapeDtypeStruct((B, S, 32), BF16),
        jax.ShapeDtypeStruct((B, 8, S), F32),
        jax.ShapeDtypeStruct((T // tm, 8, 128), F32),
    )
    out_specs = (hm(64), hm(64), hm(128),
                 pl.BlockSpec((None, 8, None, PV_ROWS, tm), lambda i: (i // nt, 0, i % nt, 0, 0)),
                 tokm(128),
                 pl.BlockSpec((None, None, PV_ROWS, tm), lambda i: (i // nt, i % nt, 0, 0)),
                 hm(32), tokm(32),
                 pl.BlockSpec((None, 8, tm), lambda i: (i // nt, 0, i % nt)),
                 pl.BlockSpec((None, 8, 128), lambda i: (i, 0, 0)))
    in_specs = [pl.BlockSpec((tm, D_MODEL), lambda i: (i, 0)),
                pl.BlockSpec((tm, 4 * LANES), lambda i: (i, 0)),
                full(gmix), pl.BlockSpec((D_MODEL, n_cols), lambda i: (0, 0)),
                full(nq), full(nk), full(gaq), full(gbq), full(gak), full(gbk)]
    return pl.pallas_call(
        functools.partial(_ab_prep_kernel, n_tiles=nt), grid=(T // tm,), in_specs=in_specs, out_specs=out_specs,
        out_shape=out_shape, compiler_params=_params(("parallel",)), name="ab_prep",
    )(x2, trig, gmix, w, nq, nk, gaq, gbq, gak, gbk)


def _pad_q(q):
    return jnp.concatenate([q, jnp.zeros_like(q)], axis=1)


def _bias_lanes(q, rows):
    n, tq = rows.shape
    parts = [jnp.zeros((HEAD_DIM, tq), F32), rows]
    if n < HEAD_DIM:
        parts.append(jnp.zeros((HEAD_DIM - n, tq), F32))
    lanes = jnp.concatenate(parts, axis=0).T.astype(BF16)
    return jnp.where(_lane_iota(q.shape) < HEAD_DIM, q, lanes)


class _Flash:
    def __init__(self, m_ref, acc_ref, s_ref, cmax_ref, p_ref, tq):
        self.m_ref, self.acc_ref, self.tq = m_ref, acc_ref, tq
        self.s_ref, self.cmax_ref, self.p_ref = s_ref, cmax_ref, p_ref

    def reset(self):
        self.m_ref[...] = jnp.full(self.m_ref.shape, M_FLOOR, F32)
        self.acc_ref[...] = jnp.zeros(self.acc_ref.shape, F32)

    def _scores(self, buf, qs, kvs, biases):
        tq = self.tq
        for i in range(len(qs)):
            s = _dot_nt(kvs[i], qs[i])
            if biases[i] is not None:
                s = s + biases[i]
            self.s_ref[buf, i, :s.shape[0], :] = s
            self.cmax_ref[buf, :, i * tq:(i + 1) * tq] = jnp.max(s, axis=0, keepdims=True)

    def update(self, qs, kvs, kvts, biases):
        self._scores(0, qs, kvs, biases)
        self._finish(0, kvts)

    def run(self, qs, count, operands):
        def scores(c, buf):
            kvs, _, biases = operands(c)
            self._scores(buf, qs, kvs, biases)

        def finish(c, buf):
            self._finish(buf, operands(c)[1])

        last = jnp.maximum(count - 1, 0)
        scores(0, 0)

        def two_chunks(pp, carry):
            c = 2 * pp
            scores(c + 1, 1)
            finish(c, 0)
            scores(jnp.minimum(c + 2, last), 0)
            finish(c + 1, 1)
            return carry

        lax.fori_loop(0, count // 2, two_chunks, 0)

        @pl.when(count % 2 == 1)
        def _():
            finish(count - 1, 0)

    def _finish(self, buf, kvts):
        n = len(kvts)
        tq = self.tq
        kc = sum(kvt.shape[1] for kvt in kvts[0])
        alphas = []
        for i in range(n):
            cols = slice(i * tq, (i + 1) * tq)
            m = self.m_ref[:, cols]
            m_new = jnp.maximum(m, self.cmax_ref[buf, :, cols])
            p = jnp.exp2(self.s_ref[buf, i, :kc, :] - m_new)
            alpha = jnp.exp2(m - m_new)
            self.m_ref[:, cols] = m_new
            self.p_ref[i, :kc, :] = p.astype(BF16)
            alphas.append(alpha)
        for i in range(n):
            cols = slice(i * tq, (i + 1) * tq)
            pv, r0 = None, 0
            for kvt in kvts[i]:
                part = _dot(kvt, self.p_ref[i, r0:r0 + kvt.shape[1], :])
                pv = part if pv is None else pv + part
                r0 += kvt.shape[1]
            self.acc_ref[:, cols] = alphas[i] * self.acc_ref[:, cols] + pv

    def result(self, slot):
        cols = slice(slot * self.tq, (slot + 1) * self.tq)
        acc = self.acc_ref[:, cols]
        return acc / jnp.maximum(acc[0:1, :], TINY)


def _flash_scratch(n_slots, tq, kc):
    return [pltpu.VMEM((1, n_slots * tq), F32), pltpu.VMEM((PV_ROWS, n_slots * tq), F32),
            pltpu.VMEM((2, n_slots, kc, tq), F32), pltpu.VMEM((2, 1, n_slots * tq), F32),
            pltpu.VMEM((n_slots, kc, tq), BF16)]


def _softmax_direct(qs, kv, kvts, bias):
    scores = [_dot_nt(kv, q) for q in qs]
    probs = []
    for s in scores:
        s = s + bias
        m = jnp.maximum(jnp.max(s, axis=0, keepdims=True), M_FLOOR)
        probs.append(jnp.exp2(s - m))
    inv_ls, outs = [], []
    for p in probs:
        pb = p.astype(BF16)
        o, r0 = None, 0
        for kvt in kvts:
            part = _dot(kvt, pb[r0:r0 + kvt.shape[1]])
            o = part if o is None else o + part
            r0 += kvt.shape[1]
        outs.append(o)
        inv_ls.append(1.0 / jnp.maximum(o[0:1, :], TINY))
    return probs, inv_ls, outs


def _causal_bias(t):
    return jnp.where(_row_iota((t, t)) <= _lane_iota((t, t)), 0.0, MASK_BIAS)


def _store_heads(o_ref, heads_t):
    for u in range(len(heads_t) // 2):
        pair = jnp.concatenate([heads_t[2 * u][PV_HEAD_ROWS:, :], heads_t[2 * u + 1][PV_HEAD_ROWS:, :]], axis=0)
        o_ref[:, u * LANES:(u + 1) * LANES] = pair.T.astype(o_ref.dtype)


def _rank_select_t(v, n_valid, n_top):
    n = v.shape[0]
    row = _row_iota(v.shape)
    rank = jnp.zeros(v.shape, F32)
    for m in range(n):
        vm = v[m:m + 1, :]
        ahead = (vm > v) | ((vm == v) & (m < row))
        if n_valid is not None:
            ahead = ahead & (m < n_valid)
        rank = rank + jnp.where(ahead, 1.0, 0.0)
    sel = rank < n_top
    if n_valid is not None:
        sel = sel & (row < n_valid)
    return jnp.where(sel, 1.0, 0.0)


def _dsa_kernel(iq_ref, iwt_ref, ik_ref, aq_ref, akv_ref, akvt_ref, o_ref,
                sk_ref, half_ref, xcut_ref, *flash_refs, k_top, index_bits):
    t = ATT_T
    i = pl.program_id(1)
    n_ch = i + 1
    kio = _row_iota((t, t))
    qio = _lane_iota((t, t))

    def causal(c):
        return (c - i) * t + kio <= qio

    def score_chunk(c):
        k0 = pl.multiple_of(c * t, t)
        ikc = ik_ref[pl.ds(k0, t), :]
        sc = jnp.zeros((t, t), F32)
        for h in range(DSA_IDX_HEADS):
            logit = _dot_nt(ikc, iq_ref[h])
            sc = sc + iwt_ref[h:h + 1, :] * jnp.maximum(logit, 0.0)
        sc = jnp.where(sc == 0.0, 0.0, sc)
        bits = pltpu.bitcast(sc, I32)
        key = bits ^ ((bits >> 31) & 0x7FFFFFFF)
        key = jnp.where(causal(c), key, INT_MIN)
        sk_ref[c] = key
        half_ref[c] = (key >> 16).astype(I16)

    def score_pair(cc, carry):
        score_chunk(2 * cc)
        score_chunk(2 * cc + 1)
        return carry

    lax.fori_loop(0, (n_ch + 1) // 2, score_pair, 0)

    def count(pred):
        def body(c, acc8):
            ind = jnp.where(pred(sk_ref[c], c), 1.0, 0.0)
            return acc8 + ind.reshape(-1, SUBLANES, t).sum(axis=0)
        acc8 = lax.fori_loop(0, n_ch, body, jnp.zeros((SUBLANES, t), F32))
        return jnp.sum(acc8, axis=0, keepdims=True)

    def count_half(cand):
        rows = 2 * SUBLANES

        def body(cc, acc):
            parts = []
            for c in (2 * cc, 2 * cc + 1):
                ind = jnp.where(half_ref[c] >= cand, jnp.bfloat16(1), jnp.bfloat16(0))
                parts += [ind[rows * j:rows * (j + 1), :] for j in range(t // rows)]
            while len(parts) > 1:
                parts = [parts[2 * j] + parts[2 * j + 1] for j in range(len(parts) // 2)]
            return acc + parts[0].astype(F32)
        acc = lax.fori_loop(0, (n_ch + 1) // 2, body, jnp.zeros((rows, t), F32))
        return jnp.sum(acc, axis=0, keepdims=True)

    def half_search(n_all):
        def bit_step(b, carry):
            v, n_ge_v = carry
            cand = v + lax.shift_left(jnp.int32(1), 15 - b)
            n_ge_cand = count_half(cand.astype(I16))
            ok = n_ge_cand >= k_top
            return jnp.where(ok, cand, v), jnp.where(ok, n_ge_cand, n_ge_v)
        return lax.fori_loop(0, 16, bit_step, (jnp.full((1, t), -(2 ** 15), I32), n_all))

    thr_hi, n_ge_hi = half_search(jnp.full((1, t), t * n_ch, I32).astype(F32))

    def low_half_pair(cc, carry):
        for c in (2 * cc, 2 * cc + 1):
            key = sk_ref[c]
            hi = key >> 16
            lo = (key & 0xFFFF) - 2 ** 15
            half_ref[c] = jnp.where(hi > thr_hi, 2 ** 15 - 1,
                                    jnp.where(hi < thr_hi, -(2 ** 15), lo)).astype(I16)
        return carry

    lax.fori_loop(0, (n_ch + 1) // 2, low_half_pair, 0)
    thr_lo, n_ge = half_search(n_ge_hi)
    thr = lax.shift_left(thr_hi, 16) + (thr_lo + 2 ** 15)

    xcut_ref[...] = jnp.full((1, t), 2 ** 30, I32)

    @pl.when(jnp.max(n_ge) > k_top)
    def _():
        need = k_top - count(lambda blk, c: blk > thr)

        def x_step(b, x):
            cand = x + lax.shift_left(jnp.int32(1), index_bits - 1 - b)
            ties_below = count(lambda blk, c: (blk == thr) & (c * t + kio < cand))
            return jnp.where(ties_below <= need, cand, x)
        xcut_ref[...] = lax.fori_loop(0, index_bits, x_step, jnp.zeros((1, t), I32))

    xcut = xcut_ref[...]

    n_pairs = (n_ch + 1) // 2

    def chunk_bias(c):
        blk = sk_ref[c]
        keep = (blk > thr) | ((blk == thr) & (c * t + kio < xcut))
        return jnp.where(keep & causal(c), 0.0, MASK_BIAS)

    flash = _Flash(*flash_refs, t)
    qs = [_pad_q(aq_ref[h]) for h in range(DSA_HEADS)]
    flash.reset()

    n = DSA_HEADS

    def att_chunk(c):
        k0 = pl.multiple_of(c * t, t)
        return [akv_ref[pl.ds(k0, t), :]] * n, [[akvt_ref[c]]] * n, [chunk_bias(c)] * n

    flash.run(qs, n_ch, att_chunk)
    _store_heads(o_ref, [flash.result(h) for h in range(DSA_HEADS)])


def _dsa_attention(iq, iwt, ik, aq, akv, akvt):
    B, _, S, _ = aq.shape
    t = ATT_T
    nt = S // t
    k_top = min(DSA_TOPK, S // 4)
    assert nt % 2 == 0
    in_specs = [
        pl.BlockSpec((None, 8, t, 32), lambda b, i: (b, 0, i, 0)),
        pl.BlockSpec((None, 8, t), lambda b, i: (b, 0, i)),
        pl.BlockSpec((None, S, 32), lambda b, i: (b, 0, 0)),
        pl.BlockSpec((None, 8, t, 64), lambda b, i: (b, 0, i, 0)),
        pl.BlockSpec((None, S, 128), lambda b, i: (b, 0, 0)),
        pl.BlockSpec((None, nt, PV_ROWS, t), lambda b, i: (b, 0, 0, 0)),
    ]
    return pl.pallas_call(
        functools.partial(_dsa_kernel, k_top=k_top, index_bits=S.bit_length()),
        grid=(B, nt), in_specs=in_specs,
        out_specs=pl.BlockSpec((None, t, 512), lambda b, i: (b, i, 0)),
        out_shape=jax.ShapeDtypeStruct((B, S, 512), BF16),
        scratch_shapes=[pltpu.VMEM((nt, t, t), I32), pltpu.VMEM((nt, t, t), I16),
                        pltpu.VMEM((1, t), I32)] + _flash_scratch(DSA_HEADS, t, t),
        compiler_params=_params(("parallel", "parallel")), name="dsa_attention",
    )(iq, iwt, ik, aq, akv, akvt)


MOBA_HPS = 8


def _moba_kernel(q_ref, kv_ref, kvt_ref, km_ref, o_ref, *flash_refs, n_top):
    t = ATT_T
    own = pl.program_id(2)
    causal = _causal_bias(t)
    flash = _Flash(*flash_refs, t)
    qs = []
    for hh in range(MOBA_HPS):
        q = _pad_q(q_ref[hh])
        km_hi, km_lo = _split_bf16(km_ref[hh])
        gate = _dot_nt(km_hi, q) + _dot_nt(km_lo, q)
        keep = _rank_select_t(gate, own, n_top)
        keep = jnp.where(_row_iota(keep.shape) == own, 1.0, keep)
        qs.append(_bias_lanes(q, (keep - 1.0) * (-MASK_BIAS)))
    flash.reset()

    def operands(cc):
        n0 = 2 * cc
        k0 = pl.multiple_of(n0 * t, 2 * t)
        heads = range(MOBA_HPS)
        return (n0, [kv_ref[hh, pl.ds(k0, 2 * t), :] for hh in heads],
                [[kvt_ref[hh, n0], kvt_ref[hh, n0 + 1]] for hh in heads])

    def past_pair(cc):
        _, kvs, kvts = operands(cc)
        return kvs, kvts, [None] * MOBA_HPS

    flash.run(qs, own // 2, past_pair)
    heads = range(MOBA_HPS)

    @pl.when(own % 2 == 0)
    def _():
        k0 = pl.multiple_of(own * t, t)
        flash.update(qs, [kv_ref[hh, pl.ds(k0, t), :] for hh in heads],
                     [[kvt_ref[hh, own]] for hh in heads], [causal] * MOBA_HPS)

    @pl.when(own % 2 == 1)
    def _():
        _, kvs, kvts = operands(own // 2)
        bias = jnp.concatenate([jnp.zeros((t, t), F32), causal], axis=0)
        flash.update(qs, kvs, kvts, [bias] * MOBA_HPS)

    _store_heads(o_ref, [flash.result(hh) for hh in range(MOBA_HPS)])


def _moba_attention(bq, bkv, bkvt, kmean):
    B, H, S, _ = bq.shape
    t = ATT_T
    hps = MOBA_HPS
    n_blk = S // MOBA_BLOCK
    assert t == MOBA_BLOCK and n_blk % 2 == 0 and H % hps == 0
    n_top = max(1, min(MOBA_TOPK, n_blk - 1))
    in_specs = [
        pl.BlockSpec((None, hps, t, 64), lambda b, h, i: (b, h, i, 0)),
        pl.BlockSpec((None, hps, S, 128), lambda b, h, i: (b, h, 0, 0)),
        pl.BlockSpec((None, hps, n_blk, PV_ROWS, t), lambda b, h, i: (b, h, 0, 0, 0)),
        pl.BlockSpec((None, hps, n_blk, 128), lambda b, h, i: (b, h, 0, 0)),
    ]
    return pl.pallas_call(
        functools.partial(_moba_kernel, n_top=n_top), grid=(B, H // hps, S // t), in_specs=in_specs,
        out_specs=pl.BlockSpec((None, t, hps * 64), lambda b, h, i: (b, i, h)),
        out_shape=jax.ShapeDtypeStruct((B, S, H * 64), BF16),
        scratch_shapes=_flash_scratch(hps, t, 2 * t),
        compiler_params=_params(("parallel", "parallel", "parallel")), name="moba_attention",
    )(bq, bkv, bkvt, kmean)


def _lane_group_norm(y, gain, width):
    outs = []
    for j in range(y.shape[1] // width):
        yc = y[:, j * width:(j + 1) * width]
        outs.append(_rms_rows(yc, gain))
    return jnp.concatenate(outs, axis=1)


def _mem_kv_kernel(m_ref, g_ref, w_ref, gk_ref, o_ref):
    mn = _rms_rows(m_ref[...], g_ref[...]).astype(BF16)
    y = _dot(mn, w_ref[...])
    hw = MEM_HEADS * MEM_HEAD_DIM
    k = _lane_group_norm(y[:, :hw], gk_ref[...], MEM_HEAD_DIM)
    o_ref[...] = jnp.concatenate([k, y[:, hw:]], axis=1).astype(BF16)


def _mem_kv(mem, g, w, gk):
    B, M, _ = mem.shape
    n = w.shape[1]
    return pl.pallas_call(
        _mem_kv_kernel, grid=(B,),
        in_specs=[pl.BlockSpec((None, M, D_MODEL), lambda b: (b, 0, 0)),
                  pl.BlockSpec(g.shape, lambda b: (0, 0)),
                  pl.BlockSpec(w.shape, lambda b: (0, 0)),
                  pl.BlockSpec(gk.shape, lambda b: (0, 0))],
        out_specs=pl.BlockSpec((None, M, n), lambda b: (b, 0, 0)),
        out_shape=jax.ShapeDtypeStruct((B, M, n), BF16),
        compiler_params=_params(("parallel",)), name="mem_kv",
    )(mem, g, w, gk)


def _mem_attend(x, g_ref, wq_ref, gq_ref, kv_ref, wo_ref):
    xn = _rms_rows(x, g_ref[...]).astype(BF16)
    q = _lane_group_norm(_dot(xn, wq_ref[...]), gq_ref[...], MEM_HEAD_DIM).astype(BF16)
    hw = MEM_HEADS * MEM_HEAD_DIM
    scale = MEM_HEAD_DIM ** -0.5
    outs = []
    for h in range(MEM_HEADS):
        cols = slice(h * MEM_HEAD_DIM, (h + 1) * MEM_HEAD_DIM)
        k = kv_ref[:, cols]
        v = kv_ref[:, hw + h * MEM_HEAD_DIM:hw + (h + 1) * MEM_HEAD_DIM]
        s = _dot_nt(q[:, cols], k) * scale
        p = jnp.exp(s - jnp.max(s, axis=-1, keepdims=True))
        p = p / jnp.sum(p, axis=-1, keepdims=True)
        outs.append(_dot(p.astype(BF16), v))
    o = jnp.concatenate(outs, axis=1).astype(BF16)
    return x + _dot(o, wo_ref[...])


def _post_mixer_kernel(*refs, n_in):
    a_refs = refs[:n_in]
    w_refs = refs[n_in:2 * n_in]
    x_ref, gm_ref, wq_ref, gq_ref, kv_ref, wo_ref, gf_ref, wg_ref, wu_ref, wd_ref, o_ref = refs[2 * n_in:]
    x = x_ref[...]
    for a_ref, w_ref in zip(a_refs, w_refs):
        x = x + _dot(a_ref[...], w_ref[...])
    x = _mem_attend(x, gm_ref, wq_ref, gq_ref, kv_ref, wo_ref)
    xn = _rms_rows(x, gf_ref[...]).astype(BF16)
    gate = _dot(xn, wg_ref[...])
    up = _dot(xn, wu_ref[...])
    act = (gate * jax.nn.sigmoid(gate) * up).astype(BF16)
    o_ref[...] = x + _dot(act, wd_ref[...])


def _post_mixer(parts, weights, x2, g_mem, wq, gq, kv, wo, g_ffn, wg, wu, wd, S, tm=512):
    T = x2.shape[0]
    nt = S // tm
    n_in = len(parts)
    M, n = kv.shape[1], kv.shape[2]

    def const(a):
        return pl.BlockSpec(a.shape, lambda i: (0,) * a.ndim, pipeline_mode=pl.Buffered(1))

    in_specs = ([pl.BlockSpec((tm, p.shape[1]), lambda i: (i, 0)) for p in parts]
                + [const(w) for w in weights]
                + [pl.BlockSpec((tm, D_MODEL), lambda i: (i, 0)),
                   const(g_mem), const(wq), const(gq),
                   pl.BlockSpec((None, M, n), lambda i: (i // nt, 0, 0)),
                   const(wo), const(g_ffn), const(wg), const(wu), const(wd)])
    return pl.pallas_call(
        functools.partial(_post_mixer_kernel, n_in=n_in), grid=(T // tm,), in_specs=in_specs,
        out_specs=pl.BlockSpec((tm, D_MODEL), lambda i: (i, 0)),
        out_shape=jax.ShapeDtypeStruct((T, D_MODEL), F32),
        compiler_params=_params(("parallel",)), name="post_mixer",
    )(*parts, *weights, x2, g_mem, wq, gq, kv, wo, g_ffn, wg, wu, wd)


def _nsa_prep_kernel(x_ref, trig_ref, gmix_ref, w_ref, nq_ref, nk_ref,
                     gq_ref, gks_ref, gkw_ref,
                     qc_ref, qr_ref, kvs_ref, kvst_ref, kvw_ref, kvwt_ref, kc_ref, vc_ref, gtt_ref,
                     stage_ref, *, n_tiles):
    xn = _rms_rows(x_ref[...], gmix_ref[...]).astype(BF16)
    c64, s64 = trig_ref[:, :LANES], trig_ref[:, LANES:]
    lane = _lane_iota(c64.shape)
    lo64 = (lane % 64) < 8
    first64 = lane < 64
    c64k = jnp.where(first64, c64, 1.0)
    s64k = jnp.where(first64, s64, 0.0)
    nq = nq_ref[...]
    nk = nk_ref[...]

    col = _ColumnProjector(xn, w_ref)

    for j in range(8):
        qn = _head_norm(col(j), nq, gq_ref[...])
        qr = _rope(qn, c64, s64, lo64, 8)
        qc_ref[2 * j] = qn[:, :64].astype(BF16)
        qc_ref[2 * j + 1] = qn[:, 64:].astype(BF16)
        qr_ref[2 * j] = qr[:, :64].astype(BF16)
        qr_ref[2 * j + 1] = qr[:, 64:].astype(BF16)
    tile = pl.program_id(0) % n_tiles
    sel_blk = tile * (ATT_T // NSA_SEL_LEN) + lax.shift_right_logical(
        _row_iota(c64.shape), NSA_SEL_LEN.bit_length() - 1)
    blk_onehot = jnp.where(lane == HEAD_DIM + sel_blk, 1.0, 0.0)
    for g in range(NSA_GROUPS):
        kv = _kv_column(col(8 + g), nk, gks_ref[...], c64k, s64k, lo64, first64)
        kvs_ref[g] = jnp.where(first64, kv, blk_onehot).astype(BF16)
        kvst_ref[g] = _pv_operand(kv)
        kv = _kv_column(col(12 + g), nk, gkw_ref[...], c64k, s64k, lo64, first64)
        kvw_ref[g] = kv.astype(BF16)
        kvwt_ref[g] = _pv_operand(kv)
    stride = NSA_CMP_STRIDE
    rows = stage_ref.shape[0] // stride
    for out_ref, first in ((kc_ref, 16), (vc_ref, 18)):
        for c in range(2):
            stage_ref[...] = col(first + c)
            for u in range(0, stride, 2):
                pair = [stage_ref[pl.ds(u + v, rows, stride=stride), :] for v in range(2)]
                for h in range(2):
                    halves = [p[:, h * HEAD_DIM:(h + 1) * HEAD_DIM] for p in pair]
                    out_ref[2 * c + h, :, u * HEAD_DIM:(u + 2) * HEAD_DIM] = jnp.concatenate(halves, axis=1)
    gates_t = jax.nn.sigmoid(col(20)).T
    for g in range(NSA_GROUPS):
        gtt_ref[g] = gates_t[NSA_GATE_ROWS * g:NSA_GATE_ROWS * (g + 1), :]


def _nsa_prep(x2, trig, gmix, w, tabs, B, S):
    T = x2.shape[0]
    tm = ATT_T
    nt = S // tm
    nq, nk, gq, gks, gkw = tabs

    def full(a):
        return pl.BlockSpec(a.shape, lambda i: (0,) * a.ndim)

    def hm(width, heads):
        return pl.BlockSpec((None, heads, tm, width), lambda i: (i // nt, 0, i % nt, 0))

    def hmt(heads):
        return pl.BlockSpec((None, heads, None, PV_ROWS, tm), lambda i: (i // nt, 0, i % nt, 0, 0))

    def tokm(width):
        return pl.BlockSpec((None, tm, width), lambda i: (i // nt, i % nt, 0))

    out_shape = (
        jax.ShapeDtypeStruct((B, 16, S, 64), BF16),
        jax.ShapeDtypeStruct((B, 16, S, 64), BF16),
        jax.ShapeDtypeStruct((B, 4, S, 128), BF16),
        jax.ShapeDtypeStruct((B, 4, nt, PV_ROWS, tm), BF16),
        jax.ShapeDtypeStruct((B, 4, S, 128), BF16),
        jax.ShapeDtypeStruct((B, 4, nt, PV_ROWS, tm), BF16),
        jax.ShapeDtypeStruct((B, 4, S // 16, 1024), F32),
        jax.ShapeDtypeStruct((B, 4, S // 16, 1024), F32),
        jax.ShapeDtypeStruct((B, 4, NSA_GATE_ROWS, S), F32),
    )
    rows16 = pl.BlockSpec((None, 4, tm // 16, 1024), lambda i: (i // nt, 0, i % nt, 0))
    out_specs = (hm(64, 16), hm(64, 16), hm(128, 4), hmt(4), hm(128, 4), hmt(4), rows16, rows16,
                 pl.BlockSpec((None, 4, NSA_GATE_ROWS, tm), lambda i: (i // nt, 0, 0, i % nt)))
    in_specs = [pl.BlockSpec((tm, D_MODEL), lambda i: (i, 0)),
                pl.BlockSpec((tm, 2 * LANES), lambda i: (i, 0)),
                full(gmix), full(w), full(nq), full(nk), full(gq), full(gks), full(gkw)]
    return pl.pallas_call(
        functools.partial(_nsa_prep_kernel, n_tiles=nt), grid=(T // tm,), in_specs=in_specs, out_specs=out_specs,
        out_shape=out_shape, scratch_shapes=[pltpu.VMEM((tm, LANES), F32)],
        compiler_params=_params(("parallel",)), name="nsa_prep",
    )(x2, trig, gmix, w, nq, nk, gq, gks, gkw)


def _compress_one(x16, pa, pb, w1a, w1b, w2):
    n16 = x16.shape[0]
    h_a = _dot((x16 + pa).astype(BF16), w1a)
    h_b = _dot((x16 + pb).astype(BF16), w1b)
    pre = h_a + pltpu.roll(h_b, n16 - 1, 0)
    act = pre * jax.nn.sigmoid(pre)
    return _dot(act.astype(BF16), w2)


def _compress_kernel(xk_ref, xv_ref, pk_ref, pv_ref, w1k_ref, w1v_ref, w2k_ref, w2v_ref, gk_ref,
                     o_ref, ot_ref):
    half = w1k_ref.shape[0] // 2
    k = _compress_one(xk_ref[...], pk_ref[0:1, :], pk_ref[1:2, :],
                      w1k_ref[:half, :], w1k_ref[half:, :], w2k_ref[...])
    k = _rms_rows(k, gk_ref[...])
    v = _compress_one(xv_ref[...], pv_ref[0:1, :], pv_ref[1:2, :],
                      w1v_ref[:half, :], w1v_ref[half:, :], w2v_ref[...])
    kv = jnp.concatenate([k, v], axis=1)
    o_ref[...] = kv.astype(BF16)
    ot_ref[...] = _pv_operand(kv)


def _compress(xk16, xv16, pk, pv, w1k, w1v, w2k, w2v, gk):
    B, G, n16, width = xk16.shape

    def full(a):
        return pl.BlockSpec(a.shape, lambda b, g: (0,) * a.ndim)

    xspec = pl.BlockSpec((None, None, n16, width), lambda b, g: (b, g, 0, 0))
    return pl.pallas_call(
        _compress_kernel, grid=(B, G),
        in_specs=[xspec, xspec, full(pk), full(pv), full(w1k), full(w1v), full(w2k), full(w2v), full(gk)],
        out_specs=(pl.BlockSpec((None, None, n16, 128), lambda b, g: (b, g, 0, 0)),
                   pl.BlockSpec((None, None, PV_ROWS, n16), lambda b, g: (b, g, 0, 0))),
        out_shape=(jax.ShapeDtypeStruct((B, G, n16, 128), BF16),
                   jax.ShapeDtypeStruct((B, G, PV_ROWS, n16), BF16)),
        compiler_params=_params(("parallel", "parallel")), name="nsa_compress",
    )(xk16, xv16, pk, pv, w1k, w1v, w2k, w2v, gk)


NSA_GPS = 4


def _nsa_tile_masks(i, n16, n_sel, n_cmp, n_top):
    t = ATT_T
    t0 = i * t
    n_id = _row_iota((n16, t))
    q_id = t0 + _lane_iota((n16, t))
    cmp_visible = (n_id < n_cmp) & (n_id * NSA_CMP_STRIDE + (NSA_CMP_LEN - 1) <= q_id)
    b_id = _row_iota((n_sel, n16)) * NSA_SEL_LEN
    r_id = _lane_iota((n_sel, n16)) * NSA_CMP_STRIDE
    cover_t = ((r_id < b_id + NSA_SEL_LEN) & (r_id + NSA_CMP_LEN > b_id)
               & (_lane_iota((n_sel, n16)) < n_cmp))
    n_wc = NSA_WINDOW // t + 1
    cw = jnp.maximum(i - (n_wc - 1), 0)
    dist = (i - cw) * t + _lane_iota((n_wc * t, t)) - _row_iota((n_wc * t, t))
    blk = _row_iota((n_sel, t))
    cur = lax.shift_right_logical(t0 + _lane_iota((n_sel, t)), NSA_SEL_LEN.bit_length() - 1)
    return dict(
        bias_c=jnp.where(cmp_visible, 0.0, MASK_BIAS),
        cover_t=jnp.where(cover_t, 1.0, 0.0).astype(BF16),
        n_wc=n_wc, cw=cw, bias_w=jnp.where((dist >= 0) & (dist < NSA_WINDOW), 0.0, MASK_BIAS),
        forced=(blk == 0) | (blk == cur) | (blk == cur - 1), visible_blk=blk <= cur,
        n_wanted=jnp.minimum(cur[0:1, :] + 1, n_top).astype(F32))


def _nsa_front(qc_ref, qr_ref, kvc_ref, kvct_ref, kvw_ref, kvwt_ref, gtt_ref, part_ref, masks):
    t = ATT_T
    HG = NSA_HEADS // NSA_GROUPS
    n16 = kvc_ref.shape[0]

    p_sum = jnp.zeros((n16, t), F32)
    o_c = []
    probs, inv_ls, outs = _softmax_direct([_pad_q(qc_ref[j]) for j in range(HG)], kvc_ref[...],
                                          [kvct_ref[...]], masks["bias_c"])
    for j in range(HG):
        p_sum = p_sum + probs[j] * inv_ls[j]
        o_c.append(outs[j] * inv_ls[j])

    p_hi, p_lo = _split_bf16(p_sum)
    imp = _dot(masks["cover_t"], p_hi) + _dot(masks["cover_t"], p_lo)

    qs = [_pad_q(qr_ref[j]) for j in range(HG)]
    n_wc, cw = masks["n_wc"], masks["cw"]
    kw0 = pl.multiple_of(cw * t, t)
    _, inv_lw, out_w = _softmax_direct(qs, kvw_ref[pl.ds(kw0, n_wc * t), :],
                                       [kvwt_ref[cw + u] for u in range(n_wc)], masks["bias_w"])
    for j in range(HG):
        part_ref[:, j * t:(j + 1) * t] = (gtt_ref[3 * j:3 * j + 1, :] * o_c[j]
                                          + gtt_ref[3 * j + 2:3 * j + 3, :] * (out_w[j] * inv_lw[j]))

    imp = jnp.where(masks["forced"], NSA_FORCE, imp)
    return qs, jnp.where(masks["visible_blk"], imp, NEG_INF)


RANK_SEGMENT = 16


def _count_larger(imp, acc, rows):
    for m in rows:
        acc = acc + jnp.where(imp[m:m + 1, :] > imp, 1.0, 0.0)
    return acc


def _nsa_kernel(qc_ref, qr_ref, kvc_ref, kvct_ref, kvs_ref, kvst_ref, kvw_ref, kvwt_ref, gtt_ref,
                o_ref, sel_ref, part_ref, *flash_refs, n_cmp, n_top):
    t = ATT_T
    HG = NSA_HEADS // NSA_GROUPS
    n_slots = NSA_GPS * HG
    i = pl.program_id(2)

    masks = _nsa_tile_masks(i, kvc_ref.shape[1], sel_ref.shape[1], n_cmp, n_top)
    n_sel = sel_ref.shape[1]
    qs, imps = [], []
    for g in range(NSA_GPS):
        heads_g = pl.ds(g * HG, HG)
        q_g, imp_g = _nsa_front(
            qc_ref.at[heads_g], qr_ref.at[heads_g], kvc_ref.at[g], kvct_ref.at[g], kvw_ref.at[g],
            kvwt_ref.at[g], gtt_ref.at[g], part_ref.at[:, pl.ds(g * HG * t, HG * t)], masks)
        qs += q_g
        imps.append(imp_g)
        sel_ref[g] = _count_larger(imp_g, jnp.zeros((n_sel, t), F32), range(RANK_SEGMENT))

    for k in range(1, n_sel // RANK_SEGMENT):
        @pl.when((i + 1) * (t // NSA_SEL_LEN) > k * RANK_SEGMENT)
        def _():
            for g in range(NSA_GPS):
                sel_ref[g] = _count_larger(imps[g], sel_ref[g],
                                           range(k * RANK_SEGMENT, (k + 1) * RANK_SEGMENT))

    miss = None
    for g in range(NSA_GPS):
        sel_fast = sel_ref[g] < n_top
        n_picked = jnp.sum(jnp.where(sel_fast & masks["visible_blk"], 1.0, 0.0), axis=0, keepdims=True)
        sel_ref[g] = jnp.where(sel_fast, 0.0, MASK_BIAS)
        miss_g = jnp.abs(n_picked - masks["n_wanted"])
        miss = miss_g if miss is None else jnp.maximum(miss, miss_g)

    @pl.when(jnp.max(miss) > 0.0)
    def _():
        for g in range(NSA_GPS):
            sel_ref[g] = (_rank_select_t(imps[g], None, n_top) - 1.0) * (-MASK_BIAS)

    qs_sel = [_bias_lanes(qs[s], sel_ref[s // HG]) for s in range(n_slots)]
    flash = _Flash(*flash_refs, t)
    flash.reset()

    def sel_operands(c):
        k0 = pl.multiple_of(c * t, t)
        kvs = [kvs_ref[s // HG, pl.ds(k0, t), :] for s in range(n_slots)]
        kvts = [[kvst_ref[s // HG, c]] for s in range(n_slots)]
        return kvs, kvts

    def past_chunk(c):
        kvs, kvts = sel_operands(c)
        return kvs, kvts, [None] * n_slots

    flash.run(qs_sel, i, past_chunk)
    kvs, kvts = sel_operands(i)
    flash.update(qs_sel, kvs, kvts, [_causal_bias(t)] * n_slots)

    heads = []
    for s in range(n_slots):
        g, j = divmod(s, HG)
        heads.append(part_ref[:, s * t:(s + 1) * t] + gtt_ref[g, 3 * j + 1:3 * j + 2, :] * flash.result(s))
    _store_heads(o_ref, heads)


def _nsa_attention(qc, qr, kvc, kvct, kvs, kvst, kvw, kvwt, gates_t):
    B, H, S, _ = qc.shape
    G = NSA_GROUPS
    HG = H // G
    t = ATT_T
    nt = S // t
    n16 = kvc.shape[2]
    n_cmp = (S - NSA_CMP_LEN) // NSA_CMP_STRIDE + 1
    n_sel = S // NSA_SEL_LEN
    n_top = min(NSA_SEL_TOPK, n_sel)
    gps = NSA_GPS
    assert G % gps == 0 and nt % 2 == 0 and n_sel <= HEAD_DIM and S >= (NSA_WINDOW // t + 1) * t
    qspec = pl.BlockSpec((None, gps * HG, t, 64), lambda b, g, i: (b, g, i, 0))
    once = pl.Buffered(1)
    kvspec = pl.BlockSpec((None, gps, S, 128), lambda b, g, i: (b, g, 0, 0), pipeline_mode=once)
    kvtspec = pl.BlockSpec((None, gps, nt, PV_ROWS, t), lambda b, g, i: (b, g, 0, 0, 0), pipeline_mode=once)
    in_specs = [qspec, qspec,
                pl.BlockSpec((None, gps, n16, 128), lambda b, g, i: (b, g, 0, 0)),
                pl.BlockSpec((None, gps, PV_ROWS, n16), lambda b, g, i: (b, g, 0, 0)),
                kvspec, kvtspec, kvspec, kvtspec,
                pl.BlockSpec((None, gps, NSA_GATE_ROWS, t), lambda b, g, i: (b, g, 0, i))]
    return pl.pallas_call(
        functools.partial(_nsa_kernel, n_cmp=n_cmp, n_top=n_top), grid=(B, G // gps, nt), in_specs=in_specs,
        out_specs=pl.BlockSpec((None, t, gps * HG * 64), lambda b, g, i: (b, i, g)),
        out_shape=jax.ShapeDtypeStruct((B, S, H * 64), BF16),
        scratch_shapes=[pltpu.VMEM((gps, n_sel, t), F32), pltpu.VMEM((PV_ROWS, gps * HG * t), F32)]
                       + _flash_scratch(gps * HG, t, t),
        compiler_params=_params(("parallel", "parallel", "parallel")), name="nsa_attention",
    )(qc, qr, kvc, kvct, kvs, kvst, kvw, kvwt, gates_t)


def _rope_freq_row(period, rot):
    half = rot // 2
    inv_freq = ROPE_THETA ** (-(jnp.arange(half, dtype=F32) * 2.0 / rot))
    lane = jnp.arange(LANES) % period
    f = jnp.where(lane < rot, inv_freq[lane % half], 0.0)
    return f.reshape(1, LANES).astype(F32)


def _norm_matrices():
    r = jnp.arange(LANES)
    same = (r[:, None] // 64) == (r[None, :] // 64)
    nq = jnp.where(same, 1.0 / 64, 0.0).astype(BF16)
    nk = jnp.where(same & (r[:, None] < 64), 1.0 / 64, 0.0).astype(BF16)
    return nq, nk


def _q_gain(g):
    return (jnp.tile(g.astype(F32), 2) * Q_SCALE).reshape(1, LANES)


def _k_gain(g):
    return jnp.concatenate([g.astype(F32), jnp.ones((64,), F32)]).reshape(1, LANES)


def _interleave_kv(wk, wv, n_heads):
    d = wk.shape[0]
    wk = wk.reshape(d, n_heads, 64)
    wv = wv.reshape(d, n_heads, 64)
    return jnp.concatenate([wk, wv], axis=2).reshape(d, n_heads * 128)


def _split_cols(w, sizes):
    out, start = [], 0
    for n in sizes:
        out.append(w[:, start:start + n])
        start += n
    return out


def _mixer_layer0(x2, trig, B, S, gmix, w_in, w_out, a_q_norm, a_k_norm, b_q_norm, b_k_norm):
    sizes = (512, 64, 64, 256, 32, 8, 512, 512, 512)
    waq, wak, wav, wiq, wik, wiw, wbq, wbk, wbv = _split_cols(w_in, sizes)
    pad = jnp.zeros((D_MODEL, LANES - 40), w_in.dtype)
    w = jnp.concatenate([waq, wbq, _interleave_kv(wbk, wbv, 8), wak, wav, wiq, wik, wiw, pad],
                        axis=1).astype(BF16)
    nq, nk = _norm_matrices()
    tabs = (nq, nk, _q_gain(a_q_norm), _q_gain(b_q_norm), _k_gain(a_k_norm), _k_gain(b_k_norm))
    aq, bq, bkv, bkvt, akv, akvt, iq, ik, iwt, km = _ab_prep(x2, trig, gmix, w, tabs, B, S)
    n_blk = S // MOBA_BLOCK
    kmean = km.reshape(B, n_blk, 8, 128).transpose(0, 2, 1, 3)
    o_a = _dsa_attention(iq, iwt, ik, aq, akv, akvt).reshape(B * S, 512)
    o_b = _moba_attention(bq, bkv, bkvt, kmean).reshape(B * S, 512)
    w_out = w_out.astype(BF16)
    return [o_a, o_b], [w_out[:512], w_out[512:]]


def _mixer_layer1(x2, trig, B, S, gmix, w_in, w_out, q_norm, kcmp_norm, ksel_norm, kwin_norm,
                  pos_k, pos_v, w1_k, w2_k, w1_v, w2_v):
    G = NSA_GROUPS
    sizes = (1024,) + (256,) * 6 + (48,)
    wq, wkc, wvc, wks, wvs, wkw, wvw, wgt = _split_cols(w_in, sizes)
    pad = jnp.zeros((D_MODEL, LANES - 48), w_in.dtype)
    w = jnp.concatenate([wq, _interleave_kv(wks, wvs, G), _interleave_kv(wkw, wvw, G),
                         wkc, wvc, wgt, pad], axis=1).astype(BF16)
    nq, nk = _norm_matrices()
    tabs = (nq, nk, _q_gain(q_norm), _k_gain(ksel_norm), _k_gain(kwin_norm))
    qc, qr, kvs, kvst, kvw, kvwt, kc16, vc16, gates_t = _nsa_prep(x2, trig, gmix, w, tabs, B, S)

    def pos_rows(p):
        return p.astype(F32).reshape(2, NSA_CMP_STRIDE * HEAD_DIM)

    kvc, kvct = _compress(kc16, vc16, pos_rows(pos_k), pos_rows(pos_v),
                          w1_k.astype(BF16), w1_v.astype(BF16), w2_k.astype(BF16), w2_v.astype(BF16),
                          kcmp_norm.astype(F32).reshape(1, HEAD_DIM))
    o = _nsa_attention(qc, qr, kvc, kvct, kvs, kvst, kvw, kvwt, gates_t)
    return [o.reshape(B * S, NSA_HEADS * HEAD_DIM)], [w_out.astype(BF16)]


def _finish_layer(parts, weights, x2, mem, S, g_mem, g_src, w_q, w_kv, w_o, q_norm, k_norm,
                  g_ffn, ffn_w_in, ffn_w_out):
    row = lambda v: v.astype(F32).reshape(1, -1)
    kv = _mem_kv(mem, row(g_src), w_kv.astype(BF16), row(k_norm))
    wg = ffn_w_in[:, :D_FF].astype(BF16)
    wu = ffn_w_in[:, D_FF:].astype(BF16)
    return _post_mixer(parts, weights, x2, row(g_mem), w_q.astype(BF16), row(q_norm), kv, w_o.astype(BF16),
                       row(g_ffn), wg, wu, ffn_w_out.astype(BF16), S)


def kernel(x, mem, positions, norm_mix, norm_mem, norm_mem_src, norm_ffn, ab_w_in, ab_w_out, dsa_q_norm, dsa_k_norm, moba_q_norm, moba_k_norm, nsa_w_in, nsa_w_out, nsa_q_norm, nsa_kcmp_norm, nsa_ksel_norm, nsa_kwin_norm, nsa_cmp_pos_k, nsa_cmp_pos_v, nsa_cmp_w1_k, nsa_cmp_w2_k, nsa_cmp_w1_v, nsa_cmp_w2_v, mem_w_q, mem_w_kv, mem_w_o, mem_q_norm, mem_k_norm, ffn_w_in, ffn_w_out):
    B, S, D = x.shape
    depth = norm_mix.shape[0]
    x2 = x.reshape(B * S, D)
    trig = _rope_trig(positions.astype(F32).reshape(B * S, 1), _rope_freq_row(64, 16), _rope_freq_row(32, 8))
    row = lambda v: v.astype(F32).reshape(1, -1)
    for i in range(depth):
        j = i // 2
        if i % 2 == 0:
            parts, weights = _mixer_layer0(x2, trig, B, S, row(norm_mix[i]), ab_w_in[j], ab_w_out[j],
                               dsa_q_norm[j], dsa_k_norm[j], moba_q_norm[j], moba_k_norm[j])
        else:
            parts, weights = _mixer_layer1(x2, trig, B, S, row(norm_mix[i]), nsa_w_in[j], nsa_w_out[j],
                               nsa_q_norm[j], nsa_kcmp_norm[j], nsa_ksel_norm[j], nsa_kwin_norm[j],
                               nsa_cmp_pos_k[j], nsa_cmp_pos_v[j], nsa_cmp_w1_k[j], nsa_cmp_w2_k[j],
                               nsa_cmp_w1_v[j], nsa_cmp_w2_v[j])
        x2 = _finish_layer(parts, weights, x2, mem, S, norm_mem[i], norm_mem_src[i], mem_w_q[i], mem_w_kv[i],
                           mem_w_o[i], mem_q_norm[i], mem_k_norm[i], norm_ffn[i], ffn_w_in[i], ffn_w_out[i])
    return x2.reshape(B, S, D)
```

```python
import functools
import math

import jax
import jax.numpy as jnp
from jax import lax
from jax.experimental import pallas as pl
from jax.experimental.pallas import tpu as pltpu

F32 = jnp.float32
BF16 = jnp.bfloat16
I32 = jnp.int32
I16 = jnp.int16

D_MODEL = 1024
N_MEM = 256
HEAD_DIM = 64
ROPE_THETA = 500000.0
RMS_EPS = 1e-6
NEG_INF = -1e30
TINY = 1e-20

DSA_HEADS = 8
DSA_IDX_HEADS = 8
DSA_IDX_DIM = 32
DSA_TOPK = 256
MOBA_HEADS = 8
MOBA_BLOCK = 256
MOBA_TOPK = 3
NSA_HEADS = 16
NSA_GROUPS = 4
NSA_CMP_LEN = 32
NSA_CMP_STRIDE = 16
NSA_SEL_LEN = 64
NSA_SEL_TOPK = 16
NSA_WINDOW = 512
NSA_FORCE = 1e4
NSA_GATE_ROWS = 3 * (NSA_HEADS // NSA_GROUPS)
MEM_HEADS = 4
MEM_HEAD_DIM = 128
D_FF = ((8 * D_MODEL + 3 * 256 - 1) // (3 * 256)) * 256

LANES = 128
SUBLANES = 8
INT_MIN = -(2 ** 31)
VMEM_LIMIT = 60 * 1024 * 1024

PV_HEAD_ROWS = 16
PV_ROWS = PV_HEAD_ROWS + HEAD_DIM
PROJ_GROUP = 4
ATT_T = 256
MASK_BIAS = -1e30
M_FLOOR = -1e29
LOG2E = math.log2(math.e)
Q_SCALE = HEAD_DIM ** -0.5 * LOG2E

NT_DIMS = (((1,), (1,)), ((), ()))


def _dot(a, b):
    return jnp.dot(a, b, preferred_element_type=F32)


def _dot_nt(a, b):
    return lax.dot_general(a, b, NT_DIMS, preferred_element_type=F32)


def _split_bf16(a):
    hi = a.astype(BF16)
    return hi, (a - hi.astype(F32)).astype(BF16)


def _split_dot(a, b):
    hi, lo = _split_bf16(a)
    return _dot(hi, b) + _dot(lo, b)


def _rms_rows(x, gain):
    ms = jnp.mean(x * x, axis=-1, keepdims=True)
    return x * lax.rsqrt(ms + RMS_EPS) * gain


def _params(sem):
    return pltpu.CompilerParams(dimension_semantics=sem, vmem_limit_bytes=VMEM_LIMIT)


def _head_norm(y, norm_m, gain):
    ms = _split_dot(y * y, norm_m)
    return y * lax.rsqrt(ms + RMS_EPS) * gain


def _rope(y, c, s, lo_mask, half):
    sw = jnp.where(lo_mask, pltpu.roll(y, LANES - half, 1), pltpu.roll(y, half, 1))
    return y * c + sw * s


def _lane_iota(shape):
    return lax.broadcasted_iota(I32, shape, 1)


def _row_iota(shape):
    return lax.broadcasted_iota(I32, shape, 0)


def _rope_tables(pos, ftab, period, half):
    ang = pos * ftab
    lane = _lane_iota(ang.shape) % period
    c = jnp.cos(ang)
    s = jnp.sin(ang) * jnp.where(lane < half, -1.0, 1.0)
    return c, s


def _pv_operand(kv):
    head = jnp.where(_row_iota((PV_HEAD_ROWS, kv.shape[0])) == 0, 1.0, 0.0)
    return jnp.concatenate([head, kv.T[HEAD_DIM:, :]], axis=0).astype(BF16)


class _ColumnProjector:
    def __init__(self, xn, w_ref):
        self.xn, self.w_ref, self.groups = xn, w_ref, {}

    def __call__(self, j):
        g, u = divmod(j, PROJ_GROUP)
        if g not in self.groups:
            width = PROJ_GROUP * LANES
            lo = g * width
            hi = min(lo + width, self.w_ref.shape[1])
            self.groups[g] = _dot(self.xn, self.w_ref[:, lo:hi])
        return self.groups[g][:, u * LANES:(u + 1) * LANES]


def _kv_column(yc, nk, gain, c64k, s64k, lo64, first64):
    kn = jnp.where(first64, _head_norm(yc, nk, gain), yc)
    return _rope(kn, c64k, s64k, lo64, 8)


def _rope_trig_kernel(pos_ref, f64_ref, f32_ref, o_ref):
    pos = pos_ref[...]
    c64, s64 = _rope_tables(pos, f64_ref[...], 64, 8)
    c32, s32 = _rope_tables(pos, f32_ref[...], 32, 4)
    o_ref[...] = jnp.concatenate([c64, s64, c32, s32], axis=1)


def _rope_trig(pos2, f64, f32t, tm=1024):
    T = pos2.shape[0]
    return pl.pallas_call(
        _rope_trig_kernel, grid=(T // tm,),
        in_specs=[pl.BlockSpec((tm, 1), lambda i: (i, 0)),
                  pl.BlockSpec(f64.shape, lambda i: (0, 0)), pl.BlockSpec(f32t.shape, lambda i: (0, 0))],
        out_specs=pl.BlockSpec((tm, 4 * LANES), lambda i: (i, 0)),
        out_shape=jax.ShapeDtypeStruct((T, 4 * LANES), F32),
        compiler_params=_params(("parallel",)), name="rope_trig",
    )(pos2, f64, f32t)


def _ab_prep_kernel(x_ref, trig_ref, gmix_ref, w_ref, nq_ref, nk_ref,
                    gaq_ref, gbq_ref, gak_ref, gbk_ref,
                    aq_ref, bq_ref, bkv_ref, bkvt_ref, akv_ref, akvt_ref, iq_ref, ik_ref, iwt_ref, km_ref,
                    *, n_tiles):
    xn = _rms_rows(x_ref[...], gmix_ref[...]).astype(BF16)
    c64, s64, c32, s32 = [trig_ref[:, j * LANES:(j + 1) * LANES] for j in range(4)]
    lane = _lane_iota(c64.shape)
    lo64 = (lane % 64) < 8
    lo32 = (lane % 32) < 4
    first64 = lane < 64
    c64k = jnp.where(first64, c64, 1.0)
    s64k = jnp.where(first64, s64, 0.0)
    first32 = lane < 32
    c32k = jnp.where(first32, c32, 1.0)
    s32k = jnp.where(first32, s32, 0.0)
    nq = nq_ref[...]
    nk = nk_ref[...]

    col = _ColumnProjector(xn, w_ref)

    for j in range(4):
        q = _rope(_head_norm(col(j), nq, gaq_ref[...]), c64, s64, lo64, 8)
        aq_ref[2 * j] = q[:, :64].astype(BF16)
        aq_ref[2 * j + 1] = q[:, 64:].astype(BF16)
    for j in range(4):
        q = _rope(_head_norm(col(4 + j), nq, gbq_ref[...]), c64, s64, lo64, 8)
        bq_ref[2 * j] = q[:, :64].astype(BF16)
        bq_ref[2 * j + 1] = q[:, 64:].astype(BF16)
    blk_onehot = jnp.where(lane == HEAD_DIM + pl.program_id(0) % n_tiles, 1.0, 0.0)
    for h in range(8):
        kv = _kv_column(col(8 + h), nk, gbk_ref[...], c64k, s64k, lo64, first64)
        bkv_ref[h] = jnp.where(first64, kv, blk_onehot).astype(BF16)
        bkvt_ref[h] = _pv_operand(kv)
        km_ref[h:h + 1, :] = jnp.mean(kv, axis=0, keepdims=True)
    kv = _kv_column(col(16), nk, gak_ref[...], c64k, s64k, lo64, first64)
    akv_ref[...] = kv.astype(BF16)
    akvt_ref[...] = _pv_operand(kv)
    for j in range(2):
        q = _rope(col(17 + j), c32, s32, lo32, 4)
        for u in range(4):
            iq_ref[4 * j + u] = q[:, 32 * u:32 * (u + 1)].astype(BF16)
    yc = col(19)
    ik_ref[...] = _rope(yc, c32k, s32k, lo32, 4)[:, :32].astype(BF16)
    iwt_ref[...] = yc.T[32:40, :]


def _ab_prep(x2, trig, gmix, w, tabs, B, S):
    T = x2.shape[0]
    tm = ATT_T
    nt = S // tm
    n_cols = w.shape[1]
    nq, nk, gaq, gbq, gak, gbk = tabs

    def full(a):
        return pl.BlockSpec(a.shape, lambda i: (0,) * a.ndim)

    def hm(width, heads=8):
        return pl.BlockSpec((None, heads, tm, width), lambda i: (i // nt, 0, i % nt, 0))

    def tokm(width):
        return pl.BlockSpec((None, tm, width), lambda i: (i // nt, i % nt, 0))

    out_shape = (
        jax.ShapeDtypeStruct((B, 8, S, 64), BF16),
        jax.ShapeDtypeStruct((B, 8, S, 64), BF16),
        jax.ShapeDtypeStruct((B, 8, S, 128), BF16),
        jax.ShapeDtypeStruct((B, 8, nt, PV_ROWS, tm), BF16),
        jax.ShapeDtypeStruct((B, S, 128), BF16),
        jax.ShapeDtypeStruct((B, nt, PV_ROWS, tm), BF16),
        jax.ShapeDtypeStruct((B, 8, S, 32), BF16),
        jax.ShapeDtypeStruct((B, S, 32), BF16),
        jax.ShapeDtypeStruct((B, 8, S), F32),
        jax.ShapeDtypeStruct((T // tm, 8, 128), F32),
    )
    out_specs = (hm(64), hm(64), hm(128),
                 pl.BlockSpec((None, 8, None, PV_ROWS, tm), lambda i: (i // nt, 0, i % nt, 0, 0)),
                 tokm(128),
                 pl.BlockSpec((None, None, PV_ROWS, tm), lambda i: (i // nt, i % nt, 0, 0)),
                 hm(32), tokm(32),
                 pl.BlockSpec((None, 8, tm), lambda i: (i // nt, 0, i % nt)),
                 pl.BlockSpec((None, 8, 128), lambda i: (i, 0, 0)))
    in_specs = [pl.BlockSpec((tm, D_MODEL), lambda i: (i, 0)),
                pl.BlockSpec((tm, 4 * LANES), lambda i: (i, 0)),
                full(gmix), pl.BlockSpec((D_MODEL, n_cols), lambda i: (0, 0)),
                full(nq), full(nk), full(gaq), full(gbq), full(gak), full(gbk)]
    return pl.pallas_call(
        functools.partial(_ab_prep_kernel, n_tiles=nt), grid=(T // tm,), in_specs=in_specs, out_specs=out_specs,
        out_shape=out_shape, compiler_params=_params(("parallel",)), name="ab_prep",
    )(x2, trig, gmix, w, nq, nk, gaq, gbq, gak, gbk)


def _pad_q(q):
    return jnp.concatenate([q, jnp.zeros_like(q)], axis=1)


def _bias_lanes(q, rows):
    n, tq = rows.shape
    parts = [jnp.zeros((HEAD_DIM, tq), F32), rows]
    if n < HEAD_DIM:
        parts.append(jnp.zeros((HEAD_DIM - n, tq), F32))
    lanes = jnp.concatenate(parts, axis=0).T.astype(BF16)
    return jnp.where(_lane_iota(q.shape) < HEAD_DIM, q, lanes)


class _Flash:
    def __init__(self, m_ref, acc_ref, s_ref, cmax_ref, p_ref, tq):
        self.m_ref, self.acc_ref, self.tq = m_ref, acc_ref, tq
        self.s_ref, self.cmax_ref, self.p_ref = s_ref, cmax_ref, p_ref

    def reset(self):
        self.m_ref[...] = jnp.full(self.m_ref.shape, M_FLOOR, F32)
        self.acc_ref[...] = jnp.zeros(self.acc_ref.shape, F32)

    def _scores(self, buf, qs, kvs, biases):
        tq = self.tq
        for i in range(len(qs)):
            s = _dot_nt(kvs[i], qs[i])
            if biases[i] is not None:
                s = s + biases[i]
            self.s_ref[buf, i, :s.shape[0], :] = s
            self.cmax_ref[buf, :, i * tq:(i + 1) * tq] = jnp.max(s, axis=0, keepdims=True)

    def update(self, qs, kvs, kvts, biases):
        self._scores(0, qs, kvs, biases)
        self._finish(0, kvts)

    def run(self, qs, count, operands):
        def scores(c, buf):
            kvs, _, biases = operands(c)
            self._scores(buf, qs, kvs, biases)

        def finish(c, buf):
            self._finish(buf, operands(c)[1])

        last = jnp.maximum(count - 1, 0)
        scores(0, 0)

        def two_chunks(pp, carry):
            c = 2 * pp
            scores(c + 1, 1)
            finish(c, 0)
            scores(jnp.minimum(c + 2, last), 0)
            finish(c + 1, 1)
            return carry

        lax.fori_loop(0, count // 2, two_chunks, 0)

        @pl.when(count % 2 == 1)
        def _():
            finish(count - 1, 0)

    def _finish(self, buf, kvts):
        n = len(kvts)
        tq = self.tq
        kc = sum(kvt.shape[1] for kvt in kvts[0])
        alphas = []
        for i in range(n):
            cols = slice(i * tq, (i + 1) * tq)
            m = self.m_ref[:, cols]
            m_new = jnp.maximum(m, self.cmax_ref[buf, :, cols])
            p = jnp.exp2(self.s_ref[buf, i, :kc, :] - m_new)
            alpha = jnp.exp2(m - m_new)
            self.m_ref[:, cols] = m_new
            self.p_ref[i, :kc, :] = p.astype(BF16)
            alphas.append(alpha)
        for i in range(n):
            cols = slice(i * tq, (i + 1) * tq)
            pv, r0 = None, 0
            for kvt in kvts[i]:
                part = _dot(kvt, self.p_ref[i, r0:r0 + kvt.shape[1], :])
                pv = part if pv is None else pv + part
                r0 += kvt.shape[1]
            self.acc_ref[:, cols] = alphas[i] * self.acc_ref[:, cols] + pv

    def result(self, slot):
        cols = slice(slot * self.tq, (slot + 1) * self.tq)
        acc = self.acc_ref[:, cols]
        return acc / jnp.maximum(acc[0:1, :], TINY)


def _flash_scratch(n_slots, tq, kc):
    return [pltpu.VMEM((1, n_slots * tq), F32), pltpu.VMEM((PV_ROWS, n_slots * tq), F32),
            pltpu.VMEM((2, n_slots, kc, tq), F32), pltpu.VMEM((2, 1, n_slots * tq), F32),
            pltpu.VMEM((n_slots, kc, tq), BF16)]


def _softmax_direct(qs, kv, kvts, bias):
    scores = [_dot_nt(kv, q) for q in qs]
    probs = []
    for s in scores:
        s = s + bias
        m = jnp.maximum(jnp.max(s, axis=0, keepdims=True), M_FLOOR)
        probs.append(jnp.exp2(s - m))
    inv_ls, outs = [], []
    for p in probs:
        pb = p.astype(BF16)
        o, r0 = None, 0
        for kvt in kvts:
            part = _dot(kvt, pb[r0:r0 + kvt.shape[1]])
            o = part if o is None else o + part
            r0 += kvt.shape[1]
        outs.append(o)
        inv_ls.append(1.0 / jnp.maximum(o[0:1, :], TINY))
    return probs, inv_ls, outs


def _causal_bias(t):
    return jnp.where(_row_iota((t, t)) <= _lane_iota((t, t)), 0.0, MASK_BIAS)


def _store_heads(o_ref, heads_t):
    for u in range(len(heads_t) // 2):
        pair = jnp.concatenate([heads_t[2 * u][PV_HEAD_ROWS:, :], heads_t[2 * u + 1][PV_HEAD_ROWS:, :]], axis=0)
        o_ref[:, u * LANES:(u + 1) * LANES] = pair.T.astype(o_ref.dtype)


def _rank_select_t(v, n_valid, n_top):
    n = v.shape[0]
    row = _row_iota(v.shape)
    rank = jnp.zeros(v.shape, F32)
    for m in range(n):
        vm = v[m:m + 1, :]
        ahead = (vm > v) | ((vm == v) & (m < row))
        if n_valid is not None:
            ahead = ahead & (m < n_valid)
        rank = rank + jnp.where(ahead, 1.0, 0.0)
    sel = rank < n_top
    if n_valid is not None:
        sel = sel & (row < n_valid)
    return jnp.where(sel, 1.0, 0.0)


def _dsa_kernel(iq_ref, iwt_ref, ik_ref, aq_ref, akv_ref, akvt_ref, o_ref,
                sk_ref, half_ref, xcut_ref, *flash_refs, k_top, index_bits):
    t = ATT_T
    i = pl.program_id(1)
    n_ch = i + 1
    kio = _row_iota((t, t))
    qio = _lane_iota((t, t))

    def causal(c):
        return (c - i) * t + kio <= qio

    def score_chunk(c):
        k0 = pl.multiple_of(c * t, t)
        ikc = ik_ref[pl.ds(k0, t), :]
        sc = jnp.zeros((t, t), F32)
        for h in range(DSA_IDX_HEADS):
            logit = _dot_nt(ikc, iq_ref[h])
            sc = sc + iwt_ref[h:h + 1, :] * jnp.maximum(logit, 0.0)
        sc = jnp.where(sc == 0.0, 0.0, sc)
        bits = pltpu.bitcast(sc, I32)
        key = bits ^ ((bits >> 31) & 0x7FFFFFFF)
        key = jnp.where(causal(c), key, INT_MIN)
        sk_ref[c] = key
        half_ref[c] = (key >> 16).astype(I16)

    def score_pair(cc, carry):
        score_chunk(2 * cc)
        score_chunk(2 * cc + 1)
        return carry

    lax.fori_loop(0, (n_ch + 1) // 2, score_pair, 0)

    def count(pred):
        def body(c, acc8):
            ind = jnp.where(pred(sk_ref[c], c), 1.0, 0.0)
            return acc8 + ind.reshape(-1, SUBLANES, t).sum(axis=0)
        acc8 = lax.fori_loop(0, n_ch, body, jnp.zeros((SUBLANES, t), F32))
        return jnp.sum(acc8, axis=0, keepdims=True)

    def count_half(cand):
        rows = 2 * SUBLANES

        def body(cc, acc):
            parts = []
            for c in (2 * cc, 2 * cc + 1):
                ind = jnp.where(half_ref[c] >= cand, jnp.bfloat16(1), jnp.bfloat16(0))
                parts += [ind[rows * j:rows * (j + 1), :] for j in range(t // rows)]
            while len(parts) > 1:
                parts = [parts[2 * j] + parts[2 * j + 1] for j in range(len(parts) // 2)]
            return acc + parts[0].astype(F32)
        acc = lax.fori_loop(0, (n_ch + 1) // 2, body, jnp.zeros((rows, t), F32))
        return jnp.sum(acc, axis=0, keepdims=True)

    def half_search(n_all):
        def bit_step(b, carry):
            v, n_ge_v = carry
            cand = v + lax.shift_left(jnp.int32(1), 15 - b)
            n_ge_cand = count_half(cand.astype(I16))
            ok = n_ge_cand >= k_top
            return jnp.where(ok, cand, v), jnp.where(ok, n_ge_cand, n_ge_v)
        return lax.fori_loop(0, 16, bit_step, (jnp.full((1, t), -(2 ** 15), I32), n_all))

    thr_hi, n_ge_hi = half_search(jnp.full((1, t), t * n_ch, I32).astype(F32))

    def low_half_pair(cc, carry):
        for c in (2 * cc, 2 * cc + 1):
            key = sk_ref[c]
            hi = key >> 16
            lo = (key & 0xFFFF) - 2 ** 15
            half_ref[c] = jnp.where(hi > thr_hi, 2 ** 15 - 1,
                                    jnp.where(hi < thr_hi, -(2 ** 15), lo)).astype(I16)
        return carry

    lax.fori_loop(0, (n_ch + 1) // 2, low_half_pair, 0)
    thr_lo, n_ge = half_search(n_ge_hi)
    thr = lax.shift_left(thr_hi, 16) + (thr_lo + 2 ** 15)

    xcut_ref[...] = jnp.full((1, t), 2 ** 30, I32)

    @pl.when(jnp.max(n_ge) > k_top)
    def _():
        need = k_top - count(lambda blk, c: blk > thr)

        def x_step(b, x):
            cand = x + lax.shift_left(jnp.int32(1), index_bits - 1 - b)
            ties_below = count(lambda blk, c: (blk == thr) & (c * t + kio < cand))
            return jnp.where(ties_below <= need, cand, x)
        xcut_ref[...] = lax.fori_loop(0, index_bits, x_step, jnp.zeros((1, t), I32))

    xcut = xcut_ref[...]

    def chunk_bias(c):
        blk = sk_ref[c]
        keep = (blk > thr) | ((blk == thr) & (c * t + kio < xcut))
        return jnp.where(keep & causal(c), 0.0, MASK_BIAS)

    flash = _Flash(*flash_refs, t)
    qs = [_pad_q(aq_ref[h]) for h in range(DSA_HEADS)]
    flash.reset()

    n = DSA_HEADS

    def att_chunk(c):
        k0 = pl.multiple_of(c * t, t)
        return [akv_ref[pl.ds(k0, t), :]] * n, [[akvt_ref[c]]] * n, [chunk_bias(c)] * n

    flash.run(qs, n_ch, att_chunk)
    _store_heads(o_ref, [flash.result(h) for h in range(DSA_HEADS)])


def _dsa_attention(iq, iwt, ik, aq, akv, akvt):
    B, _, S, _ = aq.shape
    t = ATT_T
    nt = S // t
    k_top = min(DSA_TOPK, S // 4)
    assert nt % 2 == 0
    in_specs = [
        pl.BlockSpec((None, 8, t, 32), lambda b, i: (b, 0, i, 0)),
        pl.BlockSpec((None, 8, t), lambda b, i: (b, 0, i)),
        pl.BlockSpec((None, S, 32), lambda b, i: (b, 0, 0)),
        pl.BlockSpec((None, 8, t, 64), lambda b, i: (b, 0, i, 0)),
        pl.BlockSpec((None, S, 128), lambda b, i: (b, 0, 0)),
        pl.BlockSpec((None, nt, PV_ROWS, t), lambda b, i: (b, 0, 0, 0)),
    ]
    return pl.pallas_call(
        functools.partial(_dsa_kernel, k_top=k_top, index_bits=S.bit_length()),
        grid=(B, nt), in_specs=in_specs,
        out_specs=pl.BlockSpec((None, t, 512), lambda b, i: (b, i, 0)),
        out_shape=jax.ShapeDtypeStruct((B, S, 512), BF16),
        scratch_shapes=[pltpu.VMEM((nt, t, t), I32), pltpu.VMEM((nt, t, t), I16),
                        pltpu.VMEM((1, t), I32)] + _flash_scratch(DSA_HEADS, t, t),
        compiler_params=_params(("parallel", "parallel")), name="dsa_attention",
    )(iq, iwt, ik, aq, akv, akvt)


MOBA_HPS = 8


def _moba_kernel(q_ref, kv_ref, kvt_ref, km_ref, o_ref, *flash_refs, n_top):
    t = ATT_T
    own = pl.program_id(2)
    causal = _causal_bias(t)
    flash = _Flash(*flash_refs, t)
    qs = []
    for hh in range(MOBA_HPS):
        q = _pad_q(q_ref[hh])
        km_hi, km_lo = _split_bf16(km_ref[hh])
        gate = _dot_nt(km_hi, q) + _dot_nt(km_lo, q)
        keep = _rank_select_t(gate, own, n_top)
        keep = jnp.where(_row_iota(keep.shape) == own, 1.0, keep)
        qs.append(_bias_lanes(q, (keep - 1.0) * (-MASK_BIAS)))
    flash.reset()

    heads = range(MOBA_HPS)

    def operands(n):
        k0 = pl.multiple_of(n * t, t)
        return [kv_ref[hh, pl.ds(k0, t), :] for hh in heads], [[kvt_ref[hh, n]] for hh in heads]

    def past_block(n):
        kvs, kvts = operands(n)
        return kvs, kvts, [None] * MOBA_HPS

    flash.run(qs, own, past_block)
    kvs, kvts = operands(own)
    flash.update(qs, kvs, kvts, [causal] * MOBA_HPS)

    _store_heads(o_ref, [flash.result(hh) for hh in range(MOBA_HPS)])


def _moba_attention(bq, bkv, bkvt, kmean):
    B, H, S, _ = bq.shape
    t = ATT_T
    hps = MOBA_HPS
    n_blk = S // MOBA_BLOCK
    assert t == MOBA_BLOCK and n_blk % 2 == 0 and H % hps == 0
    n_top = max(1, min(MOBA_TOPK, n_blk - 1))
    in_specs = [
        pl.BlockSpec((None, hps, t, 64), lambda b, h, i: (b, h, i, 0)),
        pl.BlockSpec((None, hps, S, 128), lambda b, h, i: (b, h, 0, 0)),
        pl.BlockSpec((None, hps, n_blk, PV_ROWS, t), lambda b, h, i: (b, h, 0, 0, 0)),
        pl.BlockSpec((None, hps, n_blk, 128), lambda b, h, i: (b, h, 0, 0)),
    ]
    return pl.pallas_call(
        functools.partial(_moba_kernel, n_top=n_top), grid=(B, H // hps, S // t), in_specs=in_specs,
        out_specs=pl.BlockSpec((None, t, hps * 64), lambda b, h, i: (b, i, h)),
        out_shape=jax.ShapeDtypeStruct((B, S, H * 64), BF16),
        scratch_shapes=_flash_scratch(hps, t, t),
        compiler_params=_params(("parallel", "parallel", "parallel")), name="moba_attention",
    )(bq, bkv, bkvt, kmean)


def _lane_group_norm(y, gain, width):
    outs = []
    for j in range(y.shape[1] // width):
        yc = y[:, j * width:(j + 1) * width]
        outs.append(_rms_rows(yc, gain))
    return jnp.concatenate(outs, axis=1)


def _mem_kv_kernel(m_ref, g_ref, w_ref, gk_ref, o_ref):
    mn = _rms_rows(m_ref[...], g_ref[...]).astype(BF16)
    y = _dot(mn, w_ref[...])
    hw = MEM_HEADS * MEM_HEAD_DIM
    k = _lane_group_norm(y[:, :hw], gk_ref[...], MEM_HEAD_DIM)
    o_ref[...] = jnp.concatenate([k, y[:, hw:]], axis=1).astype(BF16)


def _mem_kv(mem, g, w, gk):
    B, M, _ = mem.shape
    n = w.shape[1]
    return pl.pallas_call(
        _mem_kv_kernel, grid=(B,),
        in_specs=[pl.BlockSpec((None, M, D_MODEL), lambda b: (b, 0, 0)),
                  pl.BlockSpec(g.shape, lambda b: (0, 0)),
                  pl.BlockSpec(w.shape, lambda b: (0, 0)),
                  pl.BlockSpec(gk.shape, lambda b: (0, 0))],
        out_specs=pl.BlockSpec((None, M, n), lambda b: (b, 0, 0)),
        out_shape=jax.ShapeDtypeStruct((B, M, n), BF16),
        compiler_params=_params(("parallel",)), name="mem_kv",
    )(mem, g, w, gk)


def _mem_attend(x, g_ref, wq_ref, gq_ref, kv_ref, wo_ref):
    xn = _rms_rows(x, g_ref[...]).astype(BF16)
    q = _lane_group_norm(_dot(xn, wq_ref[...]), gq_ref[...], MEM_HEAD_DIM).astype(BF16)
    hw = MEM_HEADS * MEM_HEAD_DIM
    scale = MEM_HEAD_DIM ** -0.5
    outs = []
    for h in range(MEM_HEADS):
        cols = slice(h * MEM_HEAD_DIM, (h + 1) * MEM_HEAD_DIM)
        k = kv_ref[:, cols]
        v = kv_ref[:, hw + h * MEM_HEAD_DIM:hw + (h + 1) * MEM_HEAD_DIM]
        s = _dot_nt(q[:, cols], k) * scale
        p = jnp.exp(s - jnp.max(s, axis=-1, keepdims=True))
        p = p / jnp.sum(p, axis=-1, keepdims=True)
        outs.append(_dot(p.astype(BF16), v))
    o = jnp.concatenate(outs, axis=1).astype(BF16)
    return x + _dot(o, wo_ref[...])


def _post_mixer_kernel(*refs, n_in):
    a_refs = refs[:n_in]
    w_refs = refs[n_in:2 * n_in]
    x_ref, gm_ref, wq_ref, gq_ref, kv_ref, wo_ref, gf_ref, wg_ref, wu_ref, wd_ref, o_ref = refs[2 * n_in:]
    x = x_ref[...]
    for a_ref, w_ref in zip(a_refs, w_refs):
        x = x + _dot(a_ref[...], w_ref[...])
    x = _mem_attend(x, gm_ref, wq_ref, gq_ref, kv_ref, wo_ref)
    xn = _rms_rows(x, gf_ref[...]).astype(BF16)
    gate = _dot(xn, wg_ref[...])
    up = _dot(xn, wu_ref[...])
    act = (gate * jax.nn.sigmoid(gate) * up).astype(BF16)
    o_ref[...] = x + _dot(act, wd_ref[...])


def _post_mixer(parts, weights, x2, g_mem, wq, gq, kv, wo, g_ffn, wg, wu, wd, S, tm=512):
    T = x2.shape[0]
    nt = S // tm
    n_in = len(parts)
    M, n = kv.shape[1], kv.shape[2]

    def const(a):
        return pl.BlockSpec(a.shape, lambda i: (0,) * a.ndim, pipeline_mode=pl.Buffered(1))

    in_specs = ([pl.BlockSpec((tm, p.shape[1]), lambda i: (i, 0)) for p in parts]
                + [const(w) for w in weights]
                + [pl.BlockSpec((tm, D_MODEL), lambda i: (i, 0)),
                   const(g_mem), const(wq), const(gq),
                   pl.BlockSpec((None, M, n), lambda i: (i // nt, 0, 0)),
                   const(wo), const(g_ffn), const(wg), const(wu), const(wd)])
    return pl.pallas_call(
        functools.partial(_post_mixer_kernel, n_in=n_in), grid=(T // tm,), in_specs=in_specs,
        out_specs=pl.BlockSpec((tm, D_MODEL), lambda i: (i, 0)),
        out_shape=jax.ShapeDtypeStruct((T, D_MODEL), F32),
        compiler_params=_params(("parallel",)), name="post_mixer",
    )(*parts, *weights, x2, g_mem, wq, gq, kv, wo, g_ffn, wg, wu, wd)


def _nsa_prep_kernel(x_ref, trig_ref, gmix_ref, w_ref, nq_ref, nk_ref,
                     gq_ref, gks_ref, gkw_ref,
                     qc_ref, qr_ref, kvs_ref, kvst_ref, kvw_ref, kvwt_ref, kc_ref, vc_ref, gtt_ref,
                     stage_ref, *, n_tiles):
    xn = _rms_rows(x_ref[...], gmix_ref[...]).astype(BF16)
    c64, s64 = trig_ref[:, :LANES], trig_ref[:, LANES:]
    lane = _lane_iota(c64.shape)
    lo64 = (lane % 64) < 8
    first64 = lane < 64
    c64k = jnp.where(first64, c64, 1.0)
    s64k = jnp.where(first64, s64, 0.0)
    nq = nq_ref[...]
    nk = nk_ref[...]

    col = _ColumnProjector(xn, w_ref)

    for j in range(8):
        qn = _head_norm(col(j), nq, gq_ref[...])
        qr = _rope(qn, c64, s64, lo64, 8)
        qc_ref[2 * j] = qn[:, :64].astype(BF16)
        qc_ref[2 * j + 1] = qn[:, 64:].astype(BF16)
        qr_ref[2 * j] = qr[:, :64].astype(BF16)
        qr_ref[2 * j + 1] = qr[:, 64:].astype(BF16)
    tile = pl.program_id(0) % n_tiles
    sel_blk = tile * (ATT_T // NSA_SEL_LEN) + lax.shift_right_logical(
        _row_iota(c64.shape), NSA_SEL_LEN.bit_length() - 1)
    blk_onehot = jnp.where(lane == HEAD_DIM + sel_blk, 1.0, 0.0)
    for g in range(NSA_GROUPS):
        kv = _kv_column(col(8 + g), nk, gks_ref[...], c64k, s64k, lo64, first64)
        kvs_ref[g] = jnp.where(first64, kv, blk_onehot).astype(BF16)
        kvst_ref[g] = _pv_operand(kv)
        kv = _kv_column(col(12 + g), nk, gkw_ref[...], c64k, s64k, lo64, first64)
        kvw_ref[g] = kv.astype(BF16)
        kvwt_ref[g] = _pv_operand(kv)
    stride = NSA_CMP_STRIDE
    rows = stage_ref.shape[0] // stride
    for out_ref, first in ((kc_ref, 16), (vc_ref, 18)):
        for c in range(2):
            stage_ref[...] = col(first + c)
            for u in range(0, stride, 2):
                pair = [stage_ref[pl.ds(u + v, rows, stride=stride), :] for v in range(2)]
                for h in range(2):
                    halves = [p[:, h * HEAD_DIM:(h + 1) * HEAD_DIM] for p in pair]
                    out_ref[2 * c + h, :, u * HEAD_DIM:(u + 2) * HEAD_DIM] = jnp.concatenate(halves, axis=1)
    gates_t = jax.nn.sigmoid(col(20)).T
    for g in range(NSA_GROUPS):
        gtt_ref[g] = gates_t[NSA_GATE_ROWS * g:NSA_GATE_ROWS * (g + 1), :]


def _nsa_prep(x2, trig, gmix, w, tabs, B, S):
    T = x2.shape[0]
    tm = ATT_T
    nt = S // tm
    nq, nk, gq, gks, gkw = tabs

    def full(a):
        return pl.BlockSpec(a.shape, lambda i: (0,) * a.ndim)

    def hm(width, heads):
        return pl.BlockSpec((None, heads, tm, width), lambda i: (i // nt, 0, i % nt, 0))

    def hmt(heads):
        return pl.BlockSpec((None, heads, None, PV_ROWS, tm), lambda i: (i // nt, 0, i % nt, 0, 0))

    def tokm(width):
        return pl.BlockSpec((None, tm, width), lambda i: (i // nt, i % nt, 0))

    out_shape = (
        jax.ShapeDtypeStruct((B, 16, S, 64), BF16),
        jax.ShapeDtypeStruct((B, 16, S, 64), BF16),
        jax.ShapeDtypeStruct((B, 4, S, 128), BF16),
        jax.ShapeDtypeStruct((B, 4, nt, PV_ROWS, tm), BF16),
        jax.ShapeDtypeStruct((B, 4, S, 128), BF16),
        jax.ShapeDtypeStruct((B, 4, nt, PV_ROWS, tm), BF16),
        jax.ShapeDtypeStruct((B, 4, S // 16, 1024), F32),
        jax.ShapeDtypeStruct((B, 4, S // 16, 1024), F32),
        jax.ShapeDtypeStruct((B, 4, NSA_GATE_ROWS, S), F32),
    )
    rows16 = pl.BlockSpec((None, 4, tm // 16, 1024), lambda i: (i // nt, 0, i % nt, 0))
    out_specs = (hm(64, 16), hm(64, 16), hm(128, 4), hmt(4), hm(128, 4), hmt(4), rows16, rows16,
                 pl.BlockSpec((None, 4, NSA_GATE_ROWS, tm), lambda i: (i // nt, 0, 0, i % nt)))
    in_specs = [pl.BlockSpec((tm, D_MODEL), lambda i: (i, 0)),
                pl.BlockSpec((tm, 2 * LANES), lambda i: (i, 0)),
                full(gmix), full(w), full(nq), full(nk), full(gq), full(gks), full(gkw)]
    return pl.pallas_call(
        functools.partial(_nsa_prep_kernel, n_tiles=nt), grid=(T // tm,), in_specs=in_specs, out_specs=out_specs,
        out_shape=out_shape, scratch_shapes=[pltpu.VMEM((tm, LANES), F32)],
        compiler_params=_params(("parallel",)), name="nsa_prep",
    )(x2, trig, gmix, w, nq, nk, gq, gks, gkw)


def _compress_one(x16, pa, pb, w1a, w1b, w2):
    n16 = x16.shape[0]
    h_a = _dot((x16 + pa).astype(BF16), w1a)
    h_b = _dot((x16 + pb).astype(BF16), w1b)
    pre = h_a + pltpu.roll(h_b, n16 - 1, 0)
    act = pre * jax.nn.sigmoid(pre)
    return _dot(act.astype(BF16), w2)


def _compress_kernel(xk_ref, xv_ref, pk_ref, pv_ref, w1k_ref, w1v_ref, w2k_ref, w2v_ref, gk_ref,
                     o_ref, ot_ref):
    half = w1k_ref.shape[0] // 2
    k = _compress_one(xk_ref[...], pk_ref[0:1, :], pk_ref[1:2, :],
                      w1k_ref[:half, :], w1k_ref[half:, :], w2k_ref[...])
    k = _rms_rows(k, gk_ref[...])
    v = _compress_one(xv_ref[...], pv_ref[0:1, :], pv_ref[1:2, :],
                      w1v_ref[:half, :], w1v_ref[half:, :], w2v_ref[...])
    kv = jnp.concatenate([k, v], axis=1)
    o_ref[...] = kv.astype(BF16)
    ot_ref[...] = _pv_operand(kv)


def _compress(xk16, xv16, pk, pv, w1k, w1v, w2k, w2v, gk):
    B, G, n16, width = xk16.shape

    def full(a):
        return pl.BlockSpec(a.shape, lambda b, g: (0,) * a.ndim)

    xspec = pl.BlockSpec((None, None, n16, width), lambda b, g: (b, g, 0, 0))
    return pl.pallas_call(
        _compress_kernel, grid=(B, G),
        in_specs=[xspec, xspec, full(pk), full(pv), full(w1k), full(w1v), full(w2k), full(w2v), full(gk)],
        out_specs=(pl.BlockSpec((None, None, n16, 128), lambda b, g: (b, g, 0, 0)),
                   pl.BlockSpec((None, None, PV_ROWS, n16), lambda b, g: (b, g, 0, 0))),
        out_shape=(jax.ShapeDtypeStruct((B, G, n16, 128), BF16),
                   jax.ShapeDtypeStruct((B, G, PV_ROWS, n16), BF16)),
        compiler_params=_params(("parallel", "parallel")), name="nsa_compress",
    )(xk16, xv16, pk, pv, w1k, w1v, w2k, w2v, gk)


NSA_GPS = 4


def _nsa_tile_masks(i, n16, n_sel, n_cmp, n_top):
    t = ATT_T
    t0 = i * t
    n_id = _row_iota((n16, t))
    q_id = t0 + _lane_iota((n16, t))
    cmp_visible = (n_id < n_cmp) & (n_id * NSA_CMP_STRIDE + (NSA_CMP_LEN - 1) <= q_id)
    b_id = _row_iota((n_sel, n16)) * NSA_SEL_LEN
    r_id = _lane_iota((n_sel, n16)) * NSA_CMP_STRIDE
    cover_t = ((r_id < b_id + NSA_SEL_LEN) & (r_id + NSA_CMP_LEN > b_id)
               & (_lane_iota((n_sel, n16)) < n_cmp))
    n_wc = NSA_WINDOW // t + 1
    cw = jnp.maximum(i - (n_wc - 1), 0)
    dist = (i - cw) * t + _lane_iota((n_wc * t, t)) - _row_iota((n_wc * t, t))
    blk = _row_iota((n_sel, t))
    cur = lax.shift_right_logical(t0 + _lane_iota((n_sel, t)), NSA_SEL_LEN.bit_length() - 1)
    return dict(
        bias_c=jnp.where(cmp_visible, 0.0, MASK_BIAS),
        cover_t=jnp.where(cover_t, 1.0, 0.0).astype(BF16),
        n_wc=n_wc, cw=cw, bias_w=jnp.where((dist >= 0) & (dist < NSA_WINDOW), 0.0, MASK_BIAS),
        forced=(blk == 0) | (blk == cur) | (blk == cur - 1), visible_blk=blk <= cur,
        n_wanted=jnp.minimum(cur[0:1, :] + 1, n_top).astype(F32))


def _nsa_front(qc_ref, qr_ref, kvc_ref, kvct_ref, kvw_ref, kvwt_ref, gtt_ref, part_ref, masks):
    t = ATT_T
    HG = NSA_HEADS // NSA_GROUPS
    n16 = kvc_ref.shape[0]

    p_sum = jnp.zeros((n16, t), F32)
    o_c = []
    probs, inv_ls, outs = _softmax_direct([_pad_q(qc_ref[j]) for j in range(HG)], kvc_ref[...],
                                          [kvct_ref[...]], masks["bias_c"])
    for j in range(HG):
        p_sum = p_sum + probs[j] * inv_ls[j]
        o_c.append(outs[j] * inv_ls[j])

    p_hi, p_lo = _split_bf16(p_sum)
    imp = _dot(masks["cover_t"], p_hi) + _dot(masks["cover_t"], p_lo)

    qs = [_pad_q(qr_ref[j]) for j in range(HG)]
    n_wc, cw = masks["n_wc"], masks["cw"]
    kw0 = pl.multiple_of(cw * t, t)
    _, inv_lw, out_w = _softmax_direct(qs, kvw_ref[pl.ds(kw0, n_wc * t), :],
                                       [kvwt_ref[cw + u] for u in range(n_wc)], masks["bias_w"])
    for j in range(HG):
        part_ref[:, j * t:(j + 1) * t] = (gtt_ref[3 * j:3 * j + 1, :] * o_c[j]
                                          + gtt_ref[3 * j + 2:3 * j + 3, :] * (out_w[j] * inv_lw[j]))

    imp = jnp.where(masks["forced"], NSA_FORCE, imp)
    return qs, jnp.where(masks["visible_blk"], imp, NEG_INF)


RANK_SEGMENT = 16


def _count_larger(imp, acc, rows):
    for m in rows:
        acc = acc + jnp.where(imp[m:m + 1, :] > imp, 1.0, 0.0)
    return acc


def _nsa_kernel(qc_ref, qr_ref, kvc_ref, kvct_ref, kvs_ref, kvst_ref, kvw_ref, kvwt_ref, gtt_ref,
                o_ref, sel_ref, part_ref, *flash_refs, n_cmp, n_top):
    t = ATT_T
    HG = NSA_HEADS // NSA_GROUPS
    n_slots = NSA_GPS * HG
    i = pl.program_id(2)

    masks = _nsa_tile_masks(i, kvc_ref.shape[1], sel_ref.shape[1], n_cmp, n_top)
    n_sel = sel_ref.shape[1]
    qs, imps = [], []
    for g in range(NSA_GPS):
        heads_g = pl.ds(g * HG, HG)
        q_g, imp_g = _nsa_front(
            qc_ref.at[heads_g], qr_ref.at[heads_g], kvc_ref.at[g], kvct_ref.at[g], kvw_ref.at[g],
            kvwt_ref.at[g], gtt_ref.at[g], part_ref.at[:, pl.ds(g * HG * t, HG * t)], masks)
        qs += q_g
        imps.append(imp_g)
        sel_ref[g] = _count_larger(imp_g, jnp.zeros((n_sel, t), F32), range(RANK_SEGMENT))

    for k in range(1, n_sel // RANK_SEGMENT):
        @pl.when((i + 1) * (t // NSA_SEL_LEN) > k * RANK_SEGMENT)
        def _():
            for g in range(NSA_GPS):
                sel_ref[g] = _count_larger(imps[g], sel_ref[g],
                                           range(k * RANK_SEGMENT, (k + 1) * RANK_SEGMENT))

    miss = None
    for g in range(NSA_GPS):
        sel_fast = sel_ref[g] < n_top
        n_picked = jnp.sum(jnp.where(sel_fast & masks["visible_blk"], 1.0, 0.0), axis=0, keepdims=True)
        sel_ref[g] = jnp.where(sel_fast, 0.0, MASK_BIAS)
        miss_g = jnp.abs(n_picked - masks["n_wanted"])
        miss = miss_g if miss is None else jnp.maximum(miss, miss_g)

    @pl.when(jnp.max(miss) > 0.0)
    def _():
        for g in range(NSA_GPS):
            sel_ref[g] = (_rank_select_t(imps[g], None, n_top) - 1.0) * (-MASK_BIAS)

    qs_sel = [_bias_lanes(qs[s], sel_ref[s // HG]) for s in range(n_slots)]
    flash = _Flash(*flash_refs, t)
    flash.reset()

    def sel_operands(c):
        k0 = pl.multiple_of(c * t, t)
        kvs = [kvs_ref[s // HG, pl.ds(k0, t), :] for s in range(n_slots)]
        kvts = [[kvst_ref[s // HG, c]] for s in range(n_slots)]
        return kvs, kvts

    def past_chunk(c):
        kvs, kvts = sel_operands(c)
        return kvs, kvts, [None] * n_slots

    flash.run(qs_sel, i, past_chunk)
    kvs, kvts = sel_operands(i)
    flash.update(qs_sel, kvs, kvts, [_causal_bias(t)] * n_slots)

    heads = []
    for s in range(n_slots):
        g, j = divmod(s, HG)
        heads.append(part_ref[:, s * t:(s + 1) * t] + gtt_ref[g, 3 * j + 1:3 * j + 2, :] * flash.result(s))
    _store_heads(o_ref, heads)


def _nsa_attention(qc, qr, kvc, kvct, kvs, kvst, kvw, kvwt, gates_t):
    B, H, S, _ = qc.shape
    G = NSA_GROUPS
    HG = H // G
    t = ATT_T
    nt = S // t
    n16 = kvc.shape[2]
    n_cmp = (S - NSA_CMP_LEN) // NSA_CMP_STRIDE + 1
    n_sel = S // NSA_SEL_LEN
    n_top = min(NSA_SEL_TOPK, n_sel)
    gps = NSA_GPS
    assert G % gps == 0 and nt % 2 == 0 and n_sel <= HEAD_DIM and S >= (NSA_WINDOW // t + 1) * t
    qspec = pl.BlockSpec((None, gps * HG, t, 64), lambda b, g, i: (b, g, i, 0))
    once = pl.Buffered(1)
    kvspec = pl.BlockSpec((None, gps, S, 128), lambda b, g, i: (b, g, 0, 0), pipeline_mode=once)
    kvtspec = pl.BlockSpec((None, gps, nt, PV_ROWS, t), lambda b, g, i: (b, g, 0, 0, 0), pipeline_mode=once)
    in_specs = [qspec, qspec,
                pl.BlockSpec((None, gps, n16, 128), lambda b, g, i: (b, g, 0, 0)),
                pl.BlockSpec((None, gps, PV_ROWS, n16), lambda b, g, i: (b, g, 0, 0)),
                kvspec, kvtspec, kvspec, kvtspec,
                pl.BlockSpec((None, gps, NSA_GATE_ROWS, t), lambda b, g, i: (b, g, 0, i))]
    return pl.pallas_call(
        functools.partial(_nsa_kernel, n_cmp=n_cmp, n_top=n_top), grid=(B, G // gps, nt), in_specs=in_specs,
        out_specs=pl.BlockSpec((None, t, gps * HG * 64), lambda b, g, i: (b, i, g)),
        out_shape=jax.ShapeDtypeStruct((B, S, H * 64), BF16),
        scratch_shapes=[pltpu.VMEM((gps, n_sel, t), F32), pltpu.VMEM((PV_ROWS, gps * HG * t), F32)]
                       + _flash_scratch(gps * HG, t, t),
        compiler_params=_params(("parallel", "parallel", "parallel")), name="nsa_attention",
    )(qc, qr, kvc, kvct, kvs, kvst, kvw, kvwt, gates_t)


def _rope_freq_row(period, rot):
    half = rot // 2
    inv_freq = ROPE_THETA ** (-(jnp.arange(half, dtype=F32) * 2.0 / rot))
    lane = jnp.arange(LANES) % period
    f = jnp.where(lane < rot, inv_freq[lane % half], 0.0)
    return f.reshape(1, LANES).astype(F32)


def _norm_matrices():
    r = jnp.arange(LANES)
    same = (r[:, None] // 64) == (r[None, :] // 64)
    nq = jnp.where(same, 1.0 / 64, 0.0).astype(BF16)
    nk = jnp.where(same & (r[:, None] < 64), 1.0 / 64, 0.0).astype(BF16)
    return nq, nk


def _q_gain(g):
    return (jnp.tile(g.astype(F32), 2) * Q_SCALE).reshape(1, LANES)


def _k_gain(g):
    return jnp.concatenate([g.astype(F32), jnp.ones((64,), F32)]).reshape(1, LANES)


def _interleave_kv(wk, wv, n_heads):
    d = wk.shape[0]
    wk = wk.reshape(d, n_heads, 64)
    wv = wv.reshape(d, n_heads, 64)
    return jnp.concatenate([wk, wv], axis=2).reshape(d, n_heads * 128)


def _split_cols(w, sizes):
    out, start = [], 0
    for n in sizes:
        out.append(w[:, start:start + n])
        start += n
    return out


def _mixer_layer0(x2, trig, B, S, gmix, w_in, w_out, a_q_norm, a_k_norm, b_q_norm, b_k_norm):
    sizes = (512, 64, 64, 256, 32, 8, 512, 512, 512)
    waq, wak, wav, wiq, wik, wiw, wbq, wbk, wbv = _split_cols(w_in, sizes)
    pad = jnp.zeros((D_MODEL, LANES - 40), w_in.dtype)
    w = jnp.concatenate([waq, wbq, _interleave_kv(wbk, wbv, 8), wak, wav, wiq, wik, wiw, pad],
                        axis=1).astype(BF16)
    nq, nk = _norm_matrices()
    tabs = (nq, nk, _q_gain(a_q_norm), _q_gain(b_q_norm), _k_gain(a_k_norm), _k_gain(b_k_norm))
    aq, bq, bkv, bkvt, akv, akvt, iq, ik, iwt, km = _ab_prep(x2, trig, gmix, w, tabs, B, S)
    n_blk = S // MOBA_BLOCK
    kmean = km.reshape(B, n_blk, 8, 128).transpose(0, 2, 1, 3)
    o_a = _dsa_attention(iq, iwt, ik, aq, akv, akvt).reshape(B * S, 512)
    o_b = _moba_attention(bq, bkv, bkvt, kmean).reshape(B * S, 512)
    w_out = w_out.astype(BF16)
    return [o_a, o_b], [w_out[:512], w_out[512:]]


def _mixer_layer1(x2, trig, B, S, gmix, w_in, w_out, q_norm, kcmp_norm, ksel_norm, kwin_norm,
                  pos_k, pos_v, w1_k, w2_k, w1_v, w2_v):
    G = NSA_GROUPS
    sizes = (1024,) + (256,) * 6 + (48,)
    wq, wkc, wvc, wks, wvs, wkw, wvw, wgt = _split_cols(w_in, sizes)
    pad = jnp.zeros((D_MODEL, LANES - 48), w_in.dtype)
    w = jnp.concatenate([wq, _interleave_kv(wks, wvs, G), _interleave_kv(wkw, wvw, G),
                         wkc, wvc, wgt, pad], axis=1).astype(BF16)
    nq, nk = _norm_matrices()
    tabs = (nq, nk, _q_gain(q_norm), _k_gain(ksel_norm), _k_gain(kwin_norm))
    qc, qr, kvs, kvst, kvw, kvwt, kc16, vc16, gates_t = _nsa_prep(x2, trig, gmix, w, tabs, B, S)

    def pos_rows(p):
        return p.astype(F32).reshape(2, NSA_CMP_STRIDE * HEAD_DIM)

    kvc, kvct = _compress(kc16, vc16, pos_rows(pos_k), pos_rows(pos_v),
                          w1_k.astype(BF16), w1_v.astype(BF16), w2_k.astype(BF16), w2_v.astype(BF16),
                          kcmp_norm.astype(F32).reshape(1, HEAD_DIM))
    o = _nsa_attention(qc, qr, kvc, kvct, kvs, kvst, kvw, kvwt, gates_t)
    return [o.reshape(B * S, NSA_HEADS * HEAD_DIM)], [w_out.astype(BF16)]


def _finish_layer(parts, weights, x2, mem, S, g_mem, g_src, w_q, w_kv, w_o, q_norm, k_norm,
                  g_ffn, ffn_w_in, ffn_w_out):
    row = lambda v: v.astype(F32).reshape(1, -1)
    kv = _mem_kv(mem, row(g_src), w_kv.astype(BF16), row(k_norm))
    wg = ffn_w_in[:, :D_FF].astype(BF16)
    wu = ffn_w_in[:, D_FF:].astype(BF16)
    return _post_mixer(parts, weights, x2, row(g_mem), w_q.astype(BF16), row(q_norm), kv, w_o.astype(BF16),
                       row(g_ffn), wg, wu, ffn_w_out.astype(BF16), S)


def kernel(x, mem, positions, norm_mix, norm_mem, norm_mem_src, norm_ffn, ab_w_in, ab_w_out, dsa_q_norm, dsa_k_norm, moba_q_norm, moba_k_norm, nsa_w_in, nsa_w_out, nsa_q_norm, nsa_kcmp_norm, nsa_ksel_norm, nsa_kwin_norm, nsa_cmp_pos_k, nsa_cmp_pos_v, nsa_cmp_w1_k, nsa_cmp_w2_k, nsa_cmp_w1_v, nsa_cmp_w2_v, mem_w_q, mem_w_kv, mem_w_o, mem_q_norm, mem_k_norm, ffn_w_in, ffn_w_out):
    B, S, D = x.shape
    depth = norm_mix.shape[0]
    x2 = x.reshape(B * S, D)
    trig = _rope_trig(positions.astype(F32).reshape(B * S, 1), _rope_freq_row(64, 16), _rope_freq_row(32, 8))
    row = lambda v: v.astype(F32).reshape(1, -1)
    for i in range(depth):
        j = i // 2
        if i % 2 == 0:
            parts, weights = _mixer_layer0(x2, trig, B, S, row(norm_mix[i]), ab_w_in[j], ab_w_out[j],
                               dsa_q_norm[j], dsa_k_norm[j], moba_q_norm[j], moba_k_norm[j])
        else:
            parts, weights = _mixer_layer1(x2, trig, B, S, row(norm_mix[i]), nsa_w_in[j], nsa_w_out[j],
                               nsa_q_norm[j], nsa_kcmp_norm[j], nsa_ksel_norm[j], nsa_kwin_norm[j],
                               nsa_cmp_pos_k[j], nsa_cmp_pos_v[j], nsa_cmp_w1_k[j], nsa_cmp_w2_k[j],
                               nsa_cmp_w1_v[j], nsa_cmp_w2_v[j])
        x2 = _finish_layer(parts, weights, x2, mem, S, norm_mem[i], norm_mem_src[i], mem_w_q[i], mem_w_kv[i],
                           mem_w_o[i], mem_q_norm[i], mem_k_norm[i], norm_ffn[i], ffn_w_in[i], ffn_w_out[i])
    return x2.reshape(B, S, D)
```
